```python
import jax, jax.numpy as jnp
from jax import lax
import numpy as np

D_MODEL = 2048
BATCH = 8
SEQ = 2048
DEPTH = 2

N_MIXERS = 2
EXPAND = 2
BRANCH_WIDTH = EXPAND * D_MODEL
SC_WIDTH = 3
LRU_CONV_WIDTH = 4
LRU_HEAD_DIM = 256
LRU_HEADS = BRANCH_WIDTH // LRU_HEAD_DIM
RGLRU_C = 8.0
N_CONV_LAYERS = (DEPTH + 1) // N_MIXERS
N_LRU_LAYERS = DEPTH // N_MIXERS
EPS = 1e-6

kernel_name = "hybrid_shortconv_rglru_adaln"


def rmsnorm(x, g):
    xf = x.astype(jnp.float32)
    y = xf * lax.rsqrt(jnp.mean(xf * xf, axis=-1, keepdims=True) + EPS)
    return (y * g.astype(jnp.float32)).astype(x.dtype)


def adaln(h, c, w, b):
    mod = jnp.einsum('bd,df->bf', jax.nn.silu(c), w) + b
    shift, scale, gate = jnp.split(mod, 3, axis=-1)
    h = h * (1.0 + scale[:, None, :]) + shift[:, None, :]
    return h, gate


def causal_depthwise_conv(u, w):
    width, e = w.shape
    rhs = w[:, None, :].astype(u.dtype)
    return lax.conv_general_dilated(
        u, rhs, window_strides=(1,), padding=[(width - 1, 0)],
        dimension_numbers=('NWC', 'WIO', 'NWC'), feature_group_count=e)


def short_conv_mixer(h, w_in, conv_w, w_out):
    proj = jnp.einsum('bsd,de->bse', h, w_in)
    b_gate, c_gate, v, g = jnp.split(proj, 4, axis=-1)
    u = causal_depthwise_conv(c_gate * v, conv_w)
    y = b_gate * u * jax.nn.silu(g)
    return jnp.einsum('bse,ed->bsd', y, w_out)


def _linear_recurrence_combine(left, right):
    a_l, b_l = left
    a_r, b_r = right
    return a_l * a_r, a_r * b_l + b_r


def rglru_mixer(h, w_in, conv_w, conv_b, w_a, b_a, w_x, b_x, lam, w_out):
    proj = jnp.einsum('bsd,de->bse', h, w_in)
    v, g = jnp.split(proj, 2, axis=-1)
    v = causal_depthwise_conv(v, conv_w) + conv_b
    bsz, seq, e = v.shape
    n_heads, head_dim = w_a.shape[0], w_a.shape[1]
    vh = v.reshape(bsz, seq, n_heads, head_dim)
    r = jax.nn.sigmoid(jnp.einsum('bshi,hij->bshj', vh, w_a) + b_a).reshape(bsz, seq, e)
    i = jax.nn.sigmoid(jnp.einsum('bshi,hij->bshj', vh, w_x) + b_x).reshape(bsz, seq, e)
    log_a = -RGLRU_C * r.astype(jnp.float32) * jax.nn.softplus(-lam.astype(jnp.float32))
    a = jnp.exp(log_a)
    norm_mult = jnp.sqrt(-jnp.expm1(2.0 * log_a))
    b = norm_mult * (i * v).astype(jnp.float32)
    _, hs = lax.associative_scan(_linear_recurrence_combine, (a, b), axis=1)
    y = hs.astype(h.dtype) * jax.nn.silu(g)
    return jnp.einsum('bse,ed->bsd', y, w_out)


def _fwd_setup_inputs(seed: int = 0) -> dict:
    key = jax.random.key(seed)
    ks = jax.random.split(key, 24)
    D, E, H, Dh = D_MODEL, BRANCH_WIDTH, LRU_HEADS, LRU_HEAD_DIM
    nA, nB = N_CONV_LAYERS, N_LRU_LAYERS
    f32 = jnp.float32
    n = lambda k, shape, s: jax.random.normal(k, shape, f32) * s
    x = jax.random.normal(ks[0], (BATCH, SEQ, D), f32)
    c = jax.random.normal(ks[1], (BATCH, D), f32)
    norm_g = 1.0 + n(ks[2], (DEPTH, D), 0.02)
    ada_w = n(ks[3], (DEPTH, D, 3 * D), 0.5 * D ** -0.5)
    ada_b = n(ks[4], (DEPTH, 3 * D), 0.02)
    sc_w_in = n(ks[5], (nA, D, 4 * E), D ** -0.5)
    sc_conv_w = n(ks[6], (nA, SC_WIDTH, E), SC_WIDTH ** -0.5)
    sc_w_out = n(ks[7], (nA, E, D), E ** -0.5)
    lru_w_in = n(ks[8], (nB, D, 2 * E), D ** -0.5)
    lru_conv_w = n(ks[9], (nB, LRU_CONV_WIDTH, E), LRU_CONV_WIDTH ** -0.5)
    lru_conv_b = n(ks[10], (nB, E), 0.02)
    lru_w_a = n(ks[11], (nB, H, Dh, Dh), Dh ** -0.5)
    lru_b_a = n(ks[12], (nB, H, Dh), 0.02)
    lru_w_x = n(ks[13], (nB, H, Dh, Dh), Dh ** -0.5)
    lru_b_x = n(ks[14], (nB, H, Dh), 0.02)
    a_pow_c = jax.random.uniform(ks[15], (nB, E), f32, minval=0.9, maxval=0.999)
    a0 = a_pow_c ** (1.0 / RGLRU_C)
    lru_lambda = jnp.log(a0) - jnp.log1p(-a0)
    lru_w_out = n(ks[16], (nB, E, D), E ** -0.5)
    final_g = 1.0 + n(ks[17], (D,), 0.02)
    return {
        "x": x, "c": c, "norm_g": norm_g, "ada_w": ada_w, "ada_b": ada_b,
        "sc_w_in": sc_w_in, "sc_conv_w": sc_conv_w, "sc_w_out": sc_w_out,
        "lru_w_in": lru_w_in, "lru_conv_w": lru_conv_w, "lru_conv_b": lru_conv_b,
        "lru_w_a": lru_w_a, "lru_b_a": lru_b_a, "lru_w_x": lru_w_x, "lru_b_x": lru_b_x,
        "lru_lambda": lru_lambda, "lru_w_out": lru_w_out, "final_g": final_g,
    }


def _fwd_reference(x, c, norm_g, ada_w, ada_b, sc_w_in, sc_conv_w, sc_w_out,
              lru_w_in, lru_conv_w, lru_conv_b, lru_w_a, lru_b_a, lru_w_x, lru_b_x,
              lru_lambda, lru_w_out, final_g):
    for layer in range(DEPTH):
        j = layer // N_MIXERS
        h = rmsnorm(x, norm_g[layer])
        h, gate = adaln(h, c, ada_w[layer], ada_b[layer])
        if layer % N_MIXERS == 0:
            y = short_conv_mixer(h, sc_w_in[j], sc_conv_w[j], sc_w_out[j])
        else:
            y = rglru_mixer(h, lru_w_in[j], lru_conv_w[j], lru_conv_b[j],
                            lru_w_a[j], lru_b_a[j], lru_w_x[j], lru_b_x[j],
                            lru_lambda[j], lru_w_out[j])
        x = x + gate[:, None, :] * y
    return rmsnorm(x, final_g)


import jax as _jax
import jax.numpy as _jnp

TWIN_FORMAT = 'train_step'
FWD_PARAMS = ['x', 'c', 'norm_g', 'ada_w', 'ada_b', 'sc_w_in', 'sc_conv_w', 'sc_w_out', 'lru_w_in', 'lru_conv_w', 'lru_conv_b', 'lru_w_a', 'lru_b_a', 'lru_w_x', 'lru_b_x', 'lru_lambda', 'lru_w_out', 'final_g']
TWIN_WEIGHTS = ['norm_g', 'ada_w', 'ada_b', 'sc_w_in', 'sc_conv_w', 'sc_w_out', 'lru_w_in', 'lru_conv_w', 'lru_conv_b', 'lru_w_a', 'lru_b_a', 'lru_w_x', 'lru_b_x', 'lru_lambda', 'lru_w_out', 'final_g']
TWIN_DIFF_INPUT = 'x'
TWIN_INPUTS = ['x', 'c', 'norm_g', 'ada_w', 'ada_b', 'sc_w_in', 'sc_conv_w', 'sc_w_out', 'lru_w_in', 'lru_conv_w', 'lru_conv_b', 'lru_w_a', 'lru_b_a', 'lru_w_x', 'lru_b_x', 'lru_lambda', 'lru_w_out', 'final_g', 'loss_target', 'm_norm_g', 'm_ada_w', 'm_ada_b', 'm_sc_w_in', 'm_sc_conv_w', 'm_sc_w_out', 'm_lru_w_in', 'm_lru_conv_w', 'm_lru_conv_b', 'm_lru_w_a', 'm_lru_b_a', 'm_lru_w_x', 'm_lru_b_x', 'm_lru_lambda', 'm_lru_w_out', 'm_final_g', 'v_norm_g', 'v_ada_w', 'v_ada_b', 'v_sc_w_in', 'v_sc_conv_w', 'v_sc_w_out', 'v_lru_w_in', 'v_lru_conv_w', 'v_lru_conv_b', 'v_lru_w_a', 'v_lru_b_a', 'v_lru_w_x', 'v_lru_b_x', 'v_lru_lambda', 'v_lru_w_out', 'v_final_g']
TWIN_OUTPUTS = ['loss', 'grad_x', 'grad_norm_g', 'grad_ada_w', 'grad_ada_b', 'grad_sc_w_in', 'grad_sc_conv_w', 'grad_sc_w_out', 'grad_lru_w_in', 'grad_lru_conv_w', 'grad_lru_conv_b', 'grad_lru_w_a', 'grad_lru_b_a', 'grad_lru_w_x', 'grad_lru_b_x', 'grad_lru_lambda', 'grad_lru_w_out', 'grad_final_g', 'delta_norm_g', 'delta_ada_w', 'delta_ada_b', 'delta_sc_w_in', 'delta_sc_conv_w', 'delta_sc_w_out', 'delta_lru_w_in', 'delta_lru_conv_w', 'delta_lru_conv_b', 'delta_lru_w_a', 'delta_lru_b_a', 'delta_lru_w_x', 'delta_lru_b_x', 'delta_lru_lambda', 'delta_lru_w_out', 'delta_final_g', 'new_m_norm_g', 'new_m_ada_w', 'new_m_ada_b', 'new_m_sc_w_in', 'new_m_sc_conv_w', 'new_m_sc_w_out', 'new_m_lru_w_in', 'new_m_lru_conv_w', 'new_m_lru_conv_b', 'new_m_lru_w_a', 'new_m_lru_b_a', 'new_m_lru_w_x', 'new_m_lru_b_x', 'new_m_lru_lambda', 'new_m_lru_w_out', 'new_m_final_g', 'new_v_norm_g', 'new_v_ada_w', 'new_v_ada_b', 'new_v_sc_w_in', 'new_v_sc_conv_w', 'new_v_sc_w_out', 'new_v_lru_w_in', 'new_v_lru_conv_w', 'new_v_lru_conv_b', 'new_v_lru_w_a', 'new_v_lru_b_a', 'new_v_lru_w_x', 'new_v_lru_b_x', 'new_v_lru_lambda', 'new_v_lru_w_out', 'new_v_final_g']
TWIN_LEAF_KINDS = {'loss': 'loss', 'grad_x': 'grad_x', 'grad_norm_g': 'grad_w', 'grad_ada_w': 'grad_w', 'grad_ada_b': 'grad_w', 'grad_sc_w_in': 'grad_w', 'grad_sc_conv_w': 'grad_w', 'grad_sc_w_out': 'grad_w', 'grad_lru_w_in': 'grad_w', 'grad_lru_conv_w': 'grad_w', 'grad_lru_conv_b': 'grad_w', 'grad_lru_w_a': 'grad_w', 'grad_lru_b_a': 'grad_w', 'grad_lru_w_x': 'grad_w', 'grad_lru_b_x': 'grad_w', 'grad_lru_lambda': 'grad_w', 'grad_lru_w_out': 'grad_w', 'grad_final_g': 'grad_w', 'delta_norm_g': 'delta_w', 'delta_ada_w': 'delta_w', 'delta_ada_b': 'delta_w', 'delta_sc_w_in': 'delta_w', 'delta_sc_conv_w': 'delta_w', 'delta_sc_w_out': 'delta_w', 'delta_lru_w_in': 'delta_w', 'delta_lru_conv_w': 'delta_w', 'delta_lru_conv_b': 'delta_w', 'delta_lru_w_a': 'delta_w', 'delta_lru_b_a': 'delta_w', 'delta_lru_w_x': 'delta_w', 'delta_lru_b_x': 'delta_w', 'delta_lru_lambda': 'delta_w', 'delta_lru_w_out': 'delta_w', 'delta_final_g': 'delta_w', 'new_m_norm_g': 'new_m', 'new_m_ada_w': 'new_m', 'new_m_ada_b': 'new_m', 'new_m_sc_w_in': 'new_m', 'new_m_sc_conv_w': 'new_m', 'new_m_sc_w_out': 'new_m', 'new_m_lru_w_in': 'new_m', 'new_m_lru_conv_w': 'new_m', 'new_m_lru_conv_b': 'new_m', 'new_m_lru_w_a': 'new_m', 'new_m_lru_b_a': 'new_m', 'new_m_lru_w_x': 'new_m', 'new_m_lru_b_x': 'new_m', 'new_m_lru_lambda': 'new_m', 'new_m_lru_w_out': 'new_m', 'new_m_final_g': 'new_m', 'new_v_norm_g': 'new_v', 'new_v_ada_w': 'new_v', 'new_v_ada_b': 'new_v', 'new_v_sc_w_in': 'new_v', 'new_v_sc_conv_w': 'new_v', 'new_v_sc_w_out': 'new_v', 'new_v_lru_w_in': 'new_v', 'new_v_lru_conv_w': 'new_v', 'new_v_lru_conv_b': 'new_v', 'new_v_lru_w_a': 'new_v', 'new_v_lru_b_a': 'new_v', 'new_v_lru_w_x': 'new_v', 'new_v_lru_b_x': 'new_v', 'new_v_lru_lambda': 'new_v', 'new_v_lru_w_out': 'new_v', 'new_v_final_g': 'new_v'}


def _forward(args):
    return _fwd_reference(*[args[k] for k in FWD_PARAMS])


def _output_shape():
    out = _jax.eval_shape(lambda: _forward(_fwd_setup_inputs(0)))
    return out.shape, out.dtype

N_MICROBATCH = 1
ADAM_LR = 0.001
ADAM_B1 = 0.9
ADAM_B2 = 0.999
ADAM_EPS = 1e-08
ADAM_WD = 0.01
ADAM_STEP = 10
PER_EXAMPLE_BATCH_AXIS = {'x': 0, 'c': 0, 'loss_target': 0}
SHARED_INPUTS = []
_WEIGHT_DTYPES = {'norm_g': _jnp.float32, 'ada_w': _jnp.float32, 'ada_b': _jnp.float32, 'sc_w_in': _jnp.float32, 'sc_conv_w': _jnp.float32, 'sc_w_out': _jnp.float32, 'lru_w_in': _jnp.float32, 'lru_conv_w': _jnp.float32, 'lru_conv_b': _jnp.float32, 'lru_w_a': _jnp.float32, 'lru_b_a': _jnp.float32, 'lru_w_x': _jnp.float32, 'lru_b_x': _jnp.float32, 'lru_lambda': _jnp.float32, 'lru_w_out': _jnp.float32, 'final_g': _jnp.float32}
MOMENT_SCALE = {'norm_g': 2.618931e-02, 'ada_w': 3.401339e-02, 'ada_b': 5.655466e-02, 'sc_w_in': 1.106159e-02, 'sc_conv_w': 1.112944e-02, 'sc_w_out': 1.561170e-02, 'lru_w_in': 1.771683e-02, 'lru_conv_w': 1.786128e-02, 'lru_conv_b': 5.513551e-02, 'lru_w_a': 1.769668e-03, 'lru_b_a': 3.386106e-03, 'lru_w_x': 3.327073e-03, 'lru_b_x': 6.928168e-03, 'lru_lambda': 8.707617e-03, 'lru_w_out': 2.441025e-02, 'final_g': 8.024577e+00}


def _to_microbatches(a, axis):
    t = _jnp.moveaxis(a, axis, 0)
    t = t.reshape((N_MICROBATCH, t.shape[0] // N_MICROBATCH) + t.shape[1:])
    return _jnp.moveaxis(t, 1, axis + 1)


def setup_inputs(seed: int = 0) -> dict:
    inp = _fwd_setup_inputs(seed)
    key = _jax.random.fold_in(_jax.random.key(seed), 7919)
    shape, _ = _output_shape()
    out = dict(inp)
    out["loss_target"] = _jax.random.normal(_jax.random.fold_in(key, 0), shape, _jnp.float32)
    for i, name in enumerate(TWIN_WEIGHTS):
        w = inp[name].astype(_jnp.float32)
        if MOMENT_SCALE is None:
            s = _jnp.sqrt(_jnp.mean(_jnp.square(w)) + 1e-30)
        else:
            s = MOMENT_SCALE[name]
        km, kv = _jax.random.split(_jax.random.fold_in(key, i + 1))
        out[name] = w
        out["m_" + name] = s * _jax.random.normal(km, w.shape, _jnp.float32)
        out["v_" + name] = (s * s) * _jax.random.uniform(kv, w.shape, _jnp.float32, 0.5, 1.5)
    if N_MICROBATCH > 1:
        for name, axis in PER_EXAMPLE_BATCH_AXIS.items():
            out[name] = _to_microbatches(out[name], axis)
    return {'x': out['x'], 'c': out['c'], 'norm_g': out['norm_g'], 'ada_w': out['ada_w'], 'ada_b': out['ada_b'], 'sc_w_in': out['sc_w_in'], 'sc_conv_w': out['sc_conv_w'], 'sc_w_out': out['sc_w_out'], 'lru_w_in': out['lru_w_in'], 'lru_conv_w': out['lru_conv_w'], 'lru_conv_b': out['lru_conv_b'], 'lru_w_a': out['lru_w_a'], 'lru_b_a': out['lru_b_a'], 'lru_w_x': out['lru_w_x'], 'lru_b_x': out['lru_b_x'], 'lru_lambda': out['lru_lambda'], 'lru_w_out': out['lru_w_out'], 'final_g': out['final_g'], 'loss_target': out['loss_target'], 'm_norm_g': out['m_norm_g'], 'm_ada_w': out['m_ada_w'], 'm_ada_b': out['m_ada_b'], 'm_sc_w_in': out['m_sc_w_in'], 'm_sc_conv_w': out['m_sc_conv_w'], 'm_sc_w_out': out['m_sc_w_out'], 'm_lru_w_in': out['m_lru_w_in'], 'm_lru_conv_w': out['m_lru_conv_w'], 'm_lru_conv_b': out['m_lru_conv_b'], 'm_lru_w_a': out['m_lru_w_a'], 'm_lru_b_a': out['m_lru_b_a'], 'm_lru_w_x': out['m_lru_w_x'], 'm_lru_b_x': out['m_lru_b_x'], 'm_lru_lambda': out['m_lru_lambda'], 'm_lru_w_out': out['m_lru_w_out'], 'm_final_g': out['m_final_g'], 'v_norm_g': out['v_norm_g'], 'v_ada_w': out['v_ada_w'], 'v_ada_b': out['v_ada_b'], 'v_sc_w_in': out['v_sc_w_in'], 'v_sc_conv_w': out['v_sc_conv_w'], 'v_sc_w_out': out['v_sc_w_out'], 'v_lru_w_in': out['v_lru_w_in'], 'v_lru_conv_w': out['v_lru_conv_w'], 'v_lru_conv_b': out['v_lru_conv_b'], 'v_lru_w_a': out['v_lru_w_a'], 'v_lru_b_a': out['v_lru_b_a'], 'v_lru_w_x': out['v_lru_w_x'], 'v_lru_b_x': out['v_lru_b_x'], 'v_lru_lambda': out['v_lru_lambda'], 'v_lru_w_out': out['v_lru_w_out'], 'v_final_g': out['v_final_g']}


def _loss(weights, diff, rest, loss_target):
    with _jax.named_scope("forward"):
        args = {**rest, TWIN_DIFF_INPUT: diff, **{k: w.astype(_WEIGHT_DTYPES[k]) for k, w in weights.items()}}
        y = _forward(args)
    with _jax.named_scope("loss_head"):
        err = _jnp.square(y.astype(_jnp.float32) - loss_target)
        return 0.5 * _jnp.sum(_jnp.mean(err, axis=-1)) if err.ndim else 0.5 * err


def _adamw(w, g, m, v):
    m = ADAM_B1 * m + (1.0 - ADAM_B1) * g
    v = ADAM_B2 * v + (1.0 - ADAM_B2) * _jnp.square(g)
    m_hat = m / (1.0 - ADAM_B1 ** ADAM_STEP)
    v_hat = v / (1.0 - ADAM_B2 ** ADAM_STEP)
    delta = -ADAM_LR * (m_hat / (_jnp.sqrt(v_hat) + ADAM_EPS) + ADAM_WD * w)
    return delta, m, v


def reference(x, c, norm_g, ada_w, ada_b, sc_w_in, sc_conv_w, sc_w_out, lru_w_in, lru_conv_w, lru_conv_b, lru_w_a, lru_b_a, lru_w_x, lru_b_x, lru_lambda, lru_w_out, final_g, loss_target, m_norm_g, m_ada_w, m_ada_b, m_sc_w_in, m_sc_conv_w, m_sc_w_out, m_lru_w_in, m_lru_conv_w, m_lru_conv_b, m_lru_w_a, m_lru_b_a, m_lru_w_x, m_lru_b_x, m_lru_lambda, m_lru_w_out, m_final_g, v_norm_g, v_ada_w, v_ada_b, v_sc_w_in, v_sc_conv_w, v_sc_w_out, v_lru_w_in, v_lru_conv_w, v_lru_conv_b, v_lru_w_a, v_lru_b_a, v_lru_w_x, v_lru_b_x, v_lru_lambda, v_lru_w_out, v_final_g):
    given = dict(x=x, c=c, norm_g=norm_g, ada_w=ada_w, ada_b=ada_b, sc_w_in=sc_w_in, sc_conv_w=sc_conv_w, sc_w_out=sc_w_out, lru_w_in=lru_w_in, lru_conv_w=lru_conv_w, lru_conv_b=lru_conv_b, lru_w_a=lru_w_a, lru_b_a=lru_b_a, lru_w_x=lru_w_x, lru_b_x=lru_b_x, lru_lambda=lru_lambda, lru_w_out=lru_w_out, final_g=final_g, loss_target=loss_target, m_norm_g=m_norm_g, m_ada_w=m_ada_w, m_ada_b=m_ada_b, m_sc_w_in=m_sc_w_in, m_sc_conv_w=m_sc_conv_w, m_sc_w_out=m_sc_w_out, m_lru_w_in=m_lru_w_in, m_lru_conv_w=m_lru_conv_w, m_lru_conv_b=m_lru_conv_b, m_lru_w_a=m_lru_w_a, m_lru_b_a=m_lru_b_a, m_lru_w_x=m_lru_w_x, m_lru_b_x=m_lru_b_x, m_lru_lambda=m_lru_lambda, m_lru_w_out=m_lru_w_out, m_final_g=m_final_g, v_norm_g=v_norm_g, v_ada_w=v_ada_w, v_ada_b=v_ada_b, v_sc_w_in=v_sc_w_in, v_sc_conv_w=v_sc_conv_w, v_sc_w_out=v_sc_w_out, v_lru_w_in=v_lru_w_in, v_lru_conv_w=v_lru_conv_w, v_lru_conv_b=v_lru_conv_b, v_lru_w_a=v_lru_w_a, v_lru_b_a=v_lru_b_a, v_lru_w_x=v_lru_w_x, v_lru_b_x=v_lru_b_x, v_lru_lambda=v_lru_lambda, v_lru_w_out=v_lru_w_out, v_final_g=v_final_g)
    weights = {n: given[n] for n in TWIN_WEIGHTS}
    shared = {n: given[n] for n in SHARED_INPUTS}
    per_example = {n: given[n] for n in ['x', 'c']}
    grad_fn = _jax.value_and_grad(_loss, argnums=(0, 1))

    def one_microbatch(ex, loss_target):
        ex = dict(ex)
        diff = ex.pop(TWIN_DIFF_INPUT)
        return grad_fn(weights, diff, {**shared, **ex}, loss_target)

    if N_MICROBATCH == 1:
        loss, (grad_w, grad_x) = one_microbatch(per_example, given["loss_target"])
    else:
        def body(carry, xs):
            loss_sum, grad_sum = carry
            l_k, (gw_k, gx_k) = one_microbatch(xs[0], xs[1])
            with _jax.named_scope("update"):
                return (loss_sum + l_k, _jax.tree.map(_jnp.add, grad_sum, gw_k)), gx_k

        init = (_jnp.zeros((), _jnp.float32), _jax.tree.map(_jnp.zeros_like, weights))
        (loss, grad_w), grad_x = _jax.lax.scan(body, init, (per_example, given["loss_target"]))
    with _jax.named_scope("update"):
        delta_w, new_m, new_v = {}, {}, {}
        for n in TWIN_WEIGHTS:
            delta_w[n], new_m[n], new_v[n] = _adamw(weights[n], grad_w[n], given["m_" + n], given["v_" + n])
    return (loss, grad_x, *[grad_w[n] for n in TWIN_WEIGHTS], *[delta_w[n] for n in TWIN_WEIGHTS],
            *[new_m[n] for n in TWIN_WEIGHTS], *[new_v[n] for n in TWIN_WEIGHTS])
```

```python
import functools
import math

import jax
import jax.numpy as jnp
from jax import lax
from jax.experimental import pallas as pl
from jax.experimental.pallas import tpu as pltpu

N_DEV = 8
LANES = 128
EPS = 1e-6
RGLRU_C = 8.0
ADAM_LR = 0.001
ADAM_B1 = 0.9
ADAM_B2 = 0.999
ADAM_EPS = 1e-08
ADAM_WD = 0.01
ADAM_STEP = 10
VMEM_LIMIT = 56 * 1024 * 1024
MESH = pl.DeviceIdType.MESH
F32 = jnp.float32
BF16 = jnp.bfloat16
ANY = pl.BlockSpec(memory_space=pl.ANY)
VMEM_SPEC = pl.BlockSpec(memory_space=pltpu.VMEM)


def _tile(n, pref):
    t = min(n, pref)
    assert n % t == 0, (n, pref)
    return t


def _params(*sem):
    return pltpu.CompilerParams(dimension_semantics=sem, vmem_limit_bytes=VMEM_LIMIT)


def _position():
    return lax.axis_index("x"), lax.axis_index("y"), lax.axis_index("c")


def _flip(x, y, k):
    return (1 - x if k & 2 else x), (1 - y if k & 1 else y)


def _small_gather(v, name):
    rows = v.shape[0]

    def body(v_ref, out_ref, send_sems, recv_sems):
        x, y, c = _position()
        me = 4 * x + 2 * y + c
        out_ref[me] = v_ref[...]
        copies = []
        for k in range(1, N_DEV):
            px, py = _flip(x, y, k >> 1)
            pc = 1 - c if k & 1 else c
            cp = pltpu.make_async_remote_copy(
                src_ref=v_ref, dst_ref=out_ref.at[me],
                send_sem=send_sems.at[k - 1], recv_sem=recv_sems.at[k - 1],
                device_id=(px, py, pc), device_id_type=MESH)
            cp.start()
            copies.append((cp, 4 * px + 2 * py + pc))
        for k, (cp, peer) in enumerate(copies):
            pltpu.make_async_remote_copy(
                src_ref=v_ref, dst_ref=out_ref.at[peer],
                send_sem=send_sems.at[k], recv_sem=recv_sems.at[k],
                device_id=(x, y, c), device_id_type=MESH).wait_recv()
        for cp, _ in copies:
            cp.wait_send()

    return pl.pallas_call(
        body, name=name,
        out_shape=jax.ShapeDtypeStruct((N_DEV, rows, LANES), F32),
        in_specs=[VMEM_SPEC], out_specs=VMEM_SPEC,
        scratch_shapes=[pltpu.SemaphoreType.DMA((N_DEV - 1,)),
                        pltpu.SemaphoreType.DMA((N_DEV - 1,))],
        compiler_params=pltpu.CompilerParams(vmem_limit_bytes=VMEM_LIMIT),
    )(v)


def _weights_all_gather(shards, name):
    n = len(shards)

    def body(*refs):
        ins, outs = refs[:n], refs[n:2 * n]
        send_sems, recv_sems, local_sems = refs[2 * n:]
        x, y, c = _position()
        me = 4 * x + 2 * y + c
        sibling = (x, y, 1 - c)
        chips = [_flip(x, y, k) for k in (1, 2, 3)]

        def copy(i, k, block, to, src=None):
            dst = outs[i].at[block]
            return pltpu.make_async_remote_copy(
                src_ref=dst if src is None else src, dst_ref=dst,
                send_sem=send_sems.at[i, k], recv_sem=recv_sems.at[i, k],
                device_id=to, device_id_type=MESH)

        started = []
        local = []
        for i in range(n):
            for j, (px, py) in enumerate(chips):
                cp = copy(i, 1 + j, me, (px, py, c), src=ins[i])
                cp.start()
                started.append(cp)
        for i in range(n):
            cp = copy(i, 0, me, sibling, src=ins[i])
            cp.start()
            started.append(cp)
            mine = pltpu.make_async_copy(ins[i], outs[i].at[me], local_sems.at[i])
            mine.start()
            local.append(mine)
        for i in range(n):
            for j, (px, py) in enumerate(chips):
                block = 4 * px + 2 * py + c
                copy(i, 1 + j, block, (x, y, c)).wait_recv()
                cp = copy(i, 4 + j, block, sibling)
                cp.start()
                started.append(cp)
        for i in range(n):
            copy(i, 0, 4 * x + 2 * y + (1 - c), (x, y, c)).wait_recv()
            for j, (px, py) in enumerate(chips):
                copy(i, 4 + j, 4 * px + 2 * py + (1 - c), (x, y, c)).wait_recv()
        for cp in started:
            cp.wait_send()
        for mine in local:
            mine.wait()

    return pl.pallas_call(
        body, name=name,
        out_shape=[jax.ShapeDtypeStruct((N_DEV,) + s.shape, s.dtype) for s in shards],
        in_specs=[ANY] * n, out_specs=[ANY] * n,
        scratch_shapes=[pltpu.SemaphoreType.DMA((n, 7)), pltpu.SemaphoreType.DMA((n, 7)),
                        pltpu.SemaphoreType.DMA((n,))],
    )(*shards)


def _pair_exchange(parts, name):
    n = len(parts)

    def body(*refs):
        ins, outs = refs[:n], refs[n:2 * n]
        send_sems, recv_sems = refs[2 * n:]
        x, y, c = _position()
        copies = []
        for i in range(n):
            cp = pltpu.make_async_remote_copy(
                src_ref=ins[i].at[:, pl.ds(1 - c, 1)], dst_ref=outs[i],
                send_sem=send_sems.at[i], recv_sem=recv_sems.at[i],
                device_id=(x, y, 1 - c), device_id_type=MESH)
            cp.start()
            copies.append(cp)
        for cp in copies:
            cp.wait_recv()
        for cp in copies:
            cp.wait_send()

    return pl.pallas_call(
        body, name=name,
        out_shape=[jax.ShapeDtypeStruct((4, 1) + p.shape[2:], p.dtype) for p in parts],
        in_specs=[ANY] * n, out_specs=[ANY] * n,
        scratch_shapes=[pltpu.SemaphoreType.DMA((n,)), pltpu.SemaphoreType.DMA((n,))],
    )(*parts)


def _chip_exchange(sums, name):
    n = len(sums)

    def body(*refs):
        ins, outs = refs[:n], refs[n:2 * n]
        send_sems, recv_sems = refs[2 * n:]
        x, y, c = _position()
        copies = []
        for i in range(n):
            for j in range(3):
                px, py = _flip(x, y, j + 1)
                cp = pltpu.make_async_remote_copy(
                    src_ref=ins[i].at[j], dst_ref=outs[i].at[j],
                    send_sem=send_sems.at[i, j], recv_sem=recv_sems.at[i, j],
                    device_id=(px, py, c), device_id_type=MESH)
                cp.start()
                copies.append(cp)
        for cp in copies:
            cp.wait_recv()
        for cp in copies:
            cp.wait_send()

    return pl.pallas_call(
        body, name=name,
        out_shape=[jax.ShapeDtypeStruct(s.shape, s.dtype) for s in sums],
        in_specs=[ANY] * n, out_specs=[ANY] * n,
        scratch_shapes=[pltpu.SemaphoreType.DMA((n, 3)), pltpu.SemaphoreType.DMA((n, 3))],
    )(*sums)


def _pair_sum(idx, part, got, name):
    _, _, rows, cols = part.shape
    tr = _tile(rows, 512)

    def body(idx_ref, p_ref, q_ref, o_ref):
        o_ref[...] = (p_ref[...].astype(F32) + q_ref[...].astype(F32)).astype(o_ref.dtype)

    grid_spec = pltpu.PrefetchScalarGridSpec(
        num_scalar_prefetch=1, grid=(3, rows // tr),
        in_specs=[pl.BlockSpec((None, None, tr, cols), lambda j, r, idx: (idx[j], idx[4], r, 0)),
                  pl.BlockSpec((None, None, tr, cols), lambda j, r, idx: (idx[j], 0, r, 0))],
        out_specs=pl.BlockSpec((None, tr, cols), lambda j, r, idx: (j, r, 0)))
    return pl.pallas_call(
        body, name=name, grid_spec=grid_spec,
        out_shape=jax.ShapeDtypeStruct((3, rows, cols), part.dtype),
        compiler_params=_params("arbitrary", "arbitrary"),
    )(idx, part, got)


def _mm_proj(h, wg, groups, name):
    s, k = h.shape
    nchunk, _, n = wg.shape
    e = nchunk * n // groups
    tn = _tile(min(n, e), 512)

    def body(h_ref, w_ref, o_ref):
        o_ref[...] = jnp.dot(h_ref[...], w_ref[...], preferred_element_type=F32).astype(o_ref.dtype)

    return pl.pallas_call(
        body, name=name, grid=(nchunk * n // tn,),
        in_specs=[pl.BlockSpec((s, k), lambda j: (0, 0)),
                  pl.BlockSpec((None, k, tn), lambda j: ((j * tn) // n, 0, ((j * tn) % n) // tn))],
        out_specs=pl.BlockSpec((None, s, tn), lambda j: ((j * tn) // e, 0, ((j * tn) % e) // tn)),
        out_shape=jax.ShapeDtypeStruct((groups, s, e), BF16),
        compiler_params=_params("arbitrary"),
    )(h, wg)


def _mm_out(yb, w, x, gate, name):
    s, k = yb.shape
    d = w.shape[1]
    tn = _tile(d, 512)
    tk = _tile(k, 1024)
    nk = k // tk

    def body(a_ref, w_ref, x_ref, g_ref, xo_ref, y_ref, acc_ref):
        kk = pl.program_id(1)

        @pl.when(kk == 0)
        def _():
            acc_ref[...] = jnp.zeros_like(acc_ref)

        acc_ref[...] += jnp.dot(a_ref[...], w_ref[...], preferred_element_type=F32)

        @pl.when(kk == nk - 1)
        def _():
            y = acc_ref[...]
            y_ref[...] = y.astype(y_ref.dtype)
            xo_ref[...] = x_ref[...] + g_ref[...] * y

    return pl.pallas_call(
        body, name=name, grid=(d // tn, nk),
        in_specs=[pl.BlockSpec((s, tk), lambda j, kk: (0, kk)),
                  pl.BlockSpec((tk, tn), lambda j, kk: (kk, j)),
                  pl.BlockSpec((s, tn), lambda j, kk: (0, j)),
                  pl.BlockSpec((1, tn), lambda j, kk: (0, j))],
        out_specs=[pl.BlockSpec((s, tn), lambda j, kk: (0, j)),
                   pl.BlockSpec((s, tn), lambda j, kk: (0, j))],
        out_shape=[jax.ShapeDtypeStruct((s, d), F32), jax.ShapeDtypeStruct((s, d), BF16)],
        scratch_shapes=[pltpu.VMEM((s, tn), F32)],
        compiler_params=_params("arbitrary", "arbitrary"),
    )(yb, w, x, gate)


def _mm_nt(a3, w3, out_dtype, name):
    g, s, ea = a3.shape
    cw, n, nw = w3.shape
    total = g * ea
    assert total == cw * nw
    tk = _tile(min(ea, nw), 1024)
    tn = _tile(n, 1024)
    nk = total // tk

    def body(a_ref, w_ref, o_ref, acc_ref):
        kk = pl.program_id(1)

        @pl.when(kk == 0)
        def _():
            acc_ref[...] = jnp.zeros_like(acc_ref)

        acc_ref[...] += lax.dot_general(a_ref[...], w_ref[...], (((1,), (1,)), ((), ())),
                                        preferred_element_type=F32)

        @pl.when(kk == nk - 1)
        def _():
            o_ref[...] = acc_ref[...].astype(o_ref.dtype)

    return pl.pallas_call(
        body, name=name, grid=(n // tn, nk),
        in_specs=[pl.BlockSpec((None, s, tk), lambda j, kk: ((kk * tk) // ea, 0, ((kk * tk) % ea) // tk)),
                  pl.BlockSpec((None, tn, tk), lambda j, kk: ((kk * tk) // nw, j, ((kk * tk) % nw) // tk))],
        out_specs=pl.BlockSpec((s, tn), lambda j, kk: (0, j)),
        out_shape=jax.ShapeDtypeStruct((s, n), out_dtype),
        scratch_shapes=[pltpu.VMEM((s, tn), F32)],
        compiler_params=_params("arbitrary", "arbitrary"),
    )(a3, w3)


def _mm_tn(a, b3, nchunk, name):
    s, ka = a.shape
    g, _, eb = b3.shape
    n = g * eb // nchunk
    tm = _tile(ka, 1024)
    tn = _tile(min(n, eb), 1024)

    def body(a_ref, b_ref, o_ref, at_ref):
        @pl.when(pl.program_id(1) == 0)
        def _():
            at_ref[...] = a_ref[...].astype(F32).T.astype(at_ref.dtype)

        o_ref[...] = jnp.dot(at_ref[...], b_ref[...], preferred_element_type=F32).astype(o_ref.dtype)

    return pl.pallas_call(
        body, name=name, grid=(ka // tm, g * eb // tn),
        in_specs=[pl.BlockSpec((s, tm), lambda i, j: (0, i)),
                  pl.BlockSpec((None, s, tn), lambda i, j: ((j * tn) // eb, 0, ((j * tn) % eb) // tn))],
        out_specs=pl.BlockSpec((None, tm, tn), lambda i, j: ((j * tn) // n, i, ((j * tn) % n) // tn)),
        out_shape=jax.ShapeDtypeStruct((nchunk, ka, n), BF16),
        scratch_shapes=[pltpu.VMEM((tm, s), BF16)],
        compiler_params=_params("arbitrary", "arbitrary"),
    )(a, b3)


def _sigmoid(z):
    return jax.nn.sigmoid(z)


def _shift_down(v, k, fill=0.0):
    if k == 0:
        return v
    row = lax.broadcasted_iota(jnp.int32, v.shape, 0)
    return jnp.where(row >= k, pltpu.roll(v, k, 0), fill)


def _shift_up(v, k, fill=0.0):
    if k == 0:
        return v
    s = v.shape[0]
    row = lax.broadcasted_iota(jnp.int32, v.shape, 0)
    return jnp.where(row < s - k, pltpu.roll(v, s - k, 0), fill)


def _scan(a, b, shift):
    s = a.shape[0]
    k = 1
    while k < s:
        b = a * shift(b, k, 0.0) + b
        if 2 * k < s:
            a = a * shift(a, k, 1.0)
        k *= 2
    return b


def _norm_mod(x, g, scale, shift, name):
    s, d = x.shape
    ts = _tile(s, 256)

    def body(x_ref, g_ref, sc_ref, sh_ref, h_ref):
        xv = x_ref[...]
        rstd = lax.rsqrt(jnp.mean(xv * xv, axis=-1, keepdims=True) + EPS)
        nrm = xv * rstd * g_ref[...]
        h_ref[...] = (nrm * (1.0 + sc_ref[...]) + sh_ref[...]).astype(h_ref.dtype)

    vec = pl.BlockSpec((1, d), lambda i: (0, 0))
    return pl.pallas_call(
        body, name=name, grid=(s // ts,),
        in_specs=[pl.BlockSpec((ts, d), lambda i: (i, 0)), vec, vec, vec],
        out_specs=pl.BlockSpec((ts, d), lambda i: (i, 0)),
        out_shape=jax.ShapeDtypeStruct((s, d), BF16),
        compiler_params=_params("arbitrary"),
    )(x, g, scale, shift)


def _norm_mod_bwd(x, dh, dx_res, g, scale, name):
    s, d = x.shape
    ts = _tile(s, 256)

    def body(x_ref, dh_ref, dr_ref, g_ref, sc_ref, dx_ref, dsc_ref, dsh_ref, dg_ref):
        @pl.when(pl.program_id(0) == 0)
        def _():
            dsc_ref[...] = jnp.zeros_like(dsc_ref)
            dsh_ref[...] = jnp.zeros_like(dsh_ref)
            dg_ref[...] = jnp.zeros_like(dg_ref)

        xv = x_ref[...]
        dh_v = dh_ref[...].astype(F32)
        gv = g_ref[...]
        rstd = lax.rsqrt(jnp.mean(xv * xv, axis=-1, keepdims=True) + EPS)
        xhat = xv * rstd
        dsc_ref[...] += jnp.sum(dh_v * xhat * gv, axis=0, keepdims=True)
        dsh_ref[...] += jnp.sum(dh_v, axis=0, keepdims=True)
        dn = dh_v * (1.0 + sc_ref[...])
        dg_ref[...] += jnp.sum(dn * xhat, axis=0, keepdims=True)
        dxhat = dn * gv
        proj = jnp.mean(dxhat * xhat, axis=-1, keepdims=True)
        dx_ref[...] = dr_ref[...] + rstd * (dxhat - xhat * proj)

    row = pl.BlockSpec((ts, d), lambda i: (i, 0))
    vec = pl.BlockSpec((1, d), lambda i: (0, 0))
    return pl.pallas_call(
        body, name=name, grid=(s // ts,),
        in_specs=[row, row, row, vec, vec],
        out_specs=[row, vec, vec, vec],
        out_shape=[jax.ShapeDtypeStruct((s, d), F32)] + [jax.ShapeDtypeStruct((1, d), F32)] * 3,
        compiler_params=_params("arbitrary"),
    )(x, dh, dx_res, g, scale)


def _final_loss(x, g, target, name):
    s, d = x.shape
    ts = _tile(s, 256)

    def body(x_ref, g_ref, t_ref, dx_ref, loss_ref, dg_ref):
        @pl.when(pl.program_id(0) == 0)
        def _():
            loss_ref[...] = jnp.zeros_like(loss_ref)
            dg_ref[...] = jnp.zeros_like(dg_ref)

        xv = x_ref[...]
        gv = g_ref[...]
        rstd = lax.rsqrt(jnp.mean(xv * xv, axis=-1, keepdims=True) + EPS)
        xhat = xv * rstd
        err = xhat * gv - t_ref[...]
        loss_ref[...] += 0.5 * jnp.sum(jnp.mean(err * err, axis=-1, keepdims=True))
        dy = err * (1.0 / d)
        dg_ref[...] += jnp.sum(dy * xhat, axis=0, keepdims=True)
        dxhat = dy * gv
        proj = jnp.mean(dxhat * xhat, axis=-1, keepdims=True)
        dx_ref[...] = rstd * (dxhat - xhat * proj)

    row = pl.BlockSpec((ts, d), lambda i: (i, 0))
    vec = pl.BlockSpec((1, d), lambda i: (0, 0))
    return pl.pallas_call(
        body, name=name, grid=(s // ts,),
        in_specs=[row, vec, row],
        out_specs=[row, pl.BlockSpec((1, LANES), lambda i: (0, 0)), vec],
        out_shape=[jax.ShapeDtypeStruct((s, d), F32), jax.ShapeDtypeStruct((1, LANES), F32),
                   jax.ShapeDtypeStruct((1, d), F32)],
        compiler_params=_params("arbitrary"),
    )(x, g, target)


def _gate_bwd(dx, y, gate, name):
    s, d = dx.shape
    ts = _tile(s, 256)

    def body(dx_ref, y_ref, g_ref, dy_ref, dgate_ref):
        @pl.when(pl.program_id(0) == 0)
        def _():
            dgate_ref[...] = jnp.zeros_like(dgate_ref)

        dxv = dx_ref[...]
        dy_ref[...] = (dxv * g_ref[...]).astype(dy_ref.dtype)
        dgate_ref[...] += jnp.sum(dxv * y_ref[...].astype(F32), axis=0, keepdims=True)

    row = pl.BlockSpec((ts, d), lambda i: (i, 0))
    vec = pl.BlockSpec((1, d), lambda i: (0, 0))
    return pl.pallas_call(
        body, name=name, grid=(s // ts,),
        in_specs=[row, row, vec], out_specs=[row, vec],
        out_shape=[jax.ShapeDtypeStruct((s, d), BF16), jax.ShapeDtypeStruct((1, d), F32)],
        compiler_params=_params("arbitrary"),
    )(dx, y, gate)


def _conv(v, w_ref, width):
    out = w_ref[width - 1:width, :] * v
    for k in range(width - 1):
        out = out + w_ref[k:k + 1, :] * _shift_down(v, width - 1 - k)
    return out


def _sc_fwd(proj, conv_w, name):
    _, s, e = proj.shape
    te = _tile(e, 256)
    width = conv_w.shape[0]

    def body(b_ref, c_ref, v_ref, g_ref, w_ref, o_ref):
        cv = c_ref[...].astype(F32) * v_ref[...].astype(F32)
        u = _conv(cv, w_ref, width)
        gv = g_ref[...].astype(F32)
        o_ref[...] = (b_ref[...].astype(F32) * u * (gv * _sigmoid(gv))).astype(o_ref.dtype)

    def part(q):
        return pl.BlockSpec((None, s, te), lambda j, q=q: (q, 0, j))

    return pl.pallas_call(
        body, name=name, grid=(e // te,),
        in_specs=[part(0), part(1), part(2), part(3), pl.BlockSpec((width, te), lambda j: (0, j))],
        out_specs=pl.BlockSpec((s, te), lambda j: (0, j)),
        out_shape=jax.ShapeDtypeStruct((s, e), BF16),
        compiler_params=_params("arbitrary"),
    )(proj, proj, proj, proj, conv_w)


def _sc_bwd(proj, dyb, conv_w, name):
    _, s, e = proj.shape
    te = _tile(e, 256)
    width = conv_w.shape[0]

    def body(b_ref, c_ref, v_ref, g_ref, dy_ref, w_ref, dp_ref, vec_ref):
        bv = b_ref[...].astype(F32)
        cvl = c_ref[...].astype(F32)
        vv = v_ref[...].astype(F32)
        gv = g_ref[...].astype(F32)
        dyv = dy_ref[...].astype(F32)
        cv = cvl * vv
        u = _conv(cv, w_ref, width)
        sg = _sigmoid(gv)
        silu = gv * sg
        dp_ref[0] = (dyv * u * silu).astype(dp_ref.dtype)
        du = dyv * bv * silu
        dp_ref[3] = (dyv * bv * u * (sg * (1.0 + gv * (1.0 - sg)))).astype(dp_ref.dtype)
        dcv = w_ref[width - 1:width, :] * du
        vec_ref[...] = jnp.zeros_like(vec_ref)
        vec_ref[width - 1:width, :] = jnp.sum(du * cv, axis=0, keepdims=True)
        for k in range(width - 1):
            sh = width - 1 - k
            dcv = dcv + w_ref[k:k + 1, :] * _shift_up(du, sh)
            vec_ref[k:k + 1, :] = jnp.sum(du * _shift_down(cv, sh), axis=0, keepdims=True)
        dp_ref[1] = (dcv * vv).astype(dp_ref.dtype)
        dp_ref[2] = (dcv * cvl).astype(dp_ref.dtype)

    def part(q):
        return pl.BlockSpec((None, s, te), lambda j, q=q: (q, 0, j))

    return pl.pallas_call(
        body, name=name, grid=(e // te,),
        in_specs=[part(0), part(1), part(2), part(3), pl.BlockSpec((s, te), lambda j: (0, j)),
                  pl.BlockSpec((width, te), lambda j: (0, j))],
        out_specs=[pl.BlockSpec((4, s, te), lambda j: (0, 0, j)),
                   pl.BlockSpec((8, te), lambda j: (0, j))],
        out_shape=[jax.ShapeDtypeStruct((4, s, e), BF16), jax.ShapeDtypeStruct((8, e), F32)],
        compiler_params=_params("arbitrary"),
    )(proj, proj, proj, proj, dyb, conv_w)


def _lru_gates(v_pre, w_ref, cb_ref, wa_ref, ba_ref, wx_ref, bx_ref, lam_ref, width):
    v = _conv(v_pre, w_ref, width) + cb_ref[...]
    vb = v.astype(BF16)
    r = _sigmoid(jnp.dot(vb, wa_ref[...], preferred_element_type=F32) + ba_ref[...])
    i = _sigmoid(jnp.dot(vb, wx_ref[...], preferred_element_type=F32) + bx_ref[...])
    nl = -lam_ref[...]
    sp = jnp.maximum(nl, 0.0) + jnp.log1p(jnp.exp(-jnp.abs(nl)))
    log_a = (-RGLRU_C) * r * sp
    a = jnp.exp(log_a)
    one_minus_a2 = jnp.tanh(-log_a) * (1.0 + a * a)
    mult = jnp.sqrt(one_minus_a2)
    return v, vb, r, i, sp, a, mult


def _lru_specs(s, dh, heads, width):
    head_col = lambda q: pl.BlockSpec((None, s, dh), lambda h, q=q: (q, 0, h))
    vec = pl.BlockSpec((1, dh), lambda h: (0, h))
    mat = pl.BlockSpec((None, dh, dh), lambda h: (h, 0, 0))
    weights = [pl.BlockSpec((width, dh), lambda h: (0, h)), vec, mat, vec, mat, vec, vec]
    return head_col, weights


def _lru_fwd(proj, conv_w, conv_b, w_a, b_a, w_x, b_x, lam, name):
    _, s, e = proj.shape
    heads, dh, _ = w_a.shape
    width = conv_w.shape[0]

    def body(v_ref, g_ref, w_ref, cb_ref, wa_ref, ba_ref, wx_ref, bx_ref, lam_ref, yb_ref, hs_ref):
        v, _, _, i, _, a, mult = _lru_gates(v_ref[...].astype(F32), w_ref, cb_ref, wa_ref, ba_ref,
                                           wx_ref, bx_ref, lam_ref, width)
        hs = _scan(a, mult * i * v, _shift_down)
        hs_ref[...] = hs
        gv = g_ref[...].astype(F32)
        yb_ref[...] = (hs * (gv * _sigmoid(gv))).astype(yb_ref.dtype)

    head_col, weights = _lru_specs(s, dh, heads, width)
    out = pl.BlockSpec((s, dh), lambda h: (0, h))
    return pl.pallas_call(
        body, name=name, grid=(heads,),
        in_specs=[head_col(0), head_col(1)] + weights,
        out_specs=[out, out],
        out_shape=[jax.ShapeDtypeStruct((s, e), BF16), jax.ShapeDtypeStruct((s, e), F32)],
        compiler_params=_params("arbitrary"),
    )(proj, proj, conv_w, conv_b, w_a, b_a, w_x, b_x, lam)


def _lru_bwd(proj, hs, dyb, conv_w, conv_b, w_a, b_a, w_x, b_x, lam, name):
    _, s, e = proj.shape
    heads, dh, _ = w_a.shape
    width = conv_w.shape[0]

    def body(v_ref, g_ref, hs_ref, dy_ref, w_ref, cb_ref, wa_ref, ba_ref, wx_ref, bx_ref, lam_ref,
             dp_ref, dwa_ref, dwx_ref, vec_ref):
        v_pre = v_ref[...].astype(F32)
        v, vb, r, i, sp, a, mult = _lru_gates(v_pre, w_ref, cb_ref, wa_ref, ba_ref, wx_ref, bx_ref,
                                              lam_ref, width)
        hs = hs_ref[...]
        gv = g_ref[...].astype(F32)
        dyv = dy_ref[...].astype(F32)
        sg = _sigmoid(gv)
        dp_ref[1] = (dyv * hs * (sg * (1.0 + gv * (1.0 - sg)))).astype(dp_ref.dtype)
        dhs = dyv * (gv * sg)
        d_h = _scan(_shift_up(a, 1), dhs, _shift_up)
        da = d_h * _shift_down(hs, 1)
        iv = i * v
        dlog_a = da * a - (d_h * iv) * (a * a) / mult
        di = d_h * mult * v
        dv = d_h * mult * i
        dzr = dlog_a * (-RGLRU_C) * sp * r * (1.0 - r)
        dzi = di * i * (1.0 - i)
        dsp = jnp.sum(dlog_a * r, axis=0, keepdims=True) * (-RGLRU_C)
        vec_ref[...] = jnp.zeros_like(vec_ref)
        vec_ref[0:1, :] = jnp.sum(dzr, axis=0, keepdims=True)
        vec_ref[1:2, :] = jnp.sum(dzi, axis=0, keepdims=True)
        vec_ref[2:3, :] = -dsp * _sigmoid(-lam_ref[...])
        dzr_b = dzr.astype(BF16)
        dzi_b = dzi.astype(BF16)
        vt = vb.astype(F32).T.astype(BF16)
        dwa_ref[...] = jnp.dot(vt, dzr_b, preferred_element_type=F32).astype(dwa_ref.dtype)
        dwx_ref[...] = jnp.dot(vt, dzi_b, preferred_element_type=F32).astype(dwx_ref.dtype)
        nt = (((1,), (1,)), ((), ()))
        dv = dv + lax.dot_general(dzr_b, wa_ref[...], nt, preferred_element_type=F32)
        dv = dv + lax.dot_general(dzi_b, wx_ref[...], nt, preferred_element_type=F32)
        vec_ref[3:4, :] = jnp.sum(dv, axis=0, keepdims=True)
        dvp = w_ref[width - 1:width, :] * dv
        vec_ref[4 + width - 1:4 + width, :] = jnp.sum(dv * v_pre, axis=0, keepdims=True)
        for k in range(width - 1):
            sh = width - 1 - k
            dvp = dvp + w_ref[k:k + 1, :] * _shift_up(dv, sh)
            vec_ref[4 + k:5 + k, :] = jnp.sum(dv * _shift_down(v_pre, sh), axis=0, keepdims=True)
        dp_ref[0] = dvp.astype(dp_ref.dtype)

    head_col, weights = _lru_specs(s, dh, heads, width)
    col = pl.BlockSpec((s, dh), lambda h: (0, h))
    mat = pl.BlockSpec((None, dh, dh), lambda h: (h, 0, 0))
    return pl.pallas_call(
        body, name=name, grid=(heads,),
        in_specs=[head_col(0), head_col(1), col, col] + weights,
        out_specs=[pl.BlockSpec((2, s, dh), lambda h: (0, 0, h)), mat, mat,
                   pl.BlockSpec((16, dh), lambda h: (0, h))],
        out_shape=[jax.ShapeDtypeStruct((2, s, e), BF16),
                   jax.ShapeDtypeStruct((heads, dh, dh), BF16),
                   jax.ShapeDtypeStruct((heads, dh, dh), BF16),
                   jax.ShapeDtypeStruct((16, e), F32)],
        compiler_params=_params("arbitrary"),
    )(proj, proj, hs, dyb, conv_w, conv_b, w_a, b_a, w_x, b_x, lam)


def _ada_mod(c_all, w, b, name):
    layers, d, f = w.shape
    nb = c_all.shape[0]

    def body(c_ref, w_ref, b_ref, o_ref):
        cv = c_ref[...]
        sc = cv * _sigmoid(cv)
        o_ref[...] = jnp.dot(sc, w_ref[...], preferred_element_type=F32,
                             precision=lax.Precision.HIGHEST) + b_ref[...]

    return pl.pallas_call(
        body, name=name, grid=(layers,),
        in_specs=[pl.BlockSpec((nb, d), lambda l: (0, 0)),
                  pl.BlockSpec((None, d, f), lambda l: (l, 0, 0)),
                  pl.BlockSpec((None, 1, f), lambda l: (l, 0, 0))],
        out_specs=pl.BlockSpec((None, nb, f), lambda l: (l, 0, 0)),
        out_shape=jax.ShapeDtypeStruct((layers, nb, f), F32),
        compiler_params=_params("arbitrary"),
    )(c_all, w, b)


def _ada_grad(c_all_t, dmod, name):
    d, nb = c_all_t.shape
    layers, _, f = dmod.shape

    def body(c_ref, dm_ref, o_ref):
        cv = c_ref[...]
        sc = cv * _sigmoid(cv)
        acc = sc[:, 0:1] * dm_ref[0:1, :]
        for k in range(1, nb):
            acc = acc + sc[:, k:k + 1] * dm_ref[k:k + 1, :]
        o_ref[...] = acc

    return pl.pallas_call(
        body, name=name, grid=(layers,),
        in_specs=[pl.BlockSpec((d, nb), lambda l: (0, 0)),
                  pl.BlockSpec((None, nb, f), lambda l: (l, 0, 0))],
        out_specs=pl.BlockSpec((None, d, f), lambda l: (l, 0, 0)),
        out_shape=jax.ShapeDtypeStruct((layers, d, f), F32),
        compiler_params=_params("arbitrary"),
    )(c_all_t, dmod)


def _device_sum(g, name):
    _, rows, _ = g.shape

    def body(g_ref, o_ref):
        acc = g_ref[0]
        for k in range(1, N_DEV):
            acc = acc + g_ref[k]
        o_ref[...] = acc

    return pl.pallas_call(
        body, name=name,
        in_specs=[VMEM_SPEC], out_specs=VMEM_SPEC,
        out_shape=jax.ShapeDtypeStruct((rows, LANES), F32),
        compiler_params=pltpu.CompilerParams(vmem_limit_bytes=VMEM_LIMIT),
    )(g)


def _adamw_math(w, g, m, v):
    m = ADAM_B1 * m + (1.0 - ADAM_B1) * g
    v = ADAM_B2 * v + (1.0 - ADAM_B2) * (g * g)
    m_hat = m / (1.0 - ADAM_B1 ** ADAM_STEP)
    v_hat = v / (1.0 - ADAM_B2 ** ADAM_STEP)
    delta = -ADAM_LR * (m_hat / (jnp.sqrt(v_hat) + ADAM_EPS) + ADAM_WD * w)
    return delta, m, v


def _adamw(w, g, m, v, name):
    rows, cols = w.shape
    tr = _tile(rows, 256)

    def body(w_ref, g_ref, m_ref, v_ref, d_ref, mo_ref, vo_ref):
        d_ref[...], mo_ref[...], vo_ref[...] = _adamw_math(w_ref[...], g_ref[...], m_ref[...], v_ref[...])

    blk = pl.BlockSpec((tr, cols), lambda i: (i, 0))
    return pl.pallas_call(
        body, name=name, grid=(rows // tr,),
        in_specs=[blk] * 4, out_specs=[blk] * 3,
        out_shape=[jax.ShapeDtypeStruct((rows, cols), F32)] * 3,
        compiler_params=_params("arbitrary"),
    )(w, g, m, v)


def _adamw_reduced(idx, w, m, v, part, got, recv, name):
    rows, cols = w.shape
    tr = _tile(rows, 256)

    def body(idx_ref, w_ref, m_ref, v_ref, p_ref, q_ref, u_ref, g_ref, d_ref, mo_ref, vo_ref):
        g = p_ref[...].astype(F32) + q_ref[...].astype(F32)
        for j in range(3):
            g = g + u_ref[j].astype(F32)
        g_ref[...] = g
        d_ref[...], mo_ref[...], vo_ref[...] = _adamw_math(w_ref[...], g, m_ref[...], v_ref[...])

    blk = pl.BlockSpec((tr, cols), lambda i, idx: (i, 0))
    grid_spec = pltpu.PrefetchScalarGridSpec(
        num_scalar_prefetch=1, grid=(rows // tr,),
        in_specs=[blk, blk, blk,
                  pl.BlockSpec((None, None, tr, cols), lambda i, idx: (idx[3], idx[4], i, 0)),
                  pl.BlockSpec((None, None, tr, cols), lambda i, idx: (idx[3], 0, i, 0)),
                  pl.BlockSpec((3, tr, cols), lambda i, idx: (0, i, 0))],
        out_specs=[blk] * 4)
    return pl.pallas_call(
        body, name=name, grid_spec=grid_spec,
        out_shape=[jax.ShapeDtypeStruct((rows, cols), F32)] * 4,
        compiler_params=_params("arbitrary"),
    )(idx, w, m, v, part, got, recv)


def _pack(vectors):
    flat = jnp.concatenate([v.reshape(-1).astype(F32) for v in vectors])
    pad = (-flat.shape[0]) % (8 * LANES)
    return jnp.pad(flat, (0, pad)).reshape(-1, LANES)


def _unpack(flat, shapes):
    out, off = [], 0
    for shp in shapes:
        size = math.prod(shp)
        out.append(flat[..., off:off + size].reshape(flat.shape[:-1] + tuple(shp)))
        off += size
    return out


def _my_slice(full, me, axis):
    size = full.shape[axis] // N_DEV
    return lax.dynamic_slice_in_dim(full, me * size, size, axis)


def kernel(x, c, norm_g, ada_w, ada_b, sc_w_in, sc_conv_w, sc_w_out, lru_w_in, lru_conv_w, lru_conv_b, lru_w_a, lru_b_a, lru_w_x, lru_b_x, lru_lambda, lru_w_out, final_g, loss_target, m_norm_g, m_ada_w, m_ada_b, m_sc_w_in, m_sc_conv_w, m_sc_w_out, m_lru_w_in, m_lru_conv_w, m_lru_conv_b, m_lru_w_a, m_lru_b_a, m_lru_w_x, m_lru_b_x, m_lru_lambda, m_lru_w_out, m_final_g, v_norm_g, v_ada_w, v_ada_b, v_sc_w_in, v_sc_conv_w, v_sc_w_out, v_lru_w_in, v_lru_conv_w, v_lru_conv_b, v_lru_w_a, v_lru_b_a, v_lru_w_x, v_lru_b_x, v_lru_lambda, v_lru_w_out, v_final_g):
    _, s, d = x.shape
    e = sc_w_out.shape[1] * N_DEV
    heads, dh_s, dh = lru_w_a.shape[1:]
    es = e // N_DEV
    f = ada_w.shape[2]
    mx, my, mc = _position()
    me = 4 * mx + 2 * my + mc
    chip = 2 * mx + my
    idx = jnp.stack([chip ^ 1, chip ^ 2, chip ^ 3, chip, mc]).astype(jnp.int32)

    x0 = x[0]
    target = loss_target[0]

    shards = [sc_w_in[0].astype(BF16), sc_w_out[0].astype(BF16), lru_w_in[0].astype(BF16),
              lru_w_a[0].reshape(heads * dh_s, dh).astype(BF16),
              lru_w_x[0].reshape(heads * dh_s, dh).astype(BF16), lru_w_out[0].astype(BF16)]
    wg_in0, wg_out0, wg_in1, wg_a, wg_x, wg_out1 = _weights_all_gather(shards, "weights_all_gather")
    w_out0 = wg_out0.reshape(e, d)
    w_out1 = wg_out1.reshape(e, d)
    w_a = wg_a.reshape(N_DEV, heads, dh_s, dh).transpose(1, 0, 2, 3).reshape(heads, dh, dh)
    w_x = wg_x.reshape(N_DEV, heads, dh_s, dh).transpose(1, 0, 2, 3).reshape(heads, dh, dh)

    small_shapes = [(d,), (3, es), (4, es), (es,), (heads, dh_s), (heads, dh_s), (es,)]
    small = _small_gather(_pack([c, sc_conv_w, lru_conv_w, lru_conv_b, lru_b_a, lru_b_x, lru_lambda]),
                          "gather_small_weights").reshape(N_DEV, -1)
    c_all, cw3, cw4, cb, ba, bx, lam = _unpack(small, small_shapes)
    cw3 = cw3.transpose(1, 0, 2).reshape(3, e)
    cw4 = cw4.transpose(1, 0, 2).reshape(4, e)
    cb = cb.reshape(1, e)
    lam = lam.reshape(1, e)
    ba = ba.transpose(1, 0, 2).reshape(1, e)
    bx = bx.transpose(1, 0, 2).reshape(1, e)

    ada_b_mine = _my_slice(ada_b, me, 1).reshape(2, 1, f)
    mod_mine = _ada_mod(c_all, ada_w, ada_b_mine, "ada_mod")
    mod_all = _small_gather(_pack([mod_mine]), "gather_mod").reshape(N_DEV, -1)
    mod_all = mod_all[:, :2 * N_DEV * f].reshape(N_DEV, 2, N_DEV, f)
    mod_all = mod_all.transpose(1, 2, 0, 3).reshape(2, N_DEV, 3 * d)
    mod = lax.dynamic_index_in_dim(mod_all, me, 1, keepdims=False)
    shift = [mod[l:l + 1, 0:d] for l in range(2)]
    scale = [mod[l:l + 1, d:2 * d] for l in range(2)]
    gate = [mod[l:l + 1, 2 * d:3 * d] for l in range(2)]
    ng = [norm_g[l:l + 1] for l in range(2)]
    fg = final_g.reshape(1, d)

    h0 = _norm_mod(x0, ng[0], scale[0], shift[0], "norm_mod_0")
    proj0 = _mm_proj(h0, wg_in0, 4, "mm_proj_0")
    yb0 = _sc_fwd(proj0, cw3, "sc_fwd")
    x1, y0 = _mm_out(yb0, w_out0, x0, gate[0], "mm_out_0")
    h1 = _norm_mod(x1, ng[1], scale[1], shift[1], "norm_mod_1")
    proj1 = _mm_proj(h1, wg_in1, 2, "mm_proj_1")
    yb1, hs = _lru_fwd(proj1, cw4, cb, w_a, ba, w_x, bx, lam, "lru_fwd")
    x2, y1 = _mm_out(yb1, w_out1, x1, gate[1], "mm_out_1")
    dx2, loss_part, d_fg = _final_loss(x2, fg, target, "final_loss")

    dy1, dgate1 = _gate_bwd(dx2, y1, gate[1], "gate_bwd_1")
    dyb1 = _mm_nt(dy1[None], w_out1[None], BF16, "mm_dyb_1")
    dw_out1 = _mm_tn(yb1, dy1[None], 1, "mm_dw_out_1")
    dproj1, dw_a, dw_x, vecs1 = _lru_bwd(proj1, hs, dyb1, cw4, cb, w_a, ba, w_x, bx, lam, "lru_bwd")
    dh1 = _mm_nt(dproj1, wg_in1, F32, "mm_dh_1")
    dw_in1 = _mm_tn(h1, dproj1, N_DEV, "mm_dw_in_1")
    dx1, dscale1, dshift1, dng1 = _norm_mod_bwd(x1, dh1, dx2, ng[1], scale[1], "norm_mod_bwd_1")
    dy0, dgate0 = _gate_bwd(dx1, y0, gate[0], "gate_bwd_0")
    dyb0 = _mm_nt(dy0[None], w_out0[None], BF16, "mm_dyb_0")
    dw_out0 = _mm_tn(yb0, dy0[None], 1, "mm_dw_out_0")
    dproj0, vecs0 = _sc_bwd(proj0, dyb0, cw3, "sc_bwd")
    dh0 = _mm_nt(dproj0, wg_in0, F32, "mm_dh_0")
    dw_in0 = _mm_tn(h0, dproj0, N_DEV, "mm_dw_in_0")
    dx0, dscale0, dshift0, dng0 = _norm_mod_bwd(x0, dh0, dx1, ng[0], scale[0], "norm_mod_bwd_0")

    def pieces(g, rows, cols):
        return g.reshape(4, 2, rows, cols)

    def by_rows(g):
        return g.reshape(heads, N_DEV, dh_s, dh).transpose(1, 0, 2, 3).reshape(N_DEV, heads * dh_s, dh)

    parts = [pieces(dw_in0, d, 4 * es), pieces(dw_out0, es, d), pieces(dw_in1, d, 2 * es),
             pieces(by_rows(dw_a), heads * dh_s, dh), pieces(by_rows(dw_x), heads * dh_s, dh),
             pieces(dw_out1, es, d)]
    gots = _pair_exchange(parts, "grads_pair_exchange")
    names = ["sc_w_in", "sc_w_out", "lru_w_in", "lru_w_a", "lru_w_x", "lru_w_out"]
    sums = [_pair_sum(idx, p, q, "pair_sum_" + nm) for p, q, nm in zip(parts, gots, names)]
    recvs = _chip_exchange(sums, "grads_chip_exchange")

    dmod_mine = jnp.concatenate([dshift0, dscale0, dgate0, dshift1, dscale1, dgate1], axis=1)
    end_shapes = [(LANES,), (2, 3 * d), (2, d), (d,), (8, e), (16, e)]
    end_all = _small_gather(
        _pack([loss_part, dmod_mine, jnp.concatenate([dng0, dng1], axis=0), d_fg, vecs0, vecs1]),
        "gather_small_grads")
    end_sum = _device_sum(end_all, "sum_small_grads").reshape(-1)
    loss_v, g_ada_b, g_norm_g, g_final_g, sum0, sum1 = _unpack(end_sum, end_shapes)
    loss = loss_v[0]
    dmod_all = _unpack(end_all.reshape(N_DEV, -1), end_shapes)[1]
    dmod_cols = _my_slice(dmod_all, me, 2).transpose(1, 0, 2)
    g_ada_w = _ada_grad(c_all.T, dmod_cols, "ada_grad")

    g_sc_conv_w = _my_slice(sum0[0:3], me, 1)
    g_lru_b_a = _my_slice(sum1[0].reshape(heads, dh), me, 1)
    g_lru_b_x = _my_slice(sum1[1].reshape(heads, dh), me, 1)
    g_lru_lambda = _my_slice(sum1[2:3], me, 1)
    g_lru_conv_b = _my_slice(sum1[3:4], me, 1)
    g_lru_conv_w = _my_slice(sum1[4:8], me, 1)

    big = [(sc_w_in, m_sc_w_in, v_sc_w_in), (sc_w_out, m_sc_w_out, v_sc_w_out),
           (lru_w_in, m_lru_w_in, v_lru_w_in), (lru_w_a, m_lru_w_a, v_lru_w_a),
           (lru_w_x, m_lru_w_x, v_lru_w_x), (lru_w_out, m_lru_w_out, v_lru_w_out)]
    big_out = []
    for (w, m, v), p, q, u, nm in zip(big, parts, gots, recvs, names):
        shp2 = p.shape[2:]
        res = _adamw_reduced(idx, w.reshape(shp2), m.reshape(shp2), v.reshape(shp2), p, q, u, "adamw_" + nm)
        big_out.append([r.reshape(w.shape) for r in res])
    ada_res = _adamw(ada_w.reshape(2 * d, f), g_ada_w.reshape(2 * d, f), m_ada_w.reshape(2 * d, f),
                     v_ada_w.reshape(2 * d, f), "adamw_ada_w")
    ada_out = [g_ada_w] + [r.reshape(ada_w.shape) for r in ada_res]

    small_w = [norm_g, ada_b, final_g, sc_conv_w, lru_conv_w, lru_conv_b, lru_b_a, lru_b_x, lru_lambda]
    small_m = [m_norm_g, m_ada_b, m_final_g, m_sc_conv_w, m_lru_conv_w, m_lru_conv_b, m_lru_b_a, m_lru_b_x,
               m_lru_lambda]
    small_v = [v_norm_g, v_ada_b, v_final_g, v_sc_conv_w, v_lru_conv_w, v_lru_conv_b, v_lru_b_a, v_lru_b_x,
               v_lru_lambda]
    small_g = [g_norm_g, g_ada_b, g_final_g, g_sc_conv_w, g_lru_conv_w, g_lru_conv_b, g_lru_b_a, g_lru_b_x,
               g_lru_lambda]
    small_g = [g.reshape(w.shape) for g, w in zip(small_g, small_w)]
    shapes = [w.shape for w in small_w]
    packed = _adamw(_pack(small_w), _pack(small_g), _pack(small_m), _pack(small_v), "adamw_small")
    small_out = [small_g] + [_unpack(p.reshape(-1), shapes) for p in packed]

    def small(kind, i):
        return small_out[kind][i]

    def bigw(kind, i):
        return big_out[i][kind]

    outs = [loss, dx0[None]]
    for kind in range(4):
        outs += [small(kind, 0), ada_out[kind], small(kind, 1), bigw(kind, 0), small(kind, 3), bigw(kind, 1),
                 bigw(kind, 2), small(kind, 4), small(kind, 5), bigw(kind, 3), small(kind, 6), bigw(kind, 4),
                 small(kind, 7), small(kind, 8), bigw(kind, 5), small(kind, 2)]
    return tuple(outs)
```

```python
import math

import jax
import jax.numpy as jnp
from jax import lax
from jax.experimental import pallas as pl
from jax.experimental.pallas import tpu as pltpu

N_DEV = 8
LANES = 128
EPS = 1e-6
RGLRU_C = 8.0
ADAM_LR = 0.001
ADAM_B1 = 0.9
ADAM_B2 = 0.999
ADAM_EPS = 1e-08
ADAM_WD = 0.01
ADAM_STEP = 10
VMEM_LIMIT = 56 * 1024 * 1024
MESH = pl.DeviceIdType.MESH
F32 = jnp.float32
BF16 = jnp.bfloat16
ANY = pl.BlockSpec(memory_space=pl.ANY)
HBM = pl.BlockSpec(memory_space=pltpu.HBM)
SEM = pl.BlockSpec(memory_space=pltpu.SEMAPHORE)
VMEM_SPEC = pl.BlockSpec(memory_space=pltpu.VMEM)
EFFECT = pltpu.SideEffectType.DATAFLOW_SIDE_EFFECTING
TOKEN = jax.ShapeDtypeStruct((8, LANES), jnp.float32)


def _tile(n, pref):
    t = min(n, pref)
    assert n % t == 0, (n, pref)
    return t


def _params(*sem):
    return pltpu.CompilerParams(dimension_semantics=sem, vmem_limit_bytes=VMEM_LIMIT)


def _position():
    return lax.axis_index("x"), lax.axis_index("y"), lax.axis_index("c")


def _flip(x, y, k):
    return (1 - x if k & 2 else x), (1 - y if k & 1 else y)


def _after(body, n_in, deps):
    if not deps:
        return body

    def wrapped(*refs):
        return body(*refs[:n_in], *refs[n_in + len(deps):])

    return wrapped


def _small_gather(v, name, deps=()):
    rows = v.shape[0]

    def body(v_ref, out_ref, send_sems, recv_sems):
        x, y, c = _position()
        me = 4 * x + 2 * y + c
        out_ref[me] = v_ref[...]
        copies = []
        for k in range(1, N_DEV):
            px, py = _flip(x, y, k >> 1)
            pc = 1 - c if k & 1 else c
            cp = pltpu.make_async_remote_copy(
                src_ref=v_ref, dst_ref=out_ref.at[me],
                send_sem=send_sems.at[k - 1], recv_sem=recv_sems.at[k - 1],
                device_id=(px, py, pc), device_id_type=MESH)
            cp.start()
            copies.append((cp, 4 * px + 2 * py + pc))
        for k, (cp, peer) in enumerate(copies):
            pltpu.make_async_remote_copy(
                src_ref=v_ref, dst_ref=out_ref.at[peer],
                send_sem=send_sems.at[k], recv_sem=recv_sems.at[k],
                device_id=(x, y, c), device_id_type=MESH).wait_recv()
        for cp, _ in copies:
            cp.wait_send()

    return pl.pallas_call(
        _after(body, 1, deps), name=name,
        out_shape=jax.ShapeDtypeStruct((N_DEV, rows, LANES), F32),
        in_specs=[VMEM_SPEC] + [ANY] * len(deps), out_specs=VMEM_SPEC,
        scratch_shapes=[pltpu.SemaphoreType.DMA((N_DEV - 1,)),
                        pltpu.SemaphoreType.DMA((N_DEV - 1,))],
        compiler_params=pltpu.CompilerParams(vmem_limit_bytes=VMEM_LIMIT),
    )(v, *deps)


def _hbm(a):
    return pltpu.with_memory_space_constraint(a, pltpu.HBM)


def _hbm_like(arrays):
    return [pltpu.HBM(a.shape, a.dtype) for a in arrays]


def _remote(src, dst, send, recv, to):
    return pltpu.make_async_remote_copy(src_ref=src, dst_ref=dst, send_sem=send, recv_sem=recv,
                                        device_id=to, device_id_type=MESH)


def _gather_start(shards, lands, groups, name):
    n, ng = len(shards), len(groups)

    def body(*refs):
        ins, lnd = refs[:n], refs[n:2 * n]
        sems = refs[2 * n:2 * n + 2 * ng]
        token = refs[-1]
        x, y, c = _position()
        me = 4 * x + 2 * y + c
        targets = [(x, y, 1 - c)] + [(px, py, c) for px, py in (_flip(x, y, k) for k in (1, 2, 3))]
        for g, members in enumerate(groups):
            for slot, i in enumerate(members):
                for k in (1, 2, 3, 0):
                    _remote(ins[i], lnd[i].at[me], sems[2 * g].at[4 * slot + k], sems[2 * g + 1].at[4 * slot + k],
                            targets[k]).start()
        token[...] = jnp.zeros_like(token)

    sem_shapes = []
    for members in groups:
        sem_shapes += [pltpu.SemaphoreType.DMA((4 * len(members),)), pltpu.SemaphoreType.DMA((4 * len(members),))]
    out = pl.pallas_call(
        body, name=name,
        out_shape=sem_shapes + _hbm_like(shards) + _hbm_like(lands) + [TOKEN],
        in_specs=[HBM] * (2 * n), out_specs=[SEM] * (2 * ng) + [HBM] * (2 * n) + [VMEM_SPEC],
        input_output_aliases={i: 2 * ng + i for i in range(2 * n)},
        compiler_params=pltpu.CompilerParams(has_side_effects=EFFECT),
    )(*[_hbm(s) for s in shards], *[_hbm(l) for l in lands])
    sems = [(out[2 * g], out[2 * g + 1]) for g in range(ng)]
    return sems, out[2 * ng:2 * ng + n], out[2 * ng + n:2 * ng + 2 * n], out[-1]


def _gather_forward(shards, lands, send, recv, after, name):
    m = len(shards)

    def body(*refs):
        ins, lnd = refs[:m], refs[m:2 * m]
        send_ref, recv_ref = refs[2 * m], refs[2 * m + 1]
        fsend, frecv = refs[2 * m + 2 + len(after)], refs[2 * m + 3 + len(after)]
        token = refs[-1]
        x, y, c = _position()
        me = (x, y, c)
        chips = [_flip(x, y, k) for k in (1, 2, 3)]
        for slot in range(m):
            for j, (px, py) in enumerate(chips):
                block = lnd[slot].at[4 * px + 2 * py + c]
                k = 4 * slot + 1 + j
                _remote(ins[slot], block, send_ref.at[k], recv_ref.at[k], me).wait_recv()
                _remote(block, block, fsend.at[3 * slot + j], frecv.at[3 * slot + j], (x, y, 1 - c)).start()
        for slot in range(m):
            theirs = lnd[slot].at[4 * x + 2 * y + (1 - c)]
            _remote(ins[slot], theirs, send_ref.at[4 * slot], recv_ref.at[4 * slot], me).wait_recv()
            for k in range(4 * slot, 4 * slot + 4):
                _remote(ins[slot], theirs, send_ref.at[k], recv_ref.at[k], me).wait_send()
        token[...] = jnp.zeros_like(token)

    out = pl.pallas_call(
        body, name=name,
        out_shape=[pltpu.SemaphoreType.DMA((3 * m,)), pltpu.SemaphoreType.DMA((3 * m,))]
        + _hbm_like(shards) + _hbm_like(lands) + [TOKEN],
        in_specs=[HBM] * (2 * m) + [SEM, SEM] + [ANY] * len(after),
        out_specs=[SEM, SEM] + [HBM] * (2 * m) + [VMEM_SPEC],
        input_output_aliases={i: 2 + i for i in range(2 * m)},
        compiler_params=pltpu.CompilerParams(has_side_effects=EFFECT),
    )(*shards, *lands, send, recv, *after)
    return out[0], out[1], out[2 + m:2 + 2 * m], out[-1]


def _gather_finish(lands, fsend, frecv, after, name):
    m = len(lands)

    def body(*refs):
        lnd = refs[:m]
        fsend_ref, frecv_ref = refs[m], refs[m + 1]
        x, y, c = _position()
        chips = [_flip(x, y, k) for k in (1, 2, 3)]
        for slot in range(m):
            for j, (px, py) in enumerate(chips):
                sent = lnd[slot].at[4 * px + 2 * py + c]
                came = lnd[slot].at[4 * px + 2 * py + (1 - c)]
                cp = _remote(sent, came, fsend_ref.at[3 * slot + j], frecv_ref.at[3 * slot + j], (x, y, c))
                cp.wait_recv()
                cp.wait_send()

    out = pl.pallas_call(
        body, name=name,
        out_shape=_hbm_like(lands),
        in_specs=[HBM] * m + [SEM, SEM] + [ANY] * len(after), out_specs=[HBM] * m,
        input_output_aliases={i: i for i in range(m)},
        compiler_params=pltpu.CompilerParams(has_side_effects=EFFECT),
    )(*lands, fsend, frecv, *after)
    return list(out)


def _pair_exchange(parts, name):
    n = len(parts)

    def body(*refs):
        ins, outs = refs[:n], refs[n:2 * n]
        send_sems, recv_sems = refs[2 * n:]
        x, y, c = _position()
        copies = []
        for i in range(n):
            cp = pltpu.make_async_remote_copy(
                src_ref=ins[i].at[:, pl.ds(1 - c, 1)], dst_ref=outs[i],
                send_sem=send_sems.at[i], recv_sem=recv_sems.at[i],
                device_id=(x, y, 1 - c), device_id_type=MESH)
            cp.start()
            copies.append(cp)
        for cp in copies:
            cp.wait_recv()
        for cp in copies:
            cp.wait_send()

    return pl.pallas_call(
        body, name=name,
        out_shape=[jax.ShapeDtypeStruct((4, 1) + p.shape[2:], p.dtype) for p in parts],
        in_specs=[ANY] * n, out_specs=[ANY] * n,
        scratch_shapes=[pltpu.SemaphoreType.DMA((n,)), pltpu.SemaphoreType.DMA((n,))],
    )(*parts)


def _chip_start(sums, lands, name):
    n = len(sums)

    def body(*refs):
        ins, lnd = refs[:n], refs[n:2 * n]
        send_ref, recv_ref = refs[2 * n], refs[2 * n + 1]
        token = refs[-1]
        x, y, c = _position()
        for i in range(n):
            for j in range(3):
                px, py = _flip(x, y, j + 1)
                _remote(ins[i].at[j], lnd[i].at[j], send_ref.at[3 * i + j], recv_ref.at[3 * i + j], (px, py, c)).start()
        token[...] = jnp.zeros_like(token)

    out = pl.pallas_call(
        body, name=name,
        out_shape=[pltpu.SemaphoreType.DMA((3 * n,)), pltpu.SemaphoreType.DMA((3 * n,))]
        + _hbm_like(sums) + _hbm_like(lands) + [TOKEN],
        in_specs=[HBM] * (2 * n), out_specs=[SEM, SEM] + [HBM] * (2 * n) + [VMEM_SPEC],
        input_output_aliases={i: 2 + i for i in range(2 * n)},
        compiler_params=pltpu.CompilerParams(has_side_effects=EFFECT),
    )(*[_hbm(s) for s in sums], *[_hbm(l) for l in lands])
    return out[0], out[1], out[2:2 + n], out[2 + n:2 + 2 * n], out[-1]


def _chip_wait(sums, lands, send, recv, after, name):
    n = len(sums)

    def body(*refs):
        ins, lnd = refs[:n], refs[n:2 * n]
        send_ref, recv_ref = refs[2 * n], refs[2 * n + 1]
        x, y, c = _position()
        for i in range(n):
            for j in range(3):
                cp = _remote(ins[i].at[j], lnd[i].at[j], send_ref.at[3 * i + j], recv_ref.at[3 * i + j], (x, y, c))
                cp.wait_recv()
                cp.wait_send()

    out = pl.pallas_call(
        body, name=name,
        out_shape=_hbm_like(sums) + _hbm_like(lands),
        in_specs=[HBM] * (2 * n) + [SEM, SEM] + [ANY] * len(after), out_specs=[HBM] * (2 * n),
        input_output_aliases={i: i for i in range(2 * n)},
        compiler_params=pltpu.CompilerParams(has_side_effects=EFFECT),
    )(*sums, *lands, send, recv, *after)
    return list(out[n:])


def _pair_sum(idx, part, got, name):
    _, _, rows, cols = part.shape
    tr = _tile(rows, 512)

    def body(idx_ref, p_ref, q_ref, o_ref):
        o_ref[...] = (p_ref[...].astype(F32) + q_ref[...].astype(F32)).astype(o_ref.dtype)

    grid_spec = pltpu.PrefetchScalarGridSpec(
        num_scalar_prefetch=1, grid=(3, rows // tr),
        in_specs=[pl.BlockSpec((None, None, tr, cols), lambda j, r, idx: (idx[j], idx[4], r, 0)),
                  pl.BlockSpec((None, None, tr, cols), lambda j, r, idx: (idx[j], 0, r, 0))],
        out_specs=pl.BlockSpec((None, tr, cols), lambda j, r, idx: (j, r, 0)))
    return pl.pallas_call(
        body, name=name, grid_spec=grid_spec,
        out_shape=jax.ShapeDtypeStruct((3, rows, cols), part.dtype),
        compiler_params=_params("arbitrary", "arbitrary"),
    )(idx, part, got)


def _mm_proj(h, wg, groups, name):
    s, k = h.shape
    nchunk, _, n = wg.shape
    e = nchunk * n // groups
    tn = _tile(min(n, e), 512)

    def body(h_ref, w_ref, o_ref):
        o_ref[...] = jnp.dot(h_ref[...], w_ref[...], preferred_element_type=F32).astype(o_ref.dtype)

    return pl.pallas_call(
        body, name=name, grid=(nchunk * n // tn,),
        in_specs=[pl.BlockSpec((s, k), lambda j: (0, 0)),
                  pl.BlockSpec((None, k, tn), lambda j: ((j * tn) // n, 0, ((j * tn) % n) // tn))],
        out_specs=pl.BlockSpec((None, s, tn), lambda j: ((j * tn) // e, 0, ((j * tn) % e) // tn)),
        out_shape=jax.ShapeDtypeStruct((groups, s, e), BF16),
        compiler_params=_params("arbitrary"),
    )(h, wg)


def _mm_out(yb, w, x, gate, name):
    s, k = yb.shape
    d = w.shape[1]
    tn = _tile(d, 512)
    tk = _tile(k, 1024)
    nk = k // tk

    def body(a_ref, w_ref, x_ref, g_ref, xo_ref, y_ref, acc_ref):
        kk = pl.program_id(1)

        @pl.when(kk == 0)
        def _():
            acc_ref[...] = jnp.zeros_like(acc_ref)

        acc_ref[...] += jnp.dot(a_ref[...], w_ref[...], preferred_element_type=F32)

        @pl.when(kk == nk - 1)
        def _():
            y = acc_ref[...]
            y_ref[...] = y.astype(y_ref.dtype)
            xo_ref[...] = x_ref[...] + g_ref[...] * y

    return pl.pallas_call(
        body, name=name, grid=(d // tn, nk),
        in_specs=[pl.BlockSpec((s, tk), lambda j, kk: (0, kk)),
                  pl.BlockSpec((tk, tn), lambda j, kk: (kk, j)),
                  pl.BlockSpec((s, tn), lambda j, kk: (0, j)),
                  pl.BlockSpec((1, tn), lambda j, kk: (0, j))],
        out_specs=[pl.BlockSpec((s, tn), lambda j, kk: (0, j)),
                   pl.BlockSpec((s, tn), lambda j, kk: (0, j))],
        out_shape=[jax.ShapeDtypeStruct((s, d), F32), jax.ShapeDtypeStruct((s, d), BF16)],
        scratch_shapes=[pltpu.VMEM((s, tn), F32)],
        compiler_params=_params("arbitrary", "arbitrary"),
    )(yb, w, x, gate)


def _mm_nt(a3, w3, out_dtype, name, deps=()):
    g, s, ea = a3.shape
    cw, n, nw = w3.shape
    total = g * ea
    assert total == cw * nw
    tk = _tile(min(ea, nw), 1024)
    tn = _tile(n, 1024)
    nk = total // tk

    def body(a_ref, w_ref, o_ref, acc_ref):
        kk = pl.program_id(1)

        @pl.when(kk == 0)
        def _():
            acc_ref[...] = jnp.zeros_like(acc_ref)

        acc_ref[...] += lax.dot_general(a_ref[...], w_ref[...], (((1,), (1,)), ((), ())),
                                        preferred_element_type=F32)

        @pl.when(kk == nk - 1)
        def _():
            o_ref[...] = acc_ref[...].astype(o_ref.dtype)

    return pl.pallas_call(
        _after(body, 2, deps), name=name, grid=(n // tn, nk),
        in_specs=[pl.BlockSpec((None, s, tk), lambda j, kk: ((kk * tk) // ea, 0, ((kk * tk) % ea) // tk)),
                  pl.BlockSpec((None, tn, tk), lambda j, kk: ((kk * tk) // nw, j, ((kk * tk) % nw) // tk))]
        + [ANY] * len(deps),
        out_specs=pl.BlockSpec((s, tn), lambda j, kk: (0, j)),
        out_shape=jax.ShapeDtypeStruct((s, n), out_dtype),
        scratch_shapes=[pltpu.VMEM((s, tn), F32)],
        compiler_params=_params("arbitrary", "arbitrary"),
    )(a3, w3, *deps)


def _mm_tn(a, b3, nchunk, name):
    s, ka = a.shape
    g, _, eb = b3.shape
    n = g * eb // nchunk
    tm = _tile(ka, 1024)
    tn = _tile(min(n, eb), 1024)

    def body(a_ref, b_ref, o_ref, at_ref):
        @pl.when(pl.program_id(1) == 0)
        def _():
            at_ref[...] = a_ref[...].astype(F32).T.astype(at_ref.dtype)

        o_ref[...] = jnp.dot(at_ref[...], b_ref[...], preferred_element_type=F32).astype(o_ref.dtype)

    return pl.pallas_call(
        body, name=name, grid=(ka // tm, g * eb // tn),
        in_specs=[pl.BlockSpec((s, tm), lambda i, j: (0, i)),
                  pl.BlockSpec((None, s, tn), lambda i, j: ((j * tn) // eb, 0, ((j * tn) % eb) // tn))],
        out_specs=pl.BlockSpec((None, tm, tn), lambda i, j: ((j * tn) // n, i, ((j * tn) % n) // tn)),
        out_shape=jax.ShapeDtypeStruct((nchunk, ka, n), BF16),
        scratch_shapes=[pltpu.VMEM((tm, s), BF16)],
        compiler_params=_params("arbitrary", "arbitrary"),
    )(a, b3)


def _sigmoid(z):
    return jax.nn.sigmoid(z)


def _shift_down(v, k, fill=0.0):
    if k == 0:
        return v
    row = lax.broadcasted_iota(jnp.int32, v.shape, 0)
    return jnp.where(row >= k, pltpu.roll(v, k, 0), fill)


def _shift_up(v, k, fill=0.0):
    if k == 0:
        return v
    s = v.shape[0]
    row = lax.broadcasted_iota(jnp.int32, v.shape, 0)
    return jnp.where(row < s - k, pltpu.roll(v, s - k, 0), fill)


def _scan(a, b, shift):
    s = a.shape[0]
    k = 1
    while k < s:
        b = a * shift(b, k, 0.0) + b
        if 2 * k < s:
            a = a * shift(a, k, 1.0)
        k *= 2
    return b


def _norm_mod(x, g, scale, shift, name, deps=()):
    s, d = x.shape
    ts = _tile(s, 256)

    def body(x_ref, g_ref, sc_ref, sh_ref, h_ref):
        xv = x_ref[...]
        rstd = lax.rsqrt(jnp.mean(xv * xv, axis=-1, keepdims=True) + EPS)
        nrm = xv * rstd * g_ref[...]
        h_ref[...] = (nrm * (1.0 + sc_ref[...]) + sh_ref[...]).astype(h_ref.dtype)

    vec = pl.BlockSpec((1, d), lambda i: (0, 0))
    return pl.pallas_call(
        _after(body, 4, deps), name=name, grid=(s // ts,),
        in_specs=[pl.BlockSpec((ts, d), lambda i: (i, 0)), vec, vec, vec] + [ANY] * len(deps),
        out_specs=pl.BlockSpec((ts, d), lambda i: (i, 0)),
        out_shape=jax.ShapeDtypeStruct((s, d), BF16),
        compiler_params=_params("arbitrary"),
    )(x, g, scale, shift, *deps)


def _norm_mod_bwd(x, dh, dx_res, g, scale, name):
    s, d = x.shape
    ts = _tile(s, 256)

    def body(x_ref, dh_ref, dr_ref, g_ref, sc_ref, dx_ref, dsc_ref, dsh_ref, dg_ref):
        @pl.when(pl.program_id(0) == 0)
        def _():
            dsc_ref[...] = jnp.zeros_like(dsc_ref)
            dsh_ref[...] = jnp.zeros_like(dsh_ref)
            dg_ref[...] = jnp.zeros_like(dg_ref)

        xv = x_ref[...]
        dh_v = dh_ref[...].astype(F32)
        gv = g_ref[...]
        rstd = lax.rsqrt(jnp.mean(xv * xv, axis=-1, keepdims=True) + EPS)
        xhat = xv * rstd
        dsc_ref[...] += jnp.sum(dh_v * xhat * gv, axis=0, keepdims=True)
        dsh_ref[...] += jnp.sum(dh_v, axis=0, keepdims=True)
        dn = dh_v * (1.0 + sc_ref[...])
        dg_ref[...] += jnp.sum(dn * xhat, axis=0, keepdims=True)
        dxhat = dn * gv
        proj = jnp.mean(dxhat * xhat, axis=-1, keepdims=True)
        dx_ref[...] = dr_ref[...] + rstd * (dxhat - xhat * proj)

    row = pl.BlockSpec((ts, d), lambda i: (i, 0))
    vec = pl.BlockSpec((1, d), lambda i: (0, 0))
    return pl.pallas_call(
        body, name=name, grid=(s // ts,),
        in_specs=[row, row, row, vec, vec],
        out_specs=[row, vec, vec, vec],
        out_shape=[jax.ShapeDtypeStruct((s, d), F32)] + [jax.ShapeDtypeStruct((1, d), F32)] * 3,
        compiler_params=_params("arbitrary"),
    )(x, dh, dx_res, g, scale)


def _final_loss(x, g, target, name):
    s, d = x.shape
    ts = _tile(s, 256)

    def body(x_ref, g_ref, t_ref, dx_ref, loss_ref, dg_ref):
        @pl.when(pl.program_id(0) == 0)
        def _():
            loss_ref[...] = jnp.zeros_like(loss_ref)
            dg_ref[...] = jnp.zeros_like(dg_ref)

        xv = x_ref[...]
        gv = g_ref[...]
        rstd = lax.rsqrt(jnp.mean(xv * xv, axis=-1, keepdims=True) + EPS)
        xhat = xv * rstd
        err = xhat * gv - t_ref[...]
        loss_ref[...] += 0.5 * jnp.sum(jnp.mean(err * err, axis=-1, keepdims=True))
        dy = err * (1.0 / d)
        dg_ref[...] += jnp.sum(dy * xhat, axis=0, keepdims=True)
        dxhat = dy * gv
        proj = jnp.mean(dxhat * xhat, axis=-1, keepdims=True)
        dx_ref[...] = rstd * (dxhat - xhat * proj)

    row = pl.BlockSpec((ts, d), lambda i: (i, 0))
    vec = pl.BlockSpec((1, d), lambda i: (0, 0))
    return pl.pallas_call(
        body, name=name, grid=(s // ts,),
        in_specs=[row, vec, row],
        out_specs=[row, pl.BlockSpec((1, LANES), lambda i: (0, 0)), vec],
        out_shape=[jax.ShapeDtypeStruct((s, d), F32), jax.ShapeDtypeStruct((1, LANES), F32),
                   jax.ShapeDtypeStruct((1, d), F32)],
        compiler_params=_params("arbitrary"),
    )(x, g, target)


def _gate_bwd(dx, y, gate, name):
    s, d = dx.shape
    ts = _tile(s, 256)

    def body(dx_ref, y_ref, g_ref, dy_ref, dgate_ref):
        @pl.when(pl.program_id(0) == 0)
        def _():
            dgate_ref[...] = jnp.zeros_like(dgate_ref)

        dxv = dx_ref[...]
        dy_ref[...] = (dxv * g_ref[...]).astype(dy_ref.dtype)
        dgate_ref[...] += jnp.sum(dxv * y_ref[...].astype(F32), axis=0, keepdims=True)

    row = pl.BlockSpec((ts, d), lambda i: (i, 0))
    vec = pl.BlockSpec((1, d), lambda i: (0, 0))
    return pl.pallas_call(
        body, name=name, grid=(s // ts,),
        in_specs=[row, row, vec], out_specs=[row, vec],
        out_shape=[jax.ShapeDtypeStruct((s, d), BF16), jax.ShapeDtypeStruct((1, d), F32)],
        compiler_params=_params("arbitrary"),
    )(dx, y, gate)


def _conv(v, w_ref, width):
    out = w_ref[width - 1:width, :] * v
    for k in range(width - 1):
        out = out + w_ref[k:k + 1, :] * _shift_down(v, width - 1 - k)
    return out


def _sc_fwd(proj, conv_w, name, deps=()):
    _, s, e = proj.shape
    te = _tile(e, 256)
    width = conv_w.shape[0]

    def body(b_ref, c_ref, v_ref, g_ref, w_ref, o_ref):
        cv = c_ref[...].astype(F32) * v_ref[...].astype(F32)
        u = _conv(cv, w_ref, width)
        gv = g_ref[...].astype(F32)
        o_ref[...] = (b_ref[...].astype(F32) * u * (gv * _sigmoid(gv))).astype(o_ref.dtype)

    def part(q):
        return pl.BlockSpec((None, s, te), lambda j, q=q: (q, 0, j))

    return pl.pallas_call(
        _after(body, 5, deps), name=name, grid=(e // te,),
        in_specs=[part(0), part(1), part(2), part(3), pl.BlockSpec((width, te), lambda j: (0, j))]
        + [ANY] * len(deps),
        out_specs=pl.BlockSpec((s, te), lambda j: (0, j)),
        out_shape=jax.ShapeDtypeStruct((s, e), BF16),
        compiler_params=_params("arbitrary"),
    )(proj, proj, proj, proj, conv_w, *deps)


def _sc_bwd(proj, dyb, conv_w, name):
    _, s, e = proj.shape
    te = _tile(e, 256)
    width = conv_w.shape[0]

    def body(b_ref, c_ref, v_ref, g_ref, dy_ref, w_ref, dp_ref, vec_ref):
        bv = b_ref[...].astype(F32)
        cvl = c_ref[...].astype(F32)
        vv = v_ref[...].astype(F32)
        gv = g_ref[...].astype(F32)
        dyv = dy_ref[...].astype(F32)
        cv = cvl * vv
        u = _conv(cv, w_ref, width)
        sg = _sigmoid(gv)
        silu = gv * sg
        dp_ref[0] = (dyv * u * silu).astype(dp_ref.dtype)
        du = dyv * bv * silu
        dp_ref[3] = (dyv * bv * u * (sg * (1.0 + gv * (1.0 - sg)))).astype(dp_ref.dtype)
        dcv = w_ref[width - 1:width, :] * du
        vec_ref[...] = jnp.zeros_like(vec_ref)
        vec_ref[width - 1:width, :] = jnp.sum(du * cv, axis=0, keepdims=True)
        for k in range(width - 1):
            sh = width - 1 - k
            dcv = dcv + w_ref[k:k + 1, :] * _shift_up(du, sh)
            vec_ref[k:k + 1, :] = jnp.sum(du * _shift_down(cv, sh), axis=0, keepdims=True)
        dp_ref[1] = (dcv * vv).astype(dp_ref.dtype)
        dp_ref[2] = (dcv * cvl).astype(dp_ref.dtype)

    def part(q):
        return pl.BlockSpec((None, s, te), lambda j, q=q: (q, 0, j))

    return pl.pallas_call(
        body, name=name, grid=(e // te,),
        in_specs=[part(0), part(1), part(2), part(3), pl.BlockSpec((s, te), lambda j: (0, j)),
                  pl.BlockSpec((width, te), lambda j: (0, j))],
        out_specs=[pl.BlockSpec((4, s, te), lambda j: (0, 0, j)),
                   pl.BlockSpec((8, te), lambda j: (0, j))],
        out_shape=[jax.ShapeDtypeStruct((4, s, e), BF16), jax.ShapeDtypeStruct((8, e), F32)],
        compiler_params=_params("arbitrary"),
    )(proj, proj, proj, proj, dyb, conv_w)


def _lru_gates(v_pre, w_ref, cb_ref, wa_ref, ba_ref, wx_ref, bx_ref, lam_ref, width):
    v = _conv(v_pre, w_ref, width) + cb_ref[...]
    vb = v.astype(BF16)
    r = _sigmoid(jnp.dot(vb, wa_ref[...], preferred_element_type=F32) + ba_ref[...])
    i = _sigmoid(jnp.dot(vb, wx_ref[...], preferred_element_type=F32) + bx_ref[...])
    nl = -lam_ref[...]
    sp = jnp.maximum(nl, 0.0) + jnp.log1p(jnp.exp(-jnp.abs(nl)))
    log_a = (-RGLRU_C) * r * sp
    a = jnp.exp(log_a)
    one_minus_a2 = jnp.tanh(-log_a) * (1.0 + a * a)
    mult = jnp.sqrt(one_minus_a2)
    return v, vb, r, i, sp, a, mult


def _lru_specs(s, dh, heads, width):
    head_col = lambda q: pl.BlockSpec((None, s, dh), lambda h, q=q: (q, 0, h))
    vec = pl.BlockSpec((1, dh), lambda h: (0, h))
    mat = pl.BlockSpec((None, dh, dh), lambda h: (h, 0, 0))
    weights = [pl.BlockSpec((width, dh), lambda h: (0, h)), vec, mat, vec, mat, vec, vec]
    return head_col, weights


def _lru_fwd(proj, conv_w, conv_b, w_a, b_a, w_x, b_x, lam, name, deps=()):
    _, s, e = proj.shape
    heads, dh, _ = w_a.shape
    width = conv_w.shape[0]

    def body(v_ref, g_ref, w_ref, cb_ref, wa_ref, ba_ref, wx_ref, bx_ref, lam_ref, yb_ref, hs_ref):
        v, _, _, i, _, a, mult = _lru_gates(v_ref[...].astype(F32), w_ref, cb_ref, wa_ref, ba_ref,
                                           wx_ref, bx_ref, lam_ref, width)
        hs = _scan(a, mult * i * v, _shift_down)
        hs_ref[...] = hs
        gv = g_ref[...].astype(F32)
        yb_ref[...] = (hs * (gv * _sigmoid(gv))).astype(yb_ref.dtype)

    head_col, weights = _lru_specs(s, dh, heads, width)
    out = pl.BlockSpec((s, dh), lambda h: (0, h))
    return pl.pallas_call(
        _after(body, 9, deps), name=name, grid=(heads,),
        in_specs=[head_col(0), head_col(1)] + weights + [ANY] * len(deps),
        out_specs=[out, out],
        out_shape=[jax.ShapeDtypeStruct((s, e), BF16), jax.ShapeDtypeStruct((s, e), F32)],
        compiler_params=_params("arbitrary"),
    )(proj, proj, conv_w, conv_b, w_a, b_a, w_x, b_x, lam, *deps)


def _lru_bwd(proj, hs, dyb, conv_w, conv_b, w_a, b_a, w_x, b_x, lam, name):
    _, s, e = proj.shape
    heads, dh, _ = w_a.shape
    width = conv_w.shape[0]

    def body(v_ref, g_ref, hs_ref, dy_ref, w_ref, cb_ref, wa_ref, ba_ref, wx_ref, bx_ref, lam_ref,
             dp_ref, dwa_ref, dwx_ref, vec_ref):
        v_pre = v_ref[...].astype(F32)
        v, vb, r, i, sp, a, mult = _lru_gates(v_pre, w_ref, cb_ref, wa_ref, ba_ref, wx_ref, bx_ref,
                                              lam_ref, width)
        hs = hs_ref[...]
        gv = g_ref[...].astype(F32)
        dyv = dy_ref[...].astype(F32)
        sg = _sigmoid(gv)
        dp_ref[1] = (dyv * hs * (sg * (1.0 + gv * (1.0 - sg)))).astype(dp_ref.dtype)
        dhs = dyv * (gv * sg)
        d_h = _scan(_shift_up(a, 1), dhs, _shift_up)
        da = d_h * _shift_down(hs, 1)
        iv = i * v
        dlog_a = da * a - (d_h * iv) * (a * a) / mult
        di = d_h * mult * v
        dv = d_h * mult * i
        dzr = dlog_a * (-RGLRU_C) * sp * r * (1.0 - r)
        dzi = di * i * (1.0 - i)
        dsp = jnp.sum(dlog_a * r, axis=0, keepdims=True) * (-RGLRU_C)
        vec_ref[...] = jnp.zeros_like(vec_ref)
        vec_ref[0:1, :] = jnp.sum(dzr, axis=0, keepdims=True)
        vec_ref[1:2, :] = jnp.sum(dzi, axis=0, keepdims=True)
        vec_ref[2:3, :] = -dsp * _sigmoid(-lam_ref[...])
        dzr_b = dzr.astype(BF16)
        dzi_b = dzi.astype(BF16)
        vt = vb.astype(F32).T.astype(BF16)
        dwa_ref[...] = jnp.dot(vt, dzr_b, preferred_element_type=F32).astype(dwa_ref.dtype)
        dwx_ref[...] = jnp.dot(vt, dzi_b, preferred_element_type=F32).astype(dwx_ref.dtype)
        nt = (((1,), (1,)), ((), ()))
        dv = dv + lax.dot_general(dzr_b, wa_ref[...], nt, preferred_element_type=F32)
        dv = dv + lax.dot_general(dzi_b, wx_ref[...], nt, preferred_element_type=F32)
        vec_ref[3:4, :] = jnp.sum(dv, axis=0, keepdims=True)
        dvp = w_ref[width - 1:width, :] * dv
        vec_ref[4 + width - 1:4 + width, :] = jnp.sum(dv * v_pre, axis=0, keepdims=True)
        for k in range(width - 1):
            sh = width - 1 - k
            dvp = dvp + w_ref[k:k + 1, :] * _shift_up(dv, sh)
            vec_ref[4 + k:5 + k, :] = jnp.sum(dv * _shift_down(v_pre, sh), axis=0, keepdims=True)
        dp_ref[0] = dvp.astype(dp_ref.dtype)

    head_col, weights = _lru_specs(s, dh, heads, width)
    col = pl.BlockSpec((s, dh), lambda h: (0, h))
    mat = pl.BlockSpec((None, dh, dh), lambda h: (h, 0, 0))
    return pl.pallas_call(
        body, name=name, grid=(heads,),
        in_specs=[head_col(0), head_col(1), col, col] + weights,
        out_specs=[pl.BlockSpec((2, s, dh), lambda h: (0, 0, h)), mat, mat,
                   pl.BlockSpec((16, dh), lambda h: (0, h))],
        out_shape=[jax.ShapeDtypeStruct((2, s, e), BF16),
                   jax.ShapeDtypeStruct((heads, dh, dh), BF16),
                   jax.ShapeDtypeStruct((heads, dh, dh), BF16),
                   jax.ShapeDtypeStruct((16, e), F32)],
        compiler_params=_params("arbitrary"),
    )(proj, proj, hs, dyb, conv_w, conv_b, w_a, b_a, w_x, b_x, lam)


def _ada_mod(c_all, w, b, name):
    layers, d, f = w.shape
    nb = c_all.shape[0]

    def body(c_ref, w_ref, b_ref, o_ref):
        cv = c_ref[...]
        sc = cv * _sigmoid(cv)
        o_ref[...] = jnp.dot(sc, w_ref[...], preferred_element_type=F32,
                             precision=lax.Precision.HIGHEST) + b_ref[...]

    return pl.pallas_call(
        body, name=name, grid=(layers,),
        in_specs=[pl.BlockSpec((nb, d), lambda l: (0, 0)),
                  pl.BlockSpec((None, d, f), lambda l: (l, 0, 0)),
                  pl.BlockSpec((None, 1, f), lambda l: (l, 0, 0))],
        out_specs=pl.BlockSpec((None, nb, f), lambda l: (l, 0, 0)),
        out_shape=jax.ShapeDtypeStruct((layers, nb, f), F32),
        compiler_params=_params("arbitrary"),
    )(c_all, w, b)


def _ada_grad(c_all_t, dmod, name):
    d, nb = c_all_t.shape
    layers, _, f = dmod.shape

    def body(c_ref, dm_ref, o_ref):
        cv = c_ref[...]
        sc = cv * _sigmoid(cv)
        acc = sc[:, 0:1] * dm_ref[0:1, :]
        for k in range(1, nb):
            acc = acc + sc[:, k:k + 1] * dm_ref[k:k + 1, :]
        o_ref[...] = acc

    return pl.pallas_call(
        body, name=name, grid=(layers,),
        in_specs=[pl.BlockSpec((d, nb), lambda l: (0, 0)),
                  pl.BlockSpec((None, nb, f), lambda l: (l, 0, 0))],
        out_specs=pl.BlockSpec((None, d, f), lambda l: (l, 0, 0)),
        out_shape=jax.ShapeDtypeStruct((layers, d, f), F32),
        compiler_params=_params("arbitrary"),
    )(c_all_t, dmod)


def _device_sum(g, name):
    _, rows, _ = g.shape

    def body(g_ref, o_ref):
        acc = g_ref[0]
        for k in range(1, N_DEV):
            acc = acc + g_ref[k]
        o_ref[...] = acc

    return pl.pallas_call(
        body, name=name,
        in_specs=[VMEM_SPEC], out_specs=VMEM_SPEC,
        out_shape=jax.ShapeDtypeStruct((rows, LANES), F32),
        compiler_params=pltpu.CompilerParams(vmem_limit_bytes=VMEM_LIMIT),
    )(g)


def _adamw_math(w, g, m, v):
    m = ADAM_B1 * m + (1.0 - ADAM_B1) * g
    v = ADAM_B2 * v + (1.0 - ADAM_B2) * (g * g)
    m_hat = m / (1.0 - ADAM_B1 ** ADAM_STEP)
    v_hat = v / (1.0 - ADAM_B2 ** ADAM_STEP)
    delta = -ADAM_LR * (m_hat / (jnp.sqrt(v_hat) + ADAM_EPS) + ADAM_WD * w)
    return delta, m, v


def _adamw(w, g, m, v, name):
    rows, cols = w.shape
    tr = _tile(rows, 256)

    def body(w_ref, g_ref, m_ref, v_ref, d_ref, mo_ref, vo_ref):
        d_ref[...], mo_ref[...], vo_ref[...] = _adamw_math(w_ref[...], g_ref[...], m_ref[...], v_ref[...])

    blk = pl.BlockSpec((tr, cols), lambda i: (i, 0))
    return pl.pallas_call(
        body, name=name, grid=(rows // tr,),
        in_specs=[blk] * 4, out_specs=[blk] * 3,
        out_shape=[jax.ShapeDtypeStruct((rows, cols), F32)] * 3,
        compiler_params=_params("arbitrary"),
    )(w, g, m, v)


def _adamw_reduced(idx, w, m, v, part, got, recv, name):
    rows, cols = w.shape
    tr = _tile(rows, 256)

    def body(idx_ref, w_ref, m_ref, v_ref, p_ref, q_ref, u_ref, g_ref, d_ref, mo_ref, vo_ref):
        g = p_ref[...].astype(F32) + q_ref[...].astype(F32)
        for j in range(3):
            g = g + u_ref[j].astype(F32)
        g_ref[...] = g
        d_ref[...], mo_ref[...], vo_ref[...] = _adamw_math(w_ref[...], g, m_ref[...], v_ref[...])

    blk = pl.BlockSpec((tr, cols), lambda i, idx: (i, 0))
    grid_spec = pltpu.PrefetchScalarGridSpec(
        num_scalar_prefetch=1, grid=(rows // tr,),
        in_specs=[blk, blk, blk,
                  pl.BlockSpec((None, None, tr, cols), lambda i, idx: (idx[3], idx[4], i, 0)),
                  pl.BlockSpec((None, None, tr, cols), lambda i, idx: (idx[3], 0, i, 0)),
                  pl.BlockSpec((3, tr, cols), lambda i, idx: (0, i, 0))],
        out_specs=[blk] * 4)
    return pl.pallas_call(
        body, name=name, grid_spec=grid_spec,
        out_shape=[jax.ShapeDtypeStruct((rows, cols), F32)] * 4,
        compiler_params=_params("arbitrary"),
    )(idx, w, m, v, part, got, recv)


def _pack(vectors):
    flat = jnp.concatenate([v.reshape(-1).astype(F32) for v in vectors])
    pad = (-flat.shape[0]) % (8 * LANES)
    return jnp.pad(flat, (0, pad)).reshape(-1, LANES)


def _unpack(flat, shapes):
    out, off = [], 0
    for shp in shapes:
        size = math.prod(shp)
        out.append(flat[..., off:off + size].reshape(flat.shape[:-1] + tuple(shp)))
        off += size
    return out


def _my_slice(full, me, axis):
    size = full.shape[axis] // N_DEV
    return lax.dynamic_slice_in_dim(full, me * size, size, axis)


def kernel(x, c, norm_g, ada_w, ada_b, sc_w_in, sc_conv_w, sc_w_out, lru_w_in, lru_conv_w, lru_conv_b, lru_w_a, lru_b_a, lru_w_x, lru_b_x, lru_lambda, lru_w_out, final_g, loss_target, m_norm_g, m_ada_w, m_ada_b, m_sc_w_in, m_sc_conv_w, m_sc_w_out, m_lru_w_in, m_lru_conv_w, m_lru_conv_b, m_lru_w_a, m_lru_b_a, m_lru_w_x, m_lru_b_x, m_lru_lambda, m_lru_w_out, m_final_g, v_norm_g, v_ada_w, v_ada_b, v_sc_w_in, v_sc_conv_w, v_sc_w_out, v_lru_w_in, v_lru_conv_w, v_lru_conv_b, v_lru_w_a, v_lru_b_a, v_lru_w_x, v_lru_b_x, v_lru_lambda, v_lru_w_out, v_final_g):
    _, s, d = x.shape
    e = sc_w_out.shape[1] * N_DEV
    heads, dh_s, dh = lru_w_a.shape[1:]
    es = e // N_DEV
    f = ada_w.shape[2]
    mx, my, mc = _position()
    me = 4 * mx + 2 * my + mc
    chip = 2 * mx + my
    idx = jnp.stack([chip ^ 1, chip ^ 2, chip ^ 3, chip, mc]).astype(jnp.int32)

    x0 = x[0]
    target = loss_target[0]

    shards = [sc_w_in[0].astype(BF16), sc_w_out[0].astype(BF16), lru_w_in[0].astype(BF16),
              lru_w_a[0].reshape(heads * dh_s, dh).astype(BF16),
              lru_w_x[0].reshape(heads * dh_s, dh).astype(BF16), lru_w_out[0].astype(BF16)]
    lands = [lax.dynamic_update_slice(lax.empty((N_DEV,) + sh.shape, BF16), sh[None], (me, 0, 0))
             for sh in shards]
    groups = [[0], [1], [2, 3, 4], [5]]
    sems, shards, lands, started = _gather_start(shards, lands, groups, "gather_start")

    def gathered(g, after_forward, name):
        members = groups[g]
        fsend, frecv, lnd, token = _gather_forward(
            [shards[i] for i in members], [lands[i] for i in members], sems[g][0], sems[g][1],
            after_forward, "gather_forward_" + name)
        return token, lambda after: _gather_finish(lnd, fsend, frecv, after, "gather_finish_" + name)

    small_shapes = [(d,), (3, es), (4, es), (es,), (heads, dh_s), (heads, dh_s), (es,)]
    small = _small_gather(_pack([c, sc_conv_w, lru_conv_w, lru_conv_b, lru_b_a, lru_b_x, lru_lambda]),
                          "gather_small_weights", deps=[started]).reshape(N_DEV, -1)
    c_all, cw3, cw4, cb, ba, bx, lam = _unpack(small, small_shapes)
    cw3 = cw3.transpose(1, 0, 2).reshape(3, e)
    cw4 = cw4.transpose(1, 0, 2).reshape(4, e)
    cb = cb.reshape(1, e)
    lam = lam.reshape(1, e)
    ba = ba.transpose(1, 0, 2).reshape(1, e)
    bx = bx.transpose(1, 0, 2).reshape(1, e)

    ada_b_mine = _my_slice(ada_b, me, 1).reshape(2, 1, f)
    mod_mine = _ada_mod(c_all, ada_w, ada_b_mine, "ada_mod")
    mod_all = _small_gather(_pack([mod_mine]), "gather_mod").reshape(N_DEV, -1)
    mod_all = mod_all[:, :2 * N_DEV * f].reshape(N_DEV, 2, N_DEV, f)
    mod_all = mod_all.transpose(1, 2, 0, 3).reshape(2, N_DEV, 3 * d)
    mod = lax.dynamic_index_in_dim(mod_all, me, 1, keepdims=False)
    shift = [mod[l:l + 1, 0:d] for l in range(2)]
    scale = [mod[l:l + 1, d:2 * d] for l in range(2)]
    gate = [mod[l:l + 1, 2 * d:3 * d] for l in range(2)]
    ng = [norm_g[l:l + 1] for l in range(2)]
    fg = final_g.reshape(1, d)

    h0 = _norm_mod(x0, ng[0], scale[0], shift[0], "norm_mod_0")
    tok, finish = gathered(0, [h0], "sc_w_in")
    wg_in0, = finish([tok])
    proj0 = _mm_proj(h0, wg_in0, 4, "mm_proj_0")
    tok, finish = gathered(1, [proj0], "sc_w_out")
    yb0 = _sc_fwd(proj0, cw3, "sc_fwd", deps=[tok])
    w_out0 = finish([yb0])[0].reshape(e, d)
    x1, y0 = _mm_out(yb0, w_out0, x0, gate[0], "mm_out_0")
    tok, finish = gathered(2, [x1], "lru_in")
    h1 = _norm_mod(x1, ng[1], scale[1], shift[1], "norm_mod_1", deps=[tok])
    wg_in1, wg_a, wg_x = finish([h1])
    w_a = wg_a.reshape(N_DEV, heads, dh_s, dh).transpose(1, 0, 2, 3).reshape(heads, dh, dh)
    w_x = wg_x.reshape(N_DEV, heads, dh_s, dh).transpose(1, 0, 2, 3).reshape(heads, dh, dh)
    proj1 = _mm_proj(h1, wg_in1, 2, "mm_proj_1")
    tok, finish = gathered(3, [proj1], "lru_w_out")
    yb1, hs = _lru_fwd(proj1, cw4, cb, w_a, ba, w_x, bx, lam, "lru_fwd", deps=[tok])
    w_out1 = finish([yb1])[0].reshape(e, d)
    x2, y1 = _mm_out(yb1, w_out1, x1, gate[1], "mm_out_1")
    dx2, loss_part, d_fg = _final_loss(x2, fg, target, "final_loss")

    def pieces(g, rows, cols):
        return g.reshape(4, 2, rows, cols)

    def by_rows(g):
        return g.reshape(heads, N_DEV, dh_s, dh).transpose(1, 0, 2, 3).reshape(N_DEV, heads * dh_s, dh)

    def scatter_start(parts, names, group):
        gots = _pair_exchange(parts, "pair_exchange_" + group)
        sums = [_pair_sum(idx, p, q, "pair_sum_" + nm) for p, q, nm in zip(parts, gots, names)]
        empties = [lax.empty(sm.shape, sm.dtype) for sm in sums]
        send, recv, sums, lnd, token = _chip_start(sums, empties, "chip_start_" + group)
        return dict(parts=parts, gots=gots, names=names, group=group, sums=sums, lands=lnd,
                    send=send, recv=recv), token

    dy1, dgate1 = _gate_bwd(dx2, y1, gate[1], "gate_bwd_1")
    dw_out1 = _mm_tn(yb1, dy1[None], 1, "mm_dw_out_1")
    rs1, tok = scatter_start([pieces(dw_out1, es, d)], ["lru_w_out"], "lru_w_out")
    dyb1 = _mm_nt(dy1[None], w_out1[None], BF16, "mm_dyb_1", deps=[tok])
    dproj1, dw_a, dw_x, vecs1 = _lru_bwd(proj1, hs, dyb1, cw4, cb, w_a, ba, w_x, bx, lam, "lru_bwd")
    dw_in1 = _mm_tn(h1, dproj1, N_DEV, "mm_dw_in_1")
    rs2, tok = scatter_start([pieces(dw_in1, d, 2 * es), pieces(by_rows(dw_a), heads * dh_s, dh),
                              pieces(by_rows(dw_x), heads * dh_s, dh)],
                             ["lru_w_in", "lru_w_a", "lru_w_x"], "lru_in")
    dh1 = _mm_nt(dproj1, wg_in1, F32, "mm_dh_1", deps=[tok])
    dx1, dscale1, dshift1, dng1 = _norm_mod_bwd(x1, dh1, dx2, ng[1], scale[1], "norm_mod_bwd_1")
    dy0, dgate0 = _gate_bwd(dx1, y0, gate[0], "gate_bwd_0")
    dw_out0 = _mm_tn(yb0, dy0[None], 1, "mm_dw_out_0")
    rs3, tok = scatter_start([pieces(dw_out0, es, d)], ["sc_w_out"], "sc_w_out")
    dyb0 = _mm_nt(dy0[None], w_out0[None], BF16, "mm_dyb_0", deps=[tok])
    dproj0, vecs0 = _sc_bwd(proj0, dyb0, cw3, "sc_bwd")
    dw_in0 = _mm_tn(h0, dproj0, N_DEV, "mm_dw_in_0")
    rs4, tok = scatter_start([pieces(dw_in0, d, 4 * es)], ["sc_w_in"], "sc_w_in")
    dh0 = _mm_nt(dproj0, wg_in0, F32, "mm_dh_0", deps=[tok])
    dx0, dscale0, dshift0, dng0 = _norm_mod_bwd(x0, dh0, dx1, ng[0], scale[0], "norm_mod_bwd_0")

    dmod_mine = jnp.concatenate([dshift0, dscale0, dgate0, dshift1, dscale1, dgate1], axis=1)
    end_shapes = [(LANES,), (2, 3 * d), (2, d), (d,), (8, e), (16, e)]
    end_all = _small_gather(
        _pack([loss_part, dmod_mine, jnp.concatenate([dng0, dng1], axis=0), d_fg, vecs0, vecs1]),
        "gather_small_grads")
    end_sum = _device_sum(end_all, "sum_small_grads").reshape(-1)
    loss_v, g_ada_b, g_norm_g, g_final_g, sum0, sum1 = _unpack(end_sum, end_shapes)
    loss = loss_v[0]
    dmod_all = _unpack(end_all.reshape(N_DEV, -1), end_shapes)[1]
    dmod_cols = _my_slice(dmod_all, me, 2).transpose(1, 0, 2)
    g_ada_w = _ada_grad(c_all.T, dmod_cols, "ada_grad")

    g_sc_conv_w = _my_slice(sum0[0:3], me, 1)
    g_lru_b_a = _my_slice(sum1[0].reshape(heads, dh), me, 1)
    g_lru_b_x = _my_slice(sum1[1].reshape(heads, dh), me, 1)
    g_lru_lambda = _my_slice(sum1[2:3], me, 1)
    g_lru_conv_b = _my_slice(sum1[3:4], me, 1)
    g_lru_conv_w = _my_slice(sum1[4:8], me, 1)

    ada_res = _adamw(ada_w.reshape(2 * d, f), g_ada_w.reshape(2 * d, f), m_ada_w.reshape(2 * d, f),
                     v_ada_w.reshape(2 * d, f), "adamw_ada_w")
    ada_out = [g_ada_w] + [r.reshape(ada_w.shape) for r in ada_res]

    small_w = [norm_g, ada_b, final_g, sc_conv_w, lru_conv_w, lru_conv_b, lru_b_a, lru_b_x, lru_lambda]
    small_m = [m_norm_g, m_ada_b, m_final_g, m_sc_conv_w, m_lru_conv_w, m_lru_conv_b, m_lru_b_a, m_lru_b_x,
               m_lru_lambda]
    small_v = [v_norm_g, v_ada_b, v_final_g, v_sc_conv_w, v_lru_conv_w, v_lru_conv_b, v_lru_b_a, v_lru_b_x,
               v_lru_lambda]
    small_g = [g_norm_g, g_ada_b, g_final_g, g_sc_conv_w, g_lru_conv_w, g_lru_conv_b, g_lru_b_a, g_lru_b_x,
               g_lru_lambda]
    small_g = [g.reshape(w.shape) for g, w in zip(small_g, small_w)]
    shapes = [w.shape for w in small_w]
    packed = _adamw(_pack(small_w), _pack(small_g), _pack(small_m), _pack(small_v), "adamw_small")
    small_out = [small_g] + [_unpack(p.reshape(-1), shapes) for p in packed]

    big = {"sc_w_in": (sc_w_in, m_sc_w_in, v_sc_w_in), "sc_w_out": (sc_w_out, m_sc_w_out, v_sc_w_out),
           "lru_w_in": (lru_w_in, m_lru_w_in, v_lru_w_in), "lru_w_a": (lru_w_a, m_lru_w_a, v_lru_w_a),
           "lru_w_x": (lru_w_x, m_lru_w_x, v_lru_w_x), "lru_w_out": (lru_w_out, m_lru_w_out, v_lru_w_out)}
    big_res = {}
    after = [packed[0], ada_res[0]]
    for rs in (rs1, rs2, rs3, rs4):
        recvs = _chip_wait(rs["sums"], rs["lands"], rs["send"], rs["recv"], after, "chip_wait_" + rs["group"])
        after = []
        for p, q, u, nm in zip(rs["parts"], rs["gots"], recvs, rs["names"]):
            w, m, v = big[nm]
            shp2 = p.shape[2:]
            res = _adamw_reduced(idx, w.reshape(shp2), m.reshape(shp2), v.reshape(shp2), p, q, u, "adamw_" + nm)
            big_res[nm] = [r.reshape(w.shape) for r in res]
            after.append(res[1])
    big_out = [big_res[nm] for nm in ("sc_w_in", "sc_w_out", "lru_w_in", "lru_w_a", "lru_w_x", "lru_w_out")]

    def small(kind, i):
        return small_out[kind][i]

    def bigw(kind, i):
        return big_out[i][kind]

    outs = [loss, dx0[None]]
    for kind in range(4):
        outs += [small(kind, 0), ada_out[kind], small(kind, 1), bigw(kind, 0), small(kind, 3), bigw(kind, 1),
                 bigw(kind, 2), small(kind, 4), small(kind, 5), bigw(kind, 3), small(kind, 6), bigw(kind, 4),
                 small(kind, 7), small(kind, 8), bigw(kind, 5), small(kind, 2)]
    return tuple(outs)
```

```python
import math

import jax
import jax.numpy as jnp
from jax import lax
from jax.experimental import pallas as pl
from jax.experimental.pallas import tpu as pltpu

N_DEV = 8
LANES = 128
EPS = 1e-6
RGLRU_C = 8.0
ADAM_LR = 0.001
ADAM_B1 = 0.9
ADAM_B2 = 0.999
ADAM_EPS = 1e-08
ADAM_WD = 0.01
ADAM_STEP = 10
VMEM_LIMIT = 56 * 1024 * 1024
MESH = pl.DeviceIdType.MESH
F32 = jnp.float32
BF16 = jnp.bfloat16
ANY = pl.BlockSpec(memory_space=pl.ANY)
HBM = pl.BlockSpec(memory_space=pltpu.HBM)
SEM = pl.BlockSpec(memory_space=pltpu.SEMAPHORE)
VMEM_SPEC = pl.BlockSpec(memory_space=pltpu.VMEM)
EFFECT = pltpu.SideEffectType.DATAFLOW_SIDE_EFFECTING
TOKEN = jax.ShapeDtypeStruct((8, LANES), jnp.float32)


def _tile(n, pref):
    t = min(n, pref)
    assert n % t == 0, (n, pref)
    return t


def _params(*sem):
    return pltpu.CompilerParams(dimension_semantics=sem, vmem_limit_bytes=VMEM_LIMIT)


def _position():
    return lax.axis_index("x"), lax.axis_index("y"), lax.axis_index("c")


def _flip(x, y, k):
    return (1 - x if k & 2 else x), (1 - y if k & 1 else y)


def _after(body, n_in, deps):
    if not deps:
        return body

    def wrapped(*refs):
        return body(*refs[:n_in], *refs[n_in + len(deps):])

    return wrapped


def _small_gather(v, name, deps=()):
    rows = v.shape[0]

    def body(v_ref, out_ref, send_sems, recv_sems):
        x, y, c = _position()
        me = 4 * x + 2 * y + c
        out_ref[me] = v_ref[...]
        copies = []
        for k in range(1, N_DEV):
            px, py = _flip(x, y, k >> 1)
            pc = 1 - c if k & 1 else c
            cp = pltpu.make_async_remote_copy(
                src_ref=v_ref, dst_ref=out_ref.at[me],
                send_sem=send_sems.at[k - 1], recv_sem=recv_sems.at[k - 1],
                device_id=(px, py, pc), device_id_type=MESH)
            cp.start()
            copies.append((cp, 4 * px + 2 * py + pc))
        for k, (cp, peer) in enumerate(copies):
            pltpu.make_async_remote_copy(
                src_ref=v_ref, dst_ref=out_ref.at[peer],
                send_sem=send_sems.at[k], recv_sem=recv_sems.at[k],
                device_id=(x, y, c), device_id_type=MESH).wait_recv()
        for cp, _ in copies:
            cp.wait_send()

    return pl.pallas_call(
        _after(body, 1, deps), name=name,
        out_shape=jax.ShapeDtypeStruct((N_DEV, rows, LANES), F32),
        in_specs=[VMEM_SPEC] + [ANY] * len(deps), out_specs=VMEM_SPEC,
        scratch_shapes=[pltpu.SemaphoreType.DMA((N_DEV - 1,)),
                        pltpu.SemaphoreType.DMA((N_DEV - 1,))],
        compiler_params=pltpu.CompilerParams(vmem_limit_bytes=VMEM_LIMIT),
    )(v, *deps)


def _hbm(a):
    return pltpu.with_memory_space_constraint(a, pltpu.HBM)


def _hbm_like(arrays):
    return [pltpu.HBM(a.shape, a.dtype) for a in arrays]


def _remote(src, dst, send, recv, to):
    return pltpu.make_async_remote_copy(src_ref=src, dst_ref=dst, send_sem=send, recv_sem=recv,
                                        device_id=to, device_id_type=MESH)


def _gather_start(shards, lands, groups, after, name):
    n, ng = len(shards), len(groups)

    def body(*refs):
        ins, lnd = refs[:n], refs[n:2 * n]
        sems = refs[2 * n + len(after):2 * n + len(after) + 2 * ng]
        token = refs[-1]
        x, y, c = _position()
        me = 4 * x + 2 * y + c
        targets = [(x, y, 1 - c)] + [(px, py, c) for px, py in (_flip(x, y, k) for k in (1, 2, 3))]
        for g, members in enumerate(groups):
            for slot, i in enumerate(members):
                for k in (1, 2, 3, 0):
                    _remote(ins[i], lnd[i].at[me], sems[2 * g].at[4 * slot + k], sems[2 * g + 1].at[4 * slot + k],
                            targets[k]).start()
        token[...] = jnp.zeros_like(token)

    sem_shapes = []
    for members in groups:
        sem_shapes += [pltpu.SemaphoreType.DMA((4 * len(members),)), pltpu.SemaphoreType.DMA((4 * len(members),))]
    out = pl.pallas_call(
        body, name=name,
        out_shape=sem_shapes + _hbm_like(shards) + _hbm_like(lands) + [TOKEN],
        in_specs=[HBM] * (2 * n) + [ANY] * len(after),
        out_specs=[SEM] * (2 * ng) + [HBM] * (2 * n) + [VMEM_SPEC],
        input_output_aliases={i: 2 * ng + i for i in range(2 * n)},
        compiler_params=pltpu.CompilerParams(has_side_effects=EFFECT),
    )(*[_hbm(s) for s in shards], *[_hbm(l) for l in lands], *after)
    sems = [(out[2 * g], out[2 * g + 1]) for g in range(ng)]
    return sems, out[2 * ng:2 * ng + n], out[2 * ng + n:2 * ng + 2 * n], out[-1]


def _gather_forward(shards, lands, send, recv, after, name):
    m = len(shards)

    def body(*refs):
        ins, lnd = refs[:m], refs[m:2 * m]
        send_ref, recv_ref = refs[2 * m], refs[2 * m + 1]
        fsend, frecv = refs[2 * m + 2 + len(after)], refs[2 * m + 3 + len(after)]
        token = refs[-1]
        x, y, c = _position()
        me = (x, y, c)
        chips = [_flip(x, y, k) for k in (1, 2, 3)]
        for slot in range(m):
            for j, (px, py) in enumerate(chips):
                block = lnd[slot].at[4 * px + 2 * py + c]
                k = 4 * slot + 1 + j
                _remote(ins[slot], block, send_ref.at[k], recv_ref.at[k], me).wait_recv()
                _remote(block, block, fsend.at[3 * slot + j], frecv.at[3 * slot + j], (x, y, 1 - c)).start()
        for slot in range(m):
            theirs = lnd[slot].at[4 * x + 2 * y + (1 - c)]
            _remote(ins[slot], theirs, send_ref.at[4 * slot], recv_ref.at[4 * slot], me).wait_recv()
            for k in range(4 * slot, 4 * slot + 4):
                _remote(ins[slot], theirs, send_ref.at[k], recv_ref.at[k], me).wait_send()
        token[...] = jnp.zeros_like(token)

    out = pl.pallas_call(
        body, name=name,
        out_shape=[pltpu.SemaphoreType.DMA((3 * m,)), pltpu.SemaphoreType.DMA((3 * m,))]
        + _hbm_like(shards) + _hbm_like(lands) + [TOKEN],
        in_specs=[HBM] * (2 * m) + [SEM, SEM] + [ANY] * len(after),
        out_specs=[SEM, SEM] + [HBM] * (2 * m) + [VMEM_SPEC],
        input_output_aliases={i: 2 + i for i in range(2 * m)},
        compiler_params=pltpu.CompilerParams(has_side_effects=EFFECT),
    )(*shards, *lands, send, recv, *after)
    return out[0], out[1], out[2 + m:2 + 2 * m], out[-1]


def _gather_finish(lands, fsend, frecv, after, name):
    m = len(lands)

    def body(*refs):
        lnd = refs[:m]
        fsend_ref, frecv_ref = refs[m], refs[m + 1]
        x, y, c = _position()
        chips = [_flip(x, y, k) for k in (1, 2, 3)]
        for slot in range(m):
            for j, (px, py) in enumerate(chips):
                sent = lnd[slot].at[4 * px + 2 * py + c]
                came = lnd[slot].at[4 * px + 2 * py + (1 - c)]
                cp = _remote(sent, came, fsend_ref.at[3 * slot + j], frecv_ref.at[3 * slot + j], (x, y, c))
                cp.wait_recv()
                cp.wait_send()

    out = pl.pallas_call(
        body, name=name,
        out_shape=_hbm_like(lands),
        in_specs=[HBM] * m + [SEM, SEM] + [ANY] * len(after), out_specs=[HBM] * m,
        input_output_aliases={i: i for i in range(m)},
        compiler_params=pltpu.CompilerParams(has_side_effects=EFFECT),
    )(*lands, fsend, frecv, *after)
    return list(out)


def _pair_exchange(parts, name):
    n = len(parts)

    def body(*refs):
        ins, outs = refs[:n], refs[n:2 * n]
        send_sems, recv_sems = refs[2 * n:]
        x, y, c = _position()
        copies = []
        for i in range(n):
            cp = pltpu.make_async_remote_copy(
                src_ref=ins[i].at[:, pl.ds(1 - c, 1)], dst_ref=outs[i],
                send_sem=send_sems.at[i], recv_sem=recv_sems.at[i],
                device_id=(x, y, 1 - c), device_id_type=MESH)
            cp.start()
            copies.append(cp)
        for cp in copies:
            cp.wait_recv()
        for cp in copies:
            cp.wait_send()

    return pl.pallas_call(
        body, name=name,
        out_shape=[jax.ShapeDtypeStruct((4, 1) + p.shape[2:], p.dtype) for p in parts],
        in_specs=[ANY] * n, out_specs=[ANY] * n,
        scratch_shapes=[pltpu.SemaphoreType.DMA((n,)), pltpu.SemaphoreType.DMA((n,))],
    )(*parts)


def _chip_start(sums, lands, name):
    n = len(sums)

    def body(*refs):
        ins, lnd = refs[:n], refs[n:2 * n]
        send_ref, recv_ref = refs[2 * n], refs[2 * n + 1]
        token = refs[-1]
        x, y, c = _position()
        for i in range(n):
            for j in range(3):
                px, py = _flip(x, y, j + 1)
                _remote(ins[i].at[j], lnd[i].at[j], send_ref.at[3 * i + j], recv_ref.at[3 * i + j], (px, py, c)).start()
        token[...] = jnp.zeros_like(token)

    out = pl.pallas_call(
        body, name=name,
        out_shape=[pltpu.SemaphoreType.DMA((3 * n,)), pltpu.SemaphoreType.DMA((3 * n,))]
        + _hbm_like(sums) + _hbm_like(lands) + [TOKEN],
        in_specs=[HBM] * (2 * n), out_specs=[SEM, SEM] + [HBM] * (2 * n) + [VMEM_SPEC],
        input_output_aliases={i: 2 + i for i in range(2 * n)},
        compiler_params=pltpu.CompilerParams(has_side_effects=EFFECT),
    )(*[_hbm(s) for s in sums], *[_hbm(l) for l in lands])
    return out[0], out[1], out[2:2 + n], out[2 + n:2 + 2 * n], out[-1]


def _chip_wait(sums, lands, send, recv, after, name):
    n = len(sums)

    def body(*refs):
        ins, lnd = refs[:n], refs[n:2 * n]
        send_ref, recv_ref = refs[2 * n], refs[2 * n + 1]
        x, y, c = _position()
        for i in range(n):
            for j in range(3):
                cp = _remote(ins[i].at[j], lnd[i].at[j], send_ref.at[3 * i + j], recv_ref.at[3 * i + j], (x, y, c))
                cp.wait_recv()
                cp.wait_send()

    out = pl.pallas_call(
        body, name=name,
        out_shape=_hbm_like(sums) + _hbm_like(lands),
        in_specs=[HBM] * (2 * n) + [SEM, SEM] + [ANY] * len(after), out_specs=[HBM] * (2 * n),
        input_output_aliases={i: i for i in range(2 * n)},
        compiler_params=pltpu.CompilerParams(has_side_effects=EFFECT),
    )(*sums, *lands, send, recv, *after)
    return list(out[n:])


def _pair_sum(idx, part, got, name):
    _, _, rows, cols = part.shape
    tr = _tile(rows, 512)

    def body(idx_ref, p_ref, q_ref, o_ref):
        o_ref[...] = (p_ref[...].astype(F32) + q_ref[...].astype(F32)).astype(o_ref.dtype)

    grid_spec = pltpu.PrefetchScalarGridSpec(
        num_scalar_prefetch=1, grid=(3, rows // tr),
        in_specs=[pl.BlockSpec((None, None, tr, cols), lambda j, r, idx: (idx[j], idx[4], r, 0)),
                  pl.BlockSpec((None, None, tr, cols), lambda j, r, idx: (idx[j], 0, r, 0))],
        out_specs=pl.BlockSpec((None, tr, cols), lambda j, r, idx: (j, r, 0)))
    return pl.pallas_call(
        body, name=name, grid_spec=grid_spec,
        out_shape=jax.ShapeDtypeStruct((3, rows, cols), part.dtype),
        compiler_params=_params("arbitrary", "arbitrary"),
    )(idx, part, got)


def _mm_proj(h, wg, groups, name):
    s, k = h.shape
    nchunk, _, n = wg.shape
    e = nchunk * n // groups
    tn = _tile(min(n, e), 512)

    def body(h_ref, w_ref, o_ref):
        o_ref[...] = jnp.dot(h_ref[...], w_ref[...], preferred_element_type=F32).astype(o_ref.dtype)

    return pl.pallas_call(
        body, name=name, grid=(nchunk * n // tn,),
        in_specs=[pl.BlockSpec((s, k), lambda j: (0, 0)),
                  pl.BlockSpec((None, k, tn), lambda j: ((j * tn) // n, 0, ((j * tn) % n) // tn))],
        out_specs=pl.BlockSpec((None, s, tn), lambda j: ((j * tn) // e, 0, ((j * tn) % e) // tn)),
        out_shape=jax.ShapeDtypeStruct((groups, s, e), BF16),
        compiler_params=_params("arbitrary"),
    )(h, wg)


def _mm_out(yb, w, x, gate, name):
    s, k = yb.shape
    d = w.shape[1]
    tn = _tile(d, 512)
    tk = _tile(k, 1024)
    nk = k // tk

    def body(a_ref, w_ref, x_ref, g_ref, xo_ref, y_ref, acc_ref):
        kk = pl.program_id(1)

        @pl.when(kk == 0)
        def _():
            acc_ref[...] = jnp.zeros_like(acc_ref)

        acc_ref[...] += jnp.dot(a_ref[...], w_ref[...], preferred_element_type=F32)

        @pl.when(kk == nk - 1)
        def _():
            y = acc_ref[...]
            y_ref[...] = y.astype(y_ref.dtype)
            xo_ref[...] = x_ref[...] + g_ref[...] * y

    return pl.pallas_call(
        body, name=name, grid=(d // tn, nk),
        in_specs=[pl.BlockSpec((s, tk), lambda j, kk: (0, kk)),
                  pl.BlockSpec((tk, tn), lambda j, kk: (kk, j)),
                  pl.BlockSpec((s, tn), lambda j, kk: (0, j)),
                  pl.BlockSpec((1, tn), lambda j, kk: (0, j))],
        out_specs=[pl.BlockSpec((s, tn), lambda j, kk: (0, j)),
                   pl.BlockSpec((s, tn), lambda j, kk: (0, j))],
        out_shape=[jax.ShapeDtypeStruct((s, d), F32), jax.ShapeDtypeStruct((s, d), BF16)],
        scratch_shapes=[pltpu.VMEM((s, tn), F32)],
        compiler_params=_params("arbitrary", "arbitrary"),
    )(yb, w, x, gate)


def _mm_nt(a3, w3, out_dtype, name, deps=()):
    g, s, ea = a3.shape
    cw, n, nw = w3.shape
    total = g * ea
    assert total == cw * nw
    tk = _tile(min(ea, nw), 1024)
    tn = _tile(n, 1024)
    nk = total // tk

    def body(a_ref, w_ref, o_ref, acc_ref):
        kk = pl.program_id(1)

        @pl.when(kk == 0)
        def _():
            acc_ref[...] = jnp.zeros_like(acc_ref)

        acc_ref[...] += lax.dot_general(a_ref[...], w_ref[...], (((1,), (1,)), ((), ())),
                                        preferred_element_type=F32)

        @pl.when(kk == nk - 1)
        def _():
            o_ref[...] = acc_ref[...].astype(o_ref.dtype)

    return pl.pallas_call(
        _after(body, 2, deps), name=name, grid=(n // tn, nk),
        in_specs=[pl.BlockSpec((None, s, tk), lambda j, kk: ((kk * tk) // ea, 0, ((kk * tk) % ea) // tk)),
                  pl.BlockSpec((None, tn, tk), lambda j, kk: ((kk * tk) // nw, j, ((kk * tk) % nw) // tk))]
        + [ANY] * len(deps),
        out_specs=pl.BlockSpec((s, tn), lambda j, kk: (0, j)),
        out_shape=jax.ShapeDtypeStruct((s, n), out_dtype),
        scratch_shapes=[pltpu.VMEM((s, tn), F32)],
        compiler_params=_params("arbitrary", "arbitrary"),
    )(a3, w3, *deps)


def _mm_tn(a, b3, nchunk, name, deps=()):
    s, ka = a.shape
    g, _, eb = b3.shape
    n = g * eb // nchunk
    tm = _tile(ka, 1024)
    tn = _tile(min(n, eb), 1024)

    def body(a_ref, b_ref, o_ref, at_ref):
        @pl.when(pl.program_id(1) == 0)
        def _():
            at_ref[...] = a_ref[...].astype(F32).T.astype(at_ref.dtype)

        o_ref[...] = jnp.dot(at_ref[...], b_ref[...], preferred_element_type=F32).astype(o_ref.dtype)

    return pl.pallas_call(
        _after(body, 2, deps), name=name, grid=(ka // tm, g * eb // tn),
        in_specs=[pl.BlockSpec((s, tm), lambda i, j: (0, i)),
                  pl.BlockSpec((None, s, tn), lambda i, j: ((j * tn) // eb, 0, ((j * tn) % eb) // tn))]
        + [ANY] * len(deps),
        out_specs=pl.BlockSpec((None, tm, tn), lambda i, j: ((j * tn) // n, i, ((j * tn) % n) // tn)),
        out_shape=jax.ShapeDtypeStruct((nchunk, ka, n), BF16),
        scratch_shapes=[pltpu.VMEM((tm, s), BF16)],
        compiler_params=_params("arbitrary", "arbitrary"),
    )(a, b3, *deps)


def _sigmoid(z):
    return jax.nn.sigmoid(z)


def _shift_down(v, k, fill=0.0):
    if k == 0:
        return v
    row = lax.broadcasted_iota(jnp.int32, v.shape, 0)
    return jnp.where(row >= k, pltpu.roll(v, k, 0), fill)


def _shift_up(v, k, fill=0.0):
    if k == 0:
        return v
    s = v.shape[0]
    row = lax.broadcasted_iota(jnp.int32, v.shape, 0)
    return jnp.where(row < s - k, pltpu.roll(v, s - k, 0), fill)


def _scan(a, b, shift):
    s = a.shape[0]
    k = 1
    while k < s:
        b = a * shift(b, k, 0.0) + b
        if 2 * k < s:
            a = a * shift(a, k, 1.0)
        k *= 2
    return b


def _norm_mod(x, g, scale, shift, name, deps=()):
    s, d = x.shape
    ts = _tile(s, 256)

    def body(x_ref, g_ref, sc_ref, sh_ref, h_ref):
        xv = x_ref[...]
        rstd = lax.rsqrt(jnp.mean(xv * xv, axis=-1, keepdims=True) + EPS)
        nrm = xv * rstd * g_ref[...]
        h_ref[...] = (nrm * (1.0 + sc_ref[...]) + sh_ref[...]).astype(h_ref.dtype)

    vec = pl.BlockSpec((1, d), lambda i: (0, 0))
    return pl.pallas_call(
        _after(body, 4, deps), name=name, grid=(s // ts,),
        in_specs=[pl.BlockSpec((ts, d), lambda i: (i, 0)), vec, vec, vec] + [ANY] * len(deps),
        out_specs=pl.BlockSpec((ts, d), lambda i: (i, 0)),
        out_shape=jax.ShapeDtypeStruct((s, d), BF16),
        compiler_params=_params("arbitrary"),
    )(x, g, scale, shift, *deps)


def _norm_mod_bwd(x, dh, dx_res, g, scale, name):
    s, d = x.shape
    ts = _tile(s, 256)

    def body(x_ref, dh_ref, dr_ref, g_ref, sc_ref, dx_ref, dsc_ref, dsh_ref, dg_ref):
        @pl.when(pl.program_id(0) == 0)
        def _():
            dsc_ref[...] = jnp.zeros_like(dsc_ref)
            dsh_ref[...] = jnp.zeros_like(dsh_ref)
            dg_ref[...] = jnp.zeros_like(dg_ref)

        xv = x_ref[...]
        dh_v = dh_ref[...].astype(F32)
        gv = g_ref[...]
        rstd = lax.rsqrt(jnp.mean(xv * xv, axis=-1, keepdims=True) + EPS)
        xhat = xv * rstd
        dsc_ref[...] += jnp.sum(dh_v * xhat * gv, axis=0, keepdims=True)
        dsh_ref[...] += jnp.sum(dh_v, axis=0, keepdims=True)
        dn = dh_v * (1.0 + sc_ref[...])
        dg_ref[...] += jnp.sum(dn * xhat, axis=0, keepdims=True)
        dxhat = dn * gv
        proj = jnp.mean(dxhat * xhat, axis=-1, keepdims=True)
        dx_ref[...] = dr_ref[...] + rstd * (dxhat - xhat * proj)

    row = pl.BlockSpec((ts, d), lambda i: (i, 0))
    vec = pl.BlockSpec((1, d), lambda i: (0, 0))
    return pl.pallas_call(
        body, name=name, grid=(s // ts,),
        in_specs=[row, row, row, vec, vec],
        out_specs=[row, vec, vec, vec],
        out_shape=[jax.ShapeDtypeStruct((s, d), F32)] + [jax.ShapeDtypeStruct((1, d), F32)] * 3,
        compiler_params=_params("arbitrary"),
    )(x, dh, dx_res, g, scale)


def _final_loss(x, g, target, name):
    s, d = x.shape
    ts = _tile(s, 256)

    def body(x_ref, g_ref, t_ref, dx_ref, loss_ref, dg_ref):
        @pl.when(pl.program_id(0) == 0)
        def _():
            loss_ref[...] = jnp.zeros_like(loss_ref)
            dg_ref[...] = jnp.zeros_like(dg_ref)

        xv = x_ref[...]
        gv = g_ref[...]
        rstd = lax.rsqrt(jnp.mean(xv * xv, axis=-1, keepdims=True) + EPS)
        xhat = xv * rstd
        err = xhat * gv - t_ref[...]
        loss_ref[...] += 0.5 * jnp.sum(jnp.mean(err * err, axis=-1, keepdims=True))
        dy = err * (1.0 / d)
        dg_ref[...] += jnp.sum(dy * xhat, axis=0, keepdims=True)
        dxhat = dy * gv
        proj = jnp.mean(dxhat * xhat, axis=-1, keepdims=True)
        dx_ref[...] = rstd * (dxhat - xhat * proj)

    row = pl.BlockSpec((ts, d), lambda i: (i, 0))
    vec = pl.BlockSpec((1, d), lambda i: (0, 0))
    return pl.pallas_call(
        body, name=name, grid=(s // ts,),
        in_specs=[row, vec, row],
        out_specs=[row, pl.BlockSpec((1, LANES), lambda i: (0, 0)), vec],
        out_shape=[jax.ShapeDtypeStruct((s, d), F32), jax.ShapeDtypeStruct((1, LANES), F32),
                   jax.ShapeDtypeStruct((1, d), F32)],
        compiler_params=_params("arbitrary"),
    )(x, g, target)


def _gate_bwd(dx, y, gate, name):
    s, d = dx.shape
    ts = _tile(s, 256)

    def body(dx_ref, y_ref, g_ref, dy_ref, dgate_ref):
        @pl.when(pl.program_id(0) == 0)
        def _():
            dgate_ref[...] = jnp.zeros_like(dgate_ref)

        dxv = dx_ref[...]
        dy_ref[...] = (dxv * g_ref[...]).astype(dy_ref.dtype)
        dgate_ref[...] += jnp.sum(dxv * y_ref[...].astype(F32), axis=0, keepdims=True)

    row = pl.BlockSpec((ts, d), lambda i: (i, 0))
    vec = pl.BlockSpec((1, d), lambda i: (0, 0))
    return pl.pallas_call(
        body, name=name, grid=(s // ts,),
        in_specs=[row, row, vec], out_specs=[row, vec],
        out_shape=[jax.ShapeDtypeStruct((s, d), BF16), jax.ShapeDtypeStruct((1, d), F32)],
        compiler_params=_params("arbitrary"),
    )(dx, y, gate)


def _conv(v, w_ref, width):
    out = w_ref[width - 1:width, :] * v
    for k in range(width - 1):
        out = out + w_ref[k:k + 1, :] * _shift_down(v, width - 1 - k)
    return out


def _sc_fwd(proj, conv_w, name, deps=()):
    _, s, e = proj.shape
    te = _tile(e, 256)
    width = conv_w.shape[0]

    def body(b_ref, c_ref, v_ref, g_ref, w_ref, o_ref):
        cv = c_ref[...].astype(F32) * v_ref[...].astype(F32)
        u = _conv(cv, w_ref, width)
        gv = g_ref[...].astype(F32)
        o_ref[...] = (b_ref[...].astype(F32) * u * (gv * _sigmoid(gv))).astype(o_ref.dtype)

    def part(q):
        return pl.BlockSpec((None, s, te), lambda j, q=q: (q, 0, j))

    return pl.pallas_call(
        _after(body, 5, deps), name=name, grid=(e // te,),
        in_specs=[part(0), part(1), part(2), part(3), pl.BlockSpec((width, te), lambda j: (0, j))]
        + [ANY] * len(deps),
        out_specs=pl.BlockSpec((s, te), lambda j: (0, j)),
        out_shape=jax.ShapeDtypeStruct((s, e), BF16),
        compiler_params=_params("arbitrary"),
    )(proj, proj, proj, proj, conv_w, *deps)


def _sc_bwd(proj, dyb, conv_w, name):
    _, s, e = proj.shape
    te = _tile(e, 256)
    width = conv_w.shape[0]

    def body(b_ref, c_ref, v_ref, g_ref, dy_ref, w_ref, dp_ref, vec_ref):
        bv = b_ref[...].astype(F32)
        cvl = c_ref[...].astype(F32)
        vv = v_ref[...].astype(F32)
        gv = g_ref[...].astype(F32)
        dyv = dy_ref[...].astype(F32)
        cv = cvl * vv
        u = _conv(cv, w_ref, width)
        sg = _sigmoid(gv)
        silu = gv * sg
        dp_ref[0] = (dyv * u * silu).astype(dp_ref.dtype)
        du = dyv * bv * silu
        dp_ref[3] = (dyv * bv * u * (sg * (1.0 + gv * (1.0 - sg)))).astype(dp_ref.dtype)
        dcv = w_ref[width - 1:width, :] * du
        vec_ref[...] = jnp.zeros_like(vec_ref)
        vec_ref[width - 1:width, :] = jnp.sum(du * cv, axis=0, keepdims=True)
        for k in range(width - 1):
            sh = width - 1 - k
            dcv = dcv + w_ref[k:k + 1, :] * _shift_up(du, sh)
            vec_ref[k:k + 1, :] = jnp.sum(du * _shift_down(cv, sh), axis=0, keepdims=True)
        dp_ref[1] = (dcv * vv).astype(dp_ref.dtype)
        dp_ref[2] = (dcv * cvl).astype(dp_ref.dtype)

    def part(q):
        return pl.BlockSpec((None, s, te), lambda j, q=q: (q, 0, j))

    return pl.pallas_call(
        body, name=name, grid=(e // te,),
        in_specs=[part(0), part(1), part(2), part(3), pl.BlockSpec((s, te), lambda j: (0, j)),
                  pl.BlockSpec((width, te), lambda j: (0, j))],
        out_specs=[pl.BlockSpec((4, s, te), lambda j: (0, 0, j)),
                   pl.BlockSpec((8, te), lambda j: (0, j))],
        out_shape=[jax.ShapeDtypeStruct((4, s, e), BF16), jax.ShapeDtypeStruct((8, e), F32)],
        compiler_params=_params("arbitrary"),
    )(proj, proj, proj, proj, dyb, conv_w)


def _lru_gates(v_pre, w_ref, cb_ref, wa_ref, ba_ref, wx_ref, bx_ref, lam_ref, width):
    v = _conv(v_pre, w_ref, width) + cb_ref[...]
    vb = v.astype(BF16)
    r = _sigmoid(jnp.dot(vb, wa_ref[...], preferred_element_type=F32) + ba_ref[...])
    i = _sigmoid(jnp.dot(vb, wx_ref[...], preferred_element_type=F32) + bx_ref[...])
    nl = -lam_ref[...]
    sp = jnp.maximum(nl, 0.0) + jnp.log1p(jnp.exp(-jnp.abs(nl)))
    log_a = (-RGLRU_C) * r * sp
    a = jnp.exp(log_a)
    one_minus_a2 = jnp.tanh(-log_a) * (1.0 + a * a)
    mult = jnp.sqrt(one_minus_a2)
    return v, vb, r, i, sp, a, mult


def _lru_specs(s, dh, heads, width):
    head_col = lambda q: pl.BlockSpec((None, s, dh), lambda h, q=q: (q, 0, h))
    vec = pl.BlockSpec((1, dh), lambda h: (0, h))
    mat = pl.BlockSpec((None, dh, dh), lambda h: (h, 0, 0))
    weights = [pl.BlockSpec((width, dh), lambda h: (0, h)), vec, mat, vec, mat, vec, vec]
    return head_col, weights


def _lru_fwd(proj, conv_w, conv_b, w_a, b_a, w_x, b_x, lam, name, deps=()):
    _, s, e = proj.shape
    heads, dh, _ = w_a.shape
    width = conv_w.shape[0]

    def body(v_ref, g_ref, w_ref, cb_ref, wa_ref, ba_ref, wx_ref, bx_ref, lam_ref, yb_ref, hs_ref):
        v, _, _, i, _, a, mult = _lru_gates(v_ref[...].astype(F32), w_ref, cb_ref, wa_ref, ba_ref,
                                           wx_ref, bx_ref, lam_ref, width)
        hs = _scan(a, mult * i * v, _shift_down)
        hs_ref[...] = hs
        gv = g_ref[...].astype(F32)
        yb_ref[...] = (hs * (gv * _sigmoid(gv))).astype(yb_ref.dtype)

    head_col, weights = _lru_specs(s, dh, heads, width)
    out = pl.BlockSpec((s, dh), lambda h: (0, h))
    return pl.pallas_call(
        _after(body, 9, deps), name=name, grid=(heads,),
        in_specs=[head_col(0), head_col(1)] + weights + [ANY] * len(deps),
        out_specs=[out, out],
        out_shape=[jax.ShapeDtypeStruct((s, e), BF16), jax.ShapeDtypeStruct((s, e), F32)],
        compiler_params=_params("arbitrary"),
    )(proj, proj, conv_w, conv_b, w_a, b_a, w_x, b_x, lam, *deps)


def _lru_bwd(proj, hs, dyb, conv_w, conv_b, w_a, b_a, w_x, b_x, lam, name):
    _, s, e = proj.shape
    heads, dh, _ = w_a.shape
    width = conv_w.shape[0]

    def body(v_ref, g_ref, hs_ref, dy_ref, w_ref, cb_ref, wa_ref, ba_ref, wx_ref, bx_ref, lam_ref,
             dp_ref, dwa_ref, dwx_ref, vec_ref):
        v_pre = v_ref[...].astype(F32)
        v, vb, r, i, sp, a, mult = _lru_gates(v_pre, w_ref, cb_ref, wa_ref, ba_ref, wx_ref, bx_ref,
                                              lam_ref, width)
        hs = hs_ref[...]
        gv = g_ref[...].astype(F32)
        dyv = dy_ref[...].astype(F32)
        sg = _sigmoid(gv)
        dp_ref[1] = (dyv * hs * (sg * (1.0 + gv * (1.0 - sg)))).astype(dp_ref.dtype)
        dhs = dyv * (gv * sg)
        d_h = _scan(_shift_up(a, 1), dhs, _shift_up)
        da = d_h * _shift_down(hs, 1)
        iv = i * v
        dlog_a = da * a - (d_h * iv) * (a * a) / mult
        di = d_h * mult * v
        dv = d_h * mult * i
        dzr = dlog_a * (-RGLRU_C) * sp * r * (1.0 - r)
        dzi = di * i * (1.0 - i)
        dsp = jnp.sum(dlog_a * r, axis=0, keepdims=True) * (-RGLRU_C)
        vec_ref[...] = jnp.zeros_like(vec_ref)
        vec_ref[0:1, :] = jnp.sum(dzr, axis=0, keepdims=True)
        vec_ref[1:2, :] = jnp.sum(dzi, axis=0, keepdims=True)
        vec_ref[2:3, :] = -dsp * _sigmoid(-lam_ref[...])
        dzr_b = dzr.astype(BF16)
        dzi_b = dzi.astype(BF16)
        vt = vb.astype(F32).T.astype(BF16)
        dwa_ref[...] = jnp.dot(vt, dzr_b, preferred_element_type=F32).astype(dwa_ref.dtype)
        dwx_ref[...] = jnp.dot(vt, dzi_b, preferred_element_type=F32).astype(dwx_ref.dtype)
        nt = (((1,), (1,)), ((), ()))
        dv = dv + lax.dot_general(dzr_b, wa_ref[...], nt, preferred_element_type=F32)
        dv = dv + lax.dot_general(dzi_b, wx_ref[...], nt, preferred_element_type=F32)
        vec_ref[3:4, :] = jnp.sum(dv, axis=0, keepdims=True)
        dvp = w_ref[width - 1:width, :] * dv
        vec_ref[4 + width - 1:4 + width, :] = jnp.sum(dv * v_pre, axis=0, keepdims=True)
        for k in range(width - 1):
            sh = width - 1 - k
            dvp = dvp + w_ref[k:k + 1, :] * _shift_up(dv, sh)
            vec_ref[4 + k:5 + k, :] = jnp.sum(dv * _shift_down(v_pre, sh), axis=0, keepdims=True)
        dp_ref[0] = dvp.astype(dp_ref.dtype)

    head_col, weights = _lru_specs(s, dh, heads, width)
    col = pl.BlockSpec((s, dh), lambda h: (0, h))
    mat = pl.BlockSpec((None, dh, dh), lambda h: (h, 0, 0))
    return pl.pallas_call(
        body, name=name, grid=(heads,),
        in_specs=[head_col(0), head_col(1), col, col] + weights,
        out_specs=[pl.BlockSpec((2, s, dh), lambda h: (0, 0, h)), mat, mat,
                   pl.BlockSpec((16, dh), lambda h: (0, h))],
        out_shape=[jax.ShapeDtypeStruct((2, s, e), BF16),
                   jax.ShapeDtypeStruct((heads, dh, dh), BF16),
                   jax.ShapeDtypeStruct((heads, dh, dh), BF16),
                   jax.ShapeDtypeStruct((16, e), F32)],
        compiler_params=_params("arbitrary"),
    )(proj, proj, hs, dyb, conv_w, conv_b, w_a, b_a, w_x, b_x, lam)


def _ada_mod(c_all, w, b, name):
    layers, d, f = w.shape
    nb = c_all.shape[0]

    def body(c_ref, w_ref, b_ref, o_ref):
        cv = c_ref[...]
        sc = cv * _sigmoid(cv)
        o_ref[...] = jnp.dot(sc, w_ref[...], preferred_element_type=F32,
                             precision=lax.Precision.HIGHEST) + b_ref[...]

    return pl.pallas_call(
        body, name=name, grid=(layers,),
        in_specs=[pl.BlockSpec((nb, d), lambda l: (0, 0)),
                  pl.BlockSpec((None, d, f), lambda l: (l, 0, 0)),
                  pl.BlockSpec((None, 1, f), lambda l: (l, 0, 0))],
        out_specs=pl.BlockSpec((None, nb, f), lambda l: (l, 0, 0)),
        out_shape=jax.ShapeDtypeStruct((layers, nb, f), F32),
        compiler_params=_params("arbitrary"),
    )(c_all, w, b)


def _ada_grad(c_all_t, dmod, name):
    d, nb = c_all_t.shape
    layers, _, f = dmod.shape

    def body(c_ref, dm_ref, o_ref):
        cv = c_ref[...]
        sc = cv * _sigmoid(cv)
        acc = sc[:, 0:1] * dm_ref[0:1, :]
        for k in range(1, nb):
            acc = acc + sc[:, k:k + 1] * dm_ref[k:k + 1, :]
        o_ref[...] = acc

    return pl.pallas_call(
        body, name=name, grid=(layers,),
        in_specs=[pl.BlockSpec((d, nb), lambda l: (0, 0)),
                  pl.BlockSpec((None, nb, f), lambda l: (l, 0, 0))],
        out_specs=pl.BlockSpec((None, d, f), lambda l: (l, 0, 0)),
        out_shape=jax.ShapeDtypeStruct((layers, d, f), F32),
        compiler_params=_params("arbitrary"),
    )(c_all_t, dmod)


def _device_sum(g, name):
    _, rows, _ = g.shape

    def body(g_ref, o_ref):
        acc = g_ref[0]
        for k in range(1, N_DEV):
            acc = acc + g_ref[k]
        o_ref[...] = acc

    return pl.pallas_call(
        body, name=name,
        in_specs=[VMEM_SPEC], out_specs=VMEM_SPEC,
        out_shape=jax.ShapeDtypeStruct((rows, LANES), F32),
        compiler_params=pltpu.CompilerParams(vmem_limit_bytes=VMEM_LIMIT),
    )(g)


def _adamw_math(w, g, m, v):
    m = ADAM_B1 * m + (1.0 - ADAM_B1) * g
    v = ADAM_B2 * v + (1.0 - ADAM_B2) * (g * g)
    m_hat = m / (1.0 - ADAM_B1 ** ADAM_STEP)
    v_hat = v / (1.0 - ADAM_B2 ** ADAM_STEP)
    delta = -ADAM_LR * (m_hat / (jnp.sqrt(v_hat) + ADAM_EPS) + ADAM_WD * w)
    return delta, m, v


def _adamw(w, g, m, v, name):
    rows, cols = w.shape
    tr = _tile(rows, 256)

    def body(w_ref, g_ref, m_ref, v_ref, d_ref, mo_ref, vo_ref):
        d_ref[...], mo_ref[...], vo_ref[...] = _adamw_math(w_ref[...], g_ref[...], m_ref[...], v_ref[...])

    blk = pl.BlockSpec((tr, cols), lambda i: (i, 0))
    return pl.pallas_call(
        body, name=name, grid=(rows // tr,),
        in_specs=[blk] * 4, out_specs=[blk] * 3,
        out_shape=[jax.ShapeDtypeStruct((rows, cols), F32)] * 3,
        compiler_params=_params("arbitrary"),
    )(w, g, m, v)


def _adamw_reduced(idx, w, m, v, part, got, recv, name):
    rows, cols = w.shape
    tr = _tile(rows, 256)

    def body(idx_ref, w_ref, m_ref, v_ref, p_ref, q_ref, u_ref, g_ref, d_ref, mo_ref, vo_ref):
        g = p_ref[...].astype(F32) + q_ref[...].astype(F32)
        for j in range(3):
            g = g + u_ref[j].astype(F32)
        g_ref[...] = g
        d_ref[...], mo_ref[...], vo_ref[...] = _adamw_math(w_ref[...], g, m_ref[...], v_ref[...])

    blk = pl.BlockSpec((tr, cols), lambda i, idx: (i, 0))
    grid_spec = pltpu.PrefetchScalarGridSpec(
        num_scalar_prefetch=1, grid=(rows // tr,),
        in_specs=[blk, blk, blk,
                  pl.BlockSpec((None, None, tr, cols), lambda i, idx: (idx[3], idx[4], i, 0)),
                  pl.BlockSpec((None, None, tr, cols), lambda i, idx: (idx[3], 0, i, 0)),
                  pl.BlockSpec((3, tr, cols), lambda i, idx: (0, i, 0))],
        out_specs=[blk] * 4)
    return pl.pallas_call(
        body, name=name, grid_spec=grid_spec,
        out_shape=[jax.ShapeDtypeStruct((rows, cols), F32)] * 4,
        compiler_params=_params("arbitrary"),
    )(idx, w, m, v, part, got, recv)


def _pack(vectors):
    flat = jnp.concatenate([v.reshape(-1).astype(F32) for v in vectors])
    pad = (-flat.shape[0]) % (8 * LANES)
    return jnp.pad(flat, (0, pad)).reshape(-1, LANES)


def _unpack(flat, shapes):
    out, off = [], 0
    for shp in shapes:
        size = math.prod(shp)
        out.append(flat[..., off:off + size].reshape(flat.shape[:-1] + tuple(shp)))
        off += size
    return out


def _my_slice(full, me, axis):
    size = full.shape[axis] // N_DEV
    return lax.dynamic_slice_in_dim(full, me * size, size, axis)


def kernel(x, c, norm_g, ada_w, ada_b, sc_w_in, sc_conv_w, sc_w_out, lru_w_in, lru_conv_w, lru_conv_b, lru_w_a, lru_b_a, lru_w_x, lru_b_x, lru_lambda, lru_w_out, final_g, loss_target, m_norm_g, m_ada_w, m_ada_b, m_sc_w_in, m_sc_conv_w, m_sc_w_out, m_lru_w_in, m_lru_conv_w, m_lru_conv_b, m_lru_w_a, m_lru_b_a, m_lru_w_x, m_lru_b_x, m_lru_lambda, m_lru_w_out, m_final_g, v_norm_g, v_ada_w, v_ada_b, v_sc_w_in, v_sc_conv_w, v_sc_w_out, v_lru_w_in, v_lru_conv_w, v_lru_conv_b, v_lru_w_a, v_lru_b_a, v_lru_w_x, v_lru_b_x, v_lru_lambda, v_lru_w_out, v_final_g):
    _, s, d = x.shape
    e = sc_w_out.shape[1] * N_DEV
    heads, dh_s, dh = lru_w_a.shape[1:]
    es = e // N_DEV
    f = ada_w.shape[2]
    mx, my, mc = _position()
    me = 4 * mx + 2 * my + mc
    chip = 2 * mx + my
    idx = jnp.stack([chip ^ 1, chip ^ 2, chip ^ 3, chip, mc]).astype(jnp.int32)

    x0 = x[0]
    target = loss_target[0]

    small_shapes = [(d,), (3, es), (4, es), (es,), (heads, dh_s), (heads, dh_s), (es,)]
    small = _small_gather(_pack([c, sc_conv_w, lru_conv_w, lru_conv_b, lru_b_a, lru_b_x, lru_lambda]),
                          "gather_small_weights").reshape(N_DEV, -1)
    c_all, cw3, cw4, cb, ba, bx, lam = _unpack(small, small_shapes)
    cw3 = cw3.transpose(1, 0, 2).reshape(3, e)
    cw4 = cw4.transpose(1, 0, 2).reshape(4, e)
    cb = cb.reshape(1, e)
    lam = lam.reshape(1, e)
    ba = ba.transpose(1, 0, 2).reshape(1, e)
    bx = bx.transpose(1, 0, 2).reshape(1, e)

    ada_b_mine = _my_slice(ada_b, me, 1).reshape(2, 1, f)
    mod_mine = _ada_mod(c_all, ada_w, ada_b_mine, "ada_mod")
    mod_all = _small_gather(_pack([mod_mine]), "gather_mod")

    shards = [sc_w_in[0].astype(BF16), sc_w_out[0].astype(BF16), lru_w_in[0].astype(BF16),
              lru_w_a[0].reshape(heads * dh_s, dh).astype(BF16),
              lru_w_x[0].reshape(heads * dh_s, dh).astype(BF16), lru_w_out[0].astype(BF16)]
    lands = [lax.dynamic_update_slice(lax.empty((N_DEV,) + sh.shape, BF16), sh[None], (me, 0, 0))
             for sh in shards]
    groups = [[0], [1], [2, 3, 4], [5]]
    sems, shards, lands, started = _gather_start(shards, lands, groups, [mod_all], "gather_start")

    def gathered(g, after_forward, name):
        members = groups[g]
        fsend, frecv, lnd, token = _gather_forward(
            [shards[i] for i in members], [lands[i] for i in members], sems[g][0], sems[g][1],
            after_forward, "gather_forward_" + name)
        return token, lambda after: _gather_finish(lnd, fsend, frecv, after, "gather_finish_" + name)

    mod_all = mod_all.reshape(N_DEV, -1)
    mod_all = mod_all[:, :2 * N_DEV * f].reshape(N_DEV, 2, N_DEV, f)
    mod_all = mod_all.transpose(1, 2, 0, 3).reshape(2, N_DEV, 3 * d)
    mod = lax.dynamic_index_in_dim(mod_all, me, 1, keepdims=False)
    shift = [mod[l:l + 1, 0:d] for l in range(2)]
    scale = [mod[l:l + 1, d:2 * d] for l in range(2)]
    gate = [mod[l:l + 1, 2 * d:3 * d] for l in range(2)]
    ng = [norm_g[l:l + 1] for l in range(2)]
    fg = final_g.reshape(1, d)

    h0 = _norm_mod(x0, ng[0], scale[0], shift[0], "norm_mod_0", deps=[started])
    tok, finish = gathered(0, [h0], "sc_w_in")
    wg_in0, = finish([tok])
    proj0 = _mm_proj(h0, wg_in0, 4, "mm_proj_0")
    tok, finish = gathered(1, [proj0], "sc_w_out")
    yb0 = _sc_fwd(proj0, cw3, "sc_fwd", deps=[tok])
    w_out0 = finish([yb0])[0].reshape(e, d)
    x1, y0 = _mm_out(yb0, w_out0, x0, gate[0], "mm_out_0")
    tok, finish = gathered(2, [x1], "lru_in")
    h1 = _norm_mod(x1, ng[1], scale[1], shift[1], "norm_mod_1", deps=[tok])
    wg_in1, wg_a, wg_x = finish([h1])
    w_a = wg_a.reshape(N_DEV, heads, dh_s, dh).transpose(1, 0, 2, 3).reshape(heads, dh, dh)
    w_x = wg_x.reshape(N_DEV, heads, dh_s, dh).transpose(1, 0, 2, 3).reshape(heads, dh, dh)
    proj1 = _mm_proj(h1, wg_in1, 2, "mm_proj_1")
    tok, finish = gathered(3, [proj1], "lru_w_out")
    yb1, hs = _lru_fwd(proj1, cw4, cb, w_a, ba, w_x, bx, lam, "lru_fwd", deps=[tok])
    w_out1 = finish([yb1])[0].reshape(e, d)
    x2, y1 = _mm_out(yb1, w_out1, x1, gate[1], "mm_out_1")
    dx2, loss_part, d_fg = _final_loss(x2, fg, target, "final_loss")

    def pieces(g, rows, cols):
        return g.reshape(4, 2, rows, cols)

    def by_rows(g):
        return g.reshape(heads, N_DEV, dh_s, dh).transpose(1, 0, 2, 3).reshape(N_DEV, heads * dh_s, dh)

    def scatter_start(parts, names, group):
        gots = _pair_exchange(parts, "pair_exchange_" + group)
        sums = [_pair_sum(idx, p, q, "pair_sum_" + nm) for p, q, nm in zip(parts, gots, names)]
        empties = [lax.empty(sm.shape, sm.dtype) for sm in sums]
        send, recv, sums, lnd, token = _chip_start(sums, empties, "chip_start_" + group)
        return dict(parts=parts, gots=gots, names=names, group=group, sums=sums, lands=lnd,
                    send=send, recv=recv), token

    big = {"sc_w_in": (sc_w_in, m_sc_w_in, v_sc_w_in), "sc_w_out": (sc_w_out, m_sc_w_out, v_sc_w_out),
           "lru_w_in": (lru_w_in, m_lru_w_in, v_lru_w_in), "lru_w_a": (lru_w_a, m_lru_w_a, v_lru_w_a),
           "lru_w_x": (lru_w_x, m_lru_w_x, v_lru_w_x), "lru_w_out": (lru_w_out, m_lru_w_out, v_lru_w_out)}
    big_res = {}

    def scatter_finish(rs, after):
        recvs = _chip_wait(rs["sums"], rs["lands"], rs["send"], rs["recv"], after, "chip_wait_" + rs["group"])
        done = []
        for p, q, u, nm in zip(rs["parts"], rs["gots"], recvs, rs["names"]):
            w, m, v = big[nm]
            shp2 = p.shape[2:]
            res = _adamw_reduced(idx, w.reshape(shp2), m.reshape(shp2), v.reshape(shp2), p, q, u, "adamw_" + nm)
            big_res[nm] = [r.reshape(w.shape) for r in res]
            done.append(res[1])
        return done

    dy1, dgate1 = _gate_bwd(dx2, y1, gate[1], "gate_bwd_1")
    dw_out1 = _mm_tn(yb1, dy1[None], 1, "mm_dw_out_1")
    rs1, tok = scatter_start([pieces(dw_out1, es, d)], ["lru_w_out"], "lru_w_out")
    dyb1 = _mm_nt(dy1[None], w_out1[None], BF16, "mm_dyb_1", deps=[tok])
    dproj1, dw_a, dw_x, vecs1 = _lru_bwd(proj1, hs, dyb1, cw4, cb, w_a, ba, w_x, bx, lam, "lru_bwd")
    done = scatter_finish(rs1, [dproj1])
    dw_in1 = _mm_tn(h1, dproj1, N_DEV, "mm_dw_in_1", deps=done)
    rs2, tok = scatter_start([pieces(dw_in1, d, 2 * es), pieces(by_rows(dw_a), heads * dh_s, dh),
                              pieces(by_rows(dw_x), heads * dh_s, dh)],
                             ["lru_w_in", "lru_w_a", "lru_w_x"], "lru_in")
    dh1 = _mm_nt(dproj1, wg_in1, F32, "mm_dh_1", deps=[tok])
    dx1, dscale1, dshift1, dng1 = _norm_mod_bwd(x1, dh1, dx2, ng[1], scale[1], "norm_mod_bwd_1")
    dy0, dgate0 = _gate_bwd(dx1, y0, gate[0], "gate_bwd_0")
    dw_out0 = _mm_tn(yb0, dy0[None], 1, "mm_dw_out_0")
    rs3, tok = scatter_start([pieces(dw_out0, es, d)], ["sc_w_out"], "sc_w_out")
    dyb0 = _mm_nt(dy0[None], w_out0[None], BF16, "mm_dyb_0", deps=[tok])
    dproj0, vecs0 = _sc_bwd(proj0, dyb0, cw3, "sc_bwd")
    early_shapes = [(LANES,), (d,), (3 * d,), (d,), (d,), (8, e), (16, e)]
    early_all = _small_gather(
        _pack([loss_part, dgate0, jnp.concatenate([dshift1, dscale1, dgate1], axis=1), dng1, d_fg, vecs0, vecs1]),
        "gather_small_grads_early")
    done = scatter_finish(rs2, [early_all])
    dw_in0 = _mm_tn(h0, dproj0, N_DEV, "mm_dw_in_0", deps=done)
    rs4, tok = scatter_start([pieces(dw_in0, d, 4 * es)], ["sc_w_in"], "sc_w_in")
    dh0 = _mm_nt(dproj0, wg_in0, F32, "mm_dh_0", deps=[tok])
    dx0, dscale0, dshift0, dng0 = _norm_mod_bwd(x0, dh0, dx1, ng[0], scale[0], "norm_mod_bwd_0")
    done = scatter_finish(rs3, [dx0])
    early_sum = _device_sum(early_all, "sum_small_grads_early").reshape(-1)
    late_shapes = [(d,), (d,), (d,)]
    late_all = _small_gather(_pack([dshift0, dscale0, dng0]), "gather_small_grads_late", deps=done)
    late_sum = _device_sum(late_all, "sum_small_grads_late").reshape(-1)

    loss_v, s_dgate0, s_dmod1, s_dng1, g_final_g, sum0, sum1 = _unpack(early_sum, early_shapes)
    s_dshift0, s_dscale0, s_dng0 = _unpack(late_sum, late_shapes)
    loss = loss_v[0]
    g_ada_b = jnp.stack([jnp.concatenate([s_dshift0, s_dscale0, s_dgate0]), s_dmod1])
    g_norm_g = jnp.stack([s_dng0, s_dng1])
    _, a_dgate0, a_dmod1 = _unpack(early_all.reshape(N_DEV, -1), early_shapes)[:3]
    a_dshift0, a_dscale0, _ = _unpack(late_all.reshape(N_DEV, -1), late_shapes)
    dmod_all = jnp.stack([jnp.concatenate([a_dshift0, a_dscale0, a_dgate0], axis=1), a_dmod1])
    dmod_cols = _my_slice(dmod_all, me, 2)
    g_ada_w = _ada_grad(c_all.T, dmod_cols, "ada_grad")

    g_sc_conv_w = _my_slice(sum0[0:3], me, 1)
    g_lru_b_a = _my_slice(sum1[0].reshape(heads, dh), me, 1)
    g_lru_b_x = _my_slice(sum1[1].reshape(heads, dh), me, 1)
    g_lru_lambda = _my_slice(sum1[2:3], me, 1)
    g_lru_conv_b = _my_slice(sum1[3:4], me, 1)
    g_lru_conv_w = _my_slice(sum1[4:8], me, 1)

    ada_res = _adamw(ada_w.reshape(2 * d, f), g_ada_w.reshape(2 * d, f), m_ada_w.reshape(2 * d, f),
                     v_ada_w.reshape(2 * d, f), "adamw_ada_w")
    ada_out = [g_ada_w] + [r.reshape(ada_w.shape) for r in ada_res]

    small_w = [norm_g, ada_b, final_g, sc_conv_w, lru_conv_w, lru_conv_b, lru_b_a, lru_b_x, lru_lambda]
    small_m = [m_norm_g, m_ada_b, m_final_g, m_sc_conv_w, m_lru_conv_w, m_lru_conv_b, m_lru_b_a, m_lru_b_x,
               m_lru_lambda]
    small_v = [v_norm_g, v_ada_b, v_final_g, v_sc_conv_w, v_lru_conv_w, v_lru_conv_b, v_lru_b_a, v_lru_b_x,
               v_lru_lambda]
    small_g = [g_norm_g, g_ada_b, g_final_g, g_sc_conv_w, g_lru_conv_w, g_lru_conv_b, g_lru_b_a, g_lru_b_x,
               g_lru_lambda]
    small_g = [g.reshape(w.shape) for g, w in zip(small_g, small_w)]
    shapes = [w.shape for w in small_w]
    packed = _adamw(_pack(small_w), _pack(small_g), _pack(small_m), _pack(small_v), "adamw_small")
    small_out = [small_g] + [_unpack(p.reshape(-1), shapes) for p in packed]

    scatter_finish(rs4, [packed[0], ada_res[0]])
    big_out = [big_res[nm] for nm in ("sc_w_in", "sc_w_out", "lru_w_in", "lru_w_a", "lru_w_x", "lru_w_out")]

    def small(kind, i):
        return small_out[kind][i]

    def bigw(kind, i):
        return big_out[i][kind]

    outs = [loss, dx0[None]]
    for kind in range(4):
        outs += [small(kind, 0), ada_out[kind], small(kind, 1), bigw(kind, 0), small(kind, 3), bigw(kind, 1),
                 bigw(kind, 2), small(kind, 4), small(kind, 5), bigw(kind, 3), small(kind, 6), bigw(kind, 4),
                 small(kind, 7), small(kind, 8), bigw(kind, 5), small(kind, 2)]
    return tuple(outs)
```

```python
import math

import jax
import jax.numpy as jnp
from jax import lax
from jax.experimental import pallas as pl
from jax.experimental.pallas import tpu as pltpu

N_DEV = 8
LANES = 128
EPS = 1e-6
RGLRU_C = 8.0
ADAM_LR = 0.001
ADAM_B1 = 0.9
ADAM_B2 = 0.999
ADAM_EPS = 1e-08
ADAM_WD = 0.01
ADAM_STEP = 10
VMEM_LIMIT = 56 * 1024 * 1024
MESH = pl.DeviceIdType.MESH
F32 = jnp.float32
BF16 = jnp.bfloat16
ANY = pl.BlockSpec(memory_space=pl.ANY)
HBM = pl.BlockSpec(memory_space=pltpu.HBM)
SEM = pl.BlockSpec(memory_space=pltpu.SEMAPHORE)
VMEM_SPEC = pl.BlockSpec(memory_space=pltpu.VMEM)
EFFECT = pltpu.SideEffectType.DATAFLOW_SIDE_EFFECTING
TOKEN = jax.ShapeDtypeStruct((8, LANES), jnp.float32)


def _tile(n, pref):
    t = min(n, pref)
    assert n % t == 0, (n, pref)
    return t


def _params(*sem):
    return pltpu.CompilerParams(dimension_semantics=sem, vmem_limit_bytes=VMEM_LIMIT)


def _position():
    return lax.axis_index("x"), lax.axis_index("y"), lax.axis_index("c")


def _flip(x, y, k):
    return (1 - x if k & 2 else x), (1 - y if k & 1 else y)


def _after(body, n_in, deps):
    if not deps:
        return body

    def wrapped(*refs):
        return body(*refs[:n_in], *refs[n_in + len(deps):])

    return wrapped


def _small_gather(v, name, deps=()):
    rows = v.shape[0]

    def body(v_ref, out_ref, send_sems, recv_sems):
        x, y, c = _position()
        me = 4 * x + 2 * y + c
        out_ref[me] = v_ref[...]
        copies = []
        for k in range(1, N_DEV):
            px, py = _flip(x, y, k >> 1)
            pc = 1 - c if k & 1 else c
            cp = pltpu.make_async_remote_copy(
                src_ref=v_ref, dst_ref=out_ref.at[me],
                send_sem=send_sems.at[k - 1], recv_sem=recv_sems.at[k - 1],
                device_id=(px, py, pc), device_id_type=MESH)
            cp.start()
            copies.append((cp, 4 * px + 2 * py + pc))
        for k, (cp, peer) in enumerate(copies):
            pltpu.make_async_remote_copy(
                src_ref=v_ref, dst_ref=out_ref.at[peer],
                send_sem=send_sems.at[k], recv_sem=recv_sems.at[k],
                device_id=(x, y, c), device_id_type=MESH).wait_recv()
        for cp, _ in copies:
            cp.wait_send()

    return pl.pallas_call(
        _after(body, 1, deps), name=name,
        out_shape=jax.ShapeDtypeStruct((N_DEV, rows, LANES), F32),
        in_specs=[VMEM_SPEC] + [ANY] * len(deps), out_specs=VMEM_SPEC,
        scratch_shapes=[pltpu.SemaphoreType.DMA((N_DEV - 1,)),
                        pltpu.SemaphoreType.DMA((N_DEV - 1,))],
        compiler_params=pltpu.CompilerParams(vmem_limit_bytes=VMEM_LIMIT),
    )(v, *deps)


def _hbm(a):
    return pltpu.with_memory_space_constraint(a, pltpu.HBM)


def _hbm_like(arrays):
    return [pltpu.HBM(a.shape, a.dtype) for a in arrays]


def _remote(src, dst, send, recv, to):
    return pltpu.make_async_remote_copy(src_ref=src, dst_ref=dst, send_sem=send, recv_sem=recv,
                                        device_id=to, device_id_type=MESH)


def _gather_start(shards, lands, groups, after, name):
    n, ng = len(shards), len(groups)

    def body(*refs):
        ins, lnd = refs[:n], refs[n:2 * n]
        sems = refs[2 * n + len(after):2 * n + len(after) + 2 * ng]
        token = refs[-1]
        x, y, c = _position()
        me = 4 * x + 2 * y + c
        targets = [(x, y, 1 - c)] + [(px, py, c) for px, py in (_flip(x, y, k) for k in (1, 2, 3))]
        for g, members in enumerate(groups):
            for slot, i in enumerate(members):
                for k in (1, 2, 3, 0):
                    _remote(ins[i], lnd[i].at[me], sems[2 * g].at[4 * slot + k], sems[2 * g + 1].at[4 * slot + k],
                            targets[k]).start()
        token[...] = jnp.zeros_like(token)

    sem_shapes = []
    for members in groups:
        sem_shapes += [pltpu.SemaphoreType.DMA((4 * len(members),)), pltpu.SemaphoreType.DMA((4 * len(members),))]
    out = pl.pallas_call(
        body, name=name,
        out_shape=sem_shapes + _hbm_like(shards) + _hbm_like(lands) + [TOKEN],
        in_specs=[HBM] * (2 * n) + [ANY] * len(after),
        out_specs=[SEM] * (2 * ng) + [HBM] * (2 * n) + [VMEM_SPEC],
        input_output_aliases={i: 2 * ng + i for i in range(2 * n)},
        compiler_params=pltpu.CompilerParams(has_side_effects=EFFECT),
    )(*[_hbm(s) for s in shards], *[_hbm(l) for l in lands], *after)
    sems = [(out[2 * g], out[2 * g + 1]) for g in range(ng)]
    return sems, out[2 * ng:2 * ng + n], out[2 * ng + n:2 * ng + 2 * n], out[-1]


def _gather_forward(shards, lands, send, recv, after, name):
    m = len(shards)

    def body(*refs):
        ins, lnd = refs[:m], refs[m:2 * m]
        send_ref, recv_ref = refs[2 * m], refs[2 * m + 1]
        fsend, frecv = refs[2 * m + 2 + len(after)], refs[2 * m + 3 + len(after)]
        token = refs[-1]
        x, y, c = _position()
        me = (x, y, c)
        chips = [_flip(x, y, k) for k in (1, 2, 3)]
        for slot in range(m):
            for j, (px, py) in enumerate(chips):
                block = lnd[slot].at[4 * px + 2 * py + c]
                k = 4 * slot + 1 + j
                _remote(ins[slot], block, send_ref.at[k], recv_ref.at[k], me).wait_recv()
                _remote(block, block, fsend.at[3 * slot + j], frecv.at[3 * slot + j], (x, y, 1 - c)).start()
        for slot in range(m):
            theirs = lnd[slot].at[4 * x + 2 * y + (1 - c)]
            _remote(ins[slot], theirs, send_ref.at[4 * slot], recv_ref.at[4 * slot], me).wait_recv()
            for k in range(4 * slot, 4 * slot + 4):
                _remote(ins[slot], theirs, send_ref.at[k], recv_ref.at[k], me).wait_send()
        token[...] = jnp.zeros_like(token)

    out = pl.pallas_call(
        body, name=name,
        out_shape=[pltpu.SemaphoreType.DMA((3 * m,)), pltpu.SemaphoreType.DMA((3 * m,))]
        + _hbm_like(shards) + _hbm_like(lands) + [TOKEN],
        in_specs=[HBM] * (2 * m) + [SEM, SEM] + [ANY] * len(after),
        out_specs=[SEM, SEM] + [HBM] * (2 * m) + [VMEM_SPEC],
        input_output_aliases={i: 2 + i for i in range(2 * m)},
        compiler_params=pltpu.CompilerParams(has_side_effects=EFFECT),
    )(*shards, *lands, send, recv, *after)
    return out[0], out[1], out[2 + m:2 + 2 * m], out[-1]


def _gather_finish(lands, fsend, frecv, after, name):
    m = len(lands)

    def body(*refs):
        lnd = refs[:m]
        fsend_ref, frecv_ref = refs[m], refs[m + 1]
        x, y, c = _position()
        chips = [_flip(x, y, k) for k in (1, 2, 3)]
        for slot in range(m):
            for j, (px, py) in enumerate(chips):
                sent = lnd[slot].at[4 * px + 2 * py + c]
                came = lnd[slot].at[4 * px + 2 * py + (1 - c)]
                cp = _remote(sent, came, fsend_ref.at[3 * slot + j], frecv_ref.at[3 * slot + j], (x, y, c))
                cp.wait_recv()
                cp.wait_send()

    out = pl.pallas_call(
        body, name=name,
        out_shape=_hbm_like(lands),
        in_specs=[HBM] * m + [SEM, SEM] + [ANY] * len(after), out_specs=[HBM] * m,
        input_output_aliases={i: i for i in range(m)},
        compiler_params=pltpu.CompilerParams(has_side_effects=EFFECT),
    )(*lands, fsend, frecv, *after)
    return list(out)


def _pair_exchange(parts, name):
    n = len(parts)

    def body(*refs):
        ins, outs = refs[:n], refs[n:2 * n]
        send_sems, recv_sems = refs[2 * n:]
        x, y, c = _position()
        copies = []
        for i in range(n):
            cp = pltpu.make_async_remote_copy(
                src_ref=ins[i].at[:, pl.ds(1 - c, 1)], dst_ref=outs[i],
                send_sem=send_sems.at[i], recv_sem=recv_sems.at[i],
                device_id=(x, y, 1 - c), device_id_type=MESH)
            cp.start()
            copies.append(cp)
        for cp in copies:
            cp.wait_recv()
        for cp in copies:
            cp.wait_send()

    return pl.pallas_call(
        body, name=name,
        out_shape=[jax.ShapeDtypeStruct((p.shape[0], 1) + p.shape[2:], p.dtype) for p in parts],
        in_specs=[ANY] * n, out_specs=[ANY] * n,
        scratch_shapes=[pltpu.SemaphoreType.DMA((n,)), pltpu.SemaphoreType.DMA((n,))],
    )(*parts)


def _chip_start(sums, lands, name, flips=(1, 2, 3)):
    n, ns = len(sums), len(flips)

    def body(*refs):
        ins, lnd = refs[:n], refs[n:2 * n]
        send_ref, recv_ref = refs[2 * n], refs[2 * n + 1]
        token = refs[-1]
        x, y, c = _position()
        for i in range(n):
            for j, flip in enumerate(flips):
                px, py = _flip(x, y, flip)
                _remote(ins[i].at[j], lnd[i].at[j], send_ref.at[ns * i + j], recv_ref.at[ns * i + j],
                        (px, py, c)).start()
        token[...] = jnp.zeros_like(token)

    out = pl.pallas_call(
        body, name=name,
        out_shape=[pltpu.SemaphoreType.DMA((ns * n,)), pltpu.SemaphoreType.DMA((ns * n,))]
        + _hbm_like(sums) + _hbm_like(lands) + [TOKEN],
        in_specs=[HBM] * (2 * n), out_specs=[SEM, SEM] + [HBM] * (2 * n) + [VMEM_SPEC],
        input_output_aliases={i: 2 + i for i in range(2 * n)},
        compiler_params=pltpu.CompilerParams(has_side_effects=EFFECT),
    )(*[_hbm(s) for s in sums], *[_hbm(l) for l in lands])
    return out[0], out[1], out[2:2 + n], out[2 + n:2 + 2 * n], out[-1]


def _chip_wait(sums, lands, send, recv, after, name):
    n, ns = len(sums), sums[0].shape[0]

    def body(*refs):
        ins, lnd = refs[:n], refs[n:2 * n]
        send_ref, recv_ref = refs[2 * n], refs[2 * n + 1]
        x, y, c = _position()
        for i in range(n):
            for j in range(ns):
                cp = _remote(ins[i].at[j], lnd[i].at[j], send_ref.at[ns * i + j], recv_ref.at[ns * i + j], (x, y, c))
                cp.wait_recv()
                cp.wait_send()

    out = pl.pallas_call(
        body, name=name,
        out_shape=_hbm_like(sums) + _hbm_like(lands),
        in_specs=[HBM] * (2 * n) + [SEM, SEM] + [ANY] * len(after), out_specs=[HBM] * (2 * n),
        input_output_aliases={i: i for i in range(2 * n)},
        compiler_params=pltpu.CompilerParams(has_side_effects=EFFECT),
    )(*sums, *lands, send, recv, *after)
    return list(out[n:])


def _pair_sum(idx, part, got, name, nslots=3):
    _, _, rows, cols = part.shape
    tr = _tile(rows, 512)

    def body(idx_ref, p_ref, q_ref, o_ref):
        o_ref[...] = (p_ref[...].astype(F32) + q_ref[...].astype(F32)).astype(o_ref.dtype)

    grid_spec = pltpu.PrefetchScalarGridSpec(
        num_scalar_prefetch=1, grid=(nslots, rows // tr),
        in_specs=[pl.BlockSpec((None, None, tr, cols), lambda j, r, idx: (idx[j], idx[4], r, 0)),
                  pl.BlockSpec((None, None, tr, cols), lambda j, r, idx: (idx[j], 0, r, 0))],
        out_specs=pl.BlockSpec((None, tr, cols), lambda j, r, idx: (j, r, 0)))
    return pl.pallas_call(
        body, name=name, grid_spec=grid_spec,
        out_shape=jax.ShapeDtypeStruct((nslots, rows, cols), part.dtype),
        compiler_params=_params("arbitrary", "arbitrary"),
    )(idx, part, got)


def _mm_proj(h, wg, groups, name):
    s, k = h.shape
    nchunk, _, n = wg.shape
    e = nchunk * n // groups
    tn = _tile(min(n, e), 512)

    def body(h_ref, w_ref, o_ref):
        o_ref[...] = jnp.dot(h_ref[...], w_ref[...], preferred_element_type=F32).astype(o_ref.dtype)

    return pl.pallas_call(
        body, name=name, grid=(nchunk * n // tn,),
        in_specs=[pl.BlockSpec((s, k), lambda j: (0, 0)),
                  pl.BlockSpec((None, k, tn), lambda j: ((j * tn) // n, 0, ((j * tn) % n) // tn))],
        out_specs=pl.BlockSpec((None, s, tn), lambda j: ((j * tn) // e, 0, ((j * tn) % e) // tn)),
        out_shape=jax.ShapeDtypeStruct((groups, s, e), BF16),
        compiler_params=_params("arbitrary"),
    )(h, wg)


def _mm_out(yb, w, x, gate, name):
    s, k = yb.shape
    d = w.shape[1]
    tn = _tile(d, 512)
    tk = _tile(k, 1024)
    nk = k // tk

    def body(a_ref, w_ref, x_ref, g_ref, xo_ref, y_ref, acc_ref):
        kk = pl.program_id(1)

        @pl.when(kk == 0)
        def _():
            acc_ref[...] = jnp.zeros_like(acc_ref)

        acc_ref[...] += jnp.dot(a_ref[...], w_ref[...], preferred_element_type=F32)

        @pl.when(kk == nk - 1)
        def _():
            y = acc_ref[...]
            y_ref[...] = y.astype(y_ref.dtype)
            xo_ref[...] = x_ref[...] + g_ref[...] * y

    return pl.pallas_call(
        body, name=name, grid=(d // tn, nk),
        in_specs=[pl.BlockSpec((s, tk), lambda j, kk: (0, kk)),
                  pl.BlockSpec((tk, tn), lambda j, kk: (kk, j)),
                  pl.BlockSpec((s, tn), lambda j, kk: (0, j)),
                  pl.BlockSpec((1, tn), lambda j, kk: (0, j))],
        out_specs=[pl.BlockSpec((s, tn), lambda j, kk: (0, j)),
                   pl.BlockSpec((s, tn), lambda j, kk: (0, j))],
        out_shape=[jax.ShapeDtypeStruct((s, d), F32), jax.ShapeDtypeStruct((s, d), BF16)],
        scratch_shapes=[pltpu.VMEM((s, tn), F32)],
        compiler_params=_params("arbitrary", "arbitrary"),
    )(yb, w, x, gate)


def _mm_nt(a3, w3, out_dtype, name, deps=()):
    g, s, ea = a3.shape
    cw, n, nw = w3.shape
    total = g * ea
    assert total == cw * nw
    tk = _tile(min(ea, nw), 1024)
    tn = _tile(n, 1024)
    nk = total // tk

    def body(a_ref, w_ref, o_ref, acc_ref):
        kk = pl.program_id(1)

        @pl.when(kk == 0)
        def _():
            acc_ref[...] = jnp.zeros_like(acc_ref)

        acc_ref[...] += lax.dot_general(a_ref[...], w_ref[...], (((1,), (1,)), ((), ())),
                                        preferred_element_type=F32)

        @pl.when(kk == nk - 1)
        def _():
            o_ref[...] = acc_ref[...].astype(o_ref.dtype)

    return pl.pallas_call(
        _after(body, 2, deps), name=name, grid=(n // tn, nk),
        in_specs=[pl.BlockSpec((None, s, tk), lambda j, kk: ((kk * tk) // ea, 0, ((kk * tk) % ea) // tk)),
                  pl.BlockSpec((None, tn, tk), lambda j, kk: ((kk * tk) // nw, j, ((kk * tk) % nw) // tk))]
        + [ANY] * len(deps),
        out_specs=pl.BlockSpec((s, tn), lambda j, kk: (0, j)),
        out_shape=jax.ShapeDtypeStruct((s, n), out_dtype),
        scratch_shapes=[pltpu.VMEM((s, tn), F32)],
        compiler_params=_params("arbitrary", "arbitrary"),
    )(a3, w3, *deps)


def _mm_tn(a, b3, nchunk, name, deps=()):
    s, ka = a.shape
    g, _, eb = b3.shape
    n = g * eb // nchunk
    tm = _tile(ka, 1024)
    tn = _tile(min(n, eb), 1024)

    def body(a_ref, b_ref, o_ref, at_ref):
        @pl.when(pl.program_id(1) == 0)
        def _():
            at_ref[...] = a_ref[...].astype(F32).T.astype(at_ref.dtype)

        o_ref[...] = jnp.dot(at_ref[...], b_ref[...], preferred_element_type=F32).astype(o_ref.dtype)

    return pl.pallas_call(
        _after(body, 2, deps), name=name, grid=(ka // tm, g * eb // tn),
        in_specs=[pl.BlockSpec((s, tm), lambda i, j: (0, i)),
                  pl.BlockSpec((None, s, tn), lambda i, j: ((j * tn) // eb, 0, ((j * tn) % eb) // tn))]
        + [ANY] * len(deps),
        out_specs=pl.BlockSpec((None, tm, tn), lambda i, j: ((j * tn) // n, i, ((j * tn) % n) // tn)),
        out_shape=jax.ShapeDtypeStruct((nchunk, ka, n), BF16),
        scratch_shapes=[pltpu.VMEM((tm, s), BF16)],
        compiler_params=_params("arbitrary", "arbitrary"),
    )(a, b3, *deps)


def _mm_tn_group(a, b3, idx, pos, nchunk, name, deps=()):
    s, ka = a.shape
    _, _, eb = b3.shape
    n = eb // nchunk
    tm = _tile(ka, 1024)
    tn = _tile(n, 1024)
    nd = len(deps)

    def body(idx_ref, a_ref, b_ref, *rest):
        o_ref, at_ref = rest[nd:]

        @pl.when(pl.program_id(1) == 0)
        def _():
            at_ref[...] = a_ref[...].astype(F32).T.astype(at_ref.dtype)

        o_ref[...] = jnp.dot(at_ref[...], b_ref[...], preferred_element_type=F32).astype(o_ref.dtype)

    grid_spec = pltpu.PrefetchScalarGridSpec(
        num_scalar_prefetch=1, grid=(ka // tm, eb // tn),
        in_specs=[pl.BlockSpec((s, tm), lambda i, j, idx: (0, i)),
                  pl.BlockSpec((None, s, tn), lambda i, j, idx: (idx[pos], 0, j))] + [ANY] * nd,
        out_specs=pl.BlockSpec((None, tm, tn), lambda i, j, idx: ((j * tn) // n, i, ((j * tn) % n) // tn)),
        scratch_shapes=[pltpu.VMEM((tm, s), BF16)])
    return pl.pallas_call(
        body, name=name, grid_spec=grid_spec,
        out_shape=jax.ShapeDtypeStruct((nchunk, ka, n), BF16),
        compiler_params=_params("arbitrary", "arbitrary"),
    )(idx, a, b3, *deps)


def _sigmoid(z):
    return jax.nn.sigmoid(z)


def _shift_down(v, k, fill=0.0):
    if k == 0:
        return v
    row = lax.broadcasted_iota(jnp.int32, v.shape, 0)
    return jnp.where(row >= k, pltpu.roll(v, k, 0), fill)


def _shift_up(v, k, fill=0.0):
    if k == 0:
        return v
    s = v.shape[0]
    row = lax.broadcasted_iota(jnp.int32, v.shape, 0)
    return jnp.where(row < s - k, pltpu.roll(v, s - k, 0), fill)


def _scan(a, b, shift):
    s = a.shape[0]
    k = 1
    while k < s:
        b = a * shift(b, k, 0.0) + b
        if 2 * k < s:
            a = a * shift(a, k, 1.0)
        k *= 2
    return b


def _norm_mod(x, g, scale, shift, name, deps=()):
    s, d = x.shape
    ts = _tile(s, 256)

    def body(x_ref, g_ref, sc_ref, sh_ref, h_ref):
        xv = x_ref[...]
        rstd = lax.rsqrt(jnp.mean(xv * xv, axis=-1, keepdims=True) + EPS)
        nrm = xv * rstd * g_ref[...]
        h_ref[...] = (nrm * (1.0 + sc_ref[...]) + sh_ref[...]).astype(h_ref.dtype)

    vec = pl.BlockSpec((1, d), lambda i: (0, 0))
    return pl.pallas_call(
        _after(body, 4, deps), name=name, grid=(s // ts,),
        in_specs=[pl.BlockSpec((ts, d), lambda i: (i, 0)), vec, vec, vec] + [ANY] * len(deps),
        out_specs=pl.BlockSpec((ts, d), lambda i: (i, 0)),
        out_shape=jax.ShapeDtypeStruct((s, d), BF16),
        compiler_params=_params("arbitrary"),
    )(x, g, scale, shift, *deps)


def _norm_mod_bwd(x, dh, dx_res, g, scale, name):
    s, d = x.shape
    ts = _tile(s, 256)

    def body(x_ref, dh_ref, dr_ref, g_ref, sc_ref, dx_ref, dsc_ref, dsh_ref, dg_ref):
        @pl.when(pl.program_id(0) == 0)
        def _():
            dsc_ref[...] = jnp.zeros_like(dsc_ref)
            dsh_ref[...] = jnp.zeros_like(dsh_ref)
            dg_ref[...] = jnp.zeros_like(dg_ref)

        xv = x_ref[...]
        dh_v = dh_ref[...].astype(F32)
        gv = g_ref[...]
        rstd = lax.rsqrt(jnp.mean(xv * xv, axis=-1, keepdims=True) + EPS)
        xhat = xv * rstd
        dsc_ref[...] += jnp.sum(dh_v * xhat * gv, axis=0, keepdims=True)
        dsh_ref[...] += jnp.sum(dh_v, axis=0, keepdims=True)
        dn = dh_v * (1.0 + sc_ref[...])
        dg_ref[...] += jnp.sum(dn * xhat, axis=0, keepdims=True)
        dxhat = dn * gv
        proj = jnp.mean(dxhat * xhat, axis=-1, keepdims=True)
        dx_ref[...] = dr_ref[...] + rstd * (dxhat - xhat * proj)

    row = pl.BlockSpec((ts, d), lambda i: (i, 0))
    vec = pl.BlockSpec((1, d), lambda i: (0, 0))
    return pl.pallas_call(
        body, name=name, grid=(s // ts,),
        in_specs=[row, row, row, vec, vec],
        out_specs=[row, vec, vec, vec],
        out_shape=[jax.ShapeDtypeStruct((s, d), F32)] + [jax.ShapeDtypeStruct((1, d), F32)] * 3,
        compiler_params=_params("arbitrary"),
    )(x, dh, dx_res, g, scale)


def _final_loss(x, g, target, name):
    s, d = x.shape
    ts = _tile(s, 256)

    def body(x_ref, g_ref, t_ref, dx_ref, loss_ref, dg_ref):
        @pl.when(pl.program_id(0) == 0)
        def _():
            loss_ref[...] = jnp.zeros_like(loss_ref)
            dg_ref[...] = jnp.zeros_like(dg_ref)

        xv = x_ref[...]
        gv = g_ref[...]
        rstd = lax.rsqrt(jnp.mean(xv * xv, axis=-1, keepdims=True) + EPS)
        xhat = xv * rstd
        err = xhat * gv - t_ref[...]
        loss_ref[...] += 0.5 * jnp.sum(jnp.mean(err * err, axis=-1, keepdims=True))
        dy = err * (1.0 / d)
        dg_ref[...] += jnp.sum(dy * xhat, axis=0, keepdims=True)
        dxhat = dy * gv
        proj = jnp.mean(dxhat * xhat, axis=-1, keepdims=True)
        dx_ref[...] = rstd * (dxhat - xhat * proj)

    row = pl.BlockSpec((ts, d), lambda i: (i, 0))
    vec = pl.BlockSpec((1, d), lambda i: (0, 0))
    return pl.pallas_call(
        body, name=name, grid=(s // ts,),
        in_specs=[row, vec, row],
        out_specs=[row, pl.BlockSpec((1, LANES), lambda i: (0, 0)), vec],
        out_shape=[jax.ShapeDtypeStruct((s, d), F32), jax.ShapeDtypeStruct((1, LANES), F32),
                   jax.ShapeDtypeStruct((1, d), F32)],
        compiler_params=_params("arbitrary"),
    )(x, g, target)


def _gate_bwd(dx, y, gate, name):
    s, d = dx.shape
    ts = _tile(s, 256)

    def body(dx_ref, y_ref, g_ref, dy_ref, dgate_ref):
        @pl.when(pl.program_id(0) == 0)
        def _():
            dgate_ref[...] = jnp.zeros_like(dgate_ref)

        dxv = dx_ref[...]
        dy_ref[...] = (dxv * g_ref[...]).astype(dy_ref.dtype)
        dgate_ref[...] += jnp.sum(dxv * y_ref[...].astype(F32), axis=0, keepdims=True)

    row = pl.BlockSpec((ts, d), lambda i: (i, 0))
    vec = pl.BlockSpec((1, d), lambda i: (0, 0))
    return pl.pallas_call(
        body, name=name, grid=(s // ts,),
        in_specs=[row, row, vec], out_specs=[row, vec],
        out_shape=[jax.ShapeDtypeStruct((s, d), BF16), jax.ShapeDtypeStruct((1, d), F32)],
        compiler_params=_params("arbitrary"),
    )(dx, y, gate)


def _conv(v, w_ref, width):
    out = w_ref[width - 1:width, :] * v
    for k in range(width - 1):
        out = out + w_ref[k:k + 1, :] * _shift_down(v, width - 1 - k)
    return out


def _sc_fwd(proj, conv_w, name, deps=()):
    _, s, e = proj.shape
    te = _tile(e, 256)
    width = conv_w.shape[0]

    def body(b_ref, c_ref, v_ref, g_ref, w_ref, o_ref):
        cv = c_ref[...].astype(F32) * v_ref[...].astype(F32)
        u = _conv(cv, w_ref, width)
        gv = g_ref[...].astype(F32)
        o_ref[...] = (b_ref[...].astype(F32) * u * (gv * _sigmoid(gv))).astype(o_ref.dtype)

    def part(q):
        return pl.BlockSpec((None, s, te), lambda j, q=q: (q, 0, j))

    return pl.pallas_call(
        _after(body, 5, deps), name=name, grid=(e // te,),
        in_specs=[part(0), part(1), part(2), part(3), pl.BlockSpec((width, te), lambda j: (0, j))]
        + [ANY] * len(deps),
        out_specs=pl.BlockSpec((s, te), lambda j: (0, j)),
        out_shape=jax.ShapeDtypeStruct((s, e), BF16),
        compiler_params=_params("arbitrary"),
    )(proj, proj, proj, proj, conv_w, *deps)


def _sc_bwd(proj, dyb, conv_w, name):
    _, s, e = proj.shape
    te = _tile(e, 256)
    width = conv_w.shape[0]

    def body(b_ref, c_ref, v_ref, g_ref, dy_ref, w_ref, dp_ref, vec_ref):
        bv = b_ref[...].astype(F32)
        cvl = c_ref[...].astype(F32)
        vv = v_ref[...].astype(F32)
        gv = g_ref[...].astype(F32)
        dyv = dy_ref[...].astype(F32)
        cv = cvl * vv
        u = _conv(cv, w_ref, width)
        sg = _sigmoid(gv)
        silu = gv * sg
        dp_ref[0] = (dyv * u * silu).astype(dp_ref.dtype)
        du = dyv * bv * silu
        dp_ref[3] = (dyv * bv * u * (sg * (1.0 + gv * (1.0 - sg)))).astype(dp_ref.dtype)
        dcv = w_ref[width - 1:width, :] * du
        vec_ref[...] = jnp.zeros_like(vec_ref)
        vec_ref[width - 1:width, :] = jnp.sum(du * cv, axis=0, keepdims=True)
        for k in range(width - 1):
            sh = width - 1 - k
            dcv = dcv + w_ref[k:k + 1, :] * _shift_up(du, sh)
            vec_ref[k:k + 1, :] = jnp.sum(du * _shift_down(cv, sh), axis=0, keepdims=True)
        dp_ref[1] = (dcv * vv).astype(dp_ref.dtype)
        dp_ref[2] = (dcv * cvl).astype(dp_ref.dtype)

    def part(q):
        return pl.BlockSpec((None, s, te), lambda j, q=q: (q, 0, j))

    return pl.pallas_call(
        body, name=name, grid=(e // te,),
        in_specs=[part(0), part(1), part(2), part(3), pl.BlockSpec((s, te), lambda j: (0, j)),
                  pl.BlockSpec((width, te), lambda j: (0, j))],
        out_specs=[pl.BlockSpec((4, s, te), lambda j: (0, 0, j)),
                   pl.BlockSpec((8, te), lambda j: (0, j))],
        out_shape=[jax.ShapeDtypeStruct((4, s, e), BF16), jax.ShapeDtypeStruct((8, e), F32)],
        compiler_params=_params("arbitrary"),
    )(proj, proj, proj, proj, dyb, conv_w)


def _lru_gates(v_pre, w_ref, cb_ref, wa_ref, ba_ref, wx_ref, bx_ref, lam_ref, width):
    v = _conv(v_pre, w_ref, width) + cb_ref[...]
    vb = v.astype(BF16)
    r = _sigmoid(jnp.dot(vb, wa_ref[...], preferred_element_type=F32) + ba_ref[...])
    i = _sigmoid(jnp.dot(vb, wx_ref[...], preferred_element_type=F32) + bx_ref[...])
    nl = -lam_ref[...]
    sp = jnp.maximum(nl, 0.0) + jnp.log1p(jnp.exp(-jnp.abs(nl)))
    log_a = (-RGLRU_C) * r * sp
    a = jnp.exp(log_a)
    one_minus_a2 = jnp.tanh(-log_a) * (1.0 + a * a)
    mult = jnp.sqrt(one_minus_a2)
    return v, vb, r, i, sp, a, mult


def _lru_specs(s, dh, heads, width):
    head_col = lambda q: pl.BlockSpec((None, s, dh), lambda h, q=q: (q, 0, h))
    vec = pl.BlockSpec((1, dh), lambda h: (0, h))
    mat = pl.BlockSpec((None, dh, dh), lambda h: (h, 0, 0))
    weights = [pl.BlockSpec((width, dh), lambda h: (0, h)), vec, mat, vec, mat, vec, vec]
    return head_col, weights


def _lru_fwd(proj, conv_w, conv_b, w_a, b_a, w_x, b_x, lam, name, deps=()):
    _, s, e = proj.shape
    heads, dh, _ = w_a.shape
    width = conv_w.shape[0]

    def body(v_ref, g_ref, w_ref, cb_ref, wa_ref, ba_ref, wx_ref, bx_ref, lam_ref, yb_ref, hs_ref):
        v, _, _, i, _, a, mult = _lru_gates(v_ref[...].astype(F32), w_ref, cb_ref, wa_ref, ba_ref,
                                           wx_ref, bx_ref, lam_ref, width)
        hs = _scan(a, mult * i * v, _shift_down)
        hs_ref[...] = hs
        gv = g_ref[...].astype(F32)
        yb_ref[...] = (hs * (gv * _sigmoid(gv))).astype(yb_ref.dtype)

    head_col, weights = _lru_specs(s, dh, heads, width)
    out = pl.BlockSpec((s, dh), lambda h: (0, h))
    return pl.pallas_call(
        _after(body, 9, deps), name=name, grid=(heads,),
        in_specs=[head_col(0), head_col(1)] + weights + [ANY] * len(deps),
        out_specs=[out, out],
        out_shape=[jax.ShapeDtypeStruct((s, e), BF16), jax.ShapeDtypeStruct((s, e), F32)],
        compiler_params=_params("arbitrary"),
    )(proj, proj, conv_w, conv_b, w_a, b_a, w_x, b_x, lam, *deps)


def _lru_bwd(proj, hs, dyb, conv_w, conv_b, w_a, b_a, w_x, b_x, lam, name):
    _, s, e = proj.shape
    heads, dh, _ = w_a.shape
    width = conv_w.shape[0]

    def body(v_ref, g_ref, hs_ref, dy_ref, w_ref, cb_ref, wa_ref, ba_ref, wx_ref, bx_ref, lam_ref,
             dp_ref, dwa_ref, dwx_ref, vec_ref):
        v_pre = v_ref[...].astype(F32)
        v, vb, r, i, sp, a, mult = _lru_gates(v_pre, w_ref, cb_ref, wa_ref, ba_ref, wx_ref, bx_ref,
                                              lam_ref, width)
        hs = hs_ref[...]
        gv = g_ref[...].astype(F32)
        dyv = dy_ref[...].astype(F32)
        sg = _sigmoid(gv)
        dp_ref[1] = (dyv * hs * (sg * (1.0 + gv * (1.0 - sg)))).astype(dp_ref.dtype)
        dhs = dyv * (gv * sg)
        d_h = _scan(_shift_up(a, 1), dhs, _shift_up)
        da = d_h * _shift_down(hs, 1)
        iv = i * v
        dlog_a = da * a - (d_h * iv) * (a * a) / mult
        di = d_h * mult * v
        dv = d_h * mult * i
        dzr = dlog_a * (-RGLRU_C) * sp * r * (1.0 - r)
        dzi = di * i * (1.0 - i)
        dsp = jnp.sum(dlog_a * r, axis=0, keepdims=True) * (-RGLRU_C)
        vec_ref[...] = jnp.zeros_like(vec_ref)
        vec_ref[0:1, :] = jnp.sum(dzr, axis=0, keepdims=True)
        vec_ref[1:2, :] = jnp.sum(dzi, axis=0, keepdims=True)
        vec_ref[2:3, :] = -dsp * _sigmoid(-lam_ref[...])
        dzr_b = dzr.astype(BF16)
        dzi_b = dzi.astype(BF16)
        vt = vb.astype(F32).T.astype(BF16)
        dwa_ref[...] = jnp.dot(vt, dzr_b, preferred_element_type=F32).astype(dwa_ref.dtype)
        dwx_ref[...] = jnp.dot(vt, dzi_b, preferred_element_type=F32).astype(dwx_ref.dtype)
        nt = (((1,), (1,)), ((), ()))
        dv = dv + lax.dot_general(dzr_b, wa_ref[...], nt, preferred_element_type=F32)
        dv = dv + lax.dot_general(dzi_b, wx_ref[...], nt, preferred_element_type=F32)
        vec_ref[3:4, :] = jnp.sum(dv, axis=0, keepdims=True)
        dvp = w_ref[width - 1:width, :] * dv
        vec_ref[4 + width - 1:4 + width, :] = jnp.sum(dv * v_pre, axis=0, keepdims=True)
        for k in range(width - 1):
            sh = width - 1 - k
            dvp = dvp + w_ref[k:k + 1, :] * _shift_up(dv, sh)
            vec_ref[4 + k:5 + k, :] = jnp.sum(dv * _shift_down(v_pre, sh), axis=0, keepdims=True)
        dp_ref[0] = dvp.astype(dp_ref.dtype)

    head_col, weights = _lru_specs(s, dh, heads, width)
    col = pl.BlockSpec((s, dh), lambda h: (0, h))
    mat = pl.BlockSpec((None, dh, dh), lambda h: (h, 0, 0))
    return pl.pallas_call(
        body, name=name, grid=(heads,),
        in_specs=[head_col(0), head_col(1), col, col] + weights,
        out_specs=[pl.BlockSpec((2, s, dh), lambda h: (0, 0, h)), mat, mat,
                   pl.BlockSpec((16, dh), lambda h: (0, h))],
        out_shape=[jax.ShapeDtypeStruct((2, s, e), BF16),
                   jax.ShapeDtypeStruct((heads, dh, dh), BF16),
                   jax.ShapeDtypeStruct((heads, dh, dh), BF16),
                   jax.ShapeDtypeStruct((16, e), F32)],
        compiler_params=_params("arbitrary"),
    )(proj, proj, hs, dyb, conv_w, conv_b, w_a, b_a, w_x, b_x, lam)


def _ada_mod(c_all, w, b, name):
    layers, d, f = w.shape
    nb = c_all.shape[0]

    def body(c_ref, w_ref, b_ref, o_ref):
        cv = c_ref[...]
        sc = cv * _sigmoid(cv)
        o_ref[...] = jnp.dot(sc, w_ref[...], preferred_element_type=F32,
                             precision=lax.Precision.HIGHEST) + b_ref[...]

    return pl.pallas_call(
        body, name=name, grid=(layers,),
        in_specs=[pl.BlockSpec((nb, d), lambda l: (0, 0)),
                  pl.BlockSpec((None, d, f), lambda l: (l, 0, 0)),
                  pl.BlockSpec((None, 1, f), lambda l: (l, 0, 0))],
        out_specs=pl.BlockSpec((None, nb, f), lambda l: (l, 0, 0)),
        out_shape=jax.ShapeDtypeStruct((layers, nb, f), F32),
        compiler_params=_params("arbitrary"),
    )(c_all, w, b)


def _ada_grad(c_all_t, dmod, name):
    d, nb = c_all_t.shape
    layers, _, f = dmod.shape

    def body(c_ref, dm_ref, o_ref):
        cv = c_ref[...]
        sc = cv * _sigmoid(cv)
        acc = sc[:, 0:1] * dm_ref[0:1, :]
        for k in range(1, nb):
            acc = acc + sc[:, k:k + 1] * dm_ref[k:k + 1, :]
        o_ref[...] = acc

    return pl.pallas_call(
        body, name=name, grid=(layers,),
        in_specs=[pl.BlockSpec((d, nb), lambda l: (0, 0)),
                  pl.BlockSpec((None, nb, f), lambda l: (l, 0, 0))],
        out_specs=pl.BlockSpec((None, d, f), lambda l: (l, 0, 0)),
        out_shape=jax.ShapeDtypeStruct((layers, d, f), F32),
        compiler_params=_params("arbitrary"),
    )(c_all_t, dmod)


def _device_sum(g, name):
    _, rows, _ = g.shape

    def body(g_ref, o_ref):
        acc = g_ref[0]
        for k in range(1, N_DEV):
            acc = acc + g_ref[k]
        o_ref[...] = acc

    return pl.pallas_call(
        body, name=name,
        in_specs=[VMEM_SPEC], out_specs=VMEM_SPEC,
        out_shape=jax.ShapeDtypeStruct((rows, LANES), F32),
        compiler_params=pltpu.CompilerParams(vmem_limit_bytes=VMEM_LIMIT),
    )(g)


def _adamw_math(w, g, m, v):
    m = ADAM_B1 * m + (1.0 - ADAM_B1) * g
    v = ADAM_B2 * v + (1.0 - ADAM_B2) * (g * g)
    m_hat = m / (1.0 - ADAM_B1 ** ADAM_STEP)
    v_hat = v / (1.0 - ADAM_B2 ** ADAM_STEP)
    delta = -ADAM_LR * (m_hat / (jnp.sqrt(v_hat) + ADAM_EPS) + ADAM_WD * w)
    return delta, m, v


def _adamw(w, g, m, v, name):
    rows, cols = w.shape
    tr = _tile(rows, 256)

    def body(w_ref, g_ref, m_ref, v_ref, d_ref, mo_ref, vo_ref):
        d_ref[...], mo_ref[...], vo_ref[...] = _adamw_math(w_ref[...], g_ref[...], m_ref[...], v_ref[...])

    blk = pl.BlockSpec((tr, cols), lambda i: (i, 0))
    return pl.pallas_call(
        body, name=name, grid=(rows // tr,),
        in_specs=[blk] * 4, out_specs=[blk] * 3,
        out_shape=[jax.ShapeDtypeStruct((rows, cols), F32)] * 3,
        compiler_params=_params("arbitrary"),
    )(w, g, m, v)


def _adamw_reduced(idx, w, m, v, part, got, recvs, name):
    rows, cols = w.shape
    tr = _tile(rows, 256)
    nr = len(recvs)

    def body(idx_ref, w_ref, m_ref, v_ref, p_ref, q_ref, *rest):
        g_ref, d_ref, mo_ref, vo_ref = rest[nr:]
        g = p_ref[...].astype(F32) + q_ref[...].astype(F32)
        for u_ref in rest[:nr]:
            for j in range(u_ref.shape[0]):
                g = g + u_ref[j].astype(F32)
        g_ref[...] = g
        d_ref[...], mo_ref[...], vo_ref[...] = _adamw_math(w_ref[...], g, m_ref[...], v_ref[...])

    blk = pl.BlockSpec((tr, cols), lambda i, idx: (i, 0))
    grid_spec = pltpu.PrefetchScalarGridSpec(
        num_scalar_prefetch=1, grid=(rows // tr,),
        in_specs=[blk, blk, blk,
                  pl.BlockSpec((None, None, tr, cols), lambda i, idx: (idx[3], idx[4], i, 0)),
                  pl.BlockSpec((None, None, tr, cols), lambda i, idx: (idx[3], 0, i, 0))]
        + [pl.BlockSpec((u.shape[0], tr, cols), lambda i, idx: (0, i, 0)) for u in recvs],
        out_specs=[blk] * 4)
    return pl.pallas_call(
        body, name=name, grid_spec=grid_spec,
        out_shape=[jax.ShapeDtypeStruct((rows, cols), F32)] * 4,
        compiler_params=_params("arbitrary"),
    )(idx, w, m, v, part, got, *recvs)


def _pack(vectors):
    flat = jnp.concatenate([v.reshape(-1).astype(F32) for v in vectors])
    pad = (-flat.shape[0]) % (8 * LANES)
    return jnp.pad(flat, (0, pad)).reshape(-1, LANES)


def _unpack(flat, shapes):
    out, off = [], 0
    for shp in shapes:
        size = math.prod(shp)
        out.append(flat[..., off:off + size].reshape(flat.shape[:-1] + tuple(shp)))
        off += size
    return out


def _my_slice(full, me, axis):
    size = full.shape[axis] // N_DEV
    return lax.dynamic_slice_in_dim(full, me * size, size, axis)


def kernel(x, c, norm_g, ada_w, ada_b, sc_w_in, sc_conv_w, sc_w_out, lru_w_in, lru_conv_w, lru_conv_b, lru_w_a, lru_b_a, lru_w_x, lru_b_x, lru_lambda, lru_w_out, final_g, loss_target, m_norm_g, m_ada_w, m_ada_b, m_sc_w_in, m_sc_conv_w, m_sc_w_out, m_lru_w_in, m_lru_conv_w, m_lru_conv_b, m_lru_w_a, m_lru_b_a, m_lru_w_x, m_lru_b_x, m_lru_lambda, m_lru_w_out, m_final_g, v_norm_g, v_ada_w, v_ada_b, v_sc_w_in, v_sc_conv_w, v_sc_w_out, v_lru_w_in, v_lru_conv_w, v_lru_conv_b, v_lru_w_a, v_lru_b_a, v_lru_w_x, v_lru_b_x, v_lru_lambda, v_lru_w_out, v_final_g):
    _, s, d = x.shape
    e = sc_w_out.shape[1] * N_DEV
    heads, dh_s, dh = lru_w_a.shape[1:]
    es = e // N_DEV
    f = ada_w.shape[2]
    mx, my, mc = _position()
    me = 4 * mx + 2 * my + mc
    chip = 2 * mx + my
    idx = jnp.stack([chip ^ 1, chip ^ 2, chip ^ 3, chip, mc]).astype(jnp.int32)

    x0 = x[0]
    target = loss_target[0]

    small_shapes = [(d,), (3, es), (4, es), (es,), (heads, dh_s), (heads, dh_s), (es,)]
    small = _small_gather(_pack([c, sc_conv_w, lru_conv_w, lru_conv_b, lru_b_a, lru_b_x, lru_lambda]),
                          "gather_small_weights").reshape(N_DEV, -1)
    c_all, cw3, cw4, cb, ba, bx, lam = _unpack(small, small_shapes)
    cw3 = cw3.transpose(1, 0, 2).reshape(3, e)
    cw4 = cw4.transpose(1, 0, 2).reshape(4, e)
    cb = cb.reshape(1, e)
    lam = lam.reshape(1, e)
    ba = ba.transpose(1, 0, 2).reshape(1, e)
    bx = bx.transpose(1, 0, 2).reshape(1, e)

    ada_b_mine = _my_slice(ada_b, me, 1).reshape(2, 1, f)
    mod_mine = _ada_mod(c_all, ada_w, ada_b_mine, "ada_mod")
    mod_all = _small_gather(_pack([mod_mine]), "gather_mod")

    shards = [sc_w_in[0].astype(BF16), sc_w_out[0].astype(BF16), lru_w_in[0].astype(BF16),
              lru_w_a[0].reshape(heads * dh_s, dh).astype(BF16),
              lru_w_x[0].reshape(heads * dh_s, dh).astype(BF16), lru_w_out[0].astype(BF16)]
    lands = [lax.dynamic_update_slice(lax.empty((N_DEV,) + sh.shape, BF16), sh[None], (me, 0, 0))
             for sh in shards]
    groups = [[0], [1], [2, 3, 4], [5]]
    sems, shards, lands, started = _gather_start(shards, lands, groups, [mod_all], "gather_start")

    def gathered(g, after_forward, name):
        members = groups[g]
        fsend, frecv, lnd, token = _gather_forward(
            [shards[i] for i in members], [lands[i] for i in members], sems[g][0], sems[g][1],
            after_forward, "gather_forward_" + name)
        return token, lambda after: _gather_finish(lnd, fsend, frecv, after, "gather_finish_" + name)

    mod_all = mod_all.reshape(N_DEV, -1)
    mod_all = mod_all[:, :2 * N_DEV * f].reshape(N_DEV, 2, N_DEV, f)
    mod_all = mod_all.transpose(1, 2, 0, 3).reshape(2, N_DEV, 3 * d)
    mod = lax.dynamic_index_in_dim(mod_all, me, 1, keepdims=False)
    shift = [mod[l:l + 1, 0:d] for l in range(2)]
    scale = [mod[l:l + 1, d:2 * d] for l in range(2)]
    gate = [mod[l:l + 1, 2 * d:3 * d] for l in range(2)]
    ng = [norm_g[l:l + 1] for l in range(2)]
    fg = final_g.reshape(1, d)

    h0 = _norm_mod(x0, ng[0], scale[0], shift[0], "norm_mod_0", deps=[started])
    tok, finish = gathered(0, [h0], "sc_w_in")
    wg_in0, = finish([tok])
    proj0 = _mm_proj(h0, wg_in0, 4, "mm_proj_0")
    tok, finish = gathered(1, [proj0], "sc_w_out")
    yb0 = _sc_fwd(proj0, cw3, "sc_fwd", deps=[tok])
    w_out0 = finish([yb0])[0].reshape(e, d)
    x1, y0 = _mm_out(yb0, w_out0, x0, gate[0], "mm_out_0")
    tok, finish = gathered(2, [x1], "lru_in")
    h1 = _norm_mod(x1, ng[1], scale[1], shift[1], "norm_mod_1", deps=[tok])
    wg_in1, wg_a, wg_x = finish([h1])
    w_a = wg_a.reshape(N_DEV, heads, dh_s, dh).transpose(1, 0, 2, 3).reshape(heads, dh, dh)
    w_x = wg_x.reshape(N_DEV, heads, dh_s, dh).transpose(1, 0, 2, 3).reshape(heads, dh, dh)
    proj1 = _mm_proj(h1, wg_in1, 2, "mm_proj_1")
    tok, finish = gathered(3, [proj1], "lru_w_out")
    yb1, hs = _lru_fwd(proj1, cw4, cb, w_a, ba, w_x, bx, lam, "lru_fwd", deps=[tok])
    w_out1 = finish([yb1])[0].reshape(e, d)
    x2, y1 = _mm_out(yb1, w_out1, x1, gate[1], "mm_out_1")
    dx2, loss_part, d_fg = _final_loss(x2, fg, target, "final_loss")

    def pieces(g, rows, cols):
        return g.reshape(4, 2, rows, cols)

    def by_rows(g):
        return g.reshape(heads, N_DEV, dh_s, dh).transpose(1, 0, 2, 3).reshape(N_DEV, heads * dh_s, dh)

    def scatter_start(parts, names, group):
        gots = _pair_exchange(parts, "pair_exchange_" + group)
        sums = [_pair_sum(idx, p, q, "pair_sum_" + nm) for p, q, nm in zip(parts, gots, names)]
        empties = [lax.empty(sm.shape, sm.dtype) for sm in sums]
        send, recv, sums, lnd, token = _chip_start(sums, empties, "chip_start_" + group)
        return dict(parts=parts, gots=gots, names=names, group=group, sums=sums, lands=lnd,
                    send=send, recv=recv), token

    big = {"sc_w_in": (sc_w_in, m_sc_w_in, v_sc_w_in), "sc_w_out": (sc_w_out, m_sc_w_out, v_sc_w_out),
           "lru_w_in": (lru_w_in, m_lru_w_in, v_lru_w_in), "lru_w_a": (lru_w_a, m_lru_w_a, v_lru_w_a),
           "lru_w_x": (lru_w_x, m_lru_w_x, v_lru_w_x), "lru_w_out": (lru_w_out, m_lru_w_out, v_lru_w_out)}
    big_res = {}

    def scatter_finish(rs, after):
        recvs = _chip_wait(rs["sums"], rs["lands"], rs["send"], rs["recv"], after, "chip_wait_" + rs["group"])
        done = []
        for p, q, u, nm in zip(rs["parts"], rs["gots"], recvs, rs["names"]):
            w, m, v = big[nm]
            shp2 = p.shape[2:]
            res = _adamw_reduced(idx, w.reshape(shp2), m.reshape(shp2), v.reshape(shp2), p, q, [u], "adamw_" + nm)
            big_res[nm] = [r.reshape(w.shape) for r in res]
            done.append(res[1])
        return done

    dy1, dgate1 = _gate_bwd(dx2, y1, gate[1], "gate_bwd_1")
    dw_out1 = _mm_tn(yb1, dy1[None], 1, "mm_dw_out_1")
    rs1, tok = scatter_start([pieces(dw_out1, es, d)], ["lru_w_out"], "lru_w_out")
    dyb1 = _mm_nt(dy1[None], w_out1[None], BF16, "mm_dyb_1", deps=[tok])
    dproj1, dw_a, dw_x, vecs1 = _lru_bwd(proj1, hs, dyb1, cw4, cb, w_a, ba, w_x, bx, lam, "lru_bwd")
    done = scatter_finish(rs1, [dproj1])
    dw_in1 = _mm_tn(h1, dproj1, N_DEV, "mm_dw_in_1", deps=done)
    rs2, tok = scatter_start([pieces(dw_in1, d, 2 * es), pieces(by_rows(dw_a), heads * dh_s, dh),
                              pieces(by_rows(dw_x), heads * dh_s, dh)],
                             ["lru_w_in", "lru_w_a", "lru_w_x"], "lru_in")
    dh1 = _mm_nt(dproj1, wg_in1, F32, "mm_dh_1", deps=[tok])
    dx1, dscale1, dshift1, dng1 = _norm_mod_bwd(x1, dh1, dx2, ng[1], scale[1], "norm_mod_bwd_1")
    dy0, dgate0 = _gate_bwd(dx1, y0, gate[0], "gate_bwd_0")
    dw_out0 = _mm_tn(yb0, dy0[None], 1, "mm_dw_out_0")
    rs3, tok = scatter_start([pieces(dw_out0, es, d)], ["sc_w_out"], "sc_w_out")
    dyb0 = _mm_nt(dy0[None], w_out0[None], BF16, "mm_dyb_0", deps=[tok])
    dproj0, vecs0 = _sc_bwd(proj0, dyb0, cw3, "sc_bwd")
    early_shapes = [(LANES,), (d,), (3 * d,), (d,), (d,), (8, e), (16, e)]
    early_all = _small_gather(
        _pack([loss_part, dgate0, jnp.concatenate([dshift1, dscale1, dgate1], axis=1), dng1, d_fg, vecs0, vecs1]),
        "gather_small_grads_early")
    done = scatter_finish(rs2, [early_all])
    idx_one = jnp.stack([jnp.zeros_like(mc)] * 4 + [mc]).astype(jnp.int32)
    sc_w_in_steps = []
    for j in (1, 2, 3, 0):
        part = _mm_tn_group(h0, dproj0, idx, (j - 1) % 4, 2, "mm_dw_in_0_%d" % j, deps=done)[None]
        got, = _pair_exchange([part], "pair_exchange_sc_w_in_%d" % j)
        if j:
            sm = _pair_sum(idx_one, part, got, "pair_sum_sc_w_in_%d" % j, nslots=1)
            send, recv, sums, lnd, tok = _chip_start([sm], [lax.empty(sm.shape, sm.dtype)],
                                                     "chip_start_sc_w_in_%d" % j, flips=(j,))
            sc_w_in_steps.append((sums, lnd, send, recv, j))
            done = [tok]
    dh0 = _mm_nt(dproj0, wg_in0, F32, "mm_dh_0", deps=[got])
    dx0, dscale0, dshift0, dng0 = _norm_mod_bwd(x0, dh0, dx1, ng[0], scale[0], "norm_mod_bwd_0")
    done = scatter_finish(rs3, [dx0])
    early_sum = _device_sum(early_all, "sum_small_grads_early").reshape(-1)
    late_shapes = [(d,), (d,), (d,)]
    late_all = _small_gather(_pack([dshift0, dscale0, dng0]), "gather_small_grads_late", deps=done)
    late_sum = _device_sum(late_all, "sum_small_grads_late").reshape(-1)

    loss_v, s_dgate0, s_dmod1, s_dng1, g_final_g, sum0, sum1 = _unpack(early_sum, early_shapes)
    s_dshift0, s_dscale0, s_dng0 = _unpack(late_sum, late_shapes)
    loss = loss_v[0]
    g_ada_b = jnp.stack([jnp.concatenate([s_dshift0, s_dscale0, s_dgate0]), s_dmod1])
    g_norm_g = jnp.stack([s_dng0, s_dng1])
    _, a_dgate0, a_dmod1 = _unpack(early_all.reshape(N_DEV, -1), early_shapes)[:3]
    a_dshift0, a_dscale0, _ = _unpack(late_all.reshape(N_DEV, -1), late_shapes)
    dmod_all = jnp.stack([jnp.concatenate([a_dshift0, a_dscale0, a_dgate0], axis=1), a_dmod1])
    dmod_cols = _my_slice(dmod_all, me, 2)
    g_ada_w = _ada_grad(c_all.T, dmod_cols, "ada_grad")

    g_sc_conv_w = _my_slice(sum0[0:3], me, 1)
    g_lru_b_a = _my_slice(sum1[0].reshape(heads, dh), me, 1)
    g_lru_b_x = _my_slice(sum1[1].reshape(heads, dh), me, 1)
    g_lru_lambda = _my_slice(sum1[2:3], me, 1)
    g_lru_conv_b = _my_slice(sum1[3:4], me, 1)
    g_lru_conv_w = _my_slice(sum1[4:8], me, 1)

    ada_res = _adamw(ada_w.reshape(2 * d, f), g_ada_w.reshape(2 * d, f), m_ada_w.reshape(2 * d, f),
                     v_ada_w.reshape(2 * d, f), "adamw_ada_w")
    ada_out = [g_ada_w] + [r.reshape(ada_w.shape) for r in ada_res]

    small_w = [norm_g, ada_b, final_g, sc_conv_w, lru_conv_w, lru_conv_b, lru_b_a, lru_b_x, lru_lambda]
    small_m = [m_norm_g, m_ada_b, m_final_g, m_sc_conv_w, m_lru_conv_w, m_lru_conv_b, m_lru_b_a, m_lru_b_x,
               m_lru_lambda]
    small_v = [v_norm_g, v_ada_b, v_final_g, v_sc_conv_w, v_lru_conv_w, v_lru_conv_b, v_lru_b_a, v_lru_b_x,
               v_lru_lambda]
    small_g = [g_norm_g, g_ada_b, g_final_g, g_sc_conv_w, g_lru_conv_w, g_lru_conv_b, g_lru_b_a, g_lru_b_x,
               g_lru_lambda]
    small_g = [g.reshape(w.shape) for g, w in zip(small_g, small_w)]
    shapes = [w.shape for w in small_w]
    packed = _adamw(_pack(small_w), _pack(small_g), _pack(small_m), _pack(small_v), "adamw_small")
    small_out = [small_g] + [_unpack(p.reshape(-1), shapes) for p in packed]

    after = [packed[0], ada_res[0]]
    recvs = []
    for sums, lnd, send, recv, j in sc_w_in_steps:
        recvs += _chip_wait(sums, lnd, send, recv, after, "chip_wait_sc_w_in_%d" % j)
        after = []
    shp2 = part.shape[2:]
    res = _adamw_reduced(idx_one, sc_w_in.reshape(shp2), m_sc_w_in.reshape(shp2), v_sc_w_in.reshape(shp2),
                         part, got, recvs, "adamw_sc_w_in")
    big_res["sc_w_in"] = [r.reshape(sc_w_in.shape) for r in res]
    big_out = [big_res[nm] for nm in ("sc_w_in", "sc_w_out", "lru_w_in", "lru_w_a", "lru_w_x", "lru_w_out")]

    def small(kind, i):
        return small_out[kind][i]

    def bigw(kind, i):
        return big_out[i][kind]

    outs = [loss, dx0[None]]
    for kind in range(4):
        outs += [small(kind, 0), ada_out[kind], small(kind, 1), bigw(kind, 0), small(kind, 3), bigw(kind, 1),
                 bigw(kind, 2), small(kind, 4), small(kind, 5), bigw(kind, 3), small(kind, 6), bigw(kind, 4),
                 small(kind, 7), small(kind, 8), bigw(kind, 5), small(kind, 2)]
    return tuple(outs)
```

```python
import math

import jax
import jax.numpy as jnp
from jax import lax
from jax.experimental import pallas as pl
from jax.experimental.pallas import tpu as pltpu

N_DEV = 8
LANES = 128
EPS = 1e-6
RGLRU_C = 8.0
ADAM_LR = 0.001
ADAM_B1 = 0.9
ADAM_B2 = 0.999
ADAM_EPS = 1e-08
ADAM_WD = 0.01
ADAM_STEP = 10
VMEM_LIMIT = 56 * 1024 * 1024
MESH = pl.DeviceIdType.MESH
F32 = jnp.float32
BF16 = jnp.bfloat16
ANY = pl.BlockSpec(memory_space=pl.ANY)
HBM = pl.BlockSpec(memory_space=pltpu.HBM)
SEM = pl.BlockSpec(memory_space=pltpu.SEMAPHORE)
VMEM_SPEC = pl.BlockSpec(memory_space=pltpu.VMEM)
EFFECT = pltpu.SideEffectType.DATAFLOW_SIDE_EFFECTING
TOKEN = jax.ShapeDtypeStruct((8, LANES), jnp.float32)


def _tile(n, pref):
    t = min(n, pref)
    assert n % t == 0, (n, pref)
    return t


def _params(*sem):
    return pltpu.CompilerParams(dimension_semantics=sem, vmem_limit_bytes=VMEM_LIMIT)


def _position():
    return lax.axis_index("x"), lax.axis_index("y"), lax.axis_index("c")


def _flip(x, y, k):
    return (1 - x if k & 2 else x), (1 - y if k & 1 else y)


def _after(body, n_in, deps):
    if not deps:
        return body

    def wrapped(*refs):
        return body(*refs[:n_in], *refs[n_in + len(deps):])

    return wrapped


def _small_gather(v, name, deps=()):
    rows = v.shape[0]

    def body(v_ref, out_ref, send_sems, recv_sems):
        x, y, c = _position()
        me = 4 * x + 2 * y + c
        out_ref[me] = v_ref[...]
        copies = []
        for k in range(1, N_DEV):
            px, py = _flip(x, y, k >> 1)
            pc = 1 - c if k & 1 else c
            cp = pltpu.make_async_remote_copy(
                src_ref=v_ref, dst_ref=out_ref.at[me],
                send_sem=send_sems.at[k - 1], recv_sem=recv_sems.at[k - 1],
                device_id=(px, py, pc), device_id_type=MESH)
            cp.start()
            copies.append((cp, 4 * px + 2 * py + pc))
        for k, (cp, peer) in enumerate(copies):
            pltpu.make_async_remote_copy(
                src_ref=v_ref, dst_ref=out_ref.at[peer],
                send_sem=send_sems.at[k], recv_sem=recv_sems.at[k],
                device_id=(x, y, c), device_id_type=MESH).wait_recv()
        for cp, _ in copies:
            cp.wait_send()

    return pl.pallas_call(
        _after(body, 1, deps), name=name,
        out_shape=jax.ShapeDtypeStruct((N_DEV, rows, LANES), F32),
        in_specs=[VMEM_SPEC] + [ANY] * len(deps), out_specs=VMEM_SPEC,
        scratch_shapes=[pltpu.SemaphoreType.DMA((N_DEV - 1,)),
                        pltpu.SemaphoreType.DMA((N_DEV - 1,))],
        compiler_params=pltpu.CompilerParams(vmem_limit_bytes=VMEM_LIMIT),
    )(v, *deps)


def _hbm(a):
    return pltpu.with_memory_space_constraint(a, pltpu.HBM)


def _hbm_like(arrays):
    return [pltpu.HBM(a.shape, a.dtype) for a in arrays]


def _remote(src, dst, send, recv, to):
    return pltpu.make_async_remote_copy(src_ref=src, dst_ref=dst, send_sem=send, recv_sem=recv,
                                        device_id=to, device_id_type=MESH)


def _gather_start(shards, lands, units, after, name):
    n, nu = len(shards), len(units)

    def body(*refs):
        ins, lnd = refs[:n], refs[n:2 * n]
        sems = refs[2 * n + len(after):2 * n + len(after) + 2 * nu]
        token = refs[-1]
        x, y, c = _position()
        me = 4 * x + 2 * y + c
        targets = [(x, y, 1 - c)] + [(px, py, c) for px, py in (_flip(x, y, k) for k in (1, 2, 3))]
        for u, (members, ks) in enumerate(units):
            for slot, i in enumerate(members):
                for ki, k in enumerate(ks):
                    at = len(ks) * slot + ki
                    _remote(ins[i], lnd[i].at[me], sems[2 * u].at[at], sems[2 * u + 1].at[at], targets[k]).start()
        token[...] = jnp.zeros_like(token)

    sem_shapes = []
    for members, ks in units:
        count = len(members) * len(ks)
        sem_shapes += [pltpu.SemaphoreType.DMA((count,)), pltpu.SemaphoreType.DMA((count,))]
    out = pl.pallas_call(
        body, name=name,
        out_shape=sem_shapes + _hbm_like(shards) + _hbm_like(lands) + [TOKEN],
        in_specs=[HBM] * (2 * n) + [ANY] * len(after),
        out_specs=[SEM] * (2 * nu) + [HBM] * (2 * n) + [VMEM_SPEC],
        input_output_aliases={i: 2 * nu + i for i in range(2 * n)},
        compiler_params=pltpu.CompilerParams(has_side_effects=EFFECT),
    )(*[_hbm(s) for s in shards], *[_hbm(l) for l in lands], *after)
    sems = [(out[2 * u], out[2 * u + 1]) for u in range(nu)]
    return sems, list(out[2 * nu:2 * nu + n]), list(out[2 * nu + n:2 * nu + 2 * n]), out[-1]


def _gather_forward(shards, lands, ks, send, recv, after, name):
    m = len(shards)
    hops = [k for k in ks if k]
    nsem = 2 if hops else 0

    def body(*refs):
        ins, lnd = refs[:m], refs[m:2 * m]
        send_ref, recv_ref = refs[2 * m], refs[2 * m + 1]
        outs = refs[2 * m + 2 + len(after):]
        token = refs[-1]
        x, y, c = _position()
        me = (x, y, c)
        for slot in range(m):
            for ki, k in enumerate(ks):
                at = len(ks) * slot + ki
                if k:
                    px, py = _flip(x, y, k)
                    block = lnd[slot].at[4 * px + 2 * py + c]
                else:
                    block = lnd[slot].at[4 * x + 2 * y + (1 - c)]
                arrival = _remote(ins[slot], block, send_ref.at[at], recv_ref.at[at], me)
                arrival.wait_recv()
                if k:
                    fat = len(hops) * slot + hops.index(k)
                    _remote(block, block, outs[0].at[fat], outs[1].at[fat], (x, y, 1 - c)).start()
                arrival.wait_send()
        token[...] = jnp.zeros_like(token)

    count = len(hops) * m
    sem_shapes = [pltpu.SemaphoreType.DMA((count,)), pltpu.SemaphoreType.DMA((count,))] if hops else []
    out = pl.pallas_call(
        body, name=name,
        out_shape=sem_shapes + _hbm_like(shards) + _hbm_like(lands) + [TOKEN],
        in_specs=[HBM] * (2 * m) + [SEM, SEM] + [ANY] * len(after),
        out_specs=[SEM] * nsem + [HBM] * (2 * m) + [VMEM_SPEC],
        input_output_aliases={i: nsem + i for i in range(2 * m)},
        compiler_params=pltpu.CompilerParams(has_side_effects=EFFECT),
    )(*shards, *lands, send, recv, *after)
    fwd = (out[0], out[1]) if hops else None
    return fwd, list(out[nsem:nsem + m]), list(out[nsem + m:nsem + 2 * m]), out[-1]


def _gather_finish(lands, ks, fwd, after, name):
    m = len(lands)
    hops = [k for k in ks if k]

    def body(*refs):
        lnd = refs[:m]
        fsend_ref, frecv_ref = refs[m], refs[m + 1]
        x, y, c = _position()
        for slot in range(m):
            for fi, k in enumerate(hops):
                px, py = _flip(x, y, k)
                sent = lnd[slot].at[4 * px + 2 * py + c]
                came = lnd[slot].at[4 * px + 2 * py + (1 - c)]
                fat = len(hops) * slot + fi
                cp = _remote(sent, came, fsend_ref.at[fat], frecv_ref.at[fat], (x, y, c))
                cp.wait_recv()
                cp.wait_send()

    out = pl.pallas_call(
        body, name=name,
        out_shape=_hbm_like(lands),
        in_specs=[HBM] * m + [SEM, SEM] + [ANY] * len(after), out_specs=[HBM] * m,
        input_output_aliases={i: i for i in range(m)},
        compiler_params=pltpu.CompilerParams(has_side_effects=EFFECT),
    )(*lands, fwd[0], fwd[1], *after)
    return list(out)


def _pair_exchange(parts, name):
    n = len(parts)

    def body(*refs):
        ins, outs = refs[:n], refs[n:2 * n]
        send_sems, recv_sems = refs[2 * n:]
        x, y, c = _position()
        copies = []
        for i in range(n):
            cp = pltpu.make_async_remote_copy(
                src_ref=ins[i].at[:, pl.ds(1 - c, 1)], dst_ref=outs[i],
                send_sem=send_sems.at[i], recv_sem=recv_sems.at[i],
                device_id=(x, y, 1 - c), device_id_type=MESH)
            cp.start()
            copies.append(cp)
        for cp in copies:
            cp.wait_recv()
        for cp in copies:
            cp.wait_send()

    return pl.pallas_call(
        body, name=name,
        out_shape=[jax.ShapeDtypeStruct((p.shape[0], 1) + p.shape[2:], p.dtype) for p in parts],
        in_specs=[ANY] * n, out_specs=[ANY] * n,
        scratch_shapes=[pltpu.SemaphoreType.DMA((n,)), pltpu.SemaphoreType.DMA((n,))],
    )(*parts)


def _chip_start(sums, lands, name, flips=(1, 2, 3)):
    n, ns = len(sums), len(flips)

    def body(*refs):
        ins, lnd = refs[:n], refs[n:2 * n]
        send_ref, recv_ref = refs[2 * n], refs[2 * n + 1]
        token = refs[-1]
        x, y, c = _position()
        for i in range(n):
            for j, flip in enumerate(flips):
                px, py = _flip(x, y, flip)
                _remote(ins[i].at[j], lnd[i].at[j], send_ref.at[ns * i + j], recv_ref.at[ns * i + j],
                        (px, py, c)).start()
        token[...] = jnp.zeros_like(token)

    out = pl.pallas_call(
        body, name=name,
        out_shape=[pltpu.SemaphoreType.DMA((ns * n,)), pltpu.SemaphoreType.DMA((ns * n,))]
        + _hbm_like(sums) + _hbm_like(lands) + [TOKEN],
        in_specs=[HBM] * (2 * n), out_specs=[SEM, SEM] + [HBM] * (2 * n) + [VMEM_SPEC],
        input_output_aliases={i: 2 + i for i in range(2 * n)},
        compiler_params=pltpu.CompilerParams(has_side_effects=EFFECT),
    )(*[_hbm(s) for s in sums], *[_hbm(l) for l in lands])
    return out[0], out[1], out[2:2 + n], out[2 + n:2 + 2 * n], out[-1]


def _chip_wait(sums, lands, send, recv, after, name):
    n, ns = len(sums), sums[0].shape[0]

    def body(*refs):
        ins, lnd = refs[:n], refs[n:2 * n]
        send_ref, recv_ref = refs[2 * n], refs[2 * n + 1]
        x, y, c = _position()
        for i in range(n):
            for j in range(ns):
                cp = _remote(ins[i].at[j], lnd[i].at[j], send_ref.at[ns * i + j], recv_ref.at[ns * i + j], (x, y, c))
                cp.wait_recv()
                cp.wait_send()

    out = pl.pallas_call(
        body, name=name,
        out_shape=_hbm_like(sums) + _hbm_like(lands),
        in_specs=[HBM] * (2 * n) + [SEM, SEM] + [ANY] * len(after), out_specs=[HBM] * (2 * n),
        input_output_aliases={i: i for i in range(2 * n)},
        compiler_params=pltpu.CompilerParams(has_side_effects=EFFECT),
    )(*sums, *lands, send, recv, *after)
    return list(out[n:])


def _pair_sum(idx, part, got, name, nslots=3):
    _, _, rows, cols = part.shape
    tr = _tile(rows, 512)

    def body(idx_ref, p_ref, q_ref, o_ref):
        o_ref[...] = (p_ref[...].astype(F32) + q_ref[...].astype(F32)).astype(o_ref.dtype)

    grid_spec = pltpu.PrefetchScalarGridSpec(
        num_scalar_prefetch=1, grid=(nslots, rows // tr),
        in_specs=[pl.BlockSpec((None, None, tr, cols), lambda j, r, idx: (idx[j], idx[4], r, 0)),
                  pl.BlockSpec((None, None, tr, cols), lambda j, r, idx: (idx[j], 0, r, 0))],
        out_specs=pl.BlockSpec((None, tr, cols), lambda j, r, idx: (j, r, 0)))
    return pl.pallas_call(
        body, name=name, grid_spec=grid_spec,
        out_shape=jax.ShapeDtypeStruct((nslots, rows, cols), part.dtype),
        compiler_params=_params("arbitrary", "arbitrary"),
    )(idx, part, got)


def _mm_proj(h, wg, groups, name):
    s, k = h.shape
    nchunk, _, n = wg.shape
    e = nchunk * n // groups
    tn = _tile(min(n, e), 512)

    def body(h_ref, w_ref, o_ref):
        o_ref[...] = jnp.dot(h_ref[...], w_ref[...], preferred_element_type=F32).astype(o_ref.dtype)

    return pl.pallas_call(
        body, name=name, grid=(nchunk * n // tn,),
        in_specs=[pl.BlockSpec((s, k), lambda j: (0, 0)),
                  pl.BlockSpec((None, k, tn), lambda j: ((j * tn) // n, 0, ((j * tn) % n) // tn))],
        out_specs=pl.BlockSpec((None, s, tn), lambda j: ((j * tn) // e, 0, ((j * tn) % e) // tn)),
        out_shape=jax.ShapeDtypeStruct((groups, s, e), BF16),
        compiler_params=_params("arbitrary"),
    )(h, wg)


def _mm_proj_group(h, wg, idx, pos, prev, name, deps=()):
    s, k = h.shape
    nchunk, _, n = wg.shape
    groups, _, e = prev.shape
    per = nchunk // groups
    assert per * n == e
    tn = _tile(n, 512)
    nd = len(deps)

    def body(idx_ref, h_ref, w_ref, prev_ref, *rest):
        o_ref = rest[nd]
        o_ref[...] = jnp.dot(h_ref[...], w_ref[...], preferred_element_type=F32).astype(o_ref.dtype)

    grid_spec = pltpu.PrefetchScalarGridSpec(
        num_scalar_prefetch=1, grid=(e // tn,),
        in_specs=[pl.BlockSpec((s, k), lambda j, idx: (0, 0)),
                  pl.BlockSpec((None, k, tn), lambda j, idx: (per * idx[pos] + (j * tn) // n, 0, ((j * tn) % n) // tn)),
                  ANY] + [ANY] * nd,
        out_specs=pl.BlockSpec((None, s, tn), lambda j, idx: (idx[pos], 0, j)))
    return pl.pallas_call(
        body, name=name, grid_spec=grid_spec,
        out_shape=jax.ShapeDtypeStruct(prev.shape, prev.dtype),
        input_output_aliases={3: 0},
        compiler_params=_params("arbitrary"),
    )(idx, h, wg, prev, *deps)


def _mm_out(yb, w, x, gate, name):
    s, k = yb.shape
    d = w.shape[1]
    tn = _tile(d, 512)
    tk = _tile(k, 1024)
    nk = k // tk

    def body(a_ref, w_ref, x_ref, g_ref, xo_ref, y_ref, acc_ref):
        kk = pl.program_id(1)

        @pl.when(kk == 0)
        def _():
            acc_ref[...] = jnp.zeros_like(acc_ref)

        acc_ref[...] += jnp.dot(a_ref[...], w_ref[...], preferred_element_type=F32)

        @pl.when(kk == nk - 1)
        def _():
            y = acc_ref[...]
            y_ref[...] = y.astype(y_ref.dtype)
            xo_ref[...] = x_ref[...] + g_ref[...] * y

    return pl.pallas_call(
        body, name=name, grid=(d // tn, nk),
        in_specs=[pl.BlockSpec((s, tk), lambda j, kk: (0, kk)),
                  pl.BlockSpec((tk, tn), lambda j, kk: (kk, j)),
                  pl.BlockSpec((s, tn), lambda j, kk: (0, j)),
                  pl.BlockSpec((1, tn), lambda j, kk: (0, j))],
        out_specs=[pl.BlockSpec((s, tn), lambda j, kk: (0, j)),
                   pl.BlockSpec((s, tn), lambda j, kk: (0, j))],
        out_shape=[jax.ShapeDtypeStruct((s, d), F32), jax.ShapeDtypeStruct((s, d), BF16)],
        scratch_shapes=[pltpu.VMEM((s, tn), F32)],
        compiler_params=_params("arbitrary", "arbitrary"),
    )(yb, w, x, gate)


def _mm_nt(a3, w3, out_dtype, name, deps=()):
    g, s, ea = a3.shape
    cw, n, nw = w3.shape
    total = g * ea
    assert total == cw * nw
    tk = _tile(min(ea, nw), 1024)
    tn = _tile(n, 1024)
    nk = total // tk

    def body(a_ref, w_ref, o_ref, acc_ref):
        kk = pl.program_id(1)

        @pl.when(kk == 0)
        def _():
            acc_ref[...] = jnp.zeros_like(acc_ref)

        acc_ref[...] += lax.dot_general(a_ref[...], w_ref[...], (((1,), (1,)), ((), ())),
                                        preferred_element_type=F32)

        @pl.when(kk == nk - 1)
        def _():
            o_ref[...] = acc_ref[...].astype(o_ref.dtype)

    return pl.pallas_call(
        _after(body, 2, deps), name=name, grid=(n // tn, nk),
        in_specs=[pl.BlockSpec((None, s, tk), lambda j, kk: ((kk * tk) // ea, 0, ((kk * tk) % ea) // tk)),
                  pl.BlockSpec((None, tn, tk), lambda j, kk: ((kk * tk) // nw, j, ((kk * tk) % nw) // tk))]
        + [ANY] * len(deps),
        out_specs=pl.BlockSpec((s, tn), lambda j, kk: (0, j)),
        out_shape=jax.ShapeDtypeStruct((s, n), out_dtype),
        scratch_shapes=[pltpu.VMEM((s, tn), F32)],
        compiler_params=_params("arbitrary", "arbitrary"),
    )(a3, w3, *deps)


def _mm_tn(a, b3, nchunk, name, deps=()):
    s, ka = a.shape
    g, _, eb = b3.shape
    n = g * eb // nchunk
    tm = _tile(ka, 1024)
    tn = _tile(min(n, eb), 1024)

    def body(a_ref, b_ref, o_ref, at_ref):
        @pl.when(pl.program_id(1) == 0)
        def _():
            at_ref[...] = a_ref[...].astype(F32).T.astype(at_ref.dtype)

        o_ref[...] = jnp.dot(at_ref[...], b_ref[...], preferred_element_type=F32).astype(o_ref.dtype)

    return pl.pallas_call(
        _after(body, 2, deps), name=name, grid=(ka // tm, g * eb // tn),
        in_specs=[pl.BlockSpec((s, tm), lambda i, j: (0, i)),
                  pl.BlockSpec((None, s, tn), lambda i, j: ((j * tn) // eb, 0, ((j * tn) % eb) // tn))]
        + [ANY] * len(deps),
        out_specs=pl.BlockSpec((None, tm, tn), lambda i, j: ((j * tn) // n, i, ((j * tn) % n) // tn)),
        out_shape=jax.ShapeDtypeStruct((nchunk, ka, n), BF16),
        scratch_shapes=[pltpu.VMEM((tm, s), BF16)],
        compiler_params=_params("arbitrary", "arbitrary"),
    )(a, b3, *deps)


def _mm_tn_group(a, b3, idx, pos, nchunk, name, deps=()):
    s, ka = a.shape
    _, _, eb = b3.shape
    n = eb // nchunk
    tm = _tile(ka, 1024)
    tn = _tile(n, 1024)
    nd = len(deps)

    def body(idx_ref, a_ref, b_ref, *rest):
        o_ref, at_ref = rest[nd:]

        @pl.when(pl.program_id(1) == 0)
        def _():
            at_ref[...] = a_ref[...].astype(F32).T.astype(at_ref.dtype)

        o_ref[...] = jnp.dot(at_ref[...], b_ref[...], preferred_element_type=F32).astype(o_ref.dtype)

    grid_spec = pltpu.PrefetchScalarGridSpec(
        num_scalar_prefetch=1, grid=(ka // tm, eb // tn),
        in_specs=[pl.BlockSpec((s, tm), lambda i, j, idx: (0, i)),
                  pl.BlockSpec((None, s, tn), lambda i, j, idx: (idx[pos], 0, j))] + [ANY] * nd,
        out_specs=pl.BlockSpec((None, tm, tn), lambda i, j, idx: ((j * tn) // n, i, ((j * tn) % n) // tn)),
        scratch_shapes=[pltpu.VMEM((tm, s), BF16)])
    return pl.pallas_call(
        body, name=name, grid_spec=grid_spec,
        out_shape=jax.ShapeDtypeStruct((nchunk, ka, n), BF16),
        compiler_params=_params("arbitrary", "arbitrary"),
    )(idx, a, b3, *deps)


def _sigmoid(z):
    return jax.nn.sigmoid(z)


def _shift_down(v, k, fill=0.0):
    if k == 0:
        return v
    row = lax.broadcasted_iota(jnp.int32, v.shape, 0)
    return jnp.where(row >= k, pltpu.roll(v, k, 0), fill)


def _shift_up(v, k, fill=0.0):
    if k == 0:
        return v
    s = v.shape[0]
    row = lax.broadcasted_iota(jnp.int32, v.shape, 0)
    return jnp.where(row < s - k, pltpu.roll(v, s - k, 0), fill)


def _scan(a, b, shift):
    s = a.shape[0]
    k = 1
    while k < s:
        b = a * shift(b, k, 0.0) + b
        if 2 * k < s:
            a = a * shift(a, k, 1.0)
        k *= 2
    return b


def _norm_mod(x, g, scale, shift, name, deps=()):
    s, d = x.shape
    ts = _tile(s, 256)

    def body(x_ref, g_ref, sc_ref, sh_ref, h_ref):
        xv = x_ref[...]
        rstd = lax.rsqrt(jnp.mean(xv * xv, axis=-1, keepdims=True) + EPS)
        nrm = xv * rstd * g_ref[...]
        h_ref[...] = (nrm * (1.0 + sc_ref[...]) + sh_ref[...]).astype(h_ref.dtype)

    vec = pl.BlockSpec((1, d), lambda i: (0, 0))
    return pl.pallas_call(
        _after(body, 4, deps), name=name, grid=(s // ts,),
        in_specs=[pl.BlockSpec((ts, d), lambda i: (i, 0)), vec, vec, vec] + [ANY] * len(deps),
        out_specs=pl.BlockSpec((ts, d), lambda i: (i, 0)),
        out_shape=jax.ShapeDtypeStruct((s, d), BF16),
        compiler_params=_params("arbitrary"),
    )(x, g, scale, shift, *deps)


def _norm_mod_bwd(x, dh, dx_res, g, scale, name):
    s, d = x.shape
    ts = _tile(s, 256)

    def body(x_ref, dh_ref, dr_ref, g_ref, sc_ref, dx_ref, dsc_ref, dsh_ref, dg_ref):
        @pl.when(pl.program_id(0) == 0)
        def _():
            dsc_ref[...] = jnp.zeros_like(dsc_ref)
            dsh_ref[...] = jnp.zeros_like(dsh_ref)
            dg_ref[...] = jnp.zeros_like(dg_ref)

        xv = x_ref[...]
        dh_v = dh_ref[...].astype(F32)
        gv = g_ref[...]
        rstd = lax.rsqrt(jnp.mean(xv * xv, axis=-1, keepdims=True) + EPS)
        xhat = xv * rstd
        dsc_ref[...] += jnp.sum(dh_v * xhat * gv, axis=0, keepdims=True)
        dsh_ref[...] += jnp.sum(dh_v, axis=0, keepdims=True)
        dn = dh_v * (1.0 + sc_ref[...])
        dg_ref[...] += jnp.sum(dn * xhat, axis=0, keepdims=True)
        dxhat = dn * gv
        proj = jnp.mean(dxhat * xhat, axis=-1, keepdims=True)
        dx_ref[...] = dr_ref[...] + rstd * (dxhat - xhat * proj)

    row = pl.BlockSpec((ts, d), lambda i: (i, 0))
    vec = pl.BlockSpec((1, d), lambda i: (0, 0))
    return pl.pallas_call(
        body, name=name, grid=(s // ts,),
        in_specs=[row, row, row, vec, vec],
        out_specs=[row, vec, vec, vec],
        out_shape=[jax.ShapeDtypeStruct((s, d), F32)] + [jax.ShapeDtypeStruct((1, d), F32)] * 3,
        compiler_params=_params("arbitrary"),
    )(x, dh, dx_res, g, scale)


def _final_loss(x, g, target, name):
    s, d = x.shape
    ts = _tile(s, 256)

    def body(x_ref, g_ref, t_ref, dx_ref, loss_ref, dg_ref):
        @pl.when(pl.program_id(0) == 0)
        def _():
            loss_ref[...] = jnp.zeros_like(loss_ref)
            dg_ref[...] = jnp.zeros_like(dg_ref)

        xv = x_ref[...]
        gv = g_ref[...]
        rstd = lax.rsqrt(jnp.mean(xv * xv, axis=-1, keepdims=True) + EPS)
        xhat = xv * rstd
        err = xhat * gv - t_ref[...]
        loss_ref[...] += 0.5 * jnp.sum(jnp.mean(err * err, axis=-1, keepdims=True))
        dy = err * (1.0 / d)
        dg_ref[...] += jnp.sum(dy * xhat, axis=0, keepdims=True)
        dxhat = dy * gv
        proj = jnp.mean(dxhat * xhat, axis=-1, keepdims=True)
        dx_ref[...] = rstd * (dxhat - xhat * proj)

    row = pl.BlockSpec((ts, d), lambda i: (i, 0))
    vec = pl.BlockSpec((1, d), lambda i: (0, 0))
    return pl.pallas_call(
        body, name=name, grid=(s // ts,),
        in_specs=[row, vec, row],
        out_specs=[row, pl.BlockSpec((1, LANES), lambda i: (0, 0)), vec],
        out_shape=[jax.ShapeDtypeStruct((s, d), F32), jax.ShapeDtypeStruct((1, LANES), F32),
                   jax.ShapeDtypeStruct((1, d), F32)],
        compiler_params=_params("arbitrary"),
    )(x, g, target)


def _gate_bwd(dx, y, gate, name):
    s, d = dx.shape
    ts = _tile(s, 256)

    def body(dx_ref, y_ref, g_ref, dy_ref, dgate_ref):
        @pl.when(pl.program_id(0) == 0)
        def _():
            dgate_ref[...] = jnp.zeros_like(dgate_ref)

        dxv = dx_ref[...]
        dy_ref[...] = (dxv * g_ref[...]).astype(dy_ref.dtype)
        dgate_ref[...] += jnp.sum(dxv * y_ref[...].astype(F32), axis=0, keepdims=True)

    row = pl.BlockSpec((ts, d), lambda i: (i, 0))
    vec = pl.BlockSpec((1, d), lambda i: (0, 0))
    return pl.pallas_call(
        body, name=name, grid=(s // ts,),
        in_specs=[row, row, vec], out_specs=[row, vec],
        out_shape=[jax.ShapeDtypeStruct((s, d), BF16), jax.ShapeDtypeStruct((1, d), F32)],
        compiler_params=_params("arbitrary"),
    )(dx, y, gate)


def _conv(v, w_ref, width):
    out = w_ref[width - 1:width, :] * v
    for k in range(width - 1):
        out = out + w_ref[k:k + 1, :] * _shift_down(v, width - 1 - k)
    return out


def _sc_fwd(proj, conv_w, name, deps=()):
    _, s, e = proj.shape
    te = _tile(e, 256)
    width = conv_w.shape[0]

    def body(b_ref, c_ref, v_ref, g_ref, w_ref, o_ref):
        cv = c_ref[...].astype(F32) * v_ref[...].astype(F32)
        u = _conv(cv, w_ref, width)
        gv = g_ref[...].astype(F32)
        o_ref[...] = (b_ref[...].astype(F32) * u * (gv * _sigmoid(gv))).astype(o_ref.dtype)

    def part(q):
        return pl.BlockSpec((None, s, te), lambda j, q=q: (q, 0, j))

    return pl.pallas_call(
        _after(body, 5, deps), name=name, grid=(e // te,),
        in_specs=[part(0), part(1), part(2), part(3), pl.BlockSpec((width, te), lambda j: (0, j))]
        + [ANY] * len(deps),
        out_specs=pl.BlockSpec((s, te), lambda j: (0, j)),
        out_shape=jax.ShapeDtypeStruct((s, e), BF16),
        compiler_params=_params("arbitrary"),
    )(proj, proj, proj, proj, conv_w, *deps)


def _sc_bwd(proj, dyb, conv_w, name):
    _, s, e = proj.shape
    te = _tile(e, 256)
    width = conv_w.shape[0]

    def body(b_ref, c_ref, v_ref, g_ref, dy_ref, w_ref, dp_ref, vec_ref):
        bv = b_ref[...].astype(F32)
        cvl = c_ref[...].astype(F32)
        vv = v_ref[...].astype(F32)
        gv = g_ref[...].astype(F32)
        dyv = dy_ref[...].astype(F32)
        cv = cvl * vv
        u = _conv(cv, w_ref, width)
        sg = _sigmoid(gv)
        silu = gv * sg
        dp_ref[0] = (dyv * u * silu).astype(dp_ref.dtype)
        du = dyv * bv * silu
        dp_ref[3] = (dyv * bv * u * (sg * (1.0 + gv * (1.0 - sg)))).astype(dp_ref.dtype)
        dcv = w_ref[width - 1:width, :] * du
        vec_ref[...] = jnp.zeros_like(vec_ref)
        vec_ref[width - 1:width, :] = jnp.sum(du * cv, axis=0, keepdims=True)
        for k in range(width - 1):
            sh = width - 1 - k
            dcv = dcv + w_ref[k:k + 1, :] * _shift_up(du, sh)
            vec_ref[k:k + 1, :] = jnp.sum(du * _shift_down(cv, sh), axis=0, keepdims=True)
        dp_ref[1] = (dcv * vv).astype(dp_ref.dtype)
        dp_ref[2] = (dcv * cvl).astype(dp_ref.dtype)

    def part(q):
        return pl.BlockSpec((None, s, te), lambda j, q=q: (q, 0, j))

    return pl.pallas_call(
        body, name=name, grid=(e // te,),
        in_specs=[part(0), part(1), part(2), part(3), pl.BlockSpec((s, te), lambda j: (0, j)),
                  pl.BlockSpec((width, te), lambda j: (0, j))],
        out_specs=[pl.BlockSpec((4, s, te), lambda j: (0, 0, j)),
                   pl.BlockSpec((8, te), lambda j: (0, j))],
        out_shape=[jax.ShapeDtypeStruct((4, s, e), BF16), jax.ShapeDtypeStruct((8, e), F32)],
        compiler_params=_params("arbitrary"),
    )(proj, proj, proj, proj, dyb, conv_w)


def _lru_gates(v_pre, w_ref, cb_ref, wa_ref, ba_ref, wx_ref, bx_ref, lam_ref, width):
    v = _conv(v_pre, w_ref, width) + cb_ref[...]
    vb = v.astype(BF16)
    r = _sigmoid(jnp.dot(vb, wa_ref[...], preferred_element_type=F32) + ba_ref[...])
    i = _sigmoid(jnp.dot(vb, wx_ref[...], preferred_element_type=F32) + bx_ref[...])
    nl = -lam_ref[...]
    sp = jnp.maximum(nl, 0.0) + jnp.log1p(jnp.exp(-jnp.abs(nl)))
    log_a = (-RGLRU_C) * r * sp
    a = jnp.exp(log_a)
    one_minus_a2 = jnp.tanh(-log_a) * (1.0 + a * a)
    mult = jnp.sqrt(one_minus_a2)
    return v, vb, r, i, sp, a, mult


def _lru_specs(s, dh, heads, width):
    head_col = lambda q: pl.BlockSpec((None, s, dh), lambda h, q=q: (q, 0, h))
    vec = pl.BlockSpec((1, dh), lambda h: (0, h))
    mat = pl.BlockSpec((None, dh, dh), lambda h: (h, 0, 0))
    weights = [pl.BlockSpec((width, dh), lambda h: (0, h)), vec, mat, vec, mat, vec, vec]
    return head_col, weights


def _lru_fwd(proj, conv_w, conv_b, w_a, b_a, w_x, b_x, lam, name, deps=()):
    _, s, e = proj.shape
    heads, dh, _ = w_a.shape
    width = conv_w.shape[0]

    def body(v_ref, g_ref, w_ref, cb_ref, wa_ref, ba_ref, wx_ref, bx_ref, lam_ref, yb_ref, hs_ref):
        v, _, _, i, _, a, mult = _lru_gates(v_ref[...].astype(F32), w_ref, cb_ref, wa_ref, ba_ref,
                                           wx_ref, bx_ref, lam_ref, width)
        hs = _scan(a, mult * i * v, _shift_down)
        hs_ref[...] = hs
        gv = g_ref[...].astype(F32)
        yb_ref[...] = (hs * (gv * _sigmoid(gv))).astype(yb_ref.dtype)

    head_col, weights = _lru_specs(s, dh, heads, width)
    out = pl.BlockSpec((s, dh), lambda h: (0, h))
    return pl.pallas_call(
        _after(body, 9, deps), name=name, grid=(heads,),
        in_specs=[head_col(0), head_col(1)] + weights + [ANY] * len(deps),
        out_specs=[out, out],
        out_shape=[jax.ShapeDtypeStruct((s, e), BF16), jax.ShapeDtypeStruct((s, e), F32)],
        compiler_params=_params("arbitrary"),
    )(proj, proj, conv_w, conv_b, w_a, b_a, w_x, b_x, lam, *deps)


def _lru_bwd(proj, hs, dyb, conv_w, conv_b, w_a, b_a, w_x, b_x, lam, name):
    _, s, e = proj.shape
    heads, dh, _ = w_a.shape
    width = conv_w.shape[0]

    def body(v_ref, g_ref, hs_ref, dy_ref, w_ref, cb_ref, wa_ref, ba_ref, wx_ref, bx_ref, lam_ref,
             dp_ref, dwa_ref, dwx_ref, vec_ref):
        v_pre = v_ref[...].astype(F32)
        v, vb, r, i, sp, a, mult = _lru_gates(v_pre, w_ref, cb_ref, wa_ref, ba_ref, wx_ref, bx_ref,
                                              lam_ref, width)
        hs = hs_ref[...]
        gv = g_ref[...].astype(F32)
        dyv = dy_ref[...].astype(F32)
        sg = _sigmoid(gv)
        dp_ref[1] = (dyv * hs * (sg * (1.0 + gv * (1.0 - sg)))).astype(dp_ref.dtype)
        dhs = dyv * (gv * sg)
        d_h = _scan(_shift_up(a, 1), dhs, _shift_up)
        da = d_h * _shift_down(hs, 1)
        iv = i * v
        dlog_a = da * a - (d_h * iv) * (a * a) / mult
        di = d_h * mult * v
        dv = d_h * mult * i
        dzr = dlog_a * (-RGLRU_C) * sp * r * (1.0 - r)
        dzi = di * i * (1.0 - i)
        dsp = jnp.sum(dlog_a * r, axis=0, keepdims=True) * (-RGLRU_C)
        vec_ref[...] = jnp.zeros_like(vec_ref)
        vec_ref[0:1, :] = jnp.sum(dzr, axis=0, keepdims=True)
        vec_ref[1:2, :] = jnp.sum(dzi, axis=0, keepdims=True)
        vec_ref[2:3, :] = -dsp * _sigmoid(-lam_ref[...])
        dzr_b = dzr.astype(BF16)
        dzi_b = dzi.astype(BF16)
        vt = vb.astype(F32).T.astype(BF16)
        dwa_ref[...] = jnp.dot(vt, dzr_b, preferred_element_type=F32).astype(dwa_ref.dtype)
        dwx_ref[...] = jnp.dot(vt, dzi_b, preferred_element_type=F32).astype(dwx_ref.dtype)
        nt = (((1,), (1,)), ((), ()))
        dv = dv + lax.dot_general(dzr_b, wa_ref[...], nt, preferred_element_type=F32)
        dv = dv + lax.dot_general(dzi_b, wx_ref[...], nt, preferred_element_type=F32)
        vec_ref[3:4, :] = jnp.sum(dv, axis=0, keepdims=True)
        dvp = w_ref[width - 1:width, :] * dv
        vec_ref[4 + width - 1:4 + width, :] = jnp.sum(dv * v_pre, axis=0, keepdims=True)
        for k in range(width - 1):
            sh = width - 1 - k
            dvp = dvp + w_ref[k:k + 1, :] * _shift_up(dv, sh)
            vec_ref[4 + k:5 + k, :] = jnp.sum(dv * _shift_down(v_pre, sh), axis=0, keepdims=True)
        dp_ref[0] = dvp.astype(dp_ref.dtype)

    head_col, weights = _lru_specs(s, dh, heads, width)
    col = pl.BlockSpec((s, dh), lambda h: (0, h))
    mat = pl.BlockSpec((None, dh, dh), lambda h: (h, 0, 0))
    return pl.pallas_call(
        body, name=name, grid=(heads,),
        in_specs=[head_col(0), head_col(1), col, col] + weights,
        out_specs=[pl.BlockSpec((2, s, dh), lambda h: (0, 0, h)), mat, mat,
                   pl.BlockSpec((16, dh), lambda h: (0, h))],
        out_shape=[jax.ShapeDtypeStruct((2, s, e), BF16),
                   jax.ShapeDtypeStruct((heads, dh, dh), BF16),
                   jax.ShapeDtypeStruct((heads, dh, dh), BF16),
                   jax.ShapeDtypeStruct((16, e), F32)],
        compiler_params=_params("arbitrary"),
    )(proj, proj, hs, dyb, conv_w, conv_b, w_a, b_a, w_x, b_x, lam)


def _ada_mod(c_all, w, b, name):
    layers, d, f = w.shape
    nb = c_all.shape[0]

    def body(c_ref, w_ref, b_ref, o_ref):
        cv = c_ref[...]
        sc = cv * _sigmoid(cv)
        o_ref[...] = jnp.dot(sc, w_ref[...], preferred_element_type=F32,
                             precision=lax.Precision.HIGHEST) + b_ref[...]

    return pl.pallas_call(
        body, name=name, grid=(layers,),
        in_specs=[pl.BlockSpec((nb, d), lambda l: (0, 0)),
                  pl.BlockSpec((None, d, f), lambda l: (l, 0, 0)),
                  pl.BlockSpec((None, 1, f), lambda l: (l, 0, 0))],
        out_specs=pl.BlockSpec((None, nb, f), lambda l: (l, 0, 0)),
        out_shape=jax.ShapeDtypeStruct((layers, nb, f), F32),
        compiler_params=_params("arbitrary"),
    )(c_all, w, b)


def _ada_grad(c_all_t, dmod, name):
    d, nb = c_all_t.shape
    layers, _, f = dmod.shape

    def body(c_ref, dm_ref, o_ref):
        cv = c_ref[...]
        sc = cv * _sigmoid(cv)
        acc = sc[:, 0:1] * dm_ref[0:1, :]
        for k in range(1, nb):
            acc = acc + sc[:, k:k + 1] * dm_ref[k:k + 1, :]
        o_ref[...] = acc

    return pl.pallas_call(
        body, name=name, grid=(layers,),
        in_specs=[pl.BlockSpec((d, nb), lambda l: (0, 0)),
                  pl.BlockSpec((None, nb, f), lambda l: (l, 0, 0))],
        out_specs=pl.BlockSpec((None, d, f), lambda l: (l, 0, 0)),
        out_shape=jax.ShapeDtypeStruct((layers, d, f), F32),
        compiler_params=_params("arbitrary"),
    )(c_all_t, dmod)


def _device_sum(g, name):
    _, rows, _ = g.shape

    def body(g_ref, o_ref):
        acc = g_ref[0]
        for k in range(1, N_DEV):
            acc = acc + g_ref[k]
        o_ref[...] = acc

    return pl.pallas_call(
        body, name=name,
        in_specs=[VMEM_SPEC], out_specs=VMEM_SPEC,
        out_shape=jax.ShapeDtypeStruct((rows, LANES), F32),
        compiler_params=pltpu.CompilerParams(vmem_limit_bytes=VMEM_LIMIT),
    )(g)


def _adamw_math(w, g, m, v):
    m = ADAM_B1 * m + (1.0 - ADAM_B1) * g
    v = ADAM_B2 * v + (1.0 - ADAM_B2) * (g * g)
    m_hat = m / (1.0 - ADAM_B1 ** ADAM_STEP)
    v_hat = v / (1.0 - ADAM_B2 ** ADAM_STEP)
    delta = -ADAM_LR * (m_hat / (jnp.sqrt(v_hat) + ADAM_EPS) + ADAM_WD * w)
    return delta, m, v


def _adamw(w, g, m, v, name):
    rows, cols = w.shape
    tr = _tile(rows, 256)

    def body(w_ref, g_ref, m_ref, v_ref, d_ref, mo_ref, vo_ref):
        d_ref[...], mo_ref[...], vo_ref[...] = _adamw_math(w_ref[...], g_ref[...], m_ref[...], v_ref[...])

    blk = pl.BlockSpec((tr, cols), lambda i: (i, 0))
    return pl.pallas_call(
        body, name=name, grid=(rows // tr,),
        in_specs=[blk] * 4, out_specs=[blk] * 3,
        out_shape=[jax.ShapeDtypeStruct((rows, cols), F32)] * 3,
        compiler_params=_params("arbitrary"),
    )(w, g, m, v)


def _adamw_reduced(idx, w, m, v, part, got, recvs, name):
    rows, cols = w.shape
    tr = _tile(rows, 256)
    nr = len(recvs)

    def body(idx_ref, w_ref, m_ref, v_ref, p_ref, q_ref, *rest):
        g_ref, d_ref, mo_ref, vo_ref = rest[nr:]
        g = p_ref[...].astype(F32) + q_ref[...].astype(F32)
        for u_ref in rest[:nr]:
            for j in range(u_ref.shape[0]):
                g = g + u_ref[j].astype(F32)
        g_ref[...] = g
        d_ref[...], mo_ref[...], vo_ref[...] = _adamw_math(w_ref[...], g, m_ref[...], v_ref[...])

    blk = pl.BlockSpec((tr, cols), lambda i, idx: (i, 0))
    grid_spec = pltpu.PrefetchScalarGridSpec(
        num_scalar_prefetch=1, grid=(rows // tr,),
        in_specs=[blk, blk, blk,
                  pl.BlockSpec((None, None, tr, cols), lambda i, idx: (idx[3], idx[4], i, 0)),
                  pl.BlockSpec((None, None, tr, cols), lambda i, idx: (idx[3], 0, i, 0))]
        + [pl.BlockSpec((u.shape[0], tr, cols), lambda i, idx: (0, i, 0)) for u in recvs],
        out_specs=[blk] * 4)
    return pl.pallas_call(
        body, name=name, grid_spec=grid_spec,
        out_shape=[jax.ShapeDtypeStruct((rows, cols), F32)] * 4,
        compiler_params=_params("arbitrary"),
    )(idx, w, m, v, part, got, *recvs)


def _pack(vectors):
    flat = jnp.concatenate([v.reshape(-1).astype(F32) for v in vectors])
    pad = (-flat.shape[0]) % (8 * LANES)
    return jnp.pad(flat, (0, pad)).reshape(-1, LANES)


def _unpack(flat, shapes):
    out, off = [], 0
    for shp in shapes:
        size = math.prod(shp)
        out.append(flat[..., off:off + size].reshape(flat.shape[:-1] + tuple(shp)))
        off += size
    return out


def _my_slice(full, me, axis):
    size = full.shape[axis] // N_DEV
    return lax.dynamic_slice_in_dim(full, me * size, size, axis)


def kernel(x, c, norm_g, ada_w, ada_b, sc_w_in, sc_conv_w, sc_w_out, lru_w_in, lru_conv_w, lru_conv_b, lru_w_a, lru_b_a, lru_w_x, lru_b_x, lru_lambda, lru_w_out, final_g, loss_target, m_norm_g, m_ada_w, m_ada_b, m_sc_w_in, m_sc_conv_w, m_sc_w_out, m_lru_w_in, m_lru_conv_w, m_lru_conv_b, m_lru_w_a, m_lru_b_a, m_lru_w_x, m_lru_b_x, m_lru_lambda, m_lru_w_out, m_final_g, v_norm_g, v_ada_w, v_ada_b, v_sc_w_in, v_sc_conv_w, v_sc_w_out, v_lru_w_in, v_lru_conv_w, v_lru_conv_b, v_lru_w_a, v_lru_b_a, v_lru_w_x, v_lru_b_x, v_lru_lambda, v_lru_w_out, v_final_g):
    _, s, d = x.shape
    e = sc_w_out.shape[1] * N_DEV
    heads, dh_s, dh = lru_w_a.shape[1:]
    es = e // N_DEV
    f = ada_w.shape[2]
    mx, my, mc = _position()
    me = 4 * mx + 2 * my + mc
    chip = 2 * mx + my
    idx = jnp.stack([chip ^ 1, chip ^ 2, chip ^ 3, chip, mc]).astype(jnp.int32)

    x0 = x[0]
    target = loss_target[0]

    small_shapes = [(d,), (3, es), (4, es), (es,), (heads, dh_s), (heads, dh_s), (es,)]
    small = _small_gather(_pack([c, sc_conv_w, lru_conv_w, lru_conv_b, lru_b_a, lru_b_x, lru_lambda]),
                          "gather_small_weights").reshape(N_DEV, -1)
    c_all, cw3, cw4, cb, ba, bx, lam = _unpack(small, small_shapes)
    cw3 = cw3.transpose(1, 0, 2).reshape(3, e)
    cw4 = cw4.transpose(1, 0, 2).reshape(4, e)
    cb = cb.reshape(1, e)
    lam = lam.reshape(1, e)
    ba = ba.transpose(1, 0, 2).reshape(1, e)
    bx = bx.transpose(1, 0, 2).reshape(1, e)

    ada_b_mine = _my_slice(ada_b, me, 1).reshape(2, 1, f)
    mod_mine = _ada_mod(c_all, ada_w, ada_b_mine, "ada_mod")
    mod_all = _small_gather(_pack([mod_mine]), "gather_mod")

    shards = [sc_w_in[0].astype(BF16), sc_w_out[0].astype(BF16), lru_w_in[0].astype(BF16),
              lru_w_a[0].reshape(heads * dh_s, dh).astype(BF16),
              lru_w_x[0].reshape(heads * dh_s, dh).astype(BF16), lru_w_out[0].astype(BF16)]
    lands = [lax.dynamic_update_slice(lax.empty((N_DEV,) + sh.shape, BF16), sh[None], (me, 0, 0))
             for sh in shards]
    every = [1, 2, 3, 0]
    units = [([0], [0]), ([0], [1, 2]), ([0], [3]), ([1], every), ([2, 3, 4], every), ([5], every)]
    sems, shards, lands, started = _gather_start(shards, lands, units, [mod_all], "gather_start")

    def gathered(u, after_forward, name):
        members, ks = units[u]
        fwd, shs, lnd, token = _gather_forward(
            [shards[i] for i in members], [lands[i] for i in members], ks, sems[u][0], sems[u][1],
            after_forward, "gather_forward_" + name)
        for i, sh, ld in zip(members, shs, lnd):
            shards[i], lands[i] = sh, ld

        def finish(after):
            out = _gather_finish([lands[i] for i in members], ks, fwd, after, "gather_finish_" + name)
            for i, ld in zip(members, out):
                lands[i] = ld
            return out

        return token, finish

    mod_all = mod_all.reshape(N_DEV, -1)
    mod_all = mod_all[:, :2 * N_DEV * f].reshape(N_DEV, 2, N_DEV, f)
    mod_all = mod_all.transpose(1, 2, 0, 3).reshape(2, N_DEV, 3 * d)
    mod = lax.dynamic_index_in_dim(mod_all, me, 1, keepdims=False)
    shift = [mod[l:l + 1, 0:d] for l in range(2)]
    scale = [mod[l:l + 1, d:2 * d] for l in range(2)]
    gate = [mod[l:l + 1, 2 * d:3 * d] for l in range(2)]
    ng = [norm_g[l:l + 1] for l in range(2)]
    fg = final_g.reshape(1, d)

    h0 = _norm_mod(x0, ng[0], scale[0], shift[0], "norm_mod_0", deps=[started])
    proj0 = lax.empty((4, s, e), BF16)
    tok, _ = gathered(0, [h0], "sc_w_in_own")
    proj0 = _mm_proj_group(h0, lands[0], idx, 3, proj0, "mm_proj_0_own", deps=[tok])
    tok, finish = gathered(1, [proj0], "sc_w_in_near")
    finish([tok])
    proj0 = _mm_proj_group(h0, lands[0], idx, 0, proj0, "mm_proj_0_near_y")
    proj0 = _mm_proj_group(h0, lands[0], idx, 1, proj0, "mm_proj_0_near_x")
    tok, finish = gathered(2, [proj0], "sc_w_in_far")
    wg_in0, = finish([tok])
    proj0 = _mm_proj_group(h0, wg_in0, idx, 2, proj0, "mm_proj_0_far")
    tok, finish = gathered(3, [proj0], "sc_w_out")
    yb0 = _sc_fwd(proj0, cw3, "sc_fwd", deps=[tok])
    w_out0 = finish([yb0])[0].reshape(e, d)
    x1, y0 = _mm_out(yb0, w_out0, x0, gate[0], "mm_out_0")
    tok, finish = gathered(4, [x1], "lru_in")
    h1 = _norm_mod(x1, ng[1], scale[1], shift[1], "norm_mod_1", deps=[tok])
    wg_in1, wg_a, wg_x = finish([h1])
    w_a = wg_a.reshape(N_DEV, heads, dh_s, dh).transpose(1, 0, 2, 3).reshape(heads, dh, dh)
    w_x = wg_x.reshape(N_DEV, heads, dh_s, dh).transpose(1, 0, 2, 3).reshape(heads, dh, dh)
    proj1 = _mm_proj(h1, wg_in1, 2, "mm_proj_1")
    tok, finish = gathered(5, [proj1], "lru_w_out")
    yb1, hs = _lru_fwd(proj1, cw4, cb, w_a, ba, w_x, bx, lam, "lru_fwd", deps=[tok])
    w_out1 = finish([yb1])[0].reshape(e, d)
    x2, y1 = _mm_out(yb1, w_out1, x1, gate[1], "mm_out_1")
    dx2, loss_part, d_fg = _final_loss(x2, fg, target, "final_loss")

    def pieces(g, rows, cols):
        return g.reshape(4, 2, rows, cols)

    def by_rows(g):
        return g.reshape(heads, N_DEV, dh_s, dh).transpose(1, 0, 2, 3).reshape(N_DEV, heads * dh_s, dh)

    def scatter_start(parts, names, group):
        gots = _pair_exchange(parts, "pair_exchange_" + group)
        sums = [_pair_sum(idx, p, q, "pair_sum_" + nm) for p, q, nm in zip(parts, gots, names)]
        empties = [lax.empty(sm.shape, sm.dtype) for sm in sums]
        send, recv, sums, lnd, token = _chip_start(sums, empties, "chip_start_" + group)
        return dict(parts=parts, gots=gots, names=names, group=group, sums=sums, lands=lnd,
                    send=send, recv=recv), token

    big = {"sc_w_in": (sc_w_in, m_sc_w_in, v_sc_w_in), "sc_w_out": (sc_w_out, m_sc_w_out, v_sc_w_out),
           "lru_w_in": (lru_w_in, m_lru_w_in, v_lru_w_in), "lru_w_a": (lru_w_a, m_lru_w_a, v_lru_w_a),
           "lru_w_x": (lru_w_x, m_lru_w_x, v_lru_w_x), "lru_w_out": (lru_w_out, m_lru_w_out, v_lru_w_out)}
    big_res = {}

    def scatter_finish(rs, after):
        recvs = _chip_wait(rs["sums"], rs["lands"], rs["send"], rs["recv"], after, "chip_wait_" + rs["group"])
        done = []
        for p, q, u, nm in zip(rs["parts"], rs["gots"], recvs, rs["names"]):
            w, m, v = big[nm]
            shp2 = p.shape[2:]
            res = _adamw_reduced(idx, w.reshape(shp2), m.reshape(shp2), v.reshape(shp2), p, q, [u], "adamw_" + nm)
            big_res[nm] = [r.reshape(w.shape) for r in res]
            done.append(res[1])
        return done

    dy1, dgate1 = _gate_bwd(dx2, y1, gate[1], "gate_bwd_1")
    dw_out1 = _mm_tn(yb1, dy1[None], 1, "mm_dw_out_1")
    rs1, tok = scatter_start([pieces(dw_out1, es, d)], ["lru_w_out"], "lru_w_out")
    dyb1 = _mm_nt(dy1[None], w_out1[None], BF16, "mm_dyb_1", deps=[tok])
    dproj1, dw_a, dw_x, vecs1 = _lru_bwd(proj1, hs, dyb1, cw4, cb, w_a, ba, w_x, bx, lam, "lru_bwd")
    done = scatter_finish(rs1, [dproj1])
    dw_in1 = _mm_tn(h1, dproj1, N_DEV, "mm_dw_in_1", deps=done)
    rs2, tok = scatter_start([pieces(dw_in1, d, 2 * es), pieces(by_rows(dw_a), heads * dh_s, dh),
                              pieces(by_rows(dw_x), heads * dh_s, dh)],
                             ["lru_w_in", "lru_w_a", "lru_w_x"], "lru_in")
    dh1 = _mm_nt(dproj1, wg_in1, F32, "mm_dh_1", deps=[tok])
    dx1, dscale1, dshift1, dng1 = _norm_mod_bwd(x1, dh1, dx2, ng[1], scale[1], "norm_mod_bwd_1")
    dy0, dgate0 = _gate_bwd(dx1, y0, gate[0], "gate_bwd_0")
    dw_out0 = _mm_tn(yb0, dy0[None], 1, "mm_dw_out_0")
    rs3, tok = scatter_start([pieces(dw_out0, es, d)], ["sc_w_out"], "sc_w_out")
    dyb0 = _mm_nt(dy0[None], w_out0[None], BF16, "mm_dyb_0", deps=[tok])
    dproj0, vecs0 = _sc_bwd(proj0, dyb0, cw3, "sc_bwd")
    early_shapes = [(LANES,), (d,), (3 * d,), (d,), (d,), (8, e), (16, e)]
    early_all = _small_gather(
        _pack([loss_part, dgate0, jnp.concatenate([dshift1, dscale1, dgate1], axis=1), dng1, d_fg, vecs0, vecs1]),
        "gather_small_grads_early")
    done = scatter_finish(rs2, [early_all])
    idx_one = jnp.stack([jnp.zeros_like(mc)] * 4 + [mc]).astype(jnp.int32)
    sc_w_in_steps = []
    for j in (1, 2, 3, 0):
        part = _mm_tn_group(h0, dproj0, idx, (j - 1) % 4, 2, "mm_dw_in_0_%d" % j, deps=done)[None]
        got, = _pair_exchange([part], "pair_exchange_sc_w_in_%d" % j)
        if j:
            sm = _pair_sum(idx_one, part, got, "pair_sum_sc_w_in_%d" % j, nslots=1)
            send, recv, sums, lnd, tok = _chip_start([sm], [lax.empty(sm.shape, sm.dtype)],
                                                     "chip_start_sc_w_in_%d" % j, flips=(j,))
            sc_w_in_steps.append((sums, lnd, send, recv, j))
            done = [tok]
    dh0 = _mm_nt(dproj0, wg_in0, F32, "mm_dh_0", deps=[got])
    dx0, dscale0, dshift0, dng0 = _norm_mod_bwd(x0, dh0, dx1, ng[0], scale[0], "norm_mod_bwd_0")
    done = scatter_finish(rs3, [dx0])
    early_sum = _device_sum(early_all, "sum_small_grads_early").reshape(-1)
    late_shapes = [(d,), (d,), (d,)]
    late_all = _small_gather(_pack([dshift0, dscale0, dng0]), "gather_small_grads_late", deps=done)
    late_sum = _device_sum(late_all, "sum_small_grads_late").reshape(-1)

    loss_v, s_dgate0, s_dmod1, s_dng1, g_final_g, sum0, sum1 = _unpack(early_sum, early_shapes)
    s_dshift0, s_dscale0, s_dng0 = _unpack(late_sum, late_shapes)
    loss = loss_v[0]
    g_ada_b = jnp.stack([jnp.concatenate([s_dshift0, s_dscale0, s_dgate0]), s_dmod1])
    g_norm_g = jnp.stack([s_dng0, s_dng1])
    _, a_dgate0, a_dmod1 = _unpack(early_all.reshape(N_DEV, -1), early_shapes)[:3]
    a_dshift0, a_dscale0, _ = _unpack(late_all.reshape(N_DEV, -1), late_shapes)
    dmod_all = jnp.stack([jnp.concatenate([a_dshift0, a_dscale0, a_dgate0], axis=1), a_dmod1])
    dmod_cols = _my_slice(dmod_all, me, 2)
    g_ada_w = _ada_grad(c_all.T, dmod_cols, "ada_grad")

    g_sc_conv_w = _my_slice(sum0[0:3], me, 1)
    g_lru_b_a = _my_slice(sum1[0].reshape(heads, dh), me, 1)
    g_lru_b_x = _my_slice(sum1[1].reshape(heads, dh), me, 1)
    g_lru_lambda = _my_slice(sum1[2:3], me, 1)
    g_lru_conv_b = _my_slice(sum1[3:4], me, 1)
    g_lru_conv_w = _my_slice(sum1[4:8], me, 1)

    ada_res = _adamw(ada_w.reshape(2 * d, f), g_ada_w.reshape(2 * d, f), m_ada_w.reshape(2 * d, f),
                     v_ada_w.reshape(2 * d, f), "adamw_ada_w")
    ada_out = [g_ada_w] + [r.reshape(ada_w.shape) for r in ada_res]

    small_w = [norm_g, ada_b, final_g, sc_conv_w, lru_conv_w, lru_conv_b, lru_b_a, lru_b_x, lru_lambda]
    small_m = [m_norm_g, m_ada_b, m_final_g, m_sc_conv_w, m_lru_conv_w, m_lru_conv_b, m_lru_b_a, m_lru_b_x,
               m_lru_lambda]
    small_v = [v_norm_g, v_ada_b, v_final_g, v_sc_conv_w, v_lru_conv_w, v_lru_conv_b, v_lru_b_a, v_lru_b_x,
               v_lru_lambda]
    small_g = [g_norm_g, g_ada_b, g_final_g, g_sc_conv_w, g_lru_conv_w, g_lru_conv_b, g_lru_b_a, g_lru_b_x,
               g_lru_lambda]
    small_g = [g.reshape(w.shape) for g, w in zip(small_g, small_w)]
    shapes = [w.shape for w in small_w]
    packed = _adamw(_pack(small_w), _pack(small_g), _pack(small_m), _pack(small_v), "adamw_small")
    small_out = [small_g] + [_unpack(p.reshape(-1), shapes) for p in packed]

    after = [packed[0], ada_res[0]]
    recvs = []
    for sums, lnd, send, recv, j in sc_w_in_steps:
        recvs += _chip_wait(sums, lnd, send, recv, after, "chip_wait_sc_w_in_%d" % j)
    shp2 = part.shape[2:]
    res = _adamw_reduced(idx_one, sc_w_in.reshape(shp2), m_sc_w_in.reshape(shp2), v_sc_w_in.reshape(shp2),
                         part, got, recvs, "adamw_sc_w_in")
    big_res["sc_w_in"] = [r.reshape(sc_w_in.shape) for r in res]
    big_out = [big_res[nm] for nm in ("sc_w_in", "sc_w_out", "lru_w_in", "lru_w_a", "lru_w_x", "lru_w_out")]

    def small(kind, i):
        return small_out[kind][i]

    def bigw(kind, i):
        return big_out[i][kind]

    outs = [loss, dx0[None]]
    for kind in range(4):
        outs += [small(kind, 0), ada_out[kind], small(kind, 1), bigw(kind, 0), small(kind, 3), bigw(kind, 1),
                 bigw(kind, 2), small(kind, 4), small(kind, 5), bigw(kind, 3), small(kind, 6), bigw(kind, 4),
                 small(kind, 7), small(kind, 8), bigw(kind, 5), small(kind, 2)]
    return tuple(outs)
```

```python
import math

import jax
import jax.numpy as jnp
from jax import lax
from jax.experimental import pallas as pl
from jax.experimental.pallas import tpu as pltpu

N_DEV = 8
LANES = 128
EPS = 1e-6
RGLRU_C = 8.0
ADAM_LR = 0.001
ADAM_B1 = 0.9
ADAM_B2 = 0.999
ADAM_EPS = 1e-08
ADAM_WD = 0.01
ADAM_STEP = 10
VMEM_LIMIT = 56 * 1024 * 1024
MESH = pl.DeviceIdType.MESH
F32 = jnp.float32
BF16 = jnp.bfloat16
ANY = pl.BlockSpec(memory_space=pl.ANY)
HBM = pl.BlockSpec(memory_space=pltpu.HBM)
SEM = pl.BlockSpec(memory_space=pltpu.SEMAPHORE)
VMEM_SPEC = pl.BlockSpec(memory_space=pltpu.VMEM)
EFFECT = pltpu.SideEffectType.DATAFLOW_SIDE_EFFECTING
TOKEN = jax.ShapeDtypeStruct((8, LANES), jnp.float32)


def _tile(n, pref):
    t = min(n, pref)
    assert n % t == 0, (n, pref)
    return t


def _params(*sem):
    return pltpu.CompilerParams(dimension_semantics=sem, vmem_limit_bytes=VMEM_LIMIT)


def _position():
    return lax.axis_index("x"), lax.axis_index("y"), lax.axis_index("c")


def _flip(x, y, k):
    return (1 - x if k & 2 else x), (1 - y if k & 1 else y)


def _after(body, n_in, deps):
    if not deps:
        return body

    def wrapped(*refs):
        return body(*refs[:n_in], *refs[n_in + len(deps):])

    return wrapped


def _small_gather(v, name, deps=()):
    rows = v.shape[0]

    def body(v_ref, out_ref, send_sems, recv_sems):
        x, y, c = _position()
        me = 4 * x + 2 * y + c
        out_ref[me] = v_ref[...]
        copies = []
        for k in range(1, N_DEV):
            px, py = _flip(x, y, k >> 1)
            pc = 1 - c if k & 1 else c
            cp = pltpu.make_async_remote_copy(
                src_ref=v_ref, dst_ref=out_ref.at[me],
                send_sem=send_sems.at[k - 1], recv_sem=recv_sems.at[k - 1],
                device_id=(px, py, pc), device_id_type=MESH)
            cp.start()
            copies.append((cp, 4 * px + 2 * py + pc))
        for k, (cp, peer) in enumerate(copies):
            pltpu.make_async_remote_copy(
                src_ref=v_ref, dst_ref=out_ref.at[peer],
                send_sem=send_sems.at[k], recv_sem=recv_sems.at[k],
                device_id=(x, y, c), device_id_type=MESH).wait_recv()
        for cp, _ in copies:
            cp.wait_send()

    return pl.pallas_call(
        _after(body, 1, deps), name=name,
        out_shape=jax.ShapeDtypeStruct((N_DEV, rows, LANES), F32),
        in_specs=[VMEM_SPEC] + [ANY] * len(deps), out_specs=VMEM_SPEC,
        scratch_shapes=[pltpu.SemaphoreType.DMA((N_DEV - 1,)),
                        pltpu.SemaphoreType.DMA((N_DEV - 1,))],
        compiler_params=pltpu.CompilerParams(vmem_limit_bytes=VMEM_LIMIT),
    )(v, *deps)


def _hbm(a):
    return pltpu.with_memory_space_constraint(a, pltpu.HBM)


def _hbm_like(arrays):
    return [pltpu.HBM(a.shape, a.dtype) for a in arrays]


def _remote(src, dst, send, recv, to):
    return pltpu.make_async_remote_copy(src_ref=src, dst_ref=dst, send_sem=send, recv_sem=recv,
                                        device_id=to, device_id_type=MESH)


def _gather_start(shards, lands, units, after, name):
    n, nu = len(shards), len(units)

    def body(*refs):
        ins, lnd = refs[:n], refs[n:2 * n]
        sems = refs[2 * n + len(after):2 * n + len(after) + 2 * nu]
        token = refs[-1]
        x, y, c = _position()
        me = 4 * x + 2 * y + c
        targets = [(x, y, 1 - c)] + [(px, py, c) for px, py in (_flip(x, y, k) for k in (1, 2, 3))]
        for u, (members, ks) in enumerate(units):
            for slot, i in enumerate(members):
                for ki, k in enumerate(ks):
                    at = len(ks) * slot + ki
                    _remote(ins[i], lnd[i].at[me], sems[2 * u].at[at], sems[2 * u + 1].at[at], targets[k]).start()
        token[...] = jnp.zeros_like(token)

    sem_shapes = []
    for members, ks in units:
        count = len(members) * len(ks)
        sem_shapes += [pltpu.SemaphoreType.DMA((count,)), pltpu.SemaphoreType.DMA((count,))]
    out = pl.pallas_call(
        body, name=name,
        out_shape=sem_shapes + _hbm_like(shards) + _hbm_like(lands) + [TOKEN],
        in_specs=[HBM] * (2 * n) + [ANY] * len(after),
        out_specs=[SEM] * (2 * nu) + [HBM] * (2 * n) + [VMEM_SPEC],
        input_output_aliases={i: 2 * nu + i for i in range(2 * n)},
        compiler_params=pltpu.CompilerParams(has_side_effects=EFFECT),
    )(*[_hbm(s) for s in shards], *[_hbm(l) for l in lands], *after)
    sems = [(out[2 * u], out[2 * u + 1]) for u in range(nu)]
    return sems, list(out[2 * nu:2 * nu + n]), list(out[2 * nu + n:2 * nu + 2 * n]), out[-1]


def _gather_forward(shards, lands, ks, send, recv, after, name):
    m = len(shards)
    hops = [k for k in ks if k]
    nsem = 2 if hops else 0

    def body(*refs):
        ins, lnd = refs[:m], refs[m:2 * m]
        send_ref, recv_ref = refs[2 * m], refs[2 * m + 1]
        outs = refs[2 * m + 2 + len(after):]
        token = refs[-1]
        x, y, c = _position()
        me = (x, y, c)
        for slot in range(m):
            for ki, k in enumerate(ks):
                at = len(ks) * slot + ki
                if k:
                    px, py = _flip(x, y, k)
                    block = lnd[slot].at[4 * px + 2 * py + c]
                else:
                    block = lnd[slot].at[4 * x + 2 * y + (1 - c)]
                arrival = _remote(ins[slot], block, send_ref.at[at], recv_ref.at[at], me)
                arrival.wait_recv()
                if k:
                    fat = len(hops) * slot + hops.index(k)
                    _remote(block, block, outs[0].at[fat], outs[1].at[fat], (x, y, 1 - c)).start()
                arrival.wait_send()
        token[...] = jnp.zeros_like(token)

    count = len(hops) * m
    sem_shapes = [pltpu.SemaphoreType.DMA((count,)), pltpu.SemaphoreType.DMA((count,))] if hops else []
    out = pl.pallas_call(
        body, name=name,
        out_shape=sem_shapes + _hbm_like(shards) + _hbm_like(lands) + [TOKEN],
        in_specs=[HBM] * (2 * m) + [SEM, SEM] + [ANY] * len(after),
        out_specs=[SEM] * nsem + [HBM] * (2 * m) + [VMEM_SPEC],
        input_output_aliases={i: nsem + i for i in range(2 * m)},
        compiler_params=pltpu.CompilerParams(has_side_effects=EFFECT),
    )(*shards, *lands, send, recv, *after)
    fwd = (out[0], out[1]) if hops else None
    return fwd, list(out[nsem:nsem + m]), list(out[nsem + m:nsem + 2 * m]), out[-1]


def _gather_finish(lands, ks, fwd, after, name):
    m = len(lands)
    hops = [k for k in ks if k]

    def body(*refs):
        lnd = refs[:m]
        fsend_ref, frecv_ref = refs[m], refs[m + 1]
        x, y, c = _position()
        for slot in range(m):
            for fi, k in enumerate(hops):
                px, py = _flip(x, y, k)
                sent = lnd[slot].at[4 * px + 2 * py + c]
                came = lnd[slot].at[4 * px + 2 * py + (1 - c)]
                fat = len(hops) * slot + fi
                cp = _remote(sent, came, fsend_ref.at[fat], frecv_ref.at[fat], (x, y, c))
                cp.wait_recv()
                cp.wait_send()

    out = pl.pallas_call(
        body, name=name,
        out_shape=_hbm_like(lands),
        in_specs=[HBM] * m + [SEM, SEM] + [ANY] * len(after), out_specs=[HBM] * m,
        input_output_aliases={i: i for i in range(m)},
        compiler_params=pltpu.CompilerParams(has_side_effects=EFFECT),
    )(*lands, fwd[0], fwd[1], *after)
    return list(out)


def _pair_start(parts, name):
    n = len(parts)
    lands = [lax.empty((p.shape[0], 1) + p.shape[2:], p.dtype) for p in parts]

    def body(*refs):
        ins, lnd = refs[:n], refs[n:2 * n]
        send_ref, recv_ref = refs[2 * n], refs[2 * n + 1]
        token = refs[-1]
        x, y, c = _position()
        for i in range(n):
            _remote(ins[i].at[:, pl.ds(1 - c, 1)], lnd[i], send_ref.at[i], recv_ref.at[i], (x, y, 1 - c)).start()
        token[...] = jnp.zeros_like(token)

    out = pl.pallas_call(
        body, name=name,
        out_shape=[pltpu.SemaphoreType.DMA((n,)), pltpu.SemaphoreType.DMA((n,))]
        + _hbm_like(parts) + _hbm_like(lands) + [TOKEN],
        in_specs=[HBM] * (2 * n), out_specs=[SEM, SEM] + [HBM] * (2 * n) + [VMEM_SPEC],
        input_output_aliases={i: 2 + i for i in range(2 * n)},
        compiler_params=pltpu.CompilerParams(has_side_effects=EFFECT),
    )(*[_hbm(p) for p in parts], *[_hbm(l) for l in lands])
    return out[0], out[1], list(out[2:2 + n]), list(out[2 + n:2 + 2 * n]), out[-1]


def _pair_wait(parts, lands, send, recv, after, name):
    n = len(parts)

    def body(*refs):
        ins, lnd = refs[:n], refs[n:2 * n]
        send_ref, recv_ref = refs[2 * n], refs[2 * n + 1]
        x, y, c = _position()
        for i in range(n):
            cp = _remote(ins[i].at[:, pl.ds(1 - c, 1)], lnd[i], send_ref.at[i], recv_ref.at[i], (x, y, c))
            cp.wait_recv()
            cp.wait_send()

    out = pl.pallas_call(
        body, name=name,
        out_shape=_hbm_like(parts) + _hbm_like(lands),
        in_specs=[HBM] * (2 * n) + [SEM, SEM] + [ANY] * len(after), out_specs=[HBM] * (2 * n),
        input_output_aliases={i: i for i in range(2 * n)},
        compiler_params=pltpu.CompilerParams(has_side_effects=EFFECT),
    )(*parts, *lands, send, recv, *after)
    return list(out[:n]), list(out[n:])


def _chip_start(sums, lands, name, flips=(1, 2, 3)):
    n, ns = len(sums), len(flips)

    def body(*refs):
        ins, lnd = refs[:n], refs[n:2 * n]
        send_ref, recv_ref = refs[2 * n], refs[2 * n + 1]
        token = refs[-1]
        x, y, c = _position()
        for i in range(n):
            for j, flip in enumerate(flips):
                px, py = _flip(x, y, flip)
                _remote(ins[i].at[j], lnd[i].at[j], send_ref.at[ns * i + j], recv_ref.at[ns * i + j],
                        (px, py, c)).start()
        token[...] = jnp.zeros_like(token)

    out = pl.pallas_call(
        body, name=name,
        out_shape=[pltpu.SemaphoreType.DMA((ns * n,)), pltpu.SemaphoreType.DMA((ns * n,))]
        + _hbm_like(sums) + _hbm_like(lands) + [TOKEN],
        in_specs=[HBM] * (2 * n), out_specs=[SEM, SEM] + [HBM] * (2 * n) + [VMEM_SPEC],
        input_output_aliases={i: 2 + i for i in range(2 * n)},
        compiler_params=pltpu.CompilerParams(has_side_effects=EFFECT),
    )(*[_hbm(s) for s in sums], *[_hbm(l) for l in lands])
    return out[0], out[1], out[2:2 + n], out[2 + n:2 + 2 * n], out[-1]


def _chip_wait(sums, lands, send, recv, after, name):
    n, ns = len(sums), sums[0].shape[0]

    def body(*refs):
        ins, lnd = refs[:n], refs[n:2 * n]
        send_ref, recv_ref = refs[2 * n], refs[2 * n + 1]
        x, y, c = _position()
        for i in range(n):
            for j in range(ns):
                cp = _remote(ins[i].at[j], lnd[i].at[j], send_ref.at[ns * i + j], recv_ref.at[ns * i + j], (x, y, c))
                cp.wait_recv()
                cp.wait_send()

    out = pl.pallas_call(
        body, name=name,
        out_shape=_hbm_like(sums) + _hbm_like(lands),
        in_specs=[HBM] * (2 * n) + [SEM, SEM] + [ANY] * len(after), out_specs=[HBM] * (2 * n),
        input_output_aliases={i: i for i in range(2 * n)},
        compiler_params=pltpu.CompilerParams(has_side_effects=EFFECT),
    )(*sums, *lands, send, recv, *after)
    return list(out[n:])


def _pair_sum(idx, part, got, name, nslots=3):
    _, _, rows, cols = part.shape
    tr = _tile(rows, 512)

    def body(idx_ref, p_ref, q_ref, o_ref):
        o_ref[...] = (p_ref[...].astype(F32) + q_ref[...].astype(F32)).astype(o_ref.dtype)

    grid_spec = pltpu.PrefetchScalarGridSpec(
        num_scalar_prefetch=1, grid=(nslots, rows // tr),
        in_specs=[pl.BlockSpec((None, None, tr, cols), lambda j, r, idx: (idx[j], idx[4], r, 0)),
                  pl.BlockSpec((None, None, tr, cols), lambda j, r, idx: (idx[j], 0, r, 0))],
        out_specs=pl.BlockSpec((None, tr, cols), lambda j, r, idx: (j, r, 0)))
    return pl.pallas_call(
        body, name=name, grid_spec=grid_spec,
        out_shape=jax.ShapeDtypeStruct((nslots, rows, cols), part.dtype),
        compiler_params=_params("arbitrary", "arbitrary"),
    )(idx, part, got)


def _mm_proj(h, wg, groups, name):
    s, k = h.shape
    nchunk, _, n = wg.shape
    e = nchunk * n // groups
    tn = _tile(min(n, e), 512)

    def body(h_ref, w_ref, o_ref):
        o_ref[...] = jnp.dot(h_ref[...], w_ref[...], preferred_element_type=F32).astype(o_ref.dtype)

    return pl.pallas_call(
        body, name=name, grid=(nchunk * n // tn,),
        in_specs=[pl.BlockSpec((s, k), lambda j: (0, 0)),
                  pl.BlockSpec((None, k, tn), lambda j: ((j * tn) // n, 0, ((j * tn) % n) // tn))],
        out_specs=pl.BlockSpec((None, s, tn), lambda j: ((j * tn) // e, 0, ((j * tn) % e) // tn)),
        out_shape=jax.ShapeDtypeStruct((groups, s, e), BF16),
        compiler_params=_params("arbitrary"),
    )(h, wg)


def _mm_proj_group(h, wg, idx, pos, prev, name, deps=()):
    s, k = h.shape
    nchunk, _, n = wg.shape
    groups, _, e = prev.shape
    per = nchunk // groups
    assert per * n == e
    tn = _tile(n, 512)
    nd = len(deps)

    def body(idx_ref, h_ref, w_ref, prev_ref, *rest):
        o_ref = rest[nd]
        o_ref[...] = jnp.dot(h_ref[...], w_ref[...], preferred_element_type=F32).astype(o_ref.dtype)

    grid_spec = pltpu.PrefetchScalarGridSpec(
        num_scalar_prefetch=1, grid=(e // tn,),
        in_specs=[pl.BlockSpec((s, k), lambda j, idx: (0, 0)),
                  pl.BlockSpec((None, k, tn), lambda j, idx: (per * idx[pos] + (j * tn) // n, 0, ((j * tn) % n) // tn)),
                  ANY] + [ANY] * nd,
        out_specs=pl.BlockSpec((None, s, tn), lambda j, idx: (idx[pos], 0, j)))
    return pl.pallas_call(
        body, name=name, grid_spec=grid_spec,
        out_shape=jax.ShapeDtypeStruct(prev.shape, prev.dtype),
        input_output_aliases={3: 0},
        compiler_params=_params("arbitrary"),
    )(idx, h, wg, prev, *deps)


def _mm_out(yb, w, x, gate, name):
    s, k = yb.shape
    d = w.shape[1]
    tn = _tile(d, 512)
    tk = _tile(k, 1024)
    nk = k // tk

    def body(a_ref, w_ref, x_ref, g_ref, xo_ref, y_ref, acc_ref):
        kk = pl.program_id(1)

        @pl.when(kk == 0)
        def _():
            acc_ref[...] = jnp.zeros_like(acc_ref)

        acc_ref[...] += jnp.dot(a_ref[...], w_ref[...], preferred_element_type=F32)

        @pl.when(kk == nk - 1)
        def _():
            y = acc_ref[...]
            y_ref[...] = y.astype(y_ref.dtype)
            xo_ref[...] = x_ref[...] + g_ref[...] * y

    return pl.pallas_call(
        body, name=name, grid=(d // tn, nk),
        in_specs=[pl.BlockSpec((s, tk), lambda j, kk: (0, kk)),
                  pl.BlockSpec((tk, tn), lambda j, kk: (kk, j)),
                  pl.BlockSpec((s, tn), lambda j, kk: (0, j)),
                  pl.BlockSpec((1, tn), lambda j, kk: (0, j))],
        out_specs=[pl.BlockSpec((s, tn), lambda j, kk: (0, j)),
                   pl.BlockSpec((s, tn), lambda j, kk: (0, j))],
        out_shape=[jax.ShapeDtypeStruct((s, d), F32), jax.ShapeDtypeStruct((s, d), BF16)],
        scratch_shapes=[pltpu.VMEM((s, tn), F32)],
        compiler_params=_params("arbitrary", "arbitrary"),
    )(yb, w, x, gate)


def _mm_nt(a3, w3, out_dtype, name, deps=()):
    g, s, ea = a3.shape
    cw, n, nw = w3.shape
    total = g * ea
    assert total == cw * nw
    tk = _tile(min(ea, nw), 1024)
    tn = _tile(n, 1024)
    nk = total // tk

    def body(a_ref, w_ref, o_ref, acc_ref):
        kk = pl.program_id(1)

        @pl.when(kk == 0)
        def _():
            acc_ref[...] = jnp.zeros_like(acc_ref)

        acc_ref[...] += lax.dot_general(a_ref[...], w_ref[...], (((1,), (1,)), ((), ())),
                                        preferred_element_type=F32)

        @pl.when(kk == nk - 1)
        def _():
            o_ref[...] = acc_ref[...].astype(o_ref.dtype)

    return pl.pallas_call(
        _after(body, 2, deps), name=name, grid=(n // tn, nk),
        in_specs=[pl.BlockSpec((None, s, tk), lambda j, kk: ((kk * tk) // ea, 0, ((kk * tk) % ea) // tk)),
                  pl.BlockSpec((None, tn, tk), lambda j, kk: ((kk * tk) // nw, j, ((kk * tk) % nw) // tk))]
        + [ANY] * len(deps),
        out_specs=pl.BlockSpec((s, tn), lambda j, kk: (0, j)),
        out_shape=jax.ShapeDtypeStruct((s, n), out_dtype),
        scratch_shapes=[pltpu.VMEM((s, tn), F32)],
        compiler_params=_params("arbitrary", "arbitrary"),
    )(a3, w3, *deps)


def _mm_tn(a, b3, nchunk, name, deps=()):
    s, ka = a.shape
    g, _, eb = b3.shape
    n = g * eb // nchunk
    tm = _tile(ka, 1024)
    tn = _tile(min(n, eb), 1024)

    def body(a_ref, b_ref, o_ref, at_ref):
        @pl.when(pl.program_id(1) == 0)
        def _():
            at_ref[...] = a_ref[...].astype(F32).T.astype(at_ref.dtype)

        o_ref[...] = jnp.dot(at_ref[...], b_ref[...], preferred_element_type=F32).astype(o_ref.dtype)

    return pl.pallas_call(
        _after(body, 2, deps), name=name, grid=(ka // tm, g * eb // tn),
        in_specs=[pl.BlockSpec((s, tm), lambda i, j: (0, i)),
                  pl.BlockSpec((None, s, tn), lambda i, j: ((j * tn) // eb, 0, ((j * tn) % eb) // tn))]
        + [ANY] * len(deps),
        out_specs=pl.BlockSpec((None, tm, tn), lambda i, j: ((j * tn) // n, i, ((j * tn) % n) // tn)),
        out_shape=jax.ShapeDtypeStruct((nchunk, ka, n), BF16),
        scratch_shapes=[pltpu.VMEM((tm, s), BF16)],
        compiler_params=_params("arbitrary", "arbitrary"),
    )(a, b3, *deps)


def _mm_tn_group(a, b3, idx, pos, nchunk, name, deps=()):
    s, ka = a.shape
    _, _, eb = b3.shape
    n = eb // nchunk
    tm = _tile(ka, 1024)
    tn = _tile(n, 1024)
    nd = len(deps)

    def body(idx_ref, a_ref, b_ref, *rest):
        o_ref, at_ref = rest[nd:]

        @pl.when(pl.program_id(1) == 0)
        def _():
            at_ref[...] = a_ref[...].astype(F32).T.astype(at_ref.dtype)

        o_ref[...] = jnp.dot(at_ref[...], b_ref[...], preferred_element_type=F32).astype(o_ref.dtype)

    grid_spec = pltpu.PrefetchScalarGridSpec(
        num_scalar_prefetch=1, grid=(ka // tm, eb // tn),
        in_specs=[pl.BlockSpec((s, tm), lambda i, j, idx: (0, i)),
                  pl.BlockSpec((None, s, tn), lambda i, j, idx: (idx[pos], 0, j))] + [ANY] * nd,
        out_specs=pl.BlockSpec((None, tm, tn), lambda i, j, idx: ((j * tn) // n, i, ((j * tn) % n) // tn)),
        scratch_shapes=[pltpu.VMEM((tm, s), BF16)])
    return pl.pallas_call(
        body, name=name, grid_spec=grid_spec,
        out_shape=jax.ShapeDtypeStruct((nchunk, ka, n), BF16),
        compiler_params=_params("arbitrary", "arbitrary"),
    )(idx, a, b3, *deps)


def _sigmoid(z):
    return jax.nn.sigmoid(z)


def _shift_down(v, k, fill=0.0, period=None):
    if k == 0:
        return v
    row = lax.broadcasted_iota(jnp.int32, v.shape, 0)
    if period is not None:
        row = row & (period - 1)
    return jnp.where(row >= k, pltpu.roll(v, k, 0), fill)


def _shift_up(v, k, fill=0.0, period=None):
    if k == 0:
        return v
    s = v.shape[0]
    row = lax.broadcasted_iota(jnp.int32, v.shape, 0)
    if period is not None:
        row, s = row & (period - 1), period
    return jnp.where(row < s - k, pltpu.roll(v, v.shape[0] - k, 0), fill)


SCAN_BLOCK = 64


def _scan(a, b, shift):
    s = a.shape[0]
    blk = min(SCAN_BLOCK, s)
    k = 1
    while k < blk:
        b = a * shift(b, k, 0.0, blk) + b
        a = a * shift(a, k, 1.0, blk)
        k *= 2
    nblk = s // blk
    forward = shift is _shift_down
    order = range(nblk) if forward else range(nblk - 1, -1, -1)
    edge = blk - 1 if forward else 0
    out = [None] * nblk
    carry = None
    for i in order:
        h = b[i * blk:(i + 1) * blk]
        if carry is not None:
            h = a[i * blk:(i + 1) * blk] * carry + h
        carry = h[edge:edge + 1]
        out[i] = h
    return jnp.concatenate(out, axis=0) if nblk > 1 else out[0]


def _norm_mod(x, g, scale, shift, name, deps=()):
    s, d = x.shape
    ts = _tile(s, 256)

    def body(x_ref, g_ref, sc_ref, sh_ref, h_ref):
        xv = x_ref[...]
        rstd = lax.rsqrt(jnp.mean(xv * xv, axis=-1, keepdims=True) + EPS)
        nrm = xv * rstd * g_ref[...]
        h_ref[...] = (nrm * (1.0 + sc_ref[...]) + sh_ref[...]).astype(h_ref.dtype)

    vec = pl.BlockSpec((1, d), lambda i: (0, 0))
    return pl.pallas_call(
        _after(body, 4, deps), name=name, grid=(s // ts,),
        in_specs=[pl.BlockSpec((ts, d), lambda i: (i, 0)), vec, vec, vec] + [ANY] * len(deps),
        out_specs=pl.BlockSpec((ts, d), lambda i: (i, 0)),
        out_shape=jax.ShapeDtypeStruct((s, d), BF16),
        compiler_params=_params("arbitrary"),
    )(x, g, scale, shift, *deps)


def _norm_mod_bwd(x, dh, dx_res, g, scale, name, deps=()):
    s, d = x.shape
    ts = _tile(s, 256)

    def body(x_ref, dh_ref, dr_ref, g_ref, sc_ref, dx_ref, dsc_ref, dsh_ref, dg_ref):
        @pl.when(pl.program_id(0) == 0)
        def _():
            dsc_ref[...] = jnp.zeros_like(dsc_ref)
            dsh_ref[...] = jnp.zeros_like(dsh_ref)
            dg_ref[...] = jnp.zeros_like(dg_ref)

        xv = x_ref[...]
        dh_v = dh_ref[...].astype(F32)
        gv = g_ref[...]
        rstd = lax.rsqrt(jnp.mean(xv * xv, axis=-1, keepdims=True) + EPS)
        xhat = xv * rstd
        dsc_ref[...] += jnp.sum(dh_v * xhat * gv, axis=0, keepdims=True)
        dsh_ref[...] += jnp.sum(dh_v, axis=0, keepdims=True)
        dn = dh_v * (1.0 + sc_ref[...])
        dg_ref[...] += jnp.sum(dn * xhat, axis=0, keepdims=True)
        dxhat = dn * gv
        proj = jnp.mean(dxhat * xhat, axis=-1, keepdims=True)
        dx_ref[...] = dr_ref[...] + rstd * (dxhat - xhat * proj)

    row = pl.BlockSpec((ts, d), lambda i: (i, 0))
    vec = pl.BlockSpec((1, d), lambda i: (0, 0))
    return pl.pallas_call(
        _after(body, 5, deps), name=name, grid=(s // ts,),
        in_specs=[row, row, row, vec, vec] + [ANY] * len(deps),
        out_specs=[row, vec, vec, vec],
        out_shape=[jax.ShapeDtypeStruct((s, d), F32)] + [jax.ShapeDtypeStruct((1, d), F32)] * 3,
        compiler_params=_params("arbitrary"),
    )(x, dh, dx_res, g, scale, *deps)


def _final_loss(x, g, target, name):
    s, d = x.shape
    ts = _tile(s, 256)

    def body(x_ref, g_ref, t_ref, dx_ref, loss_ref, dg_ref):
        @pl.when(pl.program_id(0) == 0)
        def _():
            loss_ref[...] = jnp.zeros_like(loss_ref)
            dg_ref[...] = jnp.zeros_like(dg_ref)

        xv = x_ref[...]
        gv = g_ref[...]
        rstd = lax.rsqrt(jnp.mean(xv * xv, axis=-1, keepdims=True) + EPS)
        xhat = xv * rstd
        err = xhat * gv - t_ref[...]
        loss_ref[...] += 0.5 * jnp.sum(jnp.mean(err * err, axis=-1, keepdims=True))
        dy = err * (1.0 / d)
        dg_ref[...] += jnp.sum(dy * xhat, axis=0, keepdims=True)
        dxhat = dy * gv
        proj = jnp.mean(dxhat * xhat, axis=-1, keepdims=True)
        dx_ref[...] = rstd * (dxhat - xhat * proj)

    row = pl.BlockSpec((ts, d), lambda i: (i, 0))
    vec = pl.BlockSpec((1, d), lambda i: (0, 0))
    return pl.pallas_call(
        body, name=name, grid=(s // ts,),
        in_specs=[row, vec, row],
        out_specs=[row, pl.BlockSpec((1, LANES), lambda i: (0, 0)), vec],
        out_shape=[jax.ShapeDtypeStruct((s, d), F32), jax.ShapeDtypeStruct((1, LANES), F32),
                   jax.ShapeDtypeStruct((1, d), F32)],
        compiler_params=_params("arbitrary"),
    )(x, g, target)


def _gate_bwd(dx, y, gate, name):
    s, d = dx.shape
    ts = _tile(s, 256)

    def body(dx_ref, y_ref, g_ref, dy_ref, dgate_ref):
        @pl.when(pl.program_id(0) == 0)
        def _():
            dgate_ref[...] = jnp.zeros_like(dgate_ref)

        dxv = dx_ref[...]
        dy_ref[...] = (dxv * g_ref[...]).astype(dy_ref.dtype)
        dgate_ref[...] += jnp.sum(dxv * y_ref[...].astype(F32), axis=0, keepdims=True)

    row = pl.BlockSpec((ts, d), lambda i: (i, 0))
    vec = pl.BlockSpec((1, d), lambda i: (0, 0))
    return pl.pallas_call(
        body, name=name, grid=(s // ts,),
        in_specs=[row, row, vec], out_specs=[row, vec],
        out_shape=[jax.ShapeDtypeStruct((s, d), BF16), jax.ShapeDtypeStruct((1, d), F32)],
        compiler_params=_params("arbitrary"),
    )(dx, y, gate)


def _conv(v, w_ref, width):
    out = w_ref[width - 1:width, :] * v
    for k in range(width - 1):
        out = out + w_ref[k:k + 1, :] * _shift_down(v, width - 1 - k)
    return out


def _sc_fwd(proj, conv_w, name, deps=()):
    _, s, e = proj.shape
    te = _tile(e, 256)
    width = conv_w.shape[0]

    def body(b_ref, c_ref, v_ref, g_ref, w_ref, o_ref):
        cv = c_ref[...].astype(F32) * v_ref[...].astype(F32)
        u = _conv(cv, w_ref, width)
        gv = g_ref[...].astype(F32)
        o_ref[...] = (b_ref[...].astype(F32) * u * (gv * _sigmoid(gv))).astype(o_ref.dtype)

    def part(q):
        return pl.BlockSpec((None, s, te), lambda j, q=q: (q, 0, j))

    return pl.pallas_call(
        _after(body, 5, deps), name=name, grid=(e // te,),
        in_specs=[part(0), part(1), part(2), part(3), pl.BlockSpec((width, te), lambda j: (0, j))]
        + [ANY] * len(deps),
        out_specs=pl.BlockSpec((s, te), lambda j: (0, j)),
        out_shape=jax.ShapeDtypeStruct((s, e), BF16),
        compiler_params=_params("arbitrary"),
    )(proj, proj, proj, proj, conv_w, *deps)


def _sc_bwd(proj, dyb, conv_w, name, deps=()):
    _, s, e = proj.shape
    te = _tile(e, 256)
    width = conv_w.shape[0]

    def body(b_ref, c_ref, v_ref, g_ref, dy_ref, w_ref, dp_ref, vec_ref):
        bv = b_ref[...].astype(F32)
        cvl = c_ref[...].astype(F32)
        vv = v_ref[...].astype(F32)
        gv = g_ref[...].astype(F32)
        dyv = dy_ref[...].astype(F32)
        cv = cvl * vv
        u = _conv(cv, w_ref, width)
        sg = _sigmoid(gv)
        silu = gv * sg
        dp_ref[0] = (dyv * u * silu).astype(dp_ref.dtype)
        du = dyv * bv * silu
        dp_ref[3] = (dyv * bv * u * (sg * (1.0 + gv * (1.0 - sg)))).astype(dp_ref.dtype)
        dcv = w_ref[width - 1:width, :] * du
        vec_ref[...] = jnp.zeros_like(vec_ref)
        vec_ref[width - 1:width, :] = jnp.sum(du * cv, axis=0, keepdims=True)
        for k in range(width - 1):
            sh = width - 1 - k
            dcv = dcv + w_ref[k:k + 1, :] * _shift_up(du, sh)
            vec_ref[k:k + 1, :] = jnp.sum(du * _shift_down(cv, sh), axis=0, keepdims=True)
        dp_ref[1] = (dcv * vv).astype(dp_ref.dtype)
        dp_ref[2] = (dcv * cvl).astype(dp_ref.dtype)

    def part(q):
        return pl.BlockSpec((None, s, te), lambda j, q=q: (q, 0, j))

    return pl.pallas_call(
        _after(body, 6, deps), name=name, grid=(e // te,),
        in_specs=[part(0), part(1), part(2), part(3), pl.BlockSpec((s, te), lambda j: (0, j)),
                  pl.BlockSpec((width, te), lambda j: (0, j))] + [ANY] * len(deps),
        out_specs=[pl.BlockSpec((4, s, te), lambda j: (0, 0, j)),
                   pl.BlockSpec((8, te), lambda j: (0, j))],
        out_shape=[jax.ShapeDtypeStruct((4, s, e), BF16), jax.ShapeDtypeStruct((8, e), F32)],
        compiler_params=_params("arbitrary"),
    )(proj, proj, proj, proj, dyb, conv_w, *deps)


def _lru_gates(v_pre, w_ref, cb_ref, wa_ref, ba_ref, wx_ref, bx_ref, lam_ref, width):
    v = _conv(v_pre, w_ref, width) + cb_ref[...]
    vb = v.astype(BF16)
    r = _sigmoid(jnp.dot(vb, wa_ref[...], preferred_element_type=F32) + ba_ref[...])
    i = _sigmoid(jnp.dot(vb, wx_ref[...], preferred_element_type=F32) + bx_ref[...])
    nl = -lam_ref[...]
    sp = jnp.maximum(nl, 0.0) + jnp.log1p(jnp.exp(-jnp.abs(nl)))
    log_a = (-RGLRU_C) * r * sp
    a = jnp.exp(log_a)
    one_minus_a2 = jnp.tanh(-log_a) * (1.0 + a * a)
    mult = jnp.sqrt(one_minus_a2)
    return v, vb, r, i, sp, a, mult


def _lru_specs(s, dh, heads, width):
    head_col = lambda q: pl.BlockSpec((None, s, dh), lambda h, q=q: (q, 0, h))
    vec = pl.BlockSpec((1, dh), lambda h: (0, h))
    mat = pl.BlockSpec((None, dh, dh), lambda h: (h, 0, 0))
    weights = [pl.BlockSpec((width, dh), lambda h: (0, h)), vec, mat, vec, mat, vec, vec]
    return head_col, weights


def _lru_fwd(proj, conv_w, conv_b, w_a, b_a, w_x, b_x, lam, name, deps=()):
    _, s, e = proj.shape
    heads, dh, _ = w_a.shape
    width = conv_w.shape[0]

    def body(v_ref, g_ref, w_ref, cb_ref, wa_ref, ba_ref, wx_ref, bx_ref, lam_ref, yb_ref, hs_ref):
        v, _, _, i, _, a, mult = _lru_gates(v_ref[...].astype(F32), w_ref, cb_ref, wa_ref, ba_ref,
                                           wx_ref, bx_ref, lam_ref, width)
        hs = _scan(a, mult * i * v, _shift_down)
        hs_ref[...] = hs
        gv = g_ref[...].astype(F32)
        yb_ref[...] = (hs * (gv * _sigmoid(gv))).astype(yb_ref.dtype)

    head_col, weights = _lru_specs(s, dh, heads, width)
    out = pl.BlockSpec((s, dh), lambda h: (0, h))
    return pl.pallas_call(
        _after(body, 9, deps), name=name, grid=(heads,),
        in_specs=[head_col(0), head_col(1)] + weights + [ANY] * len(deps),
        out_specs=[out, out],
        out_shape=[jax.ShapeDtypeStruct((s, e), BF16), jax.ShapeDtypeStruct((s, e), F32)],
        compiler_params=_params("arbitrary"),
    )(proj, proj, conv_w, conv_b, w_a, b_a, w_x, b_x, lam, *deps)


def _lru_bwd(proj, hs, dyb, conv_w, conv_b, w_a, b_a, w_x, b_x, lam, name, deps=()):
    _, s, e = proj.shape
    heads, dh, _ = w_a.shape
    width = conv_w.shape[0]

    def body(v_ref, g_ref, hs_ref, dy_ref, w_ref, cb_ref, wa_ref, ba_ref, wx_ref, bx_ref, lam_ref,
             dp_ref, dwa_ref, dwx_ref, vec_ref):
        v_pre = v_ref[...].astype(F32)
        v, vb, r, i, sp, a, mult = _lru_gates(v_pre, w_ref, cb_ref, wa_ref, ba_ref, wx_ref, bx_ref,
                                              lam_ref, width)
        hs = hs_ref[...]
        gv = g_ref[...].astype(F32)
        dyv = dy_ref[...].astype(F32)
        sg = _sigmoid(gv)
        dp_ref[1] = (dyv * hs * (sg * (1.0 + gv * (1.0 - sg)))).astype(dp_ref.dtype)
        dhs = dyv * (gv * sg)
        d_h = _scan(_shift_up(a, 1), dhs, _shift_up)
        da = d_h * _shift_down(hs, 1)
        iv = i * v
        dlog_a = da * a - (d_h * iv) * (a * a) / mult
        di = d_h * mult * v
        dv = d_h * mult * i
        dzr = dlog_a * (-RGLRU_C) * sp * r * (1.0 - r)
        dzi = di * i * (1.0 - i)
        dsp = jnp.sum(dlog_a * r, axis=0, keepdims=True) * (-RGLRU_C)
        vec_ref[...] = jnp.zeros_like(vec_ref)
        vec_ref[0:1, :] = jnp.sum(dzr, axis=0, keepdims=True)
        vec_ref[1:2, :] = jnp.sum(dzi, axis=0, keepdims=True)
        vec_ref[2:3, :] = -dsp * _sigmoid(-lam_ref[...])
        dzr_b = dzr.astype(BF16)
        dzi_b = dzi.astype(BF16)
        vt = vb.astype(F32).T.astype(BF16)
        dwa_ref[...] = jnp.dot(vt, dzr_b, preferred_element_type=F32).astype(dwa_ref.dtype)
        dwx_ref[...] = jnp.dot(vt, dzi_b, preferred_element_type=F32).astype(dwx_ref.dtype)
        nt = (((1,), (1,)), ((), ()))
        dv = dv + lax.dot_general(dzr_b, wa_ref[...], nt, preferred_element_type=F32)
        dv = dv + lax.dot_general(dzi_b, wx_ref[...], nt, preferred_element_type=F32)
        vec_ref[3:4, :] = jnp.sum(dv, axis=0, keepdims=True)
        dvp = w_ref[width - 1:width, :] * dv
        vec_ref[4 + width - 1:4 + width, :] = jnp.sum(dv * v_pre, axis=0, keepdims=True)
        for k in range(width - 1):
            sh = width - 1 - k
            dvp = dvp + w_ref[k:k + 1, :] * _shift_up(dv, sh)
            vec_ref[4 + k:5 + k, :] = jnp.sum(dv * _shift_down(v_pre, sh), axis=0, keepdims=True)
        dp_ref[0] = dvp.astype(dp_ref.dtype)

    head_col, weights = _lru_specs(s, dh, heads, width)
    col = pl.BlockSpec((s, dh), lambda h: (0, h))
    mat = pl.BlockSpec((None, dh, dh), lambda h: (h, 0, 0))
    return pl.pallas_call(
        _after(body, 11, deps), name=name, grid=(heads,),
        in_specs=[head_col(0), head_col(1), col, col] + weights + [ANY] * len(deps),
        out_specs=[pl.BlockSpec((2, s, dh), lambda h: (0, 0, h)), mat, mat,
                   pl.BlockSpec((16, dh), lambda h: (0, h))],
        out_shape=[jax.ShapeDtypeStruct((2, s, e), BF16),
                   jax.ShapeDtypeStruct((heads, dh, dh), BF16),
                   jax.ShapeDtypeStruct((heads, dh, dh), BF16),
                   jax.ShapeDtypeStruct((16, e), F32)],
        compiler_params=_params("arbitrary"),
    )(proj, proj, hs, dyb, conv_w, conv_b, w_a, b_a, w_x, b_x, lam, *deps)


def _ada_mod(c_all, w, b, name):
    layers, d, f = w.shape
    nb = c_all.shape[0]

    def body(c_ref, w_ref, b_ref, o_ref):
        cv = c_ref[...]
        sc = cv * _sigmoid(cv)
        o_ref[...] = jnp.dot(sc, w_ref[...], preferred_element_type=F32,
                             precision=lax.Precision.HIGHEST) + b_ref[...]

    return pl.pallas_call(
        body, name=name, grid=(layers,),
        in_specs=[pl.BlockSpec((nb, d), lambda l: (0, 0)),
                  pl.BlockSpec((None, d, f), lambda l: (l, 0, 0)),
                  pl.BlockSpec((None, 1, f), lambda l: (l, 0, 0))],
        out_specs=pl.BlockSpec((None, nb, f), lambda l: (l, 0, 0)),
        out_shape=jax.ShapeDtypeStruct((layers, nb, f), F32),
        compiler_params=_params("arbitrary"),
    )(c_all, w, b)


def _ada_grad(c_all_t, dmod, name):
    d, nb = c_all_t.shape
    layers, _, f = dmod.shape

    def body(c_ref, dm_ref, o_ref):
        cv = c_ref[...]
        sc = cv * _sigmoid(cv)
        acc = sc[:, 0:1] * dm_ref[0:1, :]
        for k in range(1, nb):
            acc = acc + sc[:, k:k + 1] * dm_ref[k:k + 1, :]
        o_ref[...] = acc

    return pl.pallas_call(
        body, name=name, grid=(layers,),
        in_specs=[pl.BlockSpec((d, nb), lambda l: (0, 0)),
                  pl.BlockSpec((None, nb, f), lambda l: (l, 0, 0))],
        out_specs=pl.BlockSpec((None, d, f), lambda l: (l, 0, 0)),
        out_shape=jax.ShapeDtypeStruct((layers, d, f), F32),
        compiler_params=_params("arbitrary"),
    )(c_all_t, dmod)


def _device_sum(g, name):
    _, rows, _ = g.shape

    def body(g_ref, o_ref):
        acc = g_ref[0]
        for k in range(1, N_DEV):
            acc = acc + g_ref[k]
        o_ref[...] = acc

    return pl.pallas_call(
        body, name=name,
        in_specs=[VMEM_SPEC], out_specs=VMEM_SPEC,
        out_shape=jax.ShapeDtypeStruct((rows, LANES), F32),
        compiler_params=pltpu.CompilerParams(vmem_limit_bytes=VMEM_LIMIT),
    )(g)


def _adamw_math(w, g, m, v):
    m = ADAM_B1 * m + (1.0 - ADAM_B1) * g
    v = ADAM_B2 * v + (1.0 - ADAM_B2) * (g * g)
    m_hat = m / (1.0 - ADAM_B1 ** ADAM_STEP)
    v_hat = v / (1.0 - ADAM_B2 ** ADAM_STEP)
    delta = -ADAM_LR * (m_hat / (jnp.sqrt(v_hat) + ADAM_EPS) + ADAM_WD * w)
    return delta, m, v


def _adamw(w, g, m, v, name):
    rows, cols = w.shape
    tr = _tile(rows, 256)

    def body(w_ref, g_ref, m_ref, v_ref, d_ref, mo_ref, vo_ref):
        d_ref[...], mo_ref[...], vo_ref[...] = _adamw_math(w_ref[...], g_ref[...], m_ref[...], v_ref[...])

    blk = pl.BlockSpec((tr, cols), lambda i: (i, 0))
    return pl.pallas_call(
        body, name=name, grid=(rows // tr,),
        in_specs=[blk] * 4, out_specs=[blk] * 3,
        out_shape=[jax.ShapeDtypeStruct((rows, cols), F32)] * 3,
        compiler_params=_params("arbitrary"),
    )(w, g, m, v)


def _adamw_reduced(idx, w, m, v, part, got, recvs, name):
    rows, cols = w.shape
    tr = _tile(rows, 256)
    nr = len(recvs)

    def body(idx_ref, w_ref, m_ref, v_ref, p_ref, q_ref, *rest):
        g_ref, d_ref, mo_ref, vo_ref = rest[nr:]
        g = p_ref[...].astype(F32) + q_ref[...].astype(F32)
        for u_ref in rest[:nr]:
            for j in range(u_ref.shape[0]):
                g = g + u_ref[j].astype(F32)
        g_ref[...] = g
        d_ref[...], mo_ref[...], vo_ref[...] = _adamw_math(w_ref[...], g, m_ref[...], v_ref[...])

    blk = pl.BlockSpec((tr, cols), lambda i, idx: (i, 0))
    grid_spec = pltpu.PrefetchScalarGridSpec(
        num_scalar_prefetch=1, grid=(rows // tr,),
        in_specs=[blk, blk, blk,
                  pl.BlockSpec((None, None, tr, cols), lambda i, idx: (idx[3], idx[4], i, 0)),
                  pl.BlockSpec((None, None, tr, cols), lambda i, idx: (idx[3], 0, i, 0))]
        + [pl.BlockSpec((u.shape[0], tr, cols), lambda i, idx: (0, i, 0)) for u in recvs],
        out_specs=[blk] * 4)
    return pl.pallas_call(
        body, name=name, grid_spec=grid_spec,
        out_shape=[jax.ShapeDtypeStruct((rows, cols), F32)] * 4,
        compiler_params=_params("arbitrary"),
    )(idx, w, m, v, part, got, *recvs)


def _pack(vectors):
    flat = jnp.concatenate([v.reshape(-1).astype(F32) for v in vectors])
    pad = (-flat.shape[0]) % (8 * LANES)
    return jnp.pad(flat, (0, pad)).reshape(-1, LANES)


def _unpack(flat, shapes):
    out, off = [], 0
    for shp in shapes:
        size = math.prod(shp)
        out.append(flat[..., off:off + size].reshape(flat.shape[:-1] + tuple(shp)))
        off += size
    return out


def _my_slice(full, me, axis):
    size = full.shape[axis] // N_DEV
    return lax.dynamic_slice_in_dim(full, me * size, size, axis)


def kernel(x, c, norm_g, ada_w, ada_b, sc_w_in, sc_conv_w, sc_w_out, lru_w_in, lru_conv_w, lru_conv_b, lru_w_a, lru_b_a, lru_w_x, lru_b_x, lru_lambda, lru_w_out, final_g, loss_target, m_norm_g, m_ada_w, m_ada_b, m_sc_w_in, m_sc_conv_w, m_sc_w_out, m_lru_w_in, m_lru_conv_w, m_lru_conv_b, m_lru_w_a, m_lru_b_a, m_lru_w_x, m_lru_b_x, m_lru_lambda, m_lru_w_out, m_final_g, v_norm_g, v_ada_w, v_ada_b, v_sc_w_in, v_sc_conv_w, v_sc_w_out, v_lru_w_in, v_lru_conv_w, v_lru_conv_b, v_lru_w_a, v_lru_b_a, v_lru_w_x, v_lru_b_x, v_lru_lambda, v_lru_w_out, v_final_g):
    _, s, d = x.shape
    e = sc_w_out.shape[1] * N_DEV
    heads, dh_s, dh = lru_w_a.shape[1:]
    es = e // N_DEV
    f = ada_w.shape[2]
    mx, my, mc = _position()
    me = 4 * mx + 2 * my + mc
    chip = 2 * mx + my
    idx = jnp.stack([chip ^ 1, chip ^ 2, chip ^ 3, chip, mc]).astype(jnp.int32)

    x0 = x[0]
    target = loss_target[0]

    small_shapes = [(d,), (3, es), (4, es), (es,), (heads, dh_s), (heads, dh_s), (es,)]
    small = _small_gather(_pack([c, sc_conv_w, lru_conv_w, lru_conv_b, lru_b_a, lru_b_x, lru_lambda]),
                          "gather_small_weights").reshape(N_DEV, -1)
    c_all, cw3, cw4, cb, ba, bx, lam = _unpack(small, small_shapes)
    cw3 = cw3.transpose(1, 0, 2).reshape(3, e)
    cw4 = cw4.transpose(1, 0, 2).reshape(4, e)
    cb = cb.reshape(1, e)
    lam = lam.reshape(1, e)
    ba = ba.transpose(1, 0, 2).reshape(1, e)
    bx = bx.transpose(1, 0, 2).reshape(1, e)

    ada_b_mine = _my_slice(ada_b, me, 1).reshape(2, 1, f)
    mod_mine = _ada_mod(c_all, ada_w, ada_b_mine, "ada_mod")
    mod_all = _small_gather(_pack([mod_mine]), "gather_mod")

    shards = [sc_w_in[0].astype(BF16), sc_w_out[0].astype(BF16), lru_w_in[0].astype(BF16),
              lru_w_a[0].reshape(heads * dh_s, dh).astype(BF16),
              lru_w_x[0].reshape(heads * dh_s, dh).astype(BF16), lru_w_out[0].astype(BF16)]
    lands = [lax.dynamic_update_slice(lax.empty((N_DEV,) + sh.shape, BF16), sh[None], (me, 0, 0))
             for sh in shards]
    every = [1, 2, 3, 0]
    units = [([0], [0]), ([0], [1, 2]), ([0], [3]), ([1], every), ([2, 3, 4], every), ([5], every)]
    sems, shards, lands, started = _gather_start(shards, lands, units, [mod_all], "gather_start")

    def gathered(u, after_forward, name):
        members, ks = units[u]
        fwd, shs, lnd, token = _gather_forward(
            [shards[i] for i in members], [lands[i] for i in members], ks, sems[u][0], sems[u][1],
            after_forward, "gather_forward_" + name)
        for i, sh, ld in zip(members, shs, lnd):
            shards[i], lands[i] = sh, ld

        def finish(after):
            out = _gather_finish([lands[i] for i in members], ks, fwd, after, "gather_finish_" + name)
            for i, ld in zip(members, out):
                lands[i] = ld
            return out

        return token, finish

    mod_all = mod_all.reshape(N_DEV, -1)
    mod_all = mod_all[:, :2 * N_DEV * f].reshape(N_DEV, 2, N_DEV, f)
    mod_all = mod_all.transpose(1, 2, 0, 3).reshape(2, N_DEV, 3 * d)
    mod = lax.dynamic_index_in_dim(mod_all, me, 1, keepdims=False)
    shift = [mod[l:l + 1, 0:d] for l in range(2)]
    scale = [mod[l:l + 1, d:2 * d] for l in range(2)]
    gate = [mod[l:l + 1, 2 * d:3 * d] for l in range(2)]
    ng = [norm_g[l:l + 1] for l in range(2)]
    fg = final_g.reshape(1, d)

    h0 = _norm_mod(x0, ng[0], scale[0], shift[0], "norm_mod_0", deps=[started])
    proj0 = lax.empty((4, s, e), BF16)
    tok, _ = gathered(0, [h0], "sc_w_in_own")
    proj0 = _mm_proj_group(h0, lands[0], idx, 3, proj0, "mm_proj_0_own", deps=[tok])
    tok, finish = gathered(1, [proj0], "sc_w_in_near")
    finish([tok])
    proj0 = _mm_proj_group(h0, lands[0], idx, 0, proj0, "mm_proj_0_near_y")
    proj0 = _mm_proj_group(h0, lands[0], idx, 1, proj0, "mm_proj_0_near_x")
    tok, finish = gathered(2, [proj0], "sc_w_in_far")
    wg_in0, = finish([tok])
    proj0 = _mm_proj_group(h0, wg_in0, idx, 2, proj0, "mm_proj_0_far")
    tok, finish = gathered(3, [proj0], "sc_w_out")
    yb0 = _sc_fwd(proj0, cw3, "sc_fwd", deps=[tok])
    w_out0 = finish([yb0])[0].reshape(e, d)
    x1, y0 = _mm_out(yb0, w_out0, x0, gate[0], "mm_out_0")
    tok, finish = gathered(4, [x1], "lru_in")
    h1 = _norm_mod(x1, ng[1], scale[1], shift[1], "norm_mod_1", deps=[tok])
    wg_in1, wg_a, wg_x = finish([h1])
    w_a = wg_a.reshape(N_DEV, heads, dh_s, dh).transpose(1, 0, 2, 3).reshape(heads, dh, dh)
    w_x = wg_x.reshape(N_DEV, heads, dh_s, dh).transpose(1, 0, 2, 3).reshape(heads, dh, dh)
    proj1 = _mm_proj(h1, wg_in1, 2, "mm_proj_1")
    tok, finish = gathered(5, [proj1], "lru_w_out")
    yb1, hs = _lru_fwd(proj1, cw4, cb, w_a, ba, w_x, bx, lam, "lru_fwd", deps=[tok])
    w_out1 = finish([yb1])[0].reshape(e, d)
    x2, y1 = _mm_out(yb1, w_out1, x1, gate[1], "mm_out_1")
    dx2, loss_part, d_fg = _final_loss(x2, fg, target, "final_loss")

    def pieces(g, rows, cols):
        return g.reshape(4, 2, rows, cols)

    def by_rows(g):
        return g.reshape(heads, N_DEV, dh_s, dh).transpose(1, 0, 2, 3).reshape(N_DEV, heads * dh_s, dh)

    def pair_begin(parts, group):
        send, recv, parts, lnd, token = _pair_start(parts, "pair_start_" + group)
        return dict(parts=parts, lands=lnd, send=send, recv=recv, group=group), token

    def scatter_start(pair, names, after):
        group = pair["group"]
        parts, gots = _pair_wait(pair["parts"], pair["lands"], pair["send"], pair["recv"], after,
                                 "pair_wait_" + group)
        sums = [_pair_sum(idx, p, q, "pair_sum_" + nm) for p, q, nm in zip(parts, gots, names)]
        empties = [lax.empty(sm.shape, sm.dtype) for sm in sums]
        send, recv, sums, lnd, token = _chip_start(sums, empties, "chip_start_" + group)
        return dict(parts=parts, gots=gots, names=names, group=group, sums=sums, lands=lnd,
                    send=send, recv=recv), token

    big = {"sc_w_in": (sc_w_in, m_sc_w_in, v_sc_w_in), "sc_w_out": (sc_w_out, m_sc_w_out, v_sc_w_out),
           "lru_w_in": (lru_w_in, m_lru_w_in, v_lru_w_in), "lru_w_a": (lru_w_a, m_lru_w_a, v_lru_w_a),
           "lru_w_x": (lru_w_x, m_lru_w_x, v_lru_w_x), "lru_w_out": (lru_w_out, m_lru_w_out, v_lru_w_out)}
    big_res = {}

    def scatter_finish(rs, after):
        recvs = _chip_wait(rs["sums"], rs["lands"], rs["send"], rs["recv"], after, "chip_wait_" + rs["group"])
        done = []
        for p, q, u, nm in zip(rs["parts"], rs["gots"], recvs, rs["names"]):
            w, m, v = big[nm]
            shp2 = p.shape[2:]
            res = _adamw_reduced(idx, w.reshape(shp2), m.reshape(shp2), v.reshape(shp2), p, q, [u], "adamw_" + nm)
            big_res[nm] = [r.reshape(w.shape) for r in res]
            done.append(res[1])
        return done

    dy1, dgate1 = _gate_bwd(dx2, y1, gate[1], "gate_bwd_1")
    dw_out1 = _mm_tn(yb1, dy1[None], 1, "mm_dw_out_1")
    pair, tok = pair_begin([pieces(dw_out1, es, d)], "lru_w_out")
    dyb1 = _mm_nt(dy1[None], w_out1[None], BF16, "mm_dyb_1", deps=[tok])
    rs1, tok = scatter_start(pair, ["lru_w_out"], [dyb1])
    dproj1, dw_a, dw_x, vecs1 = _lru_bwd(proj1, hs, dyb1, cw4, cb, w_a, ba, w_x, bx, lam, "lru_bwd", deps=[tok])
    done = scatter_finish(rs1, [dproj1])
    dw_in1 = _mm_tn(h1, dproj1, N_DEV, "mm_dw_in_1", deps=done)
    pair, tok = pair_begin([pieces(dw_in1, d, 2 * es), pieces(by_rows(dw_a), heads * dh_s, dh),
                            pieces(by_rows(dw_x), heads * dh_s, dh)], "lru_in")
    dh1 = _mm_nt(dproj1, wg_in1, F32, "mm_dh_1", deps=[tok])
    rs2, tok = scatter_start(pair, ["lru_w_in", "lru_w_a", "lru_w_x"], [dh1])
    dx1, dscale1, dshift1, dng1 = _norm_mod_bwd(x1, dh1, dx2, ng[1], scale[1], "norm_mod_bwd_1", deps=[tok])
    dy0, dgate0 = _gate_bwd(dx1, y0, gate[0], "gate_bwd_0")
    dw_out0 = _mm_tn(yb0, dy0[None], 1, "mm_dw_out_0")
    pair, tok = pair_begin([pieces(dw_out0, es, d)], "sc_w_out")
    dyb0 = _mm_nt(dy0[None], w_out0[None], BF16, "mm_dyb_0", deps=[tok])
    rs3, tok = scatter_start(pair, ["sc_w_out"], [dyb0])
    dproj0, vecs0 = _sc_bwd(proj0, dyb0, cw3, "sc_bwd", deps=[tok])
    early_shapes = [(LANES,), (d,), (3 * d,), (d,), (d,), (8, e), (16, e)]
    early_all = _small_gather(
        _pack([loss_part, dgate0, jnp.concatenate([dshift1, dscale1, dgate1], axis=1), dng1, d_fg, vecs0, vecs1]),
        "gather_small_grads_early")
    done = scatter_finish(rs2, [early_all])
    idx_one = jnp.stack([jnp.zeros_like(mc)] * 4 + [mc]).astype(jnp.int32)
    sc_w_in_steps = []

    def chip_step(j, pair, after):
        (part,), (got,) = _pair_wait(pair["parts"], pair["lands"], pair["send"], pair["recv"], after,
                                     "pair_wait_sc_w_in_%d" % j)
        sm = _pair_sum(idx_one, part, got, "pair_sum_sc_w_in_%d" % j, nslots=1)
        send, recv, sums, lnd, token = _chip_start([sm], [lax.empty(sm.shape, sm.dtype)],
                                                   "chip_start_sc_w_in_%d" % j, flips=(j,))
        sc_w_in_steps.append((sums, lnd, send, recv, j))
        return token

    pending = None
    for j in (1, 2, 3, 0):
        part = _mm_tn_group(h0, dproj0, idx, (j - 1) % 4, 2, "mm_dw_in_0_%d" % j, deps=done)[None]
        pair, tok = pair_begin([part], "sc_w_in_%d" % j)
        done = [tok]
        if pending is not None:
            done.append(chip_step(pending[0], pending[1], [tok]))
        pending = (j, pair)
    dh0 = _mm_nt(dproj0, wg_in0, F32, "mm_dh_0", deps=done)
    pair = pending[1]
    (part,), (got,) = _pair_wait(pair["parts"], pair["lands"], pair["send"], pair["recv"], [dh0],
                                 "pair_wait_sc_w_in_0")
    dx0, dscale0, dshift0, dng0 = _norm_mod_bwd(x0, dh0, dx1, ng[0], scale[0], "norm_mod_bwd_0")
    done = scatter_finish(rs3, [dx0])
    early_sum = _device_sum(early_all, "sum_small_grads_early").reshape(-1)
    late_shapes = [(d,), (d,), (d,)]
    late_all = _small_gather(_pack([dshift0, dscale0, dng0]), "gather_small_grads_late", deps=done)
    late_sum = _device_sum(late_all, "sum_small_grads_late").reshape(-1)

    loss_v, s_dgate0, s_dmod1, s_dng1, g_final_g, sum0, sum1 = _unpack(early_sum, early_shapes)
    s_dshift0, s_dscale0, s_dng0 = _unpack(late_sum, late_shapes)
    loss = loss_v[0]
    g_ada_b = jnp.stack([jnp.concatenate([s_dshift0, s_dscale0, s_dgate0]), s_dmod1])
    g_norm_g = jnp.stack([s_dng0, s_dng1])
    _, a_dgate0, a_dmod1 = _unpack(early_all.reshape(N_DEV, -1), early_shapes)[:3]
    a_dshift0, a_dscale0, _ = _unpack(late_all.reshape(N_DEV, -1), late_shapes)
    dmod_all = jnp.stack([jnp.concatenate([a_dshift0, a_dscale0, a_dgate0], axis=1), a_dmod1])
    dmod_cols = _my_slice(dmod_all, me, 2)
    g_ada_w = _ada_grad(c_all.T, dmod_cols, "ada_grad")

    g_sc_conv_w = _my_slice(sum0[0:3], me, 1)
    g_lru_b_a = _my_slice(sum1[0].reshape(heads, dh), me, 1)
    g_lru_b_x = _my_slice(sum1[1].reshape(heads, dh), me, 1)
    g_lru_lambda = _my_slice(sum1[2:3], me, 1)
    g_lru_conv_b = _my_slice(sum1[3:4], me, 1)
    g_lru_conv_w = _my_slice(sum1[4:8], me, 1)

    ada_res = _adamw(ada_w.reshape(2 * d, f), g_ada_w.reshape(2 * d, f), m_ada_w.reshape(2 * d, f),
                     v_ada_w.reshape(2 * d, f), "adamw_ada_w")
    ada_out = [g_ada_w] + [r.reshape(ada_w.shape) for r in ada_res]

    small_w = [norm_g, ada_b, final_g, sc_conv_w, lru_conv_w, lru_conv_b, lru_b_a, lru_b_x, lru_lambda]
    small_m = [m_norm_g, m_ada_b, m_final_g, m_sc_conv_w, m_lru_conv_w, m_lru_conv_b, m_lru_b_a, m_lru_b_x,
               m_lru_lambda]
    small_v = [v_norm_g, v_ada_b, v_final_g, v_sc_conv_w, v_lru_conv_w, v_lru_conv_b, v_lru_b_a, v_lru_b_x,
               v_lru_lambda]
    small_g = [g_norm_g, g_ada_b, g_final_g, g_sc_conv_w, g_lru_conv_w, g_lru_conv_b, g_lru_b_a, g_lru_b_x,
               g_lru_lambda]
    small_g = [g.reshape(w.shape) for g, w in zip(small_g, small_w)]
    shapes = [w.shape for w in small_w]
    packed = _adamw(_pack(small_w), _pack(small_g), _pack(small_m), _pack(small_v), "adamw_small")
    small_out = [small_g] + [_unpack(p.reshape(-1), shapes) for p in packed]

    after = [packed[0], ada_res[0]]
    recvs = []
    for sums, lnd, send, recv, j in sc_w_in_steps:
        recvs += _chip_wait(sums, lnd, send, recv, after, "chip_wait_sc_w_in_%d" % j)
    shp2 = part.shape[2:]
    res = _adamw_reduced(idx_one, sc_w_in.reshape(shp2), m_sc_w_in.reshape(shp2), v_sc_w_in.reshape(shp2),
                         part, got, recvs, "adamw_sc_w_in")
    big_res["sc_w_in"] = [r.reshape(sc_w_in.shape) for r in res]
    big_out = [big_res[nm] for nm in ("sc_w_in", "sc_w_out", "lru_w_in", "lru_w_a", "lru_w_x", "lru_w_out")]

    def small(kind, i):
        return small_out[kind][i]

    def bigw(kind, i):
        return big_out[i][kind]

    outs = [loss, dx0[None]]
    for kind in range(4):
        outs += [small(kind, 0), ada_out[kind], small(kind, 1), bigw(kind, 0), small(kind, 3), bigw(kind, 1),
                 bigw(kind, 2), small(kind, 4), small(kind, 5), bigw(kind, 3), small(kind, 6), bigw(kind, 4),
                 small(kind, 7), small(kind, 8), bigw(kind, 5), small(kind, 2)]
    return tuple(outs)
```

```python
import math

import jax
import jax.numpy as jnp
from jax import lax
from jax.experimental import pallas as pl
from jax.experimental.pallas import tpu as pltpu

N_DEV = 8
LANES = 128
EPS = 1e-6
RGLRU_C = 8.0
ADAM_LR = 0.001
ADAM_B1 = 0.9
ADAM_B2 = 0.999
ADAM_EPS = 1e-08
ADAM_WD = 0.01
ADAM_STEP = 10
VMEM_LIMIT = 56 * 1024 * 1024
MESH = pl.DeviceIdType.MESH
F32 = jnp.float32
BF16 = jnp.bfloat16
ANY = pl.BlockSpec(memory_space=pl.ANY)
HBM = pl.BlockSpec(memory_space=pltpu.HBM)
SEM = pl.BlockSpec(memory_space=pltpu.SEMAPHORE)
VMEM_SPEC = pl.BlockSpec(memory_space=pltpu.VMEM)
EFFECT = pltpu.SideEffectType.DATAFLOW_SIDE_EFFECTING
TOKEN = jax.ShapeDtypeStruct((8, LANES), jnp.float32)


def _tile(n, pref):
    t = min(n, pref)
    assert n % t == 0, (n, pref)
    return t


def _params(*sem):
    return pltpu.CompilerParams(dimension_semantics=sem, vmem_limit_bytes=VMEM_LIMIT)


def _position():
    return lax.axis_index("x"), lax.axis_index("y"), lax.axis_index("c")


def _flip(x, y, k):
    return (1 - x if k & 2 else x), (1 - y if k & 1 else y)


def _after(body, n_in, deps):
    if not deps:
        return body

    def wrapped(*refs):
        return body(*refs[:n_in], *refs[n_in + len(deps):])

    return wrapped


def _small_gather(v, name, deps=()):
    rows = v.shape[0]

    def body(v_ref, out_ref, send_sems, recv_sems):
        x, y, c = _position()
        me = 4 * x + 2 * y + c
        out_ref[me] = v_ref[...]
        copies = []
        for k in range(1, N_DEV):
            px, py = _flip(x, y, k >> 1)
            pc = 1 - c if k & 1 else c
            cp = pltpu.make_async_remote_copy(
                src_ref=v_ref, dst_ref=out_ref.at[me],
                send_sem=send_sems.at[k - 1], recv_sem=recv_sems.at[k - 1],
                device_id=(px, py, pc), device_id_type=MESH)
            cp.start()
            copies.append((cp, 4 * px + 2 * py + pc))
        for k, (cp, peer) in enumerate(copies):
            pltpu.make_async_remote_copy(
                src_ref=v_ref, dst_ref=out_ref.at[peer],
                send_sem=send_sems.at[k], recv_sem=recv_sems.at[k],
                device_id=(x, y, c), device_id_type=MESH).wait_recv()
        for cp, _ in copies:
            cp.wait_send()

    return pl.pallas_call(
        _after(body, 1, deps), name=name,
        out_shape=jax.ShapeDtypeStruct((N_DEV, rows, LANES), F32),
        in_specs=[VMEM_SPEC] + [ANY] * len(deps), out_specs=VMEM_SPEC,
        scratch_shapes=[pltpu.SemaphoreType.DMA((N_DEV - 1,)),
                        pltpu.SemaphoreType.DMA((N_DEV - 1,))],
        compiler_params=pltpu.CompilerParams(vmem_limit_bytes=VMEM_LIMIT),
    )(v, *deps)


def _hbm(a):
    return pltpu.with_memory_space_constraint(a, pltpu.HBM)


def _hbm_like(arrays):
    return [pltpu.HBM(a.shape, a.dtype) for a in arrays]


def _remote(src, dst, send, recv, to):
    return pltpu.make_async_remote_copy(src_ref=src, dst_ref=dst, send_sem=send, recv_sem=recv,
                                        device_id=to, device_id_type=MESH)


def _gather_start(shards, lands, units, after, name):
    n, nu = len(shards), len(units)

    def body(*refs):
        ins, lnd = refs[:n], refs[n:2 * n]
        sems = refs[2 * n + len(after):2 * n + len(after) + 2 * nu]
        token = refs[-1]
        x, y, c = _position()
        me = 4 * x + 2 * y + c
        targets = [(x, y, 1 - c)] + [(px, py, c) for px, py in (_flip(x, y, k) for k in (1, 2, 3))]
        for u, (members, ks) in enumerate(units):
            for slot, i in enumerate(members):
                for ki, k in enumerate(ks):
                    at = len(ks) * slot + ki
                    _remote(ins[i], lnd[i].at[me], sems[2 * u].at[at], sems[2 * u + 1].at[at], targets[k]).start()
        token[...] = jnp.zeros_like(token)

    sem_shapes = []
    for members, ks in units:
        count = len(members) * len(ks)
        sem_shapes += [pltpu.SemaphoreType.DMA((count,)), pltpu.SemaphoreType.DMA((count,))]
    out = pl.pallas_call(
        body, name=name,
        out_shape=sem_shapes + _hbm_like(shards) + _hbm_like(lands) + [TOKEN],
        in_specs=[HBM] * (2 * n) + [ANY] * len(after),
        out_specs=[SEM] * (2 * nu) + [HBM] * (2 * n) + [VMEM_SPEC],
        input_output_aliases={i: 2 * nu + i for i in range(2 * n)},
        compiler_params=pltpu.CompilerParams(has_side_effects=EFFECT),
    )(*[_hbm(s) for s in shards], *[_hbm(l) for l in lands], *after)
    sems = [(out[2 * u], out[2 * u + 1]) for u in range(nu)]
    return sems, list(out[2 * nu:2 * nu + n]), list(out[2 * nu + n:2 * nu + 2 * n]), out[-1]


def _gather_forward(shards, lands, ks, send, recv, after, name):
    m = len(shards)
    hops = [k for k in ks if k]
    nsem = 2 if hops else 0

    def body(*refs):
        ins, lnd = refs[:m], refs[m:2 * m]
        send_ref, recv_ref = refs[2 * m], refs[2 * m + 1]
        outs = refs[2 * m + 2 + len(after):]
        token = refs[-1]
        x, y, c = _position()
        me = (x, y, c)
        for slot in range(m):
            for ki, k in enumerate(ks):
                at = len(ks) * slot + ki
                if k:
                    px, py = _flip(x, y, k)
                    block = lnd[slot].at[4 * px + 2 * py + c]
                else:
                    block = lnd[slot].at[4 * x + 2 * y + (1 - c)]
                arrival = _remote(ins[slot], block, send_ref.at[at], recv_ref.at[at], me)
                arrival.wait_recv()
                if k:
                    fat = len(hops) * slot + hops.index(k)
                    _remote(block, block, outs[0].at[fat], outs[1].at[fat], (x, y, 1 - c)).start()
                arrival.wait_send()
        token[...] = jnp.zeros_like(token)

    count = len(hops) * m
    sem_shapes = [pltpu.SemaphoreType.DMA((count,)), pltpu.SemaphoreType.DMA((count,))] if hops else []
    out = pl.pallas_call(
        body, name=name,
        out_shape=sem_shapes + _hbm_like(shards) + _hbm_like(lands) + [TOKEN],
        in_specs=[HBM] * (2 * m) + [SEM, SEM] + [ANY] * len(after),
        out_specs=[SEM] * nsem + [HBM] * (2 * m) + [VMEM_SPEC],
        input_output_aliases={i: nsem + i for i in range(2 * m)},
        compiler_params=pltpu.CompilerParams(has_side_effects=EFFECT),
    )(*shards, *lands, send, recv, *after)
    fwd = (out[0], out[1]) if hops else None
    return fwd, list(out[nsem:nsem + m]), list(out[nsem + m:nsem + 2 * m]), out[-1]


def _gather_finish(lands, ks, fwd, after, name):
    m = len(lands)
    hops = [k for k in ks if k]

    def body(*refs):
        lnd = refs[:m]
        fsend_ref, frecv_ref = refs[m], refs[m + 1]
        x, y, c = _position()
        for slot in range(m):
            for fi, k in enumerate(hops):
                px, py = _flip(x, y, k)
                sent = lnd[slot].at[4 * px + 2 * py + c]
                came = lnd[slot].at[4 * px + 2 * py + (1 - c)]
                fat = len(hops) * slot + fi
                cp = _remote(sent, came, fsend_ref.at[fat], frecv_ref.at[fat], (x, y, c))
                cp.wait_recv()
                cp.wait_send()

    out = pl.pallas_call(
        body, name=name,
        out_shape=_hbm_like(lands),
        in_specs=[HBM] * m + [SEM, SEM] + [ANY] * len(after), out_specs=[HBM] * m,
        input_output_aliases={i: i for i in range(m)},
        compiler_params=pltpu.CompilerParams(has_side_effects=EFFECT),
    )(*lands, fwd[0], fwd[1], *after)
    return list(out)


def _pair_start(parts, name):
    n = len(parts)
    lands = [lax.empty((p.shape[0], 1) + p.shape[2:], p.dtype) for p in parts]

    def body(*refs):
        ins, lnd = refs[:n], refs[n:2 * n]
        send_ref, recv_ref = refs[2 * n], refs[2 * n + 1]
        token = refs[-1]
        x, y, c = _position()
        for i in range(n):
            _remote(ins[i].at[:, pl.ds(1 - c, 1)], lnd[i], send_ref.at[i], recv_ref.at[i], (x, y, 1 - c)).start()
        token[...] = jnp.zeros_like(token)

    out = pl.pallas_call(
        body, name=name,
        out_shape=[pltpu.SemaphoreType.DMA((n,)), pltpu.SemaphoreType.DMA((n,))]
        + _hbm_like(parts) + _hbm_like(lands) + [TOKEN],
        in_specs=[HBM] * (2 * n), out_specs=[SEM, SEM] + [HBM] * (2 * n) + [VMEM_SPEC],
        input_output_aliases={i: 2 + i for i in range(2 * n)},
        compiler_params=pltpu.CompilerParams(has_side_effects=EFFECT),
    )(*[_hbm(p) for p in parts], *[_hbm(l) for l in lands])
    return out[0], out[1], list(out[2:2 + n]), list(out[2 + n:2 + 2 * n]), out[-1]


def _pair_wait(parts, lands, send, recv, after, name):
    n = len(parts)

    def body(*refs):
        ins, lnd = refs[:n], refs[n:2 * n]
        send_ref, recv_ref = refs[2 * n], refs[2 * n + 1]
        x, y, c = _position()
        for i in range(n):
            cp = _remote(ins[i].at[:, pl.ds(1 - c, 1)], lnd[i], send_ref.at[i], recv_ref.at[i], (x, y, c))
            cp.wait_recv()
            cp.wait_send()

    out = pl.pallas_call(
        body, name=name,
        out_shape=_hbm_like(parts) + _hbm_like(lands),
        in_specs=[HBM] * (2 * n) + [SEM, SEM] + [ANY] * len(after), out_specs=[HBM] * (2 * n),
        input_output_aliases={i: i for i in range(2 * n)},
        compiler_params=pltpu.CompilerParams(has_side_effects=EFFECT),
    )(*parts, *lands, send, recv, *after)
    return list(out[:n]), list(out[n:])


def _chip_start(sums, lands, name, flips=(1, 2, 3)):
    n, ns = len(sums), len(flips)

    def body(*refs):
        ins, lnd = refs[:n], refs[n:2 * n]
        send_ref, recv_ref = refs[2 * n], refs[2 * n + 1]
        token = refs[-1]
        x, y, c = _position()
        for i in range(n):
            for j, flip in enumerate(flips):
                px, py = _flip(x, y, flip)
                _remote(ins[i].at[j], lnd[i].at[j], send_ref.at[ns * i + j], recv_ref.at[ns * i + j],
                        (px, py, c)).start()
        token[...] = jnp.zeros_like(token)

    out = pl.pallas_call(
        body, name=name,
        out_shape=[pltpu.SemaphoreType.DMA((ns * n,)), pltpu.SemaphoreType.DMA((ns * n,))]
        + _hbm_like(sums) + _hbm_like(lands) + [TOKEN],
        in_specs=[HBM] * (2 * n), out_specs=[SEM, SEM] + [HBM] * (2 * n) + [VMEM_SPEC],
        input_output_aliases={i: 2 + i for i in range(2 * n)},
        compiler_params=pltpu.CompilerParams(has_side_effects=EFFECT),
    )(*[_hbm(s) for s in sums], *[_hbm(l) for l in lands])
    return out[0], out[1], out[2:2 + n], out[2 + n:2 + 2 * n], out[-1]


def _chip_wait(sums, lands, send, recv, after, name):
    n, ns = len(sums), sums[0].shape[0]

    def body(*refs):
        ins, lnd = refs[:n], refs[n:2 * n]
        send_ref, recv_ref = refs[2 * n], refs[2 * n + 1]
        x, y, c = _position()
        for i in range(n):
            for j in range(ns):
                cp = _remote(ins[i].at[j], lnd[i].at[j], send_ref.at[ns * i + j], recv_ref.at[ns * i + j], (x, y, c))
                cp.wait_recv()
                cp.wait_send()

    out = pl.pallas_call(
        body, name=name,
        out_shape=_hbm_like(sums) + _hbm_like(lands),
        in_specs=[HBM] * (2 * n) + [SEM, SEM] + [ANY] * len(after), out_specs=[HBM] * (2 * n),
        input_output_aliases={i: i for i in range(2 * n)},
        compiler_params=pltpu.CompilerParams(has_side_effects=EFFECT),
    )(*sums, *lands, send, recv, *after)
    return list(out[n:])


def _pair_sum(idx, part, got, name, nslots=3):
    _, _, rows, cols = part.shape
    tr = _tile(rows, 512)

    def body(idx_ref, p_ref, q_ref, o_ref):
        o_ref[...] = (p_ref[...].astype(F32) + q_ref[...].astype(F32)).astype(o_ref.dtype)

    grid_spec = pltpu.PrefetchScalarGridSpec(
        num_scalar_prefetch=1, grid=(nslots, rows // tr),
        in_specs=[pl.BlockSpec((None, None, tr, cols), lambda j, r, idx: (idx[j], idx[4], r, 0)),
                  pl.BlockSpec((None, None, tr, cols), lambda j, r, idx: (idx[j], 0, r, 0))],
        out_specs=pl.BlockSpec((None, tr, cols), lambda j, r, idx: (j, r, 0)))
    return pl.pallas_call(
        body, name=name, grid_spec=grid_spec,
        out_shape=jax.ShapeDtypeStruct((nslots, rows, cols), part.dtype),
        compiler_params=_params("arbitrary", "arbitrary"),
    )(idx, part, got)


def _mm_proj(h, wg, groups, name):
    s, k = h.shape
    nchunk, _, n = wg.shape
    e = nchunk * n // groups
    tn = _tile(min(n, e), 512)

    def body(h_ref, w_ref, o_ref):
        o_ref[...] = jnp.dot(h_ref[...], w_ref[...], preferred_element_type=F32).astype(o_ref.dtype)

    return pl.pallas_call(
        body, name=name, grid=(nchunk * n // tn,),
        in_specs=[pl.BlockSpec((s, k), lambda j: (0, 0)),
                  pl.BlockSpec((None, k, tn), lambda j: ((j * tn) // n, 0, ((j * tn) % n) // tn))],
        out_specs=pl.BlockSpec((None, s, tn), lambda j: ((j * tn) // e, 0, ((j * tn) % e) // tn)),
        out_shape=jax.ShapeDtypeStruct((groups, s, e), BF16),
        compiler_params=_params("arbitrary"),
    )(h, wg)


def _mm_proj_group(h, wg, idx, pos, prev, name, deps=()):
    s, k = h.shape
    nchunk, _, n = wg.shape
    groups, _, e = prev.shape
    per = nchunk // groups
    assert per * n == e
    tn = _tile(n, 512)
    nd = len(deps)

    def body(idx_ref, h_ref, w_ref, prev_ref, *rest):
        o_ref = rest[nd]
        o_ref[...] = jnp.dot(h_ref[...], w_ref[...], preferred_element_type=F32).astype(o_ref.dtype)

    grid_spec = pltpu.PrefetchScalarGridSpec(
        num_scalar_prefetch=1, grid=(e // tn,),
        in_specs=[pl.BlockSpec((s, k), lambda j, idx: (0, 0)),
                  pl.BlockSpec((None, k, tn), lambda j, idx: (per * idx[pos] + (j * tn) // n, 0, ((j * tn) % n) // tn)),
                  ANY] + [ANY] * nd,
        out_specs=pl.BlockSpec((None, s, tn), lambda j, idx: (idx[pos], 0, j)))
    return pl.pallas_call(
        body, name=name, grid_spec=grid_spec,
        out_shape=jax.ShapeDtypeStruct(prev.shape, prev.dtype),
        input_output_aliases={3: 0},
        compiler_params=_params("arbitrary"),
    )(idx, h, wg, prev, *deps)


def _mm_out(yb, w, x, gate, name):
    s, k = yb.shape
    d = w.shape[1]
    tn = _tile(d, 512)
    tk = _tile(k, 1024)
    nk = k // tk

    def body(a_ref, w_ref, x_ref, g_ref, xo_ref, y_ref, acc_ref):
        kk = pl.program_id(1)

        @pl.when(kk == 0)
        def _():
            acc_ref[...] = jnp.zeros_like(acc_ref)

        acc_ref[...] += jnp.dot(a_ref[...], w_ref[...], preferred_element_type=F32)

        @pl.when(kk == nk - 1)
        def _():
            y = acc_ref[...]
            y_ref[...] = y.astype(y_ref.dtype)
            xo_ref[...] = x_ref[...] + g_ref[...] * y

    return pl.pallas_call(
        body, name=name, grid=(d // tn, nk),
        in_specs=[pl.BlockSpec((s, tk), lambda j, kk: (0, kk)),
                  pl.BlockSpec((tk, tn), lambda j, kk: (kk, j)),
                  pl.BlockSpec((s, tn), lambda j, kk: (0, j)),
                  pl.BlockSpec((1, tn), lambda j, kk: (0, j))],
        out_specs=[pl.BlockSpec((s, tn), lambda j, kk: (0, j)),
                   pl.BlockSpec((s, tn), lambda j, kk: (0, j))],
        out_shape=[jax.ShapeDtypeStruct((s, d), F32), jax.ShapeDtypeStruct((s, d), BF16)],
        scratch_shapes=[pltpu.VMEM((s, tn), F32)],
        compiler_params=_params("arbitrary", "arbitrary"),
    )(yb, w, x, gate)


def _mm_nt(a3, w3, out_dtype, name, deps=()):
    g, s, ea = a3.shape
    cw, n, nw = w3.shape
    total = g * ea
    assert total == cw * nw
    tk = _tile(min(ea, nw), 1024)
    tn = _tile(n, 1024)
    nk = total // tk

    def body(a_ref, w_ref, o_ref, acc_ref):
        kk = pl.program_id(1)

        @pl.when(kk == 0)
        def _():
            acc_ref[...] = jnp.zeros_like(acc_ref)

        acc_ref[...] += lax.dot_general(a_ref[...], w_ref[...], (((1,), (1,)), ((), ())),
                                        preferred_element_type=F32)

        @pl.when(kk == nk - 1)
        def _():
            o_ref[...] = acc_ref[...].astype(o_ref.dtype)

    return pl.pallas_call(
        _after(body, 2, deps), name=name, grid=(n // tn, nk),
        in_specs=[pl.BlockSpec((None, s, tk), lambda j, kk: ((kk * tk) // ea, 0, ((kk * tk) % ea) // tk)),
                  pl.BlockSpec((None, tn, tk), lambda j, kk: ((kk * tk) // nw, j, ((kk * tk) % nw) // tk))]
        + [ANY] * len(deps),
        out_specs=pl.BlockSpec((s, tn), lambda j, kk: (0, j)),
        out_shape=jax.ShapeDtypeStruct((s, n), out_dtype),
        scratch_shapes=[pltpu.VMEM((s, tn), F32)],
        compiler_params=_params("arbitrary", "arbitrary"),
    )(a3, w3, *deps)


def _mm_tn(a, b3, nchunk, name, deps=()):
    s, ka = a.shape
    g, _, eb = b3.shape
    n = g * eb // nchunk
    tm = _tile(ka, 1024)
    tn = _tile(min(n, eb), 1024)

    def body(a_ref, b_ref, o_ref, at_ref):
        @pl.when(pl.program_id(1) == 0)
        def _():
            at_ref[...] = a_ref[...].astype(F32).T.astype(at_ref.dtype)

        o_ref[...] = jnp.dot(at_ref[...], b_ref[...], preferred_element_type=F32).astype(o_ref.dtype)

    return pl.pallas_call(
        _after(body, 2, deps), name=name, grid=(ka // tm, g * eb // tn),
        in_specs=[pl.BlockSpec((s, tm), lambda i, j: (0, i)),
                  pl.BlockSpec((None, s, tn), lambda i, j: ((j * tn) // eb, 0, ((j * tn) % eb) // tn))]
        + [ANY] * len(deps),
        out_specs=pl.BlockSpec((None, tm, tn), lambda i, j: ((j * tn) // n, i, ((j * tn) % n) // tn)),
        out_shape=jax.ShapeDtypeStruct((nchunk, ka, n), BF16),
        scratch_shapes=[pltpu.VMEM((tm, s), BF16)],
        compiler_params=_params("arbitrary", "arbitrary"),
    )(a, b3, *deps)


def _mm_tn_group(a, b3, idx, pos, nchunk, name, deps=()):
    s, ka = a.shape
    _, _, eb = b3.shape
    n = eb // nchunk
    tm = _tile(ka, 1024)
    tn = _tile(n, 1024)
    nd = len(deps)

    def body(idx_ref, a_ref, b_ref, *rest):
        o_ref, at_ref = rest[nd:]

        @pl.when(pl.program_id(1) == 0)
        def _():
            at_ref[...] = a_ref[...].astype(F32).T.astype(at_ref.dtype)

        o_ref[...] = jnp.dot(at_ref[...], b_ref[...], preferred_element_type=F32).astype(o_ref.dtype)

    grid_spec = pltpu.PrefetchScalarGridSpec(
        num_scalar_prefetch=1, grid=(ka // tm, eb // tn),
        in_specs=[pl.BlockSpec((s, tm), lambda i, j, idx: (0, i)),
                  pl.BlockSpec((None, s, tn), lambda i, j, idx: (idx[pos], 0, j))] + [ANY] * nd,
        out_specs=pl.BlockSpec((None, tm, tn), lambda i, j, idx: ((j * tn) // n, i, ((j * tn) % n) // tn)),
        scratch_shapes=[pltpu.VMEM((tm, s), BF16)])
    return pl.pallas_call(
        body, name=name, grid_spec=grid_spec,
        out_shape=jax.ShapeDtypeStruct((nchunk, ka, n), BF16),
        compiler_params=_params("arbitrary", "arbitrary"),
    )(idx, a, b3, *deps)


def _sigmoid(z):
    return jax.nn.sigmoid(z)


def _shift_down(v, k, fill=0.0, period=None):
    if k == 0:
        return v
    row = lax.broadcasted_iota(jnp.int32, v.shape, 0)
    if period is not None:
        row = row & (period - 1)
    return jnp.where(row >= k, pltpu.roll(v, k, 0), fill)


def _shift_up(v, k, fill=0.0, period=None):
    if k == 0:
        return v
    s = v.shape[0]
    row = lax.broadcasted_iota(jnp.int32, v.shape, 0)
    if period is not None:
        row, s = row & (period - 1), period
    return jnp.where(row < s - k, pltpu.roll(v, v.shape[0] - k, 0), fill)


SCAN_BLOCK = 64


def _scan(a, b, shift):
    s = a.shape[0]
    blk = min(SCAN_BLOCK, s)
    k = 1
    while k < blk:
        b = a * shift(b, k, 0.0, blk) + b
        a = a * shift(a, k, 1.0, blk)
        k *= 2
    nblk = s // blk
    forward = shift is _shift_down
    order = range(nblk) if forward else range(nblk - 1, -1, -1)
    edge = blk - 1 if forward else 0
    out = [None] * nblk
    carry = None
    for i in order:
        h = b[i * blk:(i + 1) * blk]
        if carry is not None:
            h = a[i * blk:(i + 1) * blk] * carry + h
        carry = h[edge:edge + 1]
        out[i] = h
    return jnp.concatenate(out, axis=0) if nblk > 1 else out[0]


def _norm_mod(x, g, scale, shift, name, deps=()):
    s, d = x.shape
    ts = _tile(s, 256)

    def body(x_ref, g_ref, sc_ref, sh_ref, h_ref):
        xv = x_ref[...]
        rstd = lax.rsqrt(jnp.mean(xv * xv, axis=-1, keepdims=True) + EPS)
        nrm = xv * rstd * g_ref[...]
        h_ref[...] = (nrm * (1.0 + sc_ref[...]) + sh_ref[...]).astype(h_ref.dtype)

    vec = pl.BlockSpec((1, d), lambda i: (0, 0))
    return pl.pallas_call(
        _after(body, 4, deps), name=name, grid=(s // ts,),
        in_specs=[pl.BlockSpec((ts, d), lambda i: (i, 0)), vec, vec, vec] + [ANY] * len(deps),
        out_specs=pl.BlockSpec((ts, d), lambda i: (i, 0)),
        out_shape=jax.ShapeDtypeStruct((s, d), BF16),
        compiler_params=_params("arbitrary"),
    )(x, g, scale, shift, *deps)


def _gate_terms(dx, y_ref, gate_ref, dy_ref, dgate_ref):
    dy_ref[...] = (dx * gate_ref[...]).astype(dy_ref.dtype)
    dgate_ref[...] += jnp.sum(dx * y_ref[...].astype(F32), axis=0, keepdims=True)


def _norm_mod_bwd(x, dh, dx_res, g, scale, name, below=None, deps=()):
    s, d = x.shape
    ts = _tile(s, 256)
    nb = 2 if below is not None else 0

    def body(x_ref, dh_ref, dr_ref, g_ref, sc_ref, *rest):
        dx_ref, dsc_ref, dsh_ref, dg_ref = rest[nb:nb + 4]

        @pl.when(pl.program_id(0) == 0)
        def _():
            for ref in rest[nb + 1:nb + 4] + rest[nb + 5:]:
                ref[...] = jnp.zeros_like(ref)

        xv = x_ref[...]
        dh_v = dh_ref[...].astype(F32)
        gv = g_ref[...]
        rstd = lax.rsqrt(jnp.mean(xv * xv, axis=-1, keepdims=True) + EPS)
        xhat = xv * rstd
        dsc_ref[...] += jnp.sum(dh_v * xhat * gv, axis=0, keepdims=True)
        dsh_ref[...] += jnp.sum(dh_v, axis=0, keepdims=True)
        dn = dh_v * (1.0 + sc_ref[...])
        dg_ref[...] += jnp.sum(dn * xhat, axis=0, keepdims=True)
        dxhat = dn * gv
        proj = jnp.mean(dxhat * xhat, axis=-1, keepdims=True)
        dx = dr_ref[...] + rstd * (dxhat - xhat * proj)
        dx_ref[...] = dx
        if nb:
            _gate_terms(dx, rest[0], rest[1], rest[nb + 4], rest[nb + 5])

    row = pl.BlockSpec((ts, d), lambda i: (i, 0))
    vec = pl.BlockSpec((1, d), lambda i: (0, 0))
    extra = list(below) if nb else []
    return pl.pallas_call(
        _after(body, 5 + nb, deps), name=name, grid=(s // ts,),
        in_specs=[row, row, row, vec, vec] + [row, vec][:nb] + [ANY] * len(deps),
        out_specs=[row, vec, vec, vec] + [row, vec][:nb],
        out_shape=[jax.ShapeDtypeStruct((s, d), F32)] + [jax.ShapeDtypeStruct((1, d), F32)] * 3
        + [jax.ShapeDtypeStruct((s, d), BF16), jax.ShapeDtypeStruct((1, d), F32)][:nb],
        compiler_params=_params("arbitrary"),
    )(x, dh, dx_res, g, scale, *extra, *deps)


def _final_loss(x, g, target, y, gate, name):
    s, d = x.shape
    ts = _tile(s, 256)

    def body(x_ref, g_ref, t_ref, y_ref, gate_ref, dx_ref, loss_ref, dg_ref, dy_ref, dgate_ref):
        @pl.when(pl.program_id(0) == 0)
        def _():
            loss_ref[...] = jnp.zeros_like(loss_ref)
            dg_ref[...] = jnp.zeros_like(dg_ref)
            dgate_ref[...] = jnp.zeros_like(dgate_ref)

        xv = x_ref[...]
        gv = g_ref[...]
        rstd = lax.rsqrt(jnp.mean(xv * xv, axis=-1, keepdims=True) + EPS)
        xhat = xv * rstd
        err = xhat * gv - t_ref[...]
        loss_ref[...] += 0.5 * jnp.sum(jnp.mean(err * err, axis=-1, keepdims=True))
        dy = err * (1.0 / d)
        dg_ref[...] += jnp.sum(dy * xhat, axis=0, keepdims=True)
        dxhat = dy * gv
        proj = jnp.mean(dxhat * xhat, axis=-1, keepdims=True)
        dx = rstd * (dxhat - xhat * proj)
        dx_ref[...] = dx
        _gate_terms(dx, y_ref, gate_ref, dy_ref, dgate_ref)

    row = pl.BlockSpec((ts, d), lambda i: (i, 0))
    vec = pl.BlockSpec((1, d), lambda i: (0, 0))
    return pl.pallas_call(
        body, name=name, grid=(s // ts,),
        in_specs=[row, vec, row, row, vec],
        out_specs=[row, pl.BlockSpec((1, LANES), lambda i: (0, 0)), vec, row, vec],
        out_shape=[jax.ShapeDtypeStruct((s, d), F32), jax.ShapeDtypeStruct((1, LANES), F32),
                   jax.ShapeDtypeStruct((1, d), F32), jax.ShapeDtypeStruct((s, d), BF16),
                   jax.ShapeDtypeStruct((1, d), F32)],
        compiler_params=_params("arbitrary"),
    )(x, g, target, y, gate)


def _conv(v, w_ref, width):
    out = w_ref[width - 1:width, :] * v
    for k in range(width - 1):
        out = out + w_ref[k:k + 1, :] * _shift_down(v, width - 1 - k)
    return out


def _sc_fwd(proj, conv_w, name, deps=()):
    _, s, e = proj.shape
    te = _tile(e, 256)
    width = conv_w.shape[0]

    def body(b_ref, c_ref, v_ref, g_ref, w_ref, o_ref):
        cv = c_ref[...].astype(F32) * v_ref[...].astype(F32)
        u = _conv(cv, w_ref, width)
        gv = g_ref[...].astype(F32)
        o_ref[...] = (b_ref[...].astype(F32) * u * (gv * _sigmoid(gv))).astype(o_ref.dtype)

    def part(q):
        return pl.BlockSpec((None, s, te), lambda j, q=q: (q, 0, j))

    return pl.pallas_call(
        _after(body, 5, deps), name=name, grid=(e // te,),
        in_specs=[part(0), part(1), part(2), part(3), pl.BlockSpec((width, te), lambda j: (0, j))]
        + [ANY] * len(deps),
        out_specs=pl.BlockSpec((s, te), lambda j: (0, j)),
        out_shape=jax.ShapeDtypeStruct((s, e), BF16),
        compiler_params=_params("arbitrary"),
    )(proj, proj, proj, proj, conv_w, *deps)


def _sc_bwd(proj, dyb, conv_w, name, deps=()):
    _, s, e = proj.shape
    te = _tile(e, 256)
    width = conv_w.shape[0]

    def body(b_ref, c_ref, v_ref, g_ref, dy_ref, w_ref, dp_ref, vec_ref):
        bv = b_ref[...].astype(F32)
        cvl = c_ref[...].astype(F32)
        vv = v_ref[...].astype(F32)
        gv = g_ref[...].astype(F32)
        dyv = dy_ref[...].astype(F32)
        cv = cvl * vv
        u = _conv(cv, w_ref, width)
        sg = _sigmoid(gv)
        silu = gv * sg
        dp_ref[0] = (dyv * u * silu).astype(dp_ref.dtype)
        du = dyv * bv * silu
        dp_ref[3] = (dyv * bv * u * (sg * (1.0 + gv * (1.0 - sg)))).astype(dp_ref.dtype)
        dcv = w_ref[width - 1:width, :] * du
        vec_ref[...] = jnp.zeros_like(vec_ref)
        vec_ref[width - 1:width, :] = jnp.sum(du * cv, axis=0, keepdims=True)
        for k in range(width - 1):
            sh = width - 1 - k
            dcv = dcv + w_ref[k:k + 1, :] * _shift_up(du, sh)
            vec_ref[k:k + 1, :] = jnp.sum(du * _shift_down(cv, sh), axis=0, keepdims=True)
        dp_ref[1] = (dcv * vv).astype(dp_ref.dtype)
        dp_ref[2] = (dcv * cvl).astype(dp_ref.dtype)

    def part(q):
        return pl.BlockSpec((None, s, te), lambda j, q=q: (q, 0, j))

    return pl.pallas_call(
        _after(body, 6, deps), name=name, grid=(e // te,),
        in_specs=[part(0), part(1), part(2), part(3), pl.BlockSpec((s, te), lambda j: (0, j)),
                  pl.BlockSpec((width, te), lambda j: (0, j))] + [ANY] * len(deps),
        out_specs=[pl.BlockSpec((4, s, te), lambda j: (0, 0, j)),
                   pl.BlockSpec((8, te), lambda j: (0, j))],
        out_shape=[jax.ShapeDtypeStruct((4, s, e), BF16), jax.ShapeDtypeStruct((8, e), F32)],
        compiler_params=_params("arbitrary"),
    )(proj, proj, proj, proj, dyb, conv_w, *deps)


def _lru_gates(v_pre, w_ref, cb_ref, wa_ref, ba_ref, wx_ref, bx_ref, lam_ref, width):
    v = _conv(v_pre, w_ref, width) + cb_ref[...]
    vb = v.astype(BF16)
    r = _sigmoid(jnp.dot(vb, wa_ref[...], preferred_element_type=F32) + ba_ref[...])
    i = _sigmoid(jnp.dot(vb, wx_ref[...], preferred_element_type=F32) + bx_ref[...])
    nl = -lam_ref[...]
    sp = jnp.maximum(nl, 0.0) + jnp.log1p(jnp.exp(-jnp.abs(nl)))
    log_a = (-RGLRU_C) * r * sp
    a = jnp.exp(log_a)
    one_minus_a2 = jnp.tanh(-log_a) * (1.0 + a * a)
    mult = jnp.sqrt(one_minus_a2)
    return v, vb, r, i, sp, a, mult


def _lru_specs(s, dh, heads, width):
    head_col = lambda q: pl.BlockSpec((None, s, dh), lambda h, q=q: (q, 0, h))
    vec = pl.BlockSpec((1, dh), lambda h: (0, h))
    mat = pl.BlockSpec((None, dh, dh), lambda h: (h, 0, 0))
    weights = [pl.BlockSpec((width, dh), lambda h: (0, h)), vec, mat, vec, mat, vec, vec]
    return head_col, weights


def _lru_fwd(proj, conv_w, conv_b, w_a, b_a, w_x, b_x, lam, name, deps=()):
    _, s, e = proj.shape
    heads, dh, _ = w_a.shape
    width = conv_w.shape[0]

    def body(v_ref, g_ref, w_ref, cb_ref, wa_ref, ba_ref, wx_ref, bx_ref, lam_ref, yb_ref, hs_ref):
        v, _, _, i, _, a, mult = _lru_gates(v_ref[...].astype(F32), w_ref, cb_ref, wa_ref, ba_ref,
                                           wx_ref, bx_ref, lam_ref, width)
        hs = _scan(a, mult * i * v, _shift_down)
        hs_ref[...] = hs
        gv = g_ref[...].astype(F32)
        yb_ref[...] = (hs * (gv * _sigmoid(gv))).astype(yb_ref.dtype)

    head_col, weights = _lru_specs(s, dh, heads, width)
    out = pl.BlockSpec((s, dh), lambda h: (0, h))
    return pl.pallas_call(
        _after(body, 9, deps), name=name, grid=(heads,),
        in_specs=[head_col(0), head_col(1)] + weights + [ANY] * len(deps),
        out_specs=[out, out],
        out_shape=[jax.ShapeDtypeStruct((s, e), BF16), jax.ShapeDtypeStruct((s, e), F32)],
        compiler_params=_params("arbitrary"),
    )(proj, proj, conv_w, conv_b, w_a, b_a, w_x, b_x, lam, *deps)


def _lru_bwd(proj, hs, dyb, conv_w, conv_b, w_a, b_a, w_x, b_x, lam, name, deps=()):
    _, s, e = proj.shape
    heads, dh, _ = w_a.shape
    width = conv_w.shape[0]

    def body(v_ref, g_ref, hs_ref, dy_ref, w_ref, cb_ref, wa_ref, ba_ref, wx_ref, bx_ref, lam_ref,
             dp_ref, dwa_ref, dwx_ref, vec_ref):
        v_pre = v_ref[...].astype(F32)
        v, vb, r, i, sp, a, mult = _lru_gates(v_pre, w_ref, cb_ref, wa_ref, ba_ref, wx_ref, bx_ref,
                                              lam_ref, width)
        hs = hs_ref[...]
        gv = g_ref[...].astype(F32)
        dyv = dy_ref[...].astype(F32)
        sg = _sigmoid(gv)
        dp_ref[1] = (dyv * hs * (sg * (1.0 + gv * (1.0 - sg)))).astype(dp_ref.dtype)
        dhs = dyv * (gv * sg)
        d_h = _scan(_shift_up(a, 1), dhs, _shift_up)
        da = d_h * _shift_down(hs, 1)
        iv = i * v
        dlog_a = da * a - (d_h * iv) * (a * a) / mult
        di = d_h * mult * v
        dv = d_h * mult * i
        dzr = dlog_a * (-RGLRU_C) * sp * r * (1.0 - r)
        dzi = di * i * (1.0 - i)
        dsp = jnp.sum(dlog_a * r, axis=0, keepdims=True) * (-RGLRU_C)
        vec_ref[...] = jnp.zeros_like(vec_ref)
        vec_ref[0:1, :] = jnp.sum(dzr, axis=0, keepdims=True)
        vec_ref[1:2, :] = jnp.sum(dzi, axis=0, keepdims=True)
        vec_ref[2:3, :] = -dsp * _sigmoid(-lam_ref[...])
        dzr_b = dzr.astype(BF16)
        dzi_b = dzi.astype(BF16)
        vt = vb.astype(F32).T.astype(BF16)
        dwa_ref[...] = jnp.dot(vt, dzr_b, preferred_element_type=F32).astype(dwa_ref.dtype)
        dwx_ref[...] = jnp.dot(vt, dzi_b, preferred_element_type=F32).astype(dwx_ref.dtype)
        nt = (((1,), (1,)), ((), ()))
        dv = dv + lax.dot_general(dzr_b, wa_ref[...], nt, preferred_element_type=F32)
        dv = dv + lax.dot_general(dzi_b, wx_ref[...], nt, preferred_element_type=F32)
        vec_ref[3:4, :] = jnp.sum(dv, axis=0, keepdims=True)
        dvp = w_ref[width - 1:width, :] * dv
        vec_ref[4 + width - 1:4 + width, :] = jnp.sum(dv * v_pre, axis=0, keepdims=True)
        for k in range(width - 1):
            sh = width - 1 - k
            dvp = dvp + w_ref[k:k + 1, :] * _shift_up(dv, sh)
            vec_ref[4 + k:5 + k, :] = jnp.sum(dv * _shift_down(v_pre, sh), axis=0, keepdims=True)
        dp_ref[0] = dvp.astype(dp_ref.dtype)

    head_col, weights = _lru_specs(s, dh, heads, width)
    col = pl.BlockSpec((s, dh), lambda h: (0, h))
    mat = pl.BlockSpec((None, dh, dh), lambda h: (h, 0, 0))
    return pl.pallas_call(
        _after(body, 11, deps), name=name, grid=(heads,),
        in_specs=[head_col(0), head_col(1), col, col] + weights + [ANY] * len(deps),
        out_specs=[pl.BlockSpec((2, s, dh), lambda h: (0, 0, h)), mat, mat,
                   pl.BlockSpec((16, dh), lambda h: (0, h))],
        out_shape=[jax.ShapeDtypeStruct((2, s, e), BF16),
                   jax.ShapeDtypeStruct((heads, dh, dh), BF16),
                   jax.ShapeDtypeStruct((heads, dh, dh), BF16),
                   jax.ShapeDtypeStruct((16, e), F32)],
        compiler_params=_params("arbitrary"),
    )(proj, proj, hs, dyb, conv_w, conv_b, w_a, b_a, w_x, b_x, lam, *deps)


def _ada_mod(c_all, w, b, name):
    layers, d, f = w.shape
    nb = c_all.shape[0]

    def body(c_ref, w_ref, b_ref, o_ref):
        cv = c_ref[...]
        sc = cv * _sigmoid(cv)
        o_ref[...] = jnp.dot(sc, w_ref[...], preferred_element_type=F32,
                             precision=lax.Precision.HIGHEST) + b_ref[...]

    return pl.pallas_call(
        body, name=name, grid=(layers,),
        in_specs=[pl.BlockSpec((nb, d), lambda l: (0, 0)),
                  pl.BlockSpec((None, d, f), lambda l: (l, 0, 0)),
                  pl.BlockSpec((None, 1, f), lambda l: (l, 0, 0))],
        out_specs=pl.BlockSpec((None, nb, f), lambda l: (l, 0, 0)),
        out_shape=jax.ShapeDtypeStruct((layers, nb, f), F32),
        compiler_params=_params("arbitrary"),
    )(c_all, w, b)


def _ada_grad(c_all_t, dmod, name):
    d, nb = c_all_t.shape
    layers, _, f = dmod.shape

    def body(c_ref, dm_ref, o_ref):
        cv = c_ref[...]
        sc = cv * _sigmoid(cv)
        acc = sc[:, 0:1] * dm_ref[0:1, :]
        for k in range(1, nb):
            acc = acc + sc[:, k:k + 1] * dm_ref[k:k + 1, :]
        o_ref[...] = acc

    return pl.pallas_call(
        body, name=name, grid=(layers,),
        in_specs=[pl.BlockSpec((d, nb), lambda l: (0, 0)),
                  pl.BlockSpec((None, nb, f), lambda l: (l, 0, 0))],
        out_specs=pl.BlockSpec((None, d, f), lambda l: (l, 0, 0)),
        out_shape=jax.ShapeDtypeStruct((layers, d, f), F32),
        compiler_params=_params("arbitrary"),
    )(c_all_t, dmod)


def _device_sum(g, name):
    _, rows, _ = g.shape

    def body(g_ref, o_ref):
        acc = g_ref[0]
        for k in range(1, N_DEV):
            acc = acc + g_ref[k]
        o_ref[...] = acc

    return pl.pallas_call(
        body, name=name,
        in_specs=[VMEM_SPEC], out_specs=VMEM_SPEC,
        out_shape=jax.ShapeDtypeStruct((rows, LANES), F32),
        compiler_params=pltpu.CompilerParams(vmem_limit_bytes=VMEM_LIMIT),
    )(g)


def _adamw_math(w, g, m, v):
    m = ADAM_B1 * m + (1.0 - ADAM_B1) * g
    v = ADAM_B2 * v + (1.0 - ADAM_B2) * (g * g)
    m_hat = m / (1.0 - ADAM_B1 ** ADAM_STEP)
    v_hat = v / (1.0 - ADAM_B2 ** ADAM_STEP)
    delta = -ADAM_LR * (m_hat / (jnp.sqrt(v_hat) + ADAM_EPS) + ADAM_WD * w)
    return delta, m, v


def _adamw(w, g, m, v, name):
    rows, cols = w.shape
    tr = _tile(rows, 256)

    def body(w_ref, g_ref, m_ref, v_ref, d_ref, mo_ref, vo_ref):
        d_ref[...], mo_ref[...], vo_ref[...] = _adamw_math(w_ref[...], g_ref[...], m_ref[...], v_ref[...])

    blk = pl.BlockSpec((tr, cols), lambda i: (i, 0))
    return pl.pallas_call(
        body, name=name, grid=(rows // tr,),
        in_specs=[blk] * 4, out_specs=[blk] * 3,
        out_shape=[jax.ShapeDtypeStruct((rows, cols), F32)] * 3,
        compiler_params=_params("arbitrary"),
    )(w, g, m, v)


def _adamw_reduced(idx, w, m, v, part, got, recvs, name):
    rows, cols = w.shape
    tr = _tile(rows, 256)
    nr = len(recvs)

    def body(idx_ref, w_ref, m_ref, v_ref, p_ref, q_ref, *rest):
        g_ref, d_ref, mo_ref, vo_ref = rest[nr:]
        g = p_ref[...].astype(F32) + q_ref[...].astype(F32)
        for u_ref in rest[:nr]:
            for j in range(u_ref.shape[0]):
                g = g + u_ref[j].astype(F32)
        g_ref[...] = g
        d_ref[...], mo_ref[...], vo_ref[...] = _adamw_math(w_ref[...], g, m_ref[...], v_ref[...])

    blk = pl.BlockSpec((tr, cols), lambda i, idx: (i, 0))
    grid_spec = pltpu.PrefetchScalarGridSpec(
        num_scalar_prefetch=1, grid=(rows // tr,),
        in_specs=[blk, blk, blk,
                  pl.BlockSpec((None, None, tr, cols), lambda i, idx: (idx[3], idx[4], i, 0)),
                  pl.BlockSpec((None, None, tr, cols), lambda i, idx: (idx[3], 0, i, 0))]
        + [pl.BlockSpec((u.shape[0], tr, cols), lambda i, idx: (0, i, 0)) for u in recvs],
        out_specs=[blk] * 4)
    return pl.pallas_call(
        body, name=name, grid_spec=grid_spec,
        out_shape=[jax.ShapeDtypeStruct((rows, cols), F32)] * 4,
        compiler_params=_params("arbitrary"),
    )(idx, w, m, v, part, got, *recvs)


def _pack(vectors):
    flat = jnp.concatenate([v.reshape(-1).astype(F32) for v in vectors])
    pad = (-flat.shape[0]) % (8 * LANES)
    return jnp.pad(flat, (0, pad)).reshape(-1, LANES)


def _unpack(flat, shapes):
    out, off = [], 0
    for shp in shapes:
        size = math.prod(shp)
        out.append(flat[..., off:off + size].reshape(flat.shape[:-1] + tuple(shp)))
        off += size
    return out


def _my_slice(full, me, axis):
    size = full.shape[axis] // N_DEV
    return lax.dynamic_slice_in_dim(full, me * size, size, axis)


def kernel(x, c, norm_g, ada_w, ada_b, sc_w_in, sc_conv_w, sc_w_out, lru_w_in, lru_conv_w, lru_conv_b, lru_w_a, lru_b_a, lru_w_x, lru_b_x, lru_lambda, lru_w_out, final_g, loss_target, m_norm_g, m_ada_w, m_ada_b, m_sc_w_in, m_sc_conv_w, m_sc_w_out, m_lru_w_in, m_lru_conv_w, m_lru_conv_b, m_lru_w_a, m_lru_b_a, m_lru_w_x, m_lru_b_x, m_lru_lambda, m_lru_w_out, m_final_g, v_norm_g, v_ada_w, v_ada_b, v_sc_w_in, v_sc_conv_w, v_sc_w_out, v_lru_w_in, v_lru_conv_w, v_lru_conv_b, v_lru_w_a, v_lru_b_a, v_lru_w_x, v_lru_b_x, v_lru_lambda, v_lru_w_out, v_final_g):
    _, s, d = x.shape
    e = sc_w_out.shape[1] * N_DEV
    heads, dh_s, dh = lru_w_a.shape[1:]
    es = e // N_DEV
    f = ada_w.shape[2]
    mx, my, mc = _position()
    me = 4 * mx + 2 * my + mc
    chip = 2 * mx + my
    idx = jnp.stack([chip ^ 1, chip ^ 2, chip ^ 3, chip, mc]).astype(jnp.int32)

    x0 = x[0]
    target = loss_target[0]

    small_shapes = [(d,), (3, es), (4, es), (es,), (heads, dh_s), (heads, dh_s), (es,)]
    small = _small_gather(_pack([c, sc_conv_w, lru_conv_w, lru_conv_b, lru_b_a, lru_b_x, lru_lambda]),
                          "gather_small_weights").reshape(N_DEV, -1)
    c_all, cw3, cw4, cb, ba, bx, lam = _unpack(small, small_shapes)
    cw3 = cw3.transpose(1, 0, 2).reshape(3, e)
    cw4 = cw4.transpose(1, 0, 2).reshape(4, e)
    cb = cb.reshape(1, e)
    lam = lam.reshape(1, e)
    ba = ba.transpose(1, 0, 2).reshape(1, e)
    bx = bx.transpose(1, 0, 2).reshape(1, e)

    ada_b_mine = _my_slice(ada_b, me, 1).reshape(2, 1, f)
    mod_mine = _ada_mod(c_all, ada_w, ada_b_mine, "ada_mod")
    mod_all = _small_gather(_pack([mod_mine]), "gather_mod")

    shards = [sc_w_in[0].astype(BF16), sc_w_out[0].astype(BF16), lru_w_in[0].astype(BF16),
              lru_w_a[0].reshape(heads * dh_s, dh).astype(BF16),
              lru_w_x[0].reshape(heads * dh_s, dh).astype(BF16), lru_w_out[0].astype(BF16)]
    lands = [lax.dynamic_update_slice(lax.empty((N_DEV,) + sh.shape, BF16), sh[None], (me, 0, 0))
             for sh in shards]
    every = [1, 2, 3, 0]
    units = [([0], [0]), ([0], [1, 2]), ([0], [3]), ([1], every), ([2, 3, 4], every), ([5], every)]
    sems, shards, lands, started = _gather_start(shards, lands, units, [mod_all], "gather_start")

    def gathered(u, after_forward, name):
        members, ks = units[u]
        fwd, shs, lnd, token = _gather_forward(
            [shards[i] for i in members], [lands[i] for i in members], ks, sems[u][0], sems[u][1],
            after_forward, "gather_forward_" + name)
        for i, sh, ld in zip(members, shs, lnd):
            shards[i], lands[i] = sh, ld

        def finish(after):
            out = _gather_finish([lands[i] for i in members], ks, fwd, after, "gather_finish_" + name)
            for i, ld in zip(members, out):
                lands[i] = ld
            return out

        return token, finish

    mod_all = mod_all.reshape(N_DEV, -1)
    mod_all = mod_all[:, :2 * N_DEV * f].reshape(N_DEV, 2, N_DEV, f)
    mod_all = mod_all.transpose(1, 2, 0, 3).reshape(2, N_DEV, 3 * d)
    mod = lax.dynamic_index_in_dim(mod_all, me, 1, keepdims=False)
    shift = [mod[l:l + 1, 0:d] for l in range(2)]
    scale = [mod[l:l + 1, d:2 * d] for l in range(2)]
    gate = [mod[l:l + 1, 2 * d:3 * d] for l in range(2)]
    ng = [norm_g[l:l + 1] for l in range(2)]
    fg = final_g.reshape(1, d)

    h0 = _norm_mod(x0, ng[0], scale[0], shift[0], "norm_mod_0", deps=[started])
    proj0 = lax.empty((4, s, e), BF16)
    tok, _ = gathered(0, [h0], "sc_w_in_own")
    proj0 = _mm_proj_group(h0, lands[0], idx, 3, proj0, "mm_proj_0_own", deps=[tok])
    tok, finish = gathered(1, [proj0], "sc_w_in_near")
    finish([tok])
    proj0 = _mm_proj_group(h0, lands[0], idx, 0, proj0, "mm_proj_0_near_y")
    proj0 = _mm_proj_group(h0, lands[0], idx, 1, proj0, "mm_proj_0_near_x")
    tok, finish = gathered(2, [proj0], "sc_w_in_far")
    wg_in0, = finish([tok])
    proj0 = _mm_proj_group(h0, wg_in0, idx, 2, proj0, "mm_proj_0_far")
    tok, finish = gathered(3, [proj0], "sc_w_out")
    yb0 = _sc_fwd(proj0, cw3, "sc_fwd", deps=[tok])
    w_out0 = finish([yb0])[0].reshape(e, d)
    x1, y0 = _mm_out(yb0, w_out0, x0, gate[0], "mm_out_0")
    tok, finish = gathered(4, [x1], "lru_in")
    h1 = _norm_mod(x1, ng[1], scale[1], shift[1], "norm_mod_1", deps=[tok])
    wg_in1, wg_a, wg_x = finish([h1])
    w_a = wg_a.reshape(N_DEV, heads, dh_s, dh).transpose(1, 0, 2, 3).reshape(heads, dh, dh)
    w_x = wg_x.reshape(N_DEV, heads, dh_s, dh).transpose(1, 0, 2, 3).reshape(heads, dh, dh)
    proj1 = _mm_proj(h1, wg_in1, 2, "mm_proj_1")
    tok, finish = gathered(5, [proj1], "lru_w_out")
    yb1, hs = _lru_fwd(proj1, cw4, cb, w_a, ba, w_x, bx, lam, "lru_fwd", deps=[tok])
    w_out1 = finish([yb1])[0].reshape(e, d)
    x2, y1 = _mm_out(yb1, w_out1, x1, gate[1], "mm_out_1")
    dx2, loss_part, d_fg, dy1, dgate1 = _final_loss(x2, fg, target, y1, gate[1], "final_loss")

    def pieces(g, rows, cols):
        return g.reshape(4, 2, rows, cols)

    def by_rows(g):
        return g.reshape(heads, N_DEV, dh_s, dh).transpose(1, 0, 2, 3).reshape(N_DEV, heads * dh_s, dh)

    def pair_begin(parts, group):
        send, recv, parts, lnd, token = _pair_start(parts, "pair_start_" + group)
        return dict(parts=parts, lands=lnd, send=send, recv=recv, group=group), token

    def scatter_start(pair, names, after):
        group = pair["group"]
        parts, gots = _pair_wait(pair["parts"], pair["lands"], pair["send"], pair["recv"], after,
                                 "pair_wait_" + group)
        sums = [_pair_sum(idx, p, q, "pair_sum_" + nm) for p, q, nm in zip(parts, gots, names)]
        empties = [lax.empty(sm.shape, sm.dtype) for sm in sums]
        send, recv, sums, lnd, token = _chip_start(sums, empties, "chip_start_" + group)
        return dict(parts=parts, gots=gots, names=names, group=group, sums=sums, lands=lnd,
                    send=send, recv=recv), token

    big = {"sc_w_in": (sc_w_in, m_sc_w_in, v_sc_w_in), "sc_w_out": (sc_w_out, m_sc_w_out, v_sc_w_out),
           "lru_w_in": (lru_w_in, m_lru_w_in, v_lru_w_in), "lru_w_a": (lru_w_a, m_lru_w_a, v_lru_w_a),
           "lru_w_x": (lru_w_x, m_lru_w_x, v_lru_w_x), "lru_w_out": (lru_w_out, m_lru_w_out, v_lru_w_out)}
    big_res = {}

    def scatter_finish(rs, after):
        recvs = _chip_wait(rs["sums"], rs["lands"], rs["send"], rs["recv"], after, "chip_wait_" + rs["group"])
        done = []
        for p, q, u, nm in zip(rs["parts"], rs["gots"], recvs, rs["names"]):
            w, m, v = big[nm]
            shp2 = p.shape[2:]
            res = _adamw_reduced(idx, w.reshape(shp2), m.reshape(shp2), v.reshape(shp2), p, q, [u], "adamw_" + nm)
            big_res[nm] = [r.reshape(w.shape) for r in res]
            done.append(res[1])
        return done

    dw_out1 = _mm_tn(yb1, dy1[None], 1, "mm_dw_out_1")
    pair, tok = pair_begin([pieces(dw_out1, es, d)], "lru_w_out")
    dyb1 = _mm_nt(dy1[None], w_out1[None], BF16, "mm_dyb_1", deps=[tok])
    rs1, tok = scatter_start(pair, ["lru_w_out"], [dyb1])
    dproj1, dw_a, dw_x, vecs1 = _lru_bwd(proj1, hs, dyb1, cw4, cb, w_a, ba, w_x, bx, lam, "lru_bwd", deps=[tok])
    done = scatter_finish(rs1, [dproj1])
    dw_in1 = _mm_tn(h1, dproj1, N_DEV, "mm_dw_in_1", deps=done)
    pair, tok = pair_begin([pieces(dw_in1, d, 2 * es), pieces(by_rows(dw_a), heads * dh_s, dh),
                            pieces(by_rows(dw_x), heads * dh_s, dh)], "lru_in")
    dh1 = _mm_nt(dproj1, wg_in1, F32, "mm_dh_1", deps=[tok])
    rs2, tok = scatter_start(pair, ["lru_w_in", "lru_w_a", "lru_w_x"], [dh1])
    dx1, dscale1, dshift1, dng1, dy0, dgate0 = _norm_mod_bwd(x1, dh1, dx2, ng[1], scale[1], "norm_mod_bwd_1",
                                                             below=(y0, gate[0]), deps=[tok])
    dw_out0 = _mm_tn(yb0, dy0[None], 1, "mm_dw_out_0")
    pair, tok = pair_begin([pieces(dw_out0, es, d)], "sc_w_out")
    dyb0 = _mm_nt(dy0[None], w_out0[None], BF16, "mm_dyb_0", deps=[tok])
    rs3, tok = scatter_start(pair, ["sc_w_out"], [dyb0])
    dproj0, vecs0 = _sc_bwd(proj0, dyb0, cw3, "sc_bwd", deps=[tok])
    done = scatter_finish(rs2, [dproj0])
    idx_one = jnp.stack([jnp.zeros_like(mc)] * 4 + [mc]).astype(jnp.int32)
    sc_w_in_steps = []

    def chip_step(j, pair, after):
        (part,), (got,) = _pair_wait(pair["parts"], pair["lands"], pair["send"], pair["recv"], after,
                                     "pair_wait_sc_w_in_%d" % j)
        sm = _pair_sum(idx_one, part, got, "pair_sum_sc_w_in_%d" % j, nslots=1)
        send, recv, sums, lnd, token = _chip_start([sm], [lax.empty(sm.shape, sm.dtype)],
                                                   "chip_start_sc_w_in_%d" % j, flips=(j,))
        sc_w_in_steps.append((sums, lnd, send, recv, j))
        return token

    pending = None
    for j in (1, 2, 3, 0):
        part = _mm_tn_group(h0, dproj0, idx, (j - 1) % 4, 2, "mm_dw_in_0_%d" % j, deps=done)[None]
        pair, tok = pair_begin([part], "sc_w_in_%d" % j)
        done = [tok]
        if pending is not None:
            done.append(chip_step(pending[0], pending[1], [tok]))
        pending = (j, pair)
    dh0 = _mm_nt(dproj0, wg_in0, F32, "mm_dh_0", deps=done)
    pair = pending[1]
    (part,), (got,) = _pair_wait(pair["parts"], pair["lands"], pair["send"], pair["recv"], [dh0],
                                 "pair_wait_sc_w_in_0")
    dx0, dscale0, dshift0, dng0 = _norm_mod_bwd(x0, dh0, dx1, ng[0], scale[0], "norm_mod_bwd_0")
    done = scatter_finish(rs3, [dx0])
    dmod_mine = jnp.concatenate([dshift0, dscale0, dgate0, dshift1, dscale1, dgate1], axis=1)
    end_shapes = [(LANES,), (2, 3 * d), (2, d), (d,), (8, e), (16, e)]
    end_all = _small_gather(
        _pack([loss_part, dmod_mine, jnp.concatenate([dng0, dng1], axis=0), d_fg, vecs0, vecs1]),
        "gather_small_grads", deps=done)
    end_sum = _device_sum(end_all, "sum_small_grads").reshape(-1)
    loss_v, g_ada_b, g_norm_g, g_final_g, sum0, sum1 = _unpack(end_sum, end_shapes)
    loss = loss_v[0]
    dmod_all = _unpack(end_all.reshape(N_DEV, -1), end_shapes)[1].transpose(1, 0, 2)
    dmod_cols = _my_slice(dmod_all, me, 2)
    g_ada_w = _ada_grad(c_all.T, dmod_cols, "ada_grad")

    g_sc_conv_w = _my_slice(sum0[0:3], me, 1)
    g_lru_b_a = _my_slice(sum1[0].reshape(heads, dh), me, 1)
    g_lru_b_x = _my_slice(sum1[1].reshape(heads, dh), me, 1)
    g_lru_lambda = _my_slice(sum1[2:3], me, 1)
    g_lru_conv_b = _my_slice(sum1[3:4], me, 1)
    g_lru_conv_w = _my_slice(sum1[4:8], me, 1)

    ada_res = _adamw(ada_w.reshape(2 * d, f), g_ada_w.reshape(2 * d, f), m_ada_w.reshape(2 * d, f),
                     v_ada_w.reshape(2 * d, f), "adamw_ada_w")
    ada_out = [g_ada_w] + [r.reshape(ada_w.shape) for r in ada_res]

    small_w = [norm_g, ada_b, final_g, sc_conv_w, lru_conv_w, lru_conv_b, lru_b_a, lru_b_x, lru_lambda]
    small_m = [m_norm_g, m_ada_b, m_final_g, m_sc_conv_w, m_lru_conv_w, m_lru_conv_b, m_lru_b_a, m_lru_b_x,
               m_lru_lambda]
    small_v = [v_norm_g, v_ada_b, v_final_g, v_sc_conv_w, v_lru_conv_w, v_lru_conv_b, v_lru_b_a, v_lru_b_x,
               v_lru_lambda]
    small_g = [g_norm_g, g_ada_b, g_final_g, g_sc_conv_w, g_lru_conv_w, g_lru_conv_b, g_lru_b_a, g_lru_b_x,
               g_lru_lambda]
    small_g = [g.reshape(w.shape) for g, w in zip(small_g, small_w)]
    shapes = [w.shape for w in small_w]
    packed = _adamw(_pack(small_w), _pack(small_g), _pack(small_m), _pack(small_v), "adamw_small")
    small_out = [small_g] + [_unpack(p.reshape(-1), shapes) for p in packed]

    after = [packed[0], ada_res[0]]
    recvs = []
    for sums, lnd, send, recv, j in sc_w_in_steps:
        recvs += _chip_wait(sums, lnd, send, recv, after, "chip_wait_sc_w_in_%d" % j)
    shp2 = part.shape[2:]
    res = _adamw_reduced(idx_one, sc_w_in.reshape(shp2), m_sc_w_in.reshape(shp2), v_sc_w_in.reshape(shp2),
                         part, got, recvs, "adamw_sc_w_in")
    big_res["sc_w_in"] = [r.reshape(sc_w_in.shape) for r in res]
    big_out = [big_res[nm] for nm in ("sc_w_in", "sc_w_out", "lru_w_in", "lru_w_a", "lru_w_x", "lru_w_out")]

    def small(kind, i):
        return small_out[kind][i]

    def bigw(kind, i):
        return big_out[i][kind]

    outs = [loss, dx0[None]]
    for kind in range(4):
        outs += [small(kind, 0), ada_out[kind], small(kind, 1), bigw(kind, 0), small(kind, 3), bigw(kind, 1),
                 bigw(kind, 2), small(kind, 4), small(kind, 5), bigw(kind, 3), small(kind, 6), bigw(kind, 4),
                 small(kind, 7), small(kind, 8), bigw(kind, 5), small(kind, 2)]
    return tuple(outs)
```

```python
import math

import jax
import jax.numpy as jnp
from jax import lax
from jax.experimental import pallas as pl
from jax.experimental.pallas import tpu as pltpu

N_DEV = 8
LANES = 128
EPS = 1e-6
RGLRU_C = 8.0
ADAM_LR = 0.001
ADAM_B1 = 0.9
ADAM_B2 = 0.999
ADAM_EPS = 1e-08
ADAM_WD = 0.01
ADAM_STEP = 10
VMEM_LIMIT = 56 * 1024 * 1024
MESH = pl.DeviceIdType.MESH
F32 = jnp.float32
BF16 = jnp.bfloat16
ANY = pl.BlockSpec(memory_space=pl.ANY)
HBM = pl.BlockSpec(memory_space=pltpu.HBM)
SEM = pl.BlockSpec(memory_space=pltpu.SEMAPHORE)
VMEM_SPEC = pl.BlockSpec(memory_space=pltpu.VMEM)
EFFECT = pltpu.SideEffectType.DATAFLOW_SIDE_EFFECTING
TOKEN = jax.ShapeDtypeStruct((8, LANES), jnp.float32)


def _tile(n, pref):
    t = min(n, pref)
    assert n % t == 0, (n, pref)
    return t


def _params(*sem):
    return pltpu.CompilerParams(dimension_semantics=sem, vmem_limit_bytes=VMEM_LIMIT)


def _position():
    return lax.axis_index("x"), lax.axis_index("y"), lax.axis_index("c")


def _flip(x, y, k):
    return (1 - x if k & 2 else x), (1 - y if k & 1 else y)


def _after(body, n_in, deps):
    if not deps:
        return body

    def wrapped(*refs):
        return body(*refs[:n_in], *refs[n_in + len(deps):])

    return wrapped


def _small_gather(v, name, deps=()):
    rows = v.shape[0]

    def body(v_ref, out_ref, send_sems, recv_sems):
        x, y, c = _position()
        me = 4 * x + 2 * y + c
        out_ref[me] = v_ref[...]
        copies = []
        for k in range(1, N_DEV):
            px, py = _flip(x, y, k >> 1)
            pc = 1 - c if k & 1 else c
            cp = pltpu.make_async_remote_copy(
                src_ref=v_ref, dst_ref=out_ref.at[me],
                send_sem=send_sems.at[k - 1], recv_sem=recv_sems.at[k - 1],
                device_id=(px, py, pc), device_id_type=MESH)
            cp.start()
            copies.append((cp, 4 * px + 2 * py + pc))
        for k, (cp, peer) in enumerate(copies):
            pltpu.make_async_remote_copy(
                src_ref=v_ref, dst_ref=out_ref.at[peer],
                send_sem=send_sems.at[k], recv_sem=recv_sems.at[k],
                device_id=(x, y, c), device_id_type=MESH).wait_recv()
        for cp, _ in copies:
            cp.wait_send()

    return pl.pallas_call(
        _after(body, 1, deps), name=name,
        out_shape=jax.ShapeDtypeStruct((N_DEV, rows, LANES), F32),
        in_specs=[VMEM_SPEC] + [ANY] * len(deps), out_specs=VMEM_SPEC,
        scratch_shapes=[pltpu.SemaphoreType.DMA((N_DEV - 1,)),
                        pltpu.SemaphoreType.DMA((N_DEV - 1,))],
        compiler_params=pltpu.CompilerParams(vmem_limit_bytes=VMEM_LIMIT),
    )(v, *deps)


def _hbm(a):
    return pltpu.with_memory_space_constraint(a, pltpu.HBM)


def _hbm_like(arrays):
    return [pltpu.HBM(a.shape, a.dtype) for a in arrays]


def _remote(src, dst, send, recv, to):
    return pltpu.make_async_remote_copy(src_ref=src, dst_ref=dst, send_sem=send, recv_sem=recv,
                                        device_id=to, device_id_type=MESH)


def _gather_start(shards, lands, units, after, name):
    n, nu = len(shards), len(units)

    def body(*refs):
        ins, lnd = refs[:n], refs[n:2 * n]
        sems = refs[2 * n + len(after):2 * n + len(after) + 2 * nu]
        token = refs[-1]
        x, y, c = _position()
        me = 4 * x + 2 * y + c
        targets = [(x, y, 1 - c)] + [(px, py, c) for px, py in (_flip(x, y, k) for k in (1, 2, 3))]
        for u, (members, ks) in enumerate(units):
            for slot, i in enumerate(members):
                for ki, k in enumerate(ks):
                    at = len(ks) * slot + ki
                    _remote(ins[i], lnd[i].at[me], sems[2 * u].at[at], sems[2 * u + 1].at[at], targets[k]).start()
        token[...] = jnp.zeros_like(token)

    sem_shapes = []
    for members, ks in units:
        count = len(members) * len(ks)
        sem_shapes += [pltpu.SemaphoreType.DMA((count,)), pltpu.SemaphoreType.DMA((count,))]
    out = pl.pallas_call(
        body, name=name,
        out_shape=sem_shapes + _hbm_like(shards) + _hbm_like(lands) + [TOKEN],
        in_specs=[HBM] * (2 * n) + [ANY] * len(after),
        out_specs=[SEM] * (2 * nu) + [HBM] * (2 * n) + [VMEM_SPEC],
        input_output_aliases={i: 2 * nu + i for i in range(2 * n)},
        compiler_params=pltpu.CompilerParams(has_side_effects=EFFECT),
    )(*[_hbm(s) for s in shards], *[_hbm(l) for l in lands], *after)
    sems = [(out[2 * u], out[2 * u + 1]) for u in range(nu)]
    return sems, list(out[2 * nu:2 * nu + n]), list(out[2 * nu + n:2 * nu + 2 * n]), out[-1]


def _gather_forward(shards, lands, ks, send, recv, after, name):
    m = len(shards)
    hops = [k for k in ks if k]
    nsem = 2 if hops else 0

    def body(*refs):
        ins, lnd = refs[:m], refs[m:2 * m]
        send_ref, recv_ref = refs[2 * m], refs[2 * m + 1]
        outs = refs[2 * m + 2 + len(after):]
        token = refs[-1]
        x, y, c = _position()
        me = (x, y, c)
        for slot in range(m):
            for ki, k in enumerate(ks):
                at = len(ks) * slot + ki
                if k:
                    px, py = _flip(x, y, k)
                    block = lnd[slot].at[4 * px + 2 * py + c]
                else:
                    block = lnd[slot].at[4 * x + 2 * y + (1 - c)]
                arrival = _remote(ins[slot], block, send_ref.at[at], recv_ref.at[at], me)
                arrival.wait_recv()
                if k:
                    fat = len(hops) * slot + hops.index(k)
                    _remote(block, block, outs[0].at[fat], outs[1].at[fat], (x, y, 1 - c)).start()
                arrival.wait_send()
        token[...] = jnp.zeros_like(token)

    count = len(hops) * m
    sem_shapes = [pltpu.SemaphoreType.DMA((count,)), pltpu.SemaphoreType.DMA((count,))] if hops else []
    out = pl.pallas_call(
        body, name=name,
        out_shape=sem_shapes + _hbm_like(shards) + _hbm_like(lands) + [TOKEN],
        in_specs=[HBM] * (2 * m) + [SEM, SEM] + [ANY] * len(after),
        out_specs=[SEM] * nsem + [HBM] * (2 * m) + [VMEM_SPEC],
        input_output_aliases={i: nsem + i for i in range(2 * m)},
        compiler_params=pltpu.CompilerParams(has_side_effects=EFFECT),
    )(*shards, *lands, send, recv, *after)
    fwd = (out[0], out[1]) if hops else None
    return fwd, list(out[nsem:nsem + m]), list(out[nsem + m:nsem + 2 * m]), out[-1]


def _gather_finish(lands, ks, fwd, after, name):
    m = len(lands)
    hops = [k for k in ks if k]

    def body(*refs):
        lnd = refs[:m]
        fsend_ref, frecv_ref = refs[m], refs[m + 1]
        x, y, c = _position()
        for slot in range(m):
            for fi, k in enumerate(hops):
                px, py = _flip(x, y, k)
                sent = lnd[slot].at[4 * px + 2 * py + c]
                came = lnd[slot].at[4 * px + 2 * py + (1 - c)]
                fat = len(hops) * slot + fi
                cp = _remote(sent, came, fsend_ref.at[fat], frecv_ref.at[fat], (x, y, c))
                cp.wait_recv()
                cp.wait_send()

    out = pl.pallas_call(
        body, name=name,
        out_shape=_hbm_like(lands),
        in_specs=[HBM] * m + [SEM, SEM] + [ANY] * len(after), out_specs=[HBM] * m,
        input_output_aliases={i: i for i in range(m)},
        compiler_params=pltpu.CompilerParams(has_side_effects=EFFECT),
    )(*lands, fwd[0], fwd[1], *after)
    return list(out)


def _pair_start(parts, name):
    n = len(parts)
    lands = [lax.empty((p.shape[0], 1) + p.shape[2:], p.dtype) for p in parts]

    def body(*refs):
        ins, lnd = refs[:n], refs[n:2 * n]
        send_ref, recv_ref = refs[2 * n], refs[2 * n + 1]
        token = refs[-1]
        x, y, c = _position()
        for i in range(n):
            _remote(ins[i].at[:, pl.ds(1 - c, 1)], lnd[i], send_ref.at[i], recv_ref.at[i], (x, y, 1 - c)).start()
        token[...] = jnp.zeros_like(token)

    out = pl.pallas_call(
        body, name=name,
        out_shape=[pltpu.SemaphoreType.DMA((n,)), pltpu.SemaphoreType.DMA((n,))]
        + _hbm_like(parts) + _hbm_like(lands) + [TOKEN],
        in_specs=[HBM] * (2 * n), out_specs=[SEM, SEM] + [HBM] * (2 * n) + [VMEM_SPEC],
        input_output_aliases={i: 2 + i for i in range(2 * n)},
        compiler_params=pltpu.CompilerParams(has_side_effects=EFFECT),
    )(*[_hbm(p) for p in parts], *[_hbm(l) for l in lands])
    return out[0], out[1], list(out[2:2 + n]), list(out[2 + n:2 + 2 * n]), out[-1]


def _pair_wait(parts, lands, send, recv, after, name):
    n = len(parts)

    def body(*refs):
        ins, lnd = refs[:n], refs[n:2 * n]
        send_ref, recv_ref = refs[2 * n], refs[2 * n + 1]
        x, y, c = _position()
        for i in range(n):
            cp = _remote(ins[i].at[:, pl.ds(1 - c, 1)], lnd[i], send_ref.at[i], recv_ref.at[i], (x, y, c))
            cp.wait_recv()
            cp.wait_send()

    out = pl.pallas_call(
        body, name=name,
        out_shape=_hbm_like(parts) + _hbm_like(lands),
        in_specs=[HBM] * (2 * n) + [SEM, SEM] + [ANY] * len(after), out_specs=[HBM] * (2 * n),
        input_output_aliases={i: i for i in range(2 * n)},
        compiler_params=pltpu.CompilerParams(has_side_effects=EFFECT),
    )(*parts, *lands, send, recv, *after)
    return list(out[:n]), list(out[n:])


def _chip_start(sums, lands, name, flips=(1, 2, 3)):
    n, ns = len(sums), len(flips)

    def body(*refs):
        ins, lnd = refs[:n], refs[n:2 * n]
        send_ref, recv_ref = refs[2 * n], refs[2 * n + 1]
        token = refs[-1]
        x, y, c = _position()
        for i in range(n):
            for j, flip in enumerate(flips):
                px, py = _flip(x, y, flip)
                _remote(ins[i].at[j], lnd[i].at[j], send_ref.at[ns * i + j], recv_ref.at[ns * i + j],
                        (px, py, c)).start()
        token[...] = jnp.zeros_like(token)

    out = pl.pallas_call(
        body, name=name,
        out_shape=[pltpu.SemaphoreType.DMA((ns * n,)), pltpu.SemaphoreType.DMA((ns * n,))]
        + _hbm_like(sums) + _hbm_like(lands) + [TOKEN],
        in_specs=[HBM] * (2 * n), out_specs=[SEM, SEM] + [HBM] * (2 * n) + [VMEM_SPEC],
        input_output_aliases={i: 2 + i for i in range(2 * n)},
        compiler_params=pltpu.CompilerParams(has_side_effects=EFFECT),
    )(*[_hbm(s) for s in sums], *[_hbm(l) for l in lands])
    return out[0], out[1], out[2:2 + n], out[2 + n:2 + 2 * n], out[-1]


def _chip_wait(sums, lands, send, recv, after, name):
    n, ns = len(sums), sums[0].shape[0]

    def body(*refs):
        ins, lnd = refs[:n], refs[n:2 * n]
        send_ref, recv_ref = refs[2 * n], refs[2 * n + 1]
        x, y, c = _position()
        for i in range(n):
            for j in range(ns):
                cp = _remote(ins[i].at[j], lnd[i].at[j], send_ref.at[ns * i + j], recv_ref.at[ns * i + j], (x, y, c))
                cp.wait_recv()
                cp.wait_send()

    out = pl.pallas_call(
        body, name=name,
        out_shape=_hbm_like(sums) + _hbm_like(lands),
        in_specs=[HBM] * (2 * n) + [SEM, SEM] + [ANY] * len(after), out_specs=[HBM] * (2 * n),
        input_output_aliases={i: i for i in range(2 * n)},
        compiler_params=pltpu.CompilerParams(has_side_effects=EFFECT),
    )(*sums, *lands, send, recv, *after)
    return list(out[n:])


def _pair_sum(idx, part, got, name, nslots=3):
    _, _, rows, cols = part.shape
    tr = _tile(rows, 512)

    def body(idx_ref, p_ref, q_ref, o_ref):
        o_ref[...] = (p_ref[...].astype(F32) + q_ref[...].astype(F32)).astype(o_ref.dtype)

    grid_spec = pltpu.PrefetchScalarGridSpec(
        num_scalar_prefetch=1, grid=(nslots, rows // tr),
        in_specs=[pl.BlockSpec((None, None, tr, cols), lambda j, r, idx: (idx[j], idx[4], r, 0)),
                  pl.BlockSpec((None, None, tr, cols), lambda j, r, idx: (idx[j], 0, r, 0))],
        out_specs=pl.BlockSpec((None, tr, cols), lambda j, r, idx: (j, r, 0)))
    return pl.pallas_call(
        body, name=name, grid_spec=grid_spec,
        out_shape=jax.ShapeDtypeStruct((nslots, rows, cols), part.dtype),
        compiler_params=_params("arbitrary", "arbitrary"),
    )(idx, part, got)


def _mm_proj(h, wg, groups, name):
    s, k = h.shape
    nchunk, _, n = wg.shape
    e = nchunk * n // groups
    tn = _tile(min(n, e), 512)

    def body(h_ref, w_ref, o_ref):
        o_ref[...] = jnp.dot(h_ref[...], w_ref[...], preferred_element_type=F32).astype(o_ref.dtype)

    return pl.pallas_call(
        body, name=name, grid=(nchunk * n // tn,),
        in_specs=[pl.BlockSpec((s, k), lambda j: (0, 0)),
                  pl.BlockSpec((None, k, tn), lambda j: ((j * tn) // n, 0, ((j * tn) % n) // tn))],
        out_specs=pl.BlockSpec((None, s, tn), lambda j: ((j * tn) // e, 0, ((j * tn) % e) // tn)),
        out_shape=jax.ShapeDtypeStruct((groups, s, e), BF16),
        compiler_params=_params("arbitrary"),
    )(h, wg)


def _mm_proj_group(h, wg, idx, pos, prev, name, deps=()):
    s, k = h.shape
    nchunk, _, n = wg.shape
    groups, _, e = prev.shape
    per = nchunk // groups
    assert per * n == e
    tn = _tile(n, 512)
    nd = len(deps)

    def body(idx_ref, h_ref, w_ref, prev_ref, *rest):
        o_ref = rest[nd]
        o_ref[...] = jnp.dot(h_ref[...], w_ref[...], preferred_element_type=F32).astype(o_ref.dtype)

    grid_spec = pltpu.PrefetchScalarGridSpec(
        num_scalar_prefetch=1, grid=(e // tn,),
        in_specs=[pl.BlockSpec((s, k), lambda j, idx: (0, 0)),
                  pl.BlockSpec((None, k, tn), lambda j, idx: (per * idx[pos] + (j * tn) // n, 0, ((j * tn) % n) // tn)),
                  ANY] + [ANY] * nd,
        out_specs=pl.BlockSpec((None, s, tn), lambda j, idx: (idx[pos], 0, j)))
    return pl.pallas_call(
        body, name=name, grid_spec=grid_spec,
        out_shape=jax.ShapeDtypeStruct(prev.shape, prev.dtype),
        input_output_aliases={3: 0},
        compiler_params=_params("arbitrary"),
    )(idx, h, wg, prev, *deps)


def _mm_out(yb, w, x, gate, name):
    s, k = yb.shape
    d = w.shape[1]
    tn = _tile(d, 512)
    tk = _tile(k, 1024)
    nk = k // tk

    def body(a_ref, w_ref, x_ref, g_ref, xo_ref, y_ref, acc_ref):
        kk = pl.program_id(1)

        @pl.when(kk == 0)
        def _():
            acc_ref[...] = jnp.zeros_like(acc_ref)

        acc_ref[...] += jnp.dot(a_ref[...], w_ref[...], preferred_element_type=F32)

        @pl.when(kk == nk - 1)
        def _():
            y = acc_ref[...]
            y_ref[...] = y.astype(y_ref.dtype)
            xo_ref[...] = x_ref[...] + g_ref[...] * y

    return pl.pallas_call(
        body, name=name, grid=(d // tn, nk),
        in_specs=[pl.BlockSpec((s, tk), lambda j, kk: (0, kk)),
                  pl.BlockSpec((tk, tn), lambda j, kk: (kk, j)),
                  pl.BlockSpec((s, tn), lambda j, kk: (0, j)),
                  pl.BlockSpec((1, tn), lambda j, kk: (0, j))],
        out_specs=[pl.BlockSpec((s, tn), lambda j, kk: (0, j)),
                   pl.BlockSpec((s, tn), lambda j, kk: (0, j))],
        out_shape=[jax.ShapeDtypeStruct((s, d), F32), jax.ShapeDtypeStruct((s, d), BF16)],
        scratch_shapes=[pltpu.VMEM((s, tn), F32)],
        compiler_params=_params("arbitrary", "arbitrary"),
    )(yb, w, x, gate)


def _mm_nt(a3, w3, out_dtype, name, deps=()):
    g, s, ea = a3.shape
    cw, n, nw = w3.shape
    total = g * ea
    assert total == cw * nw
    tk = _tile(min(ea, nw), 1024)
    tn = _tile(n, 1024)
    nk = total // tk

    def body(a_ref, w_ref, o_ref, acc_ref):
        kk = pl.program_id(1)

        @pl.when(kk == 0)
        def _():
            acc_ref[...] = jnp.zeros_like(acc_ref)

        acc_ref[...] += lax.dot_general(a_ref[...], w_ref[...], (((1,), (1,)), ((), ())),
                                        preferred_element_type=F32)

        @pl.when(kk == nk - 1)
        def _():
            o_ref[...] = acc_ref[...].astype(o_ref.dtype)

    return pl.pallas_call(
        _after(body, 2, deps), name=name, grid=(n // tn, nk),
        in_specs=[pl.BlockSpec((None, s, tk), lambda j, kk: ((kk * tk) // ea, 0, ((kk * tk) % ea) // tk)),
                  pl.BlockSpec((None, tn, tk), lambda j, kk: ((kk * tk) // nw, j, ((kk * tk) % nw) // tk))]
        + [ANY] * len(deps),
        out_specs=pl.BlockSpec((s, tn), lambda j, kk: (0, j)),
        out_shape=jax.ShapeDtypeStruct((s, n), out_dtype),
        scratch_shapes=[pltpu.VMEM((s, tn), F32)],
        compiler_params=_params("arbitrary", "arbitrary"),
    )(a3, w3, *deps)


def _mm_tn(a, b3, nchunk, name, deps=()):
    s, ka = a.shape
    g, _, eb = b3.shape
    n = g * eb // nchunk
    tm = _tile(ka, 1024)
    tn = _tile(min(n, eb), 1024)

    def body(a_ref, b_ref, o_ref, at_ref):
        @pl.when(pl.program_id(1) == 0)
        def _():
            at_ref[...] = a_ref[...].astype(F32).T.astype(at_ref.dtype)

        o_ref[...] = jnp.dot(at_ref[...], b_ref[...], preferred_element_type=F32).astype(o_ref.dtype)

    return pl.pallas_call(
        _after(body, 2, deps), name=name, grid=(ka // tm, g * eb // tn),
        in_specs=[pl.BlockSpec((s, tm), lambda i, j: (0, i)),
                  pl.BlockSpec((None, s, tn), lambda i, j: ((j * tn) // eb, 0, ((j * tn) % eb) // tn))]
        + [ANY] * len(deps),
        out_specs=pl.BlockSpec((None, tm, tn), lambda i, j: ((j * tn) // n, i, ((j * tn) % n) // tn)),
        out_shape=jax.ShapeDtypeStruct((nchunk, ka, n), BF16),
        scratch_shapes=[pltpu.VMEM((tm, s), BF16)],
        compiler_params=_params("arbitrary", "arbitrary"),
    )(a, b3, *deps)


def _mm_tn_group(a, b3, idx, pos, nchunk, name, deps=()):
    s, ka = a.shape
    _, _, eb = b3.shape
    n = eb // nchunk
    tm = _tile(ka, 1024)
    tn = _tile(n, 1024)
    nd = len(deps)

    def body(idx_ref, a_ref, b_ref, *rest):
        o_ref, at_ref = rest[nd:]

        @pl.when(pl.program_id(1) == 0)
        def _():
            at_ref[...] = a_ref[...].astype(F32).T.astype(at_ref.dtype)

        o_ref[...] = jnp.dot(at_ref[...], b_ref[...], preferred_element_type=F32).astype(o_ref.dtype)

    grid_spec = pltpu.PrefetchScalarGridSpec(
        num_scalar_prefetch=1, grid=(ka // tm, eb // tn),
        in_specs=[pl.BlockSpec((s, tm), lambda i, j, idx: (0, i)),
                  pl.BlockSpec((None, s, tn), lambda i, j, idx: (idx[pos], 0, j))] + [ANY] * nd,
        out_specs=pl.BlockSpec((None, tm, tn), lambda i, j, idx: ((j * tn) // n, i, ((j * tn) % n) // tn)),
        scratch_shapes=[pltpu.VMEM((tm, s), BF16)])
    return pl.pallas_call(
        body, name=name, grid_spec=grid_spec,
        out_shape=jax.ShapeDtypeStruct((nchunk, ka, n), BF16),
        compiler_params=_params("arbitrary", "arbitrary"),
    )(idx, a, b3, *deps)


def _sigmoid(z):
    return jax.nn.sigmoid(z)


def _shift_down(v, k, fill=0.0, period=None):
    if k == 0:
        return v
    row = lax.broadcasted_iota(jnp.int32, v.shape, 0)
    if period is not None:
        row = row & (period - 1)
    return jnp.where(row >= k, pltpu.roll(v, k, 0), fill)


def _shift_up(v, k, fill=0.0, period=None):
    if k == 0:
        return v
    s = v.shape[0]
    row = lax.broadcasted_iota(jnp.int32, v.shape, 0)
    if period is not None:
        row, s = row & (period - 1), period
    return jnp.where(row < s - k, pltpu.roll(v, v.shape[0] - k, 0), fill)


SCAN_BLOCK = 64


def _scan(a, b, shift):
    s = a.shape[0]
    blk = min(SCAN_BLOCK, s)
    k = 1
    while k < blk:
        b = a * shift(b, k, 0.0, blk) + b
        a = a * shift(a, k, 1.0, blk)
        k *= 2
    nblk = s // blk
    forward = shift is _shift_down
    order = range(nblk) if forward else range(nblk - 1, -1, -1)
    edge = blk - 1 if forward else 0
    out = [None] * nblk
    carry = None
    for i in order:
        h = b[i * blk:(i + 1) * blk]
        if carry is not None:
            h = a[i * blk:(i + 1) * blk] * carry + h
        carry = h[edge:edge + 1]
        out[i] = h
    return jnp.concatenate(out, axis=0) if nblk > 1 else out[0]


def _norm_mod(x, g, scale, shift, name, deps=()):
    s, d = x.shape
    ts = _tile(s, 256)

    def body(x_ref, g_ref, sc_ref, sh_ref, h_ref):
        xv = x_ref[...]
        rstd = lax.rsqrt(jnp.mean(xv * xv, axis=-1, keepdims=True) + EPS)
        nrm = xv * rstd * g_ref[...]
        h_ref[...] = (nrm * (1.0 + sc_ref[...]) + sh_ref[...]).astype(h_ref.dtype)

    vec = pl.BlockSpec((1, d), lambda i: (0, 0))
    return pl.pallas_call(
        _after(body, 4, deps), name=name, grid=(s // ts,),
        in_specs=[pl.BlockSpec((ts, d), lambda i: (i, 0)), vec, vec, vec] + [ANY] * len(deps),
        out_specs=pl.BlockSpec((ts, d), lambda i: (i, 0)),
        out_shape=jax.ShapeDtypeStruct((s, d), BF16),
        compiler_params=_params("arbitrary"),
    )(x, g, scale, shift, *deps)


def _gate_terms(dx, y_ref, gate_ref, dy_ref, dgate_ref):
    dy_ref[...] = (dx * gate_ref[...]).astype(dy_ref.dtype)
    dgate_ref[...] += jnp.sum(dx * y_ref[...].astype(F32), axis=0, keepdims=True)


def _norm_mod_bwd(x, dh, dx_res, g, scale, name, below=None, deps=()):
    s, d = x.shape
    ts = _tile(s, 256)
    nb = 2 if below is not None else 0

    def body(x_ref, dh_ref, dr_ref, g_ref, sc_ref, *rest):
        dx_ref, dsc_ref, dsh_ref, dg_ref = rest[nb:nb + 4]

        @pl.when(pl.program_id(0) == 0)
        def _():
            for ref in rest[nb + 1:nb + 4] + rest[nb + 5:]:
                ref[...] = jnp.zeros_like(ref)

        xv = x_ref[...]
        dh_v = dh_ref[...].astype(F32)
        gv = g_ref[...]
        rstd = lax.rsqrt(jnp.mean(xv * xv, axis=-1, keepdims=True) + EPS)
        xhat = xv * rstd
        dsc_ref[...] += jnp.sum(dh_v * xhat * gv, axis=0, keepdims=True)
        dsh_ref[...] += jnp.sum(dh_v, axis=0, keepdims=True)
        dn = dh_v * (1.0 + sc_ref[...])
        dg_ref[...] += jnp.sum(dn * xhat, axis=0, keepdims=True)
        dxhat = dn * gv
        proj = jnp.mean(dxhat * xhat, axis=-1, keepdims=True)
        dx = dr_ref[...] + rstd * (dxhat - xhat * proj)
        dx_ref[...] = dx
        if nb:
            _gate_terms(dx, rest[0], rest[1], rest[nb + 4], rest[nb + 5])

    row = pl.BlockSpec((ts, d), lambda i: (i, 0))
    vec = pl.BlockSpec((1, d), lambda i: (0, 0))
    extra = list(below) if nb else []
    return pl.pallas_call(
        _after(body, 5 + nb, deps), name=name, grid=(s // ts,),
        in_specs=[row, row, row, vec, vec] + [row, vec][:nb] + [ANY] * len(deps),
        out_specs=[row, vec, vec, vec] + [row, vec][:nb],
        out_shape=[jax.ShapeDtypeStruct((s, d), F32)] + [jax.ShapeDtypeStruct((1, d), F32)] * 3
        + [jax.ShapeDtypeStruct((s, d), BF16), jax.ShapeDtypeStruct((1, d), F32)][:nb],
        compiler_params=_params("arbitrary"),
    )(x, dh, dx_res, g, scale, *extra, *deps)


def _final_loss(x, g, target, y, gate, name):
    s, d = x.shape
    ts = _tile(s, 256)

    def body(x_ref, g_ref, t_ref, y_ref, gate_ref, dx_ref, loss_ref, dg_ref, dy_ref, dgate_ref):
        @pl.when(pl.program_id(0) == 0)
        def _():
            loss_ref[...] = jnp.zeros_like(loss_ref)
            dg_ref[...] = jnp.zeros_like(dg_ref)
            dgate_ref[...] = jnp.zeros_like(dgate_ref)

        xv = x_ref[...]
        gv = g_ref[...]
        rstd = lax.rsqrt(jnp.mean(xv * xv, axis=-1, keepdims=True) + EPS)
        xhat = xv * rstd
        err = xhat * gv - t_ref[...]
        loss_ref[...] += 0.5 * jnp.sum(jnp.mean(err * err, axis=-1, keepdims=True))
        dy = err * (1.0 / d)
        dg_ref[...] += jnp.sum(dy * xhat, axis=0, keepdims=True)
        dxhat = dy * gv
        proj = jnp.mean(dxhat * xhat, axis=-1, keepdims=True)
        dx = rstd * (dxhat - xhat * proj)
        dx_ref[...] = dx
        _gate_terms(dx, y_ref, gate_ref, dy_ref, dgate_ref)

    row = pl.BlockSpec((ts, d), lambda i: (i, 0))
    vec = pl.BlockSpec((1, d), lambda i: (0, 0))
    return pl.pallas_call(
        body, name=name, grid=(s // ts,),
        in_specs=[row, vec, row, row, vec],
        out_specs=[row, pl.BlockSpec((1, LANES), lambda i: (0, 0)), vec, row, vec],
        out_shape=[jax.ShapeDtypeStruct((s, d), F32), jax.ShapeDtypeStruct((1, LANES), F32),
                   jax.ShapeDtypeStruct((1, d), F32), jax.ShapeDtypeStruct((s, d), BF16),
                   jax.ShapeDtypeStruct((1, d), F32)],
        compiler_params=_params("arbitrary"),
    )(x, g, target, y, gate)


def _conv(v, w_ref, width):
    out = w_ref[width - 1:width, :] * v
    for k in range(width - 1):
        out = out + w_ref[k:k + 1, :] * _shift_down(v, width - 1 - k)
    return out


def _sc_fwd(proj, conv_w, name, deps=()):
    _, s, e = proj.shape
    te = _tile(e, 256)
    width = conv_w.shape[0]

    def body(b_ref, c_ref, v_ref, g_ref, w_ref, o_ref):
        cv = c_ref[...].astype(F32) * v_ref[...].astype(F32)
        u = _conv(cv, w_ref, width)
        gv = g_ref[...].astype(F32)
        o_ref[...] = (b_ref[...].astype(F32) * u * (gv * _sigmoid(gv))).astype(o_ref.dtype)

    def part(q):
        return pl.BlockSpec((None, s, te), lambda j, q=q: (q, 0, j))

    return pl.pallas_call(
        _after(body, 5, deps), name=name, grid=(e // te,),
        in_specs=[part(0), part(1), part(2), part(3), pl.BlockSpec((width, te), lambda j: (0, j))]
        + [ANY] * len(deps),
        out_specs=pl.BlockSpec((s, te), lambda j: (0, j)),
        out_shape=jax.ShapeDtypeStruct((s, e), BF16),
        compiler_params=_params("arbitrary"),
    )(proj, proj, proj, proj, conv_w, *deps)


def _sc_bwd(proj, dyb, conv_w, name, deps=()):
    _, s, e = proj.shape
    te = _tile(e, 256)
    width = conv_w.shape[0]

    def body(b_ref, c_ref, v_ref, g_ref, dy_ref, w_ref, dp_ref, vec_ref):
        bv = b_ref[...].astype(F32)
        cvl = c_ref[...].astype(F32)
        vv = v_ref[...].astype(F32)
        gv = g_ref[...].astype(F32)
        dyv = dy_ref[...].astype(F32)
        cv = cvl * vv
        u = _conv(cv, w_ref, width)
        sg = _sigmoid(gv)
        silu = gv * sg
        dp_ref[0] = (dyv * u * silu).astype(dp_ref.dtype)
        du = dyv * bv * silu
        dp_ref[3] = (dyv * bv * u * (sg * (1.0 + gv * (1.0 - sg)))).astype(dp_ref.dtype)
        dcv = w_ref[width - 1:width, :] * du
        vec_ref[...] = jnp.zeros_like(vec_ref)
        vec_ref[width - 1:width, :] = jnp.sum(du * cv, axis=0, keepdims=True)
        for k in range(width - 1):
            sh = width - 1 - k
            dcv = dcv + w_ref[k:k + 1, :] * _shift_up(du, sh)
            vec_ref[k:k + 1, :] = jnp.sum(du * _shift_down(cv, sh), axis=0, keepdims=True)
        dp_ref[1] = (dcv * vv).astype(dp_ref.dtype)
        dp_ref[2] = (dcv * cvl).astype(dp_ref.dtype)

    def part(q):
        return pl.BlockSpec((None, s, te), lambda j, q=q: (q, 0, j))

    return pl.pallas_call(
        _after(body, 6, deps), name=name, grid=(e // te,),
        in_specs=[part(0), part(1), part(2), part(3), pl.BlockSpec((s, te), lambda j: (0, j)),
                  pl.BlockSpec((width, te), lambda j: (0, j))] + [ANY] * len(deps),
        out_specs=[pl.BlockSpec((4, s, te), lambda j: (0, 0, j)),
                   pl.BlockSpec((8, te), lambda j: (0, j))],
        out_shape=[jax.ShapeDtypeStruct((4, s, e), BF16), jax.ShapeDtypeStruct((8, e), F32)],
        compiler_params=_params("arbitrary"),
    )(proj, proj, proj, proj, dyb, conv_w, *deps)


def _lru_gates(v_pre, w_ref, cb_ref, wa_ref, ba_ref, wx_ref, bx_ref, lam_ref, width):
    v = _conv(v_pre, w_ref, width) + cb_ref[...]
    vb = v.astype(BF16)
    r = _sigmoid(jnp.dot(vb, wa_ref[...], preferred_element_type=F32) + ba_ref[...])
    i = _sigmoid(jnp.dot(vb, wx_ref[...], preferred_element_type=F32) + bx_ref[...])
    nl = -lam_ref[...]
    sp = jnp.maximum(nl, 0.0) + jnp.log1p(jnp.exp(-jnp.abs(nl)))
    log_a = (-RGLRU_C) * r * sp
    a = jnp.exp(log_a)
    one_minus_a2 = jnp.tanh(-log_a) * (1.0 + a * a)
    mult = jnp.sqrt(one_minus_a2)
    return v, vb, r, i, sp, a, mult


def _lru_specs(s, dh, heads, width):
    head_col = lambda q: pl.BlockSpec((None, s, dh), lambda h, q=q: (q, 0, h))
    vec = pl.BlockSpec((1, dh), lambda h: (0, h))
    mat = pl.BlockSpec((None, dh, dh), lambda h: (h, 0, 0))
    weights = [pl.BlockSpec((width, dh), lambda h: (0, h)), vec, mat, vec, mat, vec, vec]
    return head_col, weights


def _lru_fwd(proj, conv_w, conv_b, w_a, b_a, w_x, b_x, lam, name, deps=()):
    _, s, e = proj.shape
    heads, dh, _ = w_a.shape
    width = conv_w.shape[0]

    def body(v_ref, g_ref, w_ref, cb_ref, wa_ref, ba_ref, wx_ref, bx_ref, lam_ref, yb_ref, hs_ref):
        v, _, _, i, _, a, mult = _lru_gates(v_ref[...].astype(F32), w_ref, cb_ref, wa_ref, ba_ref,
                                           wx_ref, bx_ref, lam_ref, width)
        hs = _scan(a, mult * i * v, _shift_down)
        hs_ref[...] = hs
        gv = g_ref[...].astype(F32)
        yb_ref[...] = (hs * (gv * _sigmoid(gv))).astype(yb_ref.dtype)

    head_col, weights = _lru_specs(s, dh, heads, width)
    out = pl.BlockSpec((s, dh), lambda h: (0, h))
    return pl.pallas_call(
        _after(body, 9, deps), name=name, grid=(heads,),
        in_specs=[head_col(0), head_col(1)] + weights + [ANY] * len(deps),
        out_specs=[out, out],
        out_shape=[jax.ShapeDtypeStruct((s, e), BF16), jax.ShapeDtypeStruct((s, e), F32)],
        compiler_params=_params("arbitrary"),
    )(proj, proj, conv_w, conv_b, w_a, b_a, w_x, b_x, lam, *deps)


def _lru_bwd(proj, hs, dyb, conv_w, conv_b, w_a, b_a, w_x, b_x, lam, name, deps=()):
    _, s, e = proj.shape
    heads, dh, _ = w_a.shape
    width = conv_w.shape[0]

    def body(v_ref, g_ref, hs_ref, dy_ref, w_ref, cb_ref, wa_ref, ba_ref, wx_ref, bx_ref, lam_ref,
             dp_ref, dwa_ref, dwx_ref, vec_ref):
        v_pre = v_ref[...].astype(F32)
        v, vb, r, i, sp, a, mult = _lru_gates(v_pre, w_ref, cb_ref, wa_ref, ba_ref, wx_ref, bx_ref,
                                              lam_ref, width)
        hs = hs_ref[...]
        gv = g_ref[...].astype(F32)
        dyv = dy_ref[...].astype(F32)
        sg = _sigmoid(gv)
        dp_ref[1] = (dyv * hs * (sg * (1.0 + gv * (1.0 - sg)))).astype(dp_ref.dtype)
        dhs = dyv * (gv * sg)
        d_h = _scan(_shift_up(a, 1), dhs, _shift_up)
        da = d_h * _shift_down(hs, 1)
        iv = i * v
        dlog_a = da * a - (d_h * iv) * (a * a) / mult
        di = d_h * mult * v
        dv = d_h * mult * i
        dzr = dlog_a * (-RGLRU_C) * sp * r * (1.0 - r)
        dzi = di * i * (1.0 - i)
        dsp = jnp.sum(dlog_a * r, axis=0, keepdims=True) * (-RGLRU_C)
        vec_ref[...] = jnp.zeros_like(vec_ref)
        vec_ref[0:1, :] = jnp.sum(dzr, axis=0, keepdims=True)
        vec_ref[1:2, :] = jnp.sum(dzi, axis=0, keepdims=True)
        vec_ref[2:3, :] = -dsp * _sigmoid(-lam_ref[...])
        dzr_b = dzr.astype(BF16)
        dzi_b = dzi.astype(BF16)
        vt = vb.astype(F32).T.astype(BF16)
        dwa_ref[...] = jnp.dot(vt, dzr_b, preferred_element_type=F32).astype(dwa_ref.dtype)
        dwx_ref[...] = jnp.dot(vt, dzi_b, preferred_element_type=F32).astype(dwx_ref.dtype)
        nt = (((1,), (1,)), ((), ()))
        dv = dv + lax.dot_general(dzr_b, wa_ref[...], nt, preferred_element_type=F32)
        dv = dv + lax.dot_general(dzi_b, wx_ref[...], nt, preferred_element_type=F32)
        vec_ref[3:4, :] = jnp.sum(dv, axis=0, keepdims=True)
        dvp = w_ref[width - 1:width, :] * dv
        vec_ref[4 + width - 1:4 + width, :] = jnp.sum(dv * v_pre, axis=0, keepdims=True)
        for k in range(width - 1):
            sh = width - 1 - k
            dvp = dvp + w_ref[k:k + 1, :] * _shift_up(dv, sh)
            vec_ref[4 + k:5 + k, :] = jnp.sum(dv * _shift_down(v_pre, sh), axis=0, keepdims=True)
        dp_ref[0] = dvp.astype(dp_ref.dtype)

    head_col, weights = _lru_specs(s, dh, heads, width)
    col = pl.BlockSpec((s, dh), lambda h: (0, h))
    mat = pl.BlockSpec((None, dh, dh), lambda h: (h, 0, 0))
    return pl.pallas_call(
        _after(body, 11, deps), name=name, grid=(heads,),
        in_specs=[head_col(0), head_col(1), col, col] + weights + [ANY] * len(deps),
        out_specs=[pl.BlockSpec((2, s, dh), lambda h: (0, 0, h)), mat, mat,
                   pl.BlockSpec((16, dh), lambda h: (0, h))],
        out_shape=[jax.ShapeDtypeStruct((2, s, e), BF16),
                   jax.ShapeDtypeStruct((heads, dh, dh), BF16),
                   jax.ShapeDtypeStruct((heads, dh, dh), BF16),
                   jax.ShapeDtypeStruct((16, e), F32)],
        compiler_params=_params("arbitrary"),
    )(proj, proj, hs, dyb, conv_w, conv_b, w_a, b_a, w_x, b_x, lam, *deps)


def _ada_mod(c_all, w, b, name):
    layers, d, f = w.shape
    nb = c_all.shape[0]

    def body(c_ref, w_ref, b_ref, o_ref):
        cv = c_ref[...]
        sc = cv * _sigmoid(cv)
        o_ref[...] = jnp.dot(sc, w_ref[...], preferred_element_type=F32,
                             precision=lax.Precision.HIGHEST) + b_ref[...]

    return pl.pallas_call(
        body, name=name, grid=(layers,),
        in_specs=[pl.BlockSpec((nb, d), lambda l: (0, 0)),
                  pl.BlockSpec((None, d, f), lambda l: (l, 0, 0)),
                  pl.BlockSpec((None, 1, f), lambda l: (l, 0, 0))],
        out_specs=pl.BlockSpec((None, nb, f), lambda l: (l, 0, 0)),
        out_shape=jax.ShapeDtypeStruct((layers, nb, f), F32),
        compiler_params=_params("arbitrary"),
    )(c_all, w, b)


def _ada_grad(c_all_t, dmod, name):
    d, nb = c_all_t.shape
    layers, _, f = dmod.shape

    def body(c_ref, dm_ref, o_ref):
        cv = c_ref[...]
        sc = cv * _sigmoid(cv)
        acc = sc[:, 0:1] * dm_ref[0:1, :]
        for k in range(1, nb):
            acc = acc + sc[:, k:k + 1] * dm_ref[k:k + 1, :]
        o_ref[...] = acc

    return pl.pallas_call(
        body, name=name, grid=(layers,),
        in_specs=[pl.BlockSpec((d, nb), lambda l: (0, 0)),
                  pl.BlockSpec((None, nb, f), lambda l: (l, 0, 0))],
        out_specs=pl.BlockSpec((None, d, f), lambda l: (l, 0, 0)),
        out_shape=jax.ShapeDtypeStruct((layers, d, f), F32),
        compiler_params=_params("arbitrary"),
    )(c_all_t, dmod)


def _device_sum(g, name):
    _, rows, _ = g.shape

    def body(g_ref, o_ref):
        acc = g_ref[0]
        for k in range(1, N_DEV):
            acc = acc + g_ref[k]
        o_ref[...] = acc

    return pl.pallas_call(
        body, name=name,
        in_specs=[VMEM_SPEC], out_specs=VMEM_SPEC,
        out_shape=jax.ShapeDtypeStruct((rows, LANES), F32),
        compiler_params=pltpu.CompilerParams(vmem_limit_bytes=VMEM_LIMIT),
    )(g)


def _adamw_math(w, g, m, v):
    m = ADAM_B1 * m + (1.0 - ADAM_B1) * g
    v = ADAM_B2 * v + (1.0 - ADAM_B2) * (g * g)
    m_hat = m / (1.0 - ADAM_B1 ** ADAM_STEP)
    v_hat = v / (1.0 - ADAM_B2 ** ADAM_STEP)
    delta = -ADAM_LR * (m_hat / (jnp.sqrt(v_hat) + ADAM_EPS) + ADAM_WD * w)
    return delta, m, v


def _adamw(w, g, m, v, name):
    rows, cols = w.shape
    tr = _tile(rows, 256)

    def body(w_ref, g_ref, m_ref, v_ref, d_ref, mo_ref, vo_ref):
        d_ref[...], mo_ref[...], vo_ref[...] = _adamw_math(w_ref[...], g_ref[...], m_ref[...], v_ref[...])

    blk = pl.BlockSpec((tr, cols), lambda i: (i, 0))
    return pl.pallas_call(
        body, name=name, grid=(rows // tr,),
        in_specs=[blk] * 4, out_specs=[blk] * 3,
        out_shape=[jax.ShapeDtypeStruct((rows, cols), F32)] * 3,
        compiler_params=_params("arbitrary"),
    )(w, g, m, v)


def _adamw_reduced(idx, w, m, v, part, got, recvs, name):
    rows, cols = w.shape
    tr = _tile(rows, 256)
    nr = len(recvs)

    def body(idx_ref, w_ref, m_ref, v_ref, p_ref, q_ref, *rest):
        g_ref, d_ref, mo_ref, vo_ref = rest[nr:]
        g = p_ref[...].astype(F32) + q_ref[...].astype(F32)
        for u_ref in rest[:nr]:
            for j in range(u_ref.shape[0]):
                g = g + u_ref[j].astype(F32)
        g_ref[...] = g
        d_ref[...], mo_ref[...], vo_ref[...] = _adamw_math(w_ref[...], g, m_ref[...], v_ref[...])

    blk = pl.BlockSpec((tr, cols), lambda i, idx: (i, 0))
    grid_spec = pltpu.PrefetchScalarGridSpec(
        num_scalar_prefetch=1, grid=(rows // tr,),
        in_specs=[blk, blk, blk,
                  pl.BlockSpec((None, None, tr, cols), lambda i, idx: (idx[3], idx[4], i, 0)),
                  pl.BlockSpec((None, None, tr, cols), lambda i, idx: (idx[3], 0, i, 0))]
        + [pl.BlockSpec((u.shape[0], tr, cols), lambda i, idx: (0, i, 0)) for u in recvs],
        out_specs=[blk] * 4)
    return pl.pallas_call(
        body, name=name, grid_spec=grid_spec,
        out_shape=[jax.ShapeDtypeStruct((rows, cols), F32)] * 4,
        compiler_params=_params("arbitrary"),
    )(idx, w, m, v, part, got, *recvs)


def _pack(vectors):
    flat = jnp.concatenate([v.reshape(-1).astype(F32) for v in vectors])
    pad = (-flat.shape[0]) % (8 * LANES)
    return jnp.pad(flat, (0, pad)).reshape(-1, LANES)


def _unpack(flat, shapes):
    out, off = [], 0
    for shp in shapes:
        size = math.prod(shp)
        out.append(flat[..., off:off + size].reshape(flat.shape[:-1] + tuple(shp)))
        off += size
    return out


def _my_slice(full, me, axis):
    size = full.shape[axis] // N_DEV
    return lax.dynamic_slice_in_dim(full, me * size, size, axis)


def kernel(x, c, norm_g, ada_w, ada_b, sc_w_in, sc_conv_w, sc_w_out, lru_w_in, lru_conv_w, lru_conv_b, lru_w_a, lru_b_a, lru_w_x, lru_b_x, lru_lambda, lru_w_out, final_g, loss_target, m_norm_g, m_ada_w, m_ada_b, m_sc_w_in, m_sc_conv_w, m_sc_w_out, m_lru_w_in, m_lru_conv_w, m_lru_conv_b, m_lru_w_a, m_lru_b_a, m_lru_w_x, m_lru_b_x, m_lru_lambda, m_lru_w_out, m_final_g, v_norm_g, v_ada_w, v_ada_b, v_sc_w_in, v_sc_conv_w, v_sc_w_out, v_lru_w_in, v_lru_conv_w, v_lru_conv_b, v_lru_w_a, v_lru_b_a, v_lru_w_x, v_lru_b_x, v_lru_lambda, v_lru_w_out, v_final_g):
    _, s, d = x.shape
    e = sc_w_out.shape[1] * N_DEV
    heads, dh_s, dh = lru_w_a.shape[1:]
    es = e // N_DEV
    f = ada_w.shape[2]
    mx, my, mc = _position()
    me = 4 * mx + 2 * my + mc
    chip = 2 * mx + my
    idx = jnp.stack([chip ^ 1, chip ^ 2, chip ^ 3, chip, mc]).astype(jnp.int32)

    x0 = x[0]
    target = loss_target[0]

    small_shapes = [(d,), (3, es), (4, es), (es,), (heads, dh_s), (heads, dh_s), (es,)]
    small = _small_gather(_pack([c, sc_conv_w, lru_conv_w, lru_conv_b, lru_b_a, lru_b_x, lru_lambda]),
                          "gather_small_weights").reshape(N_DEV, -1)
    c_all, cw3, cw4, cb, ba, bx, lam = _unpack(small, small_shapes)
    cw3 = cw3.transpose(1, 0, 2).reshape(3, e)
    cw4 = cw4.transpose(1, 0, 2).reshape(4, e)
    cb = cb.reshape(1, e)
    lam = lam.reshape(1, e)
    ba = ba.transpose(1, 0, 2).reshape(1, e)
    bx = bx.transpose(1, 0, 2).reshape(1, e)

    ada_b_mine = _my_slice(ada_b, me, 1).reshape(2, 1, f)
    mod_mine = _ada_mod(c_all, ada_w, ada_b_mine, "ada_mod")
    mod_all = _small_gather(_pack([mod_mine]), "gather_mod")

    shards = [sc_w_in[0].astype(BF16), sc_w_out[0].astype(BF16), lru_w_in[0].astype(BF16),
              lru_w_a[0].reshape(heads * dh_s, dh).astype(BF16),
              lru_w_x[0].reshape(heads * dh_s, dh).astype(BF16), lru_w_out[0].astype(BF16)]
    lands = [lax.dynamic_update_slice(lax.empty((N_DEV,) + sh.shape, BF16), sh[None], (me, 0, 0))
             for sh in shards]
    every = [1, 2, 3, 0]
    units = [([0], [0]), ([0], [1, 2]), ([0], [3]), ([1], every), ([2, 3, 4], every), ([5], every)]
    sems, first_sh, first_ld, started = _gather_start(shards[:1], lands[:1], units[:3], [mod_all],
                                                      "gather_start_first")
    rest_units = [([i - 1 for i in members], ks) for members, ks in units[3:]]
    rest_sems, rest_sh, rest_ld, started_rest = _gather_start(shards[1:], lands[1:], rest_units, [started],
                                                              "gather_start_rest")
    sems, shards, lands = sems + rest_sems, first_sh + rest_sh, first_ld + rest_ld

    def gathered(u, after_forward, name):
        members, ks = units[u]
        fwd, shs, lnd, token = _gather_forward(
            [shards[i] for i in members], [lands[i] for i in members], ks, sems[u][0], sems[u][1],
            after_forward, "gather_forward_" + name)
        for i, sh, ld in zip(members, shs, lnd):
            shards[i], lands[i] = sh, ld

        def finish(after):
            out = _gather_finish([lands[i] for i in members], ks, fwd, after, "gather_finish_" + name)
            for i, ld in zip(members, out):
                lands[i] = ld
            return out

        return token, finish

    mod_all = mod_all.reshape(N_DEV, -1)
    mod_all = mod_all[:, :2 * N_DEV * f].reshape(N_DEV, 2, N_DEV, f)
    mod_all = mod_all.transpose(1, 2, 0, 3).reshape(2, N_DEV, 3 * d)
    mod = lax.dynamic_index_in_dim(mod_all, me, 1, keepdims=False)
    shift = [mod[l:l + 1, 0:d] for l in range(2)]
    scale = [mod[l:l + 1, d:2 * d] for l in range(2)]
    gate = [mod[l:l + 1, 2 * d:3 * d] for l in range(2)]
    ng = [norm_g[l:l + 1] for l in range(2)]
    fg = final_g.reshape(1, d)

    h0 = _norm_mod(x0, ng[0], scale[0], shift[0], "norm_mod_0", deps=[started])
    proj0 = lax.empty((4, s, e), BF16)
    tok, _ = gathered(0, [h0, started_rest], "sc_w_in_own")
    proj0 = _mm_proj_group(h0, lands[0], idx, 3, proj0, "mm_proj_0_own", deps=[tok])
    tok, finish = gathered(1, [proj0], "sc_w_in_near")
    finish([tok])
    proj0 = _mm_proj_group(h0, lands[0], idx, 0, proj0, "mm_proj_0_near_y")
    proj0 = _mm_proj_group(h0, lands[0], idx, 1, proj0, "mm_proj_0_near_x")
    tok, finish = gathered(2, [proj0], "sc_w_in_far")
    wg_in0, = finish([tok])
    proj0 = _mm_proj_group(h0, wg_in0, idx, 2, proj0, "mm_proj_0_far")
    tok, finish = gathered(3, [proj0], "sc_w_out")
    yb0 = _sc_fwd(proj0, cw3, "sc_fwd", deps=[tok])
    w_out0 = finish([yb0])[0].reshape(e, d)
    x1, y0 = _mm_out(yb0, w_out0, x0, gate[0], "mm_out_0")
    tok, finish = gathered(4, [x1], "lru_in")
    h1 = _norm_mod(x1, ng[1], scale[1], shift[1], "norm_mod_1", deps=[tok])
    wg_in1, wg_a, wg_x = finish([h1])
    w_a = wg_a.reshape(N_DEV, heads, dh_s, dh).transpose(1, 0, 2, 3).reshape(heads, dh, dh)
    w_x = wg_x.reshape(N_DEV, heads, dh_s, dh).transpose(1, 0, 2, 3).reshape(heads, dh, dh)
    proj1 = _mm_proj(h1, wg_in1, 2, "mm_proj_1")
    tok, finish = gathered(5, [proj1], "lru_w_out")
    yb1, hs = _lru_fwd(proj1, cw4, cb, w_a, ba, w_x, bx, lam, "lru_fwd", deps=[tok])
    w_out1 = finish([yb1])[0].reshape(e, d)
    x2, y1 = _mm_out(yb1, w_out1, x1, gate[1], "mm_out_1")
    dx2, loss_part, d_fg, dy1, dgate1 = _final_loss(x2, fg, target, y1, gate[1], "final_loss")

    def pieces(g, rows, cols):
        return g.reshape(4, 2, rows, cols)

    def by_rows(g):
        return g.reshape(heads, N_DEV, dh_s, dh).transpose(1, 0, 2, 3).reshape(N_DEV, heads * dh_s, dh)

    def pair_begin(parts, group):
        send, recv, parts, lnd, token = _pair_start(parts, "pair_start_" + group)
        return dict(parts=parts, lands=lnd, send=send, recv=recv, group=group), token

    def scatter_start(pair, names, after):
        group = pair["group"]
        parts, gots = _pair_wait(pair["parts"], pair["lands"], pair["send"], pair["recv"], after,
                                 "pair_wait_" + group)
        sums = [_pair_sum(idx, p, q, "pair_sum_" + nm) for p, q, nm in zip(parts, gots, names)]
        empties = [lax.empty(sm.shape, sm.dtype) for sm in sums]
        send, recv, sums, lnd, token = _chip_start(sums, empties, "chip_start_" + group)
        return dict(parts=parts, gots=gots, names=names, group=group, sums=sums, lands=lnd,
                    send=send, recv=recv), token

    big = {"sc_w_in": (sc_w_in, m_sc_w_in, v_sc_w_in), "sc_w_out": (sc_w_out, m_sc_w_out, v_sc_w_out),
           "lru_w_in": (lru_w_in, m_lru_w_in, v_lru_w_in), "lru_w_a": (lru_w_a, m_lru_w_a, v_lru_w_a),
           "lru_w_x": (lru_w_x, m_lru_w_x, v_lru_w_x), "lru_w_out": (lru_w_out, m_lru_w_out, v_lru_w_out)}
    big_res = {}

    def scatter_finish(rs, after):
        recvs = _chip_wait(rs["sums"], rs["lands"], rs["send"], rs["recv"], after, "chip_wait_" + rs["group"])
        done = []
        for p, q, u, nm in zip(rs["parts"], rs["gots"], recvs, rs["names"]):
            w, m, v = big[nm]
            shp2 = p.shape[2:]
            res = _adamw_reduced(idx, w.reshape(shp2), m.reshape(shp2), v.reshape(shp2), p, q, [u], "adamw_" + nm)
            big_res[nm] = [r.reshape(w.shape) for r in res]
            done.append(res[1])
        return done

    dw_out1 = _mm_tn(yb1, dy1[None], 1, "mm_dw_out_1")
    pair, tok = pair_begin([pieces(dw_out1, es, d)], "lru_w_out")
    dyb1 = _mm_nt(dy1[None], w_out1[None], BF16, "mm_dyb_1", deps=[tok])
    rs1, tok = scatter_start(pair, ["lru_w_out"], [dyb1])
    dproj1, dw_a, dw_x, vecs1 = _lru_bwd(proj1, hs, dyb1, cw4, cb, w_a, ba, w_x, bx, lam, "lru_bwd", deps=[tok])
    done = scatter_finish(rs1, [dproj1])
    dw_in1 = _mm_tn(h1, dproj1, N_DEV, "mm_dw_in_1", deps=done)
    pair, tok = pair_begin([pieces(dw_in1, d, 2 * es), pieces(by_rows(dw_a), heads * dh_s, dh),
                            pieces(by_rows(dw_x), heads * dh_s, dh)], "lru_in")
    dh1 = _mm_nt(dproj1, wg_in1, F32, "mm_dh_1", deps=[tok])
    rs2, tok = scatter_start(pair, ["lru_w_in", "lru_w_a", "lru_w_x"], [dh1])
    dx1, dscale1, dshift1, dng1, dy0, dgate0 = _norm_mod_bwd(x1, dh1, dx2, ng[1], scale[1], "norm_mod_bwd_1",
                                                             below=(y0, gate[0]), deps=[tok])
    dw_out0 = _mm_tn(yb0, dy0[None], 1, "mm_dw_out_0")
    pair, tok = pair_begin([pieces(dw_out0, es, d)], "sc_w_out")
    dyb0 = _mm_nt(dy0[None], w_out0[None], BF16, "mm_dyb_0", deps=[tok])
    rs3, tok = scatter_start(pair, ["sc_w_out"], [dyb0])
    dproj0, vecs0 = _sc_bwd(proj0, dyb0, cw3, "sc_bwd", deps=[tok])
    done = scatter_finish(rs2, [dproj0])
    idx_one = jnp.stack([jnp.zeros_like(mc)] * 4 + [mc]).astype(jnp.int32)
    sc_w_in_steps = []

    def chip_step(j, pair, after):
        (part,), (got,) = _pair_wait(pair["parts"], pair["lands"], pair["send"], pair["recv"], after,
                                     "pair_wait_sc_w_in_%d" % j)
        sm = _pair_sum(idx_one, part, got, "pair_sum_sc_w_in_%d" % j, nslots=1)
        send, recv, sums, lnd, token = _chip_start([sm], [lax.empty(sm.shape, sm.dtype)],
                                                   "chip_start_sc_w_in_%d" % j, flips=(j,))
        sc_w_in_steps.append((sums, lnd, send, recv, j))
        return token

    pending = None
    for j in (3, 1, 2, 0):
        part = _mm_tn_group(h0, dproj0, idx, (j - 1) % 4, 2, "mm_dw_in_0_%d" % j, deps=done)[None]
        pair, tok = pair_begin([part], "sc_w_in_%d" % j)
        done = [tok]
        if pending is not None:
            done.append(chip_step(pending[0], pending[1], [tok]))
        pending = (j, pair)
    dh0 = _mm_nt(dproj0, wg_in0, F32, "mm_dh_0", deps=done)
    pair = pending[1]
    (part,), (got,) = _pair_wait(pair["parts"], pair["lands"], pair["send"], pair["recv"], [dh0],
                                 "pair_wait_sc_w_in_0")
    dx0, dscale0, dshift0, dng0 = _norm_mod_bwd(x0, dh0, dx1, ng[0], scale[0], "norm_mod_bwd_0")
    done = scatter_finish(rs3, [dx0])
    dmod_mine = jnp.concatenate([dshift0, dscale0, dgate0, dshift1, dscale1, dgate1], axis=1)
    end_shapes = [(LANES,), (2, 3 * d), (2, d), (d,), (8, e), (16, e)]
    end_all = _small_gather(
        _pack([loss_part, dmod_mine, jnp.concatenate([dng0, dng1], axis=0), d_fg, vecs0, vecs1]),
        "gather_small_grads", deps=done)
    end_sum = _device_sum(end_all, "sum_small_grads").reshape(-1)
    loss_v, g_ada_b, g_norm_g, g_final_g, sum0, sum1 = _unpack(end_sum, end_shapes)
    loss = loss_v[0]
    dmod_all = _unpack(end_all.reshape(N_DEV, -1), end_shapes)[1].transpose(1, 0, 2)
    dmod_cols = _my_slice(dmod_all, me, 2)
    g_ada_w = _ada_grad(c_all.T, dmod_cols, "ada_grad")

    g_sc_conv_w = _my_slice(sum0[0:3], me, 1)
    g_lru_b_a = _my_slice(sum1[0].reshape(heads, dh), me, 1)
    g_lru_b_x = _my_slice(sum1[1].reshape(heads, dh), me, 1)
    g_lru_lambda = _my_slice(sum1[2:3], me, 1)
    g_lru_conv_b = _my_slice(sum1[3:4], me, 1)
    g_lru_conv_w = _my_slice(sum1[4:8], me, 1)

    ada_res = _adamw(ada_w.reshape(2 * d, f), g_ada_w.reshape(2 * d, f), m_ada_w.reshape(2 * d, f),
                     v_ada_w.reshape(2 * d, f), "adamw_ada_w")
    ada_out = [g_ada_w] + [r.reshape(ada_w.shape) for r in ada_res]

    small_w = [norm_g, ada_b, final_g, sc_conv_w, lru_conv_w, lru_conv_b, lru_b_a, lru_b_x, lru_lambda]
    small_m = [m_norm_g, m_ada_b, m_final_g, m_sc_conv_w, m_lru_conv_w, m_lru_conv_b, m_lru_b_a, m_lru_b_x,
               m_lru_lambda]
    small_v = [v_norm_g, v_ada_b, v_final_g, v_sc_conv_w, v_lru_conv_w, v_lru_conv_b, v_lru_b_a, v_lru_b_x,
               v_lru_lambda]
    small_g = [g_norm_g, g_ada_b, g_final_g, g_sc_conv_w, g_lru_conv_w, g_lru_conv_b, g_lru_b_a, g_lru_b_x,
               g_lru_lambda]
    small_g = [g.reshape(w.shape) for g, w in zip(small_g, small_w)]
    shapes = [w.shape for w in small_w]
    packed = _adamw(_pack(small_w), _pack(small_g), _pack(small_m), _pack(small_v), "adamw_small")
    small_out = [small_g] + [_unpack(p.reshape(-1), shapes) for p in packed]

    after = [packed[0], ada_res[0]]
    recvs = []
    for sums, lnd, send, recv, j in sc_w_in_steps:
        recvs += _chip_wait(sums, lnd, send, recv, after, "chip_wait_sc_w_in_%d" % j)
    shp2 = part.shape[2:]
    res = _adamw_reduced(idx_one, sc_w_in.reshape(shp2), m_sc_w_in.reshape(shp2), v_sc_w_in.reshape(shp2),
                         part, got, recvs, "adamw_sc_w_in")
    big_res["sc_w_in"] = [r.reshape(sc_w_in.shape) for r in res]
    big_out = [big_res[nm] for nm in ("sc_w_in", "sc_w_out", "lru_w_in", "lru_w_a", "lru_w_x", "lru_w_out")]

    def small(kind, i):
        return small_out[kind][i]

    def bigw(kind, i):
        return big_out[i][kind]

    outs = [loss, dx0[None]]
    for kind in range(4):
        outs += [small(kind, 0), ada_out[kind], small(kind, 1), bigw(kind, 0), small(kind, 3), bigw(kind, 1),
                 bigw(kind, 2), small(kind, 4), small(kind, 5), bigw(kind, 3), small(kind, 6), bigw(kind, 4),
                 small(kind, 7), small(kind, 8), bigw(kind, 5), small(kind, 2)]
    return tuple(outs)
```

```python
import math

import jax
import jax.numpy as jnp
from jax import lax
from jax.experimental import pallas as pl
from jax.experimental.pallas import tpu as pltpu

N_DEV = 8
LANES = 128
EPS = 1e-6
RGLRU_C = 8.0
ADAM_LR = 0.001
ADAM_B1 = 0.9
ADAM_B2 = 0.999
ADAM_EPS = 1e-08
ADAM_WD = 0.01
ADAM_STEP = 10
VMEM_LIMIT = 56 * 1024 * 1024
MESH = pl.DeviceIdType.MESH
F32 = jnp.float32
BF16 = jnp.bfloat16
ANY = pl.BlockSpec(memory_space=pl.ANY)
HBM = pl.BlockSpec(memory_space=pltpu.HBM)
SEM = pl.BlockSpec(memory_space=pltpu.SEMAPHORE)
VMEM_SPEC = pl.BlockSpec(memory_space=pltpu.VMEM)
EFFECT = pltpu.SideEffectType.DATAFLOW_SIDE_EFFECTING
TOKEN = jax.ShapeDtypeStruct((8, LANES), jnp.float32)


def _tile(n, pref):
    t = min(n, pref)
    assert n % t == 0, (n, pref)
    return t


def _params(*sem):
    return pltpu.CompilerParams(dimension_semantics=sem, vmem_limit_bytes=VMEM_LIMIT)


def _position():
    return lax.axis_index("x"), lax.axis_index("y"), lax.axis_index("c")


def _flip(x, y, k):
    return (1 - x if k & 2 else x), (1 - y if k & 1 else y)


def _after(body, n_in, deps):
    if not deps:
        return body

    def wrapped(*refs):
        return body(*refs[:n_in], *refs[n_in + len(deps):])

    return wrapped


def _small_gather(v, name, deps=()):
    rows = v.shape[0]

    def body(v_ref, out_ref, send_sems, recv_sems):
        x, y, c = _position()
        me = 4 * x + 2 * y + c
        out_ref[me] = v_ref[...]
        copies = []
        for k in range(1, N_DEV):
            px, py = _flip(x, y, k >> 1)
            pc = 1 - c if k & 1 else c
            cp = pltpu.make_async_remote_copy(
                src_ref=v_ref, dst_ref=out_ref.at[me],
                send_sem=send_sems.at[k - 1], recv_sem=recv_sems.at[k - 1],
                device_id=(px, py, pc), device_id_type=MESH)
            cp.start()
            copies.append((cp, 4 * px + 2 * py + pc))
        for k, (cp, peer) in enumerate(copies):
            pltpu.make_async_remote_copy(
                src_ref=v_ref, dst_ref=out_ref.at[peer],
                send_sem=send_sems.at[k], recv_sem=recv_sems.at[k],
                device_id=(x, y, c), device_id_type=MESH).wait_recv()
        for cp, _ in copies:
            cp.wait_send()

    return pl.pallas_call(
        _after(body, 1, deps), name=name,
        out_shape=jax.ShapeDtypeStruct((N_DEV, rows, LANES), F32),
        in_specs=[VMEM_SPEC] + [ANY] * len(deps), out_specs=VMEM_SPEC,
        scratch_shapes=[pltpu.SemaphoreType.DMA((N_DEV - 1,)),
                        pltpu.SemaphoreType.DMA((N_DEV - 1,))],
        compiler_params=pltpu.CompilerParams(vmem_limit_bytes=VMEM_LIMIT),
    )(v, *deps)


def _hbm(a):
    return pltpu.with_memory_space_constraint(a, pltpu.HBM)


def _hbm_like(arrays):
    return [pltpu.HBM(a.shape, a.dtype) for a in arrays]


def _remote(src, dst, send, recv, to):
    return pltpu.make_async_remote_copy(src_ref=src, dst_ref=dst, send_sem=send, recv_sem=recv,
                                        device_id=to, device_id_type=MESH)


def _gather_start(shards, lands, units, after, name):
    n, nu = len(shards), len(units)

    def body(*refs):
        ins, lnd = refs[:n], refs[n:2 * n]
        sems = refs[2 * n + len(after):2 * n + len(after) + 2 * nu]
        token = refs[-1]
        x, y, c = _position()
        me = 4 * x + 2 * y + c
        targets = [(x, y, 1 - c)] + [(px, py, c) for px, py in (_flip(x, y, k) for k in (1, 2, 3))]
        for u, (members, ks) in enumerate(units):
            for slot, i in enumerate(members):
                for ki, k in enumerate(ks):
                    at = len(ks) * slot + ki
                    _remote(ins[i], lnd[i].at[me], sems[2 * u].at[at], sems[2 * u + 1].at[at], targets[k]).start()
        token[...] = jnp.zeros_like(token)

    sem_shapes = []
    for members, ks in units:
        count = len(members) * len(ks)
        sem_shapes += [pltpu.SemaphoreType.DMA((count,)), pltpu.SemaphoreType.DMA((count,))]
    out = pl.pallas_call(
        body, name=name,
        out_shape=sem_shapes + _hbm_like(shards) + _hbm_like(lands) + [TOKEN],
        in_specs=[HBM] * (2 * n) + [ANY] * len(after),
        out_specs=[SEM] * (2 * nu) + [HBM] * (2 * n) + [VMEM_SPEC],
        input_output_aliases={i: 2 * nu + i for i in range(2 * n)},
        compiler_params=pltpu.CompilerParams(has_side_effects=EFFECT),
    )(*[_hbm(s) for s in shards], *[_hbm(l) for l in lands], *after)
    sems = [(out[2 * u], out[2 * u + 1]) for u in range(nu)]
    return sems, list(out[2 * nu:2 * nu + n]), list(out[2 * nu + n:2 * nu + 2 * n]), out[-1]


def _gather_forward(shards, lands, ks, send, recv, after, name):
    m = len(shards)
    hops = [k for k in ks if k]
    nsem = 2 if hops else 0

    def body(*refs):
        ins, lnd = refs[:m], refs[m:2 * m]
        send_ref, recv_ref = refs[2 * m], refs[2 * m + 1]
        outs = refs[2 * m + 2 + len(after):]
        token = refs[-1]
        x, y, c = _position()
        me = (x, y, c)
        for slot in range(m):
            for ki, k in enumerate(ks):
                at = len(ks) * slot + ki
                if k:
                    px, py = _flip(x, y, k)
                    block = lnd[slot].at[4 * px + 2 * py + c]
                else:
                    block = lnd[slot].at[4 * x + 2 * y + (1 - c)]
                arrival = _remote(ins[slot], block, send_ref.at[at], recv_ref.at[at], me)
                arrival.wait_recv()
                if k:
                    fat = len(hops) * slot + hops.index(k)
                    _remote(block, block, outs[0].at[fat], outs[1].at[fat], (x, y, 1 - c)).start()
                arrival.wait_send()
        token[...] = jnp.zeros_like(token)

    count = len(hops) * m
    sem_shapes = [pltpu.SemaphoreType.DMA((count,)), pltpu.SemaphoreType.DMA((count,))] if hops else []
    out = pl.pallas_call(
        body, name=name,
        out_shape=sem_shapes + _hbm_like(shards) + _hbm_like(lands) + [TOKEN],
        in_specs=[HBM] * (2 * m) + [SEM, SEM] + [ANY] * len(after),
        out_specs=[SEM] * nsem + [HBM] * (2 * m) + [VMEM_SPEC],
        input_output_aliases={i: nsem + i for i in range(2 * m)},
        compiler_params=pltpu.CompilerParams(has_side_effects=EFFECT),
    )(*shards, *lands, send, recv, *after)
    fwd = (out[0], out[1]) if hops else None
    return fwd, list(out[nsem:nsem + m]), list(out[nsem + m:nsem + 2 * m]), out[-1]


def _gather_finish(lands, ks, fwd, after, name):
    m = len(lands)
    hops = [k for k in ks if k]

    def body(*refs):
        lnd = refs[:m]
        fsend_ref, frecv_ref = refs[m], refs[m + 1]
        x, y, c = _position()
        for slot in range(m):
            for fi, k in enumerate(hops):
                px, py = _flip(x, y, k)
                sent = lnd[slot].at[4 * px + 2 * py + c]
                came = lnd[slot].at[4 * px + 2 * py + (1 - c)]
                fat = len(hops) * slot + fi
                cp = _remote(sent, came, fsend_ref.at[fat], frecv_ref.at[fat], (x, y, c))
                cp.wait_recv()
                cp.wait_send()

    out = pl.pallas_call(
        body, name=name,
        out_shape=_hbm_like(lands),
        in_specs=[HBM] * m + [SEM, SEM] + [ANY] * len(after), out_specs=[HBM] * m,
        input_output_aliases={i: i for i in range(m)},
        compiler_params=pltpu.CompilerParams(has_side_effects=EFFECT),
    )(*lands, fwd[0], fwd[1], *after)
    return list(out)


def _pair_start(parts, name):
    n = len(parts)
    lands = [lax.empty((p.shape[0], 1) + p.shape[2:], p.dtype) for p in parts]

    def body(*refs):
        ins, lnd = refs[:n], refs[n:2 * n]
        send_ref, recv_ref = refs[2 * n], refs[2 * n + 1]
        token = refs[-1]
        x, y, c = _position()
        for i in range(n):
            _remote(ins[i].at[:, pl.ds(1 - c, 1)], lnd[i], send_ref.at[i], recv_ref.at[i], (x, y, 1 - c)).start()
        token[...] = jnp.zeros_like(token)

    out = pl.pallas_call(
        body, name=name,
        out_shape=[pltpu.SemaphoreType.DMA((n,)), pltpu.SemaphoreType.DMA((n,))]
        + _hbm_like(parts) + _hbm_like(lands) + [TOKEN],
        in_specs=[HBM] * (2 * n), out_specs=[SEM, SEM] + [HBM] * (2 * n) + [VMEM_SPEC],
        input_output_aliases={i: 2 + i for i in range(2 * n)},
        compiler_params=pltpu.CompilerParams(has_side_effects=EFFECT),
    )(*[_hbm(p) for p in parts], *[_hbm(l) for l in lands])
    return out[0], out[1], list(out[2:2 + n]), list(out[2 + n:2 + 2 * n]), out[-1]


def _pair_wait(parts, lands, send, recv, after, name):
    n = len(parts)

    def body(*refs):
        ins, lnd = refs[:n], refs[n:2 * n]
        send_ref, recv_ref = refs[2 * n], refs[2 * n + 1]
        x, y, c = _position()
        for i in range(n):
            cp = _remote(ins[i].at[:, pl.ds(1 - c, 1)], lnd[i], send_ref.at[i], recv_ref.at[i], (x, y, c))
            cp.wait_recv()
            cp.wait_send()

    out = pl.pallas_call(
        body, name=name,
        out_shape=_hbm_like(parts) + _hbm_like(lands),
        in_specs=[HBM] * (2 * n) + [SEM, SEM] + [ANY] * len(after), out_specs=[HBM] * (2 * n),
        input_output_aliases={i: i for i in range(2 * n)},
        compiler_params=pltpu.CompilerParams(has_side_effects=EFFECT),
    )(*parts, *lands, send, recv, *after)
    return list(out[:n]), list(out[n:])


def _chip_start(sums, lands, name, flips=(1, 2, 3)):
    n, ns = len(sums), len(flips)

    def body(*refs):
        ins, lnd = refs[:n], refs[n:2 * n]
        send_ref, recv_ref = refs[2 * n], refs[2 * n + 1]
        token = refs[-1]
        x, y, c = _position()
        for i in range(n):
            for j, flip in enumerate(flips):
                px, py = _flip(x, y, flip)
                _remote(ins[i].at[j], lnd[i].at[j], send_ref.at[ns * i + j], recv_ref.at[ns * i + j],
                        (px, py, c)).start()
        token[...] = jnp.zeros_like(token)

    out = pl.pallas_call(
        body, name=name,
        out_shape=[pltpu.SemaphoreType.DMA((ns * n,)), pltpu.SemaphoreType.DMA((ns * n,))]
        + _hbm_like(sums) + _hbm_like(lands) + [TOKEN],
        in_specs=[HBM] * (2 * n), out_specs=[SEM, SEM] + [HBM] * (2 * n) + [VMEM_SPEC],
        input_output_aliases={i: 2 + i for i in range(2 * n)},
        compiler_params=pltpu.CompilerParams(has_side_effects=EFFECT),
    )(*[_hbm(s) for s in sums], *[_hbm(l) for l in lands])
    return out[0], out[1], out[2:2 + n], out[2 + n:2 + 2 * n], out[-1]


def _chip_wait(sums, lands, send, recv, after, name):
    n, ns = len(sums), sums[0].shape[0]

    def body(*refs):
        ins, lnd = refs[:n], refs[n:2 * n]
        send_ref, recv_ref = refs[2 * n], refs[2 * n + 1]
        x, y, c = _position()
        for i in range(n):
            for j in range(ns):
                cp = _remote(ins[i].at[j], lnd[i].at[j], send_ref.at[ns * i + j], recv_ref.at[ns * i + j], (x, y, c))
                cp.wait_recv()
                cp.wait_send()

    out = pl.pallas_call(
        body, name=name,
        out_shape=_hbm_like(sums) + _hbm_like(lands),
        in_specs=[HBM] * (2 * n) + [SEM, SEM] + [ANY] * len(after), out_specs=[HBM] * (2 * n),
        input_output_aliases={i: i for i in range(2 * n)},
        compiler_params=pltpu.CompilerParams(has_side_effects=EFFECT),
    )(*sums, *lands, send, recv, *after)
    return list(out[n:])


def _pair_sum(idx, part, got, name, nslots=3):
    _, _, rows, cols = part.shape
    tr = _tile(rows, 512)

    def body(idx_ref, p_ref, q_ref, o_ref):
        o_ref[...] = (p_ref[...].astype(F32) + q_ref[...].astype(F32)).astype(o_ref.dtype)

    grid_spec = pltpu.PrefetchScalarGridSpec(
        num_scalar_prefetch=1, grid=(nslots, rows // tr),
        in_specs=[pl.BlockSpec((None, None, tr, cols), lambda j, r, idx: (idx[j], idx[4], r, 0)),
                  pl.BlockSpec((None, None, tr, cols), lambda j, r, idx: (idx[j], 0, r, 0))],
        out_specs=pl.BlockSpec((None, tr, cols), lambda j, r, idx: (j, r, 0)))
    return pl.pallas_call(
        body, name=name, grid_spec=grid_spec,
        out_shape=jax.ShapeDtypeStruct((nslots, rows, cols), part.dtype),
        compiler_params=_params("arbitrary", "arbitrary"),
    )(idx, part, got)


def _mm_proj(h, wg, groups, name):
    s, k = h.shape
    nchunk, _, n = wg.shape
    e = nchunk * n // groups
    tn = _tile(min(n, e), 512)

    def body(h_ref, w_ref, o_ref):
        o_ref[...] = jnp.dot(h_ref[...], w_ref[...], preferred_element_type=F32).astype(o_ref.dtype)

    return pl.pallas_call(
        body, name=name, grid=(nchunk * n // tn,),
        in_specs=[pl.BlockSpec((s, k), lambda j: (0, 0)),
                  pl.BlockSpec((None, k, tn), lambda j: ((j * tn) // n, 0, ((j * tn) % n) // tn))],
        out_specs=pl.BlockSpec((None, s, tn), lambda j: ((j * tn) // e, 0, ((j * tn) % e) // tn)),
        out_shape=jax.ShapeDtypeStruct((groups, s, e), BF16),
        compiler_params=_params("arbitrary"),
    )(h, wg)


def _mm_proj_group(h, wg, idx, pos, prev, name, deps=()):
    s, k = h.shape
    nchunk, _, n = wg.shape
    groups, _, e = prev.shape
    per = nchunk // groups
    assert per * n == e
    tn = _tile(n, 512)
    nd = len(deps)

    def body(idx_ref, h_ref, w_ref, prev_ref, *rest):
        o_ref = rest[nd]
        o_ref[...] = jnp.dot(h_ref[...], w_ref[...], preferred_element_type=F32).astype(o_ref.dtype)

    grid_spec = pltpu.PrefetchScalarGridSpec(
        num_scalar_prefetch=1, grid=(e // tn,),
        in_specs=[pl.BlockSpec((s, k), lambda j, idx: (0, 0)),
                  pl.BlockSpec((None, k, tn), lambda j, idx: (per * idx[pos] + (j * tn) // n, 0, ((j * tn) % n) // tn)),
                  ANY] + [ANY] * nd,
        out_specs=pl.BlockSpec((None, s, tn), lambda j, idx: (idx[pos], 0, j)))
    return pl.pallas_call(
        body, name=name, grid_spec=grid_spec,
        out_shape=jax.ShapeDtypeStruct(prev.shape, prev.dtype),
        input_output_aliases={3: 0},
        compiler_params=_params("arbitrary"),
    )(idx, h, wg, prev, *deps)


def _mm_out(yb, w, x, gate, name):
    s, k = yb.shape
    d = w.shape[1]
    tn = _tile(d, 512)
    tk = _tile(k, 1024)
    nk = k // tk

    def body(a_ref, w_ref, x_ref, g_ref, xo_ref, y_ref, acc_ref):
        kk = pl.program_id(1)

        @pl.when(kk == 0)
        def _():
            acc_ref[...] = jnp.zeros_like(acc_ref)

        acc_ref[...] += jnp.dot(a_ref[...], w_ref[...], preferred_element_type=F32)

        @pl.when(kk == nk - 1)
        def _():
            y = acc_ref[...]
            y_ref[...] = y.astype(y_ref.dtype)
            xo_ref[...] = x_ref[...] + g_ref[...] * y

    return pl.pallas_call(
        body, name=name, grid=(d // tn, nk),
        in_specs=[pl.BlockSpec((s, tk), lambda j, kk: (0, kk)),
                  pl.BlockSpec((tk, tn), lambda j, kk: (kk, j)),
                  pl.BlockSpec((s, tn), lambda j, kk: (0, j)),
                  pl.BlockSpec((1, tn), lambda j, kk: (0, j))],
        out_specs=[pl.BlockSpec((s, tn), lambda j, kk: (0, j)),
                   pl.BlockSpec((s, tn), lambda j, kk: (0, j))],
        out_shape=[jax.ShapeDtypeStruct((s, d), F32), jax.ShapeDtypeStruct((s, d), BF16)],
        scratch_shapes=[pltpu.VMEM((s, tn), F32)],
        compiler_params=_params("arbitrary", "arbitrary"),
    )(yb, w, x, gate)


def _mm_nt(a3, w3, out_dtype, name, deps=()):
    g, s, ea = a3.shape
    cw, n, nw = w3.shape
    total = g * ea
    assert total == cw * nw
    tk = _tile(min(ea, nw), 1024)
    tn = _tile(n, 1024)
    nk = total // tk

    def body(a_ref, w_ref, o_ref, acc_ref):
        kk = pl.program_id(1)

        @pl.when(kk == 0)
        def _():
            acc_ref[...] = jnp.zeros_like(acc_ref)

        acc_ref[...] += lax.dot_general(a_ref[...], w_ref[...], (((1,), (1,)), ((), ())),
                                        preferred_element_type=F32)

        @pl.when(kk == nk - 1)
        def _():
            o_ref[...] = acc_ref[...].astype(o_ref.dtype)

    return pl.pallas_call(
        _after(body, 2, deps), name=name, grid=(n // tn, nk),
        in_specs=[pl.BlockSpec((None, s, tk), lambda j, kk: ((kk * tk) // ea, 0, ((kk * tk) % ea) // tk)),
                  pl.BlockSpec((None, tn, tk), lambda j, kk: ((kk * tk) // nw, j, ((kk * tk) % nw) // tk))]
        + [ANY] * len(deps),
        out_specs=pl.BlockSpec((s, tn), lambda j, kk: (0, j)),
        out_shape=jax.ShapeDtypeStruct((s, n), out_dtype),
        scratch_shapes=[pltpu.VMEM((s, tn), F32)],
        compiler_params=_params("arbitrary", "arbitrary"),
    )(a3, w3, *deps)


def _mm_tn(a, b3, nchunk, name, deps=()):
    s, ka = a.shape
    g, _, eb = b3.shape
    n = g * eb // nchunk
    tm = _tile(ka, 1024)
    tn = _tile(min(n, eb), 1024)

    def body(a_ref, b_ref, o_ref, at_ref):
        @pl.when(pl.program_id(1) == 0)
        def _():
            at_ref[...] = a_ref[...].astype(F32).T.astype(at_ref.dtype)

        o_ref[...] = jnp.dot(at_ref[...], b_ref[...], preferred_element_type=F32).astype(o_ref.dtype)

    return pl.pallas_call(
        _after(body, 2, deps), name=name, grid=(ka // tm, g * eb // tn),
        in_specs=[pl.BlockSpec((s, tm), lambda i, j: (0, i)),
                  pl.BlockSpec((None, s, tn), lambda i, j: ((j * tn) // eb, 0, ((j * tn) % eb) // tn))]
        + [ANY] * len(deps),
        out_specs=pl.BlockSpec((None, tm, tn), lambda i, j: ((j * tn) // n, i, ((j * tn) % n) // tn)),
        out_shape=jax.ShapeDtypeStruct((nchunk, ka, n), BF16),
        scratch_shapes=[pltpu.VMEM((tm, s), BF16)],
        compiler_params=_params("arbitrary", "arbitrary"),
    )(a, b3, *deps)


def _mm_tn_group(a, b3, idx, pos, nchunk, name, deps=()):
    s, ka = a.shape
    _, _, eb = b3.shape
    n = eb // nchunk
    tm = _tile(ka, 1024)
    tn = _tile(n, 1024)
    nd = len(deps)

    def body(idx_ref, a_ref, b_ref, *rest):
        o_ref, at_ref = rest[nd:]

        @pl.when(pl.program_id(1) == 0)
        def _():
            at_ref[...] = a_ref[...].astype(F32).T.astype(at_ref.dtype)

        o_ref[...] = jnp.dot(at_ref[...], b_ref[...], preferred_element_type=F32).astype(o_ref.dtype)

    grid_spec = pltpu.PrefetchScalarGridSpec(
        num_scalar_prefetch=1, grid=(ka // tm, eb // tn),
        in_specs=[pl.BlockSpec((s, tm), lambda i, j, idx: (0, i)),
                  pl.BlockSpec((None, s, tn), lambda i, j, idx: (idx[pos], 0, j))] + [ANY] * nd,
        out_specs=pl.BlockSpec((None, tm, tn), lambda i, j, idx: ((j * tn) // n, i, ((j * tn) % n) // tn)),
        scratch_shapes=[pltpu.VMEM((tm, s), BF16)])
    return pl.pallas_call(
        body, name=name, grid_spec=grid_spec,
        out_shape=jax.ShapeDtypeStruct((nchunk, ka, n), BF16),
        compiler_params=_params("arbitrary", "arbitrary"),
    )(idx, a, b3, *deps)


def _sigmoid(z):
    return jax.nn.sigmoid(z)


def _shift_down(v, k, fill=0.0, period=None):
    if k == 0:
        return v
    row = lax.broadcasted_iota(jnp.int32, v.shape, 0)
    if period is not None:
        row = row & (period - 1)
    return jnp.where(row >= k, pltpu.roll(v, k, 0), fill)


def _shift_up(v, k, fill=0.0, period=None):
    if k == 0:
        return v
    s = v.shape[0]
    row = lax.broadcasted_iota(jnp.int32, v.shape, 0)
    if period is not None:
        row, s = row & (period - 1), period
    return jnp.where(row < s - k, pltpu.roll(v, v.shape[0] - k, 0), fill)


SCAN_BLOCK = 64


def _scan(a, b, shift):
    s = a.shape[0]
    blk = min(SCAN_BLOCK, s)
    k = 1
    while k < blk:
        b = a * shift(b, k, 0.0, blk) + b
        a = a * shift(a, k, 1.0, blk)
        k *= 2
    nblk = s // blk
    forward = shift is _shift_down
    order = range(nblk) if forward else range(nblk - 1, -1, -1)
    edge = blk - 1 if forward else 0
    out = [None] * nblk
    carry = None
    for i in order:
        h = b[i * blk:(i + 1) * blk]
        if carry is not None:
            h = a[i * blk:(i + 1) * blk] * carry + h
        carry = h[edge:edge + 1]
        out[i] = h
    return jnp.concatenate(out, axis=0) if nblk > 1 else out[0]


def _norm_mod(x, g, scale, shift, name, deps=()):
    s, d = x.shape
    ts = _tile(s, 256)

    def body(x_ref, g_ref, sc_ref, sh_ref, h_ref):
        xv = x_ref[...]
        rstd = lax.rsqrt(jnp.mean(xv * xv, axis=-1, keepdims=True) + EPS)
        nrm = xv * rstd * g_ref[...]
        h_ref[...] = (nrm * (1.0 + sc_ref[...]) + sh_ref[...]).astype(h_ref.dtype)

    vec = pl.BlockSpec((1, d), lambda i: (0, 0))
    return pl.pallas_call(
        _after(body, 4, deps), name=name, grid=(s // ts,),
        in_specs=[pl.BlockSpec((ts, d), lambda i: (i, 0)), vec, vec, vec] + [ANY] * len(deps),
        out_specs=pl.BlockSpec((ts, d), lambda i: (i, 0)),
        out_shape=jax.ShapeDtypeStruct((s, d), BF16),
        compiler_params=_params("arbitrary"),
    )(x, g, scale, shift, *deps)


def _gate_terms(dx, y_ref, gate_ref, dy_ref, dgate_ref):
    dy_ref[...] = (dx * gate_ref[...]).astype(dy_ref.dtype)
    dgate_ref[...] += jnp.sum(dx * y_ref[...].astype(F32), axis=0, keepdims=True)


def _norm_mod_bwd(x, dh, dx_res, g, scale, name, below=None, deps=()):
    s, d = x.shape
    ts = _tile(s, 256)
    nb = 2 if below is not None else 0

    def body(x_ref, dh_ref, dr_ref, g_ref, sc_ref, *rest):
        dx_ref, dsc_ref, dsh_ref, dg_ref = rest[nb:nb + 4]

        @pl.when(pl.program_id(0) == 0)
        def _():
            for ref in rest[nb + 1:nb + 4] + rest[nb + 5:]:
                ref[...] = jnp.zeros_like(ref)

        xv = x_ref[...]
        dh_v = dh_ref[...].astype(F32)
        gv = g_ref[...]
        rstd = lax.rsqrt(jnp.mean(xv * xv, axis=-1, keepdims=True) + EPS)
        xhat = xv * rstd
        dsc_ref[...] += jnp.sum(dh_v * xhat * gv, axis=0, keepdims=True)
        dsh_ref[...] += jnp.sum(dh_v, axis=0, keepdims=True)
        dn = dh_v * (1.0 + sc_ref[...])
        dg_ref[...] += jnp.sum(dn * xhat, axis=0, keepdims=True)
        dxhat = dn * gv
        proj = jnp.mean(dxhat * xhat, axis=-1, keepdims=True)
        dx = dr_ref[...] + rstd * (dxhat - xhat * proj)
        dx_ref[...] = dx
        if nb:
            _gate_terms(dx, rest[0], rest[1], rest[nb + 4], rest[nb + 5])

    row = pl.BlockSpec((ts, d), lambda i: (i, 0))
    vec = pl.BlockSpec((1, d), lambda i: (0, 0))
    extra = list(below) if nb else []
    return pl.pallas_call(
        _after(body, 5 + nb, deps), name=name, grid=(s // ts,),
        in_specs=[row, row, row, vec, vec] + [row, vec][:nb] + [ANY] * len(deps),
        out_specs=[row, vec, vec, vec] + [row, vec][:nb],
        out_shape=[jax.ShapeDtypeStruct((s, d), F32)] + [jax.ShapeDtypeStruct((1, d), F32)] * 3
        + [jax.ShapeDtypeStruct((s, d), BF16), jax.ShapeDtypeStruct((1, d), F32)][:nb],
        compiler_params=_params("arbitrary"),
    )(x, dh, dx_res, g, scale, *extra, *deps)


def _final_loss(x, g, target, y, gate, name):
    s, d = x.shape
    ts = _tile(s, 256)

    def body(x_ref, g_ref, t_ref, y_ref, gate_ref, dx_ref, loss_ref, dg_ref, dy_ref, dgate_ref):
        @pl.when(pl.program_id(0) == 0)
        def _():
            loss_ref[...] = jnp.zeros_like(loss_ref)
            dg_ref[...] = jnp.zeros_like(dg_ref)
            dgate_ref[...] = jnp.zeros_like(dgate_ref)

        xv = x_ref[...]
        gv = g_ref[...]
        rstd = lax.rsqrt(jnp.mean(xv * xv, axis=-1, keepdims=True) + EPS)
        xhat = xv * rstd
        err = xhat * gv - t_ref[...]
        loss_ref[...] += 0.5 * jnp.sum(jnp.mean(err * err, axis=-1, keepdims=True))
        dy = err * (1.0 / d)
        dg_ref[...] += jnp.sum(dy * xhat, axis=0, keepdims=True)
        dxhat = dy * gv
        proj = jnp.mean(dxhat * xhat, axis=-1, keepdims=True)
        dx = rstd * (dxhat - xhat * proj)
        dx_ref[...] = dx
        _gate_terms(dx, y_ref, gate_ref, dy_ref, dgate_ref)

    row = pl.BlockSpec((ts, d), lambda i: (i, 0))
    vec = pl.BlockSpec((1, d), lambda i: (0, 0))
    return pl.pallas_call(
        body, name=name, grid=(s // ts,),
        in_specs=[row, vec, row, row, vec],
        out_specs=[row, pl.BlockSpec((1, LANES), lambda i: (0, 0)), vec, row, vec],
        out_shape=[jax.ShapeDtypeStruct((s, d), F32), jax.ShapeDtypeStruct((1, LANES), F32),
                   jax.ShapeDtypeStruct((1, d), F32), jax.ShapeDtypeStruct((s, d), BF16),
                   jax.ShapeDtypeStruct((1, d), F32)],
        compiler_params=_params("arbitrary"),
    )(x, g, target, y, gate)


def _conv(v, w_ref, width):
    out = w_ref[width - 1:width, :] * v
    for k in range(width - 1):
        out = out + w_ref[k:k + 1, :] * _shift_down(v, width - 1 - k)
    return out


def _sc_fwd(proj, conv_w, name, deps=()):
    _, s, e = proj.shape
    te = _tile(e, 256)
    width = conv_w.shape[0]

    def body(b_ref, c_ref, v_ref, g_ref, w_ref, o_ref):
        cv = c_ref[...].astype(F32) * v_ref[...].astype(F32)
        u = _conv(cv, w_ref, width)
        gv = g_ref[...].astype(F32)
        o_ref[...] = (b_ref[...].astype(F32) * u * (gv * _sigmoid(gv))).astype(o_ref.dtype)

    def part(q):
        return pl.BlockSpec((None, s, te), lambda j, q=q: (q, 0, j))

    return pl.pallas_call(
        _after(body, 5, deps), name=name, grid=(e // te,),
        in_specs=[part(0), part(1), part(2), part(3), pl.BlockSpec((width, te), lambda j: (0, j))]
        + [ANY] * len(deps),
        out_specs=pl.BlockSpec((s, te), lambda j: (0, j)),
        out_shape=jax.ShapeDtypeStruct((s, e), BF16),
        compiler_params=_params("arbitrary"),
    )(proj, proj, proj, proj, conv_w, *deps)


def _sc_bwd(proj, dyb, conv_w, name, deps=()):
    _, s, e = proj.shape
    te = _tile(e, 256)
    width = conv_w.shape[0]

    def body(b_ref, c_ref, v_ref, g_ref, dy_ref, w_ref, dp_ref, vec_ref):
        bv = b_ref[...].astype(F32)
        cvl = c_ref[...].astype(F32)
        vv = v_ref[...].astype(F32)
        gv = g_ref[...].astype(F32)
        dyv = dy_ref[...].astype(F32)
        cv = cvl * vv
        u = _conv(cv, w_ref, width)
        sg = _sigmoid(gv)
        silu = gv * sg
        dp_ref[0] = (dyv * u * silu).astype(dp_ref.dtype)
        du = dyv * bv * silu
        dp_ref[3] = (dyv * bv * u * (sg * (1.0 + gv * (1.0 - sg)))).astype(dp_ref.dtype)
        dcv = w_ref[width - 1:width, :] * du
        vec_ref[...] = jnp.zeros_like(vec_ref)
        vec_ref[width - 1:width, :] = jnp.sum(du * cv, axis=0, keepdims=True)
        for k in range(width - 1):
            sh = width - 1 - k
            dcv = dcv + w_ref[k:k + 1, :] * _shift_up(du, sh)
            vec_ref[k:k + 1, :] = jnp.sum(du * _shift_down(cv, sh), axis=0, keepdims=True)
        dp_ref[1] = (dcv * vv).astype(dp_ref.dtype)
        dp_ref[2] = (dcv * cvl).astype(dp_ref.dtype)

    def part(q):
        return pl.BlockSpec((None, s, te), lambda j, q=q: (q, 0, j))

    return pl.pallas_call(
        _after(body, 6, deps), name=name, grid=(e // te,),
        in_specs=[part(0), part(1), part(2), part(3), pl.BlockSpec((s, te), lambda j: (0, j)),
                  pl.BlockSpec((width, te), lambda j: (0, j))] + [ANY] * len(deps),
        out_specs=[pl.BlockSpec((4, s, te), lambda j: (0, 0, j)),
                   pl.BlockSpec((8, te), lambda j: (0, j))],
        out_shape=[jax.ShapeDtypeStruct((4, s, e), BF16), jax.ShapeDtypeStruct((8, e), F32)],
        compiler_params=_params("arbitrary"),
    )(proj, proj, proj, proj, dyb, conv_w, *deps)


def _lru_gates(v_pre, w_ref, cb_ref, wa_ref, ba_ref, wx_ref, bx_ref, lam_ref, width):
    v = _conv(v_pre, w_ref, width) + cb_ref[...]
    vb = v.astype(BF16)
    r = _sigmoid(jnp.dot(vb, wa_ref[...], preferred_element_type=F32) + ba_ref[...])
    i = _sigmoid(jnp.dot(vb, wx_ref[...], preferred_element_type=F32) + bx_ref[...])
    nl = -lam_ref[...]
    sp = jnp.maximum(nl, 0.0) + jnp.log1p(jnp.exp(-jnp.abs(nl)))
    log_a = (-RGLRU_C) * r * sp
    a = jnp.exp(log_a)
    one_minus_a2 = jnp.tanh(-log_a) * (1.0 + a * a)
    mult = jnp.sqrt(one_minus_a2)
    return v, vb, r, i, sp, a, mult


def _lru_specs(s, dh, heads, width):
    head_col = lambda q: pl.BlockSpec((None, s, dh), lambda h, q=q: (q, 0, h))
    vec = pl.BlockSpec((1, dh), lambda h: (0, h))
    mat = pl.BlockSpec((None, dh, dh), lambda h: (h, 0, 0))
    weights = [pl.BlockSpec((width, dh), lambda h: (0, h)), vec, mat, vec, mat, vec, vec]
    return head_col, weights


def _lru_fwd(proj, conv_w, conv_b, w_a, b_a, w_x, b_x, lam, name, deps=()):
    _, s, e = proj.shape
    heads, dh, _ = w_a.shape
    width = conv_w.shape[0]

    def body(v_ref, g_ref, w_ref, cb_ref, wa_ref, ba_ref, wx_ref, bx_ref, lam_ref, yb_ref, hs_ref):
        v, _, _, i, _, a, mult = _lru_gates(v_ref[...].astype(F32), w_ref, cb_ref, wa_ref, ba_ref,
                                           wx_ref, bx_ref, lam_ref, width)
        hs = _scan(a, mult * i * v, _shift_down)
        hs_ref[...] = hs
        gv = g_ref[...].astype(F32)
        yb_ref[...] = (hs * (gv * _sigmoid(gv))).astype(yb_ref.dtype)

    head_col, weights = _lru_specs(s, dh, heads, width)
    out = pl.BlockSpec((s, dh), lambda h: (0, h))
    return pl.pallas_call(
        _after(body, 9, deps), name=name, grid=(heads,),
        in_specs=[head_col(0), head_col(1)] + weights + [ANY] * len(deps),
        out_specs=[out, out],
        out_shape=[jax.ShapeDtypeStruct((s, e), BF16), jax.ShapeDtypeStruct((s, e), F32)],
        compiler_params=_params("arbitrary"),
    )(proj, proj, conv_w, conv_b, w_a, b_a, w_x, b_x, lam, *deps)


def _lru_bwd(proj, hs, dyb, conv_w, conv_b, w_a, b_a, w_x, b_x, lam, name, deps=()):
    _, s, e = proj.shape
    heads, dh, _ = w_a.shape
    width = conv_w.shape[0]

    def body(v_ref, g_ref, hs_ref, dy_ref, w_ref, cb_ref, wa_ref, ba_ref, wx_ref, bx_ref, lam_ref,
             dp_ref, dwa_ref, dwx_ref, vec_ref):
        v_pre = v_ref[...].astype(F32)
        v, vb, r, i, sp, a, mult = _lru_gates(v_pre, w_ref, cb_ref, wa_ref, ba_ref, wx_ref, bx_ref,
                                              lam_ref, width)
        hs = hs_ref[...]
        gv = g_ref[...].astype(F32)
        dyv = dy_ref[...].astype(F32)
        sg = _sigmoid(gv)
        dp_ref[1] = (dyv * hs * (sg * (1.0 + gv * (1.0 - sg)))).astype(dp_ref.dtype)
        dhs = dyv * (gv * sg)
        d_h = _scan(_shift_up(a, 1), dhs, _shift_up)
        da = d_h * _shift_down(hs, 1)
        iv = i * v
        dlog_a = da * a - (d_h * iv) * (a * a) / mult
        di = d_h * mult * v
        dv = d_h * mult * i
        dzr = dlog_a * (-RGLRU_C) * sp * r * (1.0 - r)
        dzi = di * i * (1.0 - i)
        dsp = jnp.sum(dlog_a * r, axis=0, keepdims=True) * (-RGLRU_C)
        vec_ref[...] = jnp.zeros_like(vec_ref)
        vec_ref[0:1, :] = jnp.sum(dzr, axis=0, keepdims=True)
        vec_ref[1:2, :] = jnp.sum(dzi, axis=0, keepdims=True)
        vec_ref[2:3, :] = -dsp * _sigmoid(-lam_ref[...])
        dzr_b = dzr.astype(BF16)
        dzi_b = dzi.astype(BF16)
        vt = vb.astype(F32).T.astype(BF16)
        dwa_ref[...] = jnp.dot(vt, dzr_b, preferred_element_type=F32).astype(dwa_ref.dtype)
        dwx_ref[...] = jnp.dot(vt, dzi_b, preferred_element_type=F32).astype(dwx_ref.dtype)
        nt = (((1,), (1,)), ((), ()))
        dv = dv + lax.dot_general(dzr_b, wa_ref[...], nt, preferred_element_type=F32)
        dv = dv + lax.dot_general(dzi_b, wx_ref[...], nt, preferred_element_type=F32)
        vec_ref[3:4, :] = jnp.sum(dv, axis=0, keepdims=True)
        dvp = w_ref[width - 1:width, :] * dv
        vec_ref[4 + width - 1:4 + width, :] = jnp.sum(dv * v_pre, axis=0, keepdims=True)
        for k in range(width - 1):
            sh = width - 1 - k
            dvp = dvp + w_ref[k:k + 1, :] * _shift_up(dv, sh)
            vec_ref[4 + k:5 + k, :] = jnp.sum(dv * _shift_down(v_pre, sh), axis=0, keepdims=True)
        dp_ref[0] = dvp.astype(dp_ref.dtype)

    head_col, weights = _lru_specs(s, dh, heads, width)
    col = pl.BlockSpec((s, dh), lambda h: (0, h))
    mat = pl.BlockSpec((None, dh, dh), lambda h: (h, 0, 0))
    return pl.pallas_call(
        _after(body, 11, deps), name=name, grid=(heads,),
        in_specs=[head_col(0), head_col(1), col, col] + weights + [ANY] * len(deps),
        out_specs=[pl.BlockSpec((2, s, dh), lambda h: (0, 0, h)), mat, mat,
                   pl.BlockSpec((16, dh), lambda h: (0, h))],
        out_shape=[jax.ShapeDtypeStruct((2, s, e), BF16),
                   jax.ShapeDtypeStruct((heads, dh, dh), BF16),
                   jax.ShapeDtypeStruct((heads, dh, dh), BF16),
                   jax.ShapeDtypeStruct((16, e), F32)],
        compiler_params=_params("arbitrary"),
    )(proj, proj, hs, dyb, conv_w, conv_b, w_a, b_a, w_x, b_x, lam, *deps)


def _ada_mod(c_all, w, b, name):
    layers, d, f = w.shape
    nb = c_all.shape[0]

    def body(c_ref, w_ref, b_ref, o_ref):
        cv = c_ref[...]
        sc = cv * _sigmoid(cv)
        o_ref[...] = jnp.dot(sc, w_ref[...], preferred_element_type=F32,
                             precision=lax.Precision.HIGHEST) + b_ref[...]

    return pl.pallas_call(
        body, name=name, grid=(layers,),
        in_specs=[pl.BlockSpec((nb, d), lambda l: (0, 0)),
                  pl.BlockSpec((None, d, f), lambda l: (l, 0, 0)),
                  pl.BlockSpec((None, 1, f), lambda l: (l, 0, 0))],
        out_specs=pl.BlockSpec((None, nb, f), lambda l: (l, 0, 0)),
        out_shape=jax.ShapeDtypeStruct((layers, nb, f), F32),
        compiler_params=_params("arbitrary"),
    )(c_all, w, b)


def _ada_grad(c_all_t, dmod, name):
    d, nb = c_all_t.shape
    layers, _, f = dmod.shape

    def body(c_ref, dm_ref, o_ref):
        cv = c_ref[...]
        sc = cv * _sigmoid(cv)
        acc = sc[:, 0:1] * dm_ref[0:1, :]
        for k in range(1, nb):
            acc = acc + sc[:, k:k + 1] * dm_ref[k:k + 1, :]
        o_ref[...] = acc

    return pl.pallas_call(
        body, name=name, grid=(layers,),
        in_specs=[pl.BlockSpec((d, nb), lambda l: (0, 0)),
                  pl.BlockSpec((None, nb, f), lambda l: (l, 0, 0))],
        out_specs=pl.BlockSpec((None, d, f), lambda l: (l, 0, 0)),
        out_shape=jax.ShapeDtypeStruct((layers, d, f), F32),
        compiler_params=_params("arbitrary"),
    )(c_all_t, dmod)


def _device_sum(g, name):
    _, rows, _ = g.shape

    def body(g_ref, o_ref):
        acc = g_ref[0]
        for k in range(1, N_DEV):
            acc = acc + g_ref[k]
        o_ref[...] = acc

    return pl.pallas_call(
        body, name=name,
        in_specs=[VMEM_SPEC], out_specs=VMEM_SPEC,
        out_shape=jax.ShapeDtypeStruct((rows, LANES), F32),
        compiler_params=pltpu.CompilerParams(vmem_limit_bytes=VMEM_LIMIT),
    )(g)


def _adamw_math(w, g, m, v):
    m = ADAM_B1 * m + (1.0 - ADAM_B1) * g
    v = ADAM_B2 * v + (1.0 - ADAM_B2) * (g * g)
    m_hat = m / (1.0 - ADAM_B1 ** ADAM_STEP)
    v_hat = v / (1.0 - ADAM_B2 ** ADAM_STEP)
    delta = -ADAM_LR * (m_hat / (jnp.sqrt(v_hat) + ADAM_EPS) + ADAM_WD * w)
    return delta, m, v


def _adamw(w, g, m, v, name):
    rows, cols = w.shape
    tr = _tile(rows, 256)

    def body(w_ref, g_ref, m_ref, v_ref, d_ref, mo_ref, vo_ref):
        d_ref[...], mo_ref[...], vo_ref[...] = _adamw_math(w_ref[...], g_ref[...], m_ref[...], v_ref[...])

    blk = pl.BlockSpec((tr, cols), lambda i: (i, 0))
    return pl.pallas_call(
        body, name=name, grid=(rows // tr,),
        in_specs=[blk] * 4, out_specs=[blk] * 3,
        out_shape=[jax.ShapeDtypeStruct((rows, cols), F32)] * 3,
        compiler_params=_params("arbitrary"),
    )(w, g, m, v)


def _adamw_reduced(idx, w, m, v, part, got, recvs, name):
    rows, cols = w.shape
    tr = _tile(rows, 256)
    nr = len(recvs)

    def body(idx_ref, w_ref, m_ref, v_ref, p_ref, q_ref, *rest):
        g_ref, d_ref, mo_ref, vo_ref = rest[nr:]
        g = p_ref[...].astype(F32) + q_ref[...].astype(F32)
        for u_ref in rest[:nr]:
            for j in range(u_ref.shape[0]):
                g = g + u_ref[j].astype(F32)
        g_ref[...] = g
        d_ref[...], mo_ref[...], vo_ref[...] = _adamw_math(w_ref[...], g, m_ref[...], v_ref[...])

    blk = pl.BlockSpec((tr, cols), lambda i, idx: (i, 0))
    grid_spec = pltpu.PrefetchScalarGridSpec(
        num_scalar_prefetch=1, grid=(rows // tr,),
        in_specs=[blk, blk, blk,
                  pl.BlockSpec((None, None, tr, cols), lambda i, idx: (idx[3], idx[4], i, 0)),
                  pl.BlockSpec((None, None, tr, cols), lambda i, idx: (idx[3], 0, i, 0))]
        + [pl.BlockSpec((u.shape[0], tr, cols), lambda i, idx: (0, i, 0)) for u in recvs],
        out_specs=[blk] * 4)
    return pl.pallas_call(
        body, name=name, grid_spec=grid_spec,
        out_shape=[jax.ShapeDtypeStruct((rows, cols), F32)] * 4,
        compiler_params=_params("arbitrary"),
    )(idx, w, m, v, part, got, *recvs)


def _pack(vectors):
    flat = jnp.concatenate([v.reshape(-1).astype(F32) for v in vectors])
    pad = (-flat.shape[0]) % (8 * LANES)
    return jnp.pad(flat, (0, pad)).reshape(-1, LANES)


def _unpack(flat, shapes):
    out, off = [], 0
    for shp in shapes:
        size = math.prod(shp)
        out.append(flat[..., off:off + size].reshape(flat.shape[:-1] + tuple(shp)))
        off += size
    return out


def _my_slice(full, me, axis):
    size = full.shape[axis] // N_DEV
    return lax.dynamic_slice_in_dim(full, me * size, size, axis)


def kernel(x, c, norm_g, ada_w, ada_b, sc_w_in, sc_conv_w, sc_w_out, lru_w_in, lru_conv_w, lru_conv_b, lru_w_a, lru_b_a, lru_w_x, lru_b_x, lru_lambda, lru_w_out, final_g, loss_target, m_norm_g, m_ada_w, m_ada_b, m_sc_w_in, m_sc_conv_w, m_sc_w_out, m_lru_w_in, m_lru_conv_w, m_lru_conv_b, m_lru_w_a, m_lru_b_a, m_lru_w_x, m_lru_b_x, m_lru_lambda, m_lru_w_out, m_final_g, v_norm_g, v_ada_w, v_ada_b, v_sc_w_in, v_sc_conv_w, v_sc_w_out, v_lru_w_in, v_lru_conv_w, v_lru_conv_b, v_lru_w_a, v_lru_b_a, v_lru_w_x, v_lru_b_x, v_lru_lambda, v_lru_w_out, v_final_g):
    _, s, d = x.shape
    e = sc_w_out.shape[1] * N_DEV
    heads, dh_s, dh = lru_w_a.shape[1:]
    es = e // N_DEV
    f = ada_w.shape[2]
    mx, my, mc = _position()
    me = 4 * mx + 2 * my + mc
    chip = 2 * mx + my
    idx = jnp.stack([chip ^ 1, chip ^ 2, chip ^ 3, chip, mc]).astype(jnp.int32)

    x0 = x[0]
    target = loss_target[0]

    small_shapes = [(d,), (3, es), (4, es), (es,), (heads, dh_s), (heads, dh_s), (es,)]
    small = _small_gather(_pack([c, sc_conv_w, lru_conv_w, lru_conv_b, lru_b_a, lru_b_x, lru_lambda]),
                          "gather_small_weights").reshape(N_DEV, -1)
    c_all, cw3, cw4, cb, ba, bx, lam = _unpack(small, small_shapes)
    cw3 = cw3.transpose(1, 0, 2).reshape(3, e)
    cw4 = cw4.transpose(1, 0, 2).reshape(4, e)
    cb = cb.reshape(1, e)
    lam = lam.reshape(1, e)
    ba = ba.transpose(1, 0, 2).reshape(1, e)
    bx = bx.transpose(1, 0, 2).reshape(1, e)

    ada_b_mine = _my_slice(ada_b, me, 1).reshape(2, 1, f)
    mod_mine = _ada_mod(c_all, ada_w, ada_b_mine, "ada_mod")
    mod_all = _small_gather(_pack([mod_mine]), "gather_mod")

    shards = [sc_w_in[0].astype(BF16), sc_w_out[0].astype(BF16), lru_w_in[0].astype(BF16),
              lru_w_a[0].reshape(heads * dh_s, dh).astype(BF16),
              lru_w_x[0].reshape(heads * dh_s, dh).astype(BF16), lru_w_out[0].astype(BF16)]
    lands = [lax.dynamic_update_slice(lax.empty((N_DEV,) + sh.shape, BF16), sh[None], (me, 0, 0))
             for sh in shards]
    every = [1, 2, 3, 0]
    units = [([0], [0]), ([0], [1]), ([0], [2]), ([0], [3]), ([1], every), ([2, 3, 4], every), ([5], every)]
    sems, first_sh, first_ld, started = _gather_start(shards[:1], lands[:1], units[:4], [mod_all],
                                                      "gather_start_first")
    rest_units = [([i - 1 for i in members], ks) for members, ks in units[4:]]
    rest_sems, rest_sh, rest_ld, started_rest = _gather_start(shards[1:], lands[1:], rest_units, [started],
                                                              "gather_start_rest")
    sems, shards, lands = sems + rest_sems, first_sh + rest_sh, first_ld + rest_ld

    def gathered(u, after_forward, name):
        members, ks = units[u]
        fwd, shs, lnd, token = _gather_forward(
            [shards[i] for i in members], [lands[i] for i in members], ks, sems[u][0], sems[u][1],
            after_forward, "gather_forward_" + name)
        for i, sh, ld in zip(members, shs, lnd):
            shards[i], lands[i] = sh, ld

        def finish(after):
            out = _gather_finish([lands[i] for i in members], ks, fwd, after, "gather_finish_" + name)
            for i, ld in zip(members, out):
                lands[i] = ld
            return out

        return token, finish

    mod_all = mod_all.reshape(N_DEV, -1)
    mod_all = mod_all[:, :2 * N_DEV * f].reshape(N_DEV, 2, N_DEV, f)
    mod_all = mod_all.transpose(1, 2, 0, 3).reshape(2, N_DEV, 3 * d)
    mod = lax.dynamic_index_in_dim(mod_all, me, 1, keepdims=False)
    shift = [mod[l:l + 1, 0:d] for l in range(2)]
    scale = [mod[l:l + 1, d:2 * d] for l in range(2)]
    gate = [mod[l:l + 1, 2 * d:3 * d] for l in range(2)]
    ng = [norm_g[l:l + 1] for l in range(2)]
    fg = final_g.reshape(1, d)

    h0 = _norm_mod(x0, ng[0], scale[0], shift[0], "norm_mod_0", deps=[started])
    proj0 = lax.empty((4, s, e), BF16)
    tok, _ = gathered(0, [h0, started_rest], "sc_w_in_own")
    proj0 = _mm_proj_group(h0, lands[0], idx, 3, proj0, "mm_proj_0_own", deps=[tok])
    for u, name in ((1, "near_y"), (2, "near_x"), (3, "far")):
        tok, finish = gathered(u, [proj0], "sc_w_in_" + name)
        wg_in0, = finish([tok])
        proj0 = _mm_proj_group(h0, wg_in0, idx, u - 1, proj0, "mm_proj_0_" + name)
    tok, finish = gathered(4, [proj0], "sc_w_out")
    yb0 = _sc_fwd(proj0, cw3, "sc_fwd", deps=[tok])
    w_out0 = finish([yb0])[0].reshape(e, d)
    x1, y0 = _mm_out(yb0, w_out0, x0, gate[0], "mm_out_0")
    tok, finish = gathered(5, [x1], "lru_in")
    h1 = _norm_mod(x1, ng[1], scale[1], shift[1], "norm_mod_1", deps=[tok])
    wg_in1, wg_a, wg_x = finish([h1])
    w_a = wg_a.reshape(N_DEV, heads, dh_s, dh).transpose(1, 0, 2, 3).reshape(heads, dh, dh)
    w_x = wg_x.reshape(N_DEV, heads, dh_s, dh).transpose(1, 0, 2, 3).reshape(heads, dh, dh)
    proj1 = _mm_proj(h1, wg_in1, 2, "mm_proj_1")
    tok, finish = gathered(6, [proj1], "lru_w_out")
    yb1, hs = _lru_fwd(proj1, cw4, cb, w_a, ba, w_x, bx, lam, "lru_fwd", deps=[tok])
    w_out1 = finish([yb1])[0].reshape(e, d)
    x2, y1 = _mm_out(yb1, w_out1, x1, gate[1], "mm_out_1")
    dx2, loss_part, d_fg, dy1, dgate1 = _final_loss(x2, fg, target, y1, gate[1], "final_loss")

    def pieces(g, rows, cols):
        return g.reshape(4, 2, rows, cols)

    def by_rows(g):
        return g.reshape(heads, N_DEV, dh_s, dh).transpose(1, 0, 2, 3).reshape(N_DEV, heads * dh_s, dh)

    def pair_begin(parts, group):
        send, recv, parts, lnd, token = _pair_start(parts, "pair_start_" + group)
        return dict(parts=parts, lands=lnd, send=send, recv=recv, group=group), token

    def scatter_start(pair, names, after):
        group = pair["group"]
        parts, gots = _pair_wait(pair["parts"], pair["lands"], pair["send"], pair["recv"], after,
                                 "pair_wait_" + group)
        sums = [_pair_sum(idx, p, q, "pair_sum_" + nm) for p, q, nm in zip(parts, gots, names)]
        empties = [lax.empty(sm.shape, sm.dtype) for sm in sums]
        send, recv, sums, lnd, token = _chip_start(sums, empties, "chip_start_" + group)
        return dict(parts=parts, gots=gots, names=names, group=group, sums=sums, lands=lnd,
                    send=send, recv=recv), token

    big = {"sc_w_in": (sc_w_in, m_sc_w_in, v_sc_w_in), "sc_w_out": (sc_w_out, m_sc_w_out, v_sc_w_out),
           "lru_w_in": (lru_w_in, m_lru_w_in, v_lru_w_in), "lru_w_a": (lru_w_a, m_lru_w_a, v_lru_w_a),
           "lru_w_x": (lru_w_x, m_lru_w_x, v_lru_w_x), "lru_w_out": (lru_w_out, m_lru_w_out, v_lru_w_out)}
    big_res = {}

    def scatter_finish(rs, after):
        recvs = _chip_wait(rs["sums"], rs["lands"], rs["send"], rs["recv"], after, "chip_wait_" + rs["group"])
        done = []
        for p, q, u, nm in zip(rs["parts"], rs["gots"], recvs, rs["names"]):
            w, m, v = big[nm]
            shp2 = p.shape[2:]
            res = _adamw_reduced(idx, w.reshape(shp2), m.reshape(shp2), v.reshape(shp2), p, q, [u], "adamw_" + nm)
            big_res[nm] = [r.reshape(w.shape) for r in res]
            done.append(res[1])
        return done

    dw_out1 = _mm_tn(yb1, dy1[None], 1, "mm_dw_out_1")
    pair, tok = pair_begin([pieces(dw_out1, es, d)], "lru_w_out")
    dyb1 = _mm_nt(dy1[None], w_out1[None], BF16, "mm_dyb_1", deps=[tok])
    rs1, tok = scatter_start(pair, ["lru_w_out"], [dyb1])
    dproj1, dw_a, dw_x, vecs1 = _lru_bwd(proj1, hs, dyb1, cw4, cb, w_a, ba, w_x, bx, lam, "lru_bwd", deps=[tok])
    done = scatter_finish(rs1, [dproj1])
    dw_in1 = _mm_tn(h1, dproj1, N_DEV, "mm_dw_in_1", deps=done)
    pair, tok = pair_begin([pieces(dw_in1, d, 2 * es), pieces(by_rows(dw_a), heads * dh_s, dh),
                            pieces(by_rows(dw_x), heads * dh_s, dh)], "lru_in")
    dh1 = _mm_nt(dproj1, wg_in1, F32, "mm_dh_1", deps=[tok])
    rs2, tok = scatter_start(pair, ["lru_w_in", "lru_w_a", "lru_w_x"], [dh1])
    dx1, dscale1, dshift1, dng1, dy0, dgate0 = _norm_mod_bwd(x1, dh1, dx2, ng[1], scale[1], "norm_mod_bwd_1",
                                                             below=(y0, gate[0]), deps=[tok])
    dw_out0 = _mm_tn(yb0, dy0[None], 1, "mm_dw_out_0")
    pair, tok = pair_begin([pieces(dw_out0, es, d)], "sc_w_out")
    dyb0 = _mm_nt(dy0[None], w_out0[None], BF16, "mm_dyb_0", deps=[tok])
    rs3, tok = scatter_start(pair, ["sc_w_out"], [dyb0])
    dproj0, vecs0 = _sc_bwd(proj0, dyb0, cw3, "sc_bwd", deps=[tok])
    idx_one = jnp.stack([jnp.zeros_like(mc)] * 4 + [mc]).astype(jnp.int32)
    sc_w_in_steps = []

    def chip_step(j, pair, after):
        (part,), (got,) = _pair_wait(pair["parts"], pair["lands"], pair["send"], pair["recv"], after,
                                     "pair_wait_sc_w_in_%d" % j)
        sm = _pair_sum(idx_one, part, got, "pair_sum_sc_w_in_%d" % j, nslots=1)
        send, recv, sums, lnd, token = _chip_start([sm], [lax.empty(sm.shape, sm.dtype)],
                                                   "chip_start_sc_w_in_%d" % j, flips=(j,))
        sc_w_in_steps.append((sums, lnd, send, recv, j))
        return token

    pending, done = None, []
    for j in (3, 1, 2, 0):
        part = _mm_tn_group(h0, dproj0, idx, (j - 1) % 4, 2, "mm_dw_in_0_%d" % j, deps=done)[None]
        pair, tok = pair_begin([part], "sc_w_in_%d" % j)
        if j == 3:
            done = scatter_finish(rs2, [chip_step(j, pair, [tok])])
            continue
        done = [tok]
        if pending is not None:
            done.append(chip_step(pending[0], pending[1], [tok]))
        pending = (j, pair)
    dh0 = _mm_nt(dproj0, wg_in0, F32, "mm_dh_0", deps=done)
    pair = pending[1]
    (part,), (got,) = _pair_wait(pair["parts"], pair["lands"], pair["send"], pair["recv"], [dh0],
                                 "pair_wait_sc_w_in_0")
    dx0, dscale0, dshift0, dng0 = _norm_mod_bwd(x0, dh0, dx1, ng[0], scale[0], "norm_mod_bwd_0")
    done = scatter_finish(rs3, [dx0])
    dmod_mine = jnp.concatenate([dshift0, dscale0, dgate0, dshift1, dscale1, dgate1], axis=1)
    end_shapes = [(LANES,), (2, 3 * d), (2, d), (d,), (8, e), (16, e)]
    end_all = _small_gather(
        _pack([loss_part, dmod_mine, jnp.concatenate([dng0, dng1], axis=0), d_fg, vecs0, vecs1]),
        "gather_small_grads", deps=done)
    end_sum = _device_sum(end_all, "sum_small_grads").reshape(-1)
    loss_v, g_ada_b, g_norm_g, g_final_g, sum0, sum1 = _unpack(end_sum, end_shapes)
    loss = loss_v[0]
    dmod_all = _unpack(end_all.reshape(N_DEV, -1), end_shapes)[1].transpose(1, 0, 2)
    dmod_cols = _my_slice(dmod_all, me, 2)
    g_ada_w = _ada_grad(c_all.T, dmod_cols, "ada_grad")

    g_sc_conv_w = _my_slice(sum0[0:3], me, 1)
    g_lru_b_a = _my_slice(sum1[0].reshape(heads, dh), me, 1)
    g_lru_b_x = _my_slice(sum1[1].reshape(heads, dh), me, 1)
    g_lru_lambda = _my_slice(sum1[2:3], me, 1)
    g_lru_conv_b = _my_slice(sum1[3:4], me, 1)
    g_lru_conv_w = _my_slice(sum1[4:8], me, 1)

    ada_res = _adamw(ada_w.reshape(2 * d, f), g_ada_w.reshape(2 * d, f), m_ada_w.reshape(2 * d, f),
                     v_ada_w.reshape(2 * d, f), "adamw_ada_w")
    ada_out = [g_ada_w] + [r.reshape(ada_w.shape) for r in ada_res]

    small_w = [norm_g, ada_b, final_g, sc_conv_w, lru_conv_w, lru_conv_b, lru_b_a, lru_b_x, lru_lambda]
    small_m = [m_norm_g, m_ada_b, m_final_g, m_sc_conv_w, m_lru_conv_w, m_lru_conv_b, m_lru_b_a, m_lru_b_x,
               m_lru_lambda]
    small_v = [v_norm_g, v_ada_b, v_final_g, v_sc_conv_w, v_lru_conv_w, v_lru_conv_b, v_lru_b_a, v_lru_b_x,
               v_lru_lambda]
    small_g = [g_norm_g, g_ada_b, g_final_g, g_sc_conv_w, g_lru_conv_w, g_lru_conv_b, g_lru_b_a, g_lru_b_x,
               g_lru_lambda]
    small_g = [g.reshape(w.shape) for g, w in zip(small_g, small_w)]
    shapes = [w.shape for w in small_w]
    packed = _adamw(_pack(small_w), _pack(small_g), _pack(small_m), _pack(small_v), "adamw_small")
    small_out = [small_g] + [_unpack(p.reshape(-1), shapes) for p in packed]

    after = [packed[0], ada_res[0]]
    recvs = []
    for sums, lnd, send, recv, j in sc_w_in_steps:
        recvs += _chip_wait(sums, lnd, send, recv, after, "chip_wait_sc_w_in_%d" % j)
    shp2 = part.shape[2:]
    res = _adamw_reduced(idx_one, sc_w_in.reshape(shp2), m_sc_w_in.reshape(shp2), v_sc_w_in.reshape(shp2),
                         part, got, recvs, "adamw_sc_w_in")
    big_res["sc_w_in"] = [r.reshape(sc_w_in.shape) for r in res]
    big_out = [big_res[nm] for nm in ("sc_w_in", "sc_w_out", "lru_w_in", "lru_w_a", "lru_w_x", "lru_w_out")]

    def small(kind, i):
        return small_out[kind][i]

    def bigw(kind, i):
        return big_out[i][kind]

    outs = [loss, dx0[None]]
    for kind in range(4):
        outs += [small(kind, 0), ada_out[kind], small(kind, 1), bigw(kind, 0), small(kind, 3), bigw(kind, 1),
                 bigw(kind, 2), small(kind, 4), small(kind, 5), bigw(kind, 3), small(kind, 6), bigw(kind, 4),
                 small(kind, 7), small(kind, 8), bigw(kind, 5), small(kind, 2)]
    return tuple(outs)
```

```python
import math

import jax
import jax.numpy as jnp
from jax import lax
from jax.experimental import pallas as pl
from jax.experimental.pallas import tpu as pltpu

N_DEV = 8
LANES = 128
EPS = 1e-6
RGLRU_C = 8.0
ADAM_LR = 0.001
ADAM_B1 = 0.9
ADAM_B2 = 0.999
ADAM_EPS = 1e-08
ADAM_WD = 0.01
ADAM_STEP = 10
VMEM_LIMIT = 56 * 1024 * 1024
MESH = pl.DeviceIdType.MESH
F32 = jnp.float32
BF16 = jnp.bfloat16
ANY = pl.BlockSpec(memory_space=pl.ANY)
HBM = pl.BlockSpec(memory_space=pltpu.HBM)
SEM = pl.BlockSpec(memory_space=pltpu.SEMAPHORE)
VMEM_SPEC = pl.BlockSpec(memory_space=pltpu.VMEM)
EFFECT = pltpu.SideEffectType.DATAFLOW_SIDE_EFFECTING
TOKEN = jax.ShapeDtypeStruct((8, LANES), jnp.float32)


def _tile(n, pref):
    t = min(n, pref)
    assert n % t == 0, (n, pref)
    return t


def _params(*sem):
    return pltpu.CompilerParams(dimension_semantics=sem, vmem_limit_bytes=VMEM_LIMIT)


def _position():
    return lax.axis_index("x"), lax.axis_index("y"), lax.axis_index("c")


def _flip(x, y, k):
    return (1 - x if k & 2 else x), (1 - y if k & 1 else y)


def _after(body, n_in, deps):
    if not deps:
        return body

    def wrapped(*refs):
        return body(*refs[:n_in], *refs[n_in + len(deps):])

    return wrapped


def _small_gather(v, name, deps=()):
    rows = v.shape[0]

    def body(v_ref, out_ref, send_sems, recv_sems):
        x, y, c = _position()
        me = 4 * x + 2 * y + c
        out_ref[me] = v_ref[...]
        copies = []
        for k in range(1, N_DEV):
            px, py = _flip(x, y, k >> 1)
            pc = 1 - c if k & 1 else c
            cp = pltpu.make_async_remote_copy(
                src_ref=v_ref, dst_ref=out_ref.at[me],
                send_sem=send_sems.at[k - 1], recv_sem=recv_sems.at[k - 1],
                device_id=(px, py, pc), device_id_type=MESH)
            cp.start()
            copies.append((cp, 4 * px + 2 * py + pc))
        for k, (cp, peer) in enumerate(copies):
            pltpu.make_async_remote_copy(
                src_ref=v_ref, dst_ref=out_ref.at[peer],
                send_sem=send_sems.at[k], recv_sem=recv_sems.at[k],
                device_id=(x, y, c), device_id_type=MESH).wait_recv()
        for cp, _ in copies:
            cp.wait_send()

    return pl.pallas_call(
        _after(body, 1, deps), name=name,
        out_shape=jax.ShapeDtypeStruct((N_DEV, rows, LANES), F32),
        in_specs=[VMEM_SPEC] + [ANY] * len(deps), out_specs=VMEM_SPEC,
        scratch_shapes=[pltpu.SemaphoreType.DMA((N_DEV - 1,)),
                        pltpu.SemaphoreType.DMA((N_DEV - 1,))],
        compiler_params=pltpu.CompilerParams(vmem_limit_bytes=VMEM_LIMIT),
    )(v, *deps)


def _hbm(a):
    return pltpu.with_memory_space_constraint(a, pltpu.HBM)


def _hbm_like(arrays):
    return [pltpu.HBM(a.shape, a.dtype) for a in arrays]


def _remote(src, dst, send, recv, to):
    return pltpu.make_async_remote_copy(src_ref=src, dst_ref=dst, send_sem=send, recv_sem=recv,
                                        device_id=to, device_id_type=MESH)


def _gather_start(shards, lands, units, after, name):
    n, nu = len(shards), len(units)

    def body(*refs):
        ins, lnd = refs[:n], refs[n:2 * n]
        sems = refs[2 * n + len(after):2 * n + len(after) + 2 * nu]
        token = refs[-1]
        x, y, c = _position()
        me = 4 * x + 2 * y + c
        targets = [(x, y, 1 - c)] + [(px, py, c) for px, py in (_flip(x, y, k) for k in (1, 2, 3))]
        for u, (members, ks) in enumerate(units):
            for slot, i in enumerate(members):
                for ki, k in enumerate(ks):
                    at = len(ks) * slot + ki
                    _remote(ins[i], lnd[i].at[me], sems[2 * u].at[at], sems[2 * u + 1].at[at], targets[k]).start()
        token[...] = jnp.zeros_like(token)

    sem_shapes = []
    for members, ks in units:
        count = len(members) * len(ks)
        sem_shapes += [pltpu.SemaphoreType.DMA((count,)), pltpu.SemaphoreType.DMA((count,))]
    out = pl.pallas_call(
        body, name=name,
        out_shape=sem_shapes + _hbm_like(shards) + _hbm_like(lands) + [TOKEN],
        in_specs=[HBM] * (2 * n) + [ANY] * len(after),
        out_specs=[SEM] * (2 * nu) + [HBM] * (2 * n) + [VMEM_SPEC],
        input_output_aliases={i: 2 * nu + i for i in range(2 * n)},
        compiler_params=pltpu.CompilerParams(has_side_effects=EFFECT),
    )(*[_hbm(s) for s in shards], *[_hbm(l) for l in lands], *after)
    sems = [(out[2 * u], out[2 * u + 1]) for u in range(nu)]
    return sems, list(out[2 * nu:2 * nu + n]), list(out[2 * nu + n:2 * nu + 2 * n]), out[-1]


def _gather_forward(shards, lands, ks, send, recv, after, name):
    m = len(shards)
    hops = [k for k in ks if k]
    nsem = 2 if hops else 0

    def body(*refs):
        ins, lnd = refs[:m], refs[m:2 * m]
        send_ref, recv_ref = refs[2 * m], refs[2 * m + 1]
        outs = refs[2 * m + 2 + len(after):]
        token = refs[-1]
        x, y, c = _position()
        me = (x, y, c)
        for slot in range(m):
            for ki, k in enumerate(ks):
                at = len(ks) * slot + ki
                if k:
                    px, py = _flip(x, y, k)
                    block = lnd[slot].at[4 * px + 2 * py + c]
                else:
                    block = lnd[slot].at[4 * x + 2 * y + (1 - c)]
                arrival = _remote(ins[slot], block, send_ref.at[at], recv_ref.at[at], me)
                arrival.wait_recv()
                if k:
                    fat = len(hops) * slot + hops.index(k)
                    _remote(block, block, outs[0].at[fat], outs[1].at[fat], (x, y, 1 - c)).start()
                arrival.wait_send()
        token[...] = jnp.zeros_like(token)

    count = len(hops) * m
    sem_shapes = [pltpu.SemaphoreType.DMA((count,)), pltpu.SemaphoreType.DMA((count,))] if hops else []
    out = pl.pallas_call(
        body, name=name,
        out_shape=sem_shapes + _hbm_like(shards) + _hbm_like(lands) + [TOKEN],
        in_specs=[HBM] * (2 * m) + [SEM, SEM] + [ANY] * len(after),
        out_specs=[SEM] * nsem + [HBM] * (2 * m) + [VMEM_SPEC],
        input_output_aliases={i: nsem + i for i in range(2 * m)},
        compiler_params=pltpu.CompilerParams(has_side_effects=EFFECT),
    )(*shards, *lands, send, recv, *after)
    fwd = (out[0], out[1]) if hops else None
    return fwd, list(out[nsem:nsem + m]), list(out[nsem + m:nsem + 2 * m]), out[-1]


def _gather_finish(lands, ks, fwd, after, name):
    m = len(lands)
    hops = [k for k in ks if k]

    def body(*refs):
        lnd = refs[:m]
        fsend_ref, frecv_ref = refs[m], refs[m + 1]
        x, y, c = _position()
        for slot in range(m):
            for fi, k in enumerate(hops):
                px, py = _flip(x, y, k)
                sent = lnd[slot].at[4 * px + 2 * py + c]
                came = lnd[slot].at[4 * px + 2 * py + (1 - c)]
                fat = len(hops) * slot + fi
                cp = _remote(sent, came, fsend_ref.at[fat], frecv_ref.at[fat], (x, y, c))
                cp.wait_recv()
                cp.wait_send()

    out = pl.pallas_call(
        body, name=name,
        out_shape=_hbm_like(lands),
        in_specs=[HBM] * m + [SEM, SEM] + [ANY] * len(after), out_specs=[HBM] * m,
        input_output_aliases={i: i for i in range(m)},
        compiler_params=pltpu.CompilerParams(has_side_effects=EFFECT),
    )(*lands, fwd[0], fwd[1], *after)
    return list(out)


def _pair_start(parts, name):
    n = len(parts)
    lands = [lax.empty((p.shape[0], 1) + p.shape[2:], p.dtype) for p in parts]

    def body(*refs):
        ins, lnd = refs[:n], refs[n:2 * n]
        send_ref, recv_ref = refs[2 * n], refs[2 * n + 1]
        token = refs[-1]
        x, y, c = _position()
        for i in range(n):
            _remote(ins[i].at[:, pl.ds(1 - c, 1)], lnd[i], send_ref.at[i], recv_ref.at[i], (x, y, 1 - c)).start()
        token[...] = jnp.zeros_like(token)

    out = pl.pallas_call(
        body, name=name,
        out_shape=[pltpu.SemaphoreType.DMA((n,)), pltpu.SemaphoreType.DMA((n,))]
        + _hbm_like(parts) + _hbm_like(lands) + [TOKEN],
        in_specs=[HBM] * (2 * n), out_specs=[SEM, SEM] + [HBM] * (2 * n) + [VMEM_SPEC],
        input_output_aliases={i: 2 + i for i in range(2 * n)},
        compiler_params=pltpu.CompilerParams(has_side_effects=EFFECT),
    )(*[_hbm(p) for p in parts], *[_hbm(l) for l in lands])
    return out[0], out[1], list(out[2:2 + n]), list(out[2 + n:2 + 2 * n]), out[-1]


def _pair_wait(parts, lands, send, recv, after, name):
    n = len(parts)

    def body(*refs):
        ins, lnd = refs[:n], refs[n:2 * n]
        send_ref, recv_ref = refs[2 * n], refs[2 * n + 1]
        x, y, c = _position()
        for i in range(n):
            cp = _remote(ins[i].at[:, pl.ds(1 - c, 1)], lnd[i], send_ref.at[i], recv_ref.at[i], (x, y, c))
            cp.wait_recv()
            cp.wait_send()

    out = pl.pallas_call(
        body, name=name,
        out_shape=_hbm_like(parts) + _hbm_like(lands),
        in_specs=[HBM] * (2 * n) + [SEM, SEM] + [ANY] * len(after), out_specs=[HBM] * (2 * n),
        input_output_aliases={i: i for i in range(2 * n)},
        compiler_params=pltpu.CompilerParams(has_side_effects=EFFECT),
    )(*parts, *lands, send, recv, *after)
    return list(out[:n]), list(out[n:])


def _chip_start(sums, lands, name, flips=(1, 2, 3)):
    n, ns = len(sums), len(flips)

    def body(*refs):
        ins, lnd = refs[:n], refs[n:2 * n]
        send_ref, recv_ref = refs[2 * n], refs[2 * n + 1]
        token = refs[-1]
        x, y, c = _position()
        for i in range(n):
            for j, flip in enumerate(flips):
                px, py = _flip(x, y, flip)
                _remote(ins[i].at[j], lnd[i].at[j], send_ref.at[ns * i + j], recv_ref.at[ns * i + j],
                        (px, py, c)).start()
        token[...] = jnp.zeros_like(token)

    out = pl.pallas_call(
        body, name=name,
        out_shape=[pltpu.SemaphoreType.DMA((ns * n,)), pltpu.SemaphoreType.DMA((ns * n,))]
        + _hbm_like(sums) + _hbm_like(lands) + [TOKEN],
        in_specs=[HBM] * (2 * n), out_specs=[SEM, SEM] + [HBM] * (2 * n) + [VMEM_SPEC],
        input_output_aliases={i: 2 + i for i in range(2 * n)},
        compiler_params=pltpu.CompilerParams(has_side_effects=EFFECT),
    )(*[_hbm(s) for s in sums], *[_hbm(l) for l in lands])
    return out[0], out[1], out[2:2 + n], out[2 + n:2 + 2 * n], out[-1]


def _chip_wait(sums, lands, send, recv, after, name):
    n, ns = len(sums), sums[0].shape[0]

    def body(*refs):
        ins, lnd = refs[:n], refs[n:2 * n]
        send_ref, recv_ref = refs[2 * n], refs[2 * n + 1]
        x, y, c = _position()
        for i in range(n):
            for j in range(ns):
                cp = _remote(ins[i].at[j], lnd[i].at[j], send_ref.at[ns * i + j], recv_ref.at[ns * i + j], (x, y, c))
                cp.wait_recv()
                cp.wait_send()

    out = pl.pallas_call(
        body, name=name,
        out_shape=_hbm_like(sums) + _hbm_like(lands),
        in_specs=[HBM] * (2 * n) + [SEM, SEM] + [ANY] * len(after), out_specs=[HBM] * (2 * n),
        input_output_aliases={i: i for i in range(2 * n)},
        compiler_params=pltpu.CompilerParams(has_side_effects=EFFECT),
    )(*sums, *lands, send, recv, *after)
    return list(out[n:])


def _pair_sum(idx, part, got, name, nslots=3):
    _, _, rows, cols = part.shape
    tr = _tile(rows, 512)

    def body(idx_ref, p_ref, q_ref, o_ref):
        o_ref[...] = (p_ref[...].astype(F32) + q_ref[...].astype(F32)).astype(o_ref.dtype)

    grid_spec = pltpu.PrefetchScalarGridSpec(
        num_scalar_prefetch=1, grid=(nslots, rows // tr),
        in_specs=[pl.BlockSpec((None, None, tr, cols), lambda j, r, idx: (idx[j], idx[4], r, 0)),
                  pl.BlockSpec((None, None, tr, cols), lambda j, r, idx: (idx[j], 0, r, 0))],
        out_specs=pl.BlockSpec((None, tr, cols), lambda j, r, idx: (j, r, 0)))
    return pl.pallas_call(
        body, name=name, grid_spec=grid_spec,
        out_shape=jax.ShapeDtypeStruct((nslots, rows, cols), part.dtype),
        compiler_params=_params("arbitrary", "arbitrary"),
    )(idx, part, got)


def _mm_proj(h, wg, groups, name):
    s, k = h.shape
    nchunk, _, n = wg.shape
    e = nchunk * n // groups
    tn = _tile(min(n, e), 512)

    def body(h_ref, w_ref, o_ref):
        o_ref[...] = jnp.dot(h_ref[...], w_ref[...], preferred_element_type=F32).astype(o_ref.dtype)

    return pl.pallas_call(
        body, name=name, grid=(nchunk * n // tn,),
        in_specs=[pl.BlockSpec((s, k), lambda j: (0, 0)),
                  pl.BlockSpec((None, k, tn), lambda j: ((j * tn) // n, 0, ((j * tn) % n) // tn))],
        out_specs=pl.BlockSpec((None, s, tn), lambda j: ((j * tn) // e, 0, ((j * tn) % e) // tn)),
        out_shape=jax.ShapeDtypeStruct((groups, s, e), BF16),
        compiler_params=_params("arbitrary"),
    )(h, wg)


def _mm_proj_group(h, wg, idx, pos, prev, name, deps=()):
    s, k = h.shape
    nchunk, _, n = wg.shape
    groups, _, e = prev.shape
    per = nchunk // groups
    assert per * n == e
    tn = _tile(n, 512)
    nd = len(deps)

    def body(idx_ref, h_ref, w_ref, prev_ref, *rest):
        o_ref = rest[nd]
        o_ref[...] = jnp.dot(h_ref[...], w_ref[...], preferred_element_type=F32).astype(o_ref.dtype)

    grid_spec = pltpu.PrefetchScalarGridSpec(
        num_scalar_prefetch=1, grid=(e // tn,),
        in_specs=[pl.BlockSpec((s, k), lambda j, idx: (0, 0)),
                  pl.BlockSpec((None, k, tn), lambda j, idx: (per * idx[pos] + (j * tn) // n, 0, ((j * tn) % n) // tn)),
                  ANY] + [ANY] * nd,
        out_specs=pl.BlockSpec((None, s, tn), lambda j, idx: (idx[pos], 0, j)))
    return pl.pallas_call(
        body, name=name, grid_spec=grid_spec,
        out_shape=jax.ShapeDtypeStruct(prev.shape, prev.dtype),
        input_output_aliases={3: 0},
        compiler_params=_params("arbitrary"),
    )(idx, h, wg, prev, *deps)


def _mm_out(yb, w, x, gate, name):
    s, k = yb.shape
    d = w.shape[1]
    tn = _tile(d, 512)
    tk = _tile(k, 1024)
    nk = k // tk

    def body(a_ref, w_ref, x_ref, g_ref, xo_ref, y_ref, acc_ref):
        kk = pl.program_id(1)

        @pl.when(kk == 0)
        def _():
            acc_ref[...] = jnp.zeros_like(acc_ref)

        acc_ref[...] += jnp.dot(a_ref[...], w_ref[...], preferred_element_type=F32)

        @pl.when(kk == nk - 1)
        def _():
            y = acc_ref[...]
            y_ref[...] = y.astype(y_ref.dtype)
            xo_ref[...] = x_ref[...] + g_ref[...] * y

    return pl.pallas_call(
        body, name=name, grid=(d // tn, nk),
        in_specs=[pl.BlockSpec((s, tk), lambda j, kk: (0, kk)),
                  pl.BlockSpec((tk, tn), lambda j, kk: (kk, j)),
                  pl.BlockSpec((s, tn), lambda j, kk: (0, j)),
                  pl.BlockSpec((1, tn), lambda j, kk: (0, j))],
        out_specs=[pl.BlockSpec((s, tn), lambda j, kk: (0, j)),
                   pl.BlockSpec((s, tn), lambda j, kk: (0, j))],
        out_shape=[jax.ShapeDtypeStruct((s, d), F32), jax.ShapeDtypeStruct((s, d), BF16)],
        scratch_shapes=[pltpu.VMEM((s, tn), F32)],
        compiler_params=_params("arbitrary", "arbitrary"),
    )(yb, w, x, gate)


def _mm_nt(a3, w3, out_dtype, name, deps=()):
    g, s, ea = a3.shape
    cw, n, nw = w3.shape
    total = g * ea
    assert total == cw * nw
    tk = _tile(min(ea, nw), 1024)
    tn = _tile(n, 1024)
    nk = total // tk

    def body(a_ref, w_ref, o_ref, acc_ref):
        kk = pl.program_id(1)

        @pl.when(kk == 0)
        def _():
            acc_ref[...] = jnp.zeros_like(acc_ref)

        acc_ref[...] += lax.dot_general(a_ref[...], w_ref[...], (((1,), (1,)), ((), ())),
                                        preferred_element_type=F32)

        @pl.when(kk == nk - 1)
        def _():
            o_ref[...] = acc_ref[...].astype(o_ref.dtype)

    return pl.pallas_call(
        _after(body, 2, deps), name=name, grid=(n // tn, nk),
        in_specs=[pl.BlockSpec((None, s, tk), lambda j, kk: ((kk * tk) // ea, 0, ((kk * tk) % ea) // tk)),
                  pl.BlockSpec((None, tn, tk), lambda j, kk: ((kk * tk) // nw, j, ((kk * tk) % nw) // tk))]
        + [ANY] * len(deps),
        out_specs=pl.BlockSpec((s, tn), lambda j, kk: (0, j)),
        out_shape=jax.ShapeDtypeStruct((s, n), out_dtype),
        scratch_shapes=[pltpu.VMEM((s, tn), F32)],
        compiler_params=_params("arbitrary", "arbitrary"),
    )(a3, w3, *deps)


def _mm_tn(a, b3, nchunk, name, deps=()):
    s, ka = a.shape
    g, _, eb = b3.shape
    n = g * eb // nchunk
    tm = _tile(ka, 1024)
    tn = _tile(min(n, eb), 1024)

    def body(a_ref, b_ref, o_ref, at_ref):
        @pl.when(pl.program_id(1) == 0)
        def _():
            at_ref[...] = a_ref[...].astype(F32).T.astype(at_ref.dtype)

        o_ref[...] = jnp.dot(at_ref[...], b_ref[...], preferred_element_type=F32).astype(o_ref.dtype)

    return pl.pallas_call(
        _after(body, 2, deps), name=name, grid=(ka // tm, g * eb // tn),
        in_specs=[pl.BlockSpec((s, tm), lambda i, j: (0, i)),
                  pl.BlockSpec((None, s, tn), lambda i, j: ((j * tn) // eb, 0, ((j * tn) % eb) // tn))]
        + [ANY] * len(deps),
        out_specs=pl.BlockSpec((None, tm, tn), lambda i, j: ((j * tn) // n, i, ((j * tn) % n) // tn)),
        out_shape=jax.ShapeDtypeStruct((nchunk, ka, n), BF16),
        scratch_shapes=[pltpu.VMEM((tm, s), BF16)],
        compiler_params=_params("arbitrary", "arbitrary"),
    )(a, b3, *deps)


def _mm_tn_group(a, b3, idx, pos, nchunk, name, deps=()):
    s, ka = a.shape
    _, _, eb = b3.shape
    n = eb // nchunk
    tm = _tile(ka, 1024)
    tn = _tile(n, 1024)
    nd = len(deps)

    def body(idx_ref, a_ref, b_ref, *rest):
        o_ref, at_ref = rest[nd:]

        @pl.when(pl.program_id(1) == 0)
        def _():
            at_ref[...] = a_ref[...].astype(F32).T.astype(at_ref.dtype)

        o_ref[...] = jnp.dot(at_ref[...], b_ref[...], preferred_element_type=F32).astype(o_ref.dtype)

    grid_spec = pltpu.PrefetchScalarGridSpec(
        num_scalar_prefetch=1, grid=(ka // tm, eb // tn),
        in_specs=[pl.BlockSpec((s, tm), lambda i, j, idx: (0, i)),
                  pl.BlockSpec((None, s, tn), lambda i, j, idx: (idx[pos], 0, j))] + [ANY] * nd,
        out_specs=pl.BlockSpec((None, tm, tn), lambda i, j, idx: ((j * tn) // n, i, ((j * tn) % n) // tn)),
        scratch_shapes=[pltpu.VMEM((tm, s), BF16)])
    return pl.pallas_call(
        body, name=name, grid_spec=grid_spec,
        out_shape=jax.ShapeDtypeStruct((nchunk, ka, n), BF16),
        compiler_params=_params("arbitrary", "arbitrary"),
    )(idx, a, b3, *deps)


def _sigmoid(z):
    return jax.nn.sigmoid(z)


def _shift_down(v, k, fill=0.0, period=None):
    if k == 0:
        return v
    row = lax.broadcasted_iota(jnp.int32, v.shape, 0)
    if period is not None:
        row = row & (period - 1)
    return jnp.where(row >= k, pltpu.roll(v, k, 0), fill)


def _shift_up(v, k, fill=0.0, period=None):
    if k == 0:
        return v
    s = v.shape[0]
    row = lax.broadcasted_iota(jnp.int32, v.shape, 0)
    if period is not None:
        row, s = row & (period - 1), period
    return jnp.where(row < s - k, pltpu.roll(v, v.shape[0] - k, 0), fill)


SCAN_BLOCK = 64


def _scan(a, b, shift):
    s = a.shape[0]
    blk = min(SCAN_BLOCK, s)
    k = 1
    while k < blk:
        b = a * shift(b, k, 0.0, blk) + b
        a = a * shift(a, k, 1.0, blk)
        k *= 2
    nblk = s // blk
    forward = shift is _shift_down
    order = range(nblk) if forward else range(nblk - 1, -1, -1)
    edge = blk - 1 if forward else 0
    out = [None] * nblk
    carry = None
    for i in order:
        h = b[i * blk:(i + 1) * blk]
        if carry is not None:
            h = a[i * blk:(i + 1) * blk] * carry + h
        carry = h[edge:edge + 1]
        out[i] = h
    return jnp.concatenate(out, axis=0) if nblk > 1 else out[0]


def _norm_mod(x, g, scale, shift, name, deps=()):
    s, d = x.shape
    ts = _tile(s, 256)

    def body(x_ref, g_ref, sc_ref, sh_ref, h_ref):
        xv = x_ref[...]
        rstd = lax.rsqrt(jnp.mean(xv * xv, axis=-1, keepdims=True) + EPS)
        nrm = xv * rstd * g_ref[...]
        h_ref[...] = (nrm * (1.0 + sc_ref[...]) + sh_ref[...]).astype(h_ref.dtype)

    vec = pl.BlockSpec((1, d), lambda i: (0, 0))
    return pl.pallas_call(
        _after(body, 4, deps), name=name, grid=(s // ts,),
        in_specs=[pl.BlockSpec((ts, d), lambda i: (i, 0)), vec, vec, vec] + [ANY] * len(deps),
        out_specs=pl.BlockSpec((ts, d), lambda i: (i, 0)),
        out_shape=jax.ShapeDtypeStruct((s, d), BF16),
        compiler_params=_params("arbitrary"),
    )(x, g, scale, shift, *deps)


def _gate_terms(dx, y_ref, gate_ref, dy_ref, dgate_ref):
    dy_ref[...] = (dx * gate_ref[...]).astype(dy_ref.dtype)
    dgate_ref[...] += jnp.sum(dx * y_ref[...].astype(F32), axis=0, keepdims=True)


def _norm_mod_bwd(x, dh, dx_res, g, scale, name, below=None, deps=()):
    s, d = x.shape
    ts = _tile(s, 256)
    nb = 2 if below is not None else 0

    def body(x_ref, dh_ref, dr_ref, g_ref, sc_ref, *rest):
        dx_ref, dsc_ref, dsh_ref, dg_ref = rest[nb:nb + 4]

        @pl.when(pl.program_id(0) == 0)
        def _():
            for ref in rest[nb + 1:nb + 4] + rest[nb + 5:]:
                ref[...] = jnp.zeros_like(ref)

        xv = x_ref[...]
        dh_v = dh_ref[...].astype(F32)
        gv = g_ref[...]
        rstd = lax.rsqrt(jnp.mean(xv * xv, axis=-1, keepdims=True) + EPS)
        xhat = xv * rstd
        dsc_ref[...] += jnp.sum(dh_v * xhat * gv, axis=0, keepdims=True)
        dsh_ref[...] += jnp.sum(dh_v, axis=0, keepdims=True)
        dn = dh_v * (1.0 + sc_ref[...])
        dg_ref[...] += jnp.sum(dn * xhat, axis=0, keepdims=True)
        dxhat = dn * gv
        proj = jnp.mean(dxhat * xhat, axis=-1, keepdims=True)
        dx = dr_ref[...] + rstd * (dxhat - xhat * proj)
        dx_ref[...] = dx
        if nb:
            _gate_terms(dx, rest[0], rest[1], rest[nb + 4], rest[nb + 5])

    row = pl.BlockSpec((ts, d), lambda i: (i, 0))
    vec = pl.BlockSpec((1, d), lambda i: (0, 0))
    extra = list(below) if nb else []
    return pl.pallas_call(
        _after(body, 5 + nb, deps), name=name, grid=(s // ts,),
        in_specs=[row, row, row, vec, vec] + [row, vec][:nb] + [ANY] * len(deps),
        out_specs=[row, vec, vec, vec] + [row, vec][:nb],
        out_shape=[jax.ShapeDtypeStruct((s, d), F32)] + [jax.ShapeDtypeStruct((1, d), F32)] * 3
        + [jax.ShapeDtypeStruct((s, d), BF16), jax.ShapeDtypeStruct((1, d), F32)][:nb],
        compiler_params=_params("arbitrary"),
    )(x, dh, dx_res, g, scale, *extra, *deps)


def _final_loss(x, g, target, y, gate, name):
    s, d = x.shape
    ts = _tile(s, 256)

    def body(x_ref, g_ref, t_ref, y_ref, gate_ref, dx_ref, loss_ref, dg_ref, dy_ref, dgate_ref):
        @pl.when(pl.program_id(0) == 0)
        def _():
            loss_ref[...] = jnp.zeros_like(loss_ref)
            dg_ref[...] = jnp.zeros_like(dg_ref)
            dgate_ref[...] = jnp.zeros_like(dgate_ref)

        xv = x_ref[...]
        gv = g_ref[...]
        rstd = lax.rsqrt(jnp.mean(xv * xv, axis=-1, keepdims=True) + EPS)
        xhat = xv * rstd
        err = xhat * gv - t_ref[...]
        loss_ref[...] += 0.5 * jnp.sum(jnp.mean(err * err, axis=-1, keepdims=True))
        dy = err * (1.0 / d)
        dg_ref[...] += jnp.sum(dy * xhat, axis=0, keepdims=True)
        dxhat = dy * gv
        proj = jnp.mean(dxhat * xhat, axis=-1, keepdims=True)
        dx = rstd * (dxhat - xhat * proj)
        dx_ref[...] = dx
        _gate_terms(dx, y_ref, gate_ref, dy_ref, dgate_ref)

    row = pl.BlockSpec((ts, d), lambda i: (i, 0))
    vec = pl.BlockSpec((1, d), lambda i: (0, 0))
    return pl.pallas_call(
        body, name=name, grid=(s // ts,),
        in_specs=[row, vec, row, row, vec],
        out_specs=[row, pl.BlockSpec((1, LANES), lambda i: (0, 0)), vec, row, vec],
        out_shape=[jax.ShapeDtypeStruct((s, d), F32), jax.ShapeDtypeStruct((1, LANES), F32),
                   jax.ShapeDtypeStruct((1, d), F32), jax.ShapeDtypeStruct((s, d), BF16),
                   jax.ShapeDtypeStruct((1, d), F32)],
        compiler_params=_params("arbitrary"),
    )(x, g, target, y, gate)


def _conv(v, w_ref, width):
    out = w_ref[width - 1:width, :] * v
    for k in range(width - 1):
        out = out + w_ref[k:k + 1, :] * _shift_down(v, width - 1 - k)
    return out


def _sc_fwd(proj, conv_w, name, deps=()):
    _, s, e = proj.shape
    te = _tile(e, 256)
    width = conv_w.shape[0]

    def body(b_ref, c_ref, v_ref, g_ref, w_ref, o_ref):
        cv = c_ref[...].astype(F32) * v_ref[...].astype(F32)
        u = _conv(cv, w_ref, width)
        gv = g_ref[...].astype(F32)
        o_ref[...] = (b_ref[...].astype(F32) * u * (gv * _sigmoid(gv))).astype(o_ref.dtype)

    def part(q):
        return pl.BlockSpec((None, s, te), lambda j, q=q: (q, 0, j))

    return pl.pallas_call(
        _after(body, 5, deps), name=name, grid=(e // te,),
        in_specs=[part(0), part(1), part(2), part(3), pl.BlockSpec((width, te), lambda j: (0, j))]
        + [ANY] * len(deps),
        out_specs=pl.BlockSpec((s, te), lambda j: (0, j)),
        out_shape=jax.ShapeDtypeStruct((s, e), BF16),
        compiler_params=_params("arbitrary"),
    )(proj, proj, proj, proj, conv_w, *deps)


def _sc_bwd(proj, dyb, conv_w, name, deps=()):
    _, s, e = proj.shape
    te = _tile(e, 256)
    width = conv_w.shape[0]

    def body(b_ref, c_ref, v_ref, g_ref, dy_ref, w_ref, dp_ref, vec_ref):
        bv = b_ref[...].astype(F32)
        cvl = c_ref[...].astype(F32)
        vv = v_ref[...].astype(F32)
        gv = g_ref[...].astype(F32)
        dyv = dy_ref[...].astype(F32)
        cv = cvl * vv
        u = _conv(cv, w_ref, width)
        sg = _sigmoid(gv)
        silu = gv * sg
        dp_ref[0] = (dyv * u * silu).astype(dp_ref.dtype)
        du = dyv * bv * silu
        dp_ref[3] = (dyv * bv * u * (sg * (1.0 + gv * (1.0 - sg)))).astype(dp_ref.dtype)
        dcv = w_ref[width - 1:width, :] * du
        vec_ref[...] = jnp.zeros_like(vec_ref)
        vec_ref[width - 1:width, :] = jnp.sum(du * cv, axis=0, keepdims=True)
        for k in range(width - 1):
            sh = width - 1 - k
            dcv = dcv + w_ref[k:k + 1, :] * _shift_up(du, sh)
            vec_ref[k:k + 1, :] = jnp.sum(du * _shift_down(cv, sh), axis=0, keepdims=True)
        dp_ref[1] = (dcv * vv).astype(dp_ref.dtype)
        dp_ref[2] = (dcv * cvl).astype(dp_ref.dtype)

    def part(q):
        return pl.BlockSpec((None, s, te), lambda j, q=q: (q, 0, j))

    return pl.pallas_call(
        _after(body, 6, deps), name=name, grid=(e // te,),
        in_specs=[part(0), part(1), part(2), part(3), pl.BlockSpec((s, te), lambda j: (0, j)),
                  pl.BlockSpec((width, te), lambda j: (0, j))] + [ANY] * len(deps),
        out_specs=[pl.BlockSpec((4, s, te), lambda j: (0, 0, j)),
                   pl.BlockSpec((8, te), lambda j: (0, j))],
        out_shape=[jax.ShapeDtypeStruct((4, s, e), BF16), jax.ShapeDtypeStruct((8, e), F32)],
        compiler_params=_params("arbitrary"),
    )(proj, proj, proj, proj, dyb, conv_w, *deps)


def _lru_gates(v_pre, w_ref, cb_ref, wa_ref, ba_ref, wx_ref, bx_ref, lam_ref, width):
    v = _conv(v_pre, w_ref, width) + cb_ref[...]
    vb = v.astype(BF16)
    r = _sigmoid(jnp.dot(vb, wa_ref[...], preferred_element_type=F32) + ba_ref[...])
    i = _sigmoid(jnp.dot(vb, wx_ref[...], preferred_element_type=F32) + bx_ref[...])
    nl = -lam_ref[...]
    sp = jnp.maximum(nl, 0.0) + jnp.log1p(jnp.exp(-jnp.abs(nl)))
    log_a = (-RGLRU_C) * r * sp
    a = jnp.exp(log_a)
    one_minus_a2 = jnp.tanh(-log_a) * (1.0 + a * a)
    mult = jnp.sqrt(one_minus_a2)
    return v, vb, r, i, sp, a, mult


def _lru_specs(s, dh, heads, width):
    head_col = lambda q: pl.BlockSpec((None, s, dh), lambda h, q=q: (q, 0, h))
    vec = pl.BlockSpec((1, dh), lambda h: (0, h))
    mat = pl.BlockSpec((None, dh, dh), lambda h: (h, 0, 0))
    weights = [pl.BlockSpec((width, dh), lambda h: (0, h)), vec, mat, vec, mat, vec, vec]
    return head_col, weights


def _lru_fwd(proj, conv_w, conv_b, w_a, b_a, w_x, b_x, lam, name, deps=()):
    _, s, e = proj.shape
    heads, dh, _ = w_a.shape
    width = conv_w.shape[0]

    def body(v_ref, g_ref, w_ref, cb_ref, wa_ref, ba_ref, wx_ref, bx_ref, lam_ref, yb_ref, hs_ref):
        v, _, _, i, _, a, mult = _lru_gates(v_ref[...].astype(F32), w_ref, cb_ref, wa_ref, ba_ref,
                                           wx_ref, bx_ref, lam_ref, width)
        hs = _scan(a, mult * i * v, _shift_down)
        hs_ref[...] = hs
        gv = g_ref[...].astype(F32)
        yb_ref[...] = (hs * (gv * _sigmoid(gv))).astype(yb_ref.dtype)

    head_col, weights = _lru_specs(s, dh, heads, width)
    out = pl.BlockSpec((s, dh), lambda h: (0, h))
    return pl.pallas_call(
        _after(body, 9, deps), name=name, grid=(heads,),
        in_specs=[head_col(0), head_col(1)] + weights + [ANY] * len(deps),
        out_specs=[out, out],
        out_shape=[jax.ShapeDtypeStruct((s, e), BF16), jax.ShapeDtypeStruct((s, e), F32)],
        compiler_params=_params("arbitrary"),
    )(proj, proj, conv_w, conv_b, w_a, b_a, w_x, b_x, lam, *deps)


def _lru_bwd(proj, hs, dyb, conv_w, conv_b, w_a, b_a, w_x, b_x, lam, name, deps=()):
    _, s, e = proj.shape
    heads, dh, _ = w_a.shape
    width = conv_w.shape[0]

    def body(v_ref, g_ref, hs_ref, dy_ref, w_ref, cb_ref, wa_ref, ba_ref, wx_ref, bx_ref, lam_ref,
             dp_ref, dwa_ref, dwx_ref, vec_ref):
        v_pre = v_ref[...].astype(F32)
        v, vb, r, i, sp, a, mult = _lru_gates(v_pre, w_ref, cb_ref, wa_ref, ba_ref, wx_ref, bx_ref,
                                              lam_ref, width)
        hs = hs_ref[...]
        gv = g_ref[...].astype(F32)
        dyv = dy_ref[...].astype(F32)
        sg = _sigmoid(gv)
        dp_ref[1] = (dyv * hs * (sg * (1.0 + gv * (1.0 - sg)))).astype(dp_ref.dtype)
        dhs = dyv * (gv * sg)
        d_h = _scan(_shift_up(a, 1), dhs, _shift_up)
        da = d_h * _shift_down(hs, 1)
        iv = i * v
        dlog_a = da * a - (d_h * iv) * (a * a) / mult
        di = d_h * mult * v
        dv = d_h * mult * i
        dzr = dlog_a * (-RGLRU_C) * sp * r * (1.0 - r)
        dzi = di * i * (1.0 - i)
        dsp = jnp.sum(dlog_a * r, axis=0, keepdims=True) * (-RGLRU_C)
        vec_ref[...] = jnp.zeros_like(vec_ref)
        vec_ref[0:1, :] = jnp.sum(dzr, axis=0, keepdims=True)
        vec_ref[1:2, :] = jnp.sum(dzi, axis=0, keepdims=True)
        vec_ref[2:3, :] = -dsp * _sigmoid(-lam_ref[...])
        dzr_b = dzr.astype(BF16)
        dzi_b = dzi.astype(BF16)
        vt = vb.astype(F32).T.astype(BF16)
        dwa_ref[...] = jnp.dot(vt, dzr_b, preferred_element_type=F32).astype(dwa_ref.dtype)
        dwx_ref[...] = jnp.dot(vt, dzi_b, preferred_element_type=F32).astype(dwx_ref.dtype)
        nt = (((1,), (1,)), ((), ()))
        dv = dv + lax.dot_general(dzr_b, wa_ref[...], nt, preferred_element_type=F32)
        dv = dv + lax.dot_general(dzi_b, wx_ref[...], nt, preferred_element_type=F32)
        vec_ref[3:4, :] = jnp.sum(dv, axis=0, keepdims=True)
        dvp = w_ref[width - 1:width, :] * dv
        vec_ref[4 + width - 1:4 + width, :] = jnp.sum(dv * v_pre, axis=0, keepdims=True)
        for k in range(width - 1):
            sh = width - 1 - k
            dvp = dvp + w_ref[k:k + 1, :] * _shift_up(dv, sh)
            vec_ref[4 + k:5 + k, :] = jnp.sum(dv * _shift_down(v_pre, sh), axis=0, keepdims=True)
        dp_ref[0] = dvp.astype(dp_ref.dtype)

    head_col, weights = _lru_specs(s, dh, heads, width)
    col = pl.BlockSpec((s, dh), lambda h: (0, h))
    mat = pl.BlockSpec((None, dh, dh), lambda h: (h, 0, 0))
    return pl.pallas_call(
        _after(body, 11, deps), name=name, grid=(heads,),
        in_specs=[head_col(0), head_col(1), col, col] + weights + [ANY] * len(deps),
        out_specs=[pl.BlockSpec((2, s, dh), lambda h: (0, 0, h)), mat, mat,
                   pl.BlockSpec((16, dh), lambda h: (0, h))],
        out_shape=[jax.ShapeDtypeStruct((2, s, e), BF16),
                   jax.ShapeDtypeStruct((heads, dh, dh), BF16),
                   jax.ShapeDtypeStruct((heads, dh, dh), BF16),
                   jax.ShapeDtypeStruct((16, e), F32)],
        compiler_params=_params("arbitrary"),
    )(proj, proj, hs, dyb, conv_w, conv_b, w_a, b_a, w_x, b_x, lam, *deps)


def _ada_mod(c_all, w, b, name):
    layers, d, f = w.shape
    nb = c_all.shape[0]

    def body(c_ref, w_ref, b_ref, o_ref):
        cv = c_ref[...]
        sc = cv * _sigmoid(cv)
        o_ref[...] = jnp.dot(sc, w_ref[...], preferred_element_type=F32,
                             precision=lax.Precision.HIGHEST) + b_ref[...]

    return pl.pallas_call(
        body, name=name, grid=(layers,),
        in_specs=[pl.BlockSpec((nb, d), lambda l: (0, 0)),
                  pl.BlockSpec((None, d, f), lambda l: (l, 0, 0)),
                  pl.BlockSpec((None, 1, f), lambda l: (l, 0, 0))],
        out_specs=pl.BlockSpec((None, nb, f), lambda l: (l, 0, 0)),
        out_shape=jax.ShapeDtypeStruct((layers, nb, f), F32),
        compiler_params=_params("arbitrary"),
    )(c_all, w, b)


def _ada_grad(c_all_t, dmod, name):
    d, nb = c_all_t.shape
    layers, _, f = dmod.shape

    def body(c_ref, dm_ref, o_ref):
        cv = c_ref[...]
        sc = cv * _sigmoid(cv)
        acc = sc[:, 0:1] * dm_ref[0:1, :]
        for k in range(1, nb):
            acc = acc + sc[:, k:k + 1] * dm_ref[k:k + 1, :]
        o_ref[...] = acc

    return pl.pallas_call(
        body, name=name, grid=(layers,),
        in_specs=[pl.BlockSpec((d, nb), lambda l: (0, 0)),
                  pl.BlockSpec((None, nb, f), lambda l: (l, 0, 0))],
        out_specs=pl.BlockSpec((None, d, f), lambda l: (l, 0, 0)),
        out_shape=jax.ShapeDtypeStruct((layers, d, f), F32),
        compiler_params=_params("arbitrary"),
    )(c_all_t, dmod)


def _device_sum(g, name):
    _, rows, _ = g.shape

    def body(g_ref, o_ref):
        acc = g_ref[0]
        for k in range(1, N_DEV):
            acc = acc + g_ref[k]
        o_ref[...] = acc

    return pl.pallas_call(
        body, name=name,
        in_specs=[VMEM_SPEC], out_specs=VMEM_SPEC,
        out_shape=jax.ShapeDtypeStruct((rows, LANES), F32),
        compiler_params=pltpu.CompilerParams(vmem_limit_bytes=VMEM_LIMIT),
    )(g)


def _adamw_math(w, g, m, v):
    m = ADAM_B1 * m + (1.0 - ADAM_B1) * g
    v = ADAM_B2 * v + (1.0 - ADAM_B2) * (g * g)
    m_hat = m / (1.0 - ADAM_B1 ** ADAM_STEP)
    v_hat = v / (1.0 - ADAM_B2 ** ADAM_STEP)
    delta = -ADAM_LR * (m_hat / (jnp.sqrt(v_hat) + ADAM_EPS) + ADAM_WD * w)
    return delta, m, v


def _adamw(w, g, m, v, name):
    rows, cols = w.shape
    tr = _tile(rows, 256)

    def body(w_ref, g_ref, m_ref, v_ref, d_ref, mo_ref, vo_ref):
        d_ref[...], mo_ref[...], vo_ref[...] = _adamw_math(w_ref[...], g_ref[...], m_ref[...], v_ref[...])

    blk = pl.BlockSpec((tr, cols), lambda i: (i, 0))
    return pl.pallas_call(
        body, name=name, grid=(rows // tr,),
        in_specs=[blk] * 4, out_specs=[blk] * 3,
        out_shape=[jax.ShapeDtypeStruct((rows, cols), F32)] * 3,
        compiler_params=_params("arbitrary"),
    )(w, g, m, v)


def _adamw_reduced(idx, w, m, v, part, got, recvs, name):
    rows, cols = w.shape
    tr = _tile(rows, 256)
    nr = len(recvs)

    def body(idx_ref, w_ref, m_ref, v_ref, p_ref, q_ref, *rest):
        g_ref, d_ref, mo_ref, vo_ref = rest[nr:]
        g = p_ref[...].astype(F32) + q_ref[...].astype(F32)
        for u_ref in rest[:nr]:
            for j in range(u_ref.shape[0]):
                g = g + u_ref[j].astype(F32)
        g_ref[...] = g
        d_ref[...], mo_ref[...], vo_ref[...] = _adamw_math(w_ref[...], g, m_ref[...], v_ref[...])

    blk = pl.BlockSpec((tr, cols), lambda i, idx: (i, 0))
    grid_spec = pltpu.PrefetchScalarGridSpec(
        num_scalar_prefetch=1, grid=(rows // tr,),
        in_specs=[blk, blk, blk,
                  pl.BlockSpec((None, None, tr, cols), lambda i, idx: (idx[3], idx[4], i, 0)),
                  pl.BlockSpec((None, None, tr, cols), lambda i, idx: (idx[3], 0, i, 0))]
        + [pl.BlockSpec((u.shape[0], tr, cols), lambda i, idx: (0, i, 0)) for u in recvs],
        out_specs=[blk] * 4)
    return pl.pallas_call(
        body, name=name, grid_spec=grid_spec,
        out_shape=[jax.ShapeDtypeStruct((rows, cols), F32)] * 4,
        compiler_params=_params("arbitrary"),
    )(idx, w, m, v, part, got, *recvs)


def _pack(vectors):
    flat = jnp.concatenate([v.reshape(-1).astype(F32) for v in vectors])
    pad = (-flat.shape[0]) % (8 * LANES)
    return jnp.pad(flat, (0, pad)).reshape(-1, LANES)


def _unpack(flat, shapes):
    out, off = [], 0
    for shp in shapes:
        size = math.prod(shp)
        out.append(flat[..., off:off + size].reshape(flat.shape[:-1] + tuple(shp)))
        off += size
    return out


def _my_slice(full, me, axis):
    size = full.shape[axis] // N_DEV
    return lax.dynamic_slice_in_dim(full, me * size, size, axis)


def kernel(x, c, norm_g, ada_w, ada_b, sc_w_in, sc_conv_w, sc_w_out, lru_w_in, lru_conv_w, lru_conv_b, lru_w_a, lru_b_a, lru_w_x, lru_b_x, lru_lambda, lru_w_out, final_g, loss_target, m_norm_g, m_ada_w, m_ada_b, m_sc_w_in, m_sc_conv_w, m_sc_w_out, m_lru_w_in, m_lru_conv_w, m_lru_conv_b, m_lru_w_a, m_lru_b_a, m_lru_w_x, m_lru_b_x, m_lru_lambda, m_lru_w_out, m_final_g, v_norm_g, v_ada_w, v_ada_b, v_sc_w_in, v_sc_conv_w, v_sc_w_out, v_lru_w_in, v_lru_conv_w, v_lru_conv_b, v_lru_w_a, v_lru_b_a, v_lru_w_x, v_lru_b_x, v_lru_lambda, v_lru_w_out, v_final_g):
    _, s, d = x.shape
    e = sc_w_out.shape[1] * N_DEV
    heads, dh_s, dh = lru_w_a.shape[1:]
    es = e // N_DEV
    f = ada_w.shape[2]
    mx, my, mc = _position()
    me = 4 * mx + 2 * my + mc
    chip = 2 * mx + my
    idx = jnp.stack([chip ^ 1, chip ^ 2, chip ^ 3, chip, mc]).astype(jnp.int32)

    x0 = x[0]
    target = loss_target[0]

    small_shapes = [(d,), (3, es), (4, es), (es,), (heads, dh_s), (heads, dh_s), (es,)]
    small = _small_gather(_pack([c, sc_conv_w, lru_conv_w, lru_conv_b, lru_b_a, lru_b_x, lru_lambda]),
                          "gather_small_weights").reshape(N_DEV, -1)
    c_all, cw3, cw4, cb, ba, bx, lam = _unpack(small, small_shapes)
    cw3 = cw3.transpose(1, 0, 2).reshape(3, e)
    cw4 = cw4.transpose(1, 0, 2).reshape(4, e)
    cb = cb.reshape(1, e)
    lam = lam.reshape(1, e)
    ba = ba.transpose(1, 0, 2).reshape(1, e)
    bx = bx.transpose(1, 0, 2).reshape(1, e)

    ada_b_mine = _my_slice(ada_b, me, 1).reshape(2, 1, f)
    mod_mine = _ada_mod(c_all, ada_w, ada_b_mine, "ada_mod")
    mod_all = _small_gather(_pack([mod_mine]), "gather_mod")

    shards = [sc_w_in[0].astype(BF16), sc_w_out[0].astype(BF16), lru_w_in[0].astype(BF16),
              lru_w_a[0].reshape(heads * dh_s, dh).astype(BF16),
              lru_w_x[0].reshape(heads * dh_s, dh).astype(BF16), lru_w_out[0].astype(BF16)]
    lands = [lax.dynamic_update_slice(lax.empty((N_DEV,) + sh.shape, BF16), sh[None], (me, 0, 0))
             for sh in shards]
    every = [1, 2, 3, 0]
    units = [([0], [0]), ([0], [1]), ([0], [2]), ([0], [3]), ([1], every), ([2, 3, 4], every), ([5], every)]
    sems, first_sh, first_ld, started = _gather_start(shards[:1], lands[:1], units[:3], [mod_all],
                                                      "gather_start_first")
    shards, lands = first_sh + shards[1:], first_ld + lands[1:]

    def start_later(after):
        far_sems, far_sh, far_ld, tok = _gather_start(shards[:1], lands[:1], units[3:4], after, "gather_start_far")
        rest_units = [([i - 1 for i in members], ks) for members, ks in units[4:]]
        rest_sems, rest_sh, rest_ld, tok = _gather_start(shards[1:], lands[1:], rest_units, [tok],
                                                         "gather_start_rest")
        sems.extend(far_sems + rest_sems)
        shards[:], lands[:] = far_sh + rest_sh, far_ld + rest_ld
        return tok

    def gathered(u, after_forward, name):
        members, ks = units[u]
        fwd, shs, lnd, token = _gather_forward(
            [shards[i] for i in members], [lands[i] for i in members], ks, sems[u][0], sems[u][1],
            after_forward, "gather_forward_" + name)
        for i, sh, ld in zip(members, shs, lnd):
            shards[i], lands[i] = sh, ld

        def finish(after):
            out = _gather_finish([lands[i] for i in members], ks, fwd, after, "gather_finish_" + name)
            for i, ld in zip(members, out):
                lands[i] = ld
            return out

        return token, finish

    mod_all = mod_all.reshape(N_DEV, -1)
    mod_all = mod_all[:, :2 * N_DEV * f].reshape(N_DEV, 2, N_DEV, f)
    mod_all = mod_all.transpose(1, 2, 0, 3).reshape(2, N_DEV, 3 * d)
    mod = lax.dynamic_index_in_dim(mod_all, me, 1, keepdims=False)
    shift = [mod[l:l + 1, 0:d] for l in range(2)]
    scale = [mod[l:l + 1, d:2 * d] for l in range(2)]
    gate = [mod[l:l + 1, 2 * d:3 * d] for l in range(2)]
    ng = [norm_g[l:l + 1] for l in range(2)]
    fg = final_g.reshape(1, d)

    h0 = _norm_mod(x0, ng[0], scale[0], shift[0], "norm_mod_0", deps=[started])
    proj0 = lax.empty((4, s, e), BF16)
    tok, _ = gathered(0, [h0], "sc_w_in_own")
    proj0 = _mm_proj_group(h0, lands[0], idx, 3, proj0, "mm_proj_0_own", deps=[tok])
    tok, finish_y = gathered(1, [proj0], "sc_w_in_near_y")
    tok, finish_x = gathered(2, [tok], "sc_w_in_near_x")
    tok = start_later([tok])
    for u, name, finish in ((1, "near_y", finish_y), (2, "near_x", finish_x), (3, "far", None)):
        if finish is None:
            tok, finish = gathered(u, [proj0], "sc_w_in_" + name)
        wg_in0, = finish([tok, proj0])
        proj0 = _mm_proj_group(h0, wg_in0, idx, u - 1, proj0, "mm_proj_0_" + name)
    tok, finish = gathered(4, [proj0], "sc_w_out")
    yb0 = _sc_fwd(proj0, cw3, "sc_fwd", deps=[tok])
    w_out0 = finish([yb0])[0].reshape(e, d)
    x1, y0 = _mm_out(yb0, w_out0, x0, gate[0], "mm_out_0")
    tok, finish = gathered(5, [x1], "lru_in")
    h1 = _norm_mod(x1, ng[1], scale[1], shift[1], "norm_mod_1", deps=[tok])
    wg_in1, wg_a, wg_x = finish([h1])
    w_a = wg_a.reshape(N_DEV, heads, dh_s, dh).transpose(1, 0, 2, 3).reshape(heads, dh, dh)
    w_x = wg_x.reshape(N_DEV, heads, dh_s, dh).transpose(1, 0, 2, 3).reshape(heads, dh, dh)
    proj1 = _mm_proj(h1, wg_in1, 2, "mm_proj_1")
    tok, finish = gathered(6, [proj1], "lru_w_out")
    yb1, hs = _lru_fwd(proj1, cw4, cb, w_a, ba, w_x, bx, lam, "lru_fwd", deps=[tok])
    w_out1 = finish([yb1])[0].reshape(e, d)
    x2, y1 = _mm_out(yb1, w_out1, x1, gate[1], "mm_out_1")
    dx2, loss_part, d_fg, dy1, dgate1 = _final_loss(x2, fg, target, y1, gate[1], "final_loss")

    def pieces(g, rows, cols):
        return g.reshape(4, 2, rows, cols)

    def by_rows(g):
        return g.reshape(heads, N_DEV, dh_s, dh).transpose(1, 0, 2, 3).reshape(N_DEV, heads * dh_s, dh)

    def pair_begin(parts, group):
        send, recv, parts, lnd, token = _pair_start(parts, "pair_start_" + group)
        return dict(parts=parts, lands=lnd, send=send, recv=recv, group=group), token

    def scatter_start(pair, names, after):
        group = pair["group"]
        parts, gots = _pair_wait(pair["parts"], pair["lands"], pair["send"], pair["recv"], after,
                                 "pair_wait_" + group)
        sums = [_pair_sum(idx, p, q, "pair_sum_" + nm) for p, q, nm in zip(parts, gots, names)]
        empties = [lax.empty(sm.shape, sm.dtype) for sm in sums]
        send, recv, sums, lnd, token = _chip_start(sums, empties, "chip_start_" + group)
        return dict(parts=parts, gots=gots, names=names, group=group, sums=sums, lands=lnd,
                    send=send, recv=recv), token

    big = {"sc_w_in": (sc_w_in, m_sc_w_in, v_sc_w_in), "sc_w_out": (sc_w_out, m_sc_w_out, v_sc_w_out),
           "lru_w_in": (lru_w_in, m_lru_w_in, v_lru_w_in), "lru_w_a": (lru_w_a, m_lru_w_a, v_lru_w_a),
           "lru_w_x": (lru_w_x, m_lru_w_x, v_lru_w_x), "lru_w_out": (lru_w_out, m_lru_w_out, v_lru_w_out)}
    big_res = {}

    def scatter_finish(rs, after):
        recvs = _chip_wait(rs["sums"], rs["lands"], rs["send"], rs["recv"], after, "chip_wait_" + rs["group"])
        done = []
        for p, q, u, nm in zip(rs["parts"], rs["gots"], recvs, rs["names"]):
            w, m, v = big[nm]
            shp2 = p.shape[2:]
            res = _adamw_reduced(idx, w.reshape(shp2), m.reshape(shp2), v.reshape(shp2), p, q, [u], "adamw_" + nm)
            big_res[nm] = [r.reshape(w.shape) for r in res]
            done.append(res[1])
        return done

    dw_out1 = _mm_tn(yb1, dy1[None], 1, "mm_dw_out_1")
    pair, tok = pair_begin([pieces(dw_out1, es, d)], "lru_w_out")
    dyb1 = _mm_nt(dy1[None], w_out1[None], BF16, "mm_dyb_1", deps=[tok])
    rs1, tok = scatter_start(pair, ["lru_w_out"], [dyb1])
    dproj1, dw_a, dw_x, vecs1 = _lru_bwd(proj1, hs, dyb1, cw4, cb, w_a, ba, w_x, bx, lam, "lru_bwd", deps=[tok])
    done = scatter_finish(rs1, [dproj1])
    dw_in1 = _mm_tn(h1, dproj1, N_DEV, "mm_dw_in_1", deps=done)
    pair, tok = pair_begin([pieces(dw_in1, d, 2 * es), pieces(by_rows(dw_a), heads * dh_s, dh),
                            pieces(by_rows(dw_x), heads * dh_s, dh)], "lru_in")
    dh1 = _mm_nt(dproj1, wg_in1, F32, "mm_dh_1", deps=[tok])
    rs2, tok = scatter_start(pair, ["lru_w_in", "lru_w_a", "lru_w_x"], [dh1])
    dx1, dscale1, dshift1, dng1, dy0, dgate0 = _norm_mod_bwd(x1, dh1, dx2, ng[1], scale[1], "norm_mod_bwd_1",
                                                             below=(y0, gate[0]), deps=[tok])
    dw_out0 = _mm_tn(yb0, dy0[None], 1, "mm_dw_out_0")
    pair, tok = pair_begin([pieces(dw_out0, es, d)], "sc_w_out")
    dyb0 = _mm_nt(dy0[None], w_out0[None], BF16, "mm_dyb_0", deps=[tok])
    rs3, tok = scatter_start(pair, ["sc_w_out"], [dyb0])
    dproj0, vecs0 = _sc_bwd(proj0, dyb0, cw3, "sc_bwd", deps=[tok])
    idx_one = jnp.stack([jnp.zeros_like(mc)] * 4 + [mc]).astype(jnp.int32)
    sc_w_in_steps = []

    def chip_step(j, pair, after):
        (part,), (got,) = _pair_wait(pair["parts"], pair["lands"], pair["send"], pair["recv"], after,
                                     "pair_wait_sc_w_in_%d" % j)
        sm = _pair_sum(idx_one, part, got, "pair_sum_sc_w_in_%d" % j, nslots=1)
        send, recv, sums, lnd, token = _chip_start([sm], [lax.empty(sm.shape, sm.dtype)],
                                                   "chip_start_sc_w_in_%d" % j, flips=(j,))
        sc_w_in_steps.append((sums, lnd, send, recv, j))
        return token

    pending, done = None, []
    for j in (3, 1, 2, 0):
        part = _mm_tn_group(h0, dproj0, idx, (j - 1) % 4, 2, "mm_dw_in_0_%d" % j, deps=done)[None]
        pair, tok = pair_begin([part], "sc_w_in_%d" % j)
        if j == 3:
            done = scatter_finish(rs2, [chip_step(j, pair, [tok])])
            continue
        done = [tok]
        if pending is not None:
            done.append(chip_step(pending[0], pending[1], [tok]))
        pending = (j, pair)
    dh0 = _mm_nt(dproj0, wg_in0, F32, "mm_dh_0", deps=done)
    pair = pending[1]
    (part,), (got,) = _pair_wait(pair["parts"], pair["lands"], pair["send"], pair["recv"], [dh0],
                                 "pair_wait_sc_w_in_0")
    dx0, dscale0, dshift0, dng0 = _norm_mod_bwd(x0, dh0, dx1, ng[0], scale[0], "norm_mod_bwd_0")
    done = scatter_finish(rs3, [dx0])
    dmod_mine = jnp.concatenate([dshift0, dscale0, dgate0, dshift1, dscale1, dgate1], axis=1)
    end_shapes = [(LANES,), (2, 3 * d), (2, d), (d,), (8, e), (16, e)]
    end_all = _small_gather(
        _pack([loss_part, dmod_mine, jnp.concatenate([dng0, dng1], axis=0), d_fg, vecs0, vecs1]),
        "gather_small_grads", deps=done)
    end_sum = _device_sum(end_all, "sum_small_grads").reshape(-1)
    loss_v, g_ada_b, g_norm_g, g_final_g, sum0, sum1 = _unpack(end_sum, end_shapes)
    loss = loss_v[0]
    dmod_all = _unpack(end_all.reshape(N_DEV, -1), end_shapes)[1].transpose(1, 0, 2)
    dmod_cols = _my_slice(dmod_all, me, 2)
    g_ada_w = _ada_grad(c_all.T, dmod_cols, "ada_grad")

    g_sc_conv_w = _my_slice(sum0[0:3], me, 1)
    g_lru_b_a = _my_slice(sum1[0].reshape(heads, dh), me, 1)
    g_lru_b_x = _my_slice(sum1[1].reshape(heads, dh), me, 1)
    g_lru_lambda = _my_slice(sum1[2:3], me, 1)
    g_lru_conv_b = _my_slice(sum1[3:4], me, 1)
    g_lru_conv_w = _my_slice(sum1[4:8], me, 1)

    ada_res = _adamw(ada_w.reshape(2 * d, f), g_ada_w.reshape(2 * d, f), m_ada_w.reshape(2 * d, f),
                     v_ada_w.reshape(2 * d, f), "adamw_ada_w")
    ada_out = [g_ada_w] + [r.reshape(ada_w.shape) for r in ada_res]

    small_w = [norm_g, ada_b, final_g, sc_conv_w, lru_conv_w, lru_conv_b, lru_b_a, lru_b_x, lru_lambda]
    small_m = [m_norm_g, m_ada_b, m_final_g, m_sc_conv_w, m_lru_conv_w, m_lru_conv_b, m_lru_b_a, m_lru_b_x,
               m_lru_lambda]
    small_v = [v_norm_g, v_ada_b, v_final_g, v_sc_conv_w, v_lru_conv_w, v_lru_conv_b, v_lru_b_a, v_lru_b_x,
               v_lru_lambda]
    small_g = [g_norm_g, g_ada_b, g_final_g, g_sc_conv_w, g_lru_conv_w, g_lru_conv_b, g_lru_b_a, g_lru_b_x,
               g_lru_lambda]
    small_g = [g.reshape(w.shape) for g, w in zip(small_g, small_w)]
    shapes = [w.shape for w in small_w]
    packed = _adamw(_pack(small_w), _pack(small_g), _pack(small_m), _pack(small_v), "adamw_small")
    small_out = [small_g] + [_unpack(p.reshape(-1), shapes) for p in packed]

    after = [packed[0], ada_res[0]]
    recvs = []
    for sums, lnd, send, recv, j in sc_w_in_steps:
        recvs += _chip_wait(sums, lnd, send, recv, after, "chip_wait_sc_w_in_%d" % j)
    shp2 = part.shape[2:]
    res = _adamw_reduced(idx_one, sc_w_in.reshape(shp2), m_sc_w_in.reshape(shp2), v_sc_w_in.reshape(shp2),
                         part, got, recvs, "adamw_sc_w_in")
    big_res["sc_w_in"] = [r.reshape(sc_w_in.shape) for r in res]
    big_out = [big_res[nm] for nm in ("sc_w_in", "sc_w_out", "lru_w_in", "lru_w_a", "lru_w_x", "lru_w_out")]

    def small(kind, i):
        return small_out[kind][i]

    def bigw(kind, i):
        return big_out[i][kind]

    outs = [loss, dx0[None]]
    for kind in range(4):
        outs += [small(kind, 0), ada_out[kind], small(kind, 1), bigw(kind, 0), small(kind, 3), bigw(kind, 1),
                 bigw(kind, 2), small(kind, 4), small(kind, 5), bigw(kind, 3), small(kind, 6), bigw(kind, 4),
                 small(kind, 7), small(kind, 8), bigw(kind, 5), small(kind, 2)]
    return tuple(outs)
```

```python
import math

import jax
import jax.numpy as jnp
from jax import lax
from jax.experimental import pallas as pl
from jax.experimental.pallas import tpu as pltpu

N_DEV = 8
LANES = 128
EPS = 1e-6
RGLRU_C = 8.0
ADAM_LR = 0.001
ADAM_B1 = 0.9
ADAM_B2 = 0.999
ADAM_EPS = 1e-08
ADAM_WD = 0.01
ADAM_STEP = 10
VMEM_LIMIT = 56 * 1024 * 1024
MESH = pl.DeviceIdType.MESH
F32 = jnp.float32
BF16 = jnp.bfloat16
ANY = pl.BlockSpec(memory_space=pl.ANY)
HBM = pl.BlockSpec(memory_space=pltpu.HBM)
SEM = pl.BlockSpec(memory_space=pltpu.SEMAPHORE)
VMEM_SPEC = pl.BlockSpec(memory_space=pltpu.VMEM)
EFFECT = pltpu.SideEffectType.DATAFLOW_SIDE_EFFECTING
TOKEN = jax.ShapeDtypeStruct((8, LANES), jnp.float32)


def _tile(n, pref):
    t = min(n, pref)
    assert n % t == 0, (n, pref)
    return t


def _params(*sem):
    return pltpu.CompilerParams(dimension_semantics=sem, vmem_limit_bytes=VMEM_LIMIT)


def _position():
    return lax.axis_index("x"), lax.axis_index("y"), lax.axis_index("c")


def _flip(x, y, k):
    return (1 - x if k & 2 else x), (1 - y if k & 1 else y)


def _after(body, n_in, deps):
    if not deps:
        return body

    def wrapped(*refs):
        return body(*refs[:n_in], *refs[n_in + len(deps):])

    return wrapped


def _small_gather(v, name, deps=()):
    rows = v.shape[0]

    def body(v_ref, out_ref, send_sems, recv_sems):
        x, y, c = _position()
        me = 4 * x + 2 * y + c
        out_ref[me] = v_ref[...]
        copies = []
        for k in range(1, N_DEV):
            px, py = _flip(x, y, k >> 1)
            pc = 1 - c if k & 1 else c
            cp = pltpu.make_async_remote_copy(
                src_ref=v_ref, dst_ref=out_ref.at[me],
                send_sem=send_sems.at[k - 1], recv_sem=recv_sems.at[k - 1],
                device_id=(px, py, pc), device_id_type=MESH)
            cp.start()
            copies.append((cp, 4 * px + 2 * py + pc))
        for k, (cp, peer) in enumerate(copies):
            pltpu.make_async_remote_copy(
                src_ref=v_ref, dst_ref=out_ref.at[peer],
                send_sem=send_sems.at[k], recv_sem=recv_sems.at[k],
                device_id=(x, y, c), device_id_type=MESH).wait_recv()
        for cp, _ in copies:
            cp.wait_send()

    return pl.pallas_call(
        _after(body, 1, deps), name=name,
        out_shape=jax.ShapeDtypeStruct((N_DEV, rows, LANES), F32),
        in_specs=[VMEM_SPEC] + [ANY] * len(deps), out_specs=VMEM_SPEC,
        scratch_shapes=[pltpu.SemaphoreType.DMA((N_DEV - 1,)),
                        pltpu.SemaphoreType.DMA((N_DEV - 1,))],
        compiler_params=pltpu.CompilerParams(vmem_limit_bytes=VMEM_LIMIT),
    )(v, *deps)


def _hbm(a):
    return pltpu.with_memory_space_constraint(a, pltpu.HBM)


def _hbm_like(arrays):
    return [pltpu.HBM(a.shape, a.dtype) for a in arrays]


def _remote(src, dst, send, recv, to):
    return pltpu.make_async_remote_copy(src_ref=src, dst_ref=dst, send_sem=send, recv_sem=recv,
                                        device_id=to, device_id_type=MESH)


def _gather_start(shards, lands, units, after, name):
    n, nu = len(shards), len(units)

    def body(*refs):
        ins, lnd = refs[:n], refs[n:2 * n]
        sems = refs[2 * n + len(after):2 * n + len(after) + 2 * nu]
        token = refs[-1]
        x, y, c = _position()
        me = 4 * x + 2 * y + c
        targets = [(x, y, 1 - c)] + [(px, py, c) for px, py in (_flip(x, y, k) for k in (1, 2, 3))]
        for u, (members, ks) in enumerate(units):
            for slot, i in enumerate(members):
                for ki, k in enumerate(ks):
                    at = len(ks) * slot + ki
                    _remote(ins[i], lnd[i].at[me], sems[2 * u].at[at], sems[2 * u + 1].at[at], targets[k]).start()
        token[...] = jnp.zeros_like(token)

    sem_shapes = []
    for members, ks in units:
        count = len(members) * len(ks)
        sem_shapes += [pltpu.SemaphoreType.DMA((count,)), pltpu.SemaphoreType.DMA((count,))]
    out = pl.pallas_call(
        body, name=name,
        out_shape=sem_shapes + _hbm_like(shards) + _hbm_like(lands) + [TOKEN],
        in_specs=[HBM] * (2 * n) + [ANY] * len(after),
        out_specs=[SEM] * (2 * nu) + [HBM] * (2 * n) + [VMEM_SPEC],
        input_output_aliases={i: 2 * nu + i for i in range(2 * n)},
        compiler_params=pltpu.CompilerParams(has_side_effects=EFFECT),
    )(*[_hbm(s) for s in shards], *[_hbm(l) for l in lands], *after)
    sems = [(out[2 * u], out[2 * u + 1]) for u in range(nu)]
    return sems, list(out[2 * nu:2 * nu + n]), list(out[2 * nu + n:2 * nu + 2 * n]), out[-1]


def _gather_forward(shards, lands, ks, send, recv, after, name):
    m = len(shards)
    hops = [k for k in ks if k]
    nsem = 2 if hops else 0

    def body(*refs):
        ins, lnd = refs[:m], refs[m:2 * m]
        send_ref, recv_ref = refs[2 * m], refs[2 * m + 1]
        outs = refs[2 * m + 2 + len(after):]
        token = refs[-1]
        x, y, c = _position()
        me = (x, y, c)
        for slot in range(m):
            for ki, k in enumerate(ks):
                at = len(ks) * slot + ki
                if k:
                    px, py = _flip(x, y, k)
                    block = lnd[slot].at[4 * px + 2 * py + c]
                else:
                    block = lnd[slot].at[4 * x + 2 * y + (1 - c)]
                arrival = _remote(ins[slot], block, send_ref.at[at], recv_ref.at[at], me)
                arrival.wait_recv()
                if k:
                    fat = len(hops) * slot + hops.index(k)
                    _remote(block, block, outs[0].at[fat], outs[1].at[fat], (x, y, 1 - c)).start()
                arrival.wait_send()
        token[...] = jnp.zeros_like(token)

    count = len(hops) * m
    sem_shapes = [pltpu.SemaphoreType.DMA((count,)), pltpu.SemaphoreType.DMA((count,))] if hops else []
    out = pl.pallas_call(
        body, name=name,
        out_shape=sem_shapes + _hbm_like(shards) + _hbm_like(lands) + [TOKEN],
        in_specs=[HBM] * (2 * m) + [SEM, SEM] + [ANY] * len(after),
        out_specs=[SEM] * nsem + [HBM] * (2 * m) + [VMEM_SPEC],
        input_output_aliases={i: nsem + i for i in range(2 * m)},
        compiler_params=pltpu.CompilerParams(has_side_effects=EFFECT),
    )(*shards, *lands, send, recv, *after)
    fwd = (out[0], out[1]) if hops else None
    return fwd, list(out[nsem:nsem + m]), list(out[nsem + m:nsem + 2 * m]), out[-1]


def _gather_finish(lands, ks, fwd, after, name):
    m = len(lands)
    hops = [k for k in ks if k]

    def body(*refs):
        lnd = refs[:m]
        fsend_ref, frecv_ref = refs[m], refs[m + 1]
        x, y, c = _position()
        for slot in range(m):
            for fi, k in enumerate(hops):
                px, py = _flip(x, y, k)
                sent = lnd[slot].at[4 * px + 2 * py + c]
                came = lnd[slot].at[4 * px + 2 * py + (1 - c)]
                fat = len(hops) * slot + fi
                cp = _remote(sent, came, fsend_ref.at[fat], frecv_ref.at[fat], (x, y, c))
                cp.wait_recv()
                cp.wait_send()

    out = pl.pallas_call(
        body, name=name,
        out_shape=_hbm_like(lands),
        in_specs=[HBM] * m + [SEM, SEM] + [ANY] * len(after), out_specs=[HBM] * m,
        input_output_aliases={i: i for i in range(m)},
        compiler_params=pltpu.CompilerParams(has_side_effects=EFFECT),
    )(*lands, fwd[0], fwd[1], *after)
    return list(out)


def _pair_start(parts, name):
    n = len(parts)
    lands = [lax.empty((p.shape[0], 1) + p.shape[2:], p.dtype) for p in parts]

    def body(*refs):
        ins, lnd = refs[:n], refs[n:2 * n]
        send_ref, recv_ref = refs[2 * n], refs[2 * n + 1]
        token = refs[-1]
        x, y, c = _position()
        for i in range(n):
            _remote(ins[i].at[:, pl.ds(1 - c, 1)], lnd[i], send_ref.at[i], recv_ref.at[i], (x, y, 1 - c)).start()
        token[...] = jnp.zeros_like(token)

    out = pl.pallas_call(
        body, name=name,
        out_shape=[pltpu.SemaphoreType.DMA((n,)), pltpu.SemaphoreType.DMA((n,))]
        + _hbm_like(parts) + _hbm_like(lands) + [TOKEN],
        in_specs=[HBM] * (2 * n), out_specs=[SEM, SEM] + [HBM] * (2 * n) + [VMEM_SPEC],
        input_output_aliases={i: 2 + i for i in range(2 * n)},
        compiler_params=pltpu.CompilerParams(has_side_effects=EFFECT),
    )(*[_hbm(p) for p in parts], *[_hbm(l) for l in lands])
    return out[0], out[1], list(out[2:2 + n]), list(out[2 + n:2 + 2 * n]), out[-1]


def _pair_wait(parts, lands, send, recv, after, name):
    n = len(parts)

    def body(*refs):
        ins, lnd = refs[:n], refs[n:2 * n]
        send_ref, recv_ref = refs[2 * n], refs[2 * n + 1]
        x, y, c = _position()
        for i in range(n):
            cp = _remote(ins[i].at[:, pl.ds(1 - c, 1)], lnd[i], send_ref.at[i], recv_ref.at[i], (x, y, c))
            cp.wait_recv()
            cp.wait_send()

    out = pl.pallas_call(
        body, name=name,
        out_shape=_hbm_like(parts) + _hbm_like(lands),
        in_specs=[HBM] * (2 * n) + [SEM, SEM] + [ANY] * len(after), out_specs=[HBM] * (2 * n),
        input_output_aliases={i: i for i in range(2 * n)},
        compiler_params=pltpu.CompilerParams(has_side_effects=EFFECT),
    )(*parts, *lands, send, recv, *after)
    return list(out[:n]), list(out[n:])


def _chip_start(sums, lands, name, flips=(1, 2, 3)):
    n, ns = len(sums), len(flips)

    def body(*refs):
        ins, lnd = refs[:n], refs[n:2 * n]
        send_ref, recv_ref = refs[2 * n], refs[2 * n + 1]
        token = refs[-1]
        x, y, c = _position()
        for i in range(n):
            for j, flip in enumerate(flips):
                px, py = _flip(x, y, flip)
                _remote(ins[i].at[j], lnd[i].at[j], send_ref.at[ns * i + j], recv_ref.at[ns * i + j],
                        (px, py, c)).start()
        token[...] = jnp.zeros_like(token)

    out = pl.pallas_call(
        body, name=name,
        out_shape=[pltpu.SemaphoreType.DMA((ns * n,)), pltpu.SemaphoreType.DMA((ns * n,))]
        + _hbm_like(sums) + _hbm_like(lands) + [TOKEN],
        in_specs=[HBM] * (2 * n), out_specs=[SEM, SEM] + [HBM] * (2 * n) + [VMEM_SPEC],
        input_output_aliases={i: 2 + i for i in range(2 * n)},
        compiler_params=pltpu.CompilerParams(has_side_effects=EFFECT),
    )(*[_hbm(s) for s in sums], *[_hbm(l) for l in lands])
    return out[0], out[1], out[2:2 + n], out[2 + n:2 + 2 * n], out[-1]


def _chip_wait(sums, lands, send, recv, after, name):
    n, ns = len(sums), sums[0].shape[0]

    def body(*refs):
        ins, lnd = refs[:n], refs[n:2 * n]
        send_ref, recv_ref = refs[2 * n], refs[2 * n + 1]
        x, y, c = _position()
        for i in range(n):
            for j in range(ns):
                cp = _remote(ins[i].at[j], lnd[i].at[j], send_ref.at[ns * i + j], recv_ref.at[ns * i + j], (x, y, c))
                cp.wait_recv()
                cp.wait_send()

    out = pl.pallas_call(
        body, name=name,
        out_shape=_hbm_like(sums) + _hbm_like(lands),
        in_specs=[HBM] * (2 * n) + [SEM, SEM] + [ANY] * len(after), out_specs=[HBM] * (2 * n),
        input_output_aliases={i: i for i in range(2 * n)},
        compiler_params=pltpu.CompilerParams(has_side_effects=EFFECT),
    )(*sums, *lands, send, recv, *after)
    return list(out[n:])


def _pair_sum(idx, part, got, name, nslots=3):
    _, _, rows, cols = part.shape
    tr = _tile(rows, 512)

    def body(idx_ref, p_ref, q_ref, o_ref):
        o_ref[...] = (p_ref[...].astype(F32) + q_ref[...].astype(F32)).astype(o_ref.dtype)

    grid_spec = pltpu.PrefetchScalarGridSpec(
        num_scalar_prefetch=1, grid=(nslots, rows // tr),
        in_specs=[pl.BlockSpec((None, None, tr, cols), lambda j, r, idx: (idx[j], idx[4], r, 0)),
                  pl.BlockSpec((None, None, tr, cols), lambda j, r, idx: (idx[j], 0, r, 0))],
        out_specs=pl.BlockSpec((None, tr, cols), lambda j, r, idx: (j, r, 0)))
    return pl.pallas_call(
        body, name=name, grid_spec=grid_spec,
        out_shape=jax.ShapeDtypeStruct((nslots, rows, cols), part.dtype),
        compiler_params=_params("arbitrary", "arbitrary"),
    )(idx, part, got)


def _mm_proj(h, wg, groups, name):
    s, k = h.shape
    nchunk, _, n = wg.shape
    e = nchunk * n // groups
    tn = _tile(min(n, e), 512)

    def body(h_ref, w_ref, o_ref):
        o_ref[...] = jnp.dot(h_ref[...], w_ref[...], preferred_element_type=F32).astype(o_ref.dtype)

    return pl.pallas_call(
        body, name=name, grid=(nchunk * n // tn,),
        in_specs=[pl.BlockSpec((s, k), lambda j: (0, 0)),
                  pl.BlockSpec((None, k, tn), lambda j: ((j * tn) // n, 0, ((j * tn) % n) // tn))],
        out_specs=pl.BlockSpec((None, s, tn), lambda j: ((j * tn) // e, 0, ((j * tn) % e) // tn)),
        out_shape=jax.ShapeDtypeStruct((groups, s, e), BF16),
        compiler_params=_params("arbitrary"),
    )(h, wg)


def _mm_proj_group(h, wg, idx, pos, prev, name, deps=()):
    s, k = h.shape
    nchunk, _, n = wg.shape
    groups, _, e = prev.shape
    per = nchunk // groups
    assert per * n == e
    tn = _tile(n, 512)
    nd = len(deps)

    def body(idx_ref, h_ref, w_ref, prev_ref, *rest):
        o_ref = rest[nd]
        o_ref[...] = jnp.dot(h_ref[...], w_ref[...], preferred_element_type=F32).astype(o_ref.dtype)

    grid_spec = pltpu.PrefetchScalarGridSpec(
        num_scalar_prefetch=1, grid=(e // tn,),
        in_specs=[pl.BlockSpec((s, k), lambda j, idx: (0, 0)),
                  pl.BlockSpec((None, k, tn), lambda j, idx: (per * idx[pos] + (j * tn) // n, 0, ((j * tn) % n) // tn)),
                  ANY] + [ANY] * nd,
        out_specs=pl.BlockSpec((None, s, tn), lambda j, idx: (idx[pos], 0, j)))
    return pl.pallas_call(
        body, name=name, grid_spec=grid_spec,
        out_shape=jax.ShapeDtypeStruct(prev.shape, prev.dtype),
        input_output_aliases={3: 0},
        compiler_params=_params("arbitrary"),
    )(idx, h, wg, prev, *deps)


def _mm_out(yb, w, x, gate, name):
    s, k = yb.shape
    d = w.shape[1]
    tn = _tile(d, 512)
    tk = _tile(k, 1024)
    nk = k // tk

    def body(a_ref, w_ref, x_ref, g_ref, xo_ref, y_ref, acc_ref):
        kk = pl.program_id(1)

        @pl.when(kk == 0)
        def _():
            acc_ref[...] = jnp.zeros_like(acc_ref)

        acc_ref[...] += jnp.dot(a_ref[...], w_ref[...], preferred_element_type=F32)

        @pl.when(kk == nk - 1)
        def _():
            y = acc_ref[...]
            y_ref[...] = y.astype(y_ref.dtype)
            xo_ref[...] = x_ref[...] + g_ref[...] * y

    return pl.pallas_call(
        body, name=name, grid=(d // tn, nk),
        in_specs=[pl.BlockSpec((s, tk), lambda j, kk: (0, kk)),
                  pl.BlockSpec((tk, tn), lambda j, kk: (kk, j)),
                  pl.BlockSpec((s, tn), lambda j, kk: (0, j)),
                  pl.BlockSpec((1, tn), lambda j, kk: (0, j))],
        out_specs=[pl.BlockSpec((s, tn), lambda j, kk: (0, j)),
                   pl.BlockSpec((s, tn), lambda j, kk: (0, j))],
        out_shape=[jax.ShapeDtypeStruct((s, d), F32), jax.ShapeDtypeStruct((s, d), BF16)],
        scratch_shapes=[pltpu.VMEM((s, tn), F32)],
        compiler_params=_params("arbitrary", "arbitrary"),
    )(yb, w, x, gate)


def _mm_nt(a3, w3, out_dtype, name, deps=()):
    g, s, ea = a3.shape
    cw, n, nw = w3.shape
    total = g * ea
    assert total == cw * nw
    tk = _tile(min(ea, nw), 1024)
    tn = _tile(n, 1024)
    nk = total // tk

    def body(a_ref, w_ref, o_ref, acc_ref):
        kk = pl.program_id(1)

        @pl.when(kk == 0)
        def _():
            acc_ref[...] = jnp.zeros_like(acc_ref)

        acc_ref[...] += lax.dot_general(a_ref[...], w_ref[...], (((1,), (1,)), ((), ())),
                                        preferred_element_type=F32)

        @pl.when(kk == nk - 1)
        def _():
            o_ref[...] = acc_ref[...].astype(o_ref.dtype)

    return pl.pallas_call(
        _after(body, 2, deps), name=name, grid=(n // tn, nk),
        in_specs=[pl.BlockSpec((None, s, tk), lambda j, kk: ((kk * tk) // ea, 0, ((kk * tk) % ea) // tk)),
                  pl.BlockSpec((None, tn, tk), lambda j, kk: ((kk * tk) // nw, j, ((kk * tk) % nw) // tk))]
        + [ANY] * len(deps),
        out_specs=pl.BlockSpec((s, tn), lambda j, kk: (0, j)),
        out_shape=jax.ShapeDtypeStruct((s, n), out_dtype),
        scratch_shapes=[pltpu.VMEM((s, tn), F32)],
        compiler_params=_params("arbitrary", "arbitrary"),
    )(a3, w3, *deps)


def _mm_tn(a, b3, nchunk, name, deps=()):
    s, ka = a.shape
    g, _, eb = b3.shape
    n = g * eb // nchunk
    tm = _tile(ka, 1024)
    tn = _tile(min(n, eb), 1024)

    def body(a_ref, b_ref, o_ref, at_ref):
        @pl.when(pl.program_id(1) == 0)
        def _():
            at_ref[...] = a_ref[...].astype(F32).T.astype(at_ref.dtype)

        o_ref[...] = jnp.dot(at_ref[...], b_ref[...], preferred_element_type=F32).astype(o_ref.dtype)

    return pl.pallas_call(
        _after(body, 2, deps), name=name, grid=(ka // tm, g * eb // tn),
        in_specs=[pl.BlockSpec((s, tm), lambda i, j: (0, i)),
                  pl.BlockSpec((None, s, tn), lambda i, j: ((j * tn) // eb, 0, ((j * tn) % eb) // tn))]
        + [ANY] * len(deps),
        out_specs=pl.BlockSpec((None, tm, tn), lambda i, j: ((j * tn) // n, i, ((j * tn) % n) // tn)),
        out_shape=jax.ShapeDtypeStruct((nchunk, ka, n), BF16),
        scratch_shapes=[pltpu.VMEM((tm, s), BF16)],
        compiler_params=_params("arbitrary", "arbitrary"),
    )(a, b3, *deps)


def _mm_tn_group(a, b3, idx, pos, nchunk, name, deps=()):
    s, ka = a.shape
    _, _, eb = b3.shape
    n = eb // nchunk
    tm = _tile(ka, 1024)
    tn = _tile(n, 1024)
    nd = len(deps)

    def body(idx_ref, a_ref, b_ref, *rest):
        o_ref, at_ref = rest[nd:]

        @pl.when(pl.program_id(1) == 0)
        def _():
            at_ref[...] = a_ref[...].astype(F32).T.astype(at_ref.dtype)

        o_ref[...] = jnp.dot(at_ref[...], b_ref[...], preferred_element_type=F32).astype(o_ref.dtype)

    grid_spec = pltpu.PrefetchScalarGridSpec(
        num_scalar_prefetch=1, grid=(ka // tm, eb // tn),
        in_specs=[pl.BlockSpec((s, tm), lambda i, j, idx: (0, i)),
                  pl.BlockSpec((None, s, tn), lambda i, j, idx: (idx[pos], 0, j))] + [ANY] * nd,
        out_specs=pl.BlockSpec((None, tm, tn), lambda i, j, idx: ((j * tn) // n, i, ((j * tn) % n) // tn)),
        scratch_shapes=[pltpu.VMEM((tm, s), BF16)])
    return pl.pallas_call(
        body, name=name, grid_spec=grid_spec,
        out_shape=jax.ShapeDtypeStruct((nchunk, ka, n), BF16),
        compiler_params=_params("arbitrary", "arbitrary"),
    )(idx, a, b3, *deps)


def _sigmoid(z):
    return jax.nn.sigmoid(z)


def _shift_down(v, k, fill=0.0, period=None):
    if k == 0:
        return v
    row = lax.broadcasted_iota(jnp.int32, v.shape, 0)
    if period is not None:
        row = row & (period - 1)
    return jnp.where(row >= k, pltpu.roll(v, k, 0), fill)


def _shift_up(v, k, fill=0.0, period=None):
    if k == 0:
        return v
    s = v.shape[0]
    row = lax.broadcasted_iota(jnp.int32, v.shape, 0)
    if period is not None:
        row, s = row & (period - 1), period
    return jnp.where(row < s - k, pltpu.roll(v, v.shape[0] - k, 0), fill)


SCAN_BLOCK = 64


def _scan(a, b, shift):
    s = a.shape[0]
    blk = min(SCAN_BLOCK, s)
    k = 1
    while k < blk:
        b = a * shift(b, k, 0.0, blk) + b
        a = a * shift(a, k, 1.0, blk)
        k *= 2
    nblk = s // blk
    forward = shift is _shift_down
    order = range(nblk) if forward else range(nblk - 1, -1, -1)
    edge = blk - 1 if forward else 0
    out = [None] * nblk
    carry = None
    for i in order:
        h = b[i * blk:(i + 1) * blk]
        if carry is not None:
            h = a[i * blk:(i + 1) * blk] * carry + h
        carry = h[edge:edge + 1]
        out[i] = h
    return jnp.concatenate(out, axis=0) if nblk > 1 else out[0]


def _norm_mod(x, g, scale, shift, name, deps=()):
    s, d = x.shape
    ts = _tile(s, 256)

    def body(x_ref, g_ref, sc_ref, sh_ref, h_ref):
        xv = x_ref[...]
        rstd = lax.rsqrt(jnp.mean(xv * xv, axis=-1, keepdims=True) + EPS)
        nrm = xv * rstd * g_ref[...]
        h_ref[...] = (nrm * (1.0 + sc_ref[...]) + sh_ref[...]).astype(h_ref.dtype)

    vec = pl.BlockSpec((1, d), lambda i: (0, 0))
    return pl.pallas_call(
        _after(body, 4, deps), name=name, grid=(s // ts,),
        in_specs=[pl.BlockSpec((ts, d), lambda i: (i, 0)), vec, vec, vec] + [ANY] * len(deps),
        out_specs=pl.BlockSpec((ts, d), lambda i: (i, 0)),
        out_shape=jax.ShapeDtypeStruct((s, d), BF16),
        compiler_params=_params("arbitrary"),
    )(x, g, scale, shift, *deps)


def _gate_terms(dx, y_ref, gate_ref, dy_ref, dgate_ref):
    dy_ref[...] = (dx * gate_ref[...]).astype(dy_ref.dtype)
    dgate_ref[...] += jnp.sum(dx * y_ref[...].astype(F32), axis=0, keepdims=True)


def _norm_mod_bwd(x, dh, dx_res, g, scale, name, below=None, deps=()):
    s, d = x.shape
    ts = _tile(s, 256)
    nb = 2 if below is not None else 0

    def body(x_ref, dh_ref, dr_ref, g_ref, sc_ref, *rest):
        dx_ref, dsc_ref, dsh_ref, dg_ref = rest[nb:nb + 4]

        @pl.when(pl.program_id(0) == 0)
        def _():
            for ref in rest[nb + 1:nb + 4] + rest[nb + 5:]:
                ref[...] = jnp.zeros_like(ref)

        xv = x_ref[...]
        dh_v = dh_ref[...].astype(F32)
        gv = g_ref[...]
        rstd = lax.rsqrt(jnp.mean(xv * xv, axis=-1, keepdims=True) + EPS)
        xhat = xv * rstd
        dsc_ref[...] += jnp.sum(dh_v * xhat * gv, axis=0, keepdims=True)
        dsh_ref[...] += jnp.sum(dh_v, axis=0, keepdims=True)
        dn = dh_v * (1.0 + sc_ref[...])
        dg_ref[...] += jnp.sum(dn * xhat, axis=0, keepdims=True)
        dxhat = dn * gv
        proj = jnp.mean(dxhat * xhat, axis=-1, keepdims=True)
        dx = dr_ref[...] + rstd * (dxhat - xhat * proj)
        dx_ref[...] = dx
        if nb:
            _gate_terms(dx, rest[0], rest[1], rest[nb + 4], rest[nb + 5])

    row = pl.BlockSpec((ts, d), lambda i: (i, 0))
    vec = pl.BlockSpec((1, d), lambda i: (0, 0))
    extra = list(below) if nb else []
    return pl.pallas_call(
        _after(body, 5 + nb, deps), name=name, grid=(s // ts,),
        in_specs=[row, row, row, vec, vec] + [row, vec][:nb] + [ANY] * len(deps),
        out_specs=[row, vec, vec, vec] + [row, vec][:nb],
        out_shape=[jax.ShapeDtypeStruct((s, d), F32)] + [jax.ShapeDtypeStruct((1, d), F32)] * 3
        + [jax.ShapeDtypeStruct((s, d), BF16), jax.ShapeDtypeStruct((1, d), F32)][:nb],
        compiler_params=_params("arbitrary"),
    )(x, dh, dx_res, g, scale, *extra, *deps)


def _final_loss(x, g, target, y, gate, name):
    s, d = x.shape
    ts = _tile(s, 256)

    def body(x_ref, g_ref, t_ref, y_ref, gate_ref, dx_ref, loss_ref, dg_ref, dy_ref, dgate_ref):
        @pl.when(pl.program_id(0) == 0)
        def _():
            loss_ref[...] = jnp.zeros_like(loss_ref)
            dg_ref[...] = jnp.zeros_like(dg_ref)
            dgate_ref[...] = jnp.zeros_like(dgate_ref)

        xv = x_ref[...]
        gv = g_ref[...]
        rstd = lax.rsqrt(jnp.mean(xv * xv, axis=-1, keepdims=True) + EPS)
        xhat = xv * rstd
        err = xhat * gv - t_ref[...]
        loss_ref[...] += 0.5 * jnp.sum(jnp.mean(err * err, axis=-1, keepdims=True))
        dy = err * (1.0 / d)
        dg_ref[...] += jnp.sum(dy * xhat, axis=0, keepdims=True)
        dxhat = dy * gv
        proj = jnp.mean(dxhat * xhat, axis=-1, keepdims=True)
        dx = rstd * (dxhat - xhat * proj)
        dx_ref[...] = dx
        _gate_terms(dx, y_ref, gate_ref, dy_ref, dgate_ref)

    row = pl.BlockSpec((ts, d), lambda i: (i, 0))
    vec = pl.BlockSpec((1, d), lambda i: (0, 0))
    return pl.pallas_call(
        body, name=name, grid=(s // ts,),
        in_specs=[row, vec, row, row, vec],
        out_specs=[row, pl.BlockSpec((1, LANES), lambda i: (0, 0)), vec, row, vec],
        out_shape=[jax.ShapeDtypeStruct((s, d), F32), jax.ShapeDtypeStruct((1, LANES), F32),
                   jax.ShapeDtypeStruct((1, d), F32), jax.ShapeDtypeStruct((s, d), BF16),
                   jax.ShapeDtypeStruct((1, d), F32)],
        compiler_params=_params("arbitrary"),
    )(x, g, target, y, gate)


def _conv(v, w_ref, width):
    out = w_ref[width - 1:width, :] * v
    for k in range(width - 1):
        out = out + w_ref[k:k + 1, :] * _shift_down(v, width - 1 - k)
    return out


def _sc_fwd(proj, conv_w, name, deps=()):
    _, s, e = proj.shape
    te = _tile(e, 256)
    width = conv_w.shape[0]

    def body(b_ref, c_ref, v_ref, g_ref, w_ref, o_ref):
        cv = c_ref[...].astype(F32) * v_ref[...].astype(F32)
        u = _conv(cv, w_ref, width)
        gv = g_ref[...].astype(F32)
        o_ref[...] = (b_ref[...].astype(F32) * u * (gv * _sigmoid(gv))).astype(o_ref.dtype)

    def part(q):
        return pl.BlockSpec((None, s, te), lambda j, q=q: (q, 0, j))

    return pl.pallas_call(
        _after(body, 5, deps), name=name, grid=(e // te,),
        in_specs=[part(0), part(1), part(2), part(3), pl.BlockSpec((width, te), lambda j: (0, j))]
        + [ANY] * len(deps),
        out_specs=pl.BlockSpec((s, te), lambda j: (0, j)),
        out_shape=jax.ShapeDtypeStruct((s, e), BF16),
        compiler_params=_params("arbitrary"),
    )(proj, proj, proj, proj, conv_w, *deps)


def _sc_bwd(proj, dyb, conv_w, name, deps=()):
    _, s, e = proj.shape
    te = _tile(e, 256)
    width = conv_w.shape[0]

    def body(b_ref, c_ref, v_ref, g_ref, dy_ref, w_ref, dp_ref, vec_ref):
        bv = b_ref[...].astype(F32)
        cvl = c_ref[...].astype(F32)
        vv = v_ref[...].astype(F32)
        gv = g_ref[...].astype(F32)
        dyv = dy_ref[...].astype(F32)
        cv = cvl * vv
        u = _conv(cv, w_ref, width)
        sg = _sigmoid(gv)
        silu = gv * sg
        dp_ref[0] = (dyv * u * silu).astype(dp_ref.dtype)
        du = dyv * bv * silu
        dp_ref[3] = (dyv * bv * u * (sg * (1.0 + gv * (1.0 - sg)))).astype(dp_ref.dtype)
        dcv = w_ref[width - 1:width, :] * du
        vec_ref[...] = jnp.zeros_like(vec_ref)
        vec_ref[width - 1:width, :] = jnp.sum(du * cv, axis=0, keepdims=True)
        for k in range(width - 1):
            sh = width - 1 - k
            dcv = dcv + w_ref[k:k + 1, :] * _shift_up(du, sh)
            vec_ref[k:k + 1, :] = jnp.sum(du * _shift_down(cv, sh), axis=0, keepdims=True)
        dp_ref[1] = (dcv * vv).astype(dp_ref.dtype)
        dp_ref[2] = (dcv * cvl).astype(dp_ref.dtype)

    def part(q):
        return pl.BlockSpec((None, s, te), lambda j, q=q: (q, 0, j))

    return pl.pallas_call(
        _after(body, 6, deps), name=name, grid=(e // te,),
        in_specs=[part(0), part(1), part(2), part(3), pl.BlockSpec((s, te), lambda j: (0, j)),
                  pl.BlockSpec((width, te), lambda j: (0, j))] + [ANY] * len(deps),
        out_specs=[pl.BlockSpec((4, s, te), lambda j: (0, 0, j)),
                   pl.BlockSpec((8, te), lambda j: (0, j))],
        out_shape=[jax.ShapeDtypeStruct((4, s, e), BF16), jax.ShapeDtypeStruct((8, e), F32)],
        compiler_params=_params("arbitrary"),
    )(proj, proj, proj, proj, dyb, conv_w, *deps)


def _lru_gates(v_pre, w_ref, cb_ref, wa_ref, ba_ref, wx_ref, bx_ref, lam_ref, width):
    v = _conv(v_pre, w_ref, width) + cb_ref[...]
    vb = v.astype(BF16)
    r = _sigmoid(jnp.dot(vb, wa_ref[...], preferred_element_type=F32) + ba_ref[...])
    i = _sigmoid(jnp.dot(vb, wx_ref[...], preferred_element_type=F32) + bx_ref[...])
    nl = -lam_ref[...]
    sp = jnp.maximum(nl, 0.0) + jnp.log1p(jnp.exp(-jnp.abs(nl)))
    log_a = (-RGLRU_C) * r * sp
    a = jnp.exp(log_a)
    one_minus_a2 = jnp.tanh(-log_a) * (1.0 + a * a)
    mult = jnp.sqrt(one_minus_a2)
    return v, vb, r, i, sp, a, mult


def _lru_specs(s, dh, heads, width):
    head_col = lambda q: pl.BlockSpec((None, s, dh), lambda h, q=q: (q, 0, h))
    vec = pl.BlockSpec((1, dh), lambda h: (0, h))
    mat = pl.BlockSpec((None, dh, dh), lambda h: (h, 0, 0))
    weights = [pl.BlockSpec((width, dh), lambda h: (0, h)), vec, mat, vec, mat, vec, vec]
    return head_col, weights


def _lru_fwd(proj, conv_w, conv_b, w_a, b_a, w_x, b_x, lam, name, deps=()):
    _, s, e = proj.shape
    heads, dh, _ = w_a.shape
    width = conv_w.shape[0]

    def body(v_ref, g_ref, w_ref, cb_ref, wa_ref, ba_ref, wx_ref, bx_ref, lam_ref, yb_ref, hs_ref):
        v, _, _, i, _, a, mult = _lru_gates(v_ref[...].astype(F32), w_ref, cb_ref, wa_ref, ba_ref,
                                           wx_ref, bx_ref, lam_ref, width)
        hs = _scan(a, mult * i * v, _shift_down)
        hs_ref[...] = hs
        gv = g_ref[...].astype(F32)
        yb_ref[...] = (hs * (gv * _sigmoid(gv))).astype(yb_ref.dtype)

    head_col, weights = _lru_specs(s, dh, heads, width)
    out = pl.BlockSpec((s, dh), lambda h: (0, h))
    return pl.pallas_call(
        _after(body, 9, deps), name=name, grid=(heads,),
        in_specs=[head_col(0), head_col(1)] + weights + [ANY] * len(deps),
        out_specs=[out, out],
        out_shape=[jax.ShapeDtypeStruct((s, e), BF16), jax.ShapeDtypeStruct((s, e), F32)],
        compiler_params=_params("arbitrary"),
    )(proj, proj, conv_w, conv_b, w_a, b_a, w_x, b_x, lam, *deps)


def _lru_bwd(proj, hs, dyb, conv_w, conv_b, w_a, b_a, w_x, b_x, lam, name, deps=()):
    _, s, e = proj.shape
    heads, dh, _ = w_a.shape
    width = conv_w.shape[0]

    def body(v_ref, g_ref, hs_ref, dy_ref, w_ref, cb_ref, wa_ref, ba_ref, wx_ref, bx_ref, lam_ref,
             dp_ref, dwa_ref, dwx_ref, vec_ref):
        v_pre = v_ref[...].astype(F32)
        v, vb, r, i, sp, a, mult = _lru_gates(v_pre, w_ref, cb_ref, wa_ref, ba_ref, wx_ref, bx_ref,
                                              lam_ref, width)
        hs = hs_ref[...]
        gv = g_ref[...].astype(F32)
        dyv = dy_ref[...].astype(F32)
        sg = _sigmoid(gv)
        dp_ref[1] = (dyv * hs * (sg * (1.0 + gv * (1.0 - sg)))).astype(dp_ref.dtype)
        dhs = dyv * (gv * sg)
        d_h = _scan(_shift_up(a, 1), dhs, _shift_up)
        da = d_h * _shift_down(hs, 1)
        iv = i * v
        dlog_a = da * a - (d_h * iv) * (a * a) / mult
        di = d_h * mult * v
        dv = d_h * mult * i
        dzr = dlog_a * (-RGLRU_C) * sp * r * (1.0 - r)
        dzi = di * i * (1.0 - i)
        dsp = jnp.sum(dlog_a * r, axis=0, keepdims=True) * (-RGLRU_C)
        vec_ref[...] = jnp.zeros_like(vec_ref)
        vec_ref[0:1, :] = jnp.sum(dzr, axis=0, keepdims=True)
        vec_ref[1:2, :] = jnp.sum(dzi, axis=0, keepdims=True)
        vec_ref[2:3, :] = -dsp * _sigmoid(-lam_ref[...])
        dzr_b = dzr.astype(BF16)
        dzi_b = dzi.astype(BF16)
        vt = vb.astype(F32).T.astype(BF16)
        dwa_ref[...] = jnp.dot(vt, dzr_b, preferred_element_type=F32).astype(dwa_ref.dtype)
        dwx_ref[...] = jnp.dot(vt, dzi_b, preferred_element_type=F32).astype(dwx_ref.dtype)
        nt = (((1,), (1,)), ((), ()))
        dv = dv + lax.dot_general(dzr_b, wa_ref[...], nt, preferred_element_type=F32)
        dv = dv + lax.dot_general(dzi_b, wx_ref[...], nt, preferred_element_type=F32)
        vec_ref[3:4, :] = jnp.sum(dv, axis=0, keepdims=True)
        dvp = w_ref[width - 1:width, :] * dv
        vec_ref[4 + width - 1:4 + width, :] = jnp.sum(dv * v_pre, axis=0, keepdims=True)
        for k in range(width - 1):
            sh = width - 1 - k
            dvp = dvp + w_ref[k:k + 1, :] * _shift_up(dv, sh)
            vec_ref[4 + k:5 + k, :] = jnp.sum(dv * _shift_down(v_pre, sh), axis=0, keepdims=True)
        dp_ref[0] = dvp.astype(dp_ref.dtype)

    head_col, weights = _lru_specs(s, dh, heads, width)
    col = pl.BlockSpec((s, dh), lambda h: (0, h))
    mat = pl.BlockSpec((None, dh, dh), lambda h: (h, 0, 0))
    return pl.pallas_call(
        _after(body, 11, deps), name=name, grid=(heads,),
        in_specs=[head_col(0), head_col(1), col, col] + weights + [ANY] * len(deps),
        out_specs=[pl.BlockSpec((2, s, dh), lambda h: (0, 0, h)), mat, mat,
                   pl.BlockSpec((16, dh), lambda h: (0, h))],
        out_shape=[jax.ShapeDtypeStruct((2, s, e), BF16),
                   jax.ShapeDtypeStruct((heads, dh, dh), BF16),
                   jax.ShapeDtypeStruct((heads, dh, dh), BF16),
                   jax.ShapeDtypeStruct((16, e), F32)],
        compiler_params=_params("arbitrary"),
    )(proj, proj, hs, dyb, conv_w, conv_b, w_a, b_a, w_x, b_x, lam, *deps)


def _ada_mod(c_all, w, b, name):
    layers, d, f = w.shape
    nb = c_all.shape[0]

    def body(c_ref, w_ref, b_ref, o_ref):
        cv = c_ref[...]
        sc = cv * _sigmoid(cv)
        o_ref[...] = jnp.dot(sc, w_ref[...], preferred_element_type=F32,
                             precision=lax.Precision.HIGHEST) + b_ref[...]

    return pl.pallas_call(
        body, name=name, grid=(layers,),
        in_specs=[pl.BlockSpec((nb, d), lambda l: (0, 0)),
                  pl.BlockSpec((None, d, f), lambda l: (l, 0, 0)),
                  pl.BlockSpec((None, 1, f), lambda l: (l, 0, 0))],
        out_specs=pl.BlockSpec((None, nb, f), lambda l: (l, 0, 0)),
        out_shape=jax.ShapeDtypeStruct((layers, nb, f), F32),
        compiler_params=_params("arbitrary"),
    )(c_all, w, b)


def _ada_grad(c_all_t, dmod, name):
    d, nb = c_all_t.shape
    layers, _, f = dmod.shape

    def body(c_ref, dm_ref, o_ref):
        cv = c_ref[...]
        sc = cv * _sigmoid(cv)
        acc = sc[:, 0:1] * dm_ref[0:1, :]
        for k in range(1, nb):
            acc = acc + sc[:, k:k + 1] * dm_ref[k:k + 1, :]
        o_ref[...] = acc

    return pl.pallas_call(
        body, name=name, grid=(layers,),
        in_specs=[pl.BlockSpec((d, nb), lambda l: (0, 0)),
                  pl.BlockSpec((None, nb, f), lambda l: (l, 0, 0))],
        out_specs=pl.BlockSpec((None, d, f), lambda l: (l, 0, 0)),
        out_shape=jax.ShapeDtypeStruct((layers, d, f), F32),
        compiler_params=_params("arbitrary"),
    )(c_all_t, dmod)


def _device_sum(g, name):
    _, rows, _ = g.shape

    def body(g_ref, o_ref):
        acc = g_ref[0]
        for k in range(1, N_DEV):
            acc = acc + g_ref[k]
        o_ref[...] = acc

    return pl.pallas_call(
        body, name=name,
        in_specs=[VMEM_SPEC], out_specs=VMEM_SPEC,
        out_shape=jax.ShapeDtypeStruct((rows, LANES), F32),
        compiler_params=pltpu.CompilerParams(vmem_limit_bytes=VMEM_LIMIT),
    )(g)


def _adamw_math(w, g, m, v):
    m = ADAM_B1 * m + (1.0 - ADAM_B1) * g
    v = ADAM_B2 * v + (1.0 - ADAM_B2) * (g * g)
    m_hat = m / (1.0 - ADAM_B1 ** ADAM_STEP)
    v_hat = v / (1.0 - ADAM_B2 ** ADAM_STEP)
    delta = -ADAM_LR * (m_hat / (jnp.sqrt(v_hat) + ADAM_EPS) + ADAM_WD * w)
    return delta, m, v


def _adamw(w, g, m, v, name):
    rows, cols = w.shape
    tr = _tile(rows, 256)

    def body(w_ref, g_ref, m_ref, v_ref, d_ref, mo_ref, vo_ref):
        d_ref[...], mo_ref[...], vo_ref[...] = _adamw_math(w_ref[...], g_ref[...], m_ref[...], v_ref[...])

    blk = pl.BlockSpec((tr, cols), lambda i: (i, 0))
    return pl.pallas_call(
        body, name=name, grid=(rows // tr,),
        in_specs=[blk] * 4, out_specs=[blk] * 3,
        out_shape=[jax.ShapeDtypeStruct((rows, cols), F32)] * 3,
        compiler_params=_params("arbitrary"),
    )(w, g, m, v)


def _adamw_reduced(idx, w, m, v, part, got, recvs, name):
    rows, cols = w.shape
    tr = _tile(rows, 256)
    nr = len(recvs)

    def body(idx_ref, w_ref, m_ref, v_ref, p_ref, q_ref, *rest):
        g_ref, d_ref, mo_ref, vo_ref = rest[nr:]
        g = p_ref[...].astype(F32) + q_ref[...].astype(F32)
        for u_ref in rest[:nr]:
            for j in range(u_ref.shape[0]):
                g = g + u_ref[j].astype(F32)
        g_ref[...] = g
        d_ref[...], mo_ref[...], vo_ref[...] = _adamw_math(w_ref[...], g, m_ref[...], v_ref[...])

    blk = pl.BlockSpec((tr, cols), lambda i, idx: (i, 0))
    grid_spec = pltpu.PrefetchScalarGridSpec(
        num_scalar_prefetch=1, grid=(rows // tr,),
        in_specs=[blk, blk, blk,
                  pl.BlockSpec((None, None, tr, cols), lambda i, idx: (idx[3], idx[4], i, 0)),
                  pl.BlockSpec((None, None, tr, cols), lambda i, idx: (idx[3], 0, i, 0))]
        + [pl.BlockSpec((u.shape[0], tr, cols), lambda i, idx: (0, i, 0)) for u in recvs],
        out_specs=[blk] * 4)
    return pl.pallas_call(
        body, name=name, grid_spec=grid_spec,
        out_shape=[jax.ShapeDtypeStruct((rows, cols), F32)] * 4,
        compiler_params=_params("arbitrary"),
    )(idx, w, m, v, part, got, *recvs)


def _pack(vectors):
    flat = jnp.concatenate([v.reshape(-1).astype(F32) for v in vectors])
    pad = (-flat.shape[0]) % (8 * LANES)
    return jnp.pad(flat, (0, pad)).reshape(-1, LANES)


def _unpack(flat, shapes):
    out, off = [], 0
    for shp in shapes:
        size = math.prod(shp)
        out.append(flat[..., off:off + size].reshape(flat.shape[:-1] + tuple(shp)))
        off += size
    return out


def _my_slice(full, me, axis):
    size = full.shape[axis] // N_DEV
    return lax.dynamic_slice_in_dim(full, me * size, size, axis)


def kernel(x, c, norm_g, ada_w, ada_b, sc_w_in, sc_conv_w, sc_w_out, lru_w_in, lru_conv_w, lru_conv_b, lru_w_a, lru_b_a, lru_w_x, lru_b_x, lru_lambda, lru_w_out, final_g, loss_target, m_norm_g, m_ada_w, m_ada_b, m_sc_w_in, m_sc_conv_w, m_sc_w_out, m_lru_w_in, m_lru_conv_w, m_lru_conv_b, m_lru_w_a, m_lru_b_a, m_lru_w_x, m_lru_b_x, m_lru_lambda, m_lru_w_out, m_final_g, v_norm_g, v_ada_w, v_ada_b, v_sc_w_in, v_sc_conv_w, v_sc_w_out, v_lru_w_in, v_lru_conv_w, v_lru_conv_b, v_lru_w_a, v_lru_b_a, v_lru_w_x, v_lru_b_x, v_lru_lambda, v_lru_w_out, v_final_g):
    _, s, d = x.shape
    e = sc_w_out.shape[1] * N_DEV
    heads, dh_s, dh = lru_w_a.shape[1:]
    es = e // N_DEV
    f = ada_w.shape[2]
    mx, my, mc = _position()
    me = 4 * mx + 2 * my + mc
    chip = 2 * mx + my
    idx = jnp.stack([chip ^ 1, chip ^ 2, chip ^ 3, chip, mc]).astype(jnp.int32)

    x0 = x[0]
    target = loss_target[0]

    small_shapes = [(d,), (3, es), (4, es), (es,), (heads, dh_s), (heads, dh_s), (es,)]
    small = _small_gather(_pack([c, sc_conv_w, lru_conv_w, lru_conv_b, lru_b_a, lru_b_x, lru_lambda]),
                          "gather_small_weights").reshape(N_DEV, -1)
    c_all, cw3, cw4, cb, ba, bx, lam = _unpack(small, small_shapes)
    cw3 = cw3.transpose(1, 0, 2).reshape(3, e)
    cw4 = cw4.transpose(1, 0, 2).reshape(4, e)
    cb = cb.reshape(1, e)
    lam = lam.reshape(1, e)
    ba = ba.transpose(1, 0, 2).reshape(1, e)
    bx = bx.transpose(1, 0, 2).reshape(1, e)

    shards = [sc_w_in[0].astype(BF16), sc_w_out[0].astype(BF16), lru_w_in[0].astype(BF16),
              lru_w_a[0].reshape(heads * dh_s, dh).astype(BF16),
              lru_w_x[0].reshape(heads * dh_s, dh).astype(BF16), lru_w_out[0].astype(BF16)]
    lands = [lax.dynamic_update_slice(lax.empty((N_DEV,) + sh.shape, BF16), sh[None], (me, 0, 0))
             for sh in shards]
    every = [1, 2, 3, 0]
    units = [([0], [0]), ([0], [1]), ([0], [2]), ([0], [3]), ([1], every), ([2], every), ([3, 4], every),
             ([5], every)]
    sems, first_sh, first_ld, started = _gather_start(shards[:1], lands[:1], units[:3], [small],
                                                      "gather_start_first")
    shards, lands = first_sh + shards[1:], first_ld + lands[1:]

    ada_b_mine = _my_slice(ada_b, me, 1).reshape(2, 1, f)
    mod_mine = _ada_mod(c_all, ada_w, ada_b_mine, "ada_mod")
    mod_all = _small_gather(_pack([mod_mine]), "gather_mod", deps=[started])

    def start_later(after):
        far_sems, far_sh, far_ld, tok = _gather_start(shards[:1], lands[:1], units[3:4], after, "gather_start_far")
        rest_units = [([i - 1 for i in members], ks) for members, ks in units[4:]]
        rest_sems, rest_sh, rest_ld, tok = _gather_start(shards[1:], lands[1:], rest_units, [tok],
                                                         "gather_start_rest")
        sems.extend(far_sems + rest_sems)
        shards[:], lands[:] = far_sh + rest_sh, far_ld + rest_ld
        return tok

    def gathered(u, after_forward, name):
        members, ks = units[u]
        fwd, shs, lnd, token = _gather_forward(
            [shards[i] for i in members], [lands[i] for i in members], ks, sems[u][0], sems[u][1],
            after_forward, "gather_forward_" + name)
        for i, sh, ld in zip(members, shs, lnd):
            shards[i], lands[i] = sh, ld

        def finish(after):
            out = _gather_finish([lands[i] for i in members], ks, fwd, after, "gather_finish_" + name)
            for i, ld in zip(members, out):
                lands[i] = ld
            return out

        return token, finish

    tok, finish_y = gathered(1, [mod_all], "sc_w_in_near_y")
    tok, finish_x = gathered(2, [tok], "sc_w_in_near_x")
    queued = start_later([tok])

    mod_all = mod_all.reshape(N_DEV, -1)
    mod_all = mod_all[:, :2 * N_DEV * f].reshape(N_DEV, 2, N_DEV, f)
    mod_all = mod_all.transpose(1, 2, 0, 3).reshape(2, N_DEV, 3 * d)
    mod = lax.dynamic_index_in_dim(mod_all, me, 1, keepdims=False)
    shift = [mod[l:l + 1, 0:d] for l in range(2)]
    scale = [mod[l:l + 1, d:2 * d] for l in range(2)]
    gate = [mod[l:l + 1, 2 * d:3 * d] for l in range(2)]
    ng = [norm_g[l:l + 1] for l in range(2)]
    fg = final_g.reshape(1, d)

    h0 = _norm_mod(x0, ng[0], scale[0], shift[0], "norm_mod_0", deps=[queued])
    proj0 = lax.empty((4, s, e), BF16)
    tok, _ = gathered(0, [h0], "sc_w_in_own")
    proj0 = _mm_proj_group(h0, lands[0], idx, 3, proj0, "mm_proj_0_own", deps=[tok])
    for u, name, finish in ((1, "near_y", finish_y), (2, "near_x", finish_x), (3, "far", None)):
        after = [proj0]
        if finish is None:
            tok, finish = gathered(u, [proj0], "sc_w_in_" + name)
            after = [tok]
        wg_in0, = finish(after)
        proj0 = _mm_proj_group(h0, wg_in0, idx, u - 1, proj0, "mm_proj_0_" + name)
    tok, finish = gathered(4, [proj0], "sc_w_out")
    yb0 = _sc_fwd(proj0, cw3, "sc_fwd", deps=[tok])
    w_out0 = finish([yb0])[0].reshape(e, d)
    x1, y0 = _mm_out(yb0, w_out0, x0, gate[0], "mm_out_0")
    tok, finish = gathered(5, [x1], "lru_w_in")
    h1 = _norm_mod(x1, ng[1], scale[1], shift[1], "norm_mod_1", deps=[tok])
    wg_in1, = finish([h1])
    proj1 = _mm_proj(h1, wg_in1, 2, "mm_proj_1")
    tok, finish = gathered(6, [proj1], "lru_gates")
    wg_a, wg_x = finish([tok])
    w_a = wg_a.reshape(N_DEV, heads, dh_s, dh).transpose(1, 0, 2, 3).reshape(heads, dh, dh)
    w_x = wg_x.reshape(N_DEV, heads, dh_s, dh).transpose(1, 0, 2, 3).reshape(heads, dh, dh)
    tok, finish = gathered(7, [w_a, w_x], "lru_w_out")
    yb1, hs = _lru_fwd(proj1, cw4, cb, w_a, ba, w_x, bx, lam, "lru_fwd", deps=[tok])
    w_out1 = finish([yb1])[0].reshape(e, d)
    x2, y1 = _mm_out(yb1, w_out1, x1, gate[1], "mm_out_1")
    dx2, loss_part, d_fg, dy1, dgate1 = _final_loss(x2, fg, target, y1, gate[1], "final_loss")

    def pieces(g, rows, cols):
        return g.reshape(4, 2, rows, cols)

    def by_rows(g):
        return g.reshape(heads, N_DEV, dh_s, dh).transpose(1, 0, 2, 3).reshape(N_DEV, heads * dh_s, dh)

    def pair_begin(parts, group):
        send, recv, parts, lnd, token = _pair_start(parts, "pair_start_" + group)
        return dict(parts=parts, lands=lnd, send=send, recv=recv, group=group), token

    def scatter_start(pair, names, after):
        group = pair["group"]
        parts, gots = _pair_wait(pair["parts"], pair["lands"], pair["send"], pair["recv"], after,
                                 "pair_wait_" + group)
        sums = [_pair_sum(idx, p, q, "pair_sum_" + nm) for p, q, nm in zip(parts, gots, names)]
        empties = [lax.empty(sm.shape, sm.dtype) for sm in sums]
        send, recv, sums, lnd, token = _chip_start(sums, empties, "chip_start_" + group)
        return dict(parts=parts, gots=gots, names=names, group=group, sums=sums, lands=lnd,
                    send=send, recv=recv), token

    big = {"sc_w_in": (sc_w_in, m_sc_w_in, v_sc_w_in), "sc_w_out": (sc_w_out, m_sc_w_out, v_sc_w_out),
           "lru_w_in": (lru_w_in, m_lru_w_in, v_lru_w_in), "lru_w_a": (lru_w_a, m_lru_w_a, v_lru_w_a),
           "lru_w_x": (lru_w_x, m_lru_w_x, v_lru_w_x), "lru_w_out": (lru_w_out, m_lru_w_out, v_lru_w_out)}
    big_res = {}

    def scatter_finish(rs, after):
        recvs = _chip_wait(rs["sums"], rs["lands"], rs["send"], rs["recv"], after, "chip_wait_" + rs["group"])
        done = []
        for p, q, u, nm in zip(rs["parts"], rs["gots"], recvs, rs["names"]):
            w, m, v = big[nm]
            shp2 = p.shape[2:]
            res = _adamw_reduced(idx, w.reshape(shp2), m.reshape(shp2), v.reshape(shp2), p, q, [u], "adamw_" + nm)
            big_res[nm] = [r.reshape(w.shape) for r in res]
            done.append(res[1])
        return done

    dw_out1 = _mm_tn(yb1, dy1[None], 1, "mm_dw_out_1")
    pair, tok = pair_begin([pieces(dw_out1, es, d)], "lru_w_out")
    dyb1 = _mm_nt(dy1[None], w_out1[None], BF16, "mm_dyb_1", deps=[tok])
    rs1, tok = scatter_start(pair, ["lru_w_out"], [dyb1])
    dproj1, dw_a, dw_x, vecs1 = _lru_bwd(proj1, hs, dyb1, cw4, cb, w_a, ba, w_x, bx, lam, "lru_bwd", deps=[tok])
    done = scatter_finish(rs1, [dproj1])
    dw_in1 = _mm_tn(h1, dproj1, N_DEV, "mm_dw_in_1", deps=done)
    pair, tok = pair_begin([pieces(dw_in1, d, 2 * es), pieces(by_rows(dw_a), heads * dh_s, dh),
                            pieces(by_rows(dw_x), heads * dh_s, dh)], "lru_in")
    dh1 = _mm_nt(dproj1, wg_in1, F32, "mm_dh_1", deps=[tok])
    rs2, tok = scatter_start(pair, ["lru_w_in", "lru_w_a", "lru_w_x"], [dh1])
    dx1, dscale1, dshift1, dng1, dy0, dgate0 = _norm_mod_bwd(x1, dh1, dx2, ng[1], scale[1], "norm_mod_bwd_1",
                                                             below=(y0, gate[0]), deps=[tok])
    dw_out0 = _mm_tn(yb0, dy0[None], 1, "mm_dw_out_0")
    pair, tok = pair_begin([pieces(dw_out0, es, d)], "sc_w_out")
    dyb0 = _mm_nt(dy0[None], w_out0[None], BF16, "mm_dyb_0", deps=[tok])
    rs3, tok = scatter_start(pair, ["sc_w_out"], [dyb0])
    dproj0, vecs0 = _sc_bwd(proj0, dyb0, cw3, "sc_bwd", deps=[tok])
    idx_one = jnp.stack([jnp.zeros_like(mc)] * 4 + [mc]).astype(jnp.int32)
    sc_w_in_steps = []

    def chip_step(j, pair, after):
        (part,), (got,) = _pair_wait(pair["parts"], pair["lands"], pair["send"], pair["recv"], after,
                                     "pair_wait_sc_w_in_%d" % j)
        sm = _pair_sum(idx_one, part, got, "pair_sum_sc_w_in_%d" % j, nslots=1)
        send, recv, sums, lnd, token = _chip_start([sm], [lax.empty(sm.shape, sm.dtype)],
                                                   "chip_start_sc_w_in_%d" % j, flips=(j,))
        sc_w_in_steps.append((sums, lnd, send, recv, j))
        return token

    pending, done = None, []
    for j in (3, 1, 2, 0):
        part = _mm_tn_group(h0, dproj0, idx, (j - 1) % 4, 2, "mm_dw_in_0_%d" % j, deps=done)[None]
        pair, tok = pair_begin([part], "sc_w_in_%d" % j)
        if j == 3:
            done = scatter_finish(rs2, [chip_step(j, pair, [tok])])
            continue
        done = [tok]
        if pending is not None:
            done.append(chip_step(pending[0], pending[1], [tok]))
        pending = (j, pair)
    dh0 = _mm_nt(dproj0, wg_in0, F32, "mm_dh_0", deps=done)
    pair = pending[1]
    (part,), (got,) = _pair_wait(pair["parts"], pair["lands"], pair["send"], pair["recv"], [dh0],
                                 "pair_wait_sc_w_in_0")
    dx0, dscale0, dshift0, dng0 = _norm_mod_bwd(x0, dh0, dx1, ng[0], scale[0], "norm_mod_bwd_0")
    done = scatter_finish(rs3, [dx0])
    dmod_mine = jnp.concatenate([dshift0, dscale0, dgate0, dshift1, dscale1, dgate1], axis=1)
    end_shapes = [(LANES,), (2, 3 * d), (2, d), (d,), (8, e), (16, e)]
    end_all = _small_gather(
        _pack([loss_part, dmod_mine, jnp.concatenate([dng0, dng1], axis=0), d_fg, vecs0, vecs1]),
        "gather_small_grads", deps=done)
    end_sum = _device_sum(end_all, "sum_small_grads").reshape(-1)
    loss_v, g_ada_b, g_norm_g, g_final_g, sum0, sum1 = _unpack(end_sum, end_shapes)
    loss = loss_v[0]
    dmod_all = _unpack(end_all.reshape(N_DEV, -1), end_shapes)[1].transpose(1, 0, 2)
    dmod_cols = _my_slice(dmod_all, me, 2)
    g_ada_w = _ada_grad(c_all.T, dmod_cols, "ada_grad")

    g_sc_conv_w = _my_slice(sum0[0:3], me, 1)
    g_lru_b_a = _my_slice(sum1[0].reshape(heads, dh), me, 1)
    g_lru_b_x = _my_slice(sum1[1].reshape(heads, dh), me, 1)
    g_lru_lambda = _my_slice(sum1[2:3], me, 1)
    g_lru_conv_b = _my_slice(sum1[3:4], me, 1)
    g_lru_conv_w = _my_slice(sum1[4:8], me, 1)

    ada_res = _adamw(ada_w.reshape(2 * d, f), g_ada_w.reshape(2 * d, f), m_ada_w.reshape(2 * d, f),
                     v_ada_w.reshape(2 * d, f), "adamw_ada_w")
    ada_out = [g_ada_w] + [r.reshape(ada_w.shape) for r in ada_res]

    small_w = [norm_g, ada_b, final_g, sc_conv_w, lru_conv_w, lru_conv_b, lru_b_a, lru_b_x, lru_lambda]
    small_m = [m_norm_g, m_ada_b, m_final_g, m_sc_conv_w, m_lru_conv_w, m_lru_conv_b, m_lru_b_a, m_lru_b_x,
               m_lru_lambda]
    small_v = [v_norm_g, v_ada_b, v_final_g, v_sc_conv_w, v_lru_conv_w, v_lru_conv_b, v_lru_b_a, v_lru_b_x,
               v_lru_lambda]
    small_g = [g_norm_g, g_ada_b, g_final_g, g_sc_conv_w, g_lru_conv_w, g_lru_conv_b, g_lru_b_a, g_lru_b_x,
               g_lru_lambda]
    small_g = [g.reshape(w.shape) for g, w in zip(small_g, small_w)]
    shapes = [w.shape for w in small_w]
    packed = _adamw(_pack(small_w), _pack(small_g), _pack(small_m), _pack(small_v), "adamw_small")
    small_out = [small_g] + [_unpack(p.reshape(-1), shapes) for p in packed]

    after = [packed[0], ada_res[0]]
    recvs = []
    for sums, lnd, send, recv, j in sc_w_in_steps:
        recvs += _chip_wait(sums, lnd, send, recv, after, "chip_wait_sc_w_in_%d" % j)
    shp2 = part.shape[2:]
    res = _adamw_reduced(idx_one, sc_w_in.reshape(shp2), m_sc_w_in.reshape(shp2), v_sc_w_in.reshape(shp2),
                         part, got, recvs, "adamw_sc_w_in")
    big_res["sc_w_in"] = [r.reshape(sc_w_in.shape) for r in res]
    big_out = [big_res[nm] for nm in ("sc_w_in", "sc_w_out", "lru_w_in", "lru_w_a", "lru_w_x", "lru_w_out")]

    def small(kind, i):
        return small_out[kind][i]

    def bigw(kind, i):
        return big_out[i][kind]

    outs = [loss, dx0[None]]
    for kind in range(4):
        outs += [small(kind, 0), ada_out[kind], small(kind, 1), bigw(kind, 0), small(kind, 3), bigw(kind, 1),
                 bigw(kind, 2), small(kind, 4), small(kind, 5), bigw(kind, 3), small(kind, 6), bigw(kind, 4),
                 small(kind, 7), small(kind, 8), bigw(kind, 5), small(kind, 2)]
    return tuple(outs)
```

```python
import math

import jax
import jax.numpy as jnp
from jax import lax
from jax.experimental import pallas as pl
from jax.experimental.pallas import tpu as pltpu

N_DEV = 8
LANES = 128
EPS = 1e-6
RGLRU_C = 8.0
ADAM_LR = 0.001
ADAM_B1 = 0.9
ADAM_B2 = 0.999
ADAM_EPS = 1e-08
ADAM_WD = 0.01
ADAM_STEP = 10
VMEM_LIMIT = 56 * 1024 * 1024
MESH = pl.DeviceIdType.MESH
F32 = jnp.float32
BF16 = jnp.bfloat16
ANY = pl.BlockSpec(memory_space=pl.ANY)
HBM = pl.BlockSpec(memory_space=pltpu.HBM)
SEM = pl.BlockSpec(memory_space=pltpu.SEMAPHORE)
VMEM_SPEC = pl.BlockSpec(memory_space=pltpu.VMEM)
EFFECT = pltpu.SideEffectType.DATAFLOW_SIDE_EFFECTING
TOKEN = jax.ShapeDtypeStruct((8, LANES), jnp.float32)


def _tile(n, pref):
    t = min(n, pref)
    assert n % t == 0, (n, pref)
    return t


def _params(*sem):
    return pltpu.CompilerParams(dimension_semantics=sem, vmem_limit_bytes=VMEM_LIMIT)


def _position():
    return lax.axis_index("x"), lax.axis_index("y"), lax.axis_index("c")


def _flip(x, y, k):
    return (1 - x if k & 2 else x), (1 - y if k & 1 else y)


def _after(body, n_in, deps):
    if not deps:
        return body

    def wrapped(*refs):
        return body(*refs[:n_in], *refs[n_in + len(deps):])

    return wrapped


def _small_gather(v, name, deps=()):
    rows = v.shape[0]

    def body(v_ref, out_ref, send_sems, recv_sems):
        x, y, c = _position()
        me = 4 * x + 2 * y + c
        out_ref[me] = v_ref[...]
        copies = []
        for k in range(1, N_DEV):
            px, py = _flip(x, y, k >> 1)
            pc = 1 - c if k & 1 else c
            cp = pltpu.make_async_remote_copy(
                src_ref=v_ref, dst_ref=out_ref.at[me],
                send_sem=send_sems.at[k - 1], recv_sem=recv_sems.at[k - 1],
                device_id=(px, py, pc), device_id_type=MESH)
            cp.start()
            copies.append((cp, 4 * px + 2 * py + pc))
        for k, (cp, peer) in enumerate(copies):
            pltpu.make_async_remote_copy(
                src_ref=v_ref, dst_ref=out_ref.at[peer],
                send_sem=send_sems.at[k], recv_sem=recv_sems.at[k],
                device_id=(x, y, c), device_id_type=MESH).wait_recv()
        for cp, _ in copies:
            cp.wait_send()

    return pl.pallas_call(
        _after(body, 1, deps), name=name,
        out_shape=jax.ShapeDtypeStruct((N_DEV, rows, LANES), F32),
        in_specs=[VMEM_SPEC] + [ANY] * len(deps), out_specs=VMEM_SPEC,
        scratch_shapes=[pltpu.SemaphoreType.DMA((N_DEV - 1,)),
                        pltpu.SemaphoreType.DMA((N_DEV - 1,))],
        compiler_params=pltpu.CompilerParams(vmem_limit_bytes=VMEM_LIMIT),
    )(v, *deps)


def _hbm(a):
    return pltpu.with_memory_space_constraint(a, pltpu.HBM)


def _hbm_like(arrays):
    return [pltpu.HBM(a.shape, a.dtype) for a in arrays]


def _remote(src, dst, send, recv, to):
    return pltpu.make_async_remote_copy(src_ref=src, dst_ref=dst, send_sem=send, recv_sem=recv,
                                        device_id=to, device_id_type=MESH)


def _gather_start(shards, lands, units, after, name):
    n, nu = len(shards), len(units)

    def body(*refs):
        ins, lnd = refs[:n], refs[n:2 * n]
        sems = refs[2 * n + len(after):2 * n + len(after) + 2 * nu]
        token = refs[-1]
        x, y, c = _position()
        me = 4 * x + 2 * y + c
        targets = [(x, y, 1 - c)] + [(px, py, c) for px, py in (_flip(x, y, k) for k in (1, 2, 3))]
        for u, (members, ks) in enumerate(units):
            for slot, i in enumerate(members):
                for ki, k in enumerate(ks):
                    at = len(ks) * slot + ki
                    _remote(ins[i], lnd[i].at[me], sems[2 * u].at[at], sems[2 * u + 1].at[at], targets[k]).start()
        token[...] = jnp.zeros_like(token)

    sem_shapes = []
    for members, ks in units:
        count = len(members) * len(ks)
        sem_shapes += [pltpu.SemaphoreType.DMA((count,)), pltpu.SemaphoreType.DMA((count,))]
    out = pl.pallas_call(
        body, name=name,
        out_shape=sem_shapes + _hbm_like(shards) + _hbm_like(lands) + [TOKEN],
        in_specs=[HBM] * (2 * n) + [ANY] * len(after),
        out_specs=[SEM] * (2 * nu) + [HBM] * (2 * n) + [VMEM_SPEC],
        input_output_aliases={i: 2 * nu + i for i in range(2 * n)},
        compiler_params=pltpu.CompilerParams(has_side_effects=EFFECT),
    )(*[_hbm(s) for s in shards], *[_hbm(l) for l in lands], *after)
    sems = [(out[2 * u], out[2 * u + 1]) for u in range(nu)]
    return sems, list(out[2 * nu:2 * nu + n]), list(out[2 * nu + n:2 * nu + 2 * n]), out[-1]


def _gather_forward(shards, lands, ks, send, recv, after, name):
    m = len(shards)
    hops = [k for k in ks if k]
    nsem = 2 if hops else 0

    def body(*refs):
        ins, lnd = refs[:m], refs[m:2 * m]
        send_ref, recv_ref = refs[2 * m], refs[2 * m + 1]
        outs = refs[2 * m + 2 + len(after):]
        token = refs[-1]
        x, y, c = _position()
        me = (x, y, c)
        for slot in range(m):
            for ki, k in enumerate(ks):
                at = len(ks) * slot + ki
                if k:
                    px, py = _flip(x, y, k)
                    block = lnd[slot].at[4 * px + 2 * py + c]
                else:
                    block = lnd[slot].at[4 * x + 2 * y + (1 - c)]
                arrival = _remote(ins[slot], block, send_ref.at[at], recv_ref.at[at], me)
                arrival.wait_recv()
                if k:
                    fat = len(hops) * slot + hops.index(k)
                    _remote(block, block, outs[0].at[fat], outs[1].at[fat], (x, y, 1 - c)).start()
                arrival.wait_send()
        token[...] = jnp.zeros_like(token)

    count = len(hops) * m
    sem_shapes = [pltpu.SemaphoreType.DMA((count,)), pltpu.SemaphoreType.DMA((count,))] if hops else []
    out = pl.pallas_call(
        body, name=name,
        out_shape=sem_shapes + _hbm_like(shards) + _hbm_like(lands) + [TOKEN],
        in_specs=[HBM] * (2 * m) + [SEM, SEM] + [ANY] * len(after),
        out_specs=[SEM] * nsem + [HBM] * (2 * m) + [VMEM_SPEC],
        input_output_aliases={i: nsem + i for i in range(2 * m)},
        compiler_params=pltpu.CompilerParams(has_side_effects=EFFECT),
    )(*shards, *lands, send, recv, *after)
    fwd = (out[0], out[1]) if hops else None
    return fwd, list(out[nsem:nsem + m]), list(out[nsem + m:nsem + 2 * m]), out[-1]


def _gather_finish(lands, ks, fwd, after, name):
    m = len(lands)
    hops = [k for k in ks if k]

    def body(*refs):
        lnd = refs[:m]
        fsend_ref, frecv_ref = refs[m], refs[m + 1]
        x, y, c = _position()
        for slot in range(m):
            for fi, k in enumerate(hops):
                px, py = _flip(x, y, k)
                sent = lnd[slot].at[4 * px + 2 * py + c]
                came = lnd[slot].at[4 * px + 2 * py + (1 - c)]
                fat = len(hops) * slot + fi
                cp = _remote(sent, came, fsend_ref.at[fat], frecv_ref.at[fat], (x, y, c))
                cp.wait_recv()
                cp.wait_send()

    out = pl.pallas_call(
        body, name=name,
        out_shape=_hbm_like(lands),
        in_specs=[HBM] * m + [SEM, SEM] + [ANY] * len(after), out_specs=[HBM] * m,
        input_output_aliases={i: i for i in range(m)},
        compiler_params=pltpu.CompilerParams(has_side_effects=EFFECT),
    )(*lands, fwd[0], fwd[1], *after)
    return list(out)


def _pair_start(parts, name):
    n = len(parts)
    lands = [lax.empty((p.shape[0], 1) + p.shape[2:], p.dtype) for p in parts]

    def body(*refs):
        ins, lnd = refs[:n], refs[n:2 * n]
        send_ref, recv_ref = refs[2 * n], refs[2 * n + 1]
        token = refs[-1]
        x, y, c = _position()
        for i in range(n):
            _remote(ins[i].at[:, pl.ds(1 - c, 1)], lnd[i], send_ref.at[i], recv_ref.at[i], (x, y, 1 - c)).start()
        token[...] = jnp.zeros_like(token)

    out = pl.pallas_call(
        body, name=name,
        out_shape=[pltpu.SemaphoreType.DMA((n,)), pltpu.SemaphoreType.DMA((n,))]
        + _hbm_like(parts) + _hbm_like(lands) + [TOKEN],
        in_specs=[HBM] * (2 * n), out_specs=[SEM, SEM] + [HBM] * (2 * n) + [VMEM_SPEC],
        input_output_aliases={i: 2 + i for i in range(2 * n)},
        compiler_params=pltpu.CompilerParams(has_side_effects=EFFECT),
    )(*[_hbm(p) for p in parts], *[_hbm(l) for l in lands])
    return out[0], out[1], list(out[2:2 + n]), list(out[2 + n:2 + 2 * n]), out[-1]


def _pair_wait(parts, lands, send, recv, after, name):
    n = len(parts)

    def body(*refs):
        ins, lnd = refs[:n], refs[n:2 * n]
        send_ref, recv_ref = refs[2 * n], refs[2 * n + 1]
        x, y, c = _position()
        for i in range(n):
            cp = _remote(ins[i].at[:, pl.ds(1 - c, 1)], lnd[i], send_ref.at[i], recv_ref.at[i], (x, y, c))
            cp.wait_recv()
            cp.wait_send()

    out = pl.pallas_call(
        body, name=name,
        out_shape=_hbm_like(parts) + _hbm_like(lands),
        in_specs=[HBM] * (2 * n) + [SEM, SEM] + [ANY] * len(after), out_specs=[HBM] * (2 * n),
        input_output_aliases={i: i for i in range(2 * n)},
        compiler_params=pltpu.CompilerParams(has_side_effects=EFFECT),
    )(*parts, *lands, send, recv, *after)
    return list(out[:n]), list(out[n:])


def _chip_start(sums, lands, name, flips=(1, 2, 3)):
    n, ns = len(sums), len(flips)

    def body(*refs):
        ins, lnd = refs[:n], refs[n:2 * n]
        send_ref, recv_ref = refs[2 * n], refs[2 * n + 1]
        token = refs[-1]
        x, y, c = _position()
        for i in range(n):
            for j, flip in enumerate(flips):
                px, py = _flip(x, y, flip)
                _remote(ins[i].at[j], lnd[i].at[j], send_ref.at[ns * i + j], recv_ref.at[ns * i + j],
                        (px, py, c)).start()
        token[...] = jnp.zeros_like(token)

    out = pl.pallas_call(
        body, name=name,
        out_shape=[pltpu.SemaphoreType.DMA((ns * n,)), pltpu.SemaphoreType.DMA((ns * n,))]
        + _hbm_like(sums) + _hbm_like(lands) + [TOKEN],
        in_specs=[HBM] * (2 * n), out_specs=[SEM, SEM] + [HBM] * (2 * n) + [VMEM_SPEC],
        input_output_aliases={i: 2 + i for i in range(2 * n)},
        compiler_params=pltpu.CompilerParams(has_side_effects=EFFECT),
    )(*[_hbm(s) for s in sums], *[_hbm(l) for l in lands])
    return out[0], out[1], out[2:2 + n], out[2 + n:2 + 2 * n], out[-1]


def _chip_wait(sums, lands, send, recv, after, name):
    n, ns = len(sums), sums[0].shape[0]

    def body(*refs):
        ins, lnd = refs[:n], refs[n:2 * n]
        send_ref, recv_ref = refs[2 * n], refs[2 * n + 1]
        x, y, c = _position()
        for i in range(n):
            for j in range(ns):
                cp = _remote(ins[i].at[j], lnd[i].at[j], send_ref.at[ns * i + j], recv_ref.at[ns * i + j], (x, y, c))
                cp.wait_recv()
                cp.wait_send()

    out = pl.pallas_call(
        body, name=name,
        out_shape=_hbm_like(sums) + _hbm_like(lands),
        in_specs=[HBM] * (2 * n) + [SEM, SEM] + [ANY] * len(after), out_specs=[HBM] * (2 * n),
        input_output_aliases={i: i for i in range(2 * n)},
        compiler_params=pltpu.CompilerParams(has_side_effects=EFFECT),
    )(*sums, *lands, send, recv, *after)
    return list(out[n:])


def _pair_sum(idx, part, got, name, nslots=3):
    _, _, rows, cols = part.shape
    tr = _tile(rows, 1024)

    def body(idx_ref, p_ref, q_ref, o_ref):
        o_ref[...] = (p_ref[...].astype(F32) + q_ref[...].astype(F32)).astype(o_ref.dtype)

    grid_spec = pltpu.PrefetchScalarGridSpec(
        num_scalar_prefetch=1, grid=(nslots, rows // tr),
        in_specs=[pl.BlockSpec((None, None, tr, cols), lambda j, r, idx: (idx[j], idx[4], r, 0)),
                  pl.BlockSpec((None, None, tr, cols), lambda j, r, idx: (idx[j], 0, r, 0))],
        out_specs=pl.BlockSpec((None, tr, cols), lambda j, r, idx: (j, r, 0)))
    return pl.pallas_call(
        body, name=name, grid_spec=grid_spec,
        out_shape=jax.ShapeDtypeStruct((nslots, rows, cols), part.dtype),
        compiler_params=_params("arbitrary", "arbitrary"),
    )(idx, part, got)


def _mm_proj(h, wg, groups, name):
    s, k = h.shape
    nchunk, _, n = wg.shape
    e = nchunk * n // groups
    tn = _tile(min(n, e), 512)

    def body(h_ref, w_ref, o_ref):
        o_ref[...] = jnp.dot(h_ref[...], w_ref[...], preferred_element_type=F32).astype(o_ref.dtype)

    return pl.pallas_call(
        body, name=name, grid=(nchunk * n // tn,),
        in_specs=[pl.BlockSpec((s, k), lambda j: (0, 0)),
                  pl.BlockSpec((None, k, tn), lambda j: ((j * tn) // n, 0, ((j * tn) % n) // tn))],
        out_specs=pl.BlockSpec((None, s, tn), lambda j: ((j * tn) // e, 0, ((j * tn) % e) // tn)),
        out_shape=jax.ShapeDtypeStruct((groups, s, e), BF16),
        compiler_params=_params("arbitrary"),
    )(h, wg)


def _mm_proj_group(h, wg, idx, pos, prev, name, deps=()):
    s, k = h.shape
    nchunk, _, n = wg.shape
    groups, _, e = prev.shape
    per = nchunk // groups
    assert per * n == e
    tn = _tile(n, 512)
    nd = len(deps)

    def body(idx_ref, h_ref, w_ref, prev_ref, *rest):
        o_ref = rest[nd]
        o_ref[...] = jnp.dot(h_ref[...], w_ref[...], preferred_element_type=F32).astype(o_ref.dtype)

    grid_spec = pltpu.PrefetchScalarGridSpec(
        num_scalar_prefetch=1, grid=(e // tn,),
        in_specs=[pl.BlockSpec((s, k), lambda j, idx: (0, 0)),
                  pl.BlockSpec((None, k, tn), lambda j, idx: (per * idx[pos] + (j * tn) // n, 0, ((j * tn) % n) // tn)),
                  ANY] + [ANY] * nd,
        out_specs=pl.BlockSpec((None, s, tn), lambda j, idx: (idx[pos], 0, j)))
    return pl.pallas_call(
        body, name=name, grid_spec=grid_spec,
        out_shape=jax.ShapeDtypeStruct(prev.shape, prev.dtype),
        input_output_aliases={3: 0},
        compiler_params=_params("arbitrary"),
    )(idx, h, wg, prev, *deps)


def _mm_out(yb, w, x, gate, name):
    s, k = yb.shape
    d = w.shape[1]
    tn = _tile(d, 512)
    tk = _tile(k, 1024)
    nk = k // tk

    def body(a_ref, w_ref, x_ref, g_ref, xo_ref, y_ref, acc_ref):
        kk = pl.program_id(1)

        @pl.when(kk == 0)
        def _():
            acc_ref[...] = jnp.zeros_like(acc_ref)

        acc_ref[...] += jnp.dot(a_ref[...], w_ref[...], preferred_element_type=F32)

        @pl.when(kk == nk - 1)
        def _():
            y = acc_ref[...]
            y_ref[...] = y.astype(y_ref.dtype)
            xo_ref[...] = x_ref[...] + g_ref[...] * y

    return pl.pallas_call(
        body, name=name, grid=(d // tn, nk),
        in_specs=[pl.BlockSpec((s, tk), lambda j, kk: (0, kk)),
                  pl.BlockSpec((tk, tn), lambda j, kk: (kk, j)),
                  pl.BlockSpec((s, tn), lambda j, kk: (0, j)),
                  pl.BlockSpec((1, tn), lambda j, kk: (0, j))],
        out_specs=[pl.BlockSpec((s, tn), lambda j, kk: (0, j)),
                   pl.BlockSpec((s, tn), lambda j, kk: (0, j))],
        out_shape=[jax.ShapeDtypeStruct((s, d), F32), jax.ShapeDtypeStruct((s, d), BF16)],
        scratch_shapes=[pltpu.VMEM((s, tn), F32)],
        compiler_params=_params("arbitrary", "arbitrary"),
    )(yb, w, x, gate)


def _mm_nt(a3, w3, out_dtype, name, deps=()):
    g, s, ea = a3.shape
    cw, n, nw = w3.shape
    total = g * ea
    assert total == cw * nw
    tk = _tile(min(ea, nw), 1024)
    tn = _tile(n, 1024)
    nk = total // tk

    def body(a_ref, w_ref, o_ref, acc_ref):
        kk = pl.program_id(1)

        @pl.when(kk == 0)
        def _():
            acc_ref[...] = jnp.zeros_like(acc_ref)

        acc_ref[...] += lax.dot_general(a_ref[...], w_ref[...], (((1,), (1,)), ((), ())),
                                        preferred_element_type=F32)

        @pl.when(kk == nk - 1)
        def _():
            o_ref[...] = acc_ref[...].astype(o_ref.dtype)

    return pl.pallas_call(
        _after(body, 2, deps), name=name, grid=(n // tn, nk),
        in_specs=[pl.BlockSpec((None, s, tk), lambda j, kk: ((kk * tk) // ea, 0, ((kk * tk) % ea) // tk)),
                  pl.BlockSpec((None, tn, tk), lambda j, kk: ((kk * tk) // nw, j, ((kk * tk) % nw) // tk))]
        + [ANY] * len(deps),
        out_specs=pl.BlockSpec((s, tn), lambda j, kk: (0, j)),
        out_shape=jax.ShapeDtypeStruct((s, n), out_dtype),
        scratch_shapes=[pltpu.VMEM((s, tn), F32)],
        compiler_params=_params("arbitrary", "arbitrary"),
    )(a3, w3, *deps)


def _mm_tn(a, b3, nchunk, name, deps=()):
    s, ka = a.shape
    g, _, eb = b3.shape
    n = g * eb // nchunk
    tm = _tile(ka, 1024)
    tn = _tile(min(n, eb), 1024)

    def body(a_ref, b_ref, o_ref, at_ref):
        @pl.when(pl.program_id(1) == 0)
        def _():
            at_ref[...] = a_ref[...].astype(F32).T.astype(at_ref.dtype)

        o_ref[...] = jnp.dot(at_ref[...], b_ref[...], preferred_element_type=F32).astype(o_ref.dtype)

    return pl.pallas_call(
        _after(body, 2, deps), name=name, grid=(ka // tm, g * eb // tn),
        in_specs=[pl.BlockSpec((s, tm), lambda i, j: (0, i)),
                  pl.BlockSpec((None, s, tn), lambda i, j: ((j * tn) // eb, 0, ((j * tn) % eb) // tn))]
        + [ANY] * len(deps),
        out_specs=pl.BlockSpec((None, tm, tn), lambda i, j: ((j * tn) // n, i, ((j * tn) % n) // tn)),
        out_shape=jax.ShapeDtypeStruct((nchunk, ka, n), BF16),
        scratch_shapes=[pltpu.VMEM((tm, s), BF16)],
        compiler_params=_params("arbitrary", "arbitrary"),
    )(a, b3, *deps)


def _mm_tn_group(a, b3, idx, pos, nchunk, name, deps=()):
    s, ka = a.shape
    _, _, eb = b3.shape
    n = eb // nchunk
    tm = _tile(ka, 1024)
    tn = _tile(n, 1024)
    nd = len(deps)

    def body(idx_ref, a_ref, b_ref, *rest):
        o_ref, at_ref = rest[nd:]

        @pl.when(pl.program_id(1) == 0)
        def _():
            at_ref[...] = a_ref[...].astype(F32).T.astype(at_ref.dtype)

        o_ref[...] = jnp.dot(at_ref[...], b_ref[...], preferred_element_type=F32).astype(o_ref.dtype)

    grid_spec = pltpu.PrefetchScalarGridSpec(
        num_scalar_prefetch=1, grid=(ka // tm, eb // tn),
        in_specs=[pl.BlockSpec((s, tm), lambda i, j, idx: (0, i)),
                  pl.BlockSpec((None, s, tn), lambda i, j, idx: (idx[pos], 0, j))] + [ANY] * nd,
        out_specs=pl.BlockSpec((None, tm, tn), lambda i, j, idx: ((j * tn) // n, i, ((j * tn) % n) // tn)),
        scratch_shapes=[pltpu.VMEM((tm, s), BF16)])
    return pl.pallas_call(
        body, name=name, grid_spec=grid_spec,
        out_shape=jax.ShapeDtypeStruct((nchunk, ka, n), BF16),
        compiler_params=_params("arbitrary", "arbitrary"),
    )(idx, a, b3, *deps)


def _sigmoid(z):
    return jax.nn.sigmoid(z)


def _shift_down(v, k, fill=0.0, period=None):
    if k == 0:
        return v
    row = lax.broadcasted_iota(jnp.int32, v.shape, 0)
    if period is not None:
        row = row & (period - 1)
    return jnp.where(row >= k, pltpu.roll(v, k, 0), fill)


def _shift_up(v, k, fill=0.0, period=None):
    if k == 0:
        return v
    s = v.shape[0]
    row = lax.broadcasted_iota(jnp.int32, v.shape, 0)
    if period is not None:
        row, s = row & (period - 1), period
    return jnp.where(row < s - k, pltpu.roll(v, v.shape[0] - k, 0), fill)


SCAN_BLOCK = 64


def _scan(a, b, shift):
    s = a.shape[0]
    blk = min(SCAN_BLOCK, s)
    k = 1
    while k < blk:
        b = a * shift(b, k, 0.0, blk) + b
        a = a * shift(a, k, 1.0, blk)
        k *= 2
    nblk = s // blk
    forward = shift is _shift_down
    order = range(nblk) if forward else range(nblk - 1, -1, -1)
    edge = blk - 1 if forward else 0
    out = [None] * nblk
    carry = None
    for i in order:
        h = b[i * blk:(i + 1) * blk]
        if carry is not None:
            h = a[i * blk:(i + 1) * blk] * carry + h
        carry = h[edge:edge + 1]
        out[i] = h
    return jnp.concatenate(out, axis=0) if nblk > 1 else out[0]


def _norm_mod(x, g, scale, shift, name, deps=()):
    s, d = x.shape
    ts = _tile(s, 256)

    def body(x_ref, g_ref, sc_ref, sh_ref, h_ref):
        xv = x_ref[...]
        rstd = lax.rsqrt(jnp.mean(xv * xv, axis=-1, keepdims=True) + EPS)
        nrm = xv * rstd * g_ref[...]
        h_ref[...] = (nrm * (1.0 + sc_ref[...]) + sh_ref[...]).astype(h_ref.dtype)

    vec = pl.BlockSpec((1, d), lambda i: (0, 0))
    return pl.pallas_call(
        _after(body, 4, deps), name=name, grid=(s // ts,),
        in_specs=[pl.BlockSpec((ts, d), lambda i: (i, 0)), vec, vec, vec] + [ANY] * len(deps),
        out_specs=pl.BlockSpec((ts, d), lambda i: (i, 0)),
        out_shape=jax.ShapeDtypeStruct((s, d), BF16),
        compiler_params=_params("arbitrary"),
    )(x, g, scale, shift, *deps)


def _gate_terms(dx, y_ref, gate_ref, dy_ref, dgate_ref):
    dy_ref[...] = (dx * gate_ref[...]).astype(dy_ref.dtype)
    dgate_ref[...] += jnp.sum(dx * y_ref[...].astype(F32), axis=0, keepdims=True)


def _norm_mod_bwd(x, dh, dx_res, g, scale, name, below=None, deps=()):
    s, d = x.shape
    ts = _tile(s, 256)
    nb = 2 if below is not None else 0

    def body(x_ref, dh_ref, dr_ref, g_ref, sc_ref, *rest):
        dx_ref, dsc_ref, dsh_ref, dg_ref = rest[nb:nb + 4]

        @pl.when(pl.program_id(0) == 0)
        def _():
            for ref in rest[nb + 1:nb + 4] + rest[nb + 5:]:
                ref[...] = jnp.zeros_like(ref)

        xv = x_ref[...]
        dh_v = dh_ref[...].astype(F32)
        gv = g_ref[...]
        rstd = lax.rsqrt(jnp.mean(xv * xv, axis=-1, keepdims=True) + EPS)
        xhat = xv * rstd
        dsc_ref[...] += jnp.sum(dh_v * xhat * gv, axis=0, keepdims=True)
        dsh_ref[...] += jnp.sum(dh_v, axis=0, keepdims=True)
        dn = dh_v * (1.0 + sc_ref[...])
        dg_ref[...] += jnp.sum(dn * xhat, axis=0, keepdims=True)
        dxhat = dn * gv
        proj = jnp.mean(dxhat * xhat, axis=-1, keepdims=True)
        dx = dr_ref[...] + rstd * (dxhat - xhat * proj)
        dx_ref[...] = dx
        if nb:
            _gate_terms(dx, rest[0], rest[1], rest[nb + 4], rest[nb + 5])

    row = pl.BlockSpec((ts, d), lambda i: (i, 0))
    vec = pl.BlockSpec((1, d), lambda i: (0, 0))
    extra = list(below) if nb else []
    return pl.pallas_call(
        _after(body, 5 + nb, deps), name=name, grid=(s // ts,),
        in_specs=[row, row, row, vec, vec] + [row, vec][:nb] + [ANY] * len(deps),
        out_specs=[row, vec, vec, vec] + [row, vec][:nb],
        out_shape=[jax.ShapeDtypeStruct((s, d), F32)] + [jax.ShapeDtypeStruct((1, d), F32)] * 3
        + [jax.ShapeDtypeStruct((s, d), BF16), jax.ShapeDtypeStruct((1, d), F32)][:nb],
        compiler_params=_params("arbitrary"),
    )(x, dh, dx_res, g, scale, *extra, *deps)


def _final_loss(x, g, target, y, gate, name):
    s, d = x.shape
    ts = _tile(s, 256)

    def body(x_ref, g_ref, t_ref, y_ref, gate_ref, dx_ref, loss_ref, dg_ref, dy_ref, dgate_ref):
        @pl.when(pl.program_id(0) == 0)
        def _():
            loss_ref[...] = jnp.zeros_like(loss_ref)
            dg_ref[...] = jnp.zeros_like(dg_ref)
            dgate_ref[...] = jnp.zeros_like(dgate_ref)

        xv = x_ref[...]
        gv = g_ref[...]
        rstd = lax.rsqrt(jnp.mean(xv * xv, axis=-1, keepdims=True) + EPS)
        xhat = xv * rstd
        err = xhat * gv - t_ref[...]
        loss_ref[...] += 0.5 * jnp.sum(jnp.mean(err * err, axis=-1, keepdims=True))
        dy = err * (1.0 / d)
        dg_ref[...] += jnp.sum(dy * xhat, axis=0, keepdims=True)
        dxhat = dy * gv
        proj = jnp.mean(dxhat * xhat, axis=-1, keepdims=True)
        dx = rstd * (dxhat - xhat * proj)
        dx_ref[...] = dx
        _gate_terms(dx, y_ref, gate_ref, dy_ref, dgate_ref)

    row = pl.BlockSpec((ts, d), lambda i: (i, 0))
    vec = pl.BlockSpec((1, d), lambda i: (0, 0))
    return pl.pallas_call(
        body, name=name, grid=(s // ts,),
        in_specs=[row, vec, row, row, vec],
        out_specs=[row, pl.BlockSpec((1, LANES), lambda i: (0, 0)), vec, row, vec],
        out_shape=[jax.ShapeDtypeStruct((s, d), F32), jax.ShapeDtypeStruct((1, LANES), F32),
                   jax.ShapeDtypeStruct((1, d), F32), jax.ShapeDtypeStruct((s, d), BF16),
                   jax.ShapeDtypeStruct((1, d), F32)],
        compiler_params=_params("arbitrary"),
    )(x, g, target, y, gate)


def _conv(v, w_ref, width):
    out = w_ref[width - 1:width, :] * v
    for k in range(width - 1):
        out = out + w_ref[k:k + 1, :] * _shift_down(v, width - 1 - k)
    return out


def _sc_fwd(proj, conv_w, name, deps=()):
    _, s, e = proj.shape
    te = _tile(e, 256)
    width = conv_w.shape[0]

    def body(b_ref, c_ref, v_ref, g_ref, w_ref, o_ref):
        cv = c_ref[...].astype(F32) * v_ref[...].astype(F32)
        u = _conv(cv, w_ref, width)
        gv = g_ref[...].astype(F32)
        o_ref[...] = (b_ref[...].astype(F32) * u * (gv * _sigmoid(gv))).astype(o_ref.dtype)

    def part(q):
        return pl.BlockSpec((None, s, te), lambda j, q=q: (q, 0, j))

    return pl.pallas_call(
        _after(body, 5, deps), name=name, grid=(e // te,),
        in_specs=[part(0), part(1), part(2), part(3), pl.BlockSpec((width, te), lambda j: (0, j))]
        + [ANY] * len(deps),
        out_specs=pl.BlockSpec((s, te), lambda j: (0, j)),
        out_shape=jax.ShapeDtypeStruct((s, e), BF16),
        compiler_params=_params("arbitrary"),
    )(proj, proj, proj, proj, conv_w, *deps)


def _sc_bwd(proj, dyb, conv_w, name, deps=()):
    _, s, e = proj.shape
    te = _tile(e, 256)
    width = conv_w.shape[0]

    def body(b_ref, c_ref, v_ref, g_ref, dy_ref, w_ref, dp_ref, vec_ref):
        bv = b_ref[...].astype(F32)
        cvl = c_ref[...].astype(F32)
        vv = v_ref[...].astype(F32)
        gv = g_ref[...].astype(F32)
        dyv = dy_ref[...].astype(F32)
        cv = cvl * vv
        u = _conv(cv, w_ref, width)
        sg = _sigmoid(gv)
        silu = gv * sg
        dp_ref[0] = (dyv * u * silu).astype(dp_ref.dtype)
        du = dyv * bv * silu
        dp_ref[3] = (dyv * bv * u * (sg * (1.0 + gv * (1.0 - sg)))).astype(dp_ref.dtype)
        dcv = w_ref[width - 1:width, :] * du
        vec_ref[...] = jnp.zeros_like(vec_ref)
        vec_ref[width - 1:width, :] = jnp.sum(du * cv, axis=0, keepdims=True)
        for k in range(width - 1):
            sh = width - 1 - k
            dcv = dcv + w_ref[k:k + 1, :] * _shift_up(du, sh)
            vec_ref[k:k + 1, :] = jnp.sum(du * _shift_down(cv, sh), axis=0, keepdims=True)
        dp_ref[1] = (dcv * vv).astype(dp_ref.dtype)
        dp_ref[2] = (dcv * cvl).astype(dp_ref.dtype)

    def part(q):
        return pl.BlockSpec((None, s, te), lambda j, q=q: (q, 0, j))

    return pl.pallas_call(
        _after(body, 6, deps), name=name, grid=(e // te,),
        in_specs=[part(0), part(1), part(2), part(3), pl.BlockSpec((s, te), lambda j: (0, j)),
                  pl.BlockSpec((width, te), lambda j: (0, j))] + [ANY] * len(deps),
        out_specs=[pl.BlockSpec((4, s, te), lambda j: (0, 0, j)),
                   pl.BlockSpec((8, te), lambda j: (0, j))],
        out_shape=[jax.ShapeDtypeStruct((4, s, e), BF16), jax.ShapeDtypeStruct((8, e), F32)],
        compiler_params=_params("arbitrary"),
    )(proj, proj, proj, proj, dyb, conv_w, *deps)


def _lru_gates(v_pre, w_ref, cb_ref, wa_ref, ba_ref, wx_ref, bx_ref, lam_ref, width):
    v = _conv(v_pre, w_ref, width) + cb_ref[...]
    vb = v.astype(BF16)
    r = _sigmoid(jnp.dot(vb, wa_ref[...], preferred_element_type=F32) + ba_ref[...])
    i = _sigmoid(jnp.dot(vb, wx_ref[...], preferred_element_type=F32) + bx_ref[...])
    nl = -lam_ref[...]
    sp = jnp.maximum(nl, 0.0) + jnp.log1p(jnp.exp(-jnp.abs(nl)))
    log_a = (-RGLRU_C) * r * sp
    a = jnp.exp(log_a)
    one_minus_a2 = jnp.tanh(-log_a) * (1.0 + a * a)
    mult = jnp.sqrt(one_minus_a2)
    return v, vb, r, i, sp, a, mult


def _lru_specs(s, dh, heads, width):
    head_col = lambda q: pl.BlockSpec((None, s, dh), lambda h, q=q: (q, 0, h))
    vec = pl.BlockSpec((1, dh), lambda h: (0, h))
    mat = pl.BlockSpec((None, dh, dh), lambda h: (h, 0, 0))
    weights = [pl.BlockSpec((width, dh), lambda h: (0, h)), vec, mat, vec, mat, vec, vec]
    return head_col, weights


def _lru_fwd(proj, conv_w, conv_b, w_a, b_a, w_x, b_x, lam, name, deps=()):
    _, s, e = proj.shape
    heads, dh, _ = w_a.shape
    width = conv_w.shape[0]

    def body(v_ref, g_ref, w_ref, cb_ref, wa_ref, ba_ref, wx_ref, bx_ref, lam_ref, yb_ref, keep_ref):
        v, _, r, i, _, a, mult = _lru_gates(v_ref[...].astype(F32), w_ref, cb_ref, wa_ref, ba_ref,
                                           wx_ref, bx_ref, lam_ref, width)
        hs = _scan(a, mult * i * v, _shift_down)
        for k, val in enumerate((hs, v, r, i, a, mult)):
            keep_ref[k] = val
        gv = g_ref[...].astype(F32)
        yb_ref[...] = (hs * (gv * _sigmoid(gv))).astype(yb_ref.dtype)

    head_col, weights = _lru_specs(s, dh, heads, width)
    return pl.pallas_call(
        _after(body, 9, deps), name=name, grid=(heads,),
        in_specs=[head_col(0), head_col(1)] + weights + [ANY] * len(deps),
        out_specs=[pl.BlockSpec((s, dh), lambda h: (0, h)), pl.BlockSpec((6, s, dh), lambda h: (0, 0, h))],
        out_shape=[jax.ShapeDtypeStruct((s, e), BF16), jax.ShapeDtypeStruct((6, s, e), F32)],
        compiler_params=_params("arbitrary"),
    )(proj, proj, conv_w, conv_b, w_a, b_a, w_x, b_x, lam, *deps)


def _lru_bwd(proj, keep, dyb, conv_w, conv_b, w_a, b_a, w_x, b_x, lam, name, deps=()):
    _, s, e = proj.shape
    heads, dh, _ = w_a.shape
    width = conv_w.shape[0]

    def body(v_ref, g_ref, hs_ref, dy_ref, w_ref, cb_ref, wa_ref, ba_ref, wx_ref, bx_ref, lam_ref,
             dp_ref, dwa_ref, dwx_ref, vec_ref):
        v_pre = v_ref[...].astype(F32)
        hs, v, r, i, a, mult = (hs_ref[k] for k in range(6))
        vb = v.astype(BF16)
        nl = -lam_ref[...]
        sp = jnp.maximum(nl, 0.0) + jnp.log1p(jnp.exp(-jnp.abs(nl)))
        gv = g_ref[...].astype(F32)
        dyv = dy_ref[...].astype(F32)
        sg = _sigmoid(gv)
        dp_ref[1] = (dyv * hs * (sg * (1.0 + gv * (1.0 - sg)))).astype(dp_ref.dtype)
        dhs = dyv * (gv * sg)
        d_h = _scan(_shift_up(a, 1), dhs, _shift_up)
        da = d_h * _shift_down(hs, 1)
        iv = i * v
        dlog_a = da * a - (d_h * iv) * (a * a) / mult
        di = d_h * mult * v
        dv = d_h * mult * i
        dzr = dlog_a * (-RGLRU_C) * sp * r * (1.0 - r)
        dzi = di * i * (1.0 - i)
        dsp = jnp.sum(dlog_a * r, axis=0, keepdims=True) * (-RGLRU_C)
        vec_ref[...] = jnp.zeros_like(vec_ref)
        vec_ref[0:1, :] = jnp.sum(dzr, axis=0, keepdims=True)
        vec_ref[1:2, :] = jnp.sum(dzi, axis=0, keepdims=True)
        vec_ref[2:3, :] = -dsp * _sigmoid(-lam_ref[...])
        dzr_b = dzr.astype(BF16)
        dzi_b = dzi.astype(BF16)
        vt = vb.astype(F32).T.astype(BF16)
        dwa_ref[...] = jnp.dot(vt, dzr_b, preferred_element_type=F32).astype(dwa_ref.dtype)
        dwx_ref[...] = jnp.dot(vt, dzi_b, preferred_element_type=F32).astype(dwx_ref.dtype)
        nt = (((1,), (1,)), ((), ()))
        dv = dv + lax.dot_general(dzr_b, wa_ref[...], nt, preferred_element_type=F32)
        dv = dv + lax.dot_general(dzi_b, wx_ref[...], nt, preferred_element_type=F32)
        vec_ref[3:4, :] = jnp.sum(dv, axis=0, keepdims=True)
        dvp = w_ref[width - 1:width, :] * dv
        vec_ref[4 + width - 1:4 + width, :] = jnp.sum(dv * v_pre, axis=0, keepdims=True)
        for k in range(width - 1):
            sh = width - 1 - k
            dvp = dvp + w_ref[k:k + 1, :] * _shift_up(dv, sh)
            vec_ref[4 + k:5 + k, :] = jnp.sum(dv * _shift_down(v_pre, sh), axis=0, keepdims=True)
        dp_ref[0] = dvp.astype(dp_ref.dtype)

    head_col, weights = _lru_specs(s, dh, heads, width)
    col = pl.BlockSpec((s, dh), lambda h: (0, h))
    mat = pl.BlockSpec((None, dh, dh), lambda h: (h, 0, 0))
    return pl.pallas_call(
        _after(body, 11, deps), name=name, grid=(heads,),
        in_specs=[head_col(0), head_col(1), pl.BlockSpec((6, s, dh), lambda h: (0, 0, h)), col] + weights
        + [ANY] * len(deps),
        out_specs=[pl.BlockSpec((2, s, dh), lambda h: (0, 0, h)), mat, mat,
                   pl.BlockSpec((16, dh), lambda h: (0, h))],
        out_shape=[jax.ShapeDtypeStruct((2, s, e), BF16),
                   jax.ShapeDtypeStruct((heads, dh, dh), BF16),
                   jax.ShapeDtypeStruct((heads, dh, dh), BF16),
                   jax.ShapeDtypeStruct((16, e), F32)],
        compiler_params=_params("arbitrary"),
    )(proj, proj, keep, dyb, conv_w, conv_b, w_a, b_a, w_x, b_x, lam, *deps)


def _ada_mod(c_all, w, b, name):
    layers, d, f = w.shape
    nb = c_all.shape[0]

    def body(c_ref, w_ref, b_ref, o_ref):
        cv = c_ref[...]
        sc = cv * _sigmoid(cv)
        o_ref[...] = jnp.dot(sc, w_ref[...], preferred_element_type=F32,
                             precision=lax.Precision.HIGHEST) + b_ref[...]

    return pl.pallas_call(
        body, name=name, grid=(layers,),
        in_specs=[pl.BlockSpec((nb, d), lambda l: (0, 0)),
                  pl.BlockSpec((None, d, f), lambda l: (l, 0, 0)),
                  pl.BlockSpec((None, 1, f), lambda l: (l, 0, 0))],
        out_specs=pl.BlockSpec((None, nb, f), lambda l: (l, 0, 0)),
        out_shape=jax.ShapeDtypeStruct((layers, nb, f), F32),
        compiler_params=_params("arbitrary"),
    )(c_all, w, b)


def _ada_grad(c_all_t, dmod, name):
    d, nb = c_all_t.shape
    layers, _, f = dmod.shape

    def body(c_ref, dm_ref, o_ref):
        cv = c_ref[...]
        sc = cv * _sigmoid(cv)
        acc = sc[:, 0:1] * dm_ref[0:1, :]
        for k in range(1, nb):
            acc = acc + sc[:, k:k + 1] * dm_ref[k:k + 1, :]
        o_ref[...] = acc

    return pl.pallas_call(
        body, name=name, grid=(layers,),
        in_specs=[pl.BlockSpec((d, nb), lambda l: (0, 0)),
                  pl.BlockSpec((None, nb, f), lambda l: (l, 0, 0))],
        out_specs=pl.BlockSpec((None, d, f), lambda l: (l, 0, 0)),
        out_shape=jax.ShapeDtypeStruct((layers, d, f), F32),
        compiler_params=_params("arbitrary"),
    )(c_all_t, dmod)


def _device_sum(g, name):
    _, rows, _ = g.shape

    def body(g_ref, o_ref):
        acc = g_ref[0]
        for k in range(1, N_DEV):
            acc = acc + g_ref[k]
        o_ref[...] = acc

    return pl.pallas_call(
        body, name=name,
        in_specs=[VMEM_SPEC], out_specs=VMEM_SPEC,
        out_shape=jax.ShapeDtypeStruct((rows, LANES), F32),
        compiler_params=pltpu.CompilerParams(vmem_limit_bytes=VMEM_LIMIT),
    )(g)


def _adamw_math(w, g, m, v):
    m = ADAM_B1 * m + (1.0 - ADAM_B1) * g
    v = ADAM_B2 * v + (1.0 - ADAM_B2) * (g * g)
    m_hat = m / (1.0 - ADAM_B1 ** ADAM_STEP)
    v_hat = v / (1.0 - ADAM_B2 ** ADAM_STEP)
    delta = -ADAM_LR * (m_hat / (jnp.sqrt(v_hat) + ADAM_EPS) + ADAM_WD * w)
    return delta, m, v


def _adamw(w, g, m, v, name):
    rows, cols = w.shape
    tr = _tile(rows, 256)

    def body(w_ref, g_ref, m_ref, v_ref, d_ref, mo_ref, vo_ref):
        d_ref[...], mo_ref[...], vo_ref[...] = _adamw_math(w_ref[...], g_ref[...], m_ref[...], v_ref[...])

    blk = pl.BlockSpec((tr, cols), lambda i: (i, 0))
    return pl.pallas_call(
        body, name=name, grid=(rows // tr,),
        in_specs=[blk] * 4, out_specs=[blk] * 3,
        out_shape=[jax.ShapeDtypeStruct((rows, cols), F32)] * 3,
        compiler_params=_params("arbitrary"),
    )(w, g, m, v)


def _adamw_reduced(idx, w, m, v, part, got, recvs, name):
    rows, cols = w.shape
    tr = _tile(rows, 256)
    nr = len(recvs)

    def body(idx_ref, w_ref, m_ref, v_ref, p_ref, q_ref, *rest):
        g_ref, d_ref, mo_ref, vo_ref = rest[nr:]
        g = p_ref[...].astype(F32) + q_ref[...].astype(F32)
        for u_ref in rest[:nr]:
            for j in range(u_ref.shape[0]):
                g = g + u_ref[j].astype(F32)
        g_ref[...] = g
        d_ref[...], mo_ref[...], vo_ref[...] = _adamw_math(w_ref[...], g, m_ref[...], v_ref[...])

    blk = pl.BlockSpec((tr, cols), lambda i, idx: (i, 0))
    grid_spec = pltpu.PrefetchScalarGridSpec(
        num_scalar_prefetch=1, grid=(rows // tr,),
        in_specs=[blk, blk, blk,
                  pl.BlockSpec((None, None, tr, cols), lambda i, idx: (idx[3], idx[4], i, 0)),
                  pl.BlockSpec((None, None, tr, cols), lambda i, idx: (idx[3], 0, i, 0))]
        + [pl.BlockSpec((u.shape[0], tr, cols), lambda i, idx: (0, i, 0)) for u in recvs],
        out_specs=[blk] * 4)
    return pl.pallas_call(
        body, name=name, grid_spec=grid_spec,
        out_shape=[jax.ShapeDtypeStruct((rows, cols), F32)] * 4,
        compiler_params=_params("arbitrary"),
    )(idx, w, m, v, part, got, *recvs)


def _pack(vectors):
    flat = jnp.concatenate([v.reshape(-1).astype(F32) for v in vectors])
    pad = (-flat.shape[0]) % (8 * LANES)
    return jnp.pad(flat, (0, pad)).reshape(-1, LANES)


def _unpack(flat, shapes):
    out, off = [], 0
    for shp in shapes:
        size = math.prod(shp)
        out.append(flat[..., off:off + size].reshape(flat.shape[:-1] + tuple(shp)))
        off += size
    return out


def _my_slice(full, me, axis):
    size = full.shape[axis] // N_DEV
    return lax.dynamic_slice_in_dim(full, me * size, size, axis)


def kernel(x, c, norm_g, ada_w, ada_b, sc_w_in, sc_conv_w, sc_w_out, lru_w_in, lru_conv_w, lru_conv_b, lru_w_a, lru_b_a, lru_w_x, lru_b_x, lru_lambda, lru_w_out, final_g, loss_target, m_norm_g, m_ada_w, m_ada_b, m_sc_w_in, m_sc_conv_w, m_sc_w_out, m_lru_w_in, m_lru_conv_w, m_lru_conv_b, m_lru_w_a, m_lru_b_a, m_lru_w_x, m_lru_b_x, m_lru_lambda, m_lru_w_out, m_final_g, v_norm_g, v_ada_w, v_ada_b, v_sc_w_in, v_sc_conv_w, v_sc_w_out, v_lru_w_in, v_lru_conv_w, v_lru_conv_b, v_lru_w_a, v_lru_b_a, v_lru_w_x, v_lru_b_x, v_lru_lambda, v_lru_w_out, v_final_g):
    _, s, d = x.shape
    e = sc_w_out.shape[1] * N_DEV
    heads, dh_s, dh = lru_w_a.shape[1:]
    es = e // N_DEV
    f = ada_w.shape[2]
    mx, my, mc = _position()
    me = 4 * mx + 2 * my + mc
    chip = 2 * mx + my
    idx = jnp.stack([chip ^ 1, chip ^ 2, chip ^ 3, chip, mc]).astype(jnp.int32)

    x0 = x[0]
    target = loss_target[0]

    small_shapes = [(d,), (3, es), (4, es), (es,), (heads, dh_s), (heads, dh_s), (es,)]
    small = _small_gather(_pack([c, sc_conv_w, lru_conv_w, lru_conv_b, lru_b_a, lru_b_x, lru_lambda]),
                          "gather_small_weights").reshape(N_DEV, -1)
    c_all, cw3, cw4, cb, ba, bx, lam = _unpack(small, small_shapes)
    cw3 = cw3.transpose(1, 0, 2).reshape(3, e)
    cw4 = cw4.transpose(1, 0, 2).reshape(4, e)
    cb = cb.reshape(1, e)
    lam = lam.reshape(1, e)
    ba = ba.transpose(1, 0, 2).reshape(1, e)
    bx = bx.transpose(1, 0, 2).reshape(1, e)

    shards = [sc_w_in[0].astype(BF16), sc_w_out[0].astype(BF16), lru_w_in[0].astype(BF16),
              lru_w_a[0].reshape(heads * dh_s, dh).astype(BF16),
              lru_w_x[0].reshape(heads * dh_s, dh).astype(BF16), lru_w_out[0].astype(BF16)]
    lands = [lax.dynamic_update_slice(lax.empty((N_DEV,) + sh.shape, BF16), sh[None], (me, 0, 0))
             for sh in shards]
    every = [1, 2, 3, 0]
    units = [([0], [0]), ([0], [1]), ([0], [2]), ([0], [3]), ([1], every), ([2], every), ([3, 4], every),
             ([5], every)]
    sems, first_sh, first_ld, started = _gather_start(shards[:1], lands[:1], units[:3], [small],
                                                      "gather_start_first")
    shards, lands = first_sh + shards[1:], first_ld + lands[1:]

    ada_b_mine = _my_slice(ada_b, me, 1).reshape(2, 1, f)
    mod_mine = _ada_mod(c_all, ada_w, ada_b_mine, "ada_mod")
    mod_all = _small_gather(_pack([mod_mine]), "gather_mod", deps=[started])

    def start_later(after):
        far_sems, far_sh, far_ld, tok = _gather_start(shards[:1], lands[:1], units[3:4], after, "gather_start_far")
        rest_units = [([i - 1 for i in members], ks) for members, ks in units[4:]]
        rest_sems, rest_sh, rest_ld, tok = _gather_start(shards[1:], lands[1:], rest_units, [tok],
                                                         "gather_start_rest")
        sems.extend(far_sems + rest_sems)
        shards[:], lands[:] = far_sh + rest_sh, far_ld + rest_ld
        return tok

    def gathered(u, after_forward, name):
        members, ks = units[u]
        fwd, shs, lnd, token = _gather_forward(
            [shards[i] for i in members], [lands[i] for i in members], ks, sems[u][0], sems[u][1],
            after_forward, "gather_forward_" + name)
        for i, sh, ld in zip(members, shs, lnd):
            shards[i], lands[i] = sh, ld

        def finish(after):
            out = _gather_finish([lands[i] for i in members], ks, fwd, after, "gather_finish_" + name)
            for i, ld in zip(members, out):
                lands[i] = ld
            return out

        return token, finish

    tok, finish_y = gathered(1, [mod_all], "sc_w_in_near_y")
    tok, finish_x = gathered(2, [tok], "sc_w_in_near_x")
    queued = start_later([tok])

    mod_all = mod_all.reshape(N_DEV, -1)
    mod_all = mod_all[:, :2 * N_DEV * f].reshape(N_DEV, 2, N_DEV, f)
    mod_all = mod_all.transpose(1, 2, 0, 3).reshape(2, N_DEV, 3 * d)
    mod = lax.dynamic_index_in_dim(mod_all, me, 1, keepdims=False)
    shift = [mod[l:l + 1, 0:d] for l in range(2)]
    scale = [mod[l:l + 1, d:2 * d] for l in range(2)]
    gate = [mod[l:l + 1, 2 * d:3 * d] for l in range(2)]
    ng = [norm_g[l:l + 1] for l in range(2)]
    fg = final_g.reshape(1, d)

    h0 = _norm_mod(x0, ng[0], scale[0], shift[0], "norm_mod_0", deps=[queued])
    proj0 = lax.empty((4, s, e), BF16)
    tok, _ = gathered(0, [h0], "sc_w_in_own")
    proj0 = _mm_proj_group(h0, lands[0], idx, 3, proj0, "mm_proj_0_own", deps=[tok])
    for u, name, finish in ((1, "near_y", finish_y), (2, "near_x", finish_x), (3, "far", None)):
        after = [proj0]
        if finish is None:
            tok, finish = gathered(u, [proj0], "sc_w_in_" + name)
            after = [tok]
        wg_in0, = finish(after)
        proj0 = _mm_proj_group(h0, wg_in0, idx, u - 1, proj0, "mm_proj_0_" + name)
    tok, finish = gathered(4, [proj0], "sc_w_out")
    yb0 = _sc_fwd(proj0, cw3, "sc_fwd", deps=[tok])
    w_out0 = finish([yb0])[0].reshape(e, d)
    x1, y0 = _mm_out(yb0, w_out0, x0, gate[0], "mm_out_0")
    tok, finish = gathered(5, [x1], "lru_w_in")
    h1 = _norm_mod(x1, ng[1], scale[1], shift[1], "norm_mod_1", deps=[tok])
    wg_in1, = finish([h1])
    proj1 = _mm_proj(h1, wg_in1, 2, "mm_proj_1")
    tok, finish = gathered(6, [proj1], "lru_gates")
    wg_a, wg_x = finish([tok])
    w_a = wg_a.reshape(N_DEV, heads, dh_s, dh).transpose(1, 0, 2, 3).reshape(heads, dh, dh)
    w_x = wg_x.reshape(N_DEV, heads, dh_s, dh).transpose(1, 0, 2, 3).reshape(heads, dh, dh)
    tok, finish = gathered(7, [w_a, w_x], "lru_w_out")
    yb1, hs = _lru_fwd(proj1, cw4, cb, w_a, ba, w_x, bx, lam, "lru_fwd", deps=[tok])
    w_out1 = finish([yb1])[0].reshape(e, d)
    x2, y1 = _mm_out(yb1, w_out1, x1, gate[1], "mm_out_1")
    dx2, loss_part, d_fg, dy1, dgate1 = _final_loss(x2, fg, target, y1, gate[1], "final_loss")

    def pieces(g, rows, cols):
        return g.reshape(4, 2, rows, cols)

    def by_rows(g):
        return g.reshape(heads, N_DEV, dh_s, dh).transpose(1, 0, 2, 3).reshape(N_DEV, heads * dh_s, dh)

    def pair_begin(parts, group):
        send, recv, parts, lnd, token = _pair_start(parts, "pair_start_" + group)
        return dict(parts=parts, lands=lnd, send=send, recv=recv, group=group), token

    def scatter_start(pair, names, after):
        group = pair["group"]
        parts, gots = _pair_wait(pair["parts"], pair["lands"], pair["send"], pair["recv"], after,
                                 "pair_wait_" + group)
        sums = [_pair_sum(idx, p, q, "pair_sum_" + nm) for p, q, nm in zip(parts, gots, names)]
        empties = [lax.empty(sm.shape, sm.dtype) for sm in sums]
        send, recv, sums, lnd, token = _chip_start(sums, empties, "chip_start_" + group)
        return dict(parts=parts, gots=gots, names=names, group=group, sums=sums, lands=lnd,
                    send=send, recv=recv), token

    big = {"sc_w_in": (sc_w_in, m_sc_w_in, v_sc_w_in), "sc_w_out": (sc_w_out, m_sc_w_out, v_sc_w_out),
           "lru_w_in": (lru_w_in, m_lru_w_in, v_lru_w_in), "lru_w_a": (lru_w_a, m_lru_w_a, v_lru_w_a),
           "lru_w_x": (lru_w_x, m_lru_w_x, v_lru_w_x), "lru_w_out": (lru_w_out, m_lru_w_out, v_lru_w_out)}
    big_res = {}

    def scatter_finish(rs, after):
        recvs = _chip_wait(rs["sums"], rs["lands"], rs["send"], rs["recv"], after, "chip_wait_" + rs["group"])
        done = []
        for p, q, u, nm in zip(rs["parts"], rs["gots"], recvs, rs["names"]):
            w, m, v = big[nm]
            shp2 = p.shape[2:]
            res = _adamw_reduced(idx, w.reshape(shp2), m.reshape(shp2), v.reshape(shp2), p, q, [u], "adamw_" + nm)
            big_res[nm] = [r.reshape(w.shape) for r in res]
            done.append(res[1])
        return done

    dw_out1 = _mm_tn(yb1, dy1[None], 1, "mm_dw_out_1")
    pair, tok = pair_begin([pieces(dw_out1, es, d)], "lru_w_out")
    dyb1 = _mm_nt(dy1[None], w_out1[None], BF16, "mm_dyb_1", deps=[tok])
    rs1, tok = scatter_start(pair, ["lru_w_out"], [dyb1])
    dproj1, dw_a, dw_x, vecs1 = _lru_bwd(proj1, hs, dyb1, cw4, cb, w_a, ba, w_x, bx, lam, "lru_bwd", deps=[tok])
    done = scatter_finish(rs1, [dproj1])
    dw_in1 = _mm_tn(h1, dproj1, N_DEV, "mm_dw_in_1", deps=done)
    pair, tok = pair_begin([pieces(dw_in1, d, 2 * es), pieces(by_rows(dw_a), heads * dh_s, dh),
                            pieces(by_rows(dw_x), heads * dh_s, dh)], "lru_in")
    dh1 = _mm_nt(dproj1, wg_in1, F32, "mm_dh_1", deps=[tok])
    rs2, tok = scatter_start(pair, ["lru_w_in", "lru_w_a", "lru_w_x"], [dh1])
    dx1, dscale1, dshift1, dng1, dy0, dgate0 = _norm_mod_bwd(x1, dh1, dx2, ng[1], scale[1], "norm_mod_bwd_1",
                                                             below=(y0, gate[0]), deps=[tok])
    dw_out0 = _mm_tn(yb0, dy0[None], 1, "mm_dw_out_0")
    pair, tok = pair_begin([pieces(dw_out0, es, d)], "sc_w_out")
    dyb0 = _mm_nt(dy0[None], w_out0[None], BF16, "mm_dyb_0", deps=[tok])
    rs3, tok = scatter_start(pair, ["sc_w_out"], [dyb0])
    dproj0, vecs0 = _sc_bwd(proj0, dyb0, cw3, "sc_bwd", deps=[tok])
    idx_one = jnp.stack([jnp.zeros_like(mc)] * 4 + [mc]).astype(jnp.int32)
    sc_w_in_steps = []

    def chip_step(j, pair, after):
        (part,), (got,) = _pair_wait(pair["parts"], pair["lands"], pair["send"], pair["recv"], after,
                                     "pair_wait_sc_w_in_%d" % j)
        sm = _pair_sum(idx_one, part, got, "pair_sum_sc_w_in_%d" % j, nslots=1)
        send, recv, sums, lnd, token = _chip_start([sm], [lax.empty(sm.shape, sm.dtype)],
                                                   "chip_start_sc_w_in_%d" % j, flips=(j,))
        sc_w_in_steps.append((sums, lnd, send, recv, j))
        return token

    pending, done = None, []
    for j in (3, 1, 2, 0):
        part = _mm_tn_group(h0, dproj0, idx, (j - 1) % 4, 2, "mm_dw_in_0_%d" % j, deps=done)[None]
        pair, tok = pair_begin([part], "sc_w_in_%d" % j)
        if j == 3:
            done = scatter_finish(rs2, [chip_step(j, pair, [tok])])
            continue
        done = [tok]
        if pending is not None:
            done.append(chip_step(pending[0], pending[1], [tok]))
        pending = (j, pair)
    dh0 = _mm_nt(dproj0, wg_in0, F32, "mm_dh_0", deps=done)
    pair = pending[1]
    (part,), (got,) = _pair_wait(pair["parts"], pair["lands"], pair["send"], pair["recv"], [dh0],
                                 "pair_wait_sc_w_in_0")
    dx0, dscale0, dshift0, dng0 = _norm_mod_bwd(x0, dh0, dx1, ng[0], scale[0], "norm_mod_bwd_0")
    done = scatter_finish(rs3, [dx0])
    dmod_mine = jnp.concatenate([dshift0, dscale0, dgate0, dshift1, dscale1, dgate1], axis=1)
    end_shapes = [(LANES,), (2, 3 * d), (2, d), (d,), (8, e), (16, e)]
    end_all = _small_gather(
        _pack([loss_part, dmod_mine, jnp.concatenate([dng0, dng1], axis=0), d_fg, vecs0, vecs1]),
        "gather_small_grads", deps=done)
    end_sum = _device_sum(end_all, "sum_small_grads").reshape(-1)
    loss_v, g_ada_b, g_norm_g, g_final_g, sum0, sum1 = _unpack(end_sum, end_shapes)
    loss = loss_v[0]
    dmod_all = _unpack(end_all.reshape(N_DEV, -1), end_shapes)[1].transpose(1, 0, 2)
    dmod_cols = _my_slice(dmod_all, me, 2)
    g_ada_w = _ada_grad(c_all.T, dmod_cols, "ada_grad")

    g_sc_conv_w = _my_slice(sum0[0:3], me, 1)
    g_lru_b_a = _my_slice(sum1[0].reshape(heads, dh), me, 1)
    g_lru_b_x = _my_slice(sum1[1].reshape(heads, dh), me, 1)
    g_lru_lambda = _my_slice(sum1[2:3], me, 1)
    g_lru_conv_b = _my_slice(sum1[3:4], me, 1)
    g_lru_conv_w = _my_slice(sum1[4:8], me, 1)

    ada_res = _adamw(ada_w.reshape(2 * d, f), g_ada_w.reshape(2 * d, f), m_ada_w.reshape(2 * d, f),
                     v_ada_w.reshape(2 * d, f), "adamw_ada_w")
    ada_out = [g_ada_w] + [r.reshape(ada_w.shape) for r in ada_res]

    small_w = [norm_g, ada_b, final_g, sc_conv_w, lru_conv_w, lru_conv_b, lru_b_a, lru_b_x, lru_lambda]
    small_m = [m_norm_g, m_ada_b, m_final_g, m_sc_conv_w, m_lru_conv_w, m_lru_conv_b, m_lru_b_a, m_lru_b_x,
               m_lru_lambda]
    small_v = [v_norm_g, v_ada_b, v_final_g, v_sc_conv_w, v_lru_conv_w, v_lru_conv_b, v_lru_b_a, v_lru_b_x,
               v_lru_lambda]
    small_g = [g_norm_g, g_ada_b, g_final_g, g_sc_conv_w, g_lru_conv_w, g_lru_conv_b, g_lru_b_a, g_lru_b_x,
               g_lru_lambda]
    small_g = [g.reshape(w.shape) for g, w in zip(small_g, small_w)]
    shapes = [w.shape for w in small_w]
    packed = _adamw(_pack(small_w), _pack(small_g), _pack(small_m), _pack(small_v), "adamw_small")
    small_out = [small_g] + [_unpack(p.reshape(-1), shapes) for p in packed]

    after = [packed[0], ada_res[0]]
    recvs = []
    for sums, lnd, send, recv, j in sc_w_in_steps:
        recvs += _chip_wait(sums, lnd, send, recv, after, "chip_wait_sc_w_in_%d" % j)
    shp2 = part.shape[2:]
    res = _adamw_reduced(idx_one, sc_w_in.reshape(shp2), m_sc_w_in.reshape(shp2), v_sc_w_in.reshape(shp2),
                         part, got, recvs, "adamw_sc_w_in")
    big_res["sc_w_in"] = [r.reshape(sc_w_in.shape) for r in res]
    big_out = [big_res[nm] for nm in ("sc_w_in", "sc_w_out", "lru_w_in", "lru_w_a", "lru_w_x", "lru_w_out")]

    def small(kind, i):
        return small_out[kind][i]

    def bigw(kind, i):
        return big_out[i][kind]

    outs = [loss, dx0[None]]
    for kind in range(4):
        outs += [small(kind, 0), ada_out[kind], small(kind, 1), bigw(kind, 0), small(kind, 3), bigw(kind, 1),
                 bigw(kind, 2), small(kind, 4), small(kind, 5), bigw(kind, 3), small(kind, 6), bigw(kind, 4),
                 small(kind, 7), small(kind, 8), bigw(kind, 5), small(kind, 2)]
    return tuple(outs)
```

```python
import math

import jax
import jax.numpy as jnp
from jax import lax
from jax.experimental import pallas as pl
from jax.experimental.pallas import tpu as pltpu

N_DEV = 8
LANES = 128
EPS = 1e-6
RGLRU_C = 8.0
ADAM_LR = 0.001
ADAM_B1 = 0.9
ADAM_B2 = 0.999
ADAM_EPS = 1e-08
ADAM_WD = 0.01
ADAM_STEP = 10
VMEM_LIMIT = 56 * 1024 * 1024
MESH = pl.DeviceIdType.MESH
F32 = jnp.float32
BF16 = jnp.bfloat16
ANY = pl.BlockSpec(memory_space=pl.ANY)
HBM = pl.BlockSpec(memory_space=pltpu.HBM)
SEM = pl.BlockSpec(memory_space=pltpu.SEMAPHORE)
VMEM_SPEC = pl.BlockSpec(memory_space=pltpu.VMEM)
EFFECT = pltpu.SideEffectType.DATAFLOW_SIDE_EFFECTING
TOKEN = jax.ShapeDtypeStruct((8, LANES), jnp.float32)


def _tile(n, pref):
    t = min(n, pref)
    assert n % t == 0, (n, pref)
    return t


def _params(*sem):
    return pltpu.CompilerParams(dimension_semantics=sem, vmem_limit_bytes=VMEM_LIMIT)


def _position():
    return lax.axis_index("x"), lax.axis_index("y"), lax.axis_index("c")


def _flip(x, y, k):
    return (1 - x if k & 2 else x), (1 - y if k & 1 else y)


def _after(body, n_in, deps):
    if not deps:
        return body

    def wrapped(*refs):
        return body(*refs[:n_in], *refs[n_in + len(deps):])

    return wrapped


def _small_gather(v, name, deps=()):
    rows = v.shape[0]

    def body(v_ref, out_ref, send_sems, recv_sems):
        x, y, c = _position()
        me = 4 * x + 2 * y + c
        out_ref[me] = v_ref[...]
        copies = []
        for k in range(1, N_DEV):
            px, py = _flip(x, y, k >> 1)
            pc = 1 - c if k & 1 else c
            cp = pltpu.make_async_remote_copy(
                src_ref=v_ref, dst_ref=out_ref.at[me],
                send_sem=send_sems.at[k - 1], recv_sem=recv_sems.at[k - 1],
                device_id=(px, py, pc), device_id_type=MESH)
            cp.start()
            copies.append((cp, 4 * px + 2 * py + pc))
        for k, (cp, peer) in enumerate(copies):
            pltpu.make_async_remote_copy(
                src_ref=v_ref, dst_ref=out_ref.at[peer],
                send_sem=send_sems.at[k], recv_sem=recv_sems.at[k],
                device_id=(x, y, c), device_id_type=MESH).wait_recv()
        for cp, _ in copies:
            cp.wait_send()

    return pl.pallas_call(
        _after(body, 1, deps), name=name,
        out_shape=jax.ShapeDtypeStruct((N_DEV, rows, LANES), F32),
        in_specs=[VMEM_SPEC] + [ANY] * len(deps), out_specs=VMEM_SPEC,
        scratch_shapes=[pltpu.SemaphoreType.DMA((N_DEV - 1,)),
                        pltpu.SemaphoreType.DMA((N_DEV - 1,))],
        compiler_params=pltpu.CompilerParams(vmem_limit_bytes=VMEM_LIMIT),
    )(v, *deps)


def _hbm(a):
    return pltpu.with_memory_space_constraint(a, pltpu.HBM)


def _hbm_like(arrays):
    return [pltpu.HBM(a.shape, a.dtype) for a in arrays]


def _remote(src, dst, send, recv, to):
    return pltpu.make_async_remote_copy(src_ref=src, dst_ref=dst, send_sem=send, recv_sem=recv,
                                        device_id=to, device_id_type=MESH)


def _gather_start(shards, lands, units, after, name):
    n, nu = len(shards), len(units)

    def body(*refs):
        ins, lnd = refs[:n], refs[n:2 * n]
        sems = refs[2 * n + len(after):2 * n + len(after) + 2 * nu]
        token = refs[-1]
        x, y, c = _position()
        me = 4 * x + 2 * y + c
        targets = [(x, y, 1 - c)] + [(px, py, c) for px, py in (_flip(x, y, k) for k in (1, 2, 3))]
        for u, (members, ks) in enumerate(units):
            for slot, i in enumerate(members):
                for ki, k in enumerate(ks):
                    at = len(ks) * slot + ki
                    _remote(ins[i], lnd[i].at[me], sems[2 * u].at[at], sems[2 * u + 1].at[at], targets[k]).start()
        token[...] = jnp.zeros_like(token)

    sem_shapes = []
    for members, ks in units:
        count = len(members) * len(ks)
        sem_shapes += [pltpu.SemaphoreType.DMA((count,)), pltpu.SemaphoreType.DMA((count,))]
    out = pl.pallas_call(
        body, name=name,
        out_shape=sem_shapes + _hbm_like(shards) + _hbm_like(lands) + [TOKEN],
        in_specs=[HBM] * (2 * n) + [ANY] * len(after),
        out_specs=[SEM] * (2 * nu) + [HBM] * (2 * n) + [VMEM_SPEC],
        input_output_aliases={i: 2 * nu + i for i in range(2 * n)},
        compiler_params=pltpu.CompilerParams(has_side_effects=EFFECT),
    )(*[_hbm(s) for s in shards], *[_hbm(l) for l in lands], *after)
    sems = [(out[2 * u], out[2 * u + 1]) for u in range(nu)]
    return sems, list(out[2 * nu:2 * nu + n]), list(out[2 * nu + n:2 * nu + 2 * n]), out[-1]


def _gather_forward(shards, lands, ks, send, recv, after, name):
    m = len(shards)
    hops = [k for k in ks if k]
    nsem = 2 if hops else 0

    def body(*refs):
        ins, lnd = refs[:m], refs[m:2 * m]
        send_ref, recv_ref = refs[2 * m], refs[2 * m + 1]
        outs = refs[2 * m + 2 + len(after):]
        token = refs[-1]
        x, y, c = _position()
        me = (x, y, c)
        for slot in range(m):
            for ki, k in enumerate(ks):
                at = len(ks) * slot + ki
                if k:
                    px, py = _flip(x, y, k)
                    block = lnd[slot].at[4 * px + 2 * py + c]
                else:
                    block = lnd[slot].at[4 * x + 2 * y + (1 - c)]
                arrival = _remote(ins[slot], block, send_ref.at[at], recv_ref.at[at], me)
                arrival.wait_recv()
                if k:
                    fat = len(hops) * slot + hops.index(k)
                    _remote(block, block, outs[0].at[fat], outs[1].at[fat], (x, y, 1 - c)).start()
                arrival.wait_send()
        token[...] = jnp.zeros_like(token)

    count = len(hops) * m
    sem_shapes = [pltpu.SemaphoreType.DMA((count,)), pltpu.SemaphoreType.DMA((count,))] if hops else []
    out = pl.pallas_call(
        body, name=name,
        out_shape=sem_shapes + _hbm_like(shards) + _hbm_like(lands) + [TOKEN],
        in_specs=[HBM] * (2 * m) + [SEM, SEM] + [ANY] * len(after),
        out_specs=[SEM] * nsem + [HBM] * (2 * m) + [VMEM_SPEC],
        input_output_aliases={i: nsem + i for i in range(2 * m)},
        compiler_params=pltpu.CompilerParams(has_side_effects=EFFECT),
    )(*shards, *lands, send, recv, *after)
    fwd = (out[0], out[1]) if hops else None
    return fwd, list(out[nsem:nsem + m]), list(out[nsem + m:nsem + 2 * m]), out[-1]


def _gather_finish(lands, ks, fwd, after, name):
    m = len(lands)
    hops = [k for k in ks if k]

    def body(*refs):
        lnd = refs[:m]
        fsend_ref, frecv_ref = refs[m], refs[m + 1]
        x, y, c = _position()
        for slot in range(m):
            for fi, k in enumerate(hops):
                px, py = _flip(x, y, k)
                sent = lnd[slot].at[4 * px + 2 * py + c]
                came = lnd[slot].at[4 * px + 2 * py + (1 - c)]
                fat = len(hops) * slot + fi
                cp = _remote(sent, came, fsend_ref.at[fat], frecv_ref.at[fat], (x, y, c))
                cp.wait_recv()
                cp.wait_send()

    out = pl.pallas_call(
        body, name=name,
        out_shape=_hbm_like(lands),
        in_specs=[HBM] * m + [SEM, SEM] + [ANY] * len(after), out_specs=[HBM] * m,
        input_output_aliases={i: i for i in range(m)},
        compiler_params=pltpu.CompilerParams(has_side_effects=EFFECT),
    )(*lands, fwd[0], fwd[1], *after)
    return list(out)


def _pair_start(parts, name):
    n = len(parts)
    lands = [lax.empty((p.shape[0], 1) + p.shape[2:], p.dtype) for p in parts]

    def body(*refs):
        ins, lnd = refs[:n], refs[n:2 * n]
        send_ref, recv_ref = refs[2 * n], refs[2 * n + 1]
        token = refs[-1]
        x, y, c = _position()
        for i in range(n):
            _remote(ins[i].at[:, pl.ds(1 - c, 1)], lnd[i], send_ref.at[i], recv_ref.at[i], (x, y, 1 - c)).start()
        token[...] = jnp.zeros_like(token)

    out = pl.pallas_call(
        body, name=name,
        out_shape=[pltpu.SemaphoreType.DMA((n,)), pltpu.SemaphoreType.DMA((n,))]
        + _hbm_like(parts) + _hbm_like(lands) + [TOKEN],
        in_specs=[HBM] * (2 * n), out_specs=[SEM, SEM] + [HBM] * (2 * n) + [VMEM_SPEC],
        input_output_aliases={i: 2 + i for i in range(2 * n)},
        compiler_params=pltpu.CompilerParams(has_side_effects=EFFECT),
    )(*[_hbm(p) for p in parts], *[_hbm(l) for l in lands])
    return out[0], out[1], list(out[2:2 + n]), list(out[2 + n:2 + 2 * n]), out[-1]


def _pair_wait(parts, lands, send, recv, after, name):
    n = len(parts)

    def body(*refs):
        ins, lnd = refs[:n], refs[n:2 * n]
        send_ref, recv_ref = refs[2 * n], refs[2 * n + 1]
        x, y, c = _position()
        for i in range(n):
            cp = _remote(ins[i].at[:, pl.ds(1 - c, 1)], lnd[i], send_ref.at[i], recv_ref.at[i], (x, y, c))
            cp.wait_recv()
            cp.wait_send()

    out = pl.pallas_call(
        body, name=name,
        out_shape=_hbm_like(parts) + _hbm_like(lands),
        in_specs=[HBM] * (2 * n) + [SEM, SEM] + [ANY] * len(after), out_specs=[HBM] * (2 * n),
        input_output_aliases={i: i for i in range(2 * n)},
        compiler_params=pltpu.CompilerParams(has_side_effects=EFFECT),
    )(*parts, *lands, send, recv, *after)
    return list(out[:n]), list(out[n:])


def _chip_start(sums, lands, name, flips=(1, 2, 3)):
    n, ns = len(sums), len(flips)

    def body(*refs):
        ins, lnd = refs[:n], refs[n:2 * n]
        send_ref, recv_ref = refs[2 * n], refs[2 * n + 1]
        token = refs[-1]
        x, y, c = _position()
        for i in range(n):
            for j, flip in enumerate(flips):
                px, py = _flip(x, y, flip)
                _remote(ins[i].at[j], lnd[i].at[j], send_ref.at[ns * i + j], recv_ref.at[ns * i + j],
                        (px, py, c)).start()
        token[...] = jnp.zeros_like(token)

    out = pl.pallas_call(
        body, name=name,
        out_shape=[pltpu.SemaphoreType.DMA((ns * n,)), pltpu.SemaphoreType.DMA((ns * n,))]
        + _hbm_like(sums) + _hbm_like(lands) + [TOKEN],
        in_specs=[HBM] * (2 * n), out_specs=[SEM, SEM] + [HBM] * (2 * n) + [VMEM_SPEC],
        input_output_aliases={i: 2 + i for i in range(2 * n)},
        compiler_params=pltpu.CompilerParams(has_side_effects=EFFECT),
    )(*[_hbm(s) for s in sums], *[_hbm(l) for l in lands])
    return out[0], out[1], out[2:2 + n], out[2 + n:2 + 2 * n], out[-1]


def _chip_wait(sums, lands, send, recv, after, name):
    n, ns = len(sums), sums[0].shape[0]

    def body(*refs):
        ins, lnd = refs[:n], refs[n:2 * n]
        send_ref, recv_ref = refs[2 * n], refs[2 * n + 1]
        x, y, c = _position()
        for i in range(n):
            for j in range(ns):
                cp = _remote(ins[i].at[j], lnd[i].at[j], send_ref.at[ns * i + j], recv_ref.at[ns * i + j], (x, y, c))
                cp.wait_recv()
                cp.wait_send()

    out = pl.pallas_call(
        body, name=name,
        out_shape=_hbm_like(sums) + _hbm_like(lands),
        in_specs=[HBM] * (2 * n) + [SEM, SEM] + [ANY] * len(after), out_specs=[HBM] * (2 * n),
        input_output_aliases={i: i for i in range(2 * n)},
        compiler_params=pltpu.CompilerParams(has_side_effects=EFFECT),
    )(*sums, *lands, send, recv, *after)
    return list(out[n:])


def _pair_sum(idx, part, got, name, nslots=3):
    _, _, rows, cols = part.shape
    tr = _tile(rows, 1024)

    def body(idx_ref, p_ref, q_ref, o_ref):
        o_ref[...] = (p_ref[...].astype(F32) + q_ref[...].astype(F32)).astype(o_ref.dtype)

    grid_spec = pltpu.PrefetchScalarGridSpec(
        num_scalar_prefetch=1, grid=(nslots, rows // tr),
        in_specs=[pl.BlockSpec((None, None, tr, cols), lambda j, r, idx: (idx[j], idx[4], r, 0)),
                  pl.BlockSpec((None, None, tr, cols), lambda j, r, idx: (idx[j], 0, r, 0))],
        out_specs=pl.BlockSpec((None, tr, cols), lambda j, r, idx: (j, r, 0)))
    return pl.pallas_call(
        body, name=name, grid_spec=grid_spec,
        out_shape=jax.ShapeDtypeStruct((nslots, rows, cols), part.dtype),
        compiler_params=_params("arbitrary", "arbitrary"),
    )(idx, part, got)


def _mm_proj_group(h, wg, idx, pos, prev, name, deps=()):
    s, k = h.shape
    _, _, n = wg.shape
    _, _, e = prev.shape
    tn = _tile(n, 512)
    nd = len(deps)

    def body(idx_ref, h_ref, w_ref, prev_ref, *rest):
        o_ref = rest[nd]
        o_ref[...] = jnp.dot(h_ref[...], w_ref[...], preferred_element_type=F32).astype(o_ref.dtype)

    def col(j, idx):
        return idx[pos] * (2 * n) + j * tn

    grid_spec = pltpu.PrefetchScalarGridSpec(
        num_scalar_prefetch=1, grid=(2 * n // tn,),
        in_specs=[pl.BlockSpec((s, k), lambda j, idx: (0, 0)),
                  pl.BlockSpec((None, k, tn), lambda j, idx: (col(j, idx) // n, 0, (col(j, idx) % n) // tn)),
                  ANY] + [ANY] * nd,
        out_specs=pl.BlockSpec((None, s, tn), lambda j, idx: (col(j, idx) // e, 0, (col(j, idx) % e) // tn)))
    return pl.pallas_call(
        body, name=name, grid_spec=grid_spec,
        out_shape=jax.ShapeDtypeStruct(prev.shape, prev.dtype),
        input_output_aliases={3: 0},
        compiler_params=_params("arbitrary"),
    )(idx, h, wg, prev, *deps)


def _mm_out(yb, w, x, gate, name):
    s, k = yb.shape
    d = w.shape[1]
    tn = _tile(d, 512)
    tk = _tile(k, 1024)
    nk = k // tk

    def body(a_ref, w_ref, x_ref, g_ref, xo_ref, y_ref, acc_ref):
        kk = pl.program_id(1)

        @pl.when(kk == 0)
        def _():
            acc_ref[...] = jnp.zeros_like(acc_ref)

        acc_ref[...] += jnp.dot(a_ref[...], w_ref[...], preferred_element_type=F32)

        @pl.when(kk == nk - 1)
        def _():
            y = acc_ref[...]
            y_ref[...] = y.astype(y_ref.dtype)
            xo_ref[...] = x_ref[...] + g_ref[...] * y

    return pl.pallas_call(
        body, name=name, grid=(d // tn, nk),
        in_specs=[pl.BlockSpec((s, tk), lambda j, kk: (0, kk)),
                  pl.BlockSpec((tk, tn), lambda j, kk: (kk, j)),
                  pl.BlockSpec((s, tn), lambda j, kk: (0, j)),
                  pl.BlockSpec((1, tn), lambda j, kk: (0, j))],
        out_specs=[pl.BlockSpec((s, tn), lambda j, kk: (0, j)),
                   pl.BlockSpec((s, tn), lambda j, kk: (0, j))],
        out_shape=[jax.ShapeDtypeStruct((s, d), F32), jax.ShapeDtypeStruct((s, d), BF16)],
        scratch_shapes=[pltpu.VMEM((s, tn), F32)],
        compiler_params=_params("arbitrary", "arbitrary"),
    )(yb, w, x, gate)


def _mm_nt(a3, w3, out_dtype, name, deps=()):
    g, s, ea = a3.shape
    cw, n, nw = w3.shape
    total = g * ea
    assert total == cw * nw
    tk = _tile(min(ea, nw), 1024)
    tn = _tile(n, 1024)
    nk = total // tk

    def body(a_ref, w_ref, o_ref, acc_ref):
        kk = pl.program_id(1)

        @pl.when(kk == 0)
        def _():
            acc_ref[...] = jnp.zeros_like(acc_ref)

        acc_ref[...] += lax.dot_general(a_ref[...], w_ref[...], (((1,), (1,)), ((), ())),
                                        preferred_element_type=F32)

        @pl.when(kk == nk - 1)
        def _():
            o_ref[...] = acc_ref[...].astype(o_ref.dtype)

    return pl.pallas_call(
        _after(body, 2, deps), name=name, grid=(n // tn, nk),
        in_specs=[pl.BlockSpec((None, s, tk), lambda j, kk: ((kk * tk) // ea, 0, ((kk * tk) % ea) // tk)),
                  pl.BlockSpec((None, tn, tk), lambda j, kk: ((kk * tk) // nw, j, ((kk * tk) % nw) // tk))]
        + [ANY] * len(deps),
        out_specs=pl.BlockSpec((s, tn), lambda j, kk: (0, j)),
        out_shape=jax.ShapeDtypeStruct((s, n), out_dtype),
        scratch_shapes=[pltpu.VMEM((s, tn), F32)],
        compiler_params=_params("arbitrary", "arbitrary"),
    )(a3, w3, *deps)


def _mm_tn(a, b3, nchunk, name, deps=()):
    s, ka = a.shape
    g, _, eb = b3.shape
    n = g * eb // nchunk
    tm = _tile(ka, 1024)
    tn = _tile(min(n, eb), 1024)

    def body(a_ref, b_ref, o_ref, at_ref):
        @pl.when(pl.program_id(1) == 0)
        def _():
            at_ref[...] = a_ref[...].astype(F32).T.astype(at_ref.dtype)

        o_ref[...] = jnp.dot(at_ref[...], b_ref[...], preferred_element_type=F32).astype(o_ref.dtype)

    return pl.pallas_call(
        _after(body, 2, deps), name=name, grid=(ka // tm, g * eb // tn),
        in_specs=[pl.BlockSpec((s, tm), lambda i, j: (0, i)),
                  pl.BlockSpec((None, s, tn), lambda i, j: ((j * tn) // eb, 0, ((j * tn) % eb) // tn))]
        + [ANY] * len(deps),
        out_specs=pl.BlockSpec((None, tm, tn), lambda i, j: ((j * tn) // n, i, ((j * tn) % n) // tn)),
        out_shape=jax.ShapeDtypeStruct((nchunk, ka, n), BF16),
        scratch_shapes=[pltpu.VMEM((tm, s), BF16)],
        compiler_params=_params("arbitrary", "arbitrary"),
    )(a, b3, *deps)


def _mm_tn_group(a, b3, idx, pos, nchunk, name, deps=()):
    s, ka = a.shape
    _, _, eb = b3.shape
    n = eb // nchunk
    tm = _tile(ka, 1024)
    tn = _tile(n, 1024)
    nd = len(deps)

    def body(idx_ref, a_ref, b_ref, *rest):
        o_ref, at_ref = rest[nd:]

        @pl.when(pl.program_id(1) == 0)
        def _():
            at_ref[...] = a_ref[...].astype(F32).T.astype(at_ref.dtype)

        o_ref[...] = jnp.dot(at_ref[...], b_ref[...], preferred_element_type=F32).astype(o_ref.dtype)

    grid_spec = pltpu.PrefetchScalarGridSpec(
        num_scalar_prefetch=1, grid=(ka // tm, eb // tn),
        in_specs=[pl.BlockSpec((s, tm), lambda i, j, idx: (0, i)),
                  pl.BlockSpec((None, s, tn), lambda i, j, idx: (idx[pos], 0, j))] + [ANY] * nd,
        out_specs=pl.BlockSpec((None, tm, tn), lambda i, j, idx: ((j * tn) // n, i, ((j * tn) % n) // tn)),
        scratch_shapes=[pltpu.VMEM((tm, s), BF16)])
    return pl.pallas_call(
        body, name=name, grid_spec=grid_spec,
        out_shape=jax.ShapeDtypeStruct((nchunk, ka, n), BF16),
        compiler_params=_params("arbitrary", "arbitrary"),
    )(idx, a, b3, *deps)


def _sigmoid(z):
    return jax.nn.sigmoid(z)


def _shift_down(v, k, fill=0.0, period=None):
    if k == 0:
        return v
    row = lax.broadcasted_iota(jnp.int32, v.shape, 0)
    if period is not None:
        row = row & (period - 1)
    return jnp.where(row >= k, pltpu.roll(v, k, 0), fill)


def _shift_up(v, k, fill=0.0, period=None):
    if k == 0:
        return v
    s = v.shape[0]
    row = lax.broadcasted_iota(jnp.int32, v.shape, 0)
    if period is not None:
        row, s = row & (period - 1), period
    return jnp.where(row < s - k, pltpu.roll(v, v.shape[0] - k, 0), fill)


SCAN_BLOCK = 64


def _scan(a, b, shift):
    s = a.shape[0]
    blk = min(SCAN_BLOCK, s)
    k = 1
    while k < blk:
        b = a * shift(b, k, 0.0, blk) + b
        a = a * shift(a, k, 1.0, blk)
        k *= 2
    nblk = s // blk
    forward = shift is _shift_down
    order = range(nblk) if forward else range(nblk - 1, -1, -1)
    edge = blk - 1 if forward else 0
    out = [None] * nblk
    carry = None
    for i in order:
        h = b[i * blk:(i + 1) * blk]
        if carry is not None:
            h = a[i * blk:(i + 1) * blk] * carry + h
        carry = h[edge:edge + 1]
        out[i] = h
    return jnp.concatenate(out, axis=0) if nblk > 1 else out[0]


def _norm_mod(x, g, scale, shift, name, deps=()):
    s, d = x.shape
    ts = _tile(s, 256)

    def body(x_ref, g_ref, sc_ref, sh_ref, h_ref):
        xv = x_ref[...]
        rstd = lax.rsqrt(jnp.mean(xv * xv, axis=-1, keepdims=True) + EPS)
        nrm = xv * rstd * g_ref[...]
        h_ref[...] = (nrm * (1.0 + sc_ref[...]) + sh_ref[...]).astype(h_ref.dtype)

    vec = pl.BlockSpec((1, d), lambda i: (0, 0))
    return pl.pallas_call(
        _after(body, 4, deps), name=name, grid=(s // ts,),
        in_specs=[pl.BlockSpec((ts, d), lambda i: (i, 0)), vec, vec, vec] + [ANY] * len(deps),
        out_specs=pl.BlockSpec((ts, d), lambda i: (i, 0)),
        out_shape=jax.ShapeDtypeStruct((s, d), BF16),
        compiler_params=_params("arbitrary"),
    )(x, g, scale, shift, *deps)


def _gate_terms(dx, y_ref, gate_ref, dy_ref, dgate_ref):
    dy_ref[...] = (dx * gate_ref[...]).astype(dy_ref.dtype)
    dgate_ref[...] += jnp.sum(dx * y_ref[...].astype(F32), axis=0, keepdims=True)


def _norm_mod_bwd(x, dh, dx_res, g, scale, name, below=None, deps=()):
    s, d = x.shape
    ts = _tile(s, 256)
    nb = 2 if below is not None else 0

    def body(x_ref, dh_ref, dr_ref, g_ref, sc_ref, *rest):
        dx_ref, dsc_ref, dsh_ref, dg_ref = rest[nb:nb + 4]

        @pl.when(pl.program_id(0) == 0)
        def _():
            for ref in rest[nb + 1:nb + 4] + rest[nb + 5:]:
                ref[...] = jnp.zeros_like(ref)

        xv = x_ref[...]
        dh_v = dh_ref[...].astype(F32)
        gv = g_ref[...]
        rstd = lax.rsqrt(jnp.mean(xv * xv, axis=-1, keepdims=True) + EPS)
        xhat = xv * rstd
        dsc_ref[...] += jnp.sum(dh_v * xhat * gv, axis=0, keepdims=True)
        dsh_ref[...] += jnp.sum(dh_v, axis=0, keepdims=True)
        dn = dh_v * (1.0 + sc_ref[...])
        dg_ref[...] += jnp.sum(dn * xhat, axis=0, keepdims=True)
        dxhat = dn * gv
        proj = jnp.mean(dxhat * xhat, axis=-1, keepdims=True)
        dx = dr_ref[...] + rstd * (dxhat - xhat * proj)
        dx_ref[...] = dx
        if nb:
            _gate_terms(dx, rest[0], rest[1], rest[nb + 4], rest[nb + 5])

    row = pl.BlockSpec((ts, d), lambda i: (i, 0))
    vec = pl.BlockSpec((1, d), lambda i: (0, 0))
    extra = list(below) if nb else []
    return pl.pallas_call(
        _after(body, 5 + nb, deps), name=name, grid=(s // ts,),
        in_specs=[row, row, row, vec, vec] + [row, vec][:nb] + [ANY] * len(deps),
        out_specs=[row, vec, vec, vec] + [row, vec][:nb],
        out_shape=[jax.ShapeDtypeStruct((s, d), F32)] + [jax.ShapeDtypeStruct((1, d), F32)] * 3
        + [jax.ShapeDtypeStruct((s, d), BF16), jax.ShapeDtypeStruct((1, d), F32)][:nb],
        compiler_params=_params("arbitrary"),
    )(x, dh, dx_res, g, scale, *extra, *deps)


def _final_loss(x, g, target, y, gate, name):
    s, d = x.shape
    ts = _tile(s, 256)

    def body(x_ref, g_ref, t_ref, y_ref, gate_ref, dx_ref, loss_ref, dg_ref, dy_ref, dgate_ref):
        @pl.when(pl.program_id(0) == 0)
        def _():
            loss_ref[...] = jnp.zeros_like(loss_ref)
            dg_ref[...] = jnp.zeros_like(dg_ref)
            dgate_ref[...] = jnp.zeros_like(dgate_ref)

        xv = x_ref[...]
        gv = g_ref[...]
        rstd = lax.rsqrt(jnp.mean(xv * xv, axis=-1, keepdims=True) + EPS)
        xhat = xv * rstd
        err = xhat * gv - t_ref[...]
        loss_ref[...] += 0.5 * jnp.sum(jnp.mean(err * err, axis=-1, keepdims=True))
        dy = err * (1.0 / d)
        dg_ref[...] += jnp.sum(dy * xhat, axis=0, keepdims=True)
        dxhat = dy * gv
        proj = jnp.mean(dxhat * xhat, axis=-1, keepdims=True)
        dx = rstd * (dxhat - xhat * proj)
        dx_ref[...] = dx
        _gate_terms(dx, y_ref, gate_ref, dy_ref, dgate_ref)

    row = pl.BlockSpec((ts, d), lambda i: (i, 0))
    vec = pl.BlockSpec((1, d), lambda i: (0, 0))
    return pl.pallas_call(
        body, name=name, grid=(s // ts,),
        in_specs=[row, vec, row, row, vec],
        out_specs=[row, pl.BlockSpec((1, LANES), lambda i: (0, 0)), vec, row, vec],
        out_shape=[jax.ShapeDtypeStruct((s, d), F32), jax.ShapeDtypeStruct((1, LANES), F32),
                   jax.ShapeDtypeStruct((1, d), F32), jax.ShapeDtypeStruct((s, d), BF16),
                   jax.ShapeDtypeStruct((1, d), F32)],
        compiler_params=_params("arbitrary"),
    )(x, g, target, y, gate)


def _conv(v, w_ref, width):
    out = w_ref[width - 1:width, :] * v
    for k in range(width - 1):
        out = out + w_ref[k:k + 1, :] * _shift_down(v, width - 1 - k)
    return out


def _sc_fwd(proj, conv_w, name, deps=()):
    _, s, e = proj.shape
    te = _tile(e, 256)
    width = conv_w.shape[0]

    def body(b_ref, c_ref, v_ref, g_ref, w_ref, o_ref):
        cv = c_ref[...].astype(F32) * v_ref[...].astype(F32)
        u = _conv(cv, w_ref, width)
        gv = g_ref[...].astype(F32)
        o_ref[...] = (b_ref[...].astype(F32) * u * (gv * _sigmoid(gv))).astype(o_ref.dtype)

    def part(q):
        return pl.BlockSpec((None, s, te), lambda j, q=q: (q, 0, j))

    return pl.pallas_call(
        _after(body, 5, deps), name=name, grid=(e // te,),
        in_specs=[part(0), part(1), part(2), part(3), pl.BlockSpec((width, te), lambda j: (0, j))]
        + [ANY] * len(deps),
        out_specs=pl.BlockSpec((s, te), lambda j: (0, j)),
        out_shape=jax.ShapeDtypeStruct((s, e), BF16),
        compiler_params=_params("arbitrary"),
    )(proj, proj, proj, proj, conv_w, *deps)


def _sc_bwd(proj, dyb, conv_w, name, deps=()):
    _, s, e = proj.shape
    te = _tile(e, 256)
    width = conv_w.shape[0]

    def body(b_ref, c_ref, v_ref, g_ref, dy_ref, w_ref, dp_ref, vec_ref):
        bv = b_ref[...].astype(F32)
        cvl = c_ref[...].astype(F32)
        vv = v_ref[...].astype(F32)
        gv = g_ref[...].astype(F32)
        dyv = dy_ref[...].astype(F32)
        cv = cvl * vv
        u = _conv(cv, w_ref, width)
        sg = _sigmoid(gv)
        silu = gv * sg
        dp_ref[0] = (dyv * u * silu).astype(dp_ref.dtype)
        du = dyv * bv * silu
        dp_ref[3] = (dyv * bv * u * (sg * (1.0 + gv * (1.0 - sg)))).astype(dp_ref.dtype)
        dcv = w_ref[width - 1:width, :] * du
        vec_ref[...] = jnp.zeros_like(vec_ref)
        vec_ref[width - 1:width, :] = jnp.sum(du * cv, axis=0, keepdims=True)
        for k in range(width - 1):
            sh = width - 1 - k
            dcv = dcv + w_ref[k:k + 1, :] * _shift_up(du, sh)
            vec_ref[k:k + 1, :] = jnp.sum(du * _shift_down(cv, sh), axis=0, keepdims=True)
        dp_ref[1] = (dcv * vv).astype(dp_ref.dtype)
        dp_ref[2] = (dcv * cvl).astype(dp_ref.dtype)

    def part(q):
        return pl.BlockSpec((None, s, te), lambda j, q=q: (q, 0, j))

    return pl.pallas_call(
        _after(body, 6, deps), name=name, grid=(e // te,),
        in_specs=[part(0), part(1), part(2), part(3), pl.BlockSpec((s, te), lambda j: (0, j)),
                  pl.BlockSpec((width, te), lambda j: (0, j))] + [ANY] * len(deps),
        out_specs=[pl.BlockSpec((4, s, te), lambda j: (0, 0, j)),
                   pl.BlockSpec((8, te), lambda j: (0, j))],
        out_shape=[jax.ShapeDtypeStruct((4, s, e), BF16), jax.ShapeDtypeStruct((8, e), F32)],
        compiler_params=_params("arbitrary"),
    )(proj, proj, proj, proj, dyb, conv_w, *deps)


def _lru_gates(v_pre, w_ref, cb_ref, wa_ref, ba_ref, wx_ref, bx_ref, lam_ref, width):
    v = _conv(v_pre, w_ref, width) + cb_ref[...]
    vb = v.astype(BF16)
    r = _sigmoid(jnp.dot(vb, wa_ref[...], preferred_element_type=F32) + ba_ref[...])
    i = _sigmoid(jnp.dot(vb, wx_ref[...], preferred_element_type=F32) + bx_ref[...])
    nl = -lam_ref[...]
    sp = jnp.maximum(nl, 0.0) + jnp.log1p(jnp.exp(-jnp.abs(nl)))
    log_a = (-RGLRU_C) * r * sp
    a = jnp.exp(log_a)
    one_minus_a2 = jnp.tanh(-log_a) * (1.0 + a * a)
    mult = jnp.sqrt(one_minus_a2)
    return v, vb, r, i, sp, a, mult


def _lru_specs(s, dh, heads, width):
    head_col = lambda q: pl.BlockSpec((None, s, dh), lambda h, q=q: (q, 0, h))
    vec = pl.BlockSpec((1, dh), lambda h: (0, h))
    mat = pl.BlockSpec((None, dh, dh), lambda h: (h, 0, 0))
    weights = [pl.BlockSpec((width, dh), lambda h: (0, h)), vec, mat, vec, mat, vec, vec]
    return head_col, weights


def _lru_fwd(proj, conv_w, conv_b, w_a, b_a, w_x, b_x, lam, name, deps=()):
    _, s, e = proj.shape
    heads, dh, _ = w_a.shape
    width = conv_w.shape[0]

    def body(v_ref, g_ref, w_ref, cb_ref, wa_ref, ba_ref, wx_ref, bx_ref, lam_ref, yb_ref, keep_ref):
        v, _, r, i, _, a, mult = _lru_gates(v_ref[...].astype(F32), w_ref, cb_ref, wa_ref, ba_ref,
                                           wx_ref, bx_ref, lam_ref, width)
        hs = _scan(a, mult * i * v, _shift_down)
        for k, val in enumerate((hs, v, r, i, a, mult)):
            keep_ref[k] = val
        gv = g_ref[...].astype(F32)
        yb_ref[...] = (hs * (gv * _sigmoid(gv))).astype(yb_ref.dtype)

    head_col, weights = _lru_specs(s, dh, heads, width)
    return pl.pallas_call(
        _after(body, 9, deps), name=name, grid=(heads,),
        in_specs=[head_col(0), head_col(1)] + weights + [ANY] * len(deps),
        out_specs=[pl.BlockSpec((s, dh), lambda h: (0, h)), pl.BlockSpec((6, s, dh), lambda h: (0, 0, h))],
        out_shape=[jax.ShapeDtypeStruct((s, e), BF16), jax.ShapeDtypeStruct((6, s, e), F32)],
        compiler_params=_params("arbitrary"),
    )(proj, proj, conv_w, conv_b, w_a, b_a, w_x, b_x, lam, *deps)


def _lru_bwd(proj, keep, dyb, conv_w, conv_b, w_a, b_a, w_x, b_x, lam, name, deps=()):
    _, s, e = proj.shape
    heads, dh, _ = w_a.shape
    width = conv_w.shape[0]

    def body(v_ref, g_ref, hs_ref, dy_ref, w_ref, cb_ref, wa_ref, ba_ref, wx_ref, bx_ref, lam_ref,
             dp_ref, dwa_ref, dwx_ref, vec_ref):
        v_pre = v_ref[...].astype(F32)
        hs, v, r, i, a, mult = (hs_ref[k] for k in range(6))
        vb = v.astype(BF16)
        nl = -lam_ref[...]
        sp = jnp.maximum(nl, 0.0) + jnp.log1p(jnp.exp(-jnp.abs(nl)))
        gv = g_ref[...].astype(F32)
        dyv = dy_ref[...].astype(F32)
        sg = _sigmoid(gv)
        dp_ref[1] = (dyv * hs * (sg * (1.0 + gv * (1.0 - sg)))).astype(dp_ref.dtype)
        dhs = dyv * (gv * sg)
        d_h = _scan(_shift_up(a, 1), dhs, _shift_up)
        da = d_h * _shift_down(hs, 1)
        iv = i * v
        dlog_a = da * a - (d_h * iv) * (a * a) / mult
        di = d_h * mult * v
        dv = d_h * mult * i
        dzr = dlog_a * (-RGLRU_C) * sp * r * (1.0 - r)
        dzi = di * i * (1.0 - i)
        dsp = jnp.sum(dlog_a * r, axis=0, keepdims=True) * (-RGLRU_C)
        vec_ref[...] = jnp.zeros_like(vec_ref)
        vec_ref[0:1, :] = jnp.sum(dzr, axis=0, keepdims=True)
        vec_ref[1:2, :] = jnp.sum(dzi, axis=0, keepdims=True)
        vec_ref[2:3, :] = -dsp * _sigmoid(-lam_ref[...])
        dzr_b = dzr.astype(BF16)
        dzi_b = dzi.astype(BF16)
        vt = vb.astype(F32).T.astype(BF16)
        dwa_ref[...] = jnp.dot(vt, dzr_b, preferred_element_type=F32).astype(dwa_ref.dtype)
        dwx_ref[...] = jnp.dot(vt, dzi_b, preferred_element_type=F32).astype(dwx_ref.dtype)
        nt = (((1,), (1,)), ((), ()))
        dv = dv + lax.dot_general(dzr_b, wa_ref[...], nt, preferred_element_type=F32)
        dv = dv + lax.dot_general(dzi_b, wx_ref[...], nt, preferred_element_type=F32)
        vec_ref[3:4, :] = jnp.sum(dv, axis=0, keepdims=True)
        dvp = w_ref[width - 1:width, :] * dv
        vec_ref[4 + width - 1:4 + width, :] = jnp.sum(dv * v_pre, axis=0, keepdims=True)
        for k in range(width - 1):
            sh = width - 1 - k
            dvp = dvp + w_ref[k:k + 1, :] * _shift_up(dv, sh)
            vec_ref[4 + k:5 + k, :] = jnp.sum(dv * _shift_down(v_pre, sh), axis=0, keepdims=True)
        dp_ref[0] = dvp.astype(dp_ref.dtype)

    head_col, weights = _lru_specs(s, dh, heads, width)
    col = pl.BlockSpec((s, dh), lambda h: (0, h))
    mat = pl.BlockSpec((None, dh, dh), lambda h: (h, 0, 0))
    return pl.pallas_call(
        _after(body, 11, deps), name=name, grid=(heads,),
        in_specs=[head_col(0), head_col(1), pl.BlockSpec((6, s, dh), lambda h: (0, 0, h)), col] + weights
        + [ANY] * len(deps),
        out_specs=[pl.BlockSpec((2, s, dh), lambda h: (0, 0, h)), mat, mat,
                   pl.BlockSpec((16, dh), lambda h: (0, h))],
        out_shape=[jax.ShapeDtypeStruct((2, s, e), BF16),
                   jax.ShapeDtypeStruct((heads, dh, dh), BF16),
                   jax.ShapeDtypeStruct((heads, dh, dh), BF16),
                   jax.ShapeDtypeStruct((16, e), F32)],
        compiler_params=_params("arbitrary"),
    )(proj, proj, keep, dyb, conv_w, conv_b, w_a, b_a, w_x, b_x, lam, *deps)


def _ada_mod(c_all, w, b, name):
    layers, d, f = w.shape
    nb = c_all.shape[0]

    def body(c_ref, w_ref, b_ref, o_ref):
        cv = c_ref[...]
        sc = cv * _sigmoid(cv)
        o_ref[...] = jnp.dot(sc, w_ref[...], preferred_element_type=F32,
                             precision=lax.Precision.HIGHEST) + b_ref[...]

    return pl.pallas_call(
        body, name=name, grid=(layers,),
        in_specs=[pl.BlockSpec((nb, d), lambda l: (0, 0)),
                  pl.BlockSpec((None, d, f), lambda l: (l, 0, 0)),
                  pl.BlockSpec((None, 1, f), lambda l: (l, 0, 0))],
        out_specs=pl.BlockSpec((None, nb, f), lambda l: (l, 0, 0)),
        out_shape=jax.ShapeDtypeStruct((layers, nb, f), F32),
        compiler_params=_params("arbitrary"),
    )(c_all, w, b)


def _ada_grad(c_all_t, dmod, name):
    d, nb = c_all_t.shape
    layers, _, f = dmod.shape

    def body(c_ref, dm_ref, o_ref):
        cv = c_ref[...]
        sc = cv * _sigmoid(cv)
        acc = sc[:, 0:1] * dm_ref[0:1, :]
        for k in range(1, nb):
            acc = acc + sc[:, k:k + 1] * dm_ref[k:k + 1, :]
        o_ref[...] = acc

    return pl.pallas_call(
        body, name=name, grid=(layers,),
        in_specs=[pl.BlockSpec((d, nb), lambda l: (0, 0)),
                  pl.BlockSpec((None, nb, f), lambda l: (l, 0, 0))],
        out_specs=pl.BlockSpec((None, d, f), lambda l: (l, 0, 0)),
        out_shape=jax.ShapeDtypeStruct((layers, d, f), F32),
        compiler_params=_params("arbitrary"),
    )(c_all_t, dmod)


def _device_sum(g, name):
    _, rows, _ = g.shape

    def body(g_ref, o_ref):
        acc = g_ref[0]
        for k in range(1, N_DEV):
            acc = acc + g_ref[k]
        o_ref[...] = acc

    return pl.pallas_call(
        body, name=name,
        in_specs=[VMEM_SPEC], out_specs=VMEM_SPEC,
        out_shape=jax.ShapeDtypeStruct((rows, LANES), F32),
        compiler_params=pltpu.CompilerParams(vmem_limit_bytes=VMEM_LIMIT),
    )(g)


def _adamw_math(w, g, m, v):
    m = ADAM_B1 * m + (1.0 - ADAM_B1) * g
    v = ADAM_B2 * v + (1.0 - ADAM_B2) * (g * g)
    m_hat = m / (1.0 - ADAM_B1 ** ADAM_STEP)
    v_hat = v / (1.0 - ADAM_B2 ** ADAM_STEP)
    delta = -ADAM_LR * (m_hat / (jnp.sqrt(v_hat) + ADAM_EPS) + ADAM_WD * w)
    return delta, m, v


def _adamw(w, g, m, v, name):
    rows, cols = w.shape
    tr = _tile(rows, 256)

    def body(w_ref, g_ref, m_ref, v_ref, d_ref, mo_ref, vo_ref):
        d_ref[...], mo_ref[...], vo_ref[...] = _adamw_math(w_ref[...], g_ref[...], m_ref[...], v_ref[...])

    blk = pl.BlockSpec((tr, cols), lambda i: (i, 0))
    return pl.pallas_call(
        body, name=name, grid=(rows // tr,),
        in_specs=[blk] * 4, out_specs=[blk] * 3,
        out_shape=[jax.ShapeDtypeStruct((rows, cols), F32)] * 3,
        compiler_params=_params("arbitrary"),
    )(w, g, m, v)


def _adamw_reduced(idx, w, m, v, part, got, recvs, name):
    rows, cols = w.shape
    tr = _tile(rows, 256)
    nr = len(recvs)

    def body(idx_ref, w_ref, m_ref, v_ref, p_ref, q_ref, *rest):
        g_ref, d_ref, mo_ref, vo_ref = rest[nr:]
        g = p_ref[...].astype(F32) + q_ref[...].astype(F32)
        for u_ref in rest[:nr]:
            for j in range(u_ref.shape[0]):
                g = g + u_ref[j].astype(F32)
        g_ref[...] = g
        d_ref[...], mo_ref[...], vo_ref[...] = _adamw_math(w_ref[...], g, m_ref[...], v_ref[...])

    blk = pl.BlockSpec((tr, cols), lambda i, idx: (i, 0))
    grid_spec = pltpu.PrefetchScalarGridSpec(
        num_scalar_prefetch=1, grid=(rows // tr,),
        in_specs=[blk, blk, blk,
                  pl.BlockSpec((None, None, tr, cols), lambda i, idx: (idx[3], idx[4], i, 0)),
                  pl.BlockSpec((None, None, tr, cols), lambda i, idx: (idx[3], 0, i, 0))]
        + [pl.BlockSpec((u.shape[0], tr, cols), lambda i, idx: (0, i, 0)) for u in recvs],
        out_specs=[blk] * 4)
    return pl.pallas_call(
        body, name=name, grid_spec=grid_spec,
        out_shape=[jax.ShapeDtypeStruct((rows, cols), F32)] * 4,
        compiler_params=_params("arbitrary"),
    )(idx, w, m, v, part, got, *recvs)


def _pack(vectors):
    flat = jnp.concatenate([v.reshape(-1).astype(F32) for v in vectors])
    pad = (-flat.shape[0]) % (8 * LANES)
    return jnp.pad(flat, (0, pad)).reshape(-1, LANES)


def _unpack(flat, shapes):
    out, off = [], 0
    for shp in shapes:
        size = math.prod(shp)
        out.append(flat[..., off:off + size].reshape(flat.shape[:-1] + tuple(shp)))
        off += size
    return out


def _my_slice(full, me, axis):
    size = full.shape[axis] // N_DEV
    return lax.dynamic_slice_in_dim(full, me * size, size, axis)


def kernel(x, c, norm_g, ada_w, ada_b, sc_w_in, sc_conv_w, sc_w_out, lru_w_in, lru_conv_w, lru_conv_b, lru_w_a, lru_b_a, lru_w_x, lru_b_x, lru_lambda, lru_w_out, final_g, loss_target, m_norm_g, m_ada_w, m_ada_b, m_sc_w_in, m_sc_conv_w, m_sc_w_out, m_lru_w_in, m_lru_conv_w, m_lru_conv_b, m_lru_w_a, m_lru_b_a, m_lru_w_x, m_lru_b_x, m_lru_lambda, m_lru_w_out, m_final_g, v_norm_g, v_ada_w, v_ada_b, v_sc_w_in, v_sc_conv_w, v_sc_w_out, v_lru_w_in, v_lru_conv_w, v_lru_conv_b, v_lru_w_a, v_lru_b_a, v_lru_w_x, v_lru_b_x, v_lru_lambda, v_lru_w_out, v_final_g):
    _, s, d = x.shape
    e = sc_w_out.shape[1] * N_DEV
    heads, dh_s, dh = lru_w_a.shape[1:]
    es = e // N_DEV
    f = ada_w.shape[2]
    mx, my, mc = _position()
    me = 4 * mx + 2 * my + mc
    chip = 2 * mx + my
    idx = jnp.stack([chip ^ 1, chip ^ 2, chip ^ 3, chip, mc]).astype(jnp.int32)

    x0 = x[0]
    target = loss_target[0]

    small_shapes = [(d,), (3, es), (4, es), (es,), (heads, dh_s), (heads, dh_s), (es,)]
    small = _small_gather(_pack([c, sc_conv_w, lru_conv_w, lru_conv_b, lru_b_a, lru_b_x, lru_lambda]),
                          "gather_small_weights").reshape(N_DEV, -1)
    c_all, cw3, cw4, cb, ba, bx, lam = _unpack(small, small_shapes)
    cw3 = cw3.transpose(1, 0, 2).reshape(3, e)
    cw4 = cw4.transpose(1, 0, 2).reshape(4, e)
    cb = cb.reshape(1, e)
    lam = lam.reshape(1, e)
    ba = ba.transpose(1, 0, 2).reshape(1, e)
    bx = bx.transpose(1, 0, 2).reshape(1, e)

    shards = [sc_w_in[0].astype(BF16), sc_w_out[0].astype(BF16), lru_w_in[0].astype(BF16),
              lru_w_a[0].reshape(heads * dh_s, dh).astype(BF16),
              lru_w_x[0].reshape(heads * dh_s, dh).astype(BF16), lru_w_out[0].astype(BF16)]
    lands = [lax.dynamic_update_slice(lax.empty((N_DEV,) + sh.shape, BF16), sh[None], (me, 0, 0))
             for sh in shards]
    every = [1, 2, 3, 0]
    units = [([0], [0]), ([0], [1]), ([0], [2]), ([0], [3]), ([1], every), ([2], [1, 2, 0]), ([2], [3]),
             ([3, 4], every), ([5], every)]
    sems, first_sh, first_ld, started = _gather_start(shards[:1], lands[:1], units[:3], [small],
                                                      "gather_start_first")
    shards, lands = first_sh + shards[1:], first_ld + lands[1:]

    ada_b_mine = _my_slice(ada_b, me, 1).reshape(2, 1, f)
    mod_mine = _ada_mod(c_all, ada_w, ada_b_mine, "ada_mod")
    mod_all = _small_gather(_pack([mod_mine]), "gather_mod", deps=[started])

    def start_later(after):
        far_sems, far_sh, far_ld, tok = _gather_start(shards[:1], lands[:1], units[3:4], after, "gather_start_far")
        rest_units = [([i - 1 for i in members], ks) for members, ks in units[4:]]
        rest_sems, rest_sh, rest_ld, tok = _gather_start(shards[1:], lands[1:], rest_units, [tok],
                                                         "gather_start_rest")
        sems.extend(far_sems + rest_sems)
        shards[:], lands[:] = far_sh + rest_sh, far_ld + rest_ld
        return tok

    def gathered(u, after_forward, name):
        members, ks = units[u]
        fwd, shs, lnd, token = _gather_forward(
            [shards[i] for i in members], [lands[i] for i in members], ks, sems[u][0], sems[u][1],
            after_forward, "gather_forward_" + name)
        for i, sh, ld in zip(members, shs, lnd):
            shards[i], lands[i] = sh, ld

        def finish(after):
            out = _gather_finish([lands[i] for i in members], ks, fwd, after, "gather_finish_" + name)
            for i, ld in zip(members, out):
                lands[i] = ld
            return out

        return token, finish

    tok, finish_y = gathered(1, [mod_all], "sc_w_in_near_y")
    tok, finish_x = gathered(2, [tok], "sc_w_in_near_x")
    queued = start_later([tok])

    mod_all = mod_all.reshape(N_DEV, -1)
    mod_all = mod_all[:, :2 * N_DEV * f].reshape(N_DEV, 2, N_DEV, f)
    mod_all = mod_all.transpose(1, 2, 0, 3).reshape(2, N_DEV, 3 * d)
    mod = lax.dynamic_index_in_dim(mod_all, me, 1, keepdims=False)
    shift = [mod[l:l + 1, 0:d] for l in range(2)]
    scale = [mod[l:l + 1, d:2 * d] for l in range(2)]
    gate = [mod[l:l + 1, 2 * d:3 * d] for l in range(2)]
    ng = [norm_g[l:l + 1] for l in range(2)]
    fg = final_g.reshape(1, d)

    h0 = _norm_mod(x0, ng[0], scale[0], shift[0], "norm_mod_0", deps=[queued])
    proj0 = lax.empty((4, s, e), BF16)
    tok, _ = gathered(0, [h0], "sc_w_in_own")
    proj0 = _mm_proj_group(h0, lands[0], idx, 3, proj0, "mm_proj_0_own", deps=[tok])
    for u, name, finish in ((1, "near_y", finish_y), (2, "near_x", finish_x), (3, "far", None)):
        after = [proj0]
        if finish is None:
            tok, finish = gathered(u, [proj0], "sc_w_in_" + name)
            after = [tok]
        wg_in0, = finish(after)
        proj0 = _mm_proj_group(h0, wg_in0, idx, u - 1, proj0, "mm_proj_0_" + name)
    tok, finish = gathered(4, [proj0], "sc_w_out")
    yb0 = _sc_fwd(proj0, cw3, "sc_fwd", deps=[tok])
    w_out0 = finish([yb0])[0].reshape(e, d)
    x1, y0 = _mm_out(yb0, w_out0, x0, gate[0], "mm_out_0")
    tok, finish = gathered(5, [x1], "lru_w_in_near")
    h1 = _norm_mod(x1, ng[1], scale[1], shift[1], "norm_mod_1", deps=[tok])
    wg_in1, = finish([h1])
    proj1 = lax.empty((2, s, e), BF16)
    for pos, name in ((3, "own"), (0, "near_y"), (1, "near_x")):
        proj1 = _mm_proj_group(h1, wg_in1, idx, pos, proj1, "mm_proj_1_" + name)
    tok, finish = gathered(6, [proj1], "lru_w_in_far")
    wg_in1, = finish([tok])
    proj1 = _mm_proj_group(h1, wg_in1, idx, 2, proj1, "mm_proj_1_far")
    tok, finish = gathered(7, [proj1], "lru_gates")
    wg_a, wg_x = finish([tok])
    w_a = wg_a.reshape(N_DEV, heads, dh_s, dh).transpose(1, 0, 2, 3).reshape(heads, dh, dh)
    w_x = wg_x.reshape(N_DEV, heads, dh_s, dh).transpose(1, 0, 2, 3).reshape(heads, dh, dh)
    tok, finish = gathered(8, [w_a, w_x], "lru_w_out")
    yb1, hs = _lru_fwd(proj1, cw4, cb, w_a, ba, w_x, bx, lam, "lru_fwd", deps=[tok])
    w_out1 = finish([yb1])[0].reshape(e, d)
    x2, y1 = _mm_out(yb1, w_out1, x1, gate[1], "mm_out_1")
    dx2, loss_part, d_fg, dy1, dgate1 = _final_loss(x2, fg, target, y1, gate[1], "final_loss")

    def pieces(g, rows, cols):
        return g.reshape(4, 2, rows, cols)

    def by_rows(g):
        return g.reshape(heads, N_DEV, dh_s, dh).transpose(1, 0, 2, 3).reshape(N_DEV, heads * dh_s, dh)

    def pair_begin(parts, group):
        send, recv, parts, lnd, token = _pair_start(parts, "pair_start_" + group)
        return dict(parts=parts, lands=lnd, send=send, recv=recv, group=group), token

    def scatter_start(pair, names, after):
        group = pair["group"]
        parts, gots = _pair_wait(pair["parts"], pair["lands"], pair["send"], pair["recv"], after,
                                 "pair_wait_" + group)
        sums = [_pair_sum(idx, p, q, "pair_sum_" + nm) for p, q, nm in zip(parts, gots, names)]
        empties = [lax.empty(sm.shape, sm.dtype) for sm in sums]
        send, recv, sums, lnd, token = _chip_start(sums, empties, "chip_start_" + group)
        return dict(parts=parts, gots=gots, names=names, group=group, sums=sums, lands=lnd,
                    send=send, recv=recv), token

    big = {"sc_w_in": (sc_w_in, m_sc_w_in, v_sc_w_in), "sc_w_out": (sc_w_out, m_sc_w_out, v_sc_w_out),
           "lru_w_in": (lru_w_in, m_lru_w_in, v_lru_w_in), "lru_w_a": (lru_w_a, m_lru_w_a, v_lru_w_a),
           "lru_w_x": (lru_w_x, m_lru_w_x, v_lru_w_x), "lru_w_out": (lru_w_out, m_lru_w_out, v_lru_w_out)}
    big_res = {}

    def scatter_finish(rs, after):
        recvs = _chip_wait(rs["sums"], rs["lands"], rs["send"], rs["recv"], after, "chip_wait_" + rs["group"])
        done = []
        for p, q, u, nm in zip(rs["parts"], rs["gots"], recvs, rs["names"]):
            w, m, v = big[nm]
            shp2 = p.shape[2:]
            res = _adamw_reduced(idx, w.reshape(shp2), m.reshape(shp2), v.reshape(shp2), p, q, [u], "adamw_" + nm)
            big_res[nm] = [r.reshape(w.shape) for r in res]
            done.append(res[1])
        return done

    dw_out1 = _mm_tn(yb1, dy1[None], 1, "mm_dw_out_1")
    pair, tok = pair_begin([pieces(dw_out1, es, d)], "lru_w_out")
    dyb1 = _mm_nt(dy1[None], w_out1[None], BF16, "mm_dyb_1", deps=[tok])
    rs1, tok = scatter_start(pair, ["lru_w_out"], [dyb1])
    dproj1, dw_a, dw_x, vecs1 = _lru_bwd(proj1, hs, dyb1, cw4, cb, w_a, ba, w_x, bx, lam, "lru_bwd", deps=[tok])
    done = scatter_finish(rs1, [dproj1])
    dw_in1 = _mm_tn(h1, dproj1, N_DEV, "mm_dw_in_1", deps=done)
    pair, tok = pair_begin([pieces(dw_in1, d, 2 * es), pieces(by_rows(dw_a), heads * dh_s, dh),
                            pieces(by_rows(dw_x), heads * dh_s, dh)], "lru_in")
    dh1 = _mm_nt(dproj1, wg_in1, F32, "mm_dh_1", deps=[tok])
    rs2, tok = scatter_start(pair, ["lru_w_in", "lru_w_a", "lru_w_x"], [dh1])
    dx1, dscale1, dshift1, dng1, dy0, dgate0 = _norm_mod_bwd(x1, dh1, dx2, ng[1], scale[1], "norm_mod_bwd_1",
                                                             below=(y0, gate[0]), deps=[tok])
    dw_out0 = _mm_tn(yb0, dy0[None], 1, "mm_dw_out_0")
    pair, tok = pair_begin([pieces(dw_out0, es, d)], "sc_w_out")
    dyb0 = _mm_nt(dy0[None], w_out0[None], BF16, "mm_dyb_0", deps=[tok])
    rs3, tok = scatter_start(pair, ["sc_w_out"], [dyb0])
    dproj0, vecs0 = _sc_bwd(proj0, dyb0, cw3, "sc_bwd", deps=[tok])
    idx_one = jnp.stack([jnp.zeros_like(mc)] * 4 + [mc]).astype(jnp.int32)
    sc_w_in_steps = []

    def chip_step(j, pair, after):
        (part,), (got,) = _pair_wait(pair["parts"], pair["lands"], pair["send"], pair["recv"], after,
                                     "pair_wait_sc_w_in_%d" % j)
        sm = _pair_sum(idx_one, part, got, "pair_sum_sc_w_in_%d" % j, nslots=1)
        send, recv, sums, lnd, token = _chip_start([sm], [lax.empty(sm.shape, sm.dtype)],
                                                   "chip_start_sc_w_in_%d" % j, flips=(j,))
        sc_w_in_steps.append((sums, lnd, send, recv, j))
        return token

    pending, done = None, []
    for j in (3, 1, 2, 0):
        part = _mm_tn_group(h0, dproj0, idx, (j - 1) % 4, 2, "mm_dw_in_0_%d" % j, deps=done)[None]
        pair, tok = pair_begin([part], "sc_w_in_%d" % j)
        if j == 3:
            done = scatter_finish(rs2, [chip_step(j, pair, [tok])])
            continue
        done = [tok]
        if pending is not None:
            done.append(chip_step(pending[0], pending[1], [tok]))
        pending = (j, pair)
    dh0 = _mm_nt(dproj0, wg_in0, F32, "mm_dh_0", deps=done)
    pair = pending[1]
    (part,), (got,) = _pair_wait(pair["parts"], pair["lands"], pair["send"], pair["recv"], [dh0],
                                 "pair_wait_sc_w_in_0")
    dx0, dscale0, dshift0, dng0 = _norm_mod_bwd(x0, dh0, dx1, ng[0], scale[0], "norm_mod_bwd_0")
    done = scatter_finish(rs3, [dx0])
    dmod_mine = jnp.concatenate([dshift0, dscale0, dgate0, dshift1, dscale1, dgate1], axis=1)
    end_shapes = [(LANES,), (2, 3 * d), (2, d), (d,), (8, e), (16, e)]
    end_all = _small_gather(
        _pack([loss_part, dmod_mine, jnp.concatenate([dng0, dng1], axis=0), d_fg, vecs0, vecs1]),
        "gather_small_grads", deps=done)
    end_sum = _device_sum(end_all, "sum_small_grads").reshape(-1)
    loss_v, g_ada_b, g_norm_g, g_final_g, sum0, sum1 = _unpack(end_sum, end_shapes)
    loss = loss_v[0]
    dmod_all = _unpack(end_all.reshape(N_DEV, -1), end_shapes)[1].transpose(1, 0, 2)
    dmod_cols = _my_slice(dmod_all, me, 2)
    g_ada_w = _ada_grad(c_all.T, dmod_cols, "ada_grad")

    g_sc_conv_w = _my_slice(sum0[0:3], me, 1)
    g_lru_b_a = _my_slice(sum1[0].reshape(heads, dh), me, 1)
    g_lru_b_x = _my_slice(sum1[1].reshape(heads, dh), me, 1)
    g_lru_lambda = _my_slice(sum1[2:3], me, 1)
    g_lru_conv_b = _my_slice(sum1[3:4], me, 1)
    g_lru_conv_w = _my_slice(sum1[4:8], me, 1)

    ada_res = _adamw(ada_w.reshape(2 * d, f), g_ada_w.reshape(2 * d, f), m_ada_w.reshape(2 * d, f),
                     v_ada_w.reshape(2 * d, f), "adamw_ada_w")
    ada_out = [g_ada_w] + [r.reshape(ada_w.shape) for r in ada_res]

    small_w = [norm_g, ada_b, final_g, sc_conv_w, lru_conv_w, lru_conv_b, lru_b_a, lru_b_x, lru_lambda]
    small_m = [m_norm_g, m_ada_b, m_final_g, m_sc_conv_w, m_lru_conv_w, m_lru_conv_b, m_lru_b_a, m_lru_b_x,
               m_lru_lambda]
    small_v = [v_norm_g, v_ada_b, v_final_g, v_sc_conv_w, v_lru_conv_w, v_lru_conv_b, v_lru_b_a, v_lru_b_x,
               v_lru_lambda]
    small_g = [g_norm_g, g_ada_b, g_final_g, g_sc_conv_w, g_lru_conv_w, g_lru_conv_b, g_lru_b_a, g_lru_b_x,
               g_lru_lambda]
    small_g = [g.reshape(w.shape) for g, w in zip(small_g, small_w)]
    shapes = [w.shape for w in small_w]
    packed = _adamw(_pack(small_w), _pack(small_g), _pack(small_m), _pack(small_v), "adamw_small")
    small_out = [small_g] + [_unpack(p.reshape(-1), shapes) for p in packed]

    after = [packed[0], ada_res[0]]
    recvs = []
    for sums, lnd, send, recv, j in sc_w_in_steps:
        recvs += _chip_wait(sums, lnd, send, recv, after, "chip_wait_sc_w_in_%d" % j)
    shp2 = part.shape[2:]
    res = _adamw_reduced(idx_one, sc_w_in.reshape(shp2), m_sc_w_in.reshape(shp2), v_sc_w_in.reshape(shp2),
                         part, got, recvs, "adamw_sc_w_in")
    big_res["sc_w_in"] = [r.reshape(sc_w_in.shape) for r in res]
    big_out = [big_res[nm] for nm in ("sc_w_in", "sc_w_out", "lru_w_in", "lru_w_a", "lru_w_x", "lru_w_out")]

    def small(kind, i):
        return small_out[kind][i]

    def bigw(kind, i):
        return big_out[i][kind]

    outs = [loss, dx0[None]]
    for kind in range(4):
        outs += [small(kind, 0), ada_out[kind], small(kind, 1), bigw(kind, 0), small(kind, 3), bigw(kind, 1),
                 bigw(kind, 2), small(kind, 4), small(kind, 5), bigw(kind, 3), small(kind, 6), bigw(kind, 4),
                 small(kind, 7), small(kind, 8), bigw(kind, 5), small(kind, 2)]
    return tuple(outs)
```

```python
import math

import jax
import jax.numpy as jnp
from jax import lax
from jax.experimental import pallas as pl
from jax.experimental.pallas import tpu as pltpu

N_DEV = 8
LANES = 128
EPS = 1e-6
RGLRU_C = 8.0
ADAM_LR = 0.001
ADAM_B1 = 0.9
ADAM_B2 = 0.999
ADAM_EPS = 1e-08
ADAM_WD = 0.01
ADAM_STEP = 10
VMEM_LIMIT = 56 * 1024 * 1024
MESH = pl.DeviceIdType.MESH
F32 = jnp.float32
BF16 = jnp.bfloat16
ANY = pl.BlockSpec(memory_space=pl.ANY)
HBM = pl.BlockSpec(memory_space=pltpu.HBM)
SEM = pl.BlockSpec(memory_space=pltpu.SEMAPHORE)
VMEM_SPEC = pl.BlockSpec(memory_space=pltpu.VMEM)
EFFECT = pltpu.SideEffectType.DATAFLOW_SIDE_EFFECTING
TOKEN = jax.ShapeDtypeStruct((8, LANES), jnp.float32)


def _tile(n, pref):
    t = min(n, pref)
    assert n % t == 0, (n, pref)
    return t


def _params(*sem):
    return pltpu.CompilerParams(dimension_semantics=sem, vmem_limit_bytes=VMEM_LIMIT)


def _position():
    return lax.axis_index("x"), lax.axis_index("y"), lax.axis_index("c")


def _flip(x, y, k):
    return (1 - x if k & 2 else x), (1 - y if k & 1 else y)


def _after(body, n_in, deps):
    if not deps:
        return body

    def wrapped(*refs):
        return body(*refs[:n_in], *refs[n_in + len(deps):])

    return wrapped


def _small_gather(v, name, deps=()):
    rows = v.shape[0]

    def body(v_ref, out_ref, send_sems, recv_sems):
        x, y, c = _position()
        me = 4 * x + 2 * y + c
        out_ref[me] = v_ref[...]
        copies = []
        for k in range(1, N_DEV):
            px, py = _flip(x, y, k >> 1)
            pc = 1 - c if k & 1 else c
            cp = pltpu.make_async_remote_copy(
                src_ref=v_ref, dst_ref=out_ref.at[me],
                send_sem=send_sems.at[k - 1], recv_sem=recv_sems.at[k - 1],
                device_id=(px, py, pc), device_id_type=MESH)
            cp.start()
            copies.append((cp, 4 * px + 2 * py + pc))
        for k, (cp, peer) in enumerate(copies):
            pltpu.make_async_remote_copy(
                src_ref=v_ref, dst_ref=out_ref.at[peer],
                send_sem=send_sems.at[k], recv_sem=recv_sems.at[k],
                device_id=(x, y, c), device_id_type=MESH).wait_recv()
        for cp, _ in copies:
            cp.wait_send()

    return pl.pallas_call(
        _after(body, 1, deps), name=name,
        out_shape=jax.ShapeDtypeStruct((N_DEV, rows, LANES), F32),
        in_specs=[VMEM_SPEC] + [ANY] * len(deps), out_specs=VMEM_SPEC,
        scratch_shapes=[pltpu.SemaphoreType.DMA((N_DEV - 1,)),
                        pltpu.SemaphoreType.DMA((N_DEV - 1,))],
        compiler_params=pltpu.CompilerParams(vmem_limit_bytes=VMEM_LIMIT),
    )(v, *deps)


def _hbm(a):
    return pltpu.with_memory_space_constraint(a, pltpu.HBM)


def _hbm_like(arrays):
    return [pltpu.HBM(a.shape, a.dtype) for a in arrays]


def _remote(src, dst, send, recv, to):
    return pltpu.make_async_remote_copy(src_ref=src, dst_ref=dst, send_sem=send, recv_sem=recv,
                                        device_id=to, device_id_type=MESH)


def _gather_start(shards, lands, units, after, name):
    n, nu = len(shards), len(units)

    def body(*refs):
        ins, lnd = refs[:n], refs[n:2 * n]
        sems = refs[2 * n + len(after):2 * n + len(after) + 2 * nu]
        token = refs[-1]
        x, y, c = _position()
        me = 4 * x + 2 * y + c
        targets = [(x, y, 1 - c)] + [(px, py, c) for px, py in (_flip(x, y, k) for k in (1, 2, 3))]
        for u, (members, ks) in enumerate(units):
            for slot, i in enumerate(members):
                for ki, k in enumerate(ks):
                    at = len(ks) * slot + ki
                    _remote(ins[i], lnd[i].at[me], sems[2 * u].at[at], sems[2 * u + 1].at[at], targets[k]).start()
        token[...] = jnp.zeros_like(token)

    sem_shapes = []
    for members, ks in units:
        count = len(members) * len(ks)
        sem_shapes += [pltpu.SemaphoreType.DMA((count,)), pltpu.SemaphoreType.DMA((count,))]
    out = pl.pallas_call(
        body, name=name,
        out_shape=sem_shapes + _hbm_like(shards) + _hbm_like(lands) + [TOKEN],
        in_specs=[HBM] * (2 * n) + [ANY] * len(after),
        out_specs=[SEM] * (2 * nu) + [HBM] * (2 * n) + [VMEM_SPEC],
        input_output_aliases={i: 2 * nu + i for i in range(2 * n)},
        compiler_params=pltpu.CompilerParams(has_side_effects=EFFECT),
    )(*[_hbm(s) for s in shards], *[_hbm(l) for l in lands], *after)
    sems = [(out[2 * u], out[2 * u + 1]) for u in range(nu)]
    return sems, list(out[2 * nu:2 * nu + n]), list(out[2 * nu + n:2 * nu + 2 * n]), out[-1]


def _gather_forward(shards, lands, ks, send, recv, after, name):
    m = len(shards)
    hops = [k for k in ks if k]
    nsem = 2 if hops else 0

    def body(*refs):
        ins, lnd = refs[:m], refs[m:2 * m]
        send_ref, recv_ref = refs[2 * m], refs[2 * m + 1]
        outs = refs[2 * m + 2 + len(after):]
        token = refs[-1]
        x, y, c = _position()
        me = (x, y, c)
        for slot in range(m):
            for ki, k in enumerate(ks):
                at = len(ks) * slot + ki
                if k:
                    px, py = _flip(x, y, k)
                    block = lnd[slot].at[4 * px + 2 * py + c]
                else:
                    block = lnd[slot].at[4 * x + 2 * y + (1 - c)]
                arrival = _remote(ins[slot], block, send_ref.at[at], recv_ref.at[at], me)
                arrival.wait_recv()
                if k:
                    fat = len(hops) * slot + hops.index(k)
                    _remote(block, block, outs[0].at[fat], outs[1].at[fat], (x, y, 1 - c)).start()
                arrival.wait_send()
        token[...] = jnp.zeros_like(token)

    count = len(hops) * m
    sem_shapes = [pltpu.SemaphoreType.DMA((count,)), pltpu.SemaphoreType.DMA((count,))] if hops else []
    out = pl.pallas_call(
        body, name=name,
        out_shape=sem_shapes + _hbm_like(shards) + _hbm_like(lands) + [TOKEN],
        in_specs=[HBM] * (2 * m) + [SEM, SEM] + [ANY] * len(after),
        out_specs=[SEM] * nsem + [HBM] * (2 * m) + [VMEM_SPEC],
        input_output_aliases={i: nsem + i for i in range(2 * m)},
        compiler_params=pltpu.CompilerParams(has_side_effects=EFFECT),
    )(*shards, *lands, send, recv, *after)
    fwd = (out[0], out[1]) if hops else None
    return fwd, list(out[nsem:nsem + m]), list(out[nsem + m:nsem + 2 * m]), out[-1]


def _gather_finish(lands, ks, fwd, after, name):
    m = len(lands)
    hops = [k for k in ks if k]

    def body(*refs):
        lnd = refs[:m]
        fsend_ref, frecv_ref = refs[m], refs[m + 1]
        x, y, c = _position()
        for slot in range(m):
            for fi, k in enumerate(hops):
                px, py = _flip(x, y, k)
                sent = lnd[slot].at[4 * px + 2 * py + c]
                came = lnd[slot].at[4 * px + 2 * py + (1 - c)]
                fat = len(hops) * slot + fi
                cp = _remote(sent, came, fsend_ref.at[fat], frecv_ref.at[fat], (x, y, c))
                cp.wait_recv()
                cp.wait_send()

    out = pl.pallas_call(
        body, name=name,
        out_shape=_hbm_like(lands),
        in_specs=[HBM] * m + [SEM, SEM] + [ANY] * len(after), out_specs=[HBM] * m,
        input_output_aliases={i: i for i in range(m)},
        compiler_params=pltpu.CompilerParams(has_side_effects=EFFECT),
    )(*lands, fwd[0], fwd[1], *after)
    return list(out)


def _pair_start(parts, name):
    n = len(parts)
    lands = [lax.empty((p.shape[0], 1) + p.shape[2:], p.dtype) for p in parts]

    def body(*refs):
        ins, lnd = refs[:n], refs[n:2 * n]
        send_ref, recv_ref = refs[2 * n], refs[2 * n + 1]
        token = refs[-1]
        x, y, c = _position()
        for i in range(n):
            _remote(ins[i].at[:, pl.ds(1 - c, 1)], lnd[i], send_ref.at[i], recv_ref.at[i], (x, y, 1 - c)).start()
        token[...] = jnp.zeros_like(token)

    out = pl.pallas_call(
        body, name=name,
        out_shape=[pltpu.SemaphoreType.DMA((n,)), pltpu.SemaphoreType.DMA((n,))]
        + _hbm_like(parts) + _hbm_like(lands) + [TOKEN],
        in_specs=[HBM] * (2 * n), out_specs=[SEM, SEM] + [HBM] * (2 * n) + [VMEM_SPEC],
        input_output_aliases={i: 2 + i for i in range(2 * n)},
        compiler_params=pltpu.CompilerParams(has_side_effects=EFFECT),
    )(*[_hbm(p) for p in parts], *[_hbm(l) for l in lands])
    return out[0], out[1], list(out[2:2 + n]), list(out[2 + n:2 + 2 * n]), out[-1]


def _pair_wait(parts, lands, send, recv, after, name):
    n = len(parts)

    def body(*refs):
        ins, lnd = refs[:n], refs[n:2 * n]
        send_ref, recv_ref = refs[2 * n], refs[2 * n + 1]
        x, y, c = _position()
        for i in range(n):
            cp = _remote(ins[i].at[:, pl.ds(1 - c, 1)], lnd[i], send_ref.at[i], recv_ref.at[i], (x, y, c))
            cp.wait_recv()
            cp.wait_send()

    out = pl.pallas_call(
        body, name=name,
        out_shape=_hbm_like(parts) + _hbm_like(lands),
        in_specs=[HBM] * (2 * n) + [SEM, SEM] + [ANY] * len(after), out_specs=[HBM] * (2 * n),
        input_output_aliases={i: i for i in range(2 * n)},
        compiler_params=pltpu.CompilerParams(has_side_effects=EFFECT),
    )(*parts, *lands, send, recv, *after)
    return list(out[:n]), list(out[n:])


def _chip_start(sums, lands, name, flips=(1, 2, 3)):
    n, ns = len(sums), len(flips)

    def body(*refs):
        ins, lnd = refs[:n], refs[n:2 * n]
        send_ref, recv_ref = refs[2 * n], refs[2 * n + 1]
        token = refs[-1]
        x, y, c = _position()
        for i in range(n):
            for j, flip in enumerate(flips):
                px, py = _flip(x, y, flip)
                _remote(ins[i].at[j], lnd[i].at[j], send_ref.at[ns * i + j], recv_ref.at[ns * i + j],
                        (px, py, c)).start()
        token[...] = jnp.zeros_like(token)

    out = pl.pallas_call(
        body, name=name,
        out_shape=[pltpu.SemaphoreType.DMA((ns * n,)), pltpu.SemaphoreType.DMA((ns * n,))]
        + _hbm_like(sums) + _hbm_like(lands) + [TOKEN],
        in_specs=[HBM] * (2 * n), out_specs=[SEM, SEM] + [HBM] * (2 * n) + [VMEM_SPEC],
        input_output_aliases={i: 2 + i for i in range(2 * n)},
        compiler_params=pltpu.CompilerParams(has_side_effects=EFFECT),
    )(*[_hbm(s) for s in sums], *[_hbm(l) for l in lands])
    return out[0], out[1], out[2:2 + n], out[2 + n:2 + 2 * n], out[-1]


def _chip_wait(sums, lands, send, recv, after, name):
    n, ns = len(sums), sums[0].shape[0]

    def body(*refs):
        ins, lnd = refs[:n], refs[n:2 * n]
        send_ref, recv_ref = refs[2 * n], refs[2 * n + 1]
        x, y, c = _position()
        for i in range(n):
            for j in range(ns):
                cp = _remote(ins[i].at[j], lnd[i].at[j], send_ref.at[ns * i + j], recv_ref.at[ns * i + j], (x, y, c))
                cp.wait_recv()
                cp.wait_send()

    out = pl.pallas_call(
        body, name=name,
        out_shape=_hbm_like(sums) + _hbm_like(lands),
        in_specs=[HBM] * (2 * n) + [SEM, SEM] + [ANY] * len(after), out_specs=[HBM] * (2 * n),
        input_output_aliases={i: i for i in range(2 * n)},
        compiler_params=pltpu.CompilerParams(has_side_effects=EFFECT),
    )(*sums, *lands, send, recv, *after)
    return list(out[n:])


def _pair_sum(idx, part, got, name, nslots=3):
    _, _, rows, cols = part.shape
    tr = _tile(rows, 1024)

    def body(idx_ref, p_ref, q_ref, o_ref):
        o_ref[...] = (p_ref[...].astype(F32) + q_ref[...].astype(F32)).astype(o_ref.dtype)

    grid_spec = pltpu.PrefetchScalarGridSpec(
        num_scalar_prefetch=1, grid=(nslots, rows // tr),
        in_specs=[pl.BlockSpec((None, None, tr, cols), lambda j, r, idx: (idx[j], idx[4], r, 0)),
                  pl.BlockSpec((None, None, tr, cols), lambda j, r, idx: (idx[j], 0, r, 0))],
        out_specs=pl.BlockSpec((None, tr, cols), lambda j, r, idx: (j, r, 0)))
    return pl.pallas_call(
        body, name=name, grid_spec=grid_spec,
        out_shape=jax.ShapeDtypeStruct((nslots, rows, cols), part.dtype),
        compiler_params=_params("arbitrary", "arbitrary"),
    )(idx, part, got)


def _mm_proj(h, wg, groups, name):
    s, k = h.shape
    nchunk, _, n = wg.shape
    e = nchunk * n // groups
    tn = _tile(min(n, e), 512)

    def body(h_ref, w_ref, o_ref):
        o_ref[...] = jnp.dot(h_ref[...], w_ref[...], preferred_element_type=F32).astype(o_ref.dtype)

    return pl.pallas_call(
        body, name=name, grid=(nchunk * n // tn,),
        in_specs=[pl.BlockSpec((s, k), lambda j: (0, 0)),
                  pl.BlockSpec((None, k, tn), lambda j: ((j * tn) // n, 0, ((j * tn) % n) // tn))],
        out_specs=pl.BlockSpec((None, s, tn), lambda j: ((j * tn) // e, 0, ((j * tn) % e) // tn)),
        out_shape=jax.ShapeDtypeStruct((groups, s, e), BF16),
        compiler_params=_params("arbitrary"),
    )(h, wg)


def _mm_proj_group(h, wg, idx, pos, prev, name, deps=()):
    s, k = h.shape
    _, _, n = wg.shape
    _, _, e = prev.shape
    tn = _tile(n, 512)
    nd = len(deps)

    def body(idx_ref, h_ref, w_ref, prev_ref, *rest):
        o_ref = rest[nd]
        o_ref[...] = jnp.dot(h_ref[...], w_ref[...], preferred_element_type=F32).astype(o_ref.dtype)

    def col(j, idx):
        return idx[pos] * (2 * n) + j * tn

    grid_spec = pltpu.PrefetchScalarGridSpec(
        num_scalar_prefetch=1, grid=(2 * n // tn,),
        in_specs=[pl.BlockSpec((s, k), lambda j, idx: (0, 0)),
                  pl.BlockSpec((None, k, tn), lambda j, idx: (col(j, idx) // n, 0, (col(j, idx) % n) // tn)),
                  ANY] + [ANY] * nd,
        out_specs=pl.BlockSpec((None, s, tn), lambda j, idx: (col(j, idx) // e, 0, (col(j, idx) % e) // tn)))
    return pl.pallas_call(
        body, name=name, grid_spec=grid_spec,
        out_shape=jax.ShapeDtypeStruct(prev.shape, prev.dtype),
        input_output_aliases={3: 0},
        compiler_params=_params("arbitrary"),
    )(idx, h, wg, prev, *deps)


def _mm_out(yb, w, x, gate, name):
    s, k = yb.shape
    d = w.shape[1]
    tn = _tile(d, 512)
    tk = _tile(k, 2048)
    nk = k // tk

    def body(a_ref, w_ref, x_ref, g_ref, xo_ref, y_ref, acc_ref):
        kk = pl.program_id(1)

        @pl.when(kk == 0)
        def _():
            acc_ref[...] = jnp.zeros_like(acc_ref)

        acc_ref[...] += jnp.dot(a_ref[...], w_ref[...], preferred_element_type=F32)

        @pl.when(kk == nk - 1)
        def _():
            y = acc_ref[...]
            y_ref[...] = y.astype(y_ref.dtype)
            xo_ref[...] = x_ref[...] + g_ref[...] * y

    return pl.pallas_call(
        body, name=name, grid=(d // tn, nk),
        in_specs=[pl.BlockSpec((s, tk), lambda j, kk: (0, kk)),
                  pl.BlockSpec((tk, tn), lambda j, kk: (kk, j)),
                  pl.BlockSpec((s, tn), lambda j, kk: (0, j)),
                  pl.BlockSpec((1, tn), lambda j, kk: (0, j))],
        out_specs=[pl.BlockSpec((s, tn), lambda j, kk: (0, j)),
                   pl.BlockSpec((s, tn), lambda j, kk: (0, j))],
        out_shape=[jax.ShapeDtypeStruct((s, d), F32), jax.ShapeDtypeStruct((s, d), BF16)],
        scratch_shapes=[pltpu.VMEM((s, tn), F32)],
        compiler_params=_params("arbitrary", "arbitrary"),
    )(yb, w, x, gate)


def _mm_nt(a3, w3, out_dtype, name, deps=()):
    g, s, ea = a3.shape
    cw, n, nw = w3.shape
    total = g * ea
    assert total == cw * nw
    tk = _tile(min(ea, nw), 1024)
    tn = _tile(n, 1024)
    nk = total // tk

    def body(a_ref, w_ref, o_ref, acc_ref):
        kk = pl.program_id(1)

        @pl.when(kk == 0)
        def _():
            acc_ref[...] = jnp.zeros_like(acc_ref)

        acc_ref[...] += lax.dot_general(a_ref[...], w_ref[...], (((1,), (1,)), ((), ())),
                                        preferred_element_type=F32)

        @pl.when(kk == nk - 1)
        def _():
            o_ref[...] = acc_ref[...].astype(o_ref.dtype)

    return pl.pallas_call(
        _after(body, 2, deps), name=name, grid=(n // tn, nk),
        in_specs=[pl.BlockSpec((None, s, tk), lambda j, kk: ((kk * tk) // ea, 0, ((kk * tk) % ea) // tk)),
                  pl.BlockSpec((None, tn, tk), lambda j, kk: ((kk * tk) // nw, j, ((kk * tk) % nw) // tk))]
        + [ANY] * len(deps),
        out_specs=pl.BlockSpec((s, tn), lambda j, kk: (0, j)),
        out_shape=jax.ShapeDtypeStruct((s, n), out_dtype),
        scratch_shapes=[pltpu.VMEM((s, tn), F32)],
        compiler_params=_params("arbitrary", "arbitrary"),
    )(a3, w3, *deps)


def _mm_tn(a, b3, nchunk, name, deps=()):
    s, ka = a.shape
    g, _, eb = b3.shape
    n = g * eb // nchunk
    tm = _tile(ka, 1024)
    tn = _tile(min(n, eb), 1024)

    def body(a_ref, b_ref, o_ref, at_ref):
        @pl.when(pl.program_id(1) == 0)
        def _():
            at_ref[...] = a_ref[...].astype(F32).T.astype(at_ref.dtype)

        o_ref[...] = jnp.dot(at_ref[...], b_ref[...], preferred_element_type=F32).astype(o_ref.dtype)

    return pl.pallas_call(
        _after(body, 2, deps), name=name, grid=(ka // tm, g * eb // tn),
        in_specs=[pl.BlockSpec((s, tm), lambda i, j: (0, i)),
                  pl.BlockSpec((None, s, tn), lambda i, j: ((j * tn) // eb, 0, ((j * tn) % eb) // tn))]
        + [ANY] * len(deps),
        out_specs=pl.BlockSpec((None, tm, tn), lambda i, j: ((j * tn) // n, i, ((j * tn) % n) // tn)),
        out_shape=jax.ShapeDtypeStruct((nchunk, ka, n), BF16),
        scratch_shapes=[pltpu.VMEM((tm, s), BF16)],
        compiler_params=_params("arbitrary", "arbitrary"),
    )(a, b3, *deps)


def _mm_tn_group(a, b3, idx, pos, nchunk, name, deps=()):
    s, ka = a.shape
    _, _, eb = b3.shape
    n = eb // nchunk
    tm = _tile(ka, 1024)
    tn = _tile(n, 1024)
    nd = len(deps)

    def body(idx_ref, a_ref, b_ref, *rest):
        o_ref, at_ref = rest[nd:]

        @pl.when(pl.program_id(1) == 0)
        def _():
            at_ref[...] = a_ref[...].astype(F32).T.astype(at_ref.dtype)

        o_ref[...] = jnp.dot(at_ref[...], b_ref[...], preferred_element_type=F32).astype(o_ref.dtype)

    grid_spec = pltpu.PrefetchScalarGridSpec(
        num_scalar_prefetch=1, grid=(ka // tm, eb // tn),
        in_specs=[pl.BlockSpec((s, tm), lambda i, j, idx: (0, i)),
                  pl.BlockSpec((None, s, tn), lambda i, j, idx: (idx[pos], 0, j))] + [ANY] * nd,
        out_specs=pl.BlockSpec((None, tm, tn), lambda i, j, idx: ((j * tn) // n, i, ((j * tn) % n) // tn)),
        scratch_shapes=[pltpu.VMEM((tm, s), BF16)])
    return pl.pallas_call(
        body, name=name, grid_spec=grid_spec,
        out_shape=jax.ShapeDtypeStruct((nchunk, ka, n), BF16),
        compiler_params=_params("arbitrary", "arbitrary"),
    )(idx, a, b3, *deps)


def _sigmoid(z):
    return jax.nn.sigmoid(z)


def _shift_down(v, k, fill=0.0, period=None):
    if k == 0:
        return v
    row = lax.broadcasted_iota(jnp.int32, v.shape, 0)
    if period is not None:
        row = row & (period - 1)
    return jnp.where(row >= k, pltpu.roll(v, k, 0), fill)


def _shift_up(v, k, fill=0.0, period=None):
    if k == 0:
        return v
    s = v.shape[0]
    row = lax.broadcasted_iota(jnp.int32, v.shape, 0)
    if period is not None:
        row, s = row & (period - 1), period
    return jnp.where(row < s - k, pltpu.roll(v, v.shape[0] - k, 0), fill)


SCAN_BLOCK = 64


def _scan(a, b, shift):
    s = a.shape[0]
    blk = min(SCAN_BLOCK, s)
    k = 1
    while k < blk:
        b = a * shift(b, k, 0.0, blk) + b
        a = a * shift(a, k, 1.0, blk)
        k *= 2
    nblk = s // blk
    forward = shift is _shift_down
    order = range(nblk) if forward else range(nblk - 1, -1, -1)
    edge = blk - 1 if forward else 0
    out = [None] * nblk
    carry = None
    for i in order:
        h = b[i * blk:(i + 1) * blk]
        if carry is not None:
            h = a[i * blk:(i + 1) * blk] * carry + h
        carry = h[edge:edge + 1]
        out[i] = h
    return jnp.concatenate(out, axis=0) if nblk > 1 else out[0]


def _norm_mod(x, g, scale, shift, name, deps=()):
    s, d = x.shape
    ts = _tile(s, 256)

    def body(x_ref, g_ref, sc_ref, sh_ref, h_ref):
        xv = x_ref[...]
        rstd = lax.rsqrt(jnp.mean(xv * xv, axis=-1, keepdims=True) + EPS)
        nrm = xv * rstd * g_ref[...]
        h_ref[...] = (nrm * (1.0 + sc_ref[...]) + sh_ref[...]).astype(h_ref.dtype)

    vec = pl.BlockSpec((1, d), lambda i: (0, 0))
    return pl.pallas_call(
        _after(body, 4, deps), name=name, grid=(s // ts,),
        in_specs=[pl.BlockSpec((ts, d), lambda i: (i, 0)), vec, vec, vec] + [ANY] * len(deps),
        out_specs=pl.BlockSpec((ts, d), lambda i: (i, 0)),
        out_shape=jax.ShapeDtypeStruct((s, d), BF16),
        compiler_params=_params("arbitrary"),
    )(x, g, scale, shift, *deps)


def _gate_terms(dx, y_ref, gate_ref, dy_ref, dgate_ref):
    dy_ref[...] = (dx * gate_ref[...]).astype(dy_ref.dtype)
    dgate_ref[...] += jnp.sum(dx * y_ref[...].astype(F32), axis=0, keepdims=True)


def _norm_mod_bwd(x, dh, dx_res, g, scale, name, below=None, deps=()):
    s, d = x.shape
    ts = _tile(s, 256)
    nb = 2 if below is not None else 0

    def body(x_ref, dh_ref, dr_ref, g_ref, sc_ref, *rest):
        dx_ref, dsc_ref, dsh_ref, dg_ref = rest[nb:nb + 4]

        @pl.when(pl.program_id(0) == 0)
        def _():
            for ref in rest[nb + 1:nb + 4] + rest[nb + 5:]:
                ref[...] = jnp.zeros_like(ref)

        xv = x_ref[...]
        dh_v = dh_ref[...].astype(F32)
        gv = g_ref[...]
        rstd = lax.rsqrt(jnp.mean(xv * xv, axis=-1, keepdims=True) + EPS)
        xhat = xv * rstd
        dsc_ref[...] += jnp.sum(dh_v * xhat * gv, axis=0, keepdims=True)
        dsh_ref[...] += jnp.sum(dh_v, axis=0, keepdims=True)
        dn = dh_v * (1.0 + sc_ref[...])
        dg_ref[...] += jnp.sum(dn * xhat, axis=0, keepdims=True)
        dxhat = dn * gv
        proj = jnp.mean(dxhat * xhat, axis=-1, keepdims=True)
        dx = dr_ref[...] + rstd * (dxhat - xhat * proj)
        dx_ref[...] = dx
        if nb:
            _gate_terms(dx, rest[0], rest[1], rest[nb + 4], rest[nb + 5])

    row = pl.BlockSpec((ts, d), lambda i: (i, 0))
    vec = pl.BlockSpec((1, d), lambda i: (0, 0))
    extra = list(below) if nb else []
    return pl.pallas_call(
        _after(body, 5 + nb, deps), name=name, grid=(s // ts,),
        in_specs=[row, row, row, vec, vec] + [row, vec][:nb] + [ANY] * len(deps),
        out_specs=[row, vec, vec, vec] + [row, vec][:nb],
        out_shape=[jax.ShapeDtypeStruct((s, d), F32)] + [jax.ShapeDtypeStruct((1, d), F32)] * 3
        + [jax.ShapeDtypeStruct((s, d), BF16), jax.ShapeDtypeStruct((1, d), F32)][:nb],
        compiler_params=_params("arbitrary"),
    )(x, dh, dx_res, g, scale, *extra, *deps)


def _final_loss(x, g, target, y, gate, name):
    s, d = x.shape
    ts = _tile(s, 256)

    def body(x_ref, g_ref, t_ref, y_ref, gate_ref, dx_ref, loss_ref, dg_ref, dy_ref, dgate_ref):
        @pl.when(pl.program_id(0) == 0)
        def _():
            loss_ref[...] = jnp.zeros_like(loss_ref)
            dg_ref[...] = jnp.zeros_like(dg_ref)
            dgate_ref[...] = jnp.zeros_like(dgate_ref)

        xv = x_ref[...]
        gv = g_ref[...]
        rstd = lax.rsqrt(jnp.mean(xv * xv, axis=-1, keepdims=True) + EPS)
        xhat = xv * rstd
        err = xhat * gv - t_ref[...]
        loss_ref[...] += 0.5 * jnp.sum(jnp.mean(err * err, axis=-1, keepdims=True))
        dy = err * (1.0 / d)
        dg_ref[...] += jnp.sum(dy * xhat, axis=0, keepdims=True)
        dxhat = dy * gv
        proj = jnp.mean(dxhat * xhat, axis=-1, keepdims=True)
        dx = rstd * (dxhat - xhat * proj)
        dx_ref[...] = dx
        _gate_terms(dx, y_ref, gate_ref, dy_ref, dgate_ref)

    row = pl.BlockSpec((ts, d), lambda i: (i, 0))
    vec = pl.BlockSpec((1, d), lambda i: (0, 0))
    return pl.pallas_call(
        body, name=name, grid=(s // ts,),
        in_specs=[row, vec, row, row, vec],
        out_specs=[row, pl.BlockSpec((1, LANES), lambda i: (0, 0)), vec, row, vec],
        out_shape=[jax.ShapeDtypeStruct((s, d), F32), jax.ShapeDtypeStruct((1, LANES), F32),
                   jax.ShapeDtypeStruct((1, d), F32), jax.ShapeDtypeStruct((s, d), BF16),
                   jax.ShapeDtypeStruct((1, d), F32)],
        compiler_params=_params("arbitrary"),
    )(x, g, target, y, gate)


def _conv(v, w_ref, width):
    out = w_ref[width - 1:width, :] * v
    for k in range(width - 1):
        out = out + w_ref[k:k + 1, :] * _shift_down(v, width - 1 - k)
    return out


def _sc_fwd(proj, conv_w, name, deps=()):
    _, s, e = proj.shape
    te = _tile(e, 256)
    width = conv_w.shape[0]

    def body(b_ref, c_ref, v_ref, g_ref, w_ref, o_ref):
        cv = c_ref[...].astype(F32) * v_ref[...].astype(F32)
        u = _conv(cv, w_ref, width)
        gv = g_ref[...].astype(F32)
        o_ref[...] = (b_ref[...].astype(F32) * u * (gv * _sigmoid(gv))).astype(o_ref.dtype)

    def part(q):
        return pl.BlockSpec((None, s, te), lambda j, q=q: (q, 0, j))

    return pl.pallas_call(
        _after(body, 5, deps), name=name, grid=(e // te,),
        in_specs=[part(0), part(1), part(2), part(3), pl.BlockSpec((width, te), lambda j: (0, j))]
        + [ANY] * len(deps),
        out_specs=pl.BlockSpec((s, te), lambda j: (0, j)),
        out_shape=jax.ShapeDtypeStruct((s, e), BF16),
        compiler_params=_params("arbitrary"),
    )(proj, proj, proj, proj, conv_w, *deps)


def _sc_bwd(proj, dyb, conv_w, name, deps=()):
    _, s, e = proj.shape
    te = _tile(e, 256)
    width = conv_w.shape[0]

    def body(b_ref, c_ref, v_ref, g_ref, dy_ref, w_ref, dp_ref, vec_ref):
        bv = b_ref[...].astype(F32)
        cvl = c_ref[...].astype(F32)
        vv = v_ref[...].astype(F32)
        gv = g_ref[...].astype(F32)
        dyv = dy_ref[...].astype(F32)
        cv = cvl * vv
        u = _conv(cv, w_ref, width)
        sg = _sigmoid(gv)
        silu = gv * sg
        dp_ref[0] = (dyv * u * silu).astype(dp_ref.dtype)
        du = dyv * bv * silu
        dp_ref[3] = (dyv * bv * u * (sg * (1.0 + gv * (1.0 - sg)))).astype(dp_ref.dtype)
        dcv = w_ref[width - 1:width, :] * du
        vec_ref[...] = jnp.zeros_like(vec_ref)
        vec_ref[width - 1:width, :] = jnp.sum(du * cv, axis=0, keepdims=True)
        for k in range(width - 1):
            sh = width - 1 - k
            dcv = dcv + w_ref[k:k + 1, :] * _shift_up(du, sh)
            vec_ref[k:k + 1, :] = jnp.sum(du * _shift_down(cv, sh), axis=0, keepdims=True)
        dp_ref[1] = (dcv * vv).astype(dp_ref.dtype)
        dp_ref[2] = (dcv * cvl).astype(dp_ref.dtype)

    def part(q):
        return pl.BlockSpec((None, s, te), lambda j, q=q: (q, 0, j))

    return pl.pallas_call(
        _after(body, 6, deps), name=name, grid=(e // te,),
        in_specs=[part(0), part(1), part(2), part(3), pl.BlockSpec((s, te), lambda j: (0, j)),
                  pl.BlockSpec((width, te), lambda j: (0, j))] + [ANY] * len(deps),
        out_specs=[pl.BlockSpec((4, s, te), lambda j: (0, 0, j)),
                   pl.BlockSpec((8, te), lambda j: (0, j))],
        out_shape=[jax.ShapeDtypeStruct((4, s, e), BF16), jax.ShapeDtypeStruct((8, e), F32)],
        compiler_params=_params("arbitrary"),
    )(proj, proj, proj, proj, dyb, conv_w, *deps)


def _lru_gates(v_pre, w_ref, cb_ref, wa_ref, ba_ref, wx_ref, bx_ref, lam_ref, width):
    v = _conv(v_pre, w_ref, width) + cb_ref[...]
    vb = v.astype(BF16)
    r = _sigmoid(jnp.dot(vb, wa_ref[...], preferred_element_type=F32) + ba_ref[...])
    i = _sigmoid(jnp.dot(vb, wx_ref[...], preferred_element_type=F32) + bx_ref[...])
    nl = -lam_ref[...]
    sp = jnp.maximum(nl, 0.0) + jnp.log1p(jnp.exp(-jnp.abs(nl)))
    log_a = (-RGLRU_C) * r * sp
    a = jnp.exp(log_a)
    one_minus_a2 = jnp.tanh(-log_a) * (1.0 + a * a)
    mult = jnp.sqrt(one_minus_a2)
    return v, vb, r, i, sp, a, mult


def _lru_specs(s, dh, heads, width):
    head_col = lambda q: pl.BlockSpec((None, s, dh), lambda h, q=q: (q, 0, h))
    vec = pl.BlockSpec((1, dh), lambda h: (0, h))
    mat = pl.BlockSpec((None, dh, dh), lambda h: (h, 0, 0))
    weights = [pl.BlockSpec((width, dh), lambda h: (0, h)), vec, mat, vec, mat, vec, vec]
    return head_col, weights


def _lru_fwd(proj, conv_w, conv_b, w_a, b_a, w_x, b_x, lam, name, deps=()):
    _, s, e = proj.shape
    heads, dh, _ = w_a.shape
    width = conv_w.shape[0]

    def body(v_ref, g_ref, w_ref, cb_ref, wa_ref, ba_ref, wx_ref, bx_ref, lam_ref, yb_ref, keep_ref):
        v, _, r, i, _, a, mult = _lru_gates(v_ref[...].astype(F32), w_ref, cb_ref, wa_ref, ba_ref,
                                           wx_ref, bx_ref, lam_ref, width)
        hs = _scan(a, mult * i * v, _shift_down)
        for k, val in enumerate((hs, v, r, i, a, mult)):
            keep_ref[k] = val
        gv = g_ref[...].astype(F32)
        yb_ref[...] = (hs * (gv * _sigmoid(gv))).astype(yb_ref.dtype)

    head_col, weights = _lru_specs(s, dh, heads, width)
    return pl.pallas_call(
        _after(body, 9, deps), name=name, grid=(heads,),
        in_specs=[head_col(0), head_col(1)] + weights + [ANY] * len(deps),
        out_specs=[pl.BlockSpec((s, dh), lambda h: (0, h)), pl.BlockSpec((6, s, dh), lambda h: (0, 0, h))],
        out_shape=[jax.ShapeDtypeStruct((s, e), BF16), jax.ShapeDtypeStruct((6, s, e), F32)],
        compiler_params=_params("arbitrary"),
    )(proj, proj, conv_w, conv_b, w_a, b_a, w_x, b_x, lam, *deps)


def _lru_bwd(proj, keep, dyb, conv_w, conv_b, w_a, b_a, w_x, b_x, lam, name, deps=()):
    _, s, e = proj.shape
    heads, dh, _ = w_a.shape
    width = conv_w.shape[0]

    def body(v_ref, g_ref, hs_ref, dy_ref, w_ref, cb_ref, wa_ref, ba_ref, wx_ref, bx_ref, lam_ref,
             dp_ref, dwa_ref, dwx_ref, vec_ref):
        v_pre = v_ref[...].astype(F32)
        hs, v, r, i, a, mult = (hs_ref[k] for k in range(6))
        vb = v.astype(BF16)
        nl = -lam_ref[...]
        sp = jnp.maximum(nl, 0.0) + jnp.log1p(jnp.exp(-jnp.abs(nl)))
        gv = g_ref[...].astype(F32)
        dyv = dy_ref[...].astype(F32)
        sg = _sigmoid(gv)
        dp_ref[1] = (dyv * hs * (sg * (1.0 + gv * (1.0 - sg)))).astype(dp_ref.dtype)
        dhs = dyv * (gv * sg)
        d_h = _scan(_shift_up(a, 1), dhs, _shift_up)
        da = d_h * _shift_down(hs, 1)
        iv = i * v
        dlog_a = da * a - (d_h * iv) * (a * a) / mult
        di = d_h * mult * v
        dv = d_h * mult * i
        dzr = dlog_a * (-RGLRU_C) * sp * r * (1.0 - r)
        dzi = di * i * (1.0 - i)
        dsp = jnp.sum(dlog_a * r, axis=0, keepdims=True) * (-RGLRU_C)
        vec_ref[...] = jnp.zeros_like(vec_ref)
        vec_ref[0:1, :] = jnp.sum(dzr, axis=0, keepdims=True)
        vec_ref[1:2, :] = jnp.sum(dzi, axis=0, keepdims=True)
        vec_ref[2:3, :] = -dsp * _sigmoid(-lam_ref[...])
        dzr_b = dzr.astype(BF16)
        dzi_b = dzi.astype(BF16)
        vt = vb.astype(F32).T.astype(BF16)
        dwa_ref[...] = jnp.dot(vt, dzr_b, preferred_element_type=F32).astype(dwa_ref.dtype)
        dwx_ref[...] = jnp.dot(vt, dzi_b, preferred_element_type=F32).astype(dwx_ref.dtype)
        nt = (((1,), (1,)), ((), ()))
        dv = dv + lax.dot_general(dzr_b, wa_ref[...], nt, preferred_element_type=F32)
        dv = dv + lax.dot_general(dzi_b, wx_ref[...], nt, preferred_element_type=F32)
        vec_ref[3:4, :] = jnp.sum(dv, axis=0, keepdims=True)
        dvp = w_ref[width - 1:width, :] * dv
        vec_ref[4 + width - 1:4 + width, :] = jnp.sum(dv * v_pre, axis=0, keepdims=True)
        for k in range(width - 1):
            sh = width - 1 - k
            dvp = dvp + w_ref[k:k + 1, :] * _shift_up(dv, sh)
            vec_ref[4 + k:5 + k, :] = jnp.sum(dv * _shift_down(v_pre, sh), axis=0, keepdims=True)
        dp_ref[0] = dvp.astype(dp_ref.dtype)

    head_col, weights = _lru_specs(s, dh, heads, width)
    col = pl.BlockSpec((s, dh), lambda h: (0, h))
    mat = pl.BlockSpec((None, dh, dh), lambda h: (h, 0, 0))
    return pl.pallas_call(
        _after(body, 11, deps), name=name, grid=(heads,),
        in_specs=[head_col(0), head_col(1), pl.BlockSpec((6, s, dh), lambda h: (0, 0, h)), col] + weights
        + [ANY] * len(deps),
        out_specs=[pl.BlockSpec((2, s, dh), lambda h: (0, 0, h)), mat, mat,
                   pl.BlockSpec((16, dh), lambda h: (0, h))],
        out_shape=[jax.ShapeDtypeStruct((2, s, e), BF16),
                   jax.ShapeDtypeStruct((heads, dh, dh), BF16),
                   jax.ShapeDtypeStruct((heads, dh, dh), BF16),
                   jax.ShapeDtypeStruct((16, e), F32)],
        compiler_params=_params("arbitrary"),
    )(proj, proj, keep, dyb, conv_w, conv_b, w_a, b_a, w_x, b_x, lam, *deps)


def _ada_mod(c_all, w, b, name):
    layers, d, f = w.shape
    nb = c_all.shape[0]

    def body(c_ref, w_ref, b_ref, o_ref):
        cv = c_ref[...]
        sc = cv * _sigmoid(cv)
        o_ref[...] = jnp.dot(sc, w_ref[...], preferred_element_type=F32,
                             precision=lax.Precision.HIGHEST) + b_ref[...]

    return pl.pallas_call(
        body, name=name, grid=(layers,),
        in_specs=[pl.BlockSpec((nb, d), lambda l: (0, 0)),
                  pl.BlockSpec((None, d, f), lambda l: (l, 0, 0)),
                  pl.BlockSpec((None, 1, f), lambda l: (l, 0, 0))],
        out_specs=pl.BlockSpec((None, nb, f), lambda l: (l, 0, 0)),
        out_shape=jax.ShapeDtypeStruct((layers, nb, f), F32),
        compiler_params=_params("arbitrary"),
    )(c_all, w, b)


def _ada_update(c_all_t, dmod, w, m, v, name):
    d, nb = c_all_t.shape
    layers, _, f = dmod.shape
    tr = _tile(d, 512)

    def body(c_ref, dm_ref, w_ref, m_ref, v_ref, g_ref, d_ref, mo_ref, vo_ref):
        cv = c_ref[...]
        sc = cv * _sigmoid(cv)
        g = sc[:, 0:1] * dm_ref[0:1, :]
        for k in range(1, nb):
            g = g + sc[:, k:k + 1] * dm_ref[k:k + 1, :]
        g_ref[...] = g
        d_ref[...], mo_ref[...], vo_ref[...] = _adamw_math(w_ref[...], g, m_ref[...], v_ref[...])

    blk = pl.BlockSpec((None, tr, f), lambda l, i: (l, i, 0))
    return pl.pallas_call(
        body, name=name, grid=(layers, d // tr),
        in_specs=[pl.BlockSpec((tr, nb), lambda l, i: (i, 0)),
                  pl.BlockSpec((None, nb, f), lambda l, i: (l, 0, 0)), blk, blk, blk],
        out_specs=[blk] * 4,
        out_shape=[jax.ShapeDtypeStruct((layers, d, f), F32)] * 4,
        compiler_params=_params("arbitrary", "arbitrary"),
    )(c_all_t, dmod, w, m, v)


def _device_sum(g, name):
    _, rows, _ = g.shape

    def body(g_ref, o_ref):
        acc = g_ref[0]
        for k in range(1, N_DEV):
            acc = acc + g_ref[k]
        o_ref[...] = acc

    return pl.pallas_call(
        body, name=name,
        in_specs=[VMEM_SPEC], out_specs=VMEM_SPEC,
        out_shape=jax.ShapeDtypeStruct((rows, LANES), F32),
        compiler_params=pltpu.CompilerParams(vmem_limit_bytes=VMEM_LIMIT),
    )(g)


def _adamw_math(w, g, m, v):
    m = ADAM_B1 * m + (1.0 - ADAM_B1) * g
    v = ADAM_B2 * v + (1.0 - ADAM_B2) * (g * g)
    m_hat = m / (1.0 - ADAM_B1 ** ADAM_STEP)
    v_hat = v / (1.0 - ADAM_B2 ** ADAM_STEP)
    delta = -ADAM_LR * (m_hat / (jnp.sqrt(v_hat) + ADAM_EPS) + ADAM_WD * w)
    return delta, m, v


def _adamw(w, g, m, v, name):
    rows, cols = w.shape
    tr = _tile(rows, 256)

    def body(w_ref, g_ref, m_ref, v_ref, d_ref, mo_ref, vo_ref):
        d_ref[...], mo_ref[...], vo_ref[...] = _adamw_math(w_ref[...], g_ref[...], m_ref[...], v_ref[...])

    blk = pl.BlockSpec((tr, cols), lambda i: (i, 0))
    return pl.pallas_call(
        body, name=name, grid=(rows // tr,),
        in_specs=[blk] * 4, out_specs=[blk] * 3,
        out_shape=[jax.ShapeDtypeStruct((rows, cols), F32)] * 3,
        compiler_params=_params("arbitrary"),
    )(w, g, m, v)


def _adamw_reduced(idx, w, m, v, part, got, recvs, name):
    rows, cols = w.shape
    tr = _tile(rows, 256)
    nr = len(recvs)

    def body(idx_ref, w_ref, m_ref, v_ref, p_ref, q_ref, *rest):
        g_ref, d_ref, mo_ref, vo_ref = rest[nr:]
        g = p_ref[...].astype(F32) + q_ref[...].astype(F32)
        for u_ref in rest[:nr]:
            for j in range(u_ref.shape[0]):
                g = g + u_ref[j].astype(F32)
        g_ref[...] = g
        d_ref[...], mo_ref[...], vo_ref[...] = _adamw_math(w_ref[...], g, m_ref[...], v_ref[...])

    blk = pl.BlockSpec((tr, cols), lambda i, idx: (i, 0))
    grid_spec = pltpu.PrefetchScalarGridSpec(
        num_scalar_prefetch=1, grid=(rows // tr,),
        in_specs=[blk, blk, blk,
                  pl.BlockSpec((None, None, tr, cols), lambda i, idx: (idx[3], idx[4], i, 0)),
                  pl.BlockSpec((None, None, tr, cols), lambda i, idx: (idx[3], 0, i, 0))]
        + [pl.BlockSpec((u.shape[0], tr, cols), lambda i, idx: (0, i, 0)) for u in recvs],
        out_specs=[blk] * 4)
    return pl.pallas_call(
        body, name=name, grid_spec=grid_spec,
        out_shape=[jax.ShapeDtypeStruct((rows, cols), F32)] * 4,
        compiler_params=_params("arbitrary"),
    )(idx, w, m, v, part, got, *recvs)


def _pack(vectors):
    flat = jnp.concatenate([v.reshape(-1).astype(F32) for v in vectors])
    pad = (-flat.shape[0]) % (8 * LANES)
    return jnp.pad(flat, (0, pad)).reshape(-1, LANES)


def _unpack(flat, shapes):
    out, off = [], 0
    for shp in shapes:
        size = math.prod(shp)
        out.append(flat[..., off:off + size].reshape(flat.shape[:-1] + tuple(shp)))
        off += size
    return out


def _my_slice(full, me, axis):
    size = full.shape[axis] // N_DEV
    return lax.dynamic_slice_in_dim(full, me * size, size, axis)


def kernel(x, c, norm_g, ada_w, ada_b, sc_w_in, sc_conv_w, sc_w_out, lru_w_in, lru_conv_w, lru_conv_b, lru_w_a, lru_b_a, lru_w_x, lru_b_x, lru_lambda, lru_w_out, final_g, loss_target, m_norm_g, m_ada_w, m_ada_b, m_sc_w_in, m_sc_conv_w, m_sc_w_out, m_lru_w_in, m_lru_conv_w, m_lru_conv_b, m_lru_w_a, m_lru_b_a, m_lru_w_x, m_lru_b_x, m_lru_lambda, m_lru_w_out, m_final_g, v_norm_g, v_ada_w, v_ada_b, v_sc_w_in, v_sc_conv_w, v_sc_w_out, v_lru_w_in, v_lru_conv_w, v_lru_conv_b, v_lru_w_a, v_lru_b_a, v_lru_w_x, v_lru_b_x, v_lru_lambda, v_lru_w_out, v_final_g):
    _, s, d = x.shape
    e = sc_w_out.shape[1] * N_DEV
    heads, dh_s, dh = lru_w_a.shape[1:]
    es = e // N_DEV
    f = ada_w.shape[2]
    mx, my, mc = _position()
    me = 4 * mx + 2 * my + mc
    chip = 2 * mx + my
    idx = jnp.stack([chip ^ 1, chip ^ 2, chip ^ 3, chip, mc]).astype(jnp.int32)

    x0 = x[0]
    target = loss_target[0]

    small_shapes = [(d,), (3, es), (4, es), (es,), (heads, dh_s), (heads, dh_s), (es,)]
    small = _small_gather(_pack([c, sc_conv_w, lru_conv_w, lru_conv_b, lru_b_a, lru_b_x, lru_lambda]),
                          "gather_small_weights").reshape(N_DEV, -1)
    c_all, cw3, cw4, cb, ba, bx, lam = _unpack(small, small_shapes)
    cw3 = cw3.transpose(1, 0, 2).reshape(3, e)
    cw4 = cw4.transpose(1, 0, 2).reshape(4, e)
    cb = cb.reshape(1, e)
    lam = lam.reshape(1, e)
    ba = ba.transpose(1, 0, 2).reshape(1, e)
    bx = bx.transpose(1, 0, 2).reshape(1, e)

    shards = [sc_w_in[0].astype(BF16), sc_w_out[0].astype(BF16), lru_w_in[0].astype(BF16),
              lru_w_a[0].reshape(heads * dh_s, dh).astype(BF16),
              lru_w_x[0].reshape(heads * dh_s, dh).astype(BF16), lru_w_out[0].astype(BF16)]
    lands = [lax.dynamic_update_slice(lax.empty((N_DEV,) + sh.shape, BF16), sh[None], (me, 0, 0))
             for sh in shards]
    every = [1, 2, 3, 0]
    units = [([0], [0]), ([0], [1]), ([0], [2]), ([0], [3]), ([1], every), ([2], every), ([3, 4], every),
             ([5], every)]
    sems, first_sh, first_ld, started = _gather_start(shards[:1], lands[:1], units[:3], [small],
                                                      "gather_start_first")
    shards, lands = first_sh + shards[1:], first_ld + lands[1:]

    ada_b_mine = _my_slice(ada_b, me, 1).reshape(2, 1, f)
    mod_mine = _ada_mod(c_all, ada_w, ada_b_mine, "ada_mod")
    mod_all = _small_gather(_pack([mod_mine]), "gather_mod", deps=[started])

    def start_later(after):
        far_sems, far_sh, far_ld, tok = _gather_start(shards[:1], lands[:1], units[3:4], after, "gather_start_far")
        rest_units = [([i - 1 for i in members], ks) for members, ks in units[4:]]
        rest_sems, rest_sh, rest_ld, tok = _gather_start(shards[1:], lands[1:], rest_units, [tok],
                                                         "gather_start_rest")
        sems.extend(far_sems + rest_sems)
        shards[:], lands[:] = far_sh + rest_sh, far_ld + rest_ld
        return tok

    def gathered(u, after_forward, name):
        members, ks = units[u]
        fwd, shs, lnd, token = _gather_forward(
            [shards[i] for i in members], [lands[i] for i in members], ks, sems[u][0], sems[u][1],
            after_forward, "gather_forward_" + name)
        for i, sh, ld in zip(members, shs, lnd):
            shards[i], lands[i] = sh, ld

        def finish(after):
            out = _gather_finish([lands[i] for i in members], ks, fwd, after, "gather_finish_" + name)
            for i, ld in zip(members, out):
                lands[i] = ld
            return out

        return token, finish

    tok, finish_y = gathered(1, [mod_all], "sc_w_in_near_y")
    tok, finish_x = gathered(2, [tok], "sc_w_in_near_x")
    queued = start_later([tok])

    mod_all = mod_all.reshape(N_DEV, -1)
    mod_all = mod_all[:, :2 * N_DEV * f].reshape(N_DEV, 2, N_DEV, f)
    mod_all = mod_all.transpose(1, 2, 0, 3).reshape(2, N_DEV, 3 * d)
    mod = lax.dynamic_index_in_dim(mod_all, me, 1, keepdims=False)
    shift = [mod[l:l + 1, 0:d] for l in range(2)]
    scale = [mod[l:l + 1, d:2 * d] for l in range(2)]
    gate = [mod[l:l + 1, 2 * d:3 * d] for l in range(2)]
    ng = [norm_g[l:l + 1] for l in range(2)]
    fg = final_g.reshape(1, d)

    h0 = _norm_mod(x0, ng[0], scale[0], shift[0], "norm_mod_0", deps=[queued])
    proj0 = lax.empty((4, s, e), BF16)
    tok, _ = gathered(0, [h0], "sc_w_in_own")
    proj0 = _mm_proj_group(h0, lands[0], idx, 3, proj0, "mm_proj_0_own", deps=[tok])
    for u, name, finish in ((1, "near_y", finish_y), (2, "near_x", finish_x), (3, "far", None)):
        after = [proj0]
        if finish is None:
            tok, finish = gathered(u, [proj0], "sc_w_in_" + name)
            after = [tok]
        wg_in0, = finish(after)
        proj0 = _mm_proj_group(h0, wg_in0, idx, u - 1, proj0, "mm_proj_0_" + name)
    tok, finish = gathered(4, [proj0], "sc_w_out")
    yb0 = _sc_fwd(proj0, cw3, "sc_fwd", deps=[tok])
    w_out0 = finish([yb0])[0].reshape(e, d)
    x1, y0 = _mm_out(yb0, w_out0, x0, gate[0], "mm_out_0")
    tok, finish = gathered(5, [x1], "lru_w_in")
    h1 = _norm_mod(x1, ng[1], scale[1], shift[1], "norm_mod_1", deps=[tok])
    wg_in1, = finish([h1])
    proj1 = _mm_proj(h1, wg_in1, 2, "mm_proj_1")
    tok, finish = gathered(6, [proj1], "lru_gates")
    wg_a, wg_x = finish([tok])
    w_a = wg_a.reshape(N_DEV, heads, dh_s, dh).transpose(1, 0, 2, 3).reshape(heads, dh, dh)
    w_x = wg_x.reshape(N_DEV, heads, dh_s, dh).transpose(1, 0, 2, 3).reshape(heads, dh, dh)
    tok, finish = gathered(7, [w_a, w_x], "lru_w_out")
    yb1, hs = _lru_fwd(proj1, cw4, cb, w_a, ba, w_x, bx, lam, "lru_fwd", deps=[tok])
    w_out1 = finish([yb1])[0].reshape(e, d)
    x2, y1 = _mm_out(yb1, w_out1, x1, gate[1], "mm_out_1")
    dx2, loss_part, d_fg, dy1, dgate1 = _final_loss(x2, fg, target, y1, gate[1], "final_loss")

    def pieces(g, rows, cols):
        return g.reshape(4, 2, rows, cols)

    def by_rows(g):
        return g.reshape(heads, N_DEV, dh_s, dh).transpose(1, 0, 2, 3).reshape(N_DEV, heads * dh_s, dh)

    def pair_begin(parts, group):
        send, recv, parts, lnd, token = _pair_start(parts, "pair_start_" + group)
        return dict(parts=parts, lands=lnd, send=send, recv=recv, group=group), token

    def scatter_start(pair, names, after):
        group = pair["group"]
        parts, gots = _pair_wait(pair["parts"], pair["lands"], pair["send"], pair["recv"], after,
                                 "pair_wait_" + group)
        sums = [_pair_sum(idx, p, q, "pair_sum_" + nm) for p, q, nm in zip(parts, gots, names)]
        empties = [lax.empty(sm.shape, sm.dtype) for sm in sums]
        send, recv, sums, lnd, token = _chip_start(sums, empties, "chip_start_" + group)
        return dict(parts=parts, gots=gots, names=names, group=group, sums=sums, lands=lnd,
                    send=send, recv=recv), token

    big = {"sc_w_in": (sc_w_in, m_sc_w_in, v_sc_w_in), "sc_w_out": (sc_w_out, m_sc_w_out, v_sc_w_out),
           "lru_w_in": (lru_w_in, m_lru_w_in, v_lru_w_in), "lru_w_a": (lru_w_a, m_lru_w_a, v_lru_w_a),
           "lru_w_x": (lru_w_x, m_lru_w_x, v_lru_w_x), "lru_w_out": (lru_w_out, m_lru_w_out, v_lru_w_out)}
    big_res = {}

    def scatter_finish(rs, after):
        recvs = _chip_wait(rs["sums"], rs["lands"], rs["send"], rs["recv"], after, "chip_wait_" + rs["group"])
        done = []
        for p, q, u, nm in zip(rs["parts"], rs["gots"], recvs, rs["names"]):
            w, m, v = big[nm]
            shp2 = p.shape[2:]
            res = _adamw_reduced(idx, w.reshape(shp2), m.reshape(shp2), v.reshape(shp2), p, q, [u], "adamw_" + nm)
            big_res[nm] = [r.reshape(w.shape) for r in res]
            done.append(res[1])
        return done

    dw_out1 = _mm_tn(yb1, dy1[None], 1, "mm_dw_out_1")
    pair, tok = pair_begin([pieces(dw_out1, es, d)], "lru_w_out")
    dyb1 = _mm_nt(dy1[None], w_out1[None], BF16, "mm_dyb_1", deps=[tok])
    rs1, tok = scatter_start(pair, ["lru_w_out"], [dyb1])
    dproj1, dw_a, dw_x, vecs1 = _lru_bwd(proj1, hs, dyb1, cw4, cb, w_a, ba, w_x, bx, lam, "lru_bwd", deps=[tok])
    done = scatter_finish(rs1, [dproj1])
    dw_in1 = _mm_tn(h1, dproj1, N_DEV, "mm_dw_in_1", deps=done)
    pair, tok = pair_begin([pieces(dw_in1, d, 2 * es), pieces(by_rows(dw_a), heads * dh_s, dh),
                            pieces(by_rows(dw_x), heads * dh_s, dh)], "lru_in")
    dh1 = _mm_nt(dproj1, wg_in1, F32, "mm_dh_1", deps=[tok])
    rs2, tok = scatter_start(pair, ["lru_w_in", "lru_w_a", "lru_w_x"], [dh1])
    dx1, dscale1, dshift1, dng1, dy0, dgate0 = _norm_mod_bwd(x1, dh1, dx2, ng[1], scale[1], "norm_mod_bwd_1",
                                                             below=(y0, gate[0]), deps=[tok])
    dw_out0 = _mm_tn(yb0, dy0[None], 1, "mm_dw_out_0")
    pair, tok = pair_begin([pieces(dw_out0, es, d)], "sc_w_out")
    dyb0 = _mm_nt(dy0[None], w_out0[None], BF16, "mm_dyb_0", deps=[tok])
    rs3, tok = scatter_start(pair, ["sc_w_out"], [dyb0])
    dproj0, vecs0 = _sc_bwd(proj0, dyb0, cw3, "sc_bwd", deps=[tok])
    idx_one = jnp.stack([jnp.zeros_like(mc)] * 4 + [mc]).astype(jnp.int32)
    sc_w_in_steps = []

    def chip_step(j, pair, after):
        (part,), (got,) = _pair_wait(pair["parts"], pair["lands"], pair["send"], pair["recv"], after,
                                     "pair_wait_sc_w_in_%d" % j)
        sm = _pair_sum(idx_one, part, got, "pair_sum_sc_w_in_%d" % j, nslots=1)
        send, recv, sums, lnd, token = _chip_start([sm], [lax.empty(sm.shape, sm.dtype)],
                                                   "chip_start_sc_w_in_%d" % j, flips=(j,))
        sc_w_in_steps.append((sums, lnd, send, recv, j))
        return token

    pending, done = None, []
    for j in (3, 1, 2, 0):
        part = _mm_tn_group(h0, dproj0, idx, (j - 1) % 4, 2, "mm_dw_in_0_%d" % j, deps=done)[None]
        pair, tok = pair_begin([part], "sc_w_in_%d" % j)
        if j == 3:
            done = [chip_step(j, pair, [tok])]
            continue
        done = [tok]
        if pending is not None:
            done.append(chip_step(pending[0], pending[1], [tok]))
        pending = (j, pair)
    done += scatter_finish(rs2, done)
    dh0 = _mm_nt(dproj0, wg_in0, F32, "mm_dh_0", deps=done)
    pair = pending[1]
    (part,), (got,) = _pair_wait(pair["parts"], pair["lands"], pair["send"], pair["recv"], [dh0],
                                 "pair_wait_sc_w_in_0")
    dx0, dscale0, dshift0, dng0 = _norm_mod_bwd(x0, dh0, dx1, ng[0], scale[0], "norm_mod_bwd_0")
    done = scatter_finish(rs3, [dx0])
    dmod_mine = jnp.concatenate([dshift0, dscale0, dgate0, dshift1, dscale1, dgate1], axis=1)
    end_shapes = [(LANES,), (2, 3 * d), (2, d), (d,), (8, e), (16, e)]
    end_all = _small_gather(
        _pack([loss_part, dmod_mine, jnp.concatenate([dng0, dng1], axis=0), d_fg, vecs0, vecs1]),
        "gather_small_grads", deps=done)
    end_sum = _device_sum(end_all, "sum_small_grads").reshape(-1)
    loss_v, g_ada_b, g_norm_g, g_final_g, sum0, sum1 = _unpack(end_sum, end_shapes)
    loss = loss_v[0]
    dmod_all = _unpack(end_all.reshape(N_DEV, -1), end_shapes)[1].transpose(1, 0, 2)
    dmod_cols = _my_slice(dmod_all, me, 2)
    ada_out = _ada_update(c_all.T, dmod_cols, ada_w, m_ada_w, v_ada_w, "ada_update")

    g_sc_conv_w = _my_slice(sum0[0:3], me, 1)
    g_lru_b_a = _my_slice(sum1[0].reshape(heads, dh), me, 1)
    g_lru_b_x = _my_slice(sum1[1].reshape(heads, dh), me, 1)
    g_lru_lambda = _my_slice(sum1[2:3], me, 1)
    g_lru_conv_b = _my_slice(sum1[3:4], me, 1)
    g_lru_conv_w = _my_slice(sum1[4:8], me, 1)

    small_w = [norm_g, ada_b, final_g, sc_conv_w, lru_conv_w, lru_conv_b, lru_b_a, lru_b_x, lru_lambda]
    small_m = [m_norm_g, m_ada_b, m_final_g, m_sc_conv_w, m_lru_conv_w, m_lru_conv_b, m_lru_b_a, m_lru_b_x,
               m_lru_lambda]
    small_v = [v_norm_g, v_ada_b, v_final_g, v_sc_conv_w, v_lru_conv_w, v_lru_conv_b, v_lru_b_a, v_lru_b_x,
               v_lru_lambda]
    small_g = [g_norm_g, g_ada_b, g_final_g, g_sc_conv_w, g_lru_conv_w, g_lru_conv_b, g_lru_b_a, g_lru_b_x,
               g_lru_lambda]
    small_g = [g.reshape(w.shape) for g, w in zip(small_g, small_w)]
    shapes = [w.shape for w in small_w]
    packed = _adamw(_pack(small_w), _pack(small_g), _pack(small_m), _pack(small_v), "adamw_small")
    small_out = [small_g] + [_unpack(p.reshape(-1), shapes) for p in packed]

    after = [packed[0], ada_out[1]]
    recvs = []
    for sums, lnd, send, recv, j in sc_w_in_steps:
        recvs += _chip_wait(sums, lnd, send, recv, after, "chip_wait_sc_w_in_%d" % j)
    shp2 = part.shape[2:]
    res = _adamw_reduced(idx_one, sc_w_in.reshape(shp2), m_sc_w_in.reshape(shp2), v_sc_w_in.reshape(shp2),
                         part, got, recvs, "adamw_sc_w_in")
    big_res["sc_w_in"] = [r.reshape(sc_w_in.shape) for r in res]
    big_out = [big_res[nm] for nm in ("sc_w_in", "sc_w_out", "lru_w_in", "lru_w_a", "lru_w_x", "lru_w_out")]

    def small(kind, i):
        return small_out[kind][i]

    def bigw(kind, i):
        return big_out[i][kind]

    outs = [loss, dx0[None]]
    for kind in range(4):
        outs += [small(kind, 0), ada_out[kind], small(kind, 1), bigw(kind, 0), small(kind, 3), bigw(kind, 1),
                 bigw(kind, 2), small(kind, 4), small(kind, 5), bigw(kind, 3), small(kind, 6), bigw(kind, 4),
                 small(kind, 7), small(kind, 8), bigw(kind, 5), small(kind, 2)]
    return tuple(outs)
```

```python
import math

import jax
import jax.numpy as jnp
from jax import lax
from jax.experimental import pallas as pl
from jax.experimental.pallas import tpu as pltpu

N_DEV = 8
LANES = 128
EPS = 1e-6
RGLRU_C = 8.0
ADAM_LR = 0.001
ADAM_B1 = 0.9
ADAM_B2 = 0.999
ADAM_EPS = 1e-08
ADAM_WD = 0.01
ADAM_STEP = 10
VMEM_LIMIT = 56 * 1024 * 1024
MESH = pl.DeviceIdType.MESH
F32 = jnp.float32
BF16 = jnp.bfloat16
ANY = pl.BlockSpec(memory_space=pl.ANY)
HBM = pl.BlockSpec(memory_space=pltpu.HBM)
SEM = pl.BlockSpec(memory_space=pltpu.SEMAPHORE)
VMEM_SPEC = pl.BlockSpec(memory_space=pltpu.VMEM)
EFFECT = pltpu.SideEffectType.DATAFLOW_SIDE_EFFECTING
TOKEN = jax.ShapeDtypeStruct((8, LANES), jnp.float32)


def _tile(n, pref):
    t = min(n, pref)
    assert n % t == 0, (n, pref)
    return t


def _params(*sem):
    return pltpu.CompilerParams(dimension_semantics=sem, vmem_limit_bytes=VMEM_LIMIT)


def _position():
    return lax.axis_index("x"), lax.axis_index("y"), lax.axis_index("c")


def _flip(x, y, k):
    return (1 - x if k & 2 else x), (1 - y if k & 1 else y)


def _after(body, n_in, deps):
    if not deps:
        return body

    def wrapped(*refs):
        return body(*refs[:n_in], *refs[n_in + len(deps):])

    return wrapped


def _small_gather(v, name, deps=()):
    rows = v.shape[0]

    def body(v_ref, out_ref, send_sems, recv_sems):
        x, y, c = _position()
        me = 4 * x + 2 * y + c
        out_ref[me] = v_ref[...]
        copies = []
        for k in range(1, N_DEV):
            px, py = _flip(x, y, k >> 1)
            pc = 1 - c if k & 1 else c
            cp = pltpu.make_async_remote_copy(
                src_ref=v_ref, dst_ref=out_ref.at[me],
                send_sem=send_sems.at[k - 1], recv_sem=recv_sems.at[k - 1],
                device_id=(px, py, pc), device_id_type=MESH)
            cp.start()
            copies.append((cp, 4 * px + 2 * py + pc))
        for k, (cp, peer) in enumerate(copies):
            pltpu.make_async_remote_copy(
                src_ref=v_ref, dst_ref=out_ref.at[peer],
                send_sem=send_sems.at[k], recv_sem=recv_sems.at[k],
                device_id=(x, y, c), device_id_type=MESH).wait_recv()
        for cp, _ in copies:
            cp.wait_send()

    return pl.pallas_call(
        _after(body, 1, deps), name=name,
        out_shape=jax.ShapeDtypeStruct((N_DEV, rows, LANES), F32),
        in_specs=[VMEM_SPEC] + [ANY] * len(deps), out_specs=VMEM_SPEC,
        scratch_shapes=[pltpu.SemaphoreType.DMA((N_DEV - 1,)),
                        pltpu.SemaphoreType.DMA((N_DEV - 1,))],
        compiler_params=pltpu.CompilerParams(vmem_limit_bytes=VMEM_LIMIT),
    )(v, *deps)


def _hbm(a):
    return pltpu.with_memory_space_constraint(a, pltpu.HBM)


def _hbm_like(arrays):
    return [pltpu.HBM(a.shape, a.dtype) for a in arrays]


def _remote(src, dst, send, recv, to):
    return pltpu.make_async_remote_copy(src_ref=src, dst_ref=dst, send_sem=send, recv_sem=recv,
                                        device_id=to, device_id_type=MESH)


def _gather_start(shards, lands, units, after, name):
    n, nu = len(shards), len(units)

    def body(*refs):
        ins, lnd = refs[:n], refs[n:2 * n]
        sems = refs[2 * n + len(after):2 * n + len(after) + 2 * nu]
        token = refs[-1]
        x, y, c = _position()
        me = 4 * x + 2 * y + c
        targets = [(x, y, 1 - c)] + [(px, py, c) for px, py in (_flip(x, y, k) for k in (1, 2, 3))]
        for u, (members, ks) in enumerate(units):
            for slot, i in enumerate(members):
                for ki, k in enumerate(ks):
                    at = len(ks) * slot + ki
                    _remote(ins[i], lnd[i].at[me], sems[2 * u].at[at], sems[2 * u + 1].at[at], targets[k]).start()
        token[...] = jnp.zeros_like(token)

    sem_shapes = []
    for members, ks in units:
        count = len(members) * len(ks)
        sem_shapes += [pltpu.SemaphoreType.DMA((count,)), pltpu.SemaphoreType.DMA((count,))]
    out = pl.pallas_call(
        body, name=name,
        out_shape=sem_shapes + _hbm_like(shards) + _hbm_like(lands) + [TOKEN],
        in_specs=[HBM] * (2 * n) + [ANY] * len(after),
        out_specs=[SEM] * (2 * nu) + [HBM] * (2 * n) + [VMEM_SPEC],
        input_output_aliases={i: 2 * nu + i for i in range(2 * n)},
        compiler_params=pltpu.CompilerParams(has_side_effects=EFFECT),
    )(*[_hbm(s) for s in shards], *[_hbm(l) for l in lands], *after)
    sems = [(out[2 * u], out[2 * u + 1]) for u in range(nu)]
    return sems, list(out[2 * nu:2 * nu + n]), list(out[2 * nu + n:2 * nu + 2 * n]), out[-1]


def _gather_forward(shards, lands, ks, send, recv, after, name):
    m = len(shards)
    hops = [k for k in ks if k]
    nsem = 2 if hops else 0

    def body(*refs):
        ins, lnd = refs[:m], refs[m:2 * m]
        send_ref, recv_ref = refs[2 * m], refs[2 * m + 1]
        outs = refs[2 * m + 2 + len(after):]
        token = refs[-1]
        x, y, c = _position()
        me = (x, y, c)
        for slot in range(m):
            for ki, k in enumerate(ks):
                at = len(ks) * slot + ki
                if k:
                    px, py = _flip(x, y, k)
                    block = lnd[slot].at[4 * px + 2 * py + c]
                else:
                    block = lnd[slot].at[4 * x + 2 * y + (1 - c)]
                arrival = _remote(ins[slot], block, send_ref.at[at], recv_ref.at[at], me)
                arrival.wait_recv()
                if k:
                    fat = len(hops) * slot + hops.index(k)
                    _remote(block, block, outs[0].at[fat], outs[1].at[fat], (x, y, 1 - c)).start()
                arrival.wait_send()
        token[...] = jnp.zeros_like(token)

    count = len(hops) * m
    sem_shapes = [pltpu.SemaphoreType.DMA((count,)), pltpu.SemaphoreType.DMA((count,))] if hops else []
    out = pl.pallas_call(
        body, name=name,
        out_shape=sem_shapes + _hbm_like(shards) + _hbm_like(lands) + [TOKEN],
        in_specs=[HBM] * (2 * m) + [SEM, SEM] + [ANY] * len(after),
        out_specs=[SEM] * nsem + [HBM] * (2 * m) + [VMEM_SPEC],
        input_output_aliases={i: nsem + i for i in range(2 * m)},
        compiler_params=pltpu.CompilerParams(has_side_effects=EFFECT),
    )(*shards, *lands, send, recv, *after)
    fwd = (out[0], out[1]) if hops else None
    return fwd, list(out[nsem:nsem + m]), list(out[nsem + m:nsem + 2 * m]), out[-1]


def _gather_finish(lands, ks, fwd, after, name):
    m = len(lands)
    hops = [k for k in ks if k]

    def body(*refs):
        lnd = refs[:m]
        fsend_ref, frecv_ref = refs[m], refs[m + 1]
        x, y, c = _position()
        for slot in range(m):
            for fi, k in enumerate(hops):
                px, py = _flip(x, y, k)
                sent = lnd[slot].at[4 * px + 2 * py + c]
                came = lnd[slot].at[4 * px + 2 * py + (1 - c)]
                fat = len(hops) * slot + fi
                cp = _remote(sent, came, fsend_ref.at[fat], frecv_ref.at[fat], (x, y, c))
                cp.wait_recv()
                cp.wait_send()

    out = pl.pallas_call(
        body, name=name,
        out_shape=_hbm_like(lands),
        in_specs=[HBM] * m + [SEM, SEM] + [ANY] * len(after), out_specs=[HBM] * m,
        input_output_aliases={i: i for i in range(m)},
        compiler_params=pltpu.CompilerParams(has_side_effects=EFFECT),
    )(*lands, fwd[0], fwd[1], *after)
    return list(out)


def _pair_start(parts, name):
    n = len(parts)
    lands = [lax.empty((p.shape[0], 1) + p.shape[2:], p.dtype) for p in parts]

    def body(*refs):
        ins, lnd = refs[:n], refs[n:2 * n]
        send_ref, recv_ref = refs[2 * n], refs[2 * n + 1]
        token = refs[-1]
        x, y, c = _position()
        for i in range(n):
            _remote(ins[i].at[:, pl.ds(1 - c, 1)], lnd[i], send_ref.at[i], recv_ref.at[i], (x, y, 1 - c)).start()
        token[...] = jnp.zeros_like(token)

    out = pl.pallas_call(
        body, name=name,
        out_shape=[pltpu.SemaphoreType.DMA((n,)), pltpu.SemaphoreType.DMA((n,))]
        + _hbm_like(parts) + _hbm_like(lands) + [TOKEN],
        in_specs=[HBM] * (2 * n), out_specs=[SEM, SEM] + [HBM] * (2 * n) + [VMEM_SPEC],
        input_output_aliases={i: 2 + i for i in range(2 * n)},
        compiler_params=pltpu.CompilerParams(has_side_effects=EFFECT),
    )(*[_hbm(p) for p in parts], *[_hbm(l) for l in lands])
    return out[0], out[1], list(out[2:2 + n]), list(out[2 + n:2 + 2 * n]), out[-1]


def _pair_wait(parts, lands, send, recv, after, name):
    n = len(parts)

    def body(*refs):
        ins, lnd = refs[:n], refs[n:2 * n]
        send_ref, recv_ref = refs[2 * n], refs[2 * n + 1]
        x, y, c = _position()
        for i in range(n):
            cp = _remote(ins[i].at[:, pl.ds(1 - c, 1)], lnd[i], send_ref.at[i], recv_ref.at[i], (x, y, c))
            cp.wait_recv()
            cp.wait_send()

    out = pl.pallas_call(
        body, name=name,
        out_shape=_hbm_like(parts) + _hbm_like(lands),
        in_specs=[HBM] * (2 * n) + [SEM, SEM] + [ANY] * len(after), out_specs=[HBM] * (2 * n),
        input_output_aliases={i: i for i in range(2 * n)},
        compiler_params=pltpu.CompilerParams(has_side_effects=EFFECT),
    )(*parts, *lands, send, recv, *after)
    return list(out[:n]), list(out[n:])


def _chip_start(sums, lands, name, flips=(1, 2, 3)):
    n, ns = len(sums), len(flips)

    def body(*refs):
        ins, lnd = refs[:n], refs[n:2 * n]
        send_ref, recv_ref = refs[2 * n], refs[2 * n + 1]
        token = refs[-1]
        x, y, c = _position()
        for i in range(n):
            for j, flip in enumerate(flips):
                px, py = _flip(x, y, flip)
                _remote(ins[i].at[j], lnd[i].at[j], send_ref.at[ns * i + j], recv_ref.at[ns * i + j],
                        (px, py, c)).start()
        token[...] = jnp.zeros_like(token)

    out = pl.pallas_call(
        body, name=name,
        out_shape=[pltpu.SemaphoreType.DMA((ns * n,)), pltpu.SemaphoreType.DMA((ns * n,))]
        + _hbm_like(sums) + _hbm_like(lands) + [TOKEN],
        in_specs=[HBM] * (2 * n), out_specs=[SEM, SEM] + [HBM] * (2 * n) + [VMEM_SPEC],
        input_output_aliases={i: 2 + i for i in range(2 * n)},
        compiler_params=pltpu.CompilerParams(has_side_effects=EFFECT),
    )(*[_hbm(s) for s in sums], *[_hbm(l) for l in lands])
    return out[0], out[1], out[2:2 + n], out[2 + n:2 + 2 * n], out[-1]


def _chip_wait(sums, lands, send, recv, after, name):
    n, ns = len(sums), sums[0].shape[0]

    def body(*refs):
        ins, lnd = refs[:n], refs[n:2 * n]
        send_ref, recv_ref = refs[2 * n], refs[2 * n + 1]
        x, y, c = _position()
        for i in range(n):
            for j in range(ns):
                cp = _remote(ins[i].at[j], lnd[i].at[j], send_ref.at[ns * i + j], recv_ref.at[ns * i + j], (x, y, c))
                cp.wait_recv()
                cp.wait_send()

    out = pl.pallas_call(
        body, name=name,
        out_shape=_hbm_like(sums) + _hbm_like(lands),
        in_specs=[HBM] * (2 * n) + [SEM, SEM] + [ANY] * len(after), out_specs=[HBM] * (2 * n),
        input_output_aliases={i: i for i in range(2 * n)},
        compiler_params=pltpu.CompilerParams(has_side_effects=EFFECT),
    )(*sums, *lands, send, recv, *after)
    return list(out[n:])


def _pair_sum(idx, part, got, name, nslots=3):
    _, _, rows, cols = part.shape
    tr = _tile(rows, 1024)

    def body(idx_ref, p_ref, q_ref, o_ref):
        o_ref[...] = (p_ref[...].astype(F32) + q_ref[...].astype(F32)).astype(o_ref.dtype)

    grid_spec = pltpu.PrefetchScalarGridSpec(
        num_scalar_prefetch=1, grid=(nslots, rows // tr),
        in_specs=[pl.BlockSpec((None, None, tr, cols), lambda j, r, idx: (idx[j], idx[4], r, 0)),
                  pl.BlockSpec((None, None, tr, cols), lambda j, r, idx: (idx[j], 0, r, 0))],
        out_specs=pl.BlockSpec((None, tr, cols), lambda j, r, idx: (j, r, 0)))
    return pl.pallas_call(
        body, name=name, grid_spec=grid_spec,
        out_shape=jax.ShapeDtypeStruct((nslots, rows, cols), part.dtype),
        compiler_params=_params("arbitrary", "arbitrary"),
    )(idx, part, got)


def _mm_proj(h, wg, groups, name):
    s, k = h.shape
    nchunk, _, n = wg.shape
    e = nchunk * n // groups
    tn = _tile(min(n, e), 512)

    def body(h_ref, w_ref, o_ref):
        o_ref[...] = jnp.dot(h_ref[...], w_ref[...], preferred_element_type=F32).astype(o_ref.dtype)

    return pl.pallas_call(
        body, name=name, grid=(nchunk * n // tn,),
        in_specs=[pl.BlockSpec((s, k), lambda j: (0, 0)),
                  pl.BlockSpec((None, k, tn), lambda j: ((j * tn) // n, 0, ((j * tn) % n) // tn))],
        out_specs=pl.BlockSpec((None, s, tn), lambda j: ((j * tn) // e, 0, ((j * tn) % e) // tn)),
        out_shape=jax.ShapeDtypeStruct((groups, s, e), BF16),
        compiler_params=_params("arbitrary"),
    )(h, wg)


def _mm_proj_group(h, wg, idx, pos, prev, name, deps=()):
    s, k = h.shape
    _, _, n = wg.shape
    _, _, e = prev.shape
    tn = _tile(n, 512)
    nd = len(deps)

    def body(idx_ref, h_ref, w_ref, prev_ref, *rest):
        o_ref = rest[nd]
        o_ref[...] = jnp.dot(h_ref[...], w_ref[...], preferred_element_type=F32).astype(o_ref.dtype)

    def col(j, idx):
        return idx[pos] * (2 * n) + j * tn

    grid_spec = pltpu.PrefetchScalarGridSpec(
        num_scalar_prefetch=1, grid=(2 * n // tn,),
        in_specs=[pl.BlockSpec((s, k), lambda j, idx: (0, 0)),
                  pl.BlockSpec((None, k, tn), lambda j, idx: (col(j, idx) // n, 0, (col(j, idx) % n) // tn)),
                  ANY] + [ANY] * nd,
        out_specs=pl.BlockSpec((None, s, tn), lambda j, idx: (col(j, idx) // e, 0, (col(j, idx) % e) // tn)))
    return pl.pallas_call(
        body, name=name, grid_spec=grid_spec,
        out_shape=jax.ShapeDtypeStruct(prev.shape, prev.dtype),
        input_output_aliases={3: 0},
        compiler_params=_params("arbitrary"),
    )(idx, h, wg, prev, *deps)


def _mm_out(yb, w, x, gate, name):
    s, k = yb.shape
    d = w.shape[1]
    tn = _tile(d, 512)
    tk = _tile(k, 2048)
    nk = k // tk

    def body(a_ref, w_ref, x_ref, g_ref, xo_ref, y_ref, acc_ref):
        kk = pl.program_id(1)

        @pl.when(kk == 0)
        def _():
            acc_ref[...] = jnp.zeros_like(acc_ref)

        acc_ref[...] += jnp.dot(a_ref[...], w_ref[...], preferred_element_type=F32)

        @pl.when(kk == nk - 1)
        def _():
            y = acc_ref[...]
            y_ref[...] = y.astype(y_ref.dtype)
            xo_ref[...] = x_ref[...] + g_ref[...] * y

    return pl.pallas_call(
        body, name=name, grid=(d // tn, nk),
        in_specs=[pl.BlockSpec((s, tk), lambda j, kk: (0, kk)),
                  pl.BlockSpec((tk, tn), lambda j, kk: (kk, j)),
                  pl.BlockSpec((s, tn), lambda j, kk: (0, j)),
                  pl.BlockSpec((1, tn), lambda j, kk: (0, j))],
        out_specs=[pl.BlockSpec((s, tn), lambda j, kk: (0, j)),
                   pl.BlockSpec((s, tn), lambda j, kk: (0, j))],
        out_shape=[jax.ShapeDtypeStruct((s, d), F32), jax.ShapeDtypeStruct((s, d), BF16)],
        scratch_shapes=[pltpu.VMEM((s, tn), F32)],
        compiler_params=_params("arbitrary", "arbitrary"),
    )(yb, w, x, gate)


def _mm_nt(a3, w3, out_dtype, name, deps=()):
    g, s, ea = a3.shape
    cw, n, nw = w3.shape
    total = g * ea
    assert total == cw * nw
    tk = _tile(min(ea, nw), 2048 if out_dtype == BF16 else 1024)
    tn = _tile(n, 1024)
    nk = total // tk

    def body(a_ref, w_ref, o_ref, acc_ref):
        kk = pl.program_id(1)

        @pl.when(kk == 0)
        def _():
            acc_ref[...] = jnp.zeros_like(acc_ref)

        acc_ref[...] += lax.dot_general(a_ref[...], w_ref[...], (((1,), (1,)), ((), ())),
                                        preferred_element_type=F32)

        @pl.when(kk == nk - 1)
        def _():
            o_ref[...] = acc_ref[...].astype(o_ref.dtype)

    return pl.pallas_call(
        _after(body, 2, deps), name=name, grid=(n // tn, nk),
        in_specs=[pl.BlockSpec((None, s, tk), lambda j, kk: ((kk * tk) // ea, 0, ((kk * tk) % ea) // tk)),
                  pl.BlockSpec((None, tn, tk), lambda j, kk: ((kk * tk) // nw, j, ((kk * tk) % nw) // tk))]
        + [ANY] * len(deps),
        out_specs=pl.BlockSpec((s, tn), lambda j, kk: (0, j)),
        out_shape=jax.ShapeDtypeStruct((s, n), out_dtype),
        scratch_shapes=[pltpu.VMEM((s, tn), F32)],
        compiler_params=_params("arbitrary", "arbitrary"),
    )(a3, w3, *deps)


def _mm_tn(a, b3, nchunk, name, deps=()):
    s, ka = a.shape
    g, _, eb = b3.shape
    n = g * eb // nchunk
    tm = _tile(ka, 1024)
    tn = _tile(min(n, eb), 1024)

    def body(a_ref, b_ref, o_ref, at_ref):
        @pl.when(pl.program_id(1) == 0)
        def _():
            at_ref[...] = a_ref[...].astype(F32).T.astype(at_ref.dtype)

        o_ref[...] = jnp.dot(at_ref[...], b_ref[...], preferred_element_type=F32).astype(o_ref.dtype)

    return pl.pallas_call(
        _after(body, 2, deps), name=name, grid=(ka // tm, g * eb // tn),
        in_specs=[pl.BlockSpec((s, tm), lambda i, j: (0, i)),
                  pl.BlockSpec((None, s, tn), lambda i, j: ((j * tn) // eb, 0, ((j * tn) % eb) // tn))]
        + [ANY] * len(deps),
        out_specs=pl.BlockSpec((None, tm, tn), lambda i, j: ((j * tn) // n, i, ((j * tn) % n) // tn)),
        out_shape=jax.ShapeDtypeStruct((nchunk, ka, n), BF16),
        scratch_shapes=[pltpu.VMEM((tm, s), BF16)],
        compiler_params=_params("arbitrary", "arbitrary"),
    )(a, b3, *deps)


def _mm_tn_group(a, b3, idx, pos, nchunk, name, deps=()):
    s, ka = a.shape
    _, _, eb = b3.shape
    n = eb // nchunk
    tm = _tile(ka, 1024)
    tn = _tile(n, 1024)
    nd = len(deps)

    def body(idx_ref, a_ref, b_ref, *rest):
        o_ref, at_ref = rest[nd:]

        @pl.when(pl.program_id(1) == 0)
        def _():
            at_ref[...] = a_ref[...].astype(F32).T.astype(at_ref.dtype)

        o_ref[...] = jnp.dot(at_ref[...], b_ref[...], preferred_element_type=F32).astype(o_ref.dtype)

    grid_spec = pltpu.PrefetchScalarGridSpec(
        num_scalar_prefetch=1, grid=(ka // tm, eb // tn),
        in_specs=[pl.BlockSpec((s, tm), lambda i, j, idx: (0, i)),
                  pl.BlockSpec((None, s, tn), lambda i, j, idx: (idx[pos], 0, j))] + [ANY] * nd,
        out_specs=pl.BlockSpec((None, tm, tn), lambda i, j, idx: ((j * tn) // n, i, ((j * tn) % n) // tn)),
        scratch_shapes=[pltpu.VMEM((tm, s), BF16)])
    return pl.pallas_call(
        body, name=name, grid_spec=grid_spec,
        out_shape=jax.ShapeDtypeStruct((nchunk, ka, n), BF16),
        compiler_params=_params("arbitrary", "arbitrary"),
    )(idx, a, b3, *deps)


def _sigmoid(z):
    return jax.nn.sigmoid(z)


def _shift_down(v, k, fill=0.0, period=None):
    if k == 0:
        return v
    row = lax.broadcasted_iota(jnp.int32, v.shape, 0)
    if period is not None:
        row = row & (period - 1)
    return jnp.where(row >= k, pltpu.roll(v, k, 0), fill)


def _shift_up(v, k, fill=0.0, period=None):
    if k == 0:
        return v
    s = v.shape[0]
    row = lax.broadcasted_iota(jnp.int32, v.shape, 0)
    if period is not None:
        row, s = row & (period - 1), period
    return jnp.where(row < s - k, pltpu.roll(v, v.shape[0] - k, 0), fill)


SCAN_BLOCK = 64


def _scan(a, b, shift):
    s = a.shape[0]
    blk = min(SCAN_BLOCK, s)
    k = 1
    while k < blk:
        b = a * shift(b, k, 0.0, blk) + b
        a = a * shift(a, k, 1.0, blk)
        k *= 2
    nblk = s // blk
    forward = shift is _shift_down
    order = range(nblk) if forward else range(nblk - 1, -1, -1)
    edge = blk - 1 if forward else 0
    out = [None] * nblk
    carry = None
    for i in order:
        h = b[i * blk:(i + 1) * blk]
        if carry is not None:
            h = a[i * blk:(i + 1) * blk] * carry + h
        carry = h[edge:edge + 1]
        out[i] = h
    return jnp.concatenate(out, axis=0) if nblk > 1 else out[0]


def _norm_mod(x, g, scale, shift, name, deps=()):
    s, d = x.shape
    ts = _tile(s, 256)

    def body(x_ref, g_ref, sc_ref, sh_ref, h_ref):
        xv = x_ref[...]
        rstd = lax.rsqrt(jnp.mean(xv * xv, axis=-1, keepdims=True) + EPS)
        nrm = xv * rstd * g_ref[...]
        h_ref[...] = (nrm * (1.0 + sc_ref[...]) + sh_ref[...]).astype(h_ref.dtype)

    vec = pl.BlockSpec((1, d), lambda i: (0, 0))
    return pl.pallas_call(
        _after(body, 4, deps), name=name, grid=(s // ts,),
        in_specs=[pl.BlockSpec((ts, d), lambda i: (i, 0)), vec, vec, vec] + [ANY] * len(deps),
        out_specs=pl.BlockSpec((ts, d), lambda i: (i, 0)),
        out_shape=jax.ShapeDtypeStruct((s, d), BF16),
        compiler_params=_params("arbitrary"),
    )(x, g, scale, shift, *deps)


def _gate_terms(dx, y_ref, gate_ref, dy_ref, dgate_ref):
    dy_ref[...] = (dx * gate_ref[...]).astype(dy_ref.dtype)
    dgate_ref[...] += jnp.sum(dx * y_ref[...].astype(F32), axis=0, keepdims=True)


def _norm_mod_bwd(x, dh, dx_res, g, scale, name, below=None, deps=()):
    s, d = x.shape
    ts = _tile(s, 256)
    nb = 2 if below is not None else 0

    def body(x_ref, dh_ref, dr_ref, g_ref, sc_ref, *rest):
        dx_ref, dsc_ref, dsh_ref, dg_ref = rest[nb:nb + 4]

        @pl.when(pl.program_id(0) == 0)
        def _():
            for ref in rest[nb + 1:nb + 4] + rest[nb + 5:]:
                ref[...] = jnp.zeros_like(ref)

        xv = x_ref[...]
        dh_v = dh_ref[...].astype(F32)
        gv = g_ref[...]
        rstd = lax.rsqrt(jnp.mean(xv * xv, axis=-1, keepdims=True) + EPS)
        xhat = xv * rstd
        dsc_ref[...] += jnp.sum(dh_v * xhat * gv, axis=0, keepdims=True)
        dsh_ref[...] += jnp.sum(dh_v, axis=0, keepdims=True)
        dn = dh_v * (1.0 + sc_ref[...])
        dg_ref[...] += jnp.sum(dn * xhat, axis=0, keepdims=True)
        dxhat = dn * gv
        proj = jnp.mean(dxhat * xhat, axis=-1, keepdims=True)
        dx = dr_ref[...] + rstd * (dxhat - xhat * proj)
        dx_ref[...] = dx
        if nb:
            _gate_terms(dx, rest[0], rest[1], rest[nb + 4], rest[nb + 5])

    row = pl.BlockSpec((ts, d), lambda i: (i, 0))
    vec = pl.BlockSpec((1, d), lambda i: (0, 0))
    extra = list(below) if nb else []
    return pl.pallas_call(
        _after(body, 5 + nb, deps), name=name, grid=(s // ts,),
        in_specs=[row, row, row, vec, vec] + [row, vec][:nb] + [ANY] * len(deps),
        out_specs=[row, vec, vec, vec] + [row, vec][:nb],
        out_shape=[jax.ShapeDtypeStruct((s, d), F32)] + [jax.ShapeDtypeStruct((1, d), F32)] * 3
        + [jax.ShapeDtypeStruct((s, d), BF16), jax.ShapeDtypeStruct((1, d), F32)][:nb],
        compiler_params=_params("arbitrary"),
    )(x, dh, dx_res, g, scale, *extra, *deps)


def _final_loss(x, g, target, y, gate, name):
    s, d = x.shape
    ts = _tile(s, 256)

    def body(x_ref, g_ref, t_ref, y_ref, gate_ref, dx_ref, loss_ref, dg_ref, dy_ref, dgate_ref):
        @pl.when(pl.program_id(0) == 0)
        def _():
            loss_ref[...] = jnp.zeros_like(loss_ref)
            dg_ref[...] = jnp.zeros_like(dg_ref)
            dgate_ref[...] = jnp.zeros_like(dgate_ref)

        xv = x_ref[...]
        gv = g_ref[...]
        rstd = lax.rsqrt(jnp.mean(xv * xv, axis=-1, keepdims=True) + EPS)
        xhat = xv * rstd
        err = xhat * gv - t_ref[...]
        loss_ref[...] += 0.5 * jnp.sum(jnp.mean(err * err, axis=-1, keepdims=True))
        dy = err * (1.0 / d)
        dg_ref[...] += jnp.sum(dy * xhat, axis=0, keepdims=True)
        dxhat = dy * gv
        proj = jnp.mean(dxhat * xhat, axis=-1, keepdims=True)
        dx = rstd * (dxhat - xhat * proj)
        dx_ref[...] = dx
        _gate_terms(dx, y_ref, gate_ref, dy_ref, dgate_ref)

    row = pl.BlockSpec((ts, d), lambda i: (i, 0))
    vec = pl.BlockSpec((1, d), lambda i: (0, 0))
    return pl.pallas_call(
        body, name=name, grid=(s // ts,),
        in_specs=[row, vec, row, row, vec],
        out_specs=[row, pl.BlockSpec((1, LANES), lambda i: (0, 0)), vec, row, vec],
        out_shape=[jax.ShapeDtypeStruct((s, d), F32), jax.ShapeDtypeStruct((1, LANES), F32),
                   jax.ShapeDtypeStruct((1, d), F32), jax.ShapeDtypeStruct((s, d), BF16),
                   jax.ShapeDtypeStruct((1, d), F32)],
        compiler_params=_params("arbitrary"),
    )(x, g, target, y, gate)


def _conv(v, w_ref, width):
    out = w_ref[width - 1:width, :] * v
    for k in range(width - 1):
        out = out + w_ref[k:k + 1, :] * _shift_down(v, width - 1 - k)
    return out


def _sc_fwd(proj, conv_w, name, deps=()):
    _, s, e = proj.shape
    te = _tile(e, 256)
    width = conv_w.shape[0]

    def body(b_ref, c_ref, v_ref, g_ref, w_ref, o_ref):
        cv = c_ref[...].astype(F32) * v_ref[...].astype(F32)
        u = _conv(cv, w_ref, width)
        gv = g_ref[...].astype(F32)
        o_ref[...] = (b_ref[...].astype(F32) * u * (gv * _sigmoid(gv))).astype(o_ref.dtype)

    def part(q):
        return pl.BlockSpec((None, s, te), lambda j, q=q: (q, 0, j))

    return pl.pallas_call(
        _after(body, 5, deps), name=name, grid=(e // te,),
        in_specs=[part(0), part(1), part(2), part(3), pl.BlockSpec((width, te), lambda j: (0, j))]
        + [ANY] * len(deps),
        out_specs=pl.BlockSpec((s, te), lambda j: (0, j)),
        out_shape=jax.ShapeDtypeStruct((s, e), BF16),
        compiler_params=_params("arbitrary"),
    )(proj, proj, proj, proj, conv_w, *deps)


def _sc_bwd(proj, dyb, conv_w, name, deps=()):
    _, s, e = proj.shape
    te = _tile(e, 256)
    width = conv_w.shape[0]

    def body(b_ref, c_ref, v_ref, g_ref, dy_ref, w_ref, dp_ref, vec_ref):
        bv = b_ref[...].astype(F32)
        cvl = c_ref[...].astype(F32)
        vv = v_ref[...].astype(F32)
        gv = g_ref[...].astype(F32)
        dyv = dy_ref[...].astype(F32)
        cv = cvl * vv
        u = _conv(cv, w_ref, width)
        sg = _sigmoid(gv)
        silu = gv * sg
        dp_ref[0] = (dyv * u * silu).astype(dp_ref.dtype)
        du = dyv * bv * silu
        dp_ref[3] = (dyv * bv * u * (sg * (1.0 + gv * (1.0 - sg)))).astype(dp_ref.dtype)
        dcv = w_ref[width - 1:width, :] * du
        vec_ref[...] = jnp.zeros_like(vec_ref)
        vec_ref[width - 1:width, :] = jnp.sum(du * cv, axis=0, keepdims=True)
        for k in range(width - 1):
            sh = width - 1 - k
            dcv = dcv + w_ref[k:k + 1, :] * _shift_up(du, sh)
            vec_ref[k:k + 1, :] = jnp.sum(du * _shift_down(cv, sh), axis=0, keepdims=True)
        dp_ref[1] = (dcv * vv).astype(dp_ref.dtype)
        dp_ref[2] = (dcv * cvl).astype(dp_ref.dtype)

    def part(q):
        return pl.BlockSpec((None, s, te), lambda j, q=q: (q, 0, j))

    return pl.pallas_call(
        _after(body, 6, deps), name=name, grid=(e // te,),
        in_specs=[part(0), part(1), part(2), part(3), pl.BlockSpec((s, te), lambda j: (0, j)),
                  pl.BlockSpec((width, te), lambda j: (0, j))] + [ANY] * len(deps),
        out_specs=[pl.BlockSpec((4, s, te), lambda j: (0, 0, j)),
                   pl.BlockSpec((8, te), lambda j: (0, j))],
        out_shape=[jax.ShapeDtypeStruct((4, s, e), BF16), jax.ShapeDtypeStruct((8, e), F32)],
        compiler_params=_params("arbitrary"),
    )(proj, proj, proj, proj, dyb, conv_w, *deps)


def _lru_gates(v_pre, w_ref, cb_ref, wa_ref, ba_ref, wx_ref, bx_ref, lam_ref, width):
    v = _conv(v_pre, w_ref, width) + cb_ref[...]
    vb = v.astype(BF16)
    r = _sigmoid(jnp.dot(vb, wa_ref[...], preferred_element_type=F32) + ba_ref[...])
    i = _sigmoid(jnp.dot(vb, wx_ref[...], preferred_element_type=F32) + bx_ref[...])
    nl = -lam_ref[...]
    sp = jnp.maximum(nl, 0.0) + jnp.log1p(jnp.exp(-jnp.abs(nl)))
    log_a = (-RGLRU_C) * r * sp
    a = jnp.exp(log_a)
    one_minus_a2 = jnp.tanh(-log_a) * (1.0 + a * a)
    mult = jnp.sqrt(one_minus_a2)
    return v, vb, r, i, sp, a, mult


def _lru_specs(s, dh, heads, width):
    head_col = lambda q: pl.BlockSpec((None, s, dh), lambda h, q=q: (q, 0, h))
    vec = pl.BlockSpec((1, dh), lambda h: (0, h))
    mat = pl.BlockSpec((None, dh, dh), lambda h: (h, 0, 0))
    weights = [pl.BlockSpec((width, dh), lambda h: (0, h)), vec, mat, vec, mat, vec, vec]
    return head_col, weights


def _lru_fwd(proj, conv_w, conv_b, w_a, b_a, w_x, b_x, lam, name, deps=()):
    _, s, e = proj.shape
    heads, dh, _ = w_a.shape
    width = conv_w.shape[0]

    def body(v_ref, g_ref, w_ref, cb_ref, wa_ref, ba_ref, wx_ref, bx_ref, lam_ref, yb_ref, keep_ref):
        v, _, r, i, _, a, mult = _lru_gates(v_ref[...].astype(F32), w_ref, cb_ref, wa_ref, ba_ref,
                                           wx_ref, bx_ref, lam_ref, width)
        hs = _scan(a, mult * i * v, _shift_down)
        for k, val in enumerate((hs, v, r, i, a, mult)):
            keep_ref[k] = val
        gv = g_ref[...].astype(F32)
        yb_ref[...] = (hs * (gv * _sigmoid(gv))).astype(yb_ref.dtype)

    head_col, weights = _lru_specs(s, dh, heads, width)
    return pl.pallas_call(
        _after(body, 9, deps), name=name, grid=(heads,),
        in_specs=[head_col(0), head_col(1)] + weights + [ANY] * len(deps),
        out_specs=[pl.BlockSpec((s, dh), lambda h: (0, h)), pl.BlockSpec((6, s, dh), lambda h: (0, 0, h))],
        out_shape=[jax.ShapeDtypeStruct((s, e), BF16), jax.ShapeDtypeStruct((6, s, e), F32)],
        compiler_params=_params("arbitrary"),
    )(proj, proj, conv_w, conv_b, w_a, b_a, w_x, b_x, lam, *deps)


def _lru_bwd(proj, keep, dyb, conv_w, conv_b, w_a, b_a, w_x, b_x, lam, name, deps=()):
    _, s, e = proj.shape
    heads, dh, _ = w_a.shape
    width = conv_w.shape[0]

    def body(v_ref, g_ref, hs_ref, dy_ref, w_ref, cb_ref, wa_ref, ba_ref, wx_ref, bx_ref, lam_ref,
             dp_ref, dwa_ref, dwx_ref, vec_ref):
        v_pre = v_ref[...].astype(F32)
        hs, v, r, i, a, mult = (hs_ref[k] for k in range(6))
        vb = v.astype(BF16)
        nl = -lam_ref[...]
        sp = jnp.maximum(nl, 0.0) + jnp.log1p(jnp.exp(-jnp.abs(nl)))
        gv = g_ref[...].astype(F32)
        dyv = dy_ref[...].astype(F32)
        sg = _sigmoid(gv)
        dp_ref[1] = (dyv * hs * (sg * (1.0 + gv * (1.0 - sg)))).astype(dp_ref.dtype)
        dhs = dyv * (gv * sg)
        d_h = _scan(_shift_up(a, 1), dhs, _shift_up)
        da = d_h * _shift_down(hs, 1)
        iv = i * v
        dlog_a = da * a - (d_h * iv) * (a * a) / mult
        di = d_h * mult * v
        dv = d_h * mult * i
        dzr = dlog_a * (-RGLRU_C) * sp * r * (1.0 - r)
        dzi = di * i * (1.0 - i)
        dsp = jnp.sum(dlog_a * r, axis=0, keepdims=True) * (-RGLRU_C)
        vec_ref[...] = jnp.zeros_like(vec_ref)
        vec_ref[0:1, :] = jnp.sum(dzr, axis=0, keepdims=True)
        vec_ref[1:2, :] = jnp.sum(dzi, axis=0, keepdims=True)
        vec_ref[2:3, :] = -dsp * _sigmoid(-lam_ref[...])
        dzr_b = dzr.astype(BF16)
        dzi_b = dzi.astype(BF16)
        vt = vb.astype(F32).T.astype(BF16)
        dwa_ref[...] = jnp.dot(vt, dzr_b, preferred_element_type=F32).astype(dwa_ref.dtype)
        dwx_ref[...] = jnp.dot(vt, dzi_b, preferred_element_type=F32).astype(dwx_ref.dtype)
        nt = (((1,), (1,)), ((), ()))
        dv = dv + lax.dot_general(dzr_b, wa_ref[...], nt, preferred_element_type=F32)
        dv = dv + lax.dot_general(dzi_b, wx_ref[...], nt, preferred_element_type=F32)
        vec_ref[3:4, :] = jnp.sum(dv, axis=0, keepdims=True)
        dvp = w_ref[width - 1:width, :] * dv
        vec_ref[4 + width - 1:4 + width, :] = jnp.sum(dv * v_pre, axis=0, keepdims=True)
        for k in range(width - 1):
            sh = width - 1 - k
            dvp = dvp + w_ref[k:k + 1, :] * _shift_up(dv, sh)
            vec_ref[4 + k:5 + k, :] = jnp.sum(dv * _shift_down(v_pre, sh), axis=0, keepdims=True)
        dp_ref[0] = dvp.astype(dp_ref.dtype)

    head_col, weights = _lru_specs(s, dh, heads, width)
    col = pl.BlockSpec((s, dh), lambda h: (0, h))
    mat = pl.BlockSpec((None, dh, dh), lambda h: (h, 0, 0))
    return pl.pallas_call(
        _after(body, 11, deps), name=name, grid=(heads,),
        in_specs=[head_col(0), head_col(1), pl.BlockSpec((6, s, dh), lambda h: (0, 0, h)), col] + weights
        + [ANY] * len(deps),
        out_specs=[pl.BlockSpec((2, s, dh), lambda h: (0, 0, h)), mat, mat,
                   pl.BlockSpec((16, dh), lambda h: (0, h))],
        out_shape=[jax.ShapeDtypeStruct((2, s, e), BF16),
                   jax.ShapeDtypeStruct((heads, dh, dh), BF16),
                   jax.ShapeDtypeStruct((heads, dh, dh), BF16),
                   jax.ShapeDtypeStruct((16, e), F32)],
        compiler_params=_params("arbitrary"),
    )(proj, proj, keep, dyb, conv_w, conv_b, w_a, b_a, w_x, b_x, lam, *deps)


def _ada_mod(c_all, w, b, name):
    layers, d, f = w.shape
    nb = c_all.shape[0]

    def body(c_ref, w_ref, b_ref, o_ref):
        cv = c_ref[...]
        sc = cv * _sigmoid(cv)
        o_ref[...] = jnp.dot(sc, w_ref[...], preferred_element_type=F32,
                             precision=lax.Precision.HIGHEST) + b_ref[...]

    return pl.pallas_call(
        body, name=name, grid=(layers,),
        in_specs=[pl.BlockSpec((nb, d), lambda l: (0, 0)),
                  pl.BlockSpec((None, d, f), lambda l: (l, 0, 0)),
                  pl.BlockSpec((None, 1, f), lambda l: (l, 0, 0))],
        out_specs=pl.BlockSpec((None, nb, f), lambda l: (l, 0, 0)),
        out_shape=jax.ShapeDtypeStruct((layers, nb, f), F32),
        compiler_params=_params("arbitrary"),
    )(c_all, w, b)


def _ada_update(c_all_t, dmod, w, m, v, name):
    d, nb = c_all_t.shape
    layers, _, f = dmod.shape
    tr = _tile(d, 512)

    def body(c_ref, dm_ref, w_ref, m_ref, v_ref, g_ref, d_ref, mo_ref, vo_ref):
        cv = c_ref[...]
        sc = cv * _sigmoid(cv)
        g = sc[:, 0:1] * dm_ref[0:1, :]
        for k in range(1, nb):
            g = g + sc[:, k:k + 1] * dm_ref[k:k + 1, :]
        g_ref[...] = g
        d_ref[...], mo_ref[...], vo_ref[...] = _adamw_math(w_ref[...], g, m_ref[...], v_ref[...])

    blk = pl.BlockSpec((None, tr, f), lambda l, i: (l, i, 0))
    return pl.pallas_call(
        body, name=name, grid=(layers, d // tr),
        in_specs=[pl.BlockSpec((tr, nb), lambda l, i: (i, 0)),
                  pl.BlockSpec((None, nb, f), lambda l, i: (l, 0, 0)), blk, blk, blk],
        out_specs=[blk] * 4,
        out_shape=[jax.ShapeDtypeStruct((layers, d, f), F32)] * 4,
        compiler_params=_params("arbitrary", "arbitrary"),
    )(c_all_t, dmod, w, m, v)


def _device_sum(g, name):
    _, rows, _ = g.shape

    def body(g_ref, o_ref):
        acc = g_ref[0]
        for k in range(1, N_DEV):
            acc = acc + g_ref[k]
        o_ref[...] = acc

    return pl.pallas_call(
        body, name=name,
        in_specs=[VMEM_SPEC], out_specs=VMEM_SPEC,
        out_shape=jax.ShapeDtypeStruct((rows, LANES), F32),
        compiler_params=pltpu.CompilerParams(vmem_limit_bytes=VMEM_LIMIT),
    )(g)


def _adamw_math(w, g, m, v):
    m = ADAM_B1 * m + (1.0 - ADAM_B1) * g
    v = ADAM_B2 * v + (1.0 - ADAM_B2) * (g * g)
    m_hat = m / (1.0 - ADAM_B1 ** ADAM_STEP)
    v_hat = v / (1.0 - ADAM_B2 ** ADAM_STEP)
    delta = -ADAM_LR * (m_hat / (jnp.sqrt(v_hat) + ADAM_EPS) + ADAM_WD * w)
    return delta, m, v


def _adamw(w, g, m, v, name):
    rows, cols = w.shape
    tr = _tile(rows, 256)

    def body(w_ref, g_ref, m_ref, v_ref, d_ref, mo_ref, vo_ref):
        d_ref[...], mo_ref[...], vo_ref[...] = _adamw_math(w_ref[...], g_ref[...], m_ref[...], v_ref[...])

    blk = pl.BlockSpec((tr, cols), lambda i: (i, 0))
    return pl.pallas_call(
        body, name=name, grid=(rows // tr,),
        in_specs=[blk] * 4, out_specs=[blk] * 3,
        out_shape=[jax.ShapeDtypeStruct((rows, cols), F32)] * 3,
        compiler_params=_params("arbitrary"),
    )(w, g, m, v)


def _adamw_reduced(idx, w, m, v, part, got, recvs, name):
    rows, cols = w.shape
    tr = _tile(rows, 256)
    nr = len(recvs)

    def body(idx_ref, w_ref, m_ref, v_ref, p_ref, q_ref, *rest):
        g_ref, d_ref, mo_ref, vo_ref = rest[nr:]
        g = p_ref[...].astype(F32) + q_ref[...].astype(F32)
        for u_ref in rest[:nr]:
            for j in range(u_ref.shape[0]):
                g = g + u_ref[j].astype(F32)
        g_ref[...] = g
        d_ref[...], mo_ref[...], vo_ref[...] = _adamw_math(w_ref[...], g, m_ref[...], v_ref[...])

    blk = pl.BlockSpec((tr, cols), lambda i, idx: (i, 0))
    grid_spec = pltpu.PrefetchScalarGridSpec(
        num_scalar_prefetch=1, grid=(rows // tr,),
        in_specs=[blk, blk, blk,
                  pl.BlockSpec((None, None, tr, cols), lambda i, idx: (idx[3], idx[4], i, 0)),
                  pl.BlockSpec((None, None, tr, cols), lambda i, idx: (idx[3], 0, i, 0))]
        + [pl.BlockSpec((u.shape[0], tr, cols), lambda i, idx: (0, i, 0)) for u in recvs],
        out_specs=[blk] * 4)
    return pl.pallas_call(
        body, name=name, grid_spec=grid_spec,
        out_shape=[jax.ShapeDtypeStruct((rows, cols), F32)] * 4,
        compiler_params=_params("arbitrary"),
    )(idx, w, m, v, part, got, *recvs)


def _pack(vectors):
    flat = jnp.concatenate([v.reshape(-1).astype(F32) for v in vectors])
    pad = (-flat.shape[0]) % (8 * LANES)
    return jnp.pad(flat, (0, pad)).reshape(-1, LANES)


def _unpack(flat, shapes):
    out, off = [], 0
    for shp in shapes:
        size = math.prod(shp)
        out.append(flat[..., off:off + size].reshape(flat.shape[:-1] + tuple(shp)))
        off += size
    return out


def _my_slice(full, me, axis):
    size = full.shape[axis] // N_DEV
    return lax.dynamic_slice_in_dim(full, me * size, size, axis)


def kernel(x, c, norm_g, ada_w, ada_b, sc_w_in, sc_conv_w, sc_w_out, lru_w_in, lru_conv_w, lru_conv_b, lru_w_a, lru_b_a, lru_w_x, lru_b_x, lru_lambda, lru_w_out, final_g, loss_target, m_norm_g, m_ada_w, m_ada_b, m_sc_w_in, m_sc_conv_w, m_sc_w_out, m_lru_w_in, m_lru_conv_w, m_lru_conv_b, m_lru_w_a, m_lru_b_a, m_lru_w_x, m_lru_b_x, m_lru_lambda, m_lru_w_out, m_final_g, v_norm_g, v_ada_w, v_ada_b, v_sc_w_in, v_sc_conv_w, v_sc_w_out, v_lru_w_in, v_lru_conv_w, v_lru_conv_b, v_lru_w_a, v_lru_b_a, v_lru_w_x, v_lru_b_x, v_lru_lambda, v_lru_w_out, v_final_g):
    _, s, d = x.shape
    e = sc_w_out.shape[1] * N_DEV
    heads, dh_s, dh = lru_w_a.shape[1:]
    es = e // N_DEV
    f = ada_w.shape[2]
    mx, my, mc = _position()
    me = 4 * mx + 2 * my + mc
    chip = 2 * mx + my
    idx = jnp.stack([chip ^ 1, chip ^ 2, chip ^ 3, chip, mc]).astype(jnp.int32)

    x0 = x[0]
    target = loss_target[0]

    small_shapes = [(d,), (3, es), (4, es), (es,), (heads, dh_s), (heads, dh_s), (es,)]
    small = _small_gather(_pack([c, sc_conv_w, lru_conv_w, lru_conv_b, lru_b_a, lru_b_x, lru_lambda]),
                          "gather_small_weights").reshape(N_DEV, -1)
    c_all, cw3, cw4, cb, ba, bx, lam = _unpack(small, small_shapes)
    cw3 = cw3.transpose(1, 0, 2).reshape(3, e)
    cw4 = cw4.transpose(1, 0, 2).reshape(4, e)
    cb = cb.reshape(1, e)
    lam = lam.reshape(1, e)
    ba = ba.transpose(1, 0, 2).reshape(1, e)
    bx = bx.transpose(1, 0, 2).reshape(1, e)

    shards = [sc_w_in[0].astype(BF16), sc_w_out[0].astype(BF16), lru_w_in[0].astype(BF16),
              lru_w_a[0].reshape(heads * dh_s, dh).astype(BF16),
              lru_w_x[0].reshape(heads * dh_s, dh).astype(BF16), lru_w_out[0].astype(BF16)]
    lands = [lax.dynamic_update_slice(lax.empty((N_DEV,) + sh.shape, BF16), sh[None], (me, 0, 0))
             for sh in shards]
    every = [1, 2, 3, 0]
    units = [([0], [0]), ([0], [1]), ([0], [2]), ([0], [3]), ([1], every), ([2], every), ([3, 4], every),
             ([5], every)]
    sems, first_sh, first_ld, started = _gather_start(shards[:1], lands[:1], units[:3], [small],
                                                      "gather_start_first")
    shards, lands = first_sh + shards[1:], first_ld + lands[1:]

    ada_b_mine = _my_slice(ada_b, me, 1).reshape(2, 1, f)
    mod_mine = _ada_mod(c_all, ada_w, ada_b_mine, "ada_mod")
    mod_all = _small_gather(_pack([mod_mine]), "gather_mod", deps=[started])

    def start_later(after):
        far_sems, far_sh, far_ld, tok = _gather_start(shards[:1], lands[:1], units[3:4], after, "gather_start_far")
        rest_units = [([i - 1 for i in members], ks) for members, ks in units[4:]]
        rest_sems, rest_sh, rest_ld, tok = _gather_start(shards[1:], lands[1:], rest_units, [tok],
                                                         "gather_start_rest")
        sems.extend(far_sems + rest_sems)
        shards[:], lands[:] = far_sh + rest_sh, far_ld + rest_ld
        return tok

    def gathered(u, after_forward, name):
        members, ks = units[u]
        fwd, shs, lnd, token = _gather_forward(
            [shards[i] for i in members], [lands[i] for i in members], ks, sems[u][0], sems[u][1],
            after_forward, "gather_forward_" + name)
        for i, sh, ld in zip(members, shs, lnd):
            shards[i], lands[i] = sh, ld

        def finish(after):
            out = _gather_finish([lands[i] for i in members], ks, fwd, after, "gather_finish_" + name)
            for i, ld in zip(members, out):
                lands[i] = ld
            return out

        return token, finish

    tok, finish_y = gathered(1, [mod_all], "sc_w_in_near_y")
    tok, finish_x = gathered(2, [tok], "sc_w_in_near_x")
    queued = start_later([tok])

    mod_all = mod_all.reshape(N_DEV, -1)
    mod_all = mod_all[:, :2 * N_DEV * f].reshape(N_DEV, 2, N_DEV, f)
    mod_all = mod_all.transpose(1, 2, 0, 3).reshape(2, N_DEV, 3 * d)
    mod = lax.dynamic_index_in_dim(mod_all, me, 1, keepdims=False)
    shift = [mod[l:l + 1, 0:d] for l in range(2)]
    scale = [mod[l:l + 1, d:2 * d] for l in range(2)]
    gate = [mod[l:l + 1, 2 * d:3 * d] for l in range(2)]
    ng = [norm_g[l:l + 1] for l in range(2)]
    fg = final_g.reshape(1, d)

    h0 = _norm_mod(x0, ng[0], scale[0], shift[0], "norm_mod_0", deps=[queued])
    proj0 = lax.empty((4, s, e), BF16)
    tok, _ = gathered(0, [h0], "sc_w_in_own")
    proj0 = _mm_proj_group(h0, lands[0], idx, 3, proj0, "mm_proj_0_own", deps=[tok])
    for u, name, finish in ((1, "near_y", finish_y), (2, "near_x", finish_x), (3, "far", None)):
        after = [proj0]
        if finish is None:
            tok, finish = gathered(u, [proj0], "sc_w_in_" + name)
            after = [tok]
        wg_in0, = finish(after)
        proj0 = _mm_proj_group(h0, wg_in0, idx, u - 1, proj0, "mm_proj_0_" + name)
    tok, finish = gathered(4, [proj0], "sc_w_out")
    yb0 = _sc_fwd(proj0, cw3, "sc_fwd", deps=[tok])
    w_out0 = finish([yb0])[0].reshape(e, d)
    x1, y0 = _mm_out(yb0, w_out0, x0, gate[0], "mm_out_0")
    tok, finish = gathered(5, [x1], "lru_w_in")
    h1 = _norm_mod(x1, ng[1], scale[1], shift[1], "norm_mod_1", deps=[tok])
    wg_in1, = finish([h1])
    proj1 = _mm_proj(h1, wg_in1, 2, "mm_proj_1")
    tok, finish = gathered(6, [proj1], "lru_gates")
    wg_a, wg_x = finish([tok])
    w_a = wg_a.reshape(N_DEV, heads, dh_s, dh).transpose(1, 0, 2, 3).reshape(heads, dh, dh)
    w_x = wg_x.reshape(N_DEV, heads, dh_s, dh).transpose(1, 0, 2, 3).reshape(heads, dh, dh)
    tok, finish = gathered(7, [w_a, w_x], "lru_w_out")
    yb1, hs = _lru_fwd(proj1, cw4, cb, w_a, ba, w_x, bx, lam, "lru_fwd", deps=[tok])
    w_out1 = finish([yb1])[0].reshape(e, d)
    x2, y1 = _mm_out(yb1, w_out1, x1, gate[1], "mm_out_1")
    dx2, loss_part, d_fg, dy1, dgate1 = _final_loss(x2, fg, target, y1, gate[1], "final_loss")

    def pieces(g, rows, cols):
        return g.reshape(4, 2, rows, cols)

    def by_rows(g):
        return g.reshape(heads, N_DEV, dh_s, dh).transpose(1, 0, 2, 3).reshape(N_DEV, heads * dh_s, dh)

    def pair_begin(parts, group):
        send, recv, parts, lnd, token = _pair_start(parts, "pair_start_" + group)
        return dict(parts=parts, lands=lnd, send=send, recv=recv, group=group), token

    def scatter_start(pair, names, after):
        group = pair["group"]
        parts, gots = _pair_wait(pair["parts"], pair["lands"], pair["send"], pair["recv"], after,
                                 "pair_wait_" + group)
        sums = [_pair_sum(idx, p, q, "pair_sum_" + nm) for p, q, nm in zip(parts, gots, names)]
        empties = [lax.empty(sm.shape, sm.dtype) for sm in sums]
        send, recv, sums, lnd, token = _chip_start(sums, empties, "chip_start_" + group)
        return dict(parts=parts, gots=gots, names=names, group=group, sums=sums, lands=lnd,
                    send=send, recv=recv), token

    big = {"sc_w_in": (sc_w_in, m_sc_w_in, v_sc_w_in), "sc_w_out": (sc_w_out, m_sc_w_out, v_sc_w_out),
           "lru_w_in": (lru_w_in, m_lru_w_in, v_lru_w_in), "lru_w_a": (lru_w_a, m_lru_w_a, v_lru_w_a),
           "lru_w_x": (lru_w_x, m_lru_w_x, v_lru_w_x), "lru_w_out": (lru_w_out, m_lru_w_out, v_lru_w_out)}
    big_res = {}

    def scatter_finish(rs, after):
        recvs = _chip_wait(rs["sums"], rs["lands"], rs["send"], rs["recv"], after, "chip_wait_" + rs["group"])
        done = []
        for p, q, u, nm in zip(rs["parts"], rs["gots"], recvs, rs["names"]):
            w, m, v = big[nm]
            shp2 = p.shape[2:]
            res = _adamw_reduced(idx, w.reshape(shp2), m.reshape(shp2), v.reshape(shp2), p, q, [u], "adamw_" + nm)
            big_res[nm] = [r.reshape(w.shape) for r in res]
            done.append(res[1])
        return done

    dw_out1 = _mm_tn(yb1, dy1[None], 1, "mm_dw_out_1")
    pair, tok = pair_begin([pieces(dw_out1, es, d)], "lru_w_out")
    dyb1 = _mm_nt(dy1[None], w_out1[None], BF16, "mm_dyb_1", deps=[tok])
    rs1, tok = scatter_start(pair, ["lru_w_out"], [dyb1])
    dproj1, dw_a, dw_x, vecs1 = _lru_bwd(proj1, hs, dyb1, cw4, cb, w_a, ba, w_x, bx, lam, "lru_bwd", deps=[tok])
    done = scatter_finish(rs1, [dproj1])
    dw_in1 = _mm_tn(h1, dproj1, N_DEV, "mm_dw_in_1", deps=done)
    pair, tok = pair_begin([pieces(dw_in1, d, 2 * es), pieces(by_rows(dw_a), heads * dh_s, dh),
                            pieces(by_rows(dw_x), heads * dh_s, dh)], "lru_in")
    dh1 = _mm_nt(dproj1, wg_in1, F32, "mm_dh_1", deps=[tok])
    rs2, tok = scatter_start(pair, ["lru_w_in", "lru_w_a", "lru_w_x"], [dh1])
    dx1, dscale1, dshift1, dng1, dy0, dgate0 = _norm_mod_bwd(x1, dh1, dx2, ng[1], scale[1], "norm_mod_bwd_1",
                                                             below=(y0, gate[0]), deps=[tok])
    dw_out0 = _mm_tn(yb0, dy0[None], 1, "mm_dw_out_0")
    pair, tok = pair_begin([pieces(dw_out0, es, d)], "sc_w_out")
    dyb0 = _mm_nt(dy0[None], w_out0[None], BF16, "mm_dyb_0", deps=[tok])
    rs3, tok = scatter_start(pair, ["sc_w_out"], [dyb0])
    dproj0, vecs0 = _sc_bwd(proj0, dyb0, cw3, "sc_bwd", deps=[tok])
    idx_one = jnp.stack([jnp.zeros_like(mc)] * 4 + [mc]).astype(jnp.int32)
    sc_w_in_steps = []

    def chip_step(j, pair, after):
        (part,), (got,) = _pair_wait(pair["parts"], pair["lands"], pair["send"], pair["recv"], after,
                                     "pair_wait_sc_w_in_%d" % j)
        sm = _pair_sum(idx_one, part, got, "pair_sum_sc_w_in_%d" % j, nslots=1)
        send, recv, sums, lnd, token = _chip_start([sm], [lax.empty(sm.shape, sm.dtype)],
                                                   "chip_start_sc_w_in_%d" % j, flips=(j,))
        sc_w_in_steps.append((sums, lnd, send, recv, j))
        return token

    pending, done = None, []
    for j in (3, 2, 1, 0):
        part = _mm_tn_group(h0, dproj0, idx, (j - 1) % 4, 2, "mm_dw_in_0_%d" % j, deps=done)[None]
        pair, tok = pair_begin([part], "sc_w_in_%d" % j)
        if j == 3:
            done = [chip_step(j, pair, [tok])]
            continue
        done = [tok]
        if pending is not None:
            done.append(chip_step(pending[0], pending[1], [tok]))
        pending = (j, pair)
    done += scatter_finish(rs2, done)
    dh0 = _mm_nt(dproj0, wg_in0, F32, "mm_dh_0", deps=done)
    pair = pending[1]
    (part,), (got,) = _pair_wait(pair["parts"], pair["lands"], pair["send"], pair["recv"], [dh0],
                                 "pair_wait_sc_w_in_0")
    dx0, dscale0, dshift0, dng0 = _norm_mod_bwd(x0, dh0, dx1, ng[0], scale[0], "norm_mod_bwd_0")
    done = scatter_finish(rs3, [dx0])
    dmod_mine = jnp.concatenate([dshift0, dscale0, dgate0, dshift1, dscale1, dgate1], axis=1)
    end_shapes = [(LANES,), (2, 3 * d), (2, d), (d,), (8, e), (16, e)]
    end_all = _small_gather(
        _pack([loss_part, dmod_mine, jnp.concatenate([dng0, dng1], axis=0), d_fg, vecs0, vecs1]),
        "gather_small_grads", deps=done)
    end_sum = _device_sum(end_all, "sum_small_grads").reshape(-1)
    loss_v, g_ada_b, g_norm_g, g_final_g, sum0, sum1 = _unpack(end_sum, end_shapes)
    loss = loss_v[0]
    dmod_all = _unpack(end_all.reshape(N_DEV, -1), end_shapes)[1].transpose(1, 0, 2)
    dmod_cols = _my_slice(dmod_all, me, 2)
    ada_out = _ada_update(c_all.T, dmod_cols, ada_w, m_ada_w, v_ada_w, "ada_update")

    g_sc_conv_w = _my_slice(sum0[0:3], me, 1)
    g_lru_b_a = _my_slice(sum1[0].reshape(heads, dh), me, 1)
    g_lru_b_x = _my_slice(sum1[1].reshape(heads, dh), me, 1)
    g_lru_lambda = _my_slice(sum1[2:3], me, 1)
    g_lru_conv_b = _my_slice(sum1[3:4], me, 1)
    g_lru_conv_w = _my_slice(sum1[4:8], me, 1)

    small_w = [norm_g, ada_b, final_g, sc_conv_w, lru_conv_w, lru_conv_b, lru_b_a, lru_b_x, lru_lambda]
    small_m = [m_norm_g, m_ada_b, m_final_g, m_sc_conv_w, m_lru_conv_w, m_lru_conv_b, m_lru_b_a, m_lru_b_x,
               m_lru_lambda]
    small_v = [v_norm_g, v_ada_b, v_final_g, v_sc_conv_w, v_lru_conv_w, v_lru_conv_b, v_lru_b_a, v_lru_b_x,
               v_lru_lambda]
    small_g = [g_norm_g, g_ada_b, g_final_g, g_sc_conv_w, g_lru_conv_w, g_lru_conv_b, g_lru_b_a, g_lru_b_x,
               g_lru_lambda]
    small_g = [g.reshape(w.shape) for g, w in zip(small_g, small_w)]
    shapes = [w.shape for w in small_w]
    packed = _adamw(_pack(small_w), _pack(small_g), _pack(small_m), _pack(small_v), "adamw_small")
    small_out = [small_g] + [_unpack(p.reshape(-1), shapes) for p in packed]

    after = [packed[0], ada_out[1]]
    recvs = []
    for sums, lnd, send, recv, j in sc_w_in_steps:
        recvs += _chip_wait(sums, lnd, send, recv, after, "chip_wait_sc_w_in_%d" % j)
    shp2 = part.shape[2:]
    res = _adamw_reduced(idx_one, sc_w_in.reshape(shp2), m_sc_w_in.reshape(shp2), v_sc_w_in.reshape(shp2),
                         part, got, recvs, "adamw_sc_w_in")
    big_res["sc_w_in"] = [r.reshape(sc_w_in.shape) for r in res]
    big_out = [big_res[nm] for nm in ("sc_w_in", "sc_w_out", "lru_w_in", "lru_w_a", "lru_w_x", "lru_w_out")]

    def small(kind, i):
        return small_out[kind][i]

    def bigw(kind, i):
        return big_out[i][kind]

    outs = [loss, dx0[None]]
    for kind in range(4):
        outs += [small(kind, 0), ada_out[kind], small(kind, 1), bigw(kind, 0), small(kind, 3), bigw(kind, 1),
                 bigw(kind, 2), small(kind, 4), small(kind, 5), bigw(kind, 3), small(kind, 6), bigw(kind, 4),
                 small(kind, 7), small(kind, 8), bigw(kind, 5), small(kind, 2)]
    return tuple(outs)
```

```python
import math

import jax
import jax.numpy as jnp
from jax import lax
from jax.experimental import pallas as pl
from jax.experimental.pallas import tpu as pltpu

N_DEV = 8
LANES = 128
EPS = 1e-6
RGLRU_C = 8.0
ADAM_LR = 0.001
ADAM_B1 = 0.9
ADAM_B2 = 0.999
ADAM_EPS = 1e-08
ADAM_WD = 0.01
ADAM_STEP = 10
VMEM_LIMIT = 56 * 1024 * 1024
MESH = pl.DeviceIdType.MESH
F32 = jnp.float32
BF16 = jnp.bfloat16
ANY = pl.BlockSpec(memory_space=pl.ANY)
HBM = pl.BlockSpec(memory_space=pltpu.HBM)
SEM = pl.BlockSpec(memory_space=pltpu.SEMAPHORE)
VMEM_SPEC = pl.BlockSpec(memory_space=pltpu.VMEM)
EFFECT = pltpu.SideEffectType.DATAFLOW_SIDE_EFFECTING
TOKEN = jax.ShapeDtypeStruct((8, LANES), jnp.float32)


def _tile(n, pref):
    t = min(n, pref)
    assert n % t == 0, (n, pref)
    return t


def _params(*sem):
    return pltpu.CompilerParams(dimension_semantics=sem, vmem_limit_bytes=VMEM_LIMIT)


def _position():
    return lax.axis_index("x"), lax.axis_index("y"), lax.axis_index("c")


def _flip(x, y, k):
    return (1 - x if k & 2 else x), (1 - y if k & 1 else y)


def _after(body, n_in, deps):
    if not deps:
        return body

    def wrapped(*refs):
        return body(*refs[:n_in], *refs[n_in + len(deps):])

    return wrapped


def _small_gather(v, name, deps=()):
    rows = v.shape[0]

    def body(v_ref, out_ref, send_sems, recv_sems):
        x, y, c = _position()
        me = 4 * x + 2 * y + c
        out_ref[me] = v_ref[...]
        copies = []
        for k in range(1, N_DEV):
            px, py = _flip(x, y, k >> 1)
            pc = 1 - c if k & 1 else c
            cp = pltpu.make_async_remote_copy(
                src_ref=v_ref, dst_ref=out_ref.at[me],
                send_sem=send_sems.at[k - 1], recv_sem=recv_sems.at[k - 1],
                device_id=(px, py, pc), device_id_type=MESH)
            cp.start()
            copies.append((cp, 4 * px + 2 * py + pc))
        for k, (cp, peer) in enumerate(copies):
            pltpu.make_async_remote_copy(
                src_ref=v_ref, dst_ref=out_ref.at[peer],
                send_sem=send_sems.at[k], recv_sem=recv_sems.at[k],
                device_id=(x, y, c), device_id_type=MESH).wait_recv()
        for cp, _ in copies:
            cp.wait_send()

    return pl.pallas_call(
        _after(body, 1, deps), name=name,
        out_shape=jax.ShapeDtypeStruct((N_DEV, rows, LANES), F32),
        in_specs=[VMEM_SPEC] + [ANY] * len(deps), out_specs=VMEM_SPEC,
        scratch_shapes=[pltpu.SemaphoreType.DMA((N_DEV - 1,)),
                        pltpu.SemaphoreType.DMA((N_DEV - 1,))],
        compiler_params=pltpu.CompilerParams(vmem_limit_bytes=VMEM_LIMIT),
    )(v, *deps)


def _hbm(a):
    return pltpu.with_memory_space_constraint(a, pltpu.HBM)


def _hbm_like(arrays):
    return [pltpu.HBM(a.shape, a.dtype) for a in arrays]


def _remote(src, dst, send, recv, to):
    return pltpu.make_async_remote_copy(src_ref=src, dst_ref=dst, send_sem=send, recv_sem=recv,
                                        device_id=to, device_id_type=MESH)


def _gather_start(shards, lands, units, after, name):
    n, nu = len(shards), len(units)

    def body(*refs):
        ins, lnd = refs[:n], refs[n:2 * n]
        sems = refs[2 * n + len(after):2 * n + len(after) + 2 * nu]
        token = refs[-1]
        x, y, c = _position()
        me = 4 * x + 2 * y + c
        targets = [(x, y, 1 - c)] + [(px, py, c) for px, py in (_flip(x, y, k) for k in (1, 2, 3))]
        for u, (members, ks) in enumerate(units):
            for slot, i in enumerate(members):
                for ki, k in enumerate(ks):
                    at = len(ks) * slot + ki
                    _remote(ins[i], lnd[i].at[me], sems[2 * u].at[at], sems[2 * u + 1].at[at], targets[k]).start()
        token[...] = jnp.zeros_like(token)

    sem_shapes = []
    for members, ks in units:
        count = len(members) * len(ks)
        sem_shapes += [pltpu.SemaphoreType.DMA((count,)), pltpu.SemaphoreType.DMA((count,))]
    out = pl.pallas_call(
        body, name=name,
        out_shape=sem_shapes + _hbm_like(shards) + _hbm_like(lands) + [TOKEN],
        in_specs=[HBM] * (2 * n) + [ANY] * len(after),
        out_specs=[SEM] * (2 * nu) + [HBM] * (2 * n) + [VMEM_SPEC],
        input_output_aliases={i: 2 * nu + i for i in range(2 * n)},
        compiler_params=pltpu.CompilerParams(has_side_effects=EFFECT),
    )(*[_hbm(s) for s in shards], *[_hbm(l) for l in lands], *after)
    sems = [(out[2 * u], out[2 * u + 1]) for u in range(nu)]
    return sems, list(out[2 * nu:2 * nu + n]), list(out[2 * nu + n:2 * nu + 2 * n]), out[-1]


def _gather_forward(shards, lands, ks, send, recv, after, name):
    m = len(shards)
    hops = [k for k in ks if k]
    nsem = 2 if hops else 0

    def body(*refs):
        ins, lnd = refs[:m], refs[m:2 * m]
        send_ref, recv_ref = refs[2 * m], refs[2 * m + 1]
        outs = refs[2 * m + 2 + len(after):]
        token = refs[-1]
        x, y, c = _position()
        me = (x, y, c)
        for slot in range(m):
            for ki, k in enumerate(ks):
                at = len(ks) * slot + ki
                if k:
                    px, py = _flip(x, y, k)
                    block = lnd[slot].at[4 * px + 2 * py + c]
                else:
                    block = lnd[slot].at[4 * x + 2 * y + (1 - c)]
                arrival = _remote(ins[slot], block, send_ref.at[at], recv_ref.at[at], me)
                arrival.wait_recv()
                if k:
                    fat = len(hops) * slot + hops.index(k)
                    _remote(block, block, outs[0].at[fat], outs[1].at[fat], (x, y, 1 - c)).start()
                arrival.wait_send()
        token[...] = jnp.zeros_like(token)

    count = len(hops) * m
    sem_shapes = [pltpu.SemaphoreType.DMA((count,)), pltpu.SemaphoreType.DMA((count,))] if hops else []
    out = pl.pallas_call(
        body, name=name,
        out_shape=sem_shapes + _hbm_like(shards) + _hbm_like(lands) + [TOKEN],
        in_specs=[HBM] * (2 * m) + [SEM, SEM] + [ANY] * len(after),
        out_specs=[SEM] * nsem + [HBM] * (2 * m) + [VMEM_SPEC],
        input_output_aliases={i: nsem + i for i in range(2 * m)},
        compiler_params=pltpu.CompilerParams(has_side_effects=EFFECT),
    )(*shards, *lands, send, recv, *after)
    fwd = (out[0], out[1]) if hops else None
    return fwd, list(out[nsem:nsem + m]), list(out[nsem + m:nsem + 2 * m]), out[-1]


def _gather_finish(lands, ks, fwd, after, name):
    m = len(lands)
    hops = [k for k in ks if k]

    def body(*refs):
        lnd = refs[:m]
        fsend_ref, frecv_ref = refs[m], refs[m + 1]
        x, y, c = _position()
        for slot in range(m):
            for fi, k in enumerate(hops):
                px, py = _flip(x, y, k)
                sent = lnd[slot].at[4 * px + 2 * py + c]
                came = lnd[slot].at[4 * px + 2 * py + (1 - c)]
                fat = len(hops) * slot + fi
                cp = _remote(sent, came, fsend_ref.at[fat], frecv_ref.at[fat], (x, y, c))
                cp.wait_recv()
                cp.wait_send()

    out = pl.pallas_call(
        body, name=name,
        out_shape=_hbm_like(lands),
        in_specs=[HBM] * m + [SEM, SEM] + [ANY] * len(after), out_specs=[HBM] * m,
        input_output_aliases={i: i for i in range(m)},
        compiler_params=pltpu.CompilerParams(has_side_effects=EFFECT),
    )(*lands, fwd[0], fwd[1], *after)
    return list(out)


def _pair_start(parts, name):
    n = len(parts)
    lands = [lax.empty((p.shape[0], 1) + p.shape[2:], p.dtype) for p in parts]

    def body(*refs):
        ins, lnd = refs[:n], refs[n:2 * n]
        send_ref, recv_ref = refs[2 * n], refs[2 * n + 1]
        token = refs[-1]
        x, y, c = _position()
        for i in range(n):
            _remote(ins[i].at[:, pl.ds(1 - c, 1)], lnd[i], send_ref.at[i], recv_ref.at[i], (x, y, 1 - c)).start()
        token[...] = jnp.zeros_like(token)

    out = pl.pallas_call(
        body, name=name,
        out_shape=[pltpu.SemaphoreType.DMA((n,)), pltpu.SemaphoreType.DMA((n,))]
        + _hbm_like(parts) + _hbm_like(lands) + [TOKEN],
        in_specs=[HBM] * (2 * n), out_specs=[SEM, SEM] + [HBM] * (2 * n) + [VMEM_SPEC],
        input_output_aliases={i: 2 + i for i in range(2 * n)},
        compiler_params=pltpu.CompilerParams(has_side_effects=EFFECT),
    )(*[_hbm(p) for p in parts], *[_hbm(l) for l in lands])
    return out[0], out[1], list(out[2:2 + n]), list(out[2 + n:2 + 2 * n]), out[-1]


def _pair_wait(parts, lands, send, recv, after, name):
    n = len(parts)

    def body(*refs):
        ins, lnd = refs[:n], refs[n:2 * n]
        send_ref, recv_ref = refs[2 * n], refs[2 * n + 1]
        x, y, c = _position()
        for i in range(n):
            cp = _remote(ins[i].at[:, pl.ds(1 - c, 1)], lnd[i], send_ref.at[i], recv_ref.at[i], (x, y, c))
            cp.wait_recv()
            cp.wait_send()

    out = pl.pallas_call(
        body, name=name,
        out_shape=_hbm_like(parts) + _hbm_like(lands),
        in_specs=[HBM] * (2 * n) + [SEM, SEM] + [ANY] * len(after), out_specs=[HBM] * (2 * n),
        input_output_aliases={i: i for i in range(2 * n)},
        compiler_params=pltpu.CompilerParams(has_side_effects=EFFECT),
    )(*parts, *lands, send, recv, *after)
    return list(out[:n]), list(out[n:])


def _chip_start(sums, lands, name, flips=(1, 2, 3)):
    n, ns = len(sums), len(flips)

    def body(*refs):
        ins, lnd = refs[:n], refs[n:2 * n]
        send_ref, recv_ref = refs[2 * n], refs[2 * n + 1]
        token = refs[-1]
        x, y, c = _position()
        for i in range(n):
            for j, flip in enumerate(flips):
                px, py = _flip(x, y, flip)
                _remote(ins[i].at[j], lnd[i].at[j], send_ref.at[ns * i + j], recv_ref.at[ns * i + j],
                        (px, py, c)).start()
        token[...] = jnp.zeros_like(token)

    out = pl.pallas_call(
        body, name=name,
        out_shape=[pltpu.SemaphoreType.DMA((ns * n,)), pltpu.SemaphoreType.DMA((ns * n,))]
        + _hbm_like(sums) + _hbm_like(lands) + [TOKEN],
        in_specs=[HBM] * (2 * n), out_specs=[SEM, SEM] + [HBM] * (2 * n) + [VMEM_SPEC],
        input_output_aliases={i: 2 + i for i in range(2 * n)},
        compiler_params=pltpu.CompilerParams(has_side_effects=EFFECT),
    )(*[_hbm(s) for s in sums], *[_hbm(l) for l in lands])
    return out[0], out[1], out[2:2 + n], out[2 + n:2 + 2 * n], out[-1]


def _chip_wait(sums, lands, send, recv, after, name):
    n, ns = len(sums), sums[0].shape[0]

    def body(*refs):
        ins, lnd = refs[:n], refs[n:2 * n]
        send_ref, recv_ref = refs[2 * n], refs[2 * n + 1]
        x, y, c = _position()
        for i in range(n):
            for j in range(ns):
                cp = _remote(ins[i].at[j], lnd[i].at[j], send_ref.at[ns * i + j], recv_ref.at[ns * i + j], (x, y, c))
                cp.wait_recv()
                cp.wait_send()

    out = pl.pallas_call(
        body, name=name,
        out_shape=_hbm_like(sums) + _hbm_like(lands),
        in_specs=[HBM] * (2 * n) + [SEM, SEM] + [ANY] * len(after), out_specs=[HBM] * (2 * n),
        input_output_aliases={i: i for i in range(2 * n)},
        compiler_params=pltpu.CompilerParams(has_side_effects=EFFECT),
    )(*sums, *lands, send, recv, *after)
    return list(out[n:])


def _pair_sum(idx, part, got, name, nslots=3):
    _, _, rows, cols = part.shape
    tr = _tile(rows, 1024)

    def body(idx_ref, p_ref, q_ref, o_ref):
        o_ref[...] = (p_ref[...].astype(F32) + q_ref[...].astype(F32)).astype(o_ref.dtype)

    grid_spec = pltpu.PrefetchScalarGridSpec(
        num_scalar_prefetch=1, grid=(nslots, rows // tr),
        in_specs=[pl.BlockSpec((None, None, tr, cols), lambda j, r, idx: (idx[j], idx[4], r, 0)),
                  pl.BlockSpec((None, None, tr, cols), lambda j, r, idx: (idx[j], 0, r, 0))],
        out_specs=pl.BlockSpec((None, tr, cols), lambda j, r, idx: (j, r, 0)))
    return pl.pallas_call(
        body, name=name, grid_spec=grid_spec,
        out_shape=jax.ShapeDtypeStruct((nslots, rows, cols), part.dtype),
        compiler_params=_params("arbitrary", "arbitrary"),
    )(idx, part, got)


def _mm_proj(h, wg, groups, name):
    s, k = h.shape
    nchunk, _, n = wg.shape
    e = nchunk * n // groups
    tn = _tile(min(n, e), 512)

    def body(h_ref, w_ref, o_ref):
        o_ref[...] = jnp.dot(h_ref[...], w_ref[...], preferred_element_type=F32).astype(o_ref.dtype)

    return pl.pallas_call(
        body, name=name, grid=(nchunk * n // tn,),
        in_specs=[pl.BlockSpec((s, k), lambda j: (0, 0)),
                  pl.BlockSpec((None, k, tn), lambda j: ((j * tn) // n, 0, ((j * tn) % n) // tn))],
        out_specs=pl.BlockSpec((None, s, tn), lambda j: ((j * tn) // e, 0, ((j * tn) % e) // tn)),
        out_shape=jax.ShapeDtypeStruct((groups, s, e), BF16),
        compiler_params=_params("arbitrary"),
    )(h, wg)


def _mm_proj_group(h, wg, idx, pos, prev, name, deps=()):
    s, k = h.shape
    _, _, n = wg.shape
    _, _, e = prev.shape
    tn = _tile(n, 512)
    nd = len(deps)

    def body(idx_ref, h_ref, w_ref, prev_ref, *rest):
        o_ref = rest[nd]
        o_ref[...] = jnp.dot(h_ref[...], w_ref[...], preferred_element_type=F32).astype(o_ref.dtype)

    def col(j, idx):
        return idx[pos] * (2 * n) + j * tn

    grid_spec = pltpu.PrefetchScalarGridSpec(
        num_scalar_prefetch=1, grid=(2 * n // tn,),
        in_specs=[pl.BlockSpec((s, k), lambda j, idx: (0, 0)),
                  pl.BlockSpec((None, k, tn), lambda j, idx: (col(j, idx) // n, 0, (col(j, idx) % n) // tn)),
                  ANY] + [ANY] * nd,
        out_specs=pl.BlockSpec((None, s, tn), lambda j, idx: (col(j, idx) // e, 0, (col(j, idx) % e) // tn)))
    return pl.pallas_call(
        body, name=name, grid_spec=grid_spec,
        out_shape=jax.ShapeDtypeStruct(prev.shape, prev.dtype),
        input_output_aliases={3: 0},
        compiler_params=_params("arbitrary"),
    )(idx, h, wg, prev, *deps)


def _mm_out(yb, w, x, gate, name):
    s, k = yb.shape
    d = w.shape[1]
    tn = _tile(d, 512)
    tk = _tile(k, 2048)
    nk = k // tk

    def body(a_ref, w_ref, x_ref, g_ref, xo_ref, y_ref, acc_ref):
        kk = pl.program_id(1)

        @pl.when(kk == 0)
        def _():
            acc_ref[...] = jnp.zeros_like(acc_ref)

        acc_ref[...] += jnp.dot(a_ref[...], w_ref[...], preferred_element_type=F32)

        @pl.when(kk == nk - 1)
        def _():
            y = acc_ref[...]
            y_ref[...] = y.astype(y_ref.dtype)
            xo_ref[...] = x_ref[...] + g_ref[...] * y

    return pl.pallas_call(
        body, name=name, grid=(d // tn, nk),
        in_specs=[pl.BlockSpec((s, tk), lambda j, kk: (0, kk)),
                  pl.BlockSpec((tk, tn), lambda j, kk: (kk, j)),
                  pl.BlockSpec((s, tn), lambda j, kk: (0, j)),
                  pl.BlockSpec((1, tn), lambda j, kk: (0, j))],
        out_specs=[pl.BlockSpec((s, tn), lambda j, kk: (0, j)),
                   pl.BlockSpec((s, tn), lambda j, kk: (0, j))],
        out_shape=[jax.ShapeDtypeStruct((s, d), F32), jax.ShapeDtypeStruct((s, d), BF16)],
        scratch_shapes=[pltpu.VMEM((s, tn), F32)],
        compiler_params=_params("arbitrary", "arbitrary"),
    )(yb, w, x, gate)


def _mm_nt(a3, w3, out_dtype, name, deps=()):
    g, s, ea = a3.shape
    cw, n, nw = w3.shape
    total = g * ea
    assert total == cw * nw
    tk = _tile(min(ea, nw), 2048 if out_dtype == BF16 else 1024)
    tn = _tile(n, 1024)
    nk = total // tk

    def body(a_ref, w_ref, o_ref, acc_ref):
        kk = pl.program_id(1)

        @pl.when(kk == 0)
        def _():
            acc_ref[...] = jnp.zeros_like(acc_ref)

        acc_ref[...] += lax.dot_general(a_ref[...], w_ref[...], (((1,), (1,)), ((), ())),
                                        preferred_element_type=F32)

        @pl.when(kk == nk - 1)
        def _():
            o_ref[...] = acc_ref[...].astype(o_ref.dtype)

    return pl.pallas_call(
        _after(body, 2, deps), name=name, grid=(n // tn, nk),
        in_specs=[pl.BlockSpec((None, s, tk), lambda j, kk: ((kk * tk) // ea, 0, ((kk * tk) % ea) // tk)),
                  pl.BlockSpec((None, tn, tk), lambda j, kk: ((kk * tk) // nw, j, ((kk * tk) % nw) // tk))]
        + [ANY] * len(deps),
        out_specs=pl.BlockSpec((s, tn), lambda j, kk: (0, j)),
        out_shape=jax.ShapeDtypeStruct((s, n), out_dtype),
        scratch_shapes=[pltpu.VMEM((s, tn), F32)],
        compiler_params=_params("arbitrary", "arbitrary"),
    )(a3, w3, *deps)


def _mm_tn(a, b3, nchunk, name, deps=()):
    s, ka = a.shape
    g, _, eb = b3.shape
    n = g * eb // nchunk
    tm = _tile(ka, 1024)
    tn = _tile(min(n, eb), 1024)

    def body(a_ref, b_ref, o_ref, at_ref):
        @pl.when(pl.program_id(1) == 0)
        def _():
            at_ref[...] = a_ref[...].astype(F32).T.astype(at_ref.dtype)

        o_ref[...] = jnp.dot(at_ref[...], b_ref[...], preferred_element_type=F32).astype(o_ref.dtype)

    return pl.pallas_call(
        _after(body, 2, deps), name=name, grid=(ka // tm, g * eb // tn),
        in_specs=[pl.BlockSpec((s, tm), lambda i, j: (0, i)),
                  pl.BlockSpec((None, s, tn), lambda i, j: ((j * tn) // eb, 0, ((j * tn) % eb) // tn))]
        + [ANY] * len(deps),
        out_specs=pl.BlockSpec((None, tm, tn), lambda i, j: ((j * tn) // n, i, ((j * tn) % n) // tn)),
        out_shape=jax.ShapeDtypeStruct((nchunk, ka, n), BF16),
        scratch_shapes=[pltpu.VMEM((tm, s), BF16)],
        compiler_params=_params("arbitrary", "arbitrary"),
    )(a, b3, *deps)


def _mm_tn_group(a, b3, idx, pos, nchunk, name, deps=()):
    s, ka = a.shape
    _, _, eb = b3.shape
    n = eb // nchunk
    tm = _tile(ka, 1024)
    tn = _tile(n, 1024)
    nd = len(deps)

    def body(idx_ref, a_ref, b_ref, *rest):
        o_ref, at_ref = rest[nd:]

        @pl.when(pl.program_id(1) == 0)
        def _():
            at_ref[...] = a_ref[...].astype(F32).T.astype(at_ref.dtype)

        o_ref[...] = jnp.dot(at_ref[...], b_ref[...], preferred_element_type=F32).astype(o_ref.dtype)

    grid_spec = pltpu.PrefetchScalarGridSpec(
        num_scalar_prefetch=1, grid=(ka // tm, eb // tn),
        in_specs=[pl.BlockSpec((s, tm), lambda i, j, idx: (0, i)),
                  pl.BlockSpec((None, s, tn), lambda i, j, idx: (idx[pos], 0, j))] + [ANY] * nd,
        out_specs=pl.BlockSpec((None, tm, tn), lambda i, j, idx: ((j * tn) // n, i, ((j * tn) % n) // tn)),
        scratch_shapes=[pltpu.VMEM((tm, s), BF16)])
    return pl.pallas_call(
        body, name=name, grid_spec=grid_spec,
        out_shape=jax.ShapeDtypeStruct((nchunk, ka, n), BF16),
        compiler_params=_params("arbitrary", "arbitrary"),
    )(idx, a, b3, *deps)


def _sigmoid(z):
    return jax.nn.sigmoid(z)


def _shift_down(v, k, fill=0.0, period=None):
    if k == 0:
        return v
    row = lax.broadcasted_iota(jnp.int32, v.shape, 0)
    if period is not None:
        row = row & (period - 1)
    return jnp.where(row >= k, pltpu.roll(v, k, 0), fill)


def _shift_up(v, k, fill=0.0, period=None):
    if k == 0:
        return v
    s = v.shape[0]
    row = lax.broadcasted_iota(jnp.int32, v.shape, 0)
    if period is not None:
        row, s = row & (period - 1), period
    return jnp.where(row < s - k, pltpu.roll(v, v.shape[0] - k, 0), fill)


SCAN_BLOCK = 64


def _scan(a, b, shift):
    s = a.shape[0]
    blk = min(SCAN_BLOCK, s)
    k = 1
    while k < blk:
        b = a * shift(b, k, 0.0, blk) + b
        a = a * shift(a, k, 1.0, blk)
        k *= 2
    nblk = s // blk
    forward = shift is _shift_down
    order = range(nblk) if forward else range(nblk - 1, -1, -1)
    edge = blk - 1 if forward else 0
    out = [None] * nblk
    carry = None
    for i in order:
        h = b[i * blk:(i + 1) * blk]
        if carry is not None:
            h = a[i * blk:(i + 1) * blk] * carry + h
        carry = h[edge:edge + 1]
        out[i] = h
    return jnp.concatenate(out, axis=0) if nblk > 1 else out[0]


def _norm_mod(x, g, scale, shift, name, deps=()):
    s, d = x.shape
    ts = _tile(s, 256)

    def body(x_ref, g_ref, sc_ref, sh_ref, h_ref):
        xv = x_ref[...]
        rstd = lax.rsqrt(jnp.mean(xv * xv, axis=-1, keepdims=True) + EPS)
        nrm = xv * rstd * g_ref[...]
        h_ref[...] = (nrm * (1.0 + sc_ref[...]) + sh_ref[...]).astype(h_ref.dtype)

    vec = pl.BlockSpec((1, d), lambda i: (0, 0))
    return pl.pallas_call(
        _after(body, 4, deps), name=name, grid=(s // ts,),
        in_specs=[pl.BlockSpec((ts, d), lambda i: (i, 0)), vec, vec, vec] + [ANY] * len(deps),
        out_specs=pl.BlockSpec((ts, d), lambda i: (i, 0)),
        out_shape=jax.ShapeDtypeStruct((s, d), BF16),
        compiler_params=_params("arbitrary"),
    )(x, g, scale, shift, *deps)


def _gate_terms(dx, y_ref, gate_ref, dy_ref, dgate_ref):
    dy_ref[...] = (dx * gate_ref[...]).astype(dy_ref.dtype)
    dgate_ref[...] += jnp.sum(dx * y_ref[...].astype(F32), axis=0, keepdims=True)


def _norm_mod_bwd(x, dh, dx_res, g, scale, name, below=None, deps=()):
    s, d = x.shape
    ts = _tile(s, 256)
    nb = 2 if below is not None else 0

    def body(x_ref, dh_ref, dr_ref, g_ref, sc_ref, *rest):
        dx_ref, dsc_ref, dsh_ref, dg_ref = rest[nb:nb + 4]

        @pl.when(pl.program_id(0) == 0)
        def _():
            for ref in rest[nb + 1:nb + 4] + rest[nb + 5:]:
                ref[...] = jnp.zeros_like(ref)

        xv = x_ref[...]
        dh_v = dh_ref[...].astype(F32)
        gv = g_ref[...]
        rstd = lax.rsqrt(jnp.mean(xv * xv, axis=-1, keepdims=True) + EPS)
        xhat = xv * rstd
        dsc_ref[...] += jnp.sum(dh_v * xhat * gv, axis=0, keepdims=True)
        dsh_ref[...] += jnp.sum(dh_v, axis=0, keepdims=True)
        dn = dh_v * (1.0 + sc_ref[...])
        dg_ref[...] += jnp.sum(dn * xhat, axis=0, keepdims=True)
        dxhat = dn * gv
        proj = jnp.mean(dxhat * xhat, axis=-1, keepdims=True)
        dx = dr_ref[...] + rstd * (dxhat - xhat * proj)
        dx_ref[...] = dx
        if nb:
            _gate_terms(dx, rest[0], rest[1], rest[nb + 4], rest[nb + 5])

    row = pl.BlockSpec((ts, d), lambda i: (i, 0))
    vec = pl.BlockSpec((1, d), lambda i: (0, 0))
    extra = list(below) if nb else []
    return pl.pallas_call(
        _after(body, 5 + nb, deps), name=name, grid=(s // ts,),
        in_specs=[row, row, row, vec, vec] + [row, vec][:nb] + [ANY] * len(deps),
        out_specs=[row, vec, vec, vec] + [row, vec][:nb],
        out_shape=[jax.ShapeDtypeStruct((s, d), F32)] + [jax.ShapeDtypeStruct((1, d), F32)] * 3
        + [jax.ShapeDtypeStruct((s, d), BF16), jax.ShapeDtypeStruct((1, d), F32)][:nb],
        compiler_params=_params("arbitrary"),
    )(x, dh, dx_res, g, scale, *extra, *deps)


def _final_loss(x, g, target, y, gate, name):
    s, d = x.shape
    ts = _tile(s, 256)

    def body(x_ref, g_ref, t_ref, y_ref, gate_ref, dx_ref, loss_ref, dg_ref, dy_ref, dgate_ref):
        @pl.when(pl.program_id(0) == 0)
        def _():
            loss_ref[...] = jnp.zeros_like(loss_ref)
            dg_ref[...] = jnp.zeros_like(dg_ref)
            dgate_ref[...] = jnp.zeros_like(dgate_ref)

        xv = x_ref[...]
        gv = g_ref[...]
        rstd = lax.rsqrt(jnp.mean(xv * xv, axis=-1, keepdims=True) + EPS)
        xhat = xv * rstd
        err = xhat * gv - t_ref[...]
        loss_ref[...] += 0.5 * jnp.sum(jnp.mean(err * err, axis=-1, keepdims=True))
        dy = err * (1.0 / d)
        dg_ref[...] += jnp.sum(dy * xhat, axis=0, keepdims=True)
        dxhat = dy * gv
        proj = jnp.mean(dxhat * xhat, axis=-1, keepdims=True)
        dx = rstd * (dxhat - xhat * proj)
        dx_ref[...] = dx
        _gate_terms(dx, y_ref, gate_ref, dy_ref, dgate_ref)

    row = pl.BlockSpec((ts, d), lambda i: (i, 0))
    vec = pl.BlockSpec((1, d), lambda i: (0, 0))
    return pl.pallas_call(
        body, name=name, grid=(s // ts,),
        in_specs=[row, vec, row, row, vec],
        out_specs=[row, pl.BlockSpec((1, LANES), lambda i: (0, 0)), vec, row, vec],
        out_shape=[jax.ShapeDtypeStruct((s, d), F32), jax.ShapeDtypeStruct((1, LANES), F32),
                   jax.ShapeDtypeStruct((1, d), F32), jax.ShapeDtypeStruct((s, d), BF16),
                   jax.ShapeDtypeStruct((1, d), F32)],
        compiler_params=_params("arbitrary"),
    )(x, g, target, y, gate)


def _conv(v, w_ref, width):
    out = w_ref[width - 1:width, :] * v
    for k in range(width - 1):
        out = out + w_ref[k:k + 1, :] * _shift_down(v, width - 1 - k)
    return out


def _sc_fwd(proj, conv_w, name, deps=()):
    _, s, e = proj.shape
    te = _tile(e, 256)
    width = conv_w.shape[0]

    def body(b_ref, c_ref, v_ref, g_ref, w_ref, o_ref):
        cv = c_ref[...].astype(F32) * v_ref[...].astype(F32)
        u = _conv(cv, w_ref, width)
        gv = g_ref[...].astype(F32)
        o_ref[...] = (b_ref[...].astype(F32) * u * (gv * _sigmoid(gv))).astype(o_ref.dtype)

    def part(q):
        return pl.BlockSpec((None, s, te), lambda j, q=q: (q, 0, j))

    return pl.pallas_call(
        _after(body, 5, deps), name=name, grid=(e // te,),
        in_specs=[part(0), part(1), part(2), part(3), pl.BlockSpec((width, te), lambda j: (0, j))]
        + [ANY] * len(deps),
        out_specs=pl.BlockSpec((s, te), lambda j: (0, j)),
        out_shape=jax.ShapeDtypeStruct((s, e), BF16),
        compiler_params=_params("arbitrary"),
    )(proj, proj, proj, proj, conv_w, *deps)


def _sc_bwd(proj, dyb, conv_w, name, deps=()):
    _, s, e = proj.shape
    te = _tile(e, 256)
    width = conv_w.shape[0]

    def body(b_ref, c_ref, v_ref, g_ref, dy_ref, w_ref, dp_ref, vec_ref):
        bv = b_ref[...].astype(F32)
        cvl = c_ref[...].astype(F32)
        vv = v_ref[...].astype(F32)
        gv = g_ref[...].astype(F32)
        dyv = dy_ref[...].astype(F32)
        cv = cvl * vv
        u = _conv(cv, w_ref, width)
        sg = _sigmoid(gv)
        silu = gv * sg
        dp_ref[0] = (dyv * u * silu).astype(dp_ref.dtype)
        du = dyv * bv * silu
        dp_ref[3] = (dyv * bv * u * (sg * (1.0 + gv * (1.0 - sg)))).astype(dp_ref.dtype)
        dcv = w_ref[width - 1:width, :] * du
        vec_ref[...] = jnp.zeros_like(vec_ref)
        vec_ref[width - 1:width, :] = jnp.sum(du * cv, axis=0, keepdims=True)
        for k in range(width - 1):
            sh = width - 1 - k
            dcv = dcv + w_ref[k:k + 1, :] * _shift_up(du, sh)
            vec_ref[k:k + 1, :] = jnp.sum(du * _shift_down(cv, sh), axis=0, keepdims=True)
        dp_ref[1] = (dcv * vv).astype(dp_ref.dtype)
        dp_ref[2] = (dcv * cvl).astype(dp_ref.dtype)

    def part(q):
        return pl.BlockSpec((None, s, te), lambda j, q=q: (q, 0, j))

    return pl.pallas_call(
        _after(body, 6, deps), name=name, grid=(e // te,),
        in_specs=[part(0), part(1), part(2), part(3), pl.BlockSpec((s, te), lambda j: (0, j)),
                  pl.BlockSpec((width, te), lambda j: (0, j))] + [ANY] * len(deps),
        out_specs=[pl.BlockSpec((4, s, te), lambda j: (0, 0, j)),
                   pl.BlockSpec((8, te), lambda j: (0, j))],
        out_shape=[jax.ShapeDtypeStruct((4, s, e), BF16), jax.ShapeDtypeStruct((8, e), F32)],
        compiler_params=_params("arbitrary"),
    )(proj, proj, proj, proj, dyb, conv_w, *deps)


def _lru_gates(v_pre, w_ref, cb_ref, wa_ref, ba_ref, wx_ref, bx_ref, lam_ref, width):
    v = _conv(v_pre, w_ref, width) + cb_ref[...]
    vb = v.astype(BF16)
    r = _sigmoid(jnp.dot(vb, wa_ref[...], preferred_element_type=F32) + ba_ref[...])
    i = _sigmoid(jnp.dot(vb, wx_ref[...], preferred_element_type=F32) + bx_ref[...])
    nl = -lam_ref[...]
    sp = jnp.maximum(nl, 0.0) + jnp.log1p(jnp.exp(-jnp.abs(nl)))
    log_a = (-RGLRU_C) * r * sp
    a = jnp.exp(log_a)
    one_minus_a2 = jnp.tanh(-log_a) * (1.0 + a * a)
    mult = jnp.sqrt(one_minus_a2)
    return v, vb, r, i, sp, a, mult


def _lru_specs(s, dh, heads, width):
    head_col = lambda q: pl.BlockSpec((None, s, dh), lambda h, q=q: (q, 0, h))
    vec = pl.BlockSpec((1, dh), lambda h: (0, h))
    mat = pl.BlockSpec((None, dh, dh), lambda h: (h, 0, 0))
    weights = [pl.BlockSpec((width, dh), lambda h: (0, h)), vec, mat, vec, mat, vec, vec]
    return head_col, weights


def _lru_fwd(proj, conv_w, conv_b, w_a, b_a, w_x, b_x, lam, name, deps=()):
    _, s, e = proj.shape
    heads, dh, _ = w_a.shape
    width = conv_w.shape[0]

    def body(v_ref, g_ref, w_ref, cb_ref, wa_ref, ba_ref, wx_ref, bx_ref, lam_ref, yb_ref, keep_ref):
        v, _, r, i, _, a, mult = _lru_gates(v_ref[...].astype(F32), w_ref, cb_ref, wa_ref, ba_ref,
                                           wx_ref, bx_ref, lam_ref, width)
        hs = _scan(a, mult * i * v, _shift_down)
        for k, val in enumerate((hs, v, r, i, a, mult)):
            keep_ref[k] = val
        gv = g_ref[...].astype(F32)
        yb_ref[...] = (hs * (gv * _sigmoid(gv))).astype(yb_ref.dtype)

    head_col, weights = _lru_specs(s, dh, heads, width)
    return pl.pallas_call(
        _after(body, 9, deps), name=name, grid=(heads,),
        in_specs=[head_col(0), head_col(1)] + weights + [ANY] * len(deps),
        out_specs=[pl.BlockSpec((s, dh), lambda h: (0, h)), pl.BlockSpec((6, s, dh), lambda h: (0, 0, h))],
        out_shape=[jax.ShapeDtypeStruct((s, e), BF16), jax.ShapeDtypeStruct((6, s, e), F32)],
        compiler_params=_params("arbitrary"),
    )(proj, proj, conv_w, conv_b, w_a, b_a, w_x, b_x, lam, *deps)


def _lru_bwd(proj, keep, dyb, conv_w, conv_b, w_a, b_a, w_x, b_x, lam, name, deps=()):
    _, s, e = proj.shape
    heads, dh, _ = w_a.shape
    width = conv_w.shape[0]

    def body(v_ref, g_ref, hs_ref, dy_ref, w_ref, cb_ref, wa_ref, ba_ref, wx_ref, bx_ref, lam_ref,
             dp_ref, dwa_ref, dwx_ref, vec_ref):
        v_pre = v_ref[...].astype(F32)
        hs, v, r, i, a, mult = (hs_ref[k] for k in range(6))
        vb = v.astype(BF16)
        nl = -lam_ref[...]
        sp = jnp.maximum(nl, 0.0) + jnp.log1p(jnp.exp(-jnp.abs(nl)))
        gv = g_ref[...].astype(F32)
        dyv = dy_ref[...].astype(F32)
        sg = _sigmoid(gv)
        dp_ref[1] = (dyv * hs * (sg * (1.0 + gv * (1.0 - sg)))).astype(dp_ref.dtype)
        dhs = dyv * (gv * sg)
        d_h = _scan(_shift_up(a, 1), dhs, _shift_up)
        da = d_h * _shift_down(hs, 1)
        iv = i * v
        dlog_a = da * a - (d_h * iv) * (a * a) / mult
        di = d_h * mult * v
        dv = d_h * mult * i
        dzr = dlog_a * (-RGLRU_C) * sp * r * (1.0 - r)
        dzi = di * i * (1.0 - i)
        dsp = jnp.sum(dlog_a * r, axis=0, keepdims=True) * (-RGLRU_C)
        vec_ref[...] = jnp.zeros_like(vec_ref)
        vec_ref[0:1, :] = jnp.sum(dzr, axis=0, keepdims=True)
        vec_ref[1:2, :] = jnp.sum(dzi, axis=0, keepdims=True)
        vec_ref[2:3, :] = -dsp * _sigmoid(-lam_ref[...])
        dzr_b = dzr.astype(BF16)
        dzi_b = dzi.astype(BF16)
        vt = vb.astype(F32).T.astype(BF16)
        dwa_ref[...] = jnp.dot(vt, dzr_b, preferred_element_type=F32).astype(dwa_ref.dtype)
        dwx_ref[...] = jnp.dot(vt, dzi_b, preferred_element_type=F32).astype(dwx_ref.dtype)
        nt = (((1,), (1,)), ((), ()))
        dv = dv + lax.dot_general(dzr_b, wa_ref[...], nt, preferred_element_type=F32)
        dv = dv + lax.dot_general(dzi_b, wx_ref[...], nt, preferred_element_type=F32)
        vec_ref[3:4, :] = jnp.sum(dv, axis=0, keepdims=True)
        dvp = w_ref[width - 1:width, :] * dv
        vec_ref[4 + width - 1:4 + width, :] = jnp.sum(dv * v_pre, axis=0, keepdims=True)
        for k in range(width - 1):
            sh = width - 1 - k
            dvp = dvp + w_ref[k:k + 1, :] * _shift_up(dv, sh)
            vec_ref[4 + k:5 + k, :] = jnp.sum(dv * _shift_down(v_pre, sh), axis=0, keepdims=True)
        dp_ref[0] = dvp.astype(dp_ref.dtype)

    head_col, weights = _lru_specs(s, dh, heads, width)
    col = pl.BlockSpec((s, dh), lambda h: (0, h))
    mat = pl.BlockSpec((None, dh, dh), lambda h: (h, 0, 0))
    return pl.pallas_call(
        _after(body, 11, deps), name=name, grid=(heads,),
        in_specs=[head_col(0), head_col(1), pl.BlockSpec((6, s, dh), lambda h: (0, 0, h)), col] + weights
        + [ANY] * len(deps),
        out_specs=[pl.BlockSpec((2, s, dh), lambda h: (0, 0, h)), mat, mat,
                   pl.BlockSpec((16, dh), lambda h: (0, h))],
        out_shape=[jax.ShapeDtypeStruct((2, s, e), BF16),
                   jax.ShapeDtypeStruct((heads, dh, dh), BF16),
                   jax.ShapeDtypeStruct((heads, dh, dh), BF16),
                   jax.ShapeDtypeStruct((16, e), F32)],
        compiler_params=_params("arbitrary"),
    )(proj, proj, keep, dyb, conv_w, conv_b, w_a, b_a, w_x, b_x, lam, *deps)


def _ada_mod(c_all, w, b, name):
    layers, d, f = w.shape
    nb = c_all.shape[0]

    def body(c_ref, w_ref, b_ref, o_ref):
        cv = c_ref[...]
        sc = cv * _sigmoid(cv)
        o_ref[...] = jnp.dot(sc, w_ref[...], preferred_element_type=F32,
                             precision=lax.Precision.HIGHEST) + b_ref[...]

    return pl.pallas_call(
        body, name=name, grid=(layers,),
        in_specs=[pl.BlockSpec((nb, d), lambda l: (0, 0)),
                  pl.BlockSpec((None, d, f), lambda l: (l, 0, 0)),
                  pl.BlockSpec((None, 1, f), lambda l: (l, 0, 0))],
        out_specs=pl.BlockSpec((None, nb, f), lambda l: (l, 0, 0)),
        out_shape=jax.ShapeDtypeStruct((layers, nb, f), F32),
        compiler_params=_params("arbitrary"),
    )(c_all, w, b)


def _ada_update(c_all_t, dmod, w, m, v, name):
    d, nb = c_all_t.shape
    layers, _, f = dmod.shape
    tr = _tile(d, 512)

    def body(c_ref, dm_ref, w_ref, m_ref, v_ref, g_ref, d_ref, mo_ref, vo_ref):
        cv = c_ref[...]
        sc = cv * _sigmoid(cv)
        g = sc[:, 0:1] * dm_ref[0:1, :]
        for k in range(1, nb):
            g = g + sc[:, k:k + 1] * dm_ref[k:k + 1, :]
        g_ref[...] = g
        d_ref[...], mo_ref[...], vo_ref[...] = _adamw_math(w_ref[...], g, m_ref[...], v_ref[...])

    blk = pl.BlockSpec((None, tr, f), lambda l, i: (l, i, 0))
    return pl.pallas_call(
        body, name=name, grid=(layers, d // tr),
        in_specs=[pl.BlockSpec((tr, nb), lambda l, i: (i, 0)),
                  pl.BlockSpec((None, nb, f), lambda l, i: (l, 0, 0)), blk, blk, blk],
        out_specs=[blk] * 4,
        out_shape=[jax.ShapeDtypeStruct((layers, d, f), F32)] * 4,
        compiler_params=_params("arbitrary", "arbitrary"),
    )(c_all_t, dmod, w, m, v)


def _device_sum(g, name):
    _, rows, _ = g.shape

    def body(g_ref, o_ref):
        acc = g_ref[0]
        for k in range(1, N_DEV):
            acc = acc + g_ref[k]
        o_ref[...] = acc

    return pl.pallas_call(
        body, name=name,
        in_specs=[VMEM_SPEC], out_specs=VMEM_SPEC,
        out_shape=jax.ShapeDtypeStruct((rows, LANES), F32),
        compiler_params=pltpu.CompilerParams(vmem_limit_bytes=VMEM_LIMIT),
    )(g)


def _adamw_math(w, g, m, v):
    m = ADAM_B1 * m + (1.0 - ADAM_B1) * g
    v = ADAM_B2 * v + (1.0 - ADAM_B2) * (g * g)
    m_hat = m / (1.0 - ADAM_B1 ** ADAM_STEP)
    v_hat = v / (1.0 - ADAM_B2 ** ADAM_STEP)
    delta = -ADAM_LR * (m_hat / (jnp.sqrt(v_hat) + ADAM_EPS) + ADAM_WD * w)
    return delta, m, v


def _adamw(w, g, m, v, name):
    rows, cols = w.shape
    tr = _tile(rows, 256)

    def body(w_ref, g_ref, m_ref, v_ref, d_ref, mo_ref, vo_ref):
        d_ref[...], mo_ref[...], vo_ref[...] = _adamw_math(w_ref[...], g_ref[...], m_ref[...], v_ref[...])

    blk = pl.BlockSpec((tr, cols), lambda i: (i, 0))
    return pl.pallas_call(
        body, name=name, grid=(rows // tr,),
        in_specs=[blk] * 4, out_specs=[blk] * 3,
        out_shape=[jax.ShapeDtypeStruct((rows, cols), F32)] * 3,
        compiler_params=_params("arbitrary"),
    )(w, g, m, v)


def _adamw_reduced(idx, w, m, v, part, got, recvs, name):
    rows, cols = w.shape
    tr = _tile(rows, 256)
    nr = len(recvs)

    def body(idx_ref, w_ref, m_ref, v_ref, p_ref, q_ref, *rest):
        g_ref, d_ref, mo_ref, vo_ref = rest[nr:]
        g = p_ref[...].astype(F32) + q_ref[...].astype(F32)
        for u_ref in rest[:nr]:
            for j in range(u_ref.shape[0]):
                g = g + u_ref[j].astype(F32)
        g_ref[...] = g
        d_ref[...], mo_ref[...], vo_ref[...] = _adamw_math(w_ref[...], g, m_ref[...], v_ref[...])

    blk = pl.BlockSpec((tr, cols), lambda i, idx: (i, 0))
    grid_spec = pltpu.PrefetchScalarGridSpec(
        num_scalar_prefetch=1, grid=(rows // tr,),
        in_specs=[blk, blk, blk,
                  pl.BlockSpec((None, None, tr, cols), lambda i, idx: (idx[3], idx[4], i, 0)),
                  pl.BlockSpec((None, None, tr, cols), lambda i, idx: (idx[3], 0, i, 0))]
        + [pl.BlockSpec((u.shape[0], tr, cols), lambda i, idx: (0, i, 0)) for u in recvs],
        out_specs=[blk] * 4)
    return pl.pallas_call(
        body, name=name, grid_spec=grid_spec,
        out_shape=[jax.ShapeDtypeStruct((rows, cols), F32)] * 4,
        compiler_params=_params("arbitrary"),
    )(idx, w, m, v, part, got, *recvs)


def _pack(vectors):
    flat = jnp.concatenate([v.reshape(-1).astype(F32) for v in vectors])
    pad = (-flat.shape[0]) % (8 * LANES)
    return jnp.pad(flat, (0, pad)).reshape(-1, LANES)


def _unpack(flat, shapes):
    out, off = [], 0
    for shp in shapes:
        size = math.prod(shp)
        out.append(flat[..., off:off + size].reshape(flat.shape[:-1] + tuple(shp)))
        off += size
    return out


def _my_slice(full, me, axis):
    size = full.shape[axis] // N_DEV
    return lax.dynamic_slice_in_dim(full, me * size, size, axis)


def kernel(x, c, norm_g, ada_w, ada_b, sc_w_in, sc_conv_w, sc_w_out, lru_w_in, lru_conv_w, lru_conv_b, lru_w_a, lru_b_a, lru_w_x, lru_b_x, lru_lambda, lru_w_out, final_g, loss_target, m_norm_g, m_ada_w, m_ada_b, m_sc_w_in, m_sc_conv_w, m_sc_w_out, m_lru_w_in, m_lru_conv_w, m_lru_conv_b, m_lru_w_a, m_lru_b_a, m_lru_w_x, m_lru_b_x, m_lru_lambda, m_lru_w_out, m_final_g, v_norm_g, v_ada_w, v_ada_b, v_sc_w_in, v_sc_conv_w, v_sc_w_out, v_lru_w_in, v_lru_conv_w, v_lru_conv_b, v_lru_w_a, v_lru_b_a, v_lru_w_x, v_lru_b_x, v_lru_lambda, v_lru_w_out, v_final_g):
    _, s, d = x.shape
    e = sc_w_out.shape[1] * N_DEV
    heads, dh_s, dh = lru_w_a.shape[1:]
    es = e // N_DEV
    f = ada_w.shape[2]
    mx, my, mc = _position()
    me = 4 * mx + 2 * my + mc
    chip = 2 * mx + my
    idx = jnp.stack([chip ^ 1, chip ^ 2, chip ^ 3, chip, mc]).astype(jnp.int32)

    x0 = x[0]
    target = loss_target[0]

    small_shapes = [(d,), (3, es), (4, es), (es,), (heads, dh_s), (heads, dh_s), (es,)]
    small = _small_gather(_pack([c, sc_conv_w, lru_conv_w, lru_conv_b, lru_b_a, lru_b_x, lru_lambda]),
                          "gather_small_weights").reshape(N_DEV, -1)
    c_all, cw3, cw4, cb, ba, bx, lam = _unpack(small, small_shapes)
    cw3 = cw3.transpose(1, 0, 2).reshape(3, e)
    cw4 = cw4.transpose(1, 0, 2).reshape(4, e)
    cb = cb.reshape(1, e)
    lam = lam.reshape(1, e)
    ba = ba.transpose(1, 0, 2).reshape(1, e)
    bx = bx.transpose(1, 0, 2).reshape(1, e)

    shards = [sc_w_in[0].astype(BF16), sc_w_out[0].astype(BF16), lru_w_in[0].astype(BF16),
              lru_w_a[0].reshape(heads * dh_s, dh).astype(BF16),
              lru_w_x[0].reshape(heads * dh_s, dh).astype(BF16), lru_w_out[0].astype(BF16)]
    lands = [lax.dynamic_update_slice(lax.empty((N_DEV,) + sh.shape, BF16), sh[None], (me, 0, 0))
             for sh in shards]
    every = [1, 2, 3, 0]
    units = [([0], [0]), ([0], [1]), ([0], [2]), ([0], [3]), ([1], every), ([2], every), ([3, 4], every),
             ([5], every)]
    sems, first_sh, first_ld, started = _gather_start(shards[:1], lands[:1], units[:3], [small],
                                                      "gather_start_first")
    shards, lands = first_sh + shards[1:], first_ld + lands[1:]

    ada_b_mine = _my_slice(ada_b, me, 1).reshape(2, 1, f)
    mod_mine = _ada_mod(c_all, ada_w, ada_b_mine, "ada_mod")
    mod_all = _small_gather(_pack([mod_mine]), "gather_mod", deps=[started])

    def start_later(after):
        far_sems, far_sh, far_ld, tok = _gather_start(shards[:1], lands[:1], units[3:4], after, "gather_start_far")
        rest_units = [([i - 1 for i in members], ks) for members, ks in units[4:]]
        rest_sems, rest_sh, rest_ld, tok = _gather_start(shards[1:], lands[1:], rest_units, [tok],
                                                         "gather_start_rest")
        sems.extend(far_sems + rest_sems)
        shards[:], lands[:] = far_sh + rest_sh, far_ld + rest_ld
        return tok

    def gathered(u, after_forward, name):
        members, ks = units[u]
        fwd, shs, lnd, token = _gather_forward(
            [shards[i] for i in members], [lands[i] for i in members], ks, sems[u][0], sems[u][1],
            after_forward, "gather_forward_" + name)
        for i, sh, ld in zip(members, shs, lnd):
            shards[i], lands[i] = sh, ld

        def finish(after):
            out = _gather_finish([lands[i] for i in members], ks, fwd, after, "gather_finish_" + name)
            for i, ld in zip(members, out):
                lands[i] = ld
            return out

        return token, finish

    tok, finish_y = gathered(1, [mod_all], "sc_w_in_near_y")
    tok, finish_x = gathered(2, [tok], "sc_w_in_near_x")
    queued = start_later([tok])

    mod_all = mod_all.reshape(N_DEV, -1)
    mod_all = mod_all[:, :2 * N_DEV * f].reshape(N_DEV, 2, N_DEV, f)
    mod_all = mod_all.transpose(1, 2, 0, 3).reshape(2, N_DEV, 3 * d)
    mod = lax.dynamic_index_in_dim(mod_all, me, 1, keepdims=False)
    shift = [mod[l:l + 1, 0:d] for l in range(2)]
    scale = [mod[l:l + 1, d:2 * d] for l in range(2)]
    gate = [mod[l:l + 1, 2 * d:3 * d] for l in range(2)]
    ng = [norm_g[l:l + 1] for l in range(2)]
    fg = final_g.reshape(1, d)

    h0 = _norm_mod(x0, ng[0], scale[0], shift[0], "norm_mod_0", deps=[queued])
    proj0 = lax.empty((4, s, e), BF16)
    tok, _ = gathered(0, [h0], "sc_w_in_own")
    proj0 = _mm_proj_group(h0, lands[0], idx, 3, proj0, "mm_proj_0_own", deps=[tok])
    for u, name, finish in ((1, "near_y", finish_y), (2, "near_x", finish_x), (3, "far", None)):
        after = [proj0]
        if finish is None:
            tok, finish = gathered(u, [proj0], "sc_w_in_" + name)
            after = [tok]
        wg_in0, = finish(after)
        proj0 = _mm_proj_group(h0, wg_in0, idx, u - 1, proj0, "mm_proj_0_" + name)
    tok, finish = gathered(4, [proj0], "sc_w_out")
    yb0 = _sc_fwd(proj0, cw3, "sc_fwd", deps=[tok])
    w_out0 = finish([yb0])[0].reshape(e, d)
    x1, y0 = _mm_out(yb0, w_out0, x0, gate[0], "mm_out_0")
    tok, finish = gathered(5, [x1], "lru_w_in")
    h1 = _norm_mod(x1, ng[1], scale[1], shift[1], "norm_mod_1", deps=[tok])
    wg_in1, = finish([h1])
    proj1 = _mm_proj(h1, wg_in1, 2, "mm_proj_1")
    tok, finish = gathered(6, [proj1], "lru_gates")
    wg_a, wg_x = finish([tok])
    w_a = wg_a.reshape(N_DEV, heads, dh_s, dh).transpose(1, 0, 2, 3).reshape(heads, dh, dh)
    w_x = wg_x.reshape(N_DEV, heads, dh_s, dh).transpose(1, 0, 2, 3).reshape(heads, dh, dh)
    tok, finish = gathered(7, [w_a, w_x], "lru_w_out")
    yb1, hs = _lru_fwd(proj1, cw4, cb, w_a, ba, w_x, bx, lam, "lru_fwd", deps=[tok])
    w_out1 = finish([yb1])[0].reshape(e, d)
    x2, y1 = _mm_out(yb1, w_out1, x1, gate[1], "mm_out_1")
    dx2, loss_part, d_fg, dy1, dgate1 = _final_loss(x2, fg, target, y1, gate[1], "final_loss")

    def pieces(g, rows, cols):
        return g.reshape(4, 2, rows, cols)

    def by_rows(g):
        return g.reshape(heads, N_DEV, dh_s, dh).transpose(1, 0, 2, 3).reshape(N_DEV, heads * dh_s, dh)

    def pair_begin(parts, group):
        send, recv, parts, lnd, token = _pair_start(parts, "pair_start_" + group)
        return dict(parts=parts, lands=lnd, send=send, recv=recv, group=group), token

    def scatter_start(pair, names, after):
        group = pair["group"]
        parts, gots = _pair_wait(pair["parts"], pair["lands"], pair["send"], pair["recv"], after,
                                 "pair_wait_" + group)
        sums = [_pair_sum(idx, p, q, "pair_sum_" + nm) for p, q, nm in zip(parts, gots, names)]
        empties = [lax.empty(sm.shape, sm.dtype) for sm in sums]
        send, recv, sums, lnd, token = _chip_start(sums, empties, "chip_start_" + group)
        return dict(parts=parts, gots=gots, names=names, group=group, sums=sums, lands=lnd,
                    send=send, recv=recv), token

    big = {"sc_w_in": (sc_w_in, m_sc_w_in, v_sc_w_in), "sc_w_out": (sc_w_out, m_sc_w_out, v_sc_w_out),
           "lru_w_in": (lru_w_in, m_lru_w_in, v_lru_w_in), "lru_w_a": (lru_w_a, m_lru_w_a, v_lru_w_a),
           "lru_w_x": (lru_w_x, m_lru_w_x, v_lru_w_x), "lru_w_out": (lru_w_out, m_lru_w_out, v_lru_w_out)}
    big_res = {}

    def scatter_finish(rs, after):
        recvs = _chip_wait(rs["sums"], rs["lands"], rs["send"], rs["recv"], after, "chip_wait_" + rs["group"])
        done = []
        for p, q, u, nm in zip(rs["parts"], rs["gots"], recvs, rs["names"]):
            w, m, v = big[nm]
            shp2 = p.shape[2:]
            res = _adamw_reduced(idx, w.reshape(shp2), m.reshape(shp2), v.reshape(shp2), p, q, [u], "adamw_" + nm)
            big_res[nm] = [r.reshape(w.shape) for r in res]
            done.append(res[1])
        return done

    dw_out1 = _mm_tn(yb1, dy1[None], 1, "mm_dw_out_1")
    pair, tok = pair_begin([pieces(dw_out1, es, d)], "lru_w_out")
    dyb1 = _mm_nt(dy1[None], w_out1[None], BF16, "mm_dyb_1", deps=[tok])
    rs1, tok = scatter_start(pair, ["lru_w_out"], [dyb1])
    dproj1, dw_a, dw_x, vecs1 = _lru_bwd(proj1, hs, dyb1, cw4, cb, w_a, ba, w_x, bx, lam, "lru_bwd", deps=[tok])
    done = scatter_finish(rs1, [dproj1])
    dw_in1 = _mm_tn(h1, dproj1, N_DEV, "mm_dw_in_1", deps=done)
    pair, tok = pair_begin([pieces(dw_in1, d, 2 * es), pieces(by_rows(dw_a), heads * dh_s, dh),
                            pieces(by_rows(dw_x), heads * dh_s, dh)], "lru_in")
    rs2, tok = scatter_start(pair, ["lru_w_in", "lru_w_a", "lru_w_x"], [tok])
    dh1 = _mm_nt(dproj1, wg_in1, F32, "mm_dh_1", deps=[tok])
    dx1, dscale1, dshift1, dng1, dy0, dgate0 = _norm_mod_bwd(x1, dh1, dx2, ng[1], scale[1], "norm_mod_bwd_1",
                                                             below=(y0, gate[0]), deps=[tok])
    dw_out0 = _mm_tn(yb0, dy0[None], 1, "mm_dw_out_0")
    pair, tok = pair_begin([pieces(dw_out0, es, d)], "sc_w_out")
    dyb0 = _mm_nt(dy0[None], w_out0[None], BF16, "mm_dyb_0", deps=[tok])
    rs3, tok = scatter_start(pair, ["sc_w_out"], [dyb0])
    dproj0, vecs0 = _sc_bwd(proj0, dyb0, cw3, "sc_bwd", deps=[tok])
    idx_one = jnp.stack([jnp.zeros_like(mc)] * 4 + [mc]).astype(jnp.int32)
    sc_w_in_steps = []

    def chip_step(j, pair, after):
        (part,), (got,) = _pair_wait(pair["parts"], pair["lands"], pair["send"], pair["recv"], after,
                                     "pair_wait_sc_w_in_%d" % j)
        sm = _pair_sum(idx_one, part, got, "pair_sum_sc_w_in_%d" % j, nslots=1)
        send, recv, sums, lnd, token = _chip_start([sm], [lax.empty(sm.shape, sm.dtype)],
                                                   "chip_start_sc_w_in_%d" % j, flips=(j,))
        sc_w_in_steps.append((sums, lnd, send, recv, j))
        return token

    pending, done = None, []
    for j in (3, 2, 1, 0):
        part = _mm_tn_group(h0, dproj0, idx, (j - 1) % 4, 2, "mm_dw_in_0_%d" % j, deps=done)[None]
        pair, tok = pair_begin([part], "sc_w_in_%d" % j)
        if j == 3:
            done = [chip_step(j, pair, [tok])]
            continue
        done = [tok]
        if pending is not None:
            done.append(chip_step(pending[0], pending[1], [tok]))
        pending = (j, pair)
    done += scatter_finish(rs2, done)
    dh0 = _mm_nt(dproj0, wg_in0, F32, "mm_dh_0", deps=done)
    pair = pending[1]
    (part,), (got,) = _pair_wait(pair["parts"], pair["lands"], pair["send"], pair["recv"], [dh0],
                                 "pair_wait_sc_w_in_0")
    dx0, dscale0, dshift0, dng0 = _norm_mod_bwd(x0, dh0, dx1, ng[0], scale[0], "norm_mod_bwd_0")
    done = scatter_finish(rs3, [dx0])
    dmod_mine = jnp.concatenate([dshift0, dscale0, dgate0, dshift1, dscale1, dgate1], axis=1)
    end_shapes = [(LANES,), (2, 3 * d), (2, d), (d,), (8, e), (16, e)]
    end_all = _small_gather(
        _pack([loss_part, dmod_mine, jnp.concatenate([dng0, dng1], axis=0), d_fg, vecs0, vecs1]),
        "gather_small_grads", deps=done)
    end_sum = _device_sum(end_all, "sum_small_grads").reshape(-1)
    loss_v, g_ada_b, g_norm_g, g_final_g, sum0, sum1 = _unpack(end_sum, end_shapes)
    loss = loss_v[0]
    dmod_all = _unpack(end_all.reshape(N_DEV, -1), end_shapes)[1].transpose(1, 0, 2)
    dmod_cols = _my_slice(dmod_all, me, 2)
    ada_out = _ada_update(c_all.T, dmod_cols, ada_w, m_ada_w, v_ada_w, "ada_update")

    g_sc_conv_w = _my_slice(sum0[0:3], me, 1)
    g_lru_b_a = _my_slice(sum1[0].reshape(heads, dh), me, 1)
    g_lru_b_x = _my_slice(sum1[1].reshape(heads, dh), me, 1)
    g_lru_lambda = _my_slice(sum1[2:3], me, 1)
    g_lru_conv_b = _my_slice(sum1[3:4], me, 1)
    g_lru_conv_w = _my_slice(sum1[4:8], me, 1)

    small_w = [norm_g, ada_b, final_g, sc_conv_w, lru_conv_w, lru_conv_b, lru_b_a, lru_b_x, lru_lambda]
    small_m = [m_norm_g, m_ada_b, m_final_g, m_sc_conv_w, m_lru_conv_w, m_lru_conv_b, m_lru_b_a, m_lru_b_x,
               m_lru_lambda]
    small_v = [v_norm_g, v_ada_b, v_final_g, v_sc_conv_w, v_lru_conv_w, v_lru_conv_b, v_lru_b_a, v_lru_b_x,
               v_lru_lambda]
    small_g = [g_norm_g, g_ada_b, g_final_g, g_sc_conv_w, g_lru_conv_w, g_lru_conv_b, g_lru_b_a, g_lru_b_x,
               g_lru_lambda]
    small_g = [g.reshape(w.shape) for g, w in zip(small_g, small_w)]
    shapes = [w.shape for w in small_w]
    packed = _adamw(_pack(small_w), _pack(small_g), _pack(small_m), _pack(small_v), "adamw_small")
    small_out = [small_g] + [_unpack(p.reshape(-1), shapes) for p in packed]

    after = [packed[0], ada_out[1]]
    recvs = []
    for sums, lnd, send, recv, j in sc_w_in_steps:
        recvs += _chip_wait(sums, lnd, send, recv, after, "chip_wait_sc_w_in_%d" % j)
    shp2 = part.shape[2:]
    res = _adamw_reduced(idx_one, sc_w_in.reshape(shp2), m_sc_w_in.reshape(shp2), v_sc_w_in.reshape(shp2),
                         part, got, recvs, "adamw_sc_w_in")
    big_res["sc_w_in"] = [r.reshape(sc_w_in.shape) for r in res]
    big_out = [big_res[nm] for nm in ("sc_w_in", "sc_w_out", "lru_w_in", "lru_w_a", "lru_w_x", "lru_w_out")]

    def small(kind, i):
        return small_out[kind][i]

    def bigw(kind, i):
        return big_out[i][kind]

    outs = [loss, dx0[None]]
    for kind in range(4):
        outs += [small(kind, 0), ada_out[kind], small(kind, 1), bigw(kind, 0), small(kind, 3), bigw(kind, 1),
                 bigw(kind, 2), small(kind, 4), small(kind, 5), bigw(kind, 3), small(kind, 6), bigw(kind, 4),
                 small(kind, 7), small(kind, 8), bigw(kind, 5), small(kind, 2)]
    return tuple(outs)
```

```python
import math

import jax
import jax.numpy as jnp
from jax import lax
from jax.experimental import pallas as pl
from jax.experimental.pallas import tpu as pltpu

N_DEV = 8
LANES = 128
EPS = 1e-6
RGLRU_C = 8.0
ADAM_LR = 0.001
ADAM_B1 = 0.9
ADAM_B2 = 0.999
ADAM_EPS = 1e-08
ADAM_WD = 0.01
ADAM_STEP = 10
VMEM_LIMIT = 56 * 1024 * 1024
MESH = pl.DeviceIdType.MESH
F32 = jnp.float32
BF16 = jnp.bfloat16
ANY = pl.BlockSpec(memory_space=pl.ANY)
HBM = pl.BlockSpec(memory_space=pltpu.HBM)
SEM = pl.BlockSpec(memory_space=pltpu.SEMAPHORE)
VMEM_SPEC = pl.BlockSpec(memory_space=pltpu.VMEM)
EFFECT = pltpu.SideEffectType.DATAFLOW_SIDE_EFFECTING
TOKEN = jax.ShapeDtypeStruct((8, LANES), jnp.float32)


def _tile(n, pref):
    t = min(n, pref)
    assert n % t == 0, (n, pref)
    return t


def _params(*sem):
    return pltpu.CompilerParams(dimension_semantics=sem, vmem_limit_bytes=VMEM_LIMIT)


def _position():
    return lax.axis_index("x"), lax.axis_index("y"), lax.axis_index("c")


def _flip(x, y, k):
    return (1 - x if k & 2 else x), (1 - y if k & 1 else y)


def _after(body, n_in, deps):
    if not deps:
        return body

    def wrapped(*refs):
        return body(*refs[:n_in], *refs[n_in + len(deps):])

    return wrapped


def _small_gather(v, name, deps=()):
    rows = v.shape[0]

    def body(v_ref, out_ref, send_sems, recv_sems):
        x, y, c = _position()
        me = 4 * x + 2 * y + c
        out_ref[me] = v_ref[...]
        copies = []
        for k in range(1, N_DEV):
            px, py = _flip(x, y, k >> 1)
            pc = 1 - c if k & 1 else c
            cp = pltpu.make_async_remote_copy(
                src_ref=v_ref, dst_ref=out_ref.at[me],
                send_sem=send_sems.at[k - 1], recv_sem=recv_sems.at[k - 1],
                device_id=(px, py, pc), device_id_type=MESH)
            cp.start()
            copies.append((cp, 4 * px + 2 * py + pc))
        for k, (cp, peer) in enumerate(copies):
            pltpu.make_async_remote_copy(
                src_ref=v_ref, dst_ref=out_ref.at[peer],
                send_sem=send_sems.at[k], recv_sem=recv_sems.at[k],
                device_id=(x, y, c), device_id_type=MESH).wait_recv()
        for cp, _ in copies:
            cp.wait_send()

    return pl.pallas_call(
        _after(body, 1, deps), name=name,
        out_shape=jax.ShapeDtypeStruct((N_DEV, rows, LANES), F32),
        in_specs=[VMEM_SPEC] + [ANY] * len(deps), out_specs=VMEM_SPEC,
        scratch_shapes=[pltpu.SemaphoreType.DMA((N_DEV - 1,)),
                        pltpu.SemaphoreType.DMA((N_DEV - 1,))],
        compiler_params=pltpu.CompilerParams(vmem_limit_bytes=VMEM_LIMIT),
    )(v, *deps)


def _hbm(a):
    return pltpu.with_memory_space_constraint(a, pltpu.HBM)


def _hbm_like(arrays):
    return [pltpu.HBM(a.shape, a.dtype) for a in arrays]


def _remote(src, dst, send, recv, to):
    return pltpu.make_async_remote_copy(src_ref=src, dst_ref=dst, send_sem=send, recv_sem=recv,
                                        device_id=to, device_id_type=MESH)


def _gather_start(shards, lands, units, after, name):
    n, nu = len(shards), len(units)

    def body(*refs):
        ins, lnd = refs[:n], refs[n:2 * n]
        sems = refs[2 * n + len(after):2 * n + len(after) + 2 * nu]
        token = refs[-1]
        x, y, c = _position()
        me = 4 * x + 2 * y + c
        targets = [(x, y, 1 - c)] + [(px, py, c) for px, py in (_flip(x, y, k) for k in (1, 2, 3))]
        for u, (members, ks) in enumerate(units):
            for slot, i in enumerate(members):
                for ki, k in enumerate(ks):
                    at = len(ks) * slot + ki
                    _remote(ins[i], lnd[i].at[me], sems[2 * u].at[at], sems[2 * u + 1].at[at], targets[k]).start()
        token[...] = jnp.zeros_like(token)

    sem_shapes = []
    for members, ks in units:
        count = len(members) * len(ks)
        sem_shapes += [pltpu.SemaphoreType.DMA((count,)), pltpu.SemaphoreType.DMA((count,))]
    out = pl.pallas_call(
        body, name=name,
        out_shape=sem_shapes + _hbm_like(shards) + _hbm_like(lands) + [TOKEN],
        in_specs=[HBM] * (2 * n) + [ANY] * len(after),
        out_specs=[SEM] * (2 * nu) + [HBM] * (2 * n) + [VMEM_SPEC],
        input_output_aliases={i: 2 * nu + i for i in range(2 * n)},
        compiler_params=pltpu.CompilerParams(has_side_effects=EFFECT),
    )(*[_hbm(s) for s in shards], *[_hbm(l) for l in lands], *after)
    sems = [(out[2 * u], out[2 * u + 1]) for u in range(nu)]
    return sems, list(out[2 * nu:2 * nu + n]), list(out[2 * nu + n:2 * nu + 2 * n]), out[-1]


def _gather_forward(shards, lands, ks, send, recv, after, name):
    m = len(shards)
    hops = [k for k in ks if k]
    nsem = 2 if hops else 0

    def body(*refs):
        ins, lnd = refs[:m], refs[m:2 * m]
        send_ref, recv_ref = refs[2 * m], refs[2 * m + 1]
        outs = refs[2 * m + 2 + len(after):]
        token = refs[-1]
        x, y, c = _position()
        me = (x, y, c)
        for slot in range(m):
            for ki, k in enumerate(ks):
                at = len(ks) * slot + ki
                if k:
                    px, py = _flip(x, y, k)
                    block = lnd[slot].at[4 * px + 2 * py + c]
                else:
                    block = lnd[slot].at[4 * x + 2 * y + (1 - c)]
                arrival = _remote(ins[slot], block, send_ref.at[at], recv_ref.at[at], me)
                arrival.wait_recv()
                if k:
                    fat = len(hops) * slot + hops.index(k)
                    _remote(block, block, outs[0].at[fat], outs[1].at[fat], (x, y, 1 - c)).start()
                arrival.wait_send()
        token[...] = jnp.zeros_like(token)

    count = len(hops) * m
    sem_shapes = [pltpu.SemaphoreType.DMA((count,)), pltpu.SemaphoreType.DMA((count,))] if hops else []
    out = pl.pallas_call(
        body, name=name,
        out_shape=sem_shapes + _hbm_like(shards) + _hbm_like(lands) + [TOKEN],
        in_specs=[HBM] * (2 * m) + [SEM, SEM] + [ANY] * len(after),
        out_specs=[SEM] * nsem + [HBM] * (2 * m) + [VMEM_SPEC],
        input_output_aliases={i: nsem + i for i in range(2 * m)},
        compiler_params=pltpu.CompilerParams(has_side_effects=EFFECT),
    )(*shards, *lands, send, recv, *after)
    fwd = (out[0], out[1]) if hops else None
    return fwd, list(out[nsem:nsem + m]), list(out[nsem + m:nsem + 2 * m]), out[-1]


def _gather_finish(lands, ks, fwd, after, name):
    m = len(lands)
    hops = [k for k in ks if k]

    def body(*refs):
        lnd = refs[:m]
        fsend_ref, frecv_ref = refs[m], refs[m + 1]
        x, y, c = _position()
        for slot in range(m):
            for fi, k in enumerate(hops):
                px, py = _flip(x, y, k)
                sent = lnd[slot].at[4 * px + 2 * py + c]
                came = lnd[slot].at[4 * px + 2 * py + (1 - c)]
                fat = len(hops) * slot + fi
                cp = _remote(sent, came, fsend_ref.at[fat], frecv_ref.at[fat], (x, y, c))
                cp.wait_recv()
                cp.wait_send()

    out = pl.pallas_call(
        body, name=name,
        out_shape=_hbm_like(lands),
        in_specs=[HBM] * m + [SEM, SEM] + [ANY] * len(after), out_specs=[HBM] * m,
        input_output_aliases={i: i for i in range(m)},
        compiler_params=pltpu.CompilerParams(has_side_effects=EFFECT),
    )(*lands, fwd[0], fwd[1], *after)
    return list(out)


def _pair_start(parts, name):
    n = len(parts)
    lands = [lax.empty((p.shape[0], 1) + p.shape[2:], p.dtype) for p in parts]

    def body(*refs):
        ins, lnd = refs[:n], refs[n:2 * n]
        send_ref, recv_ref = refs[2 * n], refs[2 * n + 1]
        token = refs[-1]
        x, y, c = _position()
        for i in range(n):
            _remote(ins[i].at[:, pl.ds(1 - c, 1)], lnd[i], send_ref.at[i], recv_ref.at[i], (x, y, 1 - c)).start()
        token[...] = jnp.zeros_like(token)

    out = pl.pallas_call(
        body, name=name,
        out_shape=[pltpu.SemaphoreType.DMA((n,)), pltpu.SemaphoreType.DMA((n,))]
        + _hbm_like(parts) + _hbm_like(lands) + [TOKEN],
        in_specs=[HBM] * (2 * n), out_specs=[SEM, SEM] + [HBM] * (2 * n) + [VMEM_SPEC],
        input_output_aliases={i: 2 + i for i in range(2 * n)},
        compiler_params=pltpu.CompilerParams(has_side_effects=EFFECT),
    )(*[_hbm(p) for p in parts], *[_hbm(l) for l in lands])
    return out[0], out[1], list(out[2:2 + n]), list(out[2 + n:2 + 2 * n]), out[-1]


def _pair_wait(parts, lands, send, recv, after, name):
    n = len(parts)

    def body(*refs):
        ins, lnd = refs[:n], refs[n:2 * n]
        send_ref, recv_ref = refs[2 * n], refs[2 * n + 1]
        x, y, c = _position()
        for i in range(n):
            cp = _remote(ins[i].at[:, pl.ds(1 - c, 1)], lnd[i], send_ref.at[i], recv_ref.at[i], (x, y, c))
            cp.wait_recv()
            cp.wait_send()

    out = pl.pallas_call(
        body, name=name,
        out_shape=_hbm_like(parts) + _hbm_like(lands),
        in_specs=[HBM] * (2 * n) + [SEM, SEM] + [ANY] * len(after), out_specs=[HBM] * (2 * n),
        input_output_aliases={i: i for i in range(2 * n)},
        compiler_params=pltpu.CompilerParams(has_side_effects=EFFECT),
    )(*parts, *lands, send, recv, *after)
    return list(out[:n]), list(out[n:])


def _chip_start(sums, lands, name, flips=(1, 2, 3)):
    n, ns = len(sums), len(flips)

    def body(*refs):
        ins, lnd = refs[:n], refs[n:2 * n]
        send_ref, recv_ref = refs[2 * n], refs[2 * n + 1]
        token = refs[-1]
        x, y, c = _position()
        for i in range(n):
            for j, flip in enumerate(flips):
                px, py = _flip(x, y, flip)
                _remote(ins[i].at[j], lnd[i].at[j], send_ref.at[ns * i + j], recv_ref.at[ns * i + j],
                        (px, py, c)).start()
        token[...] = jnp.zeros_like(token)

    out = pl.pallas_call(
        body, name=name,
        out_shape=[pltpu.SemaphoreType.DMA((ns * n,)), pltpu.SemaphoreType.DMA((ns * n,))]
        + _hbm_like(sums) + _hbm_like(lands) + [TOKEN],
        in_specs=[HBM] * (2 * n), out_specs=[SEM, SEM] + [HBM] * (2 * n) + [VMEM_SPEC],
        input_output_aliases={i: 2 + i for i in range(2 * n)},
        compiler_params=pltpu.CompilerParams(has_side_effects=EFFECT),
    )(*[_hbm(s) for s in sums], *[_hbm(l) for l in lands])
    return out[0], out[1], out[2:2 + n], out[2 + n:2 + 2 * n], out[-1]


def _chip_wait(sums, lands, send, recv, after, name):
    n, ns = len(sums), sums[0].shape[0]

    def body(*refs):
        ins, lnd = refs[:n], refs[n:2 * n]
        send_ref, recv_ref = refs[2 * n], refs[2 * n + 1]
        x, y, c = _position()
        for i in range(n):
            for j in range(ns):
                cp = _remote(ins[i].at[j], lnd[i].at[j], send_ref.at[ns * i + j], recv_ref.at[ns * i + j], (x, y, c))
                cp.wait_recv()
                cp.wait_send()

    out = pl.pallas_call(
        body, name=name,
        out_shape=_hbm_like(sums) + _hbm_like(lands),
        in_specs=[HBM] * (2 * n) + [SEM, SEM] + [ANY] * len(after), out_specs=[HBM] * (2 * n),
        input_output_aliases={i: i for i in range(2 * n)},
        compiler_params=pltpu.CompilerParams(has_side_effects=EFFECT),
    )(*sums, *lands, send, recv, *after)
    return list(out[n:])


def _pair_sum(idx, part, got, name, nslots=3):
    _, _, rows, cols = part.shape
    tr = _tile(rows, 1024)

    def body(idx_ref, p_ref, q_ref, o_ref):
        o_ref[...] = (p_ref[...].astype(F32) + q_ref[...].astype(F32)).astype(o_ref.dtype)

    grid_spec = pltpu.PrefetchScalarGridSpec(
        num_scalar_prefetch=1, grid=(nslots, rows // tr),
        in_specs=[pl.BlockSpec((None, None, tr, cols), lambda j, r, idx: (idx[j], idx[4], r, 0)),
                  pl.BlockSpec((None, None, tr, cols), lambda j, r, idx: (idx[j], 0, r, 0))],
        out_specs=pl.BlockSpec((None, tr, cols), lambda j, r, idx: (j, r, 0)))
    return pl.pallas_call(
        body, name=name, grid_spec=grid_spec,
        out_shape=jax.ShapeDtypeStruct((nslots, rows, cols), part.dtype),
        compiler_params=_params("arbitrary", "arbitrary"),
    )(idx, part, got)


def _mm_proj(h, wg, groups, name):
    s, k = h.shape
    nchunk, _, n = wg.shape
    e = nchunk * n // groups
    tn = _tile(min(n, e), 512)

    def body(h_ref, w_ref, o_ref):
        o_ref[...] = jnp.dot(h_ref[...], w_ref[...], preferred_element_type=F32).astype(o_ref.dtype)

    return pl.pallas_call(
        body, name=name, grid=(nchunk * n // tn,),
        in_specs=[pl.BlockSpec((s, k), lambda j: (0, 0)),
                  pl.BlockSpec((None, k, tn), lambda j: ((j * tn) // n, 0, ((j * tn) % n) // tn))],
        out_specs=pl.BlockSpec((None, s, tn), lambda j: ((j * tn) // e, 0, ((j * tn) % e) // tn)),
        out_shape=jax.ShapeDtypeStruct((groups, s, e), BF16),
        compiler_params=_params("arbitrary"),
    )(h, wg)


def _mm_proj_group(h, wg, idx, pos, prev, name, deps=()):
    s, k = h.shape
    _, _, n = wg.shape
    _, _, e = prev.shape
    tn = _tile(n, 512)
    nd = len(deps)

    def body(idx_ref, h_ref, w_ref, prev_ref, *rest):
        o_ref = rest[nd]
        o_ref[...] = jnp.dot(h_ref[...], w_ref[...], preferred_element_type=F32).astype(o_ref.dtype)

    def col(j, idx):
        return idx[pos] * (2 * n) + j * tn

    grid_spec = pltpu.PrefetchScalarGridSpec(
        num_scalar_prefetch=1, grid=(2 * n // tn,),
        in_specs=[pl.BlockSpec((s, k), lambda j, idx: (0, 0)),
                  pl.BlockSpec((None, k, tn), lambda j, idx: (col(j, idx) // n, 0, (col(j, idx) % n) // tn)),
                  ANY] + [ANY] * nd,
        out_specs=pl.BlockSpec((None, s, tn), lambda j, idx: (col(j, idx) // e, 0, (col(j, idx) % e) // tn)))
    return pl.pallas_call(
        body, name=name, grid_spec=grid_spec,
        out_shape=jax.ShapeDtypeStruct(prev.shape, prev.dtype),
        input_output_aliases={3: 0},
        compiler_params=_params("arbitrary"),
    )(idx, h, wg, prev, *deps)


def _mm_out(yb, w, x, gate, name):
    s, k = yb.shape
    d = w.shape[1]
    tn = _tile(d, 512)
    tk = _tile(k, 2048)
    nk = k // tk

    def body(a_ref, w_ref, x_ref, g_ref, xo_ref, y_ref, acc_ref):
        kk = pl.program_id(1)

        @pl.when(kk == 0)
        def _():
            acc_ref[...] = jnp.zeros_like(acc_ref)

        acc_ref[...] += jnp.dot(a_ref[...], w_ref[...], preferred_element_type=F32)

        @pl.when(kk == nk - 1)
        def _():
            y = acc_ref[...]
            y_ref[...] = y.astype(y_ref.dtype)
            xo_ref[...] = x_ref[...] + g_ref[...] * y

    return pl.pallas_call(
        body, name=name, grid=(d // tn, nk),
        in_specs=[pl.BlockSpec((s, tk), lambda j, kk: (0, kk)),
                  pl.BlockSpec((tk, tn), lambda j, kk: (kk, j)),
                  pl.BlockSpec((s, tn), lambda j, kk: (0, j)),
                  pl.BlockSpec((1, tn), lambda j, kk: (0, j))],
        out_specs=[pl.BlockSpec((s, tn), lambda j, kk: (0, j)),
                   pl.BlockSpec((s, tn), lambda j, kk: (0, j))],
        out_shape=[jax.ShapeDtypeStruct((s, d), F32), jax.ShapeDtypeStruct((s, d), BF16)],
        scratch_shapes=[pltpu.VMEM((s, tn), F32)],
        compiler_params=_params("arbitrary", "arbitrary"),
    )(yb, w, x, gate)


def _mm_nt(a3, w3, out_dtype, name, deps=()):
    g, s, ea = a3.shape
    cw, n, nw = w3.shape
    total = g * ea
    assert total == cw * nw
    tk = _tile(min(ea, nw), 2048)
    tn = _tile(n, 1024)
    nk = total // tk

    def body(a_ref, w_ref, o_ref, acc_ref):
        kk = pl.program_id(1)

        @pl.when(kk == 0)
        def _():
            acc_ref[...] = jnp.zeros_like(acc_ref)

        acc_ref[...] += lax.dot_general(a_ref[...], w_ref[...], (((1,), (1,)), ((), ())),
                                        preferred_element_type=F32)

        @pl.when(kk == nk - 1)
        def _():
            o_ref[...] = acc_ref[...].astype(o_ref.dtype)

    return pl.pallas_call(
        _after(body, 2, deps), name=name, grid=(n // tn, nk),
        in_specs=[pl.BlockSpec((None, s, tk), lambda j, kk: ((kk * tk) // ea, 0, ((kk * tk) % ea) // tk)),
                  pl.BlockSpec((None, tn, tk), lambda j, kk: ((kk * tk) // nw, j, ((kk * tk) % nw) // tk))]
        + [ANY] * len(deps),
        out_specs=pl.BlockSpec((s, tn), lambda j, kk: (0, j)),
        out_shape=jax.ShapeDtypeStruct((s, n), out_dtype),
        scratch_shapes=[pltpu.VMEM((s, tn), F32)],
        compiler_params=_params("arbitrary", "arbitrary"),
    )(a3, w3, *deps)


def _mm_tn(a, b3, nchunk, name, deps=()):
    s, ka = a.shape
    g, _, eb = b3.shape
    n = g * eb // nchunk
    tm = _tile(ka, 1024)
    tn = _tile(min(n, eb), 1024)

    def body(a_ref, b_ref, o_ref, at_ref):
        @pl.when(pl.program_id(1) == 0)
        def _():
            at_ref[...] = a_ref[...].astype(F32).T.astype(at_ref.dtype)

        o_ref[...] = jnp.dot(at_ref[...], b_ref[...], preferred_element_type=F32).astype(o_ref.dtype)

    return pl.pallas_call(
        _after(body, 2, deps), name=name, grid=(ka // tm, g * eb // tn),
        in_specs=[pl.BlockSpec((s, tm), lambda i, j: (0, i)),
                  pl.BlockSpec((None, s, tn), lambda i, j: ((j * tn) // eb, 0, ((j * tn) % eb) // tn))]
        + [ANY] * len(deps),
        out_specs=pl.BlockSpec((None, tm, tn), lambda i, j: ((j * tn) // n, i, ((j * tn) % n) // tn)),
        out_shape=jax.ShapeDtypeStruct((nchunk, ka, n), BF16),
        scratch_shapes=[pltpu.VMEM((tm, s), BF16)],
        compiler_params=_params("arbitrary", "arbitrary"),
    )(a, b3, *deps)


def _mm_tn_group(a, b3, idx, pos, nchunk, name, deps=()):
    s, ka = a.shape
    _, _, eb = b3.shape
    n = eb // nchunk
    tm = _tile(ka, 1024)
    tn = _tile(n, 1024)
    nd = len(deps)

    def body(idx_ref, a_ref, b_ref, *rest):
        o_ref, at_ref = rest[nd:]

        @pl.when(pl.program_id(1) == 0)
        def _():
            at_ref[...] = a_ref[...].astype(F32).T.astype(at_ref.dtype)

        o_ref[...] = jnp.dot(at_ref[...], b_ref[...], preferred_element_type=F32).astype(o_ref.dtype)

    grid_spec = pltpu.PrefetchScalarGridSpec(
        num_scalar_prefetch=1, grid=(ka // tm, eb // tn),
        in_specs=[pl.BlockSpec((s, tm), lambda i, j, idx: (0, i)),
                  pl.BlockSpec((None, s, tn), lambda i, j, idx: (idx[pos], 0, j))] + [ANY] * nd,
        out_specs=pl.BlockSpec((None, tm, tn), lambda i, j, idx: ((j * tn) // n, i, ((j * tn) % n) // tn)),
        scratch_shapes=[pltpu.VMEM((tm, s), BF16)])
    return pl.pallas_call(
        body, name=name, grid_spec=grid_spec,
        out_shape=jax.ShapeDtypeStruct((nchunk, ka, n), BF16),
        compiler_params=_params("arbitrary", "arbitrary"),
    )(idx, a, b3, *deps)


def _sigmoid(z):
    return jax.nn.sigmoid(z)


def _shift_down(v, k, fill=0.0, period=None):
    if k == 0:
        return v
    row = lax.broadcasted_iota(jnp.int32, v.shape, 0)
    if period is not None:
        row = row & (period - 1)
    return jnp.where(row >= k, pltpu.roll(v, k, 0), fill)


def _shift_up(v, k, fill=0.0, period=None):
    if k == 0:
        return v
    s = v.shape[0]
    row = lax.broadcasted_iota(jnp.int32, v.shape, 0)
    if period is not None:
        row, s = row & (period - 1), period
    return jnp.where(row < s - k, pltpu.roll(v, v.shape[0] - k, 0), fill)


SCAN_BLOCK = 64


def _scan(a, b, shift):
    s = a.shape[0]
    blk = min(SCAN_BLOCK, s)
    k = 1
    while k < blk:
        b = a * shift(b, k, 0.0, blk) + b
        a = a * shift(a, k, 1.0, blk)
        k *= 2
    nblk = s // blk
    forward = shift is _shift_down
    order = range(nblk) if forward else range(nblk - 1, -1, -1)
    edge = blk - 1 if forward else 0
    out = [None] * nblk
    carry = None
    for i in order:
        h = b[i * blk:(i + 1) * blk]
        if carry is not None:
            h = a[i * blk:(i + 1) * blk] * carry + h
        carry = h[edge:edge + 1]
        out[i] = h
    return jnp.concatenate(out, axis=0) if nblk > 1 else out[0]


def _norm_mod(x, g, scale, shift, name, deps=()):
    s, d = x.shape
    ts = _tile(s, 256)

    def body(x_ref, g_ref, sc_ref, sh_ref, h_ref):
        xv = x_ref[...]
        rstd = lax.rsqrt(jnp.mean(xv * xv, axis=-1, keepdims=True) + EPS)
        nrm = xv * rstd * g_ref[...]
        h_ref[...] = (nrm * (1.0 + sc_ref[...]) + sh_ref[...]).astype(h_ref.dtype)

    vec = pl.BlockSpec((1, d), lambda i: (0, 0))
    return pl.pallas_call(
        _after(body, 4, deps), name=name, grid=(s // ts,),
        in_specs=[pl.BlockSpec((ts, d), lambda i: (i, 0)), vec, vec, vec] + [ANY] * len(deps),
        out_specs=pl.BlockSpec((ts, d), lambda i: (i, 0)),
        out_shape=jax.ShapeDtypeStruct((s, d), BF16),
        compiler_params=_params("arbitrary"),
    )(x, g, scale, shift, *deps)


def _gate_terms(dx, y_ref, gate_ref, dy_ref, dgate_ref):
    dy_ref[...] = (dx * gate_ref[...]).astype(dy_ref.dtype)
    dgate_ref[...] += jnp.sum(dx * y_ref[...].astype(F32), axis=0, keepdims=True)


def _norm_mod_bwd(x, dh, dx_res, g, scale, name, below=None, deps=()):
    s, d = x.shape
    ts = _tile(s, 256)
    nb = 2 if below is not None else 0

    def body(x_ref, dh_ref, dr_ref, g_ref, sc_ref, *rest):
        dx_ref, dsc_ref, dsh_ref, dg_ref = rest[nb:nb + 4]

        @pl.when(pl.program_id(0) == 0)
        def _():
            for ref in rest[nb + 1:nb + 4] + rest[nb + 5:]:
                ref[...] = jnp.zeros_like(ref)

        xv = x_ref[...]
        dh_v = dh_ref[...].astype(F32)
        gv = g_ref[...]
        rstd = lax.rsqrt(jnp.mean(xv * xv, axis=-1, keepdims=True) + EPS)
        xhat = xv * rstd
        dsc_ref[...] += jnp.sum(dh_v * xhat * gv, axis=0, keepdims=True)
        dsh_ref[...] += jnp.sum(dh_v, axis=0, keepdims=True)
        dn = dh_v * (1.0 + sc_ref[...])
        dg_ref[...] += jnp.sum(dn * xhat, axis=0, keepdims=True)
        dxhat = dn * gv
        proj = jnp.mean(dxhat * xhat, axis=-1, keepdims=True)
        dx = dr_ref[...] + rstd * (dxhat - xhat * proj)
        dx_ref[...] = dx
        if nb:
            _gate_terms(dx, rest[0], rest[1], rest[nb + 4], rest[nb + 5])

    row = pl.BlockSpec((ts, d), lambda i: (i, 0))
    vec = pl.BlockSpec((1, d), lambda i: (0, 0))
    extra = list(below) if nb else []
    return pl.pallas_call(
        _after(body, 5 + nb, deps), name=name, grid=(s // ts,),
        in_specs=[row, row, row, vec, vec] + [row, vec][:nb] + [ANY] * len(deps),
        out_specs=[row, vec, vec, vec] + [row, vec][:nb],
        out_shape=[jax.ShapeDtypeStruct((s, d), F32)] + [jax.ShapeDtypeStruct((1, d), F32)] * 3
        + [jax.ShapeDtypeStruct((s, d), BF16), jax.ShapeDtypeStruct((1, d), F32)][:nb],
        compiler_params=_params("arbitrary"),
    )(x, dh, dx_res, g, scale, *extra, *deps)


def _final_loss(x, g, target, y, gate, name):
    s, d = x.shape
    ts = _tile(s, 256)

    def body(x_ref, g_ref, t_ref, y_ref, gate_ref, dx_ref, loss_ref, dg_ref, dy_ref, dgate_ref):
        @pl.when(pl.program_id(0) == 0)
        def _():
            loss_ref[...] = jnp.zeros_like(loss_ref)
            dg_ref[...] = jnp.zeros_like(dg_ref)
            dgate_ref[...] = jnp.zeros_like(dgate_ref)

        xv = x_ref[...]
        gv = g_ref[...]
        rstd = lax.rsqrt(jnp.mean(xv * xv, axis=-1, keepdims=True) + EPS)
        xhat = xv * rstd
        err = xhat * gv - t_ref[...]
        loss_ref[...] += 0.5 * jnp.sum(jnp.mean(err * err, axis=-1, keepdims=True))
        dy = err * (1.0 / d)
        dg_ref[...] += jnp.sum(dy * xhat, axis=0, keepdims=True)
        dxhat = dy * gv
        proj = jnp.mean(dxhat * xhat, axis=-1, keepdims=True)
        dx = rstd * (dxhat - xhat * proj)
        dx_ref[...] = dx
        _gate_terms(dx, y_ref, gate_ref, dy_ref, dgate_ref)

    row = pl.BlockSpec((ts, d), lambda i: (i, 0))
    vec = pl.BlockSpec((1, d), lambda i: (0, 0))
    return pl.pallas_call(
        body, name=name, grid=(s // ts,),
        in_specs=[row, vec, row, row, vec],
        out_specs=[row, pl.BlockSpec((1, LANES), lambda i: (0, 0)), vec, row, vec],
        out_shape=[jax.ShapeDtypeStruct((s, d), F32), jax.ShapeDtypeStruct((1, LANES), F32),
                   jax.ShapeDtypeStruct((1, d), F32), jax.ShapeDtypeStruct((s, d), BF16),
                   jax.ShapeDtypeStruct((1, d), F32)],
        compiler_params=_params("arbitrary"),
    )(x, g, target, y, gate)


def _conv(v, w_ref, width):
    out = w_ref[width - 1:width, :] * v
    for k in range(width - 1):
        out = out + w_ref[k:k + 1, :] * _shift_down(v, width - 1 - k)
    return out


def _sc_fwd(proj, conv_w, name, deps=()):
    _, s, e = proj.shape
    te = _tile(e, 256)
    width = conv_w.shape[0]

    def body(b_ref, c_ref, v_ref, g_ref, w_ref, o_ref):
        cv = c_ref[...].astype(F32) * v_ref[...].astype(F32)
        u = _conv(cv, w_ref, width)
        gv = g_ref[...].astype(F32)
        o_ref[...] = (b_ref[...].astype(F32) * u * (gv * _sigmoid(gv))).astype(o_ref.dtype)

    def part(q):
        return pl.BlockSpec((None, s, te), lambda j, q=q: (q, 0, j))

    return pl.pallas_call(
        _after(body, 5, deps), name=name, grid=(e // te,),
        in_specs=[part(0), part(1), part(2), part(3), pl.BlockSpec((width, te), lambda j: (0, j))]
        + [ANY] * len(deps),
        out_specs=pl.BlockSpec((s, te), lambda j: (0, j)),
        out_shape=jax.ShapeDtypeStruct((s, e), BF16),
        compiler_params=_params("arbitrary"),
    )(proj, proj, proj, proj, conv_w, *deps)


def _sc_bwd(proj, dyb, conv_w, name, deps=()):
    _, s, e = proj.shape
    te = _tile(e, 256)
    width = conv_w.shape[0]

    def body(b_ref, c_ref, v_ref, g_ref, dy_ref, w_ref, dp_ref, vec_ref):
        bv = b_ref[...].astype(F32)
        cvl = c_ref[...].astype(F32)
        vv = v_ref[...].astype(F32)
        gv = g_ref[...].astype(F32)
        dyv = dy_ref[...].astype(F32)
        cv = cvl * vv
        u = _conv(cv, w_ref, width)
        sg = _sigmoid(gv)
        silu = gv * sg
        dp_ref[0] = (dyv * u * silu).astype(dp_ref.dtype)
        du = dyv * bv * silu
        dp_ref[3] = (dyv * bv * u * (sg * (1.0 + gv * (1.0 - sg)))).astype(dp_ref.dtype)
        dcv = w_ref[width - 1:width, :] * du
        vec_ref[...] = jnp.zeros_like(vec_ref)
        vec_ref[width - 1:width, :] = jnp.sum(du * cv, axis=0, keepdims=True)
        for k in range(width - 1):
            sh = width - 1 - k
            dcv = dcv + w_ref[k:k + 1, :] * _shift_up(du, sh)
            vec_ref[k:k + 1, :] = jnp.sum(du * _shift_down(cv, sh), axis=0, keepdims=True)
        dp_ref[1] = (dcv * vv).astype(dp_ref.dtype)
        dp_ref[2] = (dcv * cvl).astype(dp_ref.dtype)

    def part(q):
        return pl.BlockSpec((None, s, te), lambda j, q=q: (q, 0, j))

    return pl.pallas_call(
        _after(body, 6, deps), name=name, grid=(e // te,),
        in_specs=[part(0), part(1), part(2), part(3), pl.BlockSpec((s, te), lambda j: (0, j)),
                  pl.BlockSpec((width, te), lambda j: (0, j))] + [ANY] * len(deps),
        out_specs=[pl.BlockSpec((4, s, te), lambda j: (0, 0, j)),
                   pl.BlockSpec((8, te), lambda j: (0, j))],
        out_shape=[jax.ShapeDtypeStruct((4, s, e), BF16), jax.ShapeDtypeStruct((8, e), F32)],
        compiler_params=_params("arbitrary"),
    )(proj, proj, proj, proj, dyb, conv_w, *deps)


def _lru_gates(v_pre, w_ref, cb_ref, wa_ref, ba_ref, wx_ref, bx_ref, lam_ref, width):
    v = _conv(v_pre, w_ref, width) + cb_ref[...]
    vb = v.astype(BF16)
    r = _sigmoid(jnp.dot(vb, wa_ref[...], preferred_element_type=F32) + ba_ref[...])
    i = _sigmoid(jnp.dot(vb, wx_ref[...], preferred_element_type=F32) + bx_ref[...])
    nl = -lam_ref[...]
    sp = jnp.maximum(nl, 0.0) + jnp.log1p(jnp.exp(-jnp.abs(nl)))
    log_a = (-RGLRU_C) * r * sp
    a = jnp.exp(log_a)
    one_minus_a2 = jnp.tanh(-log_a) * (1.0 + a * a)
    mult = jnp.sqrt(one_minus_a2)
    return v, vb, r, i, sp, a, mult


def _lru_specs(s, dh, heads, width):
    head_col = lambda q: pl.BlockSpec((None, s, dh), lambda h, q=q: (q, 0, h))
    vec = pl.BlockSpec((1, dh), lambda h: (0, h))
    mat = pl.BlockSpec((None, dh, dh), lambda h: (h, 0, 0))
    weights = [pl.BlockSpec((width, dh), lambda h: (0, h)), vec, mat, vec, mat, vec, vec]
    return head_col, weights


def _lru_fwd(proj, conv_w, conv_b, w_a, b_a, w_x, b_x, lam, name, deps=()):
    _, s, e = proj.shape
    heads, dh, _ = w_a.shape
    width = conv_w.shape[0]

    def body(v_ref, g_ref, w_ref, cb_ref, wa_ref, ba_ref, wx_ref, bx_ref, lam_ref, yb_ref, keep_ref):
        v, _, r, i, _, a, mult = _lru_gates(v_ref[...].astype(F32), w_ref, cb_ref, wa_ref, ba_ref,
                                           wx_ref, bx_ref, lam_ref, width)
        hs = _scan(a, mult * i * v, _shift_down)
        for k, val in enumerate((hs, v, r, i, a, mult)):
            keep_ref[k] = val
        gv = g_ref[...].astype(F32)
        yb_ref[...] = (hs * (gv * _sigmoid(gv))).astype(yb_ref.dtype)

    head_col, weights = _lru_specs(s, dh, heads, width)
    return pl.pallas_call(
        _after(body, 9, deps), name=name, grid=(heads,),
        in_specs=[head_col(0), head_col(1)] + weights + [ANY] * len(deps),
        out_specs=[pl.BlockSpec((s, dh), lambda h: (0, h)), pl.BlockSpec((6, s, dh), lambda h: (0, 0, h))],
        out_shape=[jax.ShapeDtypeStruct((s, e), BF16), jax.ShapeDtypeStruct((6, s, e), F32)],
        compiler_params=_params("arbitrary"),
    )(proj, proj, conv_w, conv_b, w_a, b_a, w_x, b_x, lam, *deps)


def _lru_bwd(proj, keep, dyb, conv_w, conv_b, w_a, b_a, w_x, b_x, lam, name, deps=()):
    _, s, e = proj.shape
    heads, dh, _ = w_a.shape
    width = conv_w.shape[0]

    def body(v_ref, g_ref, hs_ref, dy_ref, w_ref, cb_ref, wa_ref, ba_ref, wx_ref, bx_ref, lam_ref,
             dp_ref, dwa_ref, dwx_ref, vec_ref):
        v_pre = v_ref[...].astype(F32)
        hs, v, r, i, a, mult = (hs_ref[k] for k in range(6))
        vb = v.astype(BF16)
        nl = -lam_ref[...]
        sp = jnp.maximum(nl, 0.0) + jnp.log1p(jnp.exp(-jnp.abs(nl)))
        gv = g_ref[...].astype(F32)
        dyv = dy_ref[...].astype(F32)
        sg = _sigmoid(gv)
        dp_ref[1] = (dyv * hs * (sg * (1.0 + gv * (1.0 - sg)))).astype(dp_ref.dtype)
        dhs = dyv * (gv * sg)
        d_h = _scan(_shift_up(a, 1), dhs, _shift_up)
        da = d_h * _shift_down(hs, 1)
        iv = i * v
        dlog_a = da * a - (d_h * iv) * (a * a) / mult
        di = d_h * mult * v
        dv = d_h * mult * i
        dzr = dlog_a * (-RGLRU_C) * sp * r * (1.0 - r)
        dzi = di * i * (1.0 - i)
        dsp = jnp.sum(dlog_a * r, axis=0, keepdims=True) * (-RGLRU_C)
        vec_ref[...] = jnp.zeros_like(vec_ref)
        vec_ref[0:1, :] = jnp.sum(dzr, axis=0, keepdims=True)
        vec_ref[1:2, :] = jnp.sum(dzi, axis=0, keepdims=True)
        vec_ref[2:3, :] = -dsp * _sigmoid(-lam_ref[...])
        dzr_b = dzr.astype(BF16)
        dzi_b = dzi.astype(BF16)
        vt = vb.astype(F32).T.astype(BF16)
        dwa_ref[...] = jnp.dot(vt, dzr_b, preferred_element_type=F32).astype(dwa_ref.dtype)
        dwx_ref[...] = jnp.dot(vt, dzi_b, preferred_element_type=F32).astype(dwx_ref.dtype)
        nt = (((1,), (1,)), ((), ()))
        dv = dv + lax.dot_general(dzr_b, wa_ref[...], nt, preferred_element_type=F32)
        dv = dv + lax.dot_general(dzi_b, wx_ref[...], nt, preferred_element_type=F32)
        vec_ref[3:4, :] = jnp.sum(dv, axis=0, keepdims=True)
        dvp = w_ref[width - 1:width, :] * dv
        vec_ref[4 + width - 1:4 + width, :] = jnp.sum(dv * v_pre, axis=0, keepdims=True)
        for k in range(width - 1):
            sh = width - 1 - k
            dvp = dvp + w_ref[k:k + 1, :] * _shift_up(dv, sh)
            vec_ref[4 + k:5 + k, :] = jnp.sum(dv * _shift_down(v_pre, sh), axis=0, keepdims=True)
        dp_ref[0] = dvp.astype(dp_ref.dtype)

    head_col, weights = _lru_specs(s, dh, heads, width)
    col = pl.BlockSpec((s, dh), lambda h: (0, h))
    mat = pl.BlockSpec((None, dh, dh), lambda h: (h, 0, 0))
    return pl.pallas_call(
        _after(body, 11, deps), name=name, grid=(heads,),
        in_specs=[head_col(0), head_col(1), pl.BlockSpec((6, s, dh), lambda h: (0, 0, h)), col] + weights
        + [ANY] * len(deps),
        out_specs=[pl.BlockSpec((2, s, dh), lambda h: (0, 0, h)), mat, mat,
                   pl.BlockSpec((16, dh), lambda h: (0, h))],
        out_shape=[jax.ShapeDtypeStruct((2, s, e), BF16),
                   jax.ShapeDtypeStruct((heads, dh, dh), BF16),
                   jax.ShapeDtypeStruct((heads, dh, dh), BF16),
                   jax.ShapeDtypeStruct((16, e), F32)],
        compiler_params=_params("arbitrary"),
    )(proj, proj, keep, dyb, conv_w, conv_b, w_a, b_a, w_x, b_x, lam, *deps)


def _ada_mod(c_all, w, b, name):
    layers, d, f = w.shape
    nb = c_all.shape[0]

    def body(c_ref, w_ref, b_ref, o_ref):
        cv = c_ref[...]
        sc = cv * _sigmoid(cv)
        o_ref[...] = jnp.dot(sc, w_ref[...], preferred_element_type=F32,
                             precision=lax.Precision.HIGHEST) + b_ref[...]

    return pl.pallas_call(
        body, name=name, grid=(layers,),
        in_specs=[pl.BlockSpec((nb, d), lambda l: (0, 0)),
                  pl.BlockSpec((None, d, f), lambda l: (l, 0, 0)),
                  pl.BlockSpec((None, 1, f), lambda l: (l, 0, 0))],
        out_specs=pl.BlockSpec((None, nb, f), lambda l: (l, 0, 0)),
        out_shape=jax.ShapeDtypeStruct((layers, nb, f), F32),
        compiler_params=_params("arbitrary"),
    )(c_all, w, b)


def _ada_update(c_all_t, dmod, w, m, v, name):
    d, nb = c_all_t.shape
    layers, _, f = dmod.shape
    tr = _tile(d, 512)

    def body(c_ref, dm_ref, w_ref, m_ref, v_ref, g_ref, d_ref, mo_ref, vo_ref):
        cv = c_ref[...]
        sc = cv * _sigmoid(cv)
        g = sc[:, 0:1] * dm_ref[0:1, :]
        for k in range(1, nb):
            g = g + sc[:, k:k + 1] * dm_ref[k:k + 1, :]
        g_ref[...] = g
        d_ref[...], mo_ref[...], vo_ref[...] = _adamw_math(w_ref[...], g, m_ref[...], v_ref[...])

    blk = pl.BlockSpec((None, tr, f), lambda l, i: (l, i, 0))
    return pl.pallas_call(
        body, name=name, grid=(layers, d // tr),
        in_specs=[pl.BlockSpec((tr, nb), lambda l, i: (i, 0)),
                  pl.BlockSpec((None, nb, f), lambda l, i: (l, 0, 0)), blk, blk, blk],
        out_specs=[blk] * 4,
        out_shape=[jax.ShapeDtypeStruct((layers, d, f), F32)] * 4,
        compiler_params=_params("arbitrary", "arbitrary"),
    )(c_all_t, dmod, w, m, v)


def _device_sum(g, name):
    _, rows, _ = g.shape

    def body(g_ref, o_ref):
        acc = g_ref[0]
        for k in range(1, N_DEV):
            acc = acc + g_ref[k]
        o_ref[...] = acc

    return pl.pallas_call(
        body, name=name,
        in_specs=[VMEM_SPEC], out_specs=VMEM_SPEC,
        out_shape=jax.ShapeDtypeStruct((rows, LANES), F32),
        compiler_params=pltpu.CompilerParams(vmem_limit_bytes=VMEM_LIMIT),
    )(g)


def _adamw_math(w, g, m, v):
    m = ADAM_B1 * m + (1.0 - ADAM_B1) * g
    v = ADAM_B2 * v + (1.0 - ADAM_B2) * (g * g)
    m_hat = m / (1.0 - ADAM_B1 ** ADAM_STEP)
    v_hat = v / (1.0 - ADAM_B2 ** ADAM_STEP)
    delta = -ADAM_LR * (m_hat / (jnp.sqrt(v_hat) + ADAM_EPS) + ADAM_WD * w)
    return delta, m, v


def _adamw(w, g, m, v, name):
    rows, cols = w.shape
    tr = _tile(rows, 256)

    def body(w_ref, g_ref, m_ref, v_ref, d_ref, mo_ref, vo_ref):
        d_ref[...], mo_ref[...], vo_ref[...] = _adamw_math(w_ref[...], g_ref[...], m_ref[...], v_ref[...])

    blk = pl.BlockSpec((tr, cols), lambda i: (i, 0))
    return pl.pallas_call(
        body, name=name, grid=(rows // tr,),
        in_specs=[blk] * 4, out_specs=[blk] * 3,
        out_shape=[jax.ShapeDtypeStruct((rows, cols), F32)] * 3,
        compiler_params=_params("arbitrary"),
    )(w, g, m, v)


def _adamw_reduced(idx, w, m, v, part, got, recvs, name):
    rows, cols = w.shape
    tr = _tile(rows, 256)
    nr = len(recvs)

    def body(idx_ref, w_ref, m_ref, v_ref, p_ref, q_ref, *rest):
        g_ref, d_ref, mo_ref, vo_ref = rest[nr:]
        g = p_ref[...].astype(F32) + q_ref[...].astype(F32)
        for u_ref in rest[:nr]:
            for j in range(u_ref.shape[0]):
                g = g + u_ref[j].astype(F32)
        g_ref[...] = g
        d_ref[...], mo_ref[...], vo_ref[...] = _adamw_math(w_ref[...], g, m_ref[...], v_ref[...])

    blk = pl.BlockSpec((tr, cols), lambda i, idx: (i, 0))
    grid_spec = pltpu.PrefetchScalarGridSpec(
        num_scalar_prefetch=1, grid=(rows // tr,),
        in_specs=[blk, blk, blk,
                  pl.BlockSpec((None, None, tr, cols), lambda i, idx: (idx[3], idx[4], i, 0)),
                  pl.BlockSpec((None, None, tr, cols), lambda i, idx: (idx[3], 0, i, 0))]
        + [pl.BlockSpec((u.shape[0], tr, cols), lambda i, idx: (0, i, 0)) for u in recvs],
        out_specs=[blk] * 4)
    return pl.pallas_call(
        body, name=name, grid_spec=grid_spec,
        out_shape=[jax.ShapeDtypeStruct((rows, cols), F32)] * 4,
        compiler_params=_params("arbitrary"),
    )(idx, w, m, v, part, got, *recvs)


def _pack(vectors):
    flat = jnp.concatenate([v.reshape(-1).astype(F32) for v in vectors])
    pad = (-flat.shape[0]) % (8 * LANES)
    return jnp.pad(flat, (0, pad)).reshape(-1, LANES)


def _unpack(flat, shapes):
    out, off = [], 0
    for shp in shapes:
        size = math.prod(shp)
        out.append(flat[..., off:off + size].reshape(flat.shape[:-1] + tuple(shp)))
        off += size
    return out


def _my_slice(full, me, axis):
    size = full.shape[axis] // N_DEV
    return lax.dynamic_slice_in_dim(full, me * size, size, axis)


def kernel(x, c, norm_g, ada_w, ada_b, sc_w_in, sc_conv_w, sc_w_out, lru_w_in, lru_conv_w, lru_conv_b, lru_w_a, lru_b_a, lru_w_x, lru_b_x, lru_lambda, lru_w_out, final_g, loss_target, m_norm_g, m_ada_w, m_ada_b, m_sc_w_in, m_sc_conv_w, m_sc_w_out, m_lru_w_in, m_lru_conv_w, m_lru_conv_b, m_lru_w_a, m_lru_b_a, m_lru_w_x, m_lru_b_x, m_lru_lambda, m_lru_w_out, m_final_g, v_norm_g, v_ada_w, v_ada_b, v_sc_w_in, v_sc_conv_w, v_sc_w_out, v_lru_w_in, v_lru_conv_w, v_lru_conv_b, v_lru_w_a, v_lru_b_a, v_lru_w_x, v_lru_b_x, v_lru_lambda, v_lru_w_out, v_final_g):
    _, s, d = x.shape
    e = sc_w_out.shape[1] * N_DEV
    heads, dh_s, dh = lru_w_a.shape[1:]
    es = e // N_DEV
    f = ada_w.shape[2]
    mx, my, mc = _position()
    me = 4 * mx + 2 * my + mc
    chip = 2 * mx + my
    idx = jnp.stack([chip ^ 1, chip ^ 2, chip ^ 3, chip, mc]).astype(jnp.int32)

    x0 = x[0]
    target = loss_target[0]

    small_shapes = [(d,), (3, es), (4, es), (es,), (heads, dh_s), (heads, dh_s), (es,)]
    small = _small_gather(_pack([c, sc_conv_w, lru_conv_w, lru_conv_b, lru_b_a, lru_b_x, lru_lambda]),
                          "gather_small_weights").reshape(N_DEV, -1)
    c_all, cw3, cw4, cb, ba, bx, lam = _unpack(small, small_shapes)
    cw3 = cw3.transpose(1, 0, 2).reshape(3, e)
    cw4 = cw4.transpose(1, 0, 2).reshape(4, e)
    cb = cb.reshape(1, e)
    lam = lam.reshape(1, e)
    ba = ba.transpose(1, 0, 2).reshape(1, e)
    bx = bx.transpose(1, 0, 2).reshape(1, e)

    shards = [sc_w_in[0].astype(BF16), sc_w_out[0].astype(BF16), lru_w_in[0].astype(BF16),
              lru_w_a[0].reshape(heads * dh_s, dh).astype(BF16),
              lru_w_x[0].reshape(heads * dh_s, dh).astype(BF16), lru_w_out[0].astype(BF16)]
    lands = [lax.dynamic_update_slice(lax.empty((N_DEV,) + sh.shape, BF16), sh[None], (me, 0, 0))
             for sh in shards]
    every = [1, 2, 3, 0]
    units = [([0], [0]), ([0], [1]), ([0], [2]), ([0], [3]), ([1], every), ([2], every), ([3, 4], every),
             ([5], every)]
    sems, first_sh, first_ld, started = _gather_start(shards[:1], lands[:1], units[:3], [small],
                                                      "gather_start_first")
    shards, lands = first_sh + shards[1:], first_ld + lands[1:]

    ada_b_mine = _my_slice(ada_b, me, 1).reshape(2, 1, f)
    mod_mine = _ada_mod(c_all, ada_w, ada_b_mine, "ada_mod")
    mod_all = _small_gather(_pack([mod_mine]), "gather_mod", deps=[started])

    def start_later(after):
        far_sems, far_sh, far_ld, tok = _gather_start(shards[:1], lands[:1], units[3:4], after, "gather_start_far")
        rest_units = [([i - 1 for i in members], ks) for members, ks in units[4:]]
        rest_sems, rest_sh, rest_ld, tok = _gather_start(shards[1:], lands[1:], rest_units, [tok],
                                                         "gather_start_rest")
        sems.extend(far_sems + rest_sems)
        shards[:], lands[:] = far_sh + rest_sh, far_ld + rest_ld
        return tok

    def gathered(u, after_forward, name):
        members, ks = units[u]
        fwd, shs, lnd, token = _gather_forward(
            [shards[i] for i in members], [lands[i] for i in members], ks, sems[u][0], sems[u][1],
            after_forward, "gather_forward_" + name)
        for i, sh, ld in zip(members, shs, lnd):
            shards[i], lands[i] = sh, ld

        def finish(after):
            out = _gather_finish([lands[i] for i in members], ks, fwd, after, "gather_finish_" + name)
            for i, ld in zip(members, out):
                lands[i] = ld
            return out

        return token, finish

    tok, finish_y = gathered(1, [mod_all], "sc_w_in_near_y")
    tok, finish_x = gathered(2, [tok], "sc_w_in_near_x")
    queued = start_later([tok])

    mod_all = mod_all.reshape(N_DEV, -1)
    mod_all = mod_all[:, :2 * N_DEV * f].reshape(N_DEV, 2, N_DEV, f)
    mod_all = mod_all.transpose(1, 2, 0, 3).reshape(2, N_DEV, 3 * d)
    mod = lax.dynamic_index_in_dim(mod_all, me, 1, keepdims=False)
    shift = [mod[l:l + 1, 0:d] for l in range(2)]
    scale = [mod[l:l + 1, d:2 * d] for l in range(2)]
    gate = [mod[l:l + 1, 2 * d:3 * d] for l in range(2)]
    ng = [norm_g[l:l + 1] for l in range(2)]
    fg = final_g.reshape(1, d)

    h0 = _norm_mod(x0, ng[0], scale[0], shift[0], "norm_mod_0", deps=[queued])
    proj0 = lax.empty((4, s, e), BF16)
    tok, _ = gathered(0, [h0], "sc_w_in_own")
    proj0 = _mm_proj_group(h0, lands[0], idx, 3, proj0, "mm_proj_0_own", deps=[tok])
    for u, name, finish in ((1, "near_y", finish_y), (2, "near_x", finish_x), (3, "far", None)):
        after = [proj0]
        if finish is None:
            tok, finish = gathered(u, [proj0], "sc_w_in_" + name)
            after = [tok]
        wg_in0, = finish(after)
        proj0 = _mm_proj_group(h0, wg_in0, idx, u - 1, proj0, "mm_proj_0_" + name)
    tok, finish = gathered(4, [proj0], "sc_w_out")
    yb0 = _sc_fwd(proj0, cw3, "sc_fwd", deps=[tok])
    w_out0 = finish([yb0])[0].reshape(e, d)
    x1, y0 = _mm_out(yb0, w_out0, x0, gate[0], "mm_out_0")
    tok, finish = gathered(5, [x1], "lru_w_in")
    h1 = _norm_mod(x1, ng[1], scale[1], shift[1], "norm_mod_1", deps=[tok])
    wg_in1, = finish([h1])
    proj1 = _mm_proj(h1, wg_in1, 2, "mm_proj_1")
    tok, finish = gathered(6, [proj1], "lru_gates")
    wg_a, wg_x = finish([tok])
    w_a = wg_a.reshape(N_DEV, heads, dh_s, dh).transpose(1, 0, 2, 3).reshape(heads, dh, dh)
    w_x = wg_x.reshape(N_DEV, heads, dh_s, dh).transpose(1, 0, 2, 3).reshape(heads, dh, dh)
    tok, finish = gathered(7, [w_a, w_x], "lru_w_out")
    yb1, hs = _lru_fwd(proj1, cw4, cb, w_a, ba, w_x, bx, lam, "lru_fwd", deps=[tok])
    w_out1 = finish([yb1])[0].reshape(e, d)
    x2, y1 = _mm_out(yb1, w_out1, x1, gate[1], "mm_out_1")
    dx2, loss_part, d_fg, dy1, dgate1 = _final_loss(x2, fg, target, y1, gate[1], "final_loss")

    def pieces(g, rows, cols):
        return g.reshape(4, 2, rows, cols)

    def by_rows(g):
        return g.reshape(heads, N_DEV, dh_s, dh).transpose(1, 0, 2, 3).reshape(N_DEV, heads * dh_s, dh)

    def pair_begin(parts, group):
        send, recv, parts, lnd, token = _pair_start(parts, "pair_start_" + group)
        return dict(parts=parts, lands=lnd, send=send, recv=recv, group=group), token

    def scatter_start(pair, names, after):
        group = pair["group"]
        parts, gots = _pair_wait(pair["parts"], pair["lands"], pair["send"], pair["recv"], after,
                                 "pair_wait_" + group)
        sums = [_pair_sum(idx, p, q, "pair_sum_" + nm) for p, q, nm in zip(parts, gots, names)]
        empties = [lax.empty(sm.shape, sm.dtype) for sm in sums]
        send, recv, sums, lnd, token = _chip_start(sums, empties, "chip_start_" + group)
        return dict(parts=parts, gots=gots, names=names, group=group, sums=sums, lands=lnd,
                    send=send, recv=recv), token

    big = {"sc_w_in": (sc_w_in, m_sc_w_in, v_sc_w_in), "sc_w_out": (sc_w_out, m_sc_w_out, v_sc_w_out),
           "lru_w_in": (lru_w_in, m_lru_w_in, v_lru_w_in), "lru_w_a": (lru_w_a, m_lru_w_a, v_lru_w_a),
           "lru_w_x": (lru_w_x, m_lru_w_x, v_lru_w_x), "lru_w_out": (lru_w_out, m_lru_w_out, v_lru_w_out)}
    big_res = {}

    def scatter_finish(rs, after):
        recvs = _chip_wait(rs["sums"], rs["lands"], rs["send"], rs["recv"], after, "chip_wait_" + rs["group"])
        done = []
        for p, q, u, nm in zip(rs["parts"], rs["gots"], recvs, rs["names"]):
            w, m, v = big[nm]
            shp2 = p.shape[2:]
            res = _adamw_reduced(idx, w.reshape(shp2), m.reshape(shp2), v.reshape(shp2), p, q, [u], "adamw_" + nm)
            big_res[nm] = [r.reshape(w.shape) for r in res]
            done.append(res[1])
        return done

    dw_out1 = _mm_tn(yb1, dy1[None], 1, "mm_dw_out_1")
    pair, tok = pair_begin([pieces(dw_out1, es, d)], "lru_w_out")
    dyb1 = _mm_nt(dy1[None], w_out1[None], BF16, "mm_dyb_1", deps=[tok])
    rs1, tok = scatter_start(pair, ["lru_w_out"], [dyb1])
    dproj1, dw_a, dw_x, vecs1 = _lru_bwd(proj1, hs, dyb1, cw4, cb, w_a, ba, w_x, bx, lam, "lru_bwd", deps=[tok])
    done = scatter_finish(rs1, [dproj1])
    dw_in1 = _mm_tn(h1, dproj1, N_DEV, "mm_dw_in_1", deps=done)
    pair, tok = pair_begin([pieces(dw_in1, d, 2 * es), pieces(by_rows(dw_a), heads * dh_s, dh),
                            pieces(by_rows(dw_x), heads * dh_s, dh)], "lru_in")
    dh1 = _mm_nt(dproj1, wg_in1, BF16, "mm_dh_1", deps=[tok])
    rs2, tok = scatter_start(pair, ["lru_w_in", "lru_w_a", "lru_w_x"], [dh1])
    dx1, dscale1, dshift1, dng1, dy0, dgate0 = _norm_mod_bwd(x1, dh1, dx2, ng[1], scale[1], "norm_mod_bwd_1",
                                                             below=(y0, gate[0]), deps=[tok])
    dw_out0 = _mm_tn(yb0, dy0[None], 1, "mm_dw_out_0")
    pair, tok = pair_begin([pieces(dw_out0, es, d)], "sc_w_out")
    dyb0 = _mm_nt(dy0[None], w_out0[None], BF16, "mm_dyb_0", deps=[tok])
    rs3, tok = scatter_start(pair, ["sc_w_out"], [dyb0])
    dproj0, vecs0 = _sc_bwd(proj0, dyb0, cw3, "sc_bwd", deps=[tok])
    idx_one = jnp.stack([jnp.zeros_like(mc)] * 4 + [mc]).astype(jnp.int32)
    sc_w_in_steps = []

    def chip_step(j, pair, after):
        (part,), (got,) = _pair_wait(pair["parts"], pair["lands"], pair["send"], pair["recv"], after,
                                     "pair_wait_sc_w_in_%d" % j)
        sm = _pair_sum(idx_one, part, got, "pair_sum_sc_w_in_%d" % j, nslots=1)
        send, recv, sums, lnd, token = _chip_start([sm], [lax.empty(sm.shape, sm.dtype)],
                                                   "chip_start_sc_w_in_%d" % j, flips=(j,))
        sc_w_in_steps.append((sums, lnd, send, recv, j))
        return token

    pending, done = None, []
    for j in (3, 2, 1, 0):
        part = _mm_tn_group(h0, dproj0, idx, (j - 1) % 4, 2, "mm_dw_in_0_%d" % j, deps=done)[None]
        pair, tok = pair_begin([part], "sc_w_in_%d" % j)
        if j == 3:
            done = [chip_step(j, pair, [tok])]
            continue
        done = [tok]
        if pending is not None:
            done.append(chip_step(pending[0], pending[1], [tok]))
        pending = (j, pair)
    done += scatter_finish(rs2, done)
    dh0 = _mm_nt(dproj0, wg_in0, BF16, "mm_dh_0", deps=done)
    pair = pending[1]
    (part,), (got,) = _pair_wait(pair["parts"], pair["lands"], pair["send"], pair["recv"], [dh0],
                                 "pair_wait_sc_w_in_0")
    dx0, dscale0, dshift0, dng0 = _norm_mod_bwd(x0, dh0, dx1, ng[0], scale[0], "norm_mod_bwd_0")
    done = scatter_finish(rs3, [dx0])
    dmod_mine = jnp.concatenate([dshift0, dscale0, dgate0, dshift1, dscale1, dgate1], axis=1)
    end_shapes = [(LANES,), (2, 3 * d), (2, d), (d,), (8, e), (16, e)]
    end_all = _small_gather(
        _pack([loss_part, dmod_mine, jnp.concatenate([dng0, dng1], axis=0), d_fg, vecs0, vecs1]),
        "gather_small_grads", deps=done)
    end_sum = _device_sum(end_all, "sum_small_grads").reshape(-1)
    loss_v, g_ada_b, g_norm_g, g_final_g, sum0, sum1 = _unpack(end_sum, end_shapes)
    loss = loss_v[0]
    dmod_all = _unpack(end_all.reshape(N_DEV, -1), end_shapes)[1].transpose(1, 0, 2)
    dmod_cols = _my_slice(dmod_all, me, 2)
    ada_out = _ada_update(c_all.T, dmod_cols, ada_w, m_ada_w, v_ada_w, "ada_update")

    g_sc_conv_w = _my_slice(sum0[0:3], me, 1)
    g_lru_b_a = _my_slice(sum1[0].reshape(heads, dh), me, 1)
    g_lru_b_x = _my_slice(sum1[1].reshape(heads, dh), me, 1)
    g_lru_lambda = _my_slice(sum1[2:3], me, 1)
    g_lru_conv_b = _my_slice(sum1[3:4], me, 1)
    g_lru_conv_w = _my_slice(sum1[4:8], me, 1)

    small_w = [norm_g, ada_b, final_g, sc_conv_w, lru_conv_w, lru_conv_b, lru_b_a, lru_b_x, lru_lambda]
    small_m = [m_norm_g, m_ada_b, m_final_g, m_sc_conv_w, m_lru_conv_w, m_lru_conv_b, m_lru_b_a, m_lru_b_x,
               m_lru_lambda]
    small_v = [v_norm_g, v_ada_b, v_final_g, v_sc_conv_w, v_lru_conv_w, v_lru_conv_b, v_lru_b_a, v_lru_b_x,
               v_lru_lambda]
    small_g = [g_norm_g, g_ada_b, g_final_g, g_sc_conv_w, g_lru_conv_w, g_lru_conv_b, g_lru_b_a, g_lru_b_x,
               g_lru_lambda]
    small_g = [g.reshape(w.shape) for g, w in zip(small_g, small_w)]
    shapes = [w.shape for w in small_w]
    packed = _adamw(_pack(small_w), _pack(small_g), _pack(small_m), _pack(small_v), "adamw_small")
    small_out = [small_g] + [_unpack(p.reshape(-1), shapes) for p in packed]

    after = [packed[0], ada_out[1]]
    recvs = []
    for sums, lnd, send, recv, j in sc_w_in_steps:
        recvs += _chip_wait(sums, lnd, send, recv, after, "chip_wait_sc_w_in_%d" % j)
    shp2 = part.shape[2:]
    res = _adamw_reduced(idx_one, sc_w_in.reshape(shp2), m_sc_w_in.reshape(shp2), v_sc_w_in.reshape(shp2),
                         part, got, recvs, "adamw_sc_w_in")
    big_res["sc_w_in"] = [r.reshape(sc_w_in.shape) for r in res]
    big_out = [big_res[nm] for nm in ("sc_w_in", "sc_w_out", "lru_w_in", "lru_w_a", "lru_w_x", "lru_w_out")]

    def small(kind, i):
        return small_out[kind][i]

    def bigw(kind, i):
        return big_out[i][kind]

    outs = [loss, dx0[None]]
    for kind in range(4):
        outs += [small(kind, 0), ada_out[kind], small(kind, 1), bigw(kind, 0), small(kind, 3), bigw(kind, 1),
                 bigw(kind, 2), small(kind, 4), small(kind, 5), bigw(kind, 3), small(kind, 6), bigw(kind, 4),
                 small(kind, 7), small(kind, 8), bigw(kind, 5), small(kind, 2)]
    return tuple(outs)
```

```python
import math

import jax
import jax.numpy as jnp
from jax import lax
from jax.experimental import pallas as pl
from jax.experimental.pallas import tpu as pltpu

N_DEV = 8
LANES = 128
EPS = 1e-6
RGLRU_C = 8.0
ADAM_LR = 0.001
ADAM_B1 = 0.9
ADAM_B2 = 0.999
ADAM_EPS = 1e-08
ADAM_WD = 0.01
ADAM_STEP = 10
VMEM_LIMIT = 56 * 1024 * 1024
MESH = pl.DeviceIdType.MESH
F32 = jnp.float32
BF16 = jnp.bfloat16
ANY = pl.BlockSpec(memory_space=pl.ANY)
HBM = pl.BlockSpec(memory_space=pltpu.HBM)
SEM = pl.BlockSpec(memory_space=pltpu.SEMAPHORE)
VMEM_SPEC = pl.BlockSpec(memory_space=pltpu.VMEM)
EFFECT = pltpu.SideEffectType.DATAFLOW_SIDE_EFFECTING
TOKEN = jax.ShapeDtypeStruct((8, LANES), jnp.float32)


def _tile(n, pref):
    t = min(n, pref)
    assert n % t == 0, (n, pref)
    return t


def _params(*sem):
    return pltpu.CompilerParams(dimension_semantics=sem, vmem_limit_bytes=VMEM_LIMIT)


def _position():
    return lax.axis_index("x"), lax.axis_index("y"), lax.axis_index("c")


def _flip(x, y, k):
    return (1 - x if k & 2 else x), (1 - y if k & 1 else y)


def _after(body, n_in, deps):
    if not deps:
        return body

    def wrapped(*refs):
        return body(*refs[:n_in], *refs[n_in + len(deps):])

    return wrapped


def _small_gather(v, name, deps=()):
    rows = v.shape[0]

    def body(v_ref, out_ref, send_sems, recv_sems):
        x, y, c = _position()
        me = 4 * x + 2 * y + c
        out_ref[me] = v_ref[...]
        copies = []
        for k in range(1, N_DEV):
            px, py = _flip(x, y, k >> 1)
            pc = 1 - c if k & 1 else c
            cp = pltpu.make_async_remote_copy(
                src_ref=v_ref, dst_ref=out_ref.at[me],
                send_sem=send_sems.at[k - 1], recv_sem=recv_sems.at[k - 1],
                device_id=(px, py, pc), device_id_type=MESH)
            cp.start()
            copies.append((cp, 4 * px + 2 * py + pc))
        for k, (cp, peer) in enumerate(copies):
            pltpu.make_async_remote_copy(
                src_ref=v_ref, dst_ref=out_ref.at[peer],
                send_sem=send_sems.at[k], recv_sem=recv_sems.at[k],
                device_id=(x, y, c), device_id_type=MESH).wait_recv()
        for cp, _ in copies:
            cp.wait_send()

    return pl.pallas_call(
        _after(body, 1, deps), name=name,
        out_shape=jax.ShapeDtypeStruct((N_DEV, rows, LANES), F32),
        in_specs=[VMEM_SPEC] + [ANY] * len(deps), out_specs=VMEM_SPEC,
        scratch_shapes=[pltpu.SemaphoreType.DMA((N_DEV - 1,)),
                        pltpu.SemaphoreType.DMA((N_DEV - 1,))],
        compiler_params=pltpu.CompilerParams(vmem_limit_bytes=VMEM_LIMIT),
    )(v, *deps)


def _hbm(a):
    return pltpu.with_memory_space_constraint(a, pltpu.HBM)


def _hbm_like(arrays):
    return [pltpu.HBM(a.shape, a.dtype) for a in arrays]


def _remote(src, dst, send, recv, to):
    return pltpu.make_async_remote_copy(src_ref=src, dst_ref=dst, send_sem=send, recv_sem=recv,
                                        device_id=to, device_id_type=MESH)


def _gather_start(shards, lands, units, after, name):
    n, nu = len(shards), len(units)

    def body(*refs):
        ins, lnd = refs[:n], refs[n:2 * n]
        sems = refs[2 * n + len(after):2 * n + len(after) + 2 * nu]
        token = refs[-1]
        x, y, c = _position()
        me = 4 * x + 2 * y + c
        targets = [(x, y, 1 - c)] + [(px, py, c) for px, py in (_flip(x, y, k) for k in (1, 2, 3))]
        for u, (members, ks) in enumerate(units):
            for slot, i in enumerate(members):
                for ki, k in enumerate(ks):
                    at = len(ks) * slot + ki
                    _remote(ins[i], lnd[i].at[me], sems[2 * u].at[at], sems[2 * u + 1].at[at], targets[k]).start()
        token[...] = jnp.zeros_like(token)

    sem_shapes = []
    for members, ks in units:
        count = len(members) * len(ks)
        sem_shapes += [pltpu.SemaphoreType.DMA((count,)), pltpu.SemaphoreType.DMA((count,))]
    out = pl.pallas_call(
        body, name=name,
        out_shape=sem_shapes + _hbm_like(shards) + _hbm_like(lands) + [TOKEN],
        in_specs=[HBM] * (2 * n) + [ANY] * len(after),
        out_specs=[SEM] * (2 * nu) + [HBM] * (2 * n) + [VMEM_SPEC],
        input_output_aliases={i: 2 * nu + i for i in range(2 * n)},
        compiler_params=pltpu.CompilerParams(has_side_effects=EFFECT),
    )(*[_hbm(s) for s in shards], *[_hbm(l) for l in lands], *after)
    sems = [(out[2 * u], out[2 * u + 1]) for u in range(nu)]
    return sems, list(out[2 * nu:2 * nu + n]), list(out[2 * nu + n:2 * nu + 2 * n]), out[-1]


def _gather_forward(shards, lands, ks, send, recv, after, name):
    m = len(shards)
    hops = [k for k in ks if k]
    nsem = 2 if hops else 0

    def body(*refs):
        ins, lnd = refs[:m], refs[m:2 * m]
        send_ref, recv_ref = refs[2 * m], refs[2 * m + 1]
        outs = refs[2 * m + 2 + len(after):]
        token = refs[-1]
        x, y, c = _position()
        me = (x, y, c)
        for slot in range(m):
            for ki, k in enumerate(ks):
                at = len(ks) * slot + ki
                if k:
                    px, py = _flip(x, y, k)
                    block = lnd[slot].at[4 * px + 2 * py + c]
                else:
                    block = lnd[slot].at[4 * x + 2 * y + (1 - c)]
                arrival = _remote(ins[slot], block, send_ref.at[at], recv_ref.at[at], me)
                arrival.wait_recv()
                if k:
                    fat = len(hops) * slot + hops.index(k)
                    _remote(block, block, outs[0].at[fat], outs[1].at[fat], (x, y, 1 - c)).start()
                arrival.wait_send()
        token[...] = jnp.zeros_like(token)

    count = len(hops) * m
    sem_shapes = [pltpu.SemaphoreType.DMA((count,)), pltpu.SemaphoreType.DMA((count,))] if hops else []
    out = pl.pallas_call(
        body, name=name,
        out_shape=sem_shapes + _hbm_like(shards) + _hbm_like(lands) + [TOKEN],
        in_specs=[HBM] * (2 * m) + [SEM, SEM] + [ANY] * len(after),
        out_specs=[SEM] * nsem + [HBM] * (2 * m) + [VMEM_SPEC],
        input_output_aliases={i: nsem + i for i in range(2 * m)},
        compiler_params=pltpu.CompilerParams(has_side_effects=EFFECT),
    )(*shards, *lands, send, recv, *after)
    fwd = (out[0], out[1]) if hops else None
    return fwd, list(out[nsem:nsem + m]), list(out[nsem + m:nsem + 2 * m]), out[-1]


def _gather_finish(lands, ks, fwd, after, name):
    m = len(lands)
    hops = [k for k in ks if k]

    def body(*refs):
        lnd = refs[:m]
        fsend_ref, frecv_ref = refs[m], refs[m + 1]
        x, y, c = _position()
        for slot in range(m):
            for fi, k in enumerate(hops):
                px, py = _flip(x, y, k)
                sent = lnd[slot].at[4 * px + 2 * py + c]
                came = lnd[slot].at[4 * px + 2 * py + (1 - c)]
                fat = len(hops) * slot + fi
                cp = _remote(sent, came, fsend_ref.at[fat], frecv_ref.at[fat], (x, y, c))
                cp.wait_recv()
                cp.wait_send()

    out = pl.pallas_call(
        body, name=name,
        out_shape=_hbm_like(lands),
        in_specs=[HBM] * m + [SEM, SEM] + [ANY] * len(after), out_specs=[HBM] * m,
        input_output_aliases={i: i for i in range(m)},
        compiler_params=pltpu.CompilerParams(has_side_effects=EFFECT),
    )(*lands, fwd[0], fwd[1], *after)
    return list(out)


def _pair_start(parts, name):
    n = len(parts)
    lands = [lax.empty((p.shape[0], 1) + p.shape[2:], p.dtype) for p in parts]

    def body(*refs):
        ins, lnd = refs[:n], refs[n:2 * n]
        send_ref, recv_ref = refs[2 * n], refs[2 * n + 1]
        token = refs[-1]
        x, y, c = _position()
        for i in range(n):
            _remote(ins[i].at[:, pl.ds(1 - c, 1)], lnd[i], send_ref.at[i], recv_ref.at[i], (x, y, 1 - c)).start()
        token[...] = jnp.zeros_like(token)

    out = pl.pallas_call(
        body, name=name,
        out_shape=[pltpu.SemaphoreType.DMA((n,)), pltpu.SemaphoreType.DMA((n,))]
        + _hbm_like(parts) + _hbm_like(lands) + [TOKEN],
        in_specs=[HBM] * (2 * n), out_specs=[SEM, SEM] + [HBM] * (2 * n) + [VMEM_SPEC],
        input_output_aliases={i: 2 + i for i in range(2 * n)},
        compiler_params=pltpu.CompilerParams(has_side_effects=EFFECT),
    )(*[_hbm(p) for p in parts], *[_hbm(l) for l in lands])
    return out[0], out[1], list(out[2:2 + n]), list(out[2 + n:2 + 2 * n]), out[-1]


def _pair_wait(parts, lands, send, recv, after, name):
    n = len(parts)

    def body(*refs):
        ins, lnd = refs[:n], refs[n:2 * n]
        send_ref, recv_ref = refs[2 * n], refs[2 * n + 1]
        x, y, c = _position()
        for i in range(n):
            cp = _remote(ins[i].at[:, pl.ds(1 - c, 1)], lnd[i], send_ref.at[i], recv_ref.at[i], (x, y, c))
            cp.wait_recv()
            cp.wait_send()

    out = pl.pallas_call(
        body, name=name,
        out_shape=_hbm_like(parts) + _hbm_like(lands),
        in_specs=[HBM] * (2 * n) + [SEM, SEM] + [ANY] * len(after), out_specs=[HBM] * (2 * n),
        input_output_aliases={i: i for i in range(2 * n)},
        compiler_params=pltpu.CompilerParams(has_side_effects=EFFECT),
    )(*parts, *lands, send, recv, *after)
    return list(out[:n]), list(out[n:])


def _chip_start(sums, lands, name, flips=(1, 2, 3)):
    n, ns = len(sums), len(flips)

    def body(*refs):
        ins, lnd = refs[:n], refs[n:2 * n]
        send_ref, recv_ref = refs[2 * n], refs[2 * n + 1]
        token = refs[-1]
        x, y, c = _position()
        for i in range(n):
            for j, flip in enumerate(flips):
                px, py = _flip(x, y, flip)
                _remote(ins[i].at[j], lnd[i].at[j], send_ref.at[ns * i + j], recv_ref.at[ns * i + j],
                        (px, py, c)).start()
        token[...] = jnp.zeros_like(token)

    out = pl.pallas_call(
        body, name=name,
        out_shape=[pltpu.SemaphoreType.DMA((ns * n,)), pltpu.SemaphoreType.DMA((ns * n,))]
        + _hbm_like(sums) + _hbm_like(lands) + [TOKEN],
        in_specs=[HBM] * (2 * n), out_specs=[SEM, SEM] + [HBM] * (2 * n) + [VMEM_SPEC],
        input_output_aliases={i: 2 + i for i in range(2 * n)},
        compiler_params=pltpu.CompilerParams(has_side_effects=EFFECT),
    )(*[_hbm(s) for s in sums], *[_hbm(l) for l in lands])
    return out[0], out[1], out[2:2 + n], out[2 + n:2 + 2 * n], out[-1]


def _chip_wait(sums, lands, send, recv, after, name):
    n, ns = len(sums), sums[0].shape[0]

    def body(*refs):
        ins, lnd = refs[:n], refs[n:2 * n]
        send_ref, recv_ref = refs[2 * n], refs[2 * n + 1]
        x, y, c = _position()
        for i in range(n):
            for j in range(ns):
                cp = _remote(ins[i].at[j], lnd[i].at[j], send_ref.at[ns * i + j], recv_ref.at[ns * i + j], (x, y, c))
                cp.wait_recv()
                cp.wait_send()

    out = pl.pallas_call(
        body, name=name,
        out_shape=_hbm_like(sums) + _hbm_like(lands),
        in_specs=[HBM] * (2 * n) + [SEM, SEM] + [ANY] * len(after), out_specs=[HBM] * (2 * n),
        input_output_aliases={i: i for i in range(2 * n)},
        compiler_params=pltpu.CompilerParams(has_side_effects=EFFECT),
    )(*sums, *lands, send, recv, *after)
    return list(out[n:])


def _pair_sum(idx, part, got, name, nslots=3):
    _, _, rows, cols = part.shape
    tr = _tile(rows, 1024)

    def body(idx_ref, p_ref, q_ref, o_ref):
        o_ref[...] = (p_ref[...].astype(F32) + q_ref[...].astype(F32)).astype(o_ref.dtype)

    grid_spec = pltpu.PrefetchScalarGridSpec(
        num_scalar_prefetch=1, grid=(nslots, rows // tr),
        in_specs=[pl.BlockSpec((None, None, tr, cols), lambda j, r, idx: (idx[j], idx[4], r, 0)),
                  pl.BlockSpec((None, None, tr, cols), lambda j, r, idx: (idx[j], 0, r, 0))],
        out_specs=pl.BlockSpec((None, tr, cols), lambda j, r, idx: (j, r, 0)))
    return pl.pallas_call(
        body, name=name, grid_spec=grid_spec,
        out_shape=jax.ShapeDtypeStruct((nslots, rows, cols), part.dtype),
        compiler_params=_params("arbitrary", "arbitrary"),
    )(idx, part, got)


def _mm_proj(h, wg, groups, name):
    s, k = h.shape
    nchunk, _, n = wg.shape
    e = nchunk * n // groups
    tn = _tile(min(n, e), 512)

    def body(h_ref, w_ref, o_ref):
        o_ref[...] = jnp.dot(h_ref[...], w_ref[...], preferred_element_type=F32).astype(o_ref.dtype)

    return pl.pallas_call(
        body, name=name, grid=(nchunk * n // tn,),
        in_specs=[pl.BlockSpec((s, k), lambda j: (0, 0)),
                  pl.BlockSpec((None, k, tn), lambda j: ((j * tn) // n, 0, ((j * tn) % n) // tn))],
        out_specs=pl.BlockSpec((None, s, tn), lambda j: ((j * tn) // e, 0, ((j * tn) % e) // tn)),
        out_shape=jax.ShapeDtypeStruct((groups, s, e), BF16),
        compiler_params=_params("arbitrary"),
    )(h, wg)


def _mm_proj_group(h, wg, idx, pos, prev, name, deps=()):
    s, k = h.shape
    _, _, n = wg.shape
    _, _, e = prev.shape
    tn = _tile(n, 512)
    nd = len(deps)

    def body(idx_ref, h_ref, w_ref, prev_ref, *rest):
        o_ref = rest[nd]
        o_ref[...] = jnp.dot(h_ref[...], w_ref[...], preferred_element_type=F32).astype(o_ref.dtype)

    def col(j, idx):
        return idx[pos] * (2 * n) + j * tn

    grid_spec = pltpu.PrefetchScalarGridSpec(
        num_scalar_prefetch=1, grid=(2 * n // tn,),
        in_specs=[pl.BlockSpec((s, k), lambda j, idx: (0, 0)),
                  pl.BlockSpec((None, k, tn), lambda j, idx: (col(j, idx) // n, 0, (col(j, idx) % n) // tn)),
                  ANY] + [ANY] * nd,
        out_specs=pl.BlockSpec((None, s, tn), lambda j, idx: (col(j, idx) // e, 0, (col(j, idx) % e) // tn)))
    return pl.pallas_call(
        body, name=name, grid_spec=grid_spec,
        out_shape=jax.ShapeDtypeStruct(prev.shape, prev.dtype),
        input_output_aliases={3: 0},
        compiler_params=_params("arbitrary"),
    )(idx, h, wg, prev, *deps)


def _mm_out(yb, w, x, gate, name):
    s, k = yb.shape
    d = w.shape[1]
    tn = _tile(d, 512)
    tk = _tile(k, 2048)
    nk = k // tk

    def body(a_ref, w_ref, x_ref, g_ref, xo_ref, y_ref, acc_ref):
        kk = pl.program_id(1)

        @pl.when(kk == 0)
        def _():
            acc_ref[...] = jnp.zeros_like(acc_ref)

        acc_ref[...] += jnp.dot(a_ref[...], w_ref[...], preferred_element_type=F32)

        @pl.when(kk == nk - 1)
        def _():
            y = acc_ref[...]
            y_ref[...] = y.astype(y_ref.dtype)
            xo_ref[...] = x_ref[...] + g_ref[...] * y

    return pl.pallas_call(
        body, name=name, grid=(d // tn, nk),
        in_specs=[pl.BlockSpec((s, tk), lambda j, kk: (0, kk)),
                  pl.BlockSpec((tk, tn), lambda j, kk: (kk, j)),
                  pl.BlockSpec((s, tn), lambda j, kk: (0, j)),
                  pl.BlockSpec((1, tn), lambda j, kk: (0, j))],
        out_specs=[pl.BlockSpec((s, tn), lambda j, kk: (0, j)),
                   pl.BlockSpec((s, tn), lambda j, kk: (0, j))],
        out_shape=[jax.ShapeDtypeStruct((s, d), F32), jax.ShapeDtypeStruct((s, d), BF16)],
        scratch_shapes=[pltpu.VMEM((s, tn), F32)],
        compiler_params=_params("arbitrary", "arbitrary"),
    )(yb, w, x, gate)


def _mm_nt(a3, w3, out_dtype, name, deps=()):
    g, s, ea = a3.shape
    cw, n, nw = w3.shape
    total = g * ea
    assert total == cw * nw
    tk = _tile(min(ea, nw), 2048)
    tn = _tile(n, 1024)
    nk = total // tk

    def body(a_ref, w_ref, o_ref, acc_ref):
        kk = pl.program_id(1)

        @pl.when(kk == 0)
        def _():
            acc_ref[...] = jnp.zeros_like(acc_ref)

        acc_ref[...] += lax.dot_general(a_ref[...], w_ref[...], (((1,), (1,)), ((), ())),
                                        preferred_element_type=F32)

        @pl.when(kk == nk - 1)
        def _():
            o_ref[...] = acc_ref[...].astype(o_ref.dtype)

    return pl.pallas_call(
        _after(body, 2, deps), name=name, grid=(n // tn, nk),
        in_specs=[pl.BlockSpec((None, s, tk), lambda j, kk: ((kk * tk) // ea, 0, ((kk * tk) % ea) // tk)),
                  pl.BlockSpec((None, tn, tk), lambda j, kk: ((kk * tk) // nw, j, ((kk * tk) % nw) // tk))]
        + [ANY] * len(deps),
        out_specs=pl.BlockSpec((s, tn), lambda j, kk: (0, j)),
        out_shape=jax.ShapeDtypeStruct((s, n), out_dtype),
        scratch_shapes=[pltpu.VMEM((s, tn), F32)],
        compiler_params=_params("arbitrary", "arbitrary"),
    )(a3, w3, *deps)


def _mm_tn(a, b3, nchunk, name, deps=()):
    s, ka = a.shape
    g, _, eb = b3.shape
    n = g * eb // nchunk
    tm = _tile(ka, 1024)
    tn = _tile(min(n, eb), 1024)

    def body(a_ref, b_ref, o_ref, at_ref):
        @pl.when(pl.program_id(1) == 0)
        def _():
            at_ref[...] = a_ref[...].astype(F32).T.astype(at_ref.dtype)

        o_ref[...] = jnp.dot(at_ref[...], b_ref[...], preferred_element_type=F32).astype(o_ref.dtype)

    return pl.pallas_call(
        _after(body, 2, deps), name=name, grid=(ka // tm, g * eb // tn),
        in_specs=[pl.BlockSpec((s, tm), lambda i, j: (0, i)),
                  pl.BlockSpec((None, s, tn), lambda i, j: ((j * tn) // eb, 0, ((j * tn) % eb) // tn))]
        + [ANY] * len(deps),
        out_specs=pl.BlockSpec((None, tm, tn), lambda i, j: ((j * tn) // n, i, ((j * tn) % n) // tn)),
        out_shape=jax.ShapeDtypeStruct((nchunk, ka, n), BF16),
        scratch_shapes=[pltpu.VMEM((tm, s), BF16)],
        compiler_params=_params("arbitrary", "arbitrary"),
    )(a, b3, *deps)


def _mm_tn_group(a, b3, idx, pos, nchunk, name, deps=()):
    s, ka = a.shape
    _, _, eb = b3.shape
    n = eb // nchunk
    tm = _tile(ka, 1024)
    tn = _tile(n, 1024)
    nd = len(deps)

    def body(idx_ref, a_ref, b_ref, *rest):
        o_ref, at_ref = rest[nd:]

        @pl.when(pl.program_id(1) == 0)
        def _():
            at_ref[...] = a_ref[...].astype(F32).T.astype(at_ref.dtype)

        o_ref[...] = jnp.dot(at_ref[...], b_ref[...], preferred_element_type=F32).astype(o_ref.dtype)

    grid_spec = pltpu.PrefetchScalarGridSpec(
        num_scalar_prefetch=1, grid=(ka // tm, eb // tn),
        in_specs=[pl.BlockSpec((s, tm), lambda i, j, idx: (0, i)),
                  pl.BlockSpec((None, s, tn), lambda i, j, idx: (idx[pos], 0, j))] + [ANY] * nd,
        out_specs=pl.BlockSpec((None, tm, tn), lambda i, j, idx: ((j * tn) // n, i, ((j * tn) % n) // tn)),
        scratch_shapes=[pltpu.VMEM((tm, s), BF16)])
    return pl.pallas_call(
        body, name=name, grid_spec=grid_spec,
        out_shape=jax.ShapeDtypeStruct((nchunk, ka, n), BF16),
        compiler_params=_params("arbitrary", "arbitrary"),
    )(idx, a, b3, *deps)


def _sigmoid(z):
    return jax.nn.sigmoid(z)


def _shift_down(v, k, fill=0.0, period=None):
    if k == 0:
        return v
    row = lax.broadcasted_iota(jnp.int32, v.shape, 0)
    if period is not None:
        row = row & (period - 1)
    return jnp.where(row >= k, pltpu.roll(v, k, 0), fill)


def _shift_up(v, k, fill=0.0, period=None):
    if k == 0:
        return v
    s = v.shape[0]
    row = lax.broadcasted_iota(jnp.int32, v.shape, 0)
    if period is not None:
        row, s = row & (period - 1), period
    return jnp.where(row < s - k, pltpu.roll(v, v.shape[0] - k, 0), fill)


SCAN_BLOCK = 64


def _scan(a, b, shift):
    s = a.shape[0]
    blk = min(SCAN_BLOCK, s)
    k = 1
    while k < blk:
        b = a * shift(b, k, 0.0, blk) + b
        a = a * shift(a, k, 1.0, blk)
        k *= 2
    nblk = s // blk
    forward = shift is _shift_down
    order = range(nblk) if forward else range(nblk - 1, -1, -1)
    edge = blk - 1 if forward else 0
    out = [None] * nblk
    carry = None
    for i in order:
        h = b[i * blk:(i + 1) * blk]
        if carry is not None:
            h = a[i * blk:(i + 1) * blk] * carry + h
        carry = h[edge:edge + 1]
        out[i] = h
    return jnp.concatenate(out, axis=0) if nblk > 1 else out[0]


def _norm_mod(x, g, scale, shift, name, deps=()):
    s, d = x.shape
    ts = _tile(s, 256)

    def body(x_ref, g_ref, sc_ref, sh_ref, h_ref):
        xv = x_ref[...]
        rstd = lax.rsqrt(jnp.mean(xv * xv, axis=-1, keepdims=True) + EPS)
        nrm = xv * rstd * g_ref[...]
        h_ref[...] = (nrm * (1.0 + sc_ref[...]) + sh_ref[...]).astype(h_ref.dtype)

    vec = pl.BlockSpec((1, d), lambda i: (0, 0))
    return pl.pallas_call(
        _after(body, 4, deps), name=name, grid=(s // ts,),
        in_specs=[pl.BlockSpec((ts, d), lambda i: (i, 0)), vec, vec, vec] + [ANY] * len(deps),
        out_specs=pl.BlockSpec((ts, d), lambda i: (i, 0)),
        out_shape=jax.ShapeDtypeStruct((s, d), BF16),
        compiler_params=_params("arbitrary"),
    )(x, g, scale, shift, *deps)


def _gate_terms(dx, y_ref, gate_ref, dy_ref, dgate_ref):
    dy_ref[...] = (dx * gate_ref[...]).astype(dy_ref.dtype)
    dgate_ref[...] += jnp.sum(dx * y_ref[...].astype(F32), axis=0, keepdims=True)


def _norm_mod_bwd(x, dh, dx_res, g, scale, name, below=None, deps=()):
    s, d = x.shape
    ts = _tile(s, 256)
    nb = 2 if below is not None else 0

    def body(x_ref, dh_ref, dr_ref, g_ref, sc_ref, *rest):
        dx_ref, dsc_ref, dsh_ref, dg_ref = rest[nb:nb + 4]

        @pl.when(pl.program_id(0) == 0)
        def _():
            for ref in rest[nb + 1:nb + 4] + rest[nb + 5:]:
                ref[...] = jnp.zeros_like(ref)

        xv = x_ref[...]
        dh_v = dh_ref[...].astype(F32)
        gv = g_ref[...]
        rstd = lax.rsqrt(jnp.mean(xv * xv, axis=-1, keepdims=True) + EPS)
        xhat = xv * rstd
        dsc_ref[...] += jnp.sum(dh_v * xhat * gv, axis=0, keepdims=True)
        dsh_ref[...] += jnp.sum(dh_v, axis=0, keepdims=True)
        dn = dh_v * (1.0 + sc_ref[...])
        dg_ref[...] += jnp.sum(dn * xhat, axis=0, keepdims=True)
        dxhat = dn * gv
        proj = jnp.mean(dxhat * xhat, axis=-1, keepdims=True)
        dx = dr_ref[...] + rstd * (dxhat - xhat * proj)
        dx_ref[...] = dx
        if nb:
            _gate_terms(dx, rest[0], rest[1], rest[nb + 4], rest[nb + 5])

    row = pl.BlockSpec((ts, d), lambda i: (i, 0))
    vec = pl.BlockSpec((1, d), lambda i: (0, 0))
    extra = list(below) if nb else []
    return pl.pallas_call(
        _after(body, 5 + nb, deps), name=name, grid=(s // ts,),
        in_specs=[row, row, row, vec, vec] + [row, vec][:nb] + [ANY] * len(deps),
        out_specs=[row, vec, vec, vec] + [row, vec][:nb],
        out_shape=[jax.ShapeDtypeStruct((s, d), F32)] + [jax.ShapeDtypeStruct((1, d), F32)] * 3
        + [jax.ShapeDtypeStruct((s, d), BF16), jax.ShapeDtypeStruct((1, d), F32)][:nb],
        compiler_params=_params("arbitrary"),
    )(x, dh, dx_res, g, scale, *extra, *deps)


def _final_loss(x, g, target, y, gate, name):
    s, d = x.shape
    ts = _tile(s, 256)

    def body(x_ref, g_ref, t_ref, y_ref, gate_ref, dx_ref, loss_ref, dg_ref, dy_ref, dgate_ref):
        @pl.when(pl.program_id(0) == 0)
        def _():
            loss_ref[...] = jnp.zeros_like(loss_ref)
            dg_ref[...] = jnp.zeros_like(dg_ref)
            dgate_ref[...] = jnp.zeros_like(dgate_ref)

        xv = x_ref[...]
        gv = g_ref[...]
        rstd = lax.rsqrt(jnp.mean(xv * xv, axis=-1, keepdims=True) + EPS)
        xhat = xv * rstd
        err = xhat * gv - t_ref[...]
        loss_ref[...] += 0.5 * jnp.sum(jnp.mean(err * err, axis=-1, keepdims=True))
        dy = err * (1.0 / d)
        dg_ref[...] += jnp.sum(dy * xhat, axis=0, keepdims=True)
        dxhat = dy * gv
        proj = jnp.mean(dxhat * xhat, axis=-1, keepdims=True)
        dx = rstd * (dxhat - xhat * proj)
        dx_ref[...] = dx
        _gate_terms(dx, y_ref, gate_ref, dy_ref, dgate_ref)

    row = pl.BlockSpec((ts, d), lambda i: (i, 0))
    vec = pl.BlockSpec((1, d), lambda i: (0, 0))
    return pl.pallas_call(
        body, name=name, grid=(s // ts,),
        in_specs=[row, vec, row, row, vec],
        out_specs=[row, pl.BlockSpec((1, LANES), lambda i: (0, 0)), vec, row, vec],
        out_shape=[jax.ShapeDtypeStruct((s, d), F32), jax.ShapeDtypeStruct((1, LANES), F32),
                   jax.ShapeDtypeStruct((1, d), F32), jax.ShapeDtypeStruct((s, d), BF16),
                   jax.ShapeDtypeStruct((1, d), F32)],
        compiler_params=_params("arbitrary"),
    )(x, g, target, y, gate)


def _conv(v, w_ref, width):
    out = w_ref[width - 1:width, :] * v
    for k in range(width - 1):
        out = out + w_ref[k:k + 1, :] * _shift_down(v, width - 1 - k)
    return out


def _sc_fwd(proj, conv_w, name, deps=()):
    _, s, e = proj.shape
    te = _tile(e, 256)
    width = conv_w.shape[0]

    def body(b_ref, c_ref, v_ref, g_ref, w_ref, o_ref):
        cv = c_ref[...].astype(F32) * v_ref[...].astype(F32)
        u = _conv(cv, w_ref, width)
        gv = g_ref[...].astype(F32)
        o_ref[...] = (b_ref[...].astype(F32) * u * (gv * _sigmoid(gv))).astype(o_ref.dtype)

    def part(q):
        return pl.BlockSpec((None, s, te), lambda j, q=q: (q, 0, j))

    return pl.pallas_call(
        _after(body, 5, deps), name=name, grid=(e // te,),
        in_specs=[part(0), part(1), part(2), part(3), pl.BlockSpec((width, te), lambda j: (0, j))]
        + [ANY] * len(deps),
        out_specs=pl.BlockSpec((s, te), lambda j: (0, j)),
        out_shape=jax.ShapeDtypeStruct((s, e), BF16),
        compiler_params=_params("arbitrary"),
    )(proj, proj, proj, proj, conv_w, *deps)


def _sc_bwd(proj, dyb, conv_w, name, deps=()):
    _, s, e = proj.shape
    te = _tile(e, 256)
    width = conv_w.shape[0]

    def body(b_ref, c_ref, v_ref, g_ref, dy_ref, w_ref, dp_ref, vec_ref):
        bv = b_ref[...].astype(F32)
        cvl = c_ref[...].astype(F32)
        vv = v_ref[...].astype(F32)
        gv = g_ref[...].astype(F32)
        dyv = dy_ref[...].astype(F32)
        cv = cvl * vv
        u = _conv(cv, w_ref, width)
        sg = _sigmoid(gv)
        silu = gv * sg
        dp_ref[0] = (dyv * u * silu).astype(dp_ref.dtype)
        du = dyv * bv * silu
        dp_ref[3] = (dyv * bv * u * (sg * (1.0 + gv * (1.0 - sg)))).astype(dp_ref.dtype)
        dcv = w_ref[width - 1:width, :] * du
        vec_ref[...] = jnp.zeros_like(vec_ref)
        vec_ref[width - 1:width, :] = jnp.sum(du * cv, axis=0, keepdims=True)
        for k in range(width - 1):
            sh = width - 1 - k
            dcv = dcv + w_ref[k:k + 1, :] * _shift_up(du, sh)
            vec_ref[k:k + 1, :] = jnp.sum(du * _shift_down(cv, sh), axis=0, keepdims=True)
        dp_ref[1] = (dcv * vv).astype(dp_ref.dtype)
        dp_ref[2] = (dcv * cvl).astype(dp_ref.dtype)

    def part(q):
        return pl.BlockSpec((None, s, te), lambda j, q=q: (q, 0, j))

    return pl.pallas_call(
        _after(body, 6, deps), name=name, grid=(e // te,),
        in_specs=[part(0), part(1), part(2), part(3), pl.BlockSpec((s, te), lambda j: (0, j)),
                  pl.BlockSpec((width, te), lambda j: (0, j))] + [ANY] * len(deps),
        out_specs=[pl.BlockSpec((4, s, te), lambda j: (0, 0, j)),
                   pl.BlockSpec((8, te), lambda j: (0, j))],
        out_shape=[jax.ShapeDtypeStruct((4, s, e), BF16), jax.ShapeDtypeStruct((8, e), F32)],
        compiler_params=_params("arbitrary"),
    )(proj, proj, proj, proj, dyb, conv_w, *deps)


def _lru_gates(v_pre, w_ref, cb_ref, wa_ref, ba_ref, wx_ref, bx_ref, lam_ref, width):
    v = _conv(v_pre, w_ref, width) + cb_ref[...]
    vb = v.astype(BF16)
    r = _sigmoid(jnp.dot(vb, wa_ref[...], preferred_element_type=F32) + ba_ref[...])
    i = _sigmoid(jnp.dot(vb, wx_ref[...], preferred_element_type=F32) + bx_ref[...])
    nl = -lam_ref[...]
    sp = jnp.maximum(nl, 0.0) + jnp.log1p(jnp.exp(-jnp.abs(nl)))
    log_a = (-RGLRU_C) * r * sp
    a = jnp.exp(log_a)
    one_minus_a2 = jnp.tanh(-log_a) * (1.0 + a * a)
    mult = jnp.sqrt(one_minus_a2)
    return v, vb, r, i, sp, a, mult


def _lru_specs(s, dh, heads, width):
    head_col = lambda q: pl.BlockSpec((None, s, dh), lambda h, q=q: (q, 0, h))
    vec = pl.BlockSpec((1, dh), lambda h: (0, h))
    mat = pl.BlockSpec((None, dh, dh), lambda h: (h, 0, 0))
    weights = [pl.BlockSpec((width, dh), lambda h: (0, h)), vec, mat, vec, mat, vec, vec]
    return head_col, weights


def _lru_fwd(proj, conv_w, conv_b, w_a, b_a, w_x, b_x, lam, name, deps=()):
    _, s, e = proj.shape
    heads, dh, _ = w_a.shape
    width = conv_w.shape[0]

    def body(v_ref, g_ref, w_ref, cb_ref, wa_ref, ba_ref, wx_ref, bx_ref, lam_ref, yb_ref, keep_ref):
        v, _, r, i, _, a, mult = _lru_gates(v_ref[...].astype(F32), w_ref, cb_ref, wa_ref, ba_ref,
                                           wx_ref, bx_ref, lam_ref, width)
        hs = _scan(a, mult * i * v, _shift_down)
        for k, val in enumerate((hs, v, r, i, a, mult)):
            keep_ref[k] = val
        gv = g_ref[...].astype(F32)
        yb_ref[...] = (hs * (gv * _sigmoid(gv))).astype(yb_ref.dtype)

    head_col, weights = _lru_specs(s, dh, heads, width)
    return pl.pallas_call(
        _after(body, 9, deps), name=name, grid=(heads,),
        in_specs=[head_col(0), head_col(1)] + weights + [ANY] * len(deps),
        out_specs=[pl.BlockSpec((s, dh), lambda h: (0, h)), pl.BlockSpec((6, s, dh), lambda h: (0, 0, h))],
        out_shape=[jax.ShapeDtypeStruct((s, e), BF16), jax.ShapeDtypeStruct((6, s, e), F32)],
        compiler_params=_params("arbitrary"),
    )(proj, proj, conv_w, conv_b, w_a, b_a, w_x, b_x, lam, *deps)


def _lru_bwd(proj, keep, dyb, conv_w, conv_b, w_a, b_a, w_x, b_x, lam, name, deps=()):
    _, s, e = proj.shape
    heads, dh, _ = w_a.shape
    width = conv_w.shape[0]

    def body(v_ref, g_ref, hs_ref, dy_ref, w_ref, cb_ref, wa_ref, ba_ref, wx_ref, bx_ref, lam_ref,
             dp_ref, dwa_ref, dwx_ref, vec_ref):
        v_pre = v_ref[...].astype(F32)
        hs, v, r, i, a, mult = (hs_ref[k] for k in range(6))
        vb = v.astype(BF16)
        nl = -lam_ref[...]
        sp = jnp.maximum(nl, 0.0) + jnp.log1p(jnp.exp(-jnp.abs(nl)))
        gv = g_ref[...].astype(F32)
        dyv = dy_ref[...].astype(F32)
        sg = _sigmoid(gv)
        dp_ref[1] = (dyv * hs * (sg * (1.0 + gv * (1.0 - sg)))).astype(dp_ref.dtype)
        dhs = dyv * (gv * sg)
        d_h = _scan(_shift_up(a, 1), dhs, _shift_up)
        da = d_h * _shift_down(hs, 1)
        iv = i * v
        dlog_a = da * a - (d_h * iv) * (a * a) / mult
        di = d_h * mult * v
        dv = d_h * mult * i
        dzr = dlog_a * (-RGLRU_C) * sp * r * (1.0 - r)
        dzi = di * i * (1.0 - i)
        dsp = jnp.sum(dlog_a * r, axis=0, keepdims=True) * (-RGLRU_C)
        vec_ref[...] = jnp.zeros_like(vec_ref)
        vec_ref[0:1, :] = jnp.sum(dzr, axis=0, keepdims=True)
        vec_ref[1:2, :] = jnp.sum(dzi, axis=0, keepdims=True)
        vec_ref[2:3, :] = -dsp * _sigmoid(-lam_ref[...])
        dzr_b = dzr.astype(BF16)
        dzi_b = dzi.astype(BF16)
        vt = vb.astype(F32).T.astype(BF16)
        dwa_ref[...] = jnp.dot(vt, dzr_b, preferred_element_type=F32).astype(dwa_ref.dtype)
        dwx_ref[...] = jnp.dot(vt, dzi_b, preferred_element_type=F32).astype(dwx_ref.dtype)
        nt = (((1,), (1,)), ((), ()))
        dv = dv + lax.dot_general(dzr_b, wa_ref[...], nt, preferred_element_type=F32)
        dv = dv + lax.dot_general(dzi_b, wx_ref[...], nt, preferred_element_type=F32)
        vec_ref[3:4, :] = jnp.sum(dv, axis=0, keepdims=True)
        dvp = w_ref[width - 1:width, :] * dv
        vec_ref[4 + width - 1:4 + width, :] = jnp.sum(dv * v_pre, axis=0, keepdims=True)
        for k in range(width - 1):
            sh = width - 1 - k
            dvp = dvp + w_ref[k:k + 1, :] * _shift_up(dv, sh)
            vec_ref[4 + k:5 + k, :] = jnp.sum(dv * _shift_down(v_pre, sh), axis=0, keepdims=True)
        dp_ref[0] = dvp.astype(dp_ref.dtype)

    head_col, weights = _lru_specs(s, dh, heads, width)
    col = pl.BlockSpec((s, dh), lambda h: (0, h))
    mat = pl.BlockSpec((None, dh, dh), lambda h: (h, 0, 0))
    return pl.pallas_call(
        _after(body, 11, deps), name=name, grid=(heads,),
        in_specs=[head_col(0), head_col(1), pl.BlockSpec((6, s, dh), lambda h: (0, 0, h)), col] + weights
        + [ANY] * len(deps),
        out_specs=[pl.BlockSpec((2, s, dh), lambda h: (0, 0, h)), mat, mat,
                   pl.BlockSpec((16, dh), lambda h: (0, h))],
        out_shape=[jax.ShapeDtypeStruct((2, s, e), BF16),
                   jax.ShapeDtypeStruct((heads, dh, dh), BF16),
                   jax.ShapeDtypeStruct((heads, dh, dh), BF16),
                   jax.ShapeDtypeStruct((16, e), F32)],
        compiler_params=_params("arbitrary"),
    )(proj, proj, keep, dyb, conv_w, conv_b, w_a, b_a, w_x, b_x, lam, *deps)


def _ada_mod(c_all, w, b, name):
    layers, d, f = w.shape
    nb = c_all.shape[0]

    def body(c_ref, w_ref, b_ref, o_ref):
        cv = c_ref[...]
        sc = cv * _sigmoid(cv)
        o_ref[...] = jnp.dot(sc, w_ref[...], preferred_element_type=F32,
                             precision=lax.Precision.HIGHEST) + b_ref[...]

    return pl.pallas_call(
        body, name=name, grid=(layers,),
        in_specs=[pl.BlockSpec((nb, d), lambda l: (0, 0)),
                  pl.BlockSpec((None, d, f), lambda l: (l, 0, 0)),
                  pl.BlockSpec((None, 1, f), lambda l: (l, 0, 0))],
        out_specs=pl.BlockSpec((None, nb, f), lambda l: (l, 0, 0)),
        out_shape=jax.ShapeDtypeStruct((layers, nb, f), F32),
        compiler_params=_params("arbitrary"),
    )(c_all, w, b)


def _ada_update(c_all_t, dmod, w, m, v, name):
    d, nb = c_all_t.shape
    layers, _, f = dmod.shape
    tr = _tile(d, 512)

    def body(c_ref, dm_ref, w_ref, m_ref, v_ref, g_ref, d_ref, mo_ref, vo_ref):
        cv = c_ref[...]
        sc = cv * _sigmoid(cv)
        g = sc[:, 0:1] * dm_ref[0:1, :]
        for k in range(1, nb):
            g = g + sc[:, k:k + 1] * dm_ref[k:k + 1, :]
        g_ref[...] = g
        d_ref[...], mo_ref[...], vo_ref[...] = _adamw_math(w_ref[...], g, m_ref[...], v_ref[...])

    blk = pl.BlockSpec((None, tr, f), lambda l, i: (l, i, 0))
    return pl.pallas_call(
        body, name=name, grid=(layers, d // tr),
        in_specs=[pl.BlockSpec((tr, nb), lambda l, i: (i, 0)),
                  pl.BlockSpec((None, nb, f), lambda l, i: (l, 0, 0)), blk, blk, blk],
        out_specs=[blk] * 4,
        out_shape=[jax.ShapeDtypeStruct((layers, d, f), F32)] * 4,
        compiler_params=_params("arbitrary", "arbitrary"),
    )(c_all_t, dmod, w, m, v)


def _device_sum(g, name):
    _, rows, _ = g.shape

    def body(g_ref, o_ref):
        acc = g_ref[0]
        for k in range(1, N_DEV):
            acc = acc + g_ref[k]
        o_ref[...] = acc

    return pl.pallas_call(
        body, name=name,
        in_specs=[VMEM_SPEC], out_specs=VMEM_SPEC,
        out_shape=jax.ShapeDtypeStruct((rows, LANES), F32),
        compiler_params=pltpu.CompilerParams(vmem_limit_bytes=VMEM_LIMIT),
    )(g)


def _adamw_math(w, g, m, v):
    m = ADAM_B1 * m + (1.0 - ADAM_B1) * g
    v = ADAM_B2 * v + (1.0 - ADAM_B2) * (g * g)
    m_hat = m / (1.0 - ADAM_B1 ** ADAM_STEP)
    v_hat = v / (1.0 - ADAM_B2 ** ADAM_STEP)
    delta = -ADAM_LR * (m_hat / (jnp.sqrt(v_hat) + ADAM_EPS) + ADAM_WD * w)
    return delta, m, v


def _adamw(w, g, m, v, name):
    rows, cols = w.shape
    tr = _tile(rows, 256)

    def body(w_ref, g_ref, m_ref, v_ref, d_ref, mo_ref, vo_ref):
        d_ref[...], mo_ref[...], vo_ref[...] = _adamw_math(w_ref[...], g_ref[...], m_ref[...], v_ref[...])

    blk = pl.BlockSpec((tr, cols), lambda i: (i, 0))
    return pl.pallas_call(
        body, name=name, grid=(rows // tr,),
        in_specs=[blk] * 4, out_specs=[blk] * 3,
        out_shape=[jax.ShapeDtypeStruct((rows, cols), F32)] * 3,
        compiler_params=_params("arbitrary"),
    )(w, g, m, v)


def _adamw_reduced(idx, w, m, v, part, got, recvs, name):
    rows, cols = w.shape
    tr = _tile(rows, 256)
    nr = len(recvs)

    def body(idx_ref, w_ref, m_ref, v_ref, p_ref, q_ref, *rest):
        g_ref, d_ref, mo_ref, vo_ref = rest[nr:]
        g = p_ref[...].astype(F32) + q_ref[...].astype(F32)
        for u_ref in rest[:nr]:
            for j in range(u_ref.shape[0]):
                g = g + u_ref[j].astype(F32)
        g_ref[...] = g
        d_ref[...], mo_ref[...], vo_ref[...] = _adamw_math(w_ref[...], g, m_ref[...], v_ref[...])

    blk = pl.BlockSpec((tr, cols), lambda i, idx: (i, 0))
    grid_spec = pltpu.PrefetchScalarGridSpec(
        num_scalar_prefetch=1, grid=(rows // tr,),
        in_specs=[blk, blk, blk,
                  pl.BlockSpec((None, None, tr, cols), lambda i, idx: (idx[3], idx[4], i, 0)),
                  pl.BlockSpec((None, None, tr, cols), lambda i, idx: (idx[3], 0, i, 0))]
        + [pl.BlockSpec((u.shape[0], tr, cols), lambda i, idx: (0, i, 0)) for u in recvs],
        out_specs=[blk] * 4)
    return pl.pallas_call(
        body, name=name, grid_spec=grid_spec,
        out_shape=[jax.ShapeDtypeStruct((rows, cols), F32)] * 4,
        compiler_params=_params("arbitrary"),
    )(idx, w, m, v, part, got, *recvs)


def _pack(vectors):
    flat = jnp.concatenate([v.reshape(-1).astype(F32) for v in vectors])
    pad = (-flat.shape[0]) % (8 * LANES)
    return jnp.pad(flat, (0, pad)).reshape(-1, LANES)


def _unpack(flat, shapes):
    out, off = [], 0
    for shp in shapes:
        size = math.prod(shp)
        out.append(flat[..., off:off + size].reshape(flat.shape[:-1] + tuple(shp)))
        off += size
    return out


def _my_slice(full, me, axis):
    size = full.shape[axis] // N_DEV
    return lax.dynamic_slice_in_dim(full, me * size, size, axis)


def kernel(x, c, norm_g, ada_w, ada_b, sc_w_in, sc_conv_w, sc_w_out, lru_w_in, lru_conv_w, lru_conv_b, lru_w_a, lru_b_a, lru_w_x, lru_b_x, lru_lambda, lru_w_out, final_g, loss_target, m_norm_g, m_ada_w, m_ada_b, m_sc_w_in, m_sc_conv_w, m_sc_w_out, m_lru_w_in, m_lru_conv_w, m_lru_conv_b, m_lru_w_a, m_lru_b_a, m_lru_w_x, m_lru_b_x, m_lru_lambda, m_lru_w_out, m_final_g, v_norm_g, v_ada_w, v_ada_b, v_sc_w_in, v_sc_conv_w, v_sc_w_out, v_lru_w_in, v_lru_conv_w, v_lru_conv_b, v_lru_w_a, v_lru_b_a, v_lru_w_x, v_lru_b_x, v_lru_lambda, v_lru_w_out, v_final_g):
    _, s, d = x.shape
    e = sc_w_out.shape[1] * N_DEV
    heads, dh_s, dh = lru_w_a.shape[1:]
    es = e // N_DEV
    f = ada_w.shape[2]
    mx, my, mc = _position()
    me = 4 * mx + 2 * my + mc
    chip = 2 * mx + my
    idx = jnp.stack([chip ^ 1, chip ^ 2, chip ^ 3, chip, mc]).astype(jnp.int32)

    x0 = x[0]
    target = loss_target[0]

    small_shapes = [(d,), (3, es), (4, es), (es,), (heads, dh_s), (heads, dh_s), (es,)]
    small = _small_gather(_pack([c, sc_conv_w, lru_conv_w, lru_conv_b, lru_b_a, lru_b_x, lru_lambda]),
                          "gather_small_weights").reshape(N_DEV, -1)
    c_all, cw3, cw4, cb, ba, bx, lam = _unpack(small, small_shapes)
    cw3 = cw3.transpose(1, 0, 2).reshape(3, e)
    cw4 = cw4.transpose(1, 0, 2).reshape(4, e)
    cb = cb.reshape(1, e)
    lam = lam.reshape(1, e)
    ba = ba.transpose(1, 0, 2).reshape(1, e)
    bx = bx.transpose(1, 0, 2).reshape(1, e)

    shards = [sc_w_in[0].astype(BF16), sc_w_out[0].astype(BF16), lru_w_in[0].astype(BF16),
              lru_w_a[0].reshape(heads * dh_s, dh).astype(BF16),
              lru_w_x[0].reshape(heads * dh_s, dh).astype(BF16), lru_w_out[0].astype(BF16)]
    lands = [lax.dynamic_update_slice(lax.empty((N_DEV,) + sh.shape, BF16), sh[None], (me, 0, 0))
             for sh in shards]
    every = [1, 2, 3, 0]
    units = [([0], [0]), ([0], [1]), ([0], [2]), ([0], [3]), ([1], every), ([2], every), ([3, 4], every),
             ([5], every)]
    sems, first_sh, first_ld, started = _gather_start(shards[:1], lands[:1], units[:3], [small],
                                                      "gather_start_first")
    shards, lands = first_sh + shards[1:], first_ld + lands[1:]

    ada_b_mine = _my_slice(ada_b, me, 1).reshape(2, 1, f)
    mod_mine = _ada_mod(c_all, ada_w, ada_b_mine, "ada_mod")
    mod_all = _small_gather(_pack([mod_mine]), "gather_mod", deps=[started])

    def start_later(after):
        far_sems, far_sh, far_ld, tok = _gather_start(shards[:1], lands[:1], units[3:4], after, "gather_start_far")
        rest_units = [([i - 1 for i in members], ks) for members, ks in units[4:]]
        rest_sems, rest_sh, rest_ld, tok = _gather_start(shards[1:], lands[1:], rest_units, [tok],
                                                         "gather_start_rest")
        sems.extend(far_sems + rest_sems)
        shards[:], lands[:] = far_sh + rest_sh, far_ld + rest_ld
        return tok

    def gathered(u, after_forward, name):
        members, ks = units[u]
        fwd, shs, lnd, token = _gather_forward(
            [shards[i] for i in members], [lands[i] for i in members], ks, sems[u][0], sems[u][1],
            after_forward, "gather_forward_" + name)
        for i, sh, ld in zip(members, shs, lnd):
            shards[i], lands[i] = sh, ld

        def finish(after):
            out = _gather_finish([lands[i] for i in members], ks, fwd, after, "gather_finish_" + name)
            for i, ld in zip(members, out):
                lands[i] = ld
            return out

        return token, finish

    tok, finish_y = gathered(1, [mod_all], "sc_w_in_near_y")
    tok, finish_x = gathered(2, [tok], "sc_w_in_near_x")
    queued = start_later([tok])

    mod_all = mod_all.reshape(N_DEV, -1)
    mod_all = mod_all[:, :2 * N_DEV * f].reshape(N_DEV, 2, N_DEV, f)
    mod_all = mod_all.transpose(1, 2, 0, 3).reshape(2, N_DEV, 3 * d)
    mod = lax.dynamic_index_in_dim(mod_all, me, 1, keepdims=False)
    shift = [mod[l:l + 1, 0:d] for l in range(2)]
    scale = [mod[l:l + 1, d:2 * d] for l in range(2)]
    gate = [mod[l:l + 1, 2 * d:3 * d] for l in range(2)]
    ng = [norm_g[l:l + 1] for l in range(2)]
    fg = final_g.reshape(1, d)

    h0 = _norm_mod(x0, ng[0], scale[0], shift[0], "norm_mod_0", deps=[queued])
    proj0 = lax.empty((4, s, e), BF16)
    tok, _ = gathered(0, [h0], "sc_w_in_own")
    proj0 = _mm_proj_group(h0, lands[0], idx, 3, proj0, "mm_proj_0_own", deps=[tok])
    for u, name, finish in ((1, "near_y", finish_y), (2, "near_x", finish_x), (3, "far", None)):
        after = [proj0]
        if finish is None:
            tok, finish = gathered(u, [proj0], "sc_w_in_" + name)
            after = [tok]
        wg_in0, = finish(after)
        proj0 = _mm_proj_group(h0, wg_in0, idx, u - 1, proj0, "mm_proj_0_" + name)
    tok, finish = gathered(4, [proj0], "sc_w_out")
    yb0 = _sc_fwd(proj0, cw3, "sc_fwd", deps=[tok])
    w_out0 = finish([yb0])[0].reshape(e, d)
    x1, y0 = _mm_out(yb0, w_out0, x0, gate[0], "mm_out_0")
    tok, finish = gathered(5, [x1], "lru_w_in")
    h1 = _norm_mod(x1, ng[1], scale[1], shift[1], "norm_mod_1", deps=[tok])
    wg_in1, = finish([h1])
    proj1 = _mm_proj(h1, wg_in1, 2, "mm_proj_1")
    tok, finish = gathered(6, [proj1], "lru_gates")
    wg_a, wg_x = finish([tok])
    w_a = wg_a.reshape(N_DEV, heads, dh_s, dh).transpose(1, 0, 2, 3).reshape(heads, dh, dh)
    w_x = wg_x.reshape(N_DEV, heads, dh_s, dh).transpose(1, 0, 2, 3).reshape(heads, dh, dh)
    tok, finish = gathered(7, [w_a, w_x], "lru_w_out")
    yb1, hs = _lru_fwd(proj1, cw4, cb, w_a, ba, w_x, bx, lam, "lru_fwd", deps=[tok])
    w_out1 = finish([yb1])[0].reshape(e, d)
    x2, y1 = _mm_out(yb1, w_out1, x1, gate[1], "mm_out_1")
    dx2, loss_part, d_fg, dy1, dgate1 = _final_loss(x2, fg, target, y1, gate[1], "final_loss")

    def pieces(g, rows, cols):
        return g.reshape(4, 2, rows, cols)

    def by_rows(g):
        return g.reshape(heads, N_DEV, dh_s, dh).transpose(1, 0, 2, 3).reshape(N_DEV, heads * dh_s, dh)

    def pair_begin(parts, group):
        send, recv, parts, lnd, token = _pair_start(parts, "pair_start_" + group)
        return dict(parts=parts, lands=lnd, send=send, recv=recv, group=group), token

    def scatter_start(pair, names, after):
        group = pair["group"]
        parts, gots = _pair_wait(pair["parts"], pair["lands"], pair["send"], pair["recv"], after,
                                 "pair_wait_" + group)
        sums = [_pair_sum(idx, p, q, "pair_sum_" + nm) for p, q, nm in zip(parts, gots, names)]
        empties = [lax.empty(sm.shape, sm.dtype) for sm in sums]
        send, recv, sums, lnd, token = _chip_start(sums, empties, "chip_start_" + group)
        return dict(parts=parts, gots=gots, names=names, group=group, sums=sums, lands=lnd,
                    send=send, recv=recv), token

    big = {"sc_w_in": (sc_w_in, m_sc_w_in, v_sc_w_in), "sc_w_out": (sc_w_out, m_sc_w_out, v_sc_w_out),
           "lru_w_in": (lru_w_in, m_lru_w_in, v_lru_w_in), "lru_w_a": (lru_w_a, m_lru_w_a, v_lru_w_a),
           "lru_w_x": (lru_w_x, m_lru_w_x, v_lru_w_x), "lru_w_out": (lru_w_out, m_lru_w_out, v_lru_w_out)}
    big_res = {}

    def scatter_finish(rs, after):
        recvs = _chip_wait(rs["sums"], rs["lands"], rs["send"], rs["recv"], after, "chip_wait_" + rs["group"])
        done = []
        for p, q, u, nm in zip(rs["parts"], rs["gots"], recvs, rs["names"]):
            w, m, v = big[nm]
            shp2 = p.shape[2:]
            res = _adamw_reduced(idx, w.reshape(shp2), m.reshape(shp2), v.reshape(shp2), p, q, [u], "adamw_" + nm)
            big_res[nm] = [r.reshape(w.shape) for r in res]
            done.append(res[1])
        return done

    dw_out1 = _mm_tn(yb1, dy1[None], 1, "mm_dw_out_1")
    pair, tok = pair_begin([pieces(dw_out1, es, d)], "lru_w_out")
    dyb1 = _mm_nt(dy1[None], w_out1[None], BF16, "mm_dyb_1", deps=[tok])
    rs1, tok = scatter_start(pair, ["lru_w_out"], [dyb1])
    dproj1, dw_a, dw_x, vecs1 = _lru_bwd(proj1, hs, dyb1, cw4, cb, w_a, ba, w_x, bx, lam, "lru_bwd", deps=[tok])
    dw_in1 = _mm_tn(h1, dproj1, N_DEV, "mm_dw_in_1")
    pair, tok = pair_begin([pieces(dw_in1, d, 2 * es), pieces(by_rows(dw_a), heads * dh_s, dh),
                            pieces(by_rows(dw_x), heads * dh_s, dh)], "lru_in")
    dh1 = _mm_nt(dproj1, wg_in1, BF16, "mm_dh_1", deps=[tok])
    rs2, tok = scatter_start(pair, ["lru_w_in", "lru_w_a", "lru_w_x"], [dh1])
    dx1, dscale1, dshift1, dng1, dy0, dgate0 = _norm_mod_bwd(x1, dh1, dx2, ng[1], scale[1], "norm_mod_bwd_1",
                                                             below=(y0, gate[0]), deps=[tok])
    dw_out0 = _mm_tn(yb0, dy0[None], 1, "mm_dw_out_0")
    pair, tok = pair_begin([pieces(dw_out0, es, d)], "sc_w_out")
    dyb0 = _mm_nt(dy0[None], w_out0[None], BF16, "mm_dyb_0", deps=[tok])
    rs3, tok = scatter_start(pair, ["sc_w_out"], [dyb0])
    dproj0, vecs0 = _sc_bwd(proj0, dyb0, cw3, "sc_bwd", deps=[tok])
    idx_one = jnp.stack([jnp.zeros_like(mc)] * 4 + [mc]).astype(jnp.int32)
    sc_w_in_steps = []

    def chip_step(j, pair, after):
        (part,), (got,) = _pair_wait(pair["parts"], pair["lands"], pair["send"], pair["recv"], after,
                                     "pair_wait_sc_w_in_%d" % j)
        sm = _pair_sum(idx_one, part, got, "pair_sum_sc_w_in_%d" % j, nslots=1)
        send, recv, sums, lnd, token = _chip_start([sm], [lax.empty(sm.shape, sm.dtype)],
                                                   "chip_start_sc_w_in_%d" % j, flips=(j,))
        sc_w_in_steps.append((sums, lnd, send, recv, j))
        return token

    pending, done = None, []
    for j in (3, 2, 1, 0):
        part = _mm_tn_group(h0, dproj0, idx, (j - 1) % 4, 2, "mm_dw_in_0_%d" % j, deps=done)[None]
        pair, tok = pair_begin([part], "sc_w_in_%d" % j)
        if j == 3:
            done = [chip_step(j, pair, [tok])]
            continue
        done = [tok]
        if pending is not None:
            done.append(chip_step(pending[0], pending[1], [tok]))
        pending = (j, pair)
    done += scatter_finish(rs1, done)
    done += scatter_finish(rs2, done)
    dh0 = _mm_nt(dproj0, wg_in0, BF16, "mm_dh_0", deps=done)
    pair = pending[1]
    (part,), (got,) = _pair_wait(pair["parts"], pair["lands"], pair["send"], pair["recv"], [dh0],
                                 "pair_wait_sc_w_in_0")
    dx0, dscale0, dshift0, dng0 = _norm_mod_bwd(x0, dh0, dx1, ng[0], scale[0], "norm_mod_bwd_0")
    done = scatter_finish(rs3, [dx0])
    dmod_mine = jnp.concatenate([dshift0, dscale0, dgate0, dshift1, dscale1, dgate1], axis=1)
    end_shapes = [(LANES,), (2, 3 * d), (2, d), (d,), (8, e), (16, e)]
    end_all = _small_gather(
        _pack([loss_part, dmod_mine, jnp.concatenate([dng0, dng1], axis=0), d_fg, vecs0, vecs1]),
        "gather_small_grads", deps=done)
    end_sum = _device_sum(end_all, "sum_small_grads").reshape(-1)
    loss_v, g_ada_b, g_norm_g, g_final_g, sum0, sum1 = _unpack(end_sum, end_shapes)
    loss = loss_v[0]
    dmod_all = _unpack(end_all.reshape(N_DEV, -1), end_shapes)[1].transpose(1, 0, 2)
    dmod_cols = _my_slice(dmod_all, me, 2)
    ada_out = _ada_update(c_all.T, dmod_cols, ada_w, m_ada_w, v_ada_w, "ada_update")

    g_sc_conv_w = _my_slice(sum0[0:3], me, 1)
    g_lru_b_a = _my_slice(sum1[0].reshape(heads, dh), me, 1)
    g_lru_b_x = _my_slice(sum1[1].reshape(heads, dh), me, 1)
    g_lru_lambda = _my_slice(sum1[2:3], me, 1)
    g_lru_conv_b = _my_slice(sum1[3:4], me, 1)
    g_lru_conv_w = _my_slice(sum1[4:8], me, 1)

    small_w = [norm_g, ada_b, final_g, sc_conv_w, lru_conv_w, lru_conv_b, lru_b_a, lru_b_x, lru_lambda]
    small_m = [m_norm_g, m_ada_b, m_final_g, m_sc_conv_w, m_lru_conv_w, m_lru_conv_b, m_lru_b_a, m_lru_b_x,
               m_lru_lambda]
    small_v = [v_norm_g, v_ada_b, v_final_g, v_sc_conv_w, v_lru_conv_w, v_lru_conv_b, v_lru_b_a, v_lru_b_x,
               v_lru_lambda]
    small_g = [g_norm_g, g_ada_b, g_final_g, g_sc_conv_w, g_lru_conv_w, g_lru_conv_b, g_lru_b_a, g_lru_b_x,
               g_lru_lambda]
    small_g = [g.reshape(w.shape) for g, w in zip(small_g, small_w)]
    shapes = [w.shape for w in small_w]
    packed = _adamw(_pack(small_w), _pack(small_g), _pack(small_m), _pack(small_v), "adamw_small")
    small_out = [small_g] + [_unpack(p.reshape(-1), shapes) for p in packed]

    after = [packed[0], ada_out[1]]
    recvs = []
    for sums, lnd, send, recv, j in sc_w_in_steps:
        recvs += _chip_wait(sums, lnd, send, recv, after, "chip_wait_sc_w_in_%d" % j)
    shp2 = part.shape[2:]
    res = _adamw_reduced(idx_one, sc_w_in.reshape(shp2), m_sc_w_in.reshape(shp2), v_sc_w_in.reshape(shp2),
                         part, got, recvs, "adamw_sc_w_in")
    big_res["sc_w_in"] = [r.reshape(sc_w_in.shape) for r in res]
    big_out = [big_res[nm] for nm in ("sc_w_in", "sc_w_out", "lru_w_in", "lru_w_a", "lru_w_x", "lru_w_out")]

    def small(kind, i):
        return small_out[kind][i]

    def bigw(kind, i):
        return big_out[i][kind]

    outs = [loss, dx0[None]]
    for kind in range(4):
        outs += [small(kind, 0), ada_out[kind], small(kind, 1), bigw(kind, 0), small(kind, 3), bigw(kind, 1),
                 bigw(kind, 2), small(kind, 4), small(kind, 5), bigw(kind, 3), small(kind, 6), bigw(kind, 4),
                 small(kind, 7), small(kind, 8), bigw(kind, 5), small(kind, 2)]
    return tuple(outs)
```

```python
import math

import jax
import jax.numpy as jnp
from jax import lax
from jax.experimental import pallas as pl
from jax.experimental.pallas import tpu as pltpu

N_DEV = 8
LANES = 128
EPS = 1e-6
RGLRU_C = 8.0
ADAM_LR = 0.001
ADAM_B1 = 0.9
ADAM_B2 = 0.999
ADAM_EPS = 1e-08
ADAM_WD = 0.01
ADAM_STEP = 10
VMEM_LIMIT = 56 * 1024 * 1024
MESH = pl.DeviceIdType.MESH
F32 = jnp.float32
BF16 = jnp.bfloat16
ANY = pl.BlockSpec(memory_space=pl.ANY)
HBM = pl.BlockSpec(memory_space=pltpu.HBM)
SEM = pl.BlockSpec(memory_space=pltpu.SEMAPHORE)
VMEM_SPEC = pl.BlockSpec(memory_space=pltpu.VMEM)
EFFECT = pltpu.SideEffectType.DATAFLOW_SIDE_EFFECTING
TOKEN = jax.ShapeDtypeStruct((8, LANES), jnp.float32)


def _tile(n, pref):
    t = min(n, pref)
    assert n % t == 0, (n, pref)
    return t


def _params(*sem):
    return pltpu.CompilerParams(dimension_semantics=sem, vmem_limit_bytes=VMEM_LIMIT)


def _position():
    return lax.axis_index("x"), lax.axis_index("y"), lax.axis_index("c")


def _flip(x, y, k):
    return (1 - x if k & 2 else x), (1 - y if k & 1 else y)


def _after(body, n_in, deps):
    if not deps:
        return body

    def wrapped(*refs):
        return body(*refs[:n_in], *refs[n_in + len(deps):])

    return wrapped


def _small_gather(v, name, deps=()):
    rows = v.shape[0]

    def body(v_ref, out_ref, send_sems, recv_sems):
        x, y, c = _position()
        me = 4 * x + 2 * y + c
        out_ref[me] = v_ref[...]
        copies = []
        for k in range(1, N_DEV):
            px, py = _flip(x, y, k >> 1)
            pc = 1 - c if k & 1 else c
            cp = pltpu.make_async_remote_copy(
                src_ref=v_ref, dst_ref=out_ref.at[me],
                send_sem=send_sems.at[k - 1], recv_sem=recv_sems.at[k - 1],
                device_id=(px, py, pc), device_id_type=MESH)
            cp.start()
            copies.append((cp, 4 * px + 2 * py + pc))
        for k, (cp, peer) in enumerate(copies):
            pltpu.make_async_remote_copy(
                src_ref=v_ref, dst_ref=out_ref.at[peer],
                send_sem=send_sems.at[k], recv_sem=recv_sems.at[k],
                device_id=(x, y, c), device_id_type=MESH).wait_recv()
        for cp, _ in copies:
            cp.wait_send()

    return pl.pallas_call(
        _after(body, 1, deps), name=name,
        out_shape=jax.ShapeDtypeStruct((N_DEV, rows, LANES), F32),
        in_specs=[VMEM_SPEC] + [ANY] * len(deps), out_specs=VMEM_SPEC,
        scratch_shapes=[pltpu.SemaphoreType.DMA((N_DEV - 1,)),
                        pltpu.SemaphoreType.DMA((N_DEV - 1,))],
        compiler_params=pltpu.CompilerParams(vmem_limit_bytes=VMEM_LIMIT),
    )(v, *deps)


def _hbm(a):
    return pltpu.with_memory_space_constraint(a, pltpu.HBM)


def _hbm_like(arrays):
    return [pltpu.HBM(a.shape, a.dtype) for a in arrays]


def _remote(src, dst, send, recv, to):
    return pltpu.make_async_remote_copy(src_ref=src, dst_ref=dst, send_sem=send, recv_sem=recv,
                                        device_id=to, device_id_type=MESH)


def _gather_start(shards, lands, units, after, name):
    n, nu = len(shards), len(units)

    def body(*refs):
        ins, lnd = refs[:n], refs[n:2 * n]
        sems = refs[2 * n + len(after):2 * n + len(after) + 2 * nu]
        token = refs[-1]
        x, y, c = _position()
        me = 4 * x + 2 * y + c
        targets = [(x, y, 1 - c)] + [(px, py, c) for px, py in (_flip(x, y, k) for k in (1, 2, 3))]
        for u, (members, ks) in enumerate(units):
            for slot, i in enumerate(members):
                for ki, k in enumerate(ks):
                    at = len(ks) * slot + ki
                    _remote(ins[i], lnd[i].at[me], sems[2 * u].at[at], sems[2 * u + 1].at[at], targets[k]).start()
        token[...] = jnp.zeros_like(token)

    sem_shapes = []
    for members, ks in units:
        count = len(members) * len(ks)
        sem_shapes += [pltpu.SemaphoreType.DMA((count,)), pltpu.SemaphoreType.DMA((count,))]
    out = pl.pallas_call(
        body, name=name,
        out_shape=sem_shapes + _hbm_like(shards) + _hbm_like(lands) + [TOKEN],
        in_specs=[HBM] * (2 * n) + [ANY] * len(after),
        out_specs=[SEM] * (2 * nu) + [HBM] * (2 * n) + [VMEM_SPEC],
        input_output_aliases={i: 2 * nu + i for i in range(2 * n)},
        compiler_params=pltpu.CompilerParams(has_side_effects=EFFECT),
    )(*[_hbm(s) for s in shards], *[_hbm(l) for l in lands], *after)
    sems = [(out[2 * u], out[2 * u + 1]) for u in range(nu)]
    return sems, list(out[2 * nu:2 * nu + n]), list(out[2 * nu + n:2 * nu + 2 * n]), out[-1]


def _gather_forward(shards, lands, ks, send, recv, after, name):
    m = len(shards)
    hops = [k for k in ks if k]
    nsem = 2 if hops else 0

    def body(*refs):
        ins, lnd = refs[:m], refs[m:2 * m]
        send_ref, recv_ref = refs[2 * m], refs[2 * m + 1]
        outs = refs[2 * m + 2 + len(after):]
        token = refs[-1]
        x, y, c = _position()
        me = (x, y, c)
        for slot in range(m):
            for ki, k in enumerate(ks):
                at = len(ks) * slot + ki
                if k:
                    px, py = _flip(x, y, k)
                    block = lnd[slot].at[4 * px + 2 * py + c]
                else:
                    block = lnd[slot].at[4 * x + 2 * y + (1 - c)]
                arrival = _remote(ins[slot], block, send_ref.at[at], recv_ref.at[at], me)
                arrival.wait_recv()
                if k:
                    fat = len(hops) * slot + hops.index(k)
                    _remote(block, block, outs[0].at[fat], outs[1].at[fat], (x, y, 1 - c)).start()
                arrival.wait_send()
        token[...] = jnp.zeros_like(token)

    count = len(hops) * m
    sem_shapes = [pltpu.SemaphoreType.DMA((count,)), pltpu.SemaphoreType.DMA((count,))] if hops else []
    out = pl.pallas_call(
        body, name=name,
        out_shape=sem_shapes + _hbm_like(shards) + _hbm_like(lands) + [TOKEN],
        in_specs=[HBM] * (2 * m) + [SEM, SEM] + [ANY] * len(after),
        out_specs=[SEM] * nsem + [HBM] * (2 * m) + [VMEM_SPEC],
        input_output_aliases={i: nsem + i for i in range(2 * m)},
        compiler_params=pltpu.CompilerParams(has_side_effects=EFFECT),
    )(*shards, *lands, send, recv, *after)
    fwd = (out[0], out[1]) if hops else None
    return fwd, list(out[nsem:nsem + m]), list(out[nsem + m:nsem + 2 * m]), out[-1]


def _gather_finish(lands, ks, fwd, after, name):
    m = len(lands)
    hops = [k for k in ks if k]

    def body(*refs):
        lnd = refs[:m]
        fsend_ref, frecv_ref = refs[m], refs[m + 1]
        x, y, c = _position()
        for slot in range(m):
            for fi, k in enumerate(hops):
                px, py = _flip(x, y, k)
                sent = lnd[slot].at[4 * px + 2 * py + c]
                came = lnd[slot].at[4 * px + 2 * py + (1 - c)]
                fat = len(hops) * slot + fi
                cp = _remote(sent, came, fsend_ref.at[fat], frecv_ref.at[fat], (x, y, c))
                cp.wait_recv()
                cp.wait_send()

    out = pl.pallas_call(
        body, name=name,
        out_shape=_hbm_like(lands),
        in_specs=[HBM] * m + [SEM, SEM] + [ANY] * len(after), out_specs=[HBM] * m,
        input_output_aliases={i: i for i in range(m)},
        compiler_params=pltpu.CompilerParams(has_side_effects=EFFECT),
    )(*lands, fwd[0], fwd[1], *after)
    return list(out)


def _pair_start(parts, name):
    n = len(parts)
    lands = [lax.empty((p.shape[0], 1) + p.shape[2:], p.dtype) for p in parts]

    def body(*refs):
        ins, lnd = refs[:n], refs[n:2 * n]
        send_ref, recv_ref = refs[2 * n], refs[2 * n + 1]
        token = refs[-1]
        x, y, c = _position()
        for i in range(n):
            _remote(ins[i].at[:, pl.ds(1 - c, 1)], lnd[i], send_ref.at[i], recv_ref.at[i], (x, y, 1 - c)).start()
        token[...] = jnp.zeros_like(token)

    out = pl.pallas_call(
        body, name=name,
        out_shape=[pltpu.SemaphoreType.DMA((n,)), pltpu.SemaphoreType.DMA((n,))]
        + _hbm_like(parts) + _hbm_like(lands) + [TOKEN],
        in_specs=[HBM] * (2 * n), out_specs=[SEM, SEM] + [HBM] * (2 * n) + [VMEM_SPEC],
        input_output_aliases={i: 2 + i for i in range(2 * n)},
        compiler_params=pltpu.CompilerParams(has_side_effects=EFFECT),
    )(*[_hbm(p) for p in parts], *[_hbm(l) for l in lands])
    return out[0], out[1], list(out[2:2 + n]), list(out[2 + n:2 + 2 * n]), out[-1]


def _pair_wait(parts, lands, send, recv, after, name):
    n = len(parts)

    def body(*refs):
        ins, lnd = refs[:n], refs[n:2 * n]
        send_ref, recv_ref = refs[2 * n], refs[2 * n + 1]
        x, y, c = _position()
        for i in range(n):
            cp = _remote(ins[i].at[:, pl.ds(1 - c, 1)], lnd[i], send_ref.at[i], recv_ref.at[i], (x, y, c))
            cp.wait_recv()
            cp.wait_send()

    out = pl.pallas_call(
        body, name=name,
        out_shape=_hbm_like(parts) + _hbm_like(lands),
        in_specs=[HBM] * (2 * n) + [SEM, SEM] + [ANY] * len(after), out_specs=[HBM] * (2 * n),
        input_output_aliases={i: i for i in range(2 * n)},
        compiler_params=pltpu.CompilerParams(has_side_effects=EFFECT),
    )(*parts, *lands, send, recv, *after)
    return list(out[:n]), list(out[n:])


def _chip_start(sums, lands, name, flips=(1, 2, 3)):
    n, ns = len(sums), len(flips)

    def body(*refs):
        ins, lnd = refs[:n], refs[n:2 * n]
        send_ref, recv_ref = refs[2 * n], refs[2 * n + 1]
        token = refs[-1]
        x, y, c = _position()
        for i in range(n):
            for j, flip in enumerate(flips):
                px, py = _flip(x, y, flip)
                _remote(ins[i].at[j], lnd[i].at[j], send_ref.at[ns * i + j], recv_ref.at[ns * i + j],
                        (px, py, c)).start()
        token[...] = jnp.zeros_like(token)

    out = pl.pallas_call(
        body, name=name,
        out_shape=[pltpu.SemaphoreType.DMA((ns * n,)), pltpu.SemaphoreType.DMA((ns * n,))]
        + _hbm_like(sums) + _hbm_like(lands) + [TOKEN],
        in_specs=[HBM] * (2 * n), out_specs=[SEM, SEM] + [HBM] * (2 * n) + [VMEM_SPEC],
        input_output_aliases={i: 2 + i for i in range(2 * n)},
        compiler_params=pltpu.CompilerParams(has_side_effects=EFFECT),
    )(*[_hbm(s) for s in sums], *[_hbm(l) for l in lands])
    return out[0], out[1], out[2:2 + n], out[2 + n:2 + 2 * n], out[-1]


def _chip_wait(sums, lands, send, recv, after, name):
    n, ns = len(sums), sums[0].shape[0]

    def body(*refs):
        ins, lnd = refs[:n], refs[n:2 * n]
        send_ref, recv_ref = refs[2 * n], refs[2 * n + 1]
        x, y, c = _position()
        for i in range(n):
            for j in range(ns):
                cp = _remote(ins[i].at[j], lnd[i].at[j], send_ref.at[ns * i + j], recv_ref.at[ns * i + j], (x, y, c))
                cp.wait_recv()
                cp.wait_send()

    out = pl.pallas_call(
        body, name=name,
        out_shape=_hbm_like(sums) + _hbm_like(lands),
        in_specs=[HBM] * (2 * n) + [SEM, SEM] + [ANY] * len(after), out_specs=[HBM] * (2 * n),
        input_output_aliases={i: i for i in range(2 * n)},
        compiler_params=pltpu.CompilerParams(has_side_effects=EFFECT),
    )(*sums, *lands, send, recv, *after)
    return list(out[n:])


def _pair_sum(idx, part, got, name, nslots=3):
    _, _, rows, cols = part.shape
    tr = _tile(rows, 256)

    def body(idx_ref, p_ref, q_ref, o_ref):
        o_ref[...] = (p_ref[...].astype(F32) + q_ref[...].astype(F32)).astype(o_ref.dtype)

    grid_spec = pltpu.PrefetchScalarGridSpec(
        num_scalar_prefetch=1, grid=(nslots, rows // tr),
        in_specs=[pl.BlockSpec((None, None, tr, cols), lambda j, r, idx: (idx[j], idx[4], r, 0)),
                  pl.BlockSpec((None, None, tr, cols), lambda j, r, idx: (idx[j], 0, r, 0))],
        out_specs=pl.BlockSpec((None, tr, cols), lambda j, r, idx: (j, r, 0)))
    return pl.pallas_call(
        body, name=name, grid_spec=grid_spec,
        out_shape=jax.ShapeDtypeStruct((nslots, rows, cols), part.dtype),
        compiler_params=_params("arbitrary", "arbitrary"),
    )(idx, part, got)


def _mm_proj(h, wg, groups, name):
    s, k = h.shape
    nchunk, _, n = wg.shape
    e = nchunk * n // groups
    tn = _tile(min(n, e), 512)

    def body(h_ref, w_ref, o_ref):
        o_ref[...] = jnp.dot(h_ref[...], w_ref[...], preferred_element_type=F32).astype(o_ref.dtype)

    return pl.pallas_call(
        body, name=name, grid=(nchunk * n // tn,),
        in_specs=[pl.BlockSpec((s, k), lambda j: (0, 0)),
                  pl.BlockSpec((None, k, tn), lambda j: ((j * tn) // n, 0, ((j * tn) % n) // tn))],
        out_specs=pl.BlockSpec((None, s, tn), lambda j: ((j * tn) // e, 0, ((j * tn) % e) // tn)),
        out_shape=jax.ShapeDtypeStruct((groups, s, e), BF16),
        compiler_params=_params("arbitrary"),
    )(h, wg)


def _mm_proj_group(h, wg, idx, pos, prev, name, deps=()):
    s, k = h.shape
    _, _, n = wg.shape
    _, _, e = prev.shape
    tn = _tile(n, 512)
    nd = len(deps)

    def body(idx_ref, h_ref, w_ref, prev_ref, *rest):
        o_ref = rest[nd]
        o_ref[...] = jnp.dot(h_ref[...], w_ref[...], preferred_element_type=F32).astype(o_ref.dtype)

    def col(j, idx):
        return idx[pos] * (2 * n) + j * tn

    grid_spec = pltpu.PrefetchScalarGridSpec(
        num_scalar_prefetch=1, grid=(2 * n // tn,),
        in_specs=[pl.BlockSpec((s, k), lambda j, idx: (0, 0)),
                  pl.BlockSpec((None, k, tn), lambda j, idx: (col(j, idx) // n, 0, (col(j, idx) % n) // tn)),
                  ANY] + [ANY] * nd,
        out_specs=pl.BlockSpec((None, s, tn), lambda j, idx: (col(j, idx) // e, 0, (col(j, idx) % e) // tn)))
    return pl.pallas_call(
        body, name=name, grid_spec=grid_spec,
        out_shape=jax.ShapeDtypeStruct(prev.shape, prev.dtype),
        input_output_aliases={3: 0},
        compiler_params=_params("arbitrary"),
    )(idx, h, wg, prev, *deps)


def _mm_out(yb, w, x, gate, name):
    s, k = yb.shape
    d = w.shape[1]
    tn = _tile(d, 512)
    tk = _tile(k, 2048)
    nk = k // tk

    def body(a_ref, w_ref, x_ref, g_ref, xo_ref, y_ref, acc_ref):
        kk = pl.program_id(1)

        @pl.when(kk == 0)
        def _():
            acc_ref[...] = jnp.zeros_like(acc_ref)

        acc_ref[...] += jnp.dot(a_ref[...], w_ref[...], preferred_element_type=F32)

        @pl.when(kk == nk - 1)
        def _():
            y = acc_ref[...]
            y_ref[...] = y.astype(y_ref.dtype)
            xo_ref[...] = x_ref[...] + g_ref[...] * y

    return pl.pallas_call(
        body, name=name, grid=(d // tn, nk),
        in_specs=[pl.BlockSpec((s, tk), lambda j, kk: (0, kk)),
                  pl.BlockSpec((tk, tn), lambda j, kk: (kk, j)),
                  pl.BlockSpec((s, tn), lambda j, kk: (0, j)),
                  pl.BlockSpec((1, tn), lambda j, kk: (0, j))],
        out_specs=[pl.BlockSpec((s, tn), lambda j, kk: (0, j)),
                   pl.BlockSpec((s, tn), lambda j, kk: (0, j))],
        out_shape=[jax.ShapeDtypeStruct((s, d), F32), jax.ShapeDtypeStruct((s, d), BF16)],
        scratch_shapes=[pltpu.VMEM((s, tn), F32)],
        compiler_params=_params("arbitrary", "arbitrary"),
    )(yb, w, x, gate)


def _mm_nt(a3, w3, out_dtype, name, deps=()):
    g, s, ea = a3.shape
    cw, n, nw = w3.shape
    total = g * ea
    assert total == cw * nw
    tk = _tile(min(ea, nw), 2048)
    tn = _tile(n, 1024)
    nk = total // tk

    def body(a_ref, w_ref, o_ref, acc_ref):
        kk = pl.program_id(1)

        @pl.when(kk == 0)
        def _():
            acc_ref[...] = jnp.zeros_like(acc_ref)

        acc_ref[...] += lax.dot_general(a_ref[...], w_ref[...], (((1,), (1,)), ((), ())),
                                        preferred_element_type=F32)

        @pl.when(kk == nk - 1)
        def _():
            o_ref[...] = acc_ref[...].astype(o_ref.dtype)

    return pl.pallas_call(
        _after(body, 2, deps), name=name, grid=(n // tn, nk),
        in_specs=[pl.BlockSpec((None, s, tk), lambda j, kk: ((kk * tk) // ea, 0, ((kk * tk) % ea) // tk)),
                  pl.BlockSpec((None, tn, tk), lambda j, kk: ((kk * tk) // nw, j, ((kk * tk) % nw) // tk))]
        + [ANY] * len(deps),
        out_specs=pl.BlockSpec((s, tn), lambda j, kk: (0, j)),
        out_shape=jax.ShapeDtypeStruct((s, n), out_dtype),
        scratch_shapes=[pltpu.VMEM((s, tn), F32)],
        compiler_params=_params("arbitrary", "arbitrary"),
    )(a3, w3, *deps)


def _mm_tn(a, b3, nchunk, name, deps=()):
    s, ka = a.shape
    g, _, eb = b3.shape
    n = g * eb // nchunk
    tm = _tile(ka, 1024)
    tn = _tile(min(n, eb), 1024)

    def body(a_ref, b_ref, o_ref, at_ref):
        @pl.when(pl.program_id(1) == 0)
        def _():
            at_ref[...] = a_ref[...].astype(F32).T.astype(at_ref.dtype)

        o_ref[...] = jnp.dot(at_ref[...], b_ref[...], preferred_element_type=F32).astype(o_ref.dtype)

    return pl.pallas_call(
        _after(body, 2, deps), name=name, grid=(ka // tm, g * eb // tn),
        in_specs=[pl.BlockSpec((s, tm), lambda i, j: (0, i)),
                  pl.BlockSpec((None, s, tn), lambda i, j: ((j * tn) // eb, 0, ((j * tn) % eb) // tn))]
        + [ANY] * len(deps),
        out_specs=pl.BlockSpec((None, tm, tn), lambda i, j: ((j * tn) // n, i, ((j * tn) % n) // tn)),
        out_shape=jax.ShapeDtypeStruct((nchunk, ka, n), BF16),
        scratch_shapes=[pltpu.VMEM((tm, s), BF16)],
        compiler_params=_params("arbitrary", "arbitrary"),
    )(a, b3, *deps)


def _mm_tn_group(a, b3, idx, pos, nchunk, name, deps=()):
    s, ka = a.shape
    _, _, eb = b3.shape
    n = eb // nchunk
    tm = _tile(ka, 1024)
    tn = _tile(n, 1024)
    nd = len(deps)

    def body(idx_ref, a_ref, b_ref, *rest):
        o_ref, at_ref = rest[nd:]

        @pl.when(pl.program_id(1) == 0)
        def _():
            at_ref[...] = a_ref[...].astype(F32).T.astype(at_ref.dtype)

        o_ref[...] = jnp.dot(at_ref[...], b_ref[...], preferred_element_type=F32).astype(o_ref.dtype)

    grid_spec = pltpu.PrefetchScalarGridSpec(
        num_scalar_prefetch=1, grid=(ka // tm, eb // tn),
        in_specs=[pl.BlockSpec((s, tm), lambda i, j, idx: (0, i)),
                  pl.BlockSpec((None, s, tn), lambda i, j, idx: (idx[pos], 0, j))] + [ANY] * nd,
        out_specs=pl.BlockSpec((None, tm, tn), lambda i, j, idx: ((j * tn) // n, i, ((j * tn) % n) // tn)),
        scratch_shapes=[pltpu.VMEM((tm, s), BF16)])
    return pl.pallas_call(
        body, name=name, grid_spec=grid_spec,
        out_shape=jax.ShapeDtypeStruct((nchunk, ka, n), BF16),
        compiler_params=_params("arbitrary", "arbitrary"),
    )(idx, a, b3, *deps)


def _sigmoid(z):
    return jax.nn.sigmoid(z)


def _shift_down(v, k, fill=0.0, period=None):
    if k == 0:
        return v
    row = lax.broadcasted_iota(jnp.int32, v.shape, 0)
    if period is not None:
        row = row & (period - 1)
    return jnp.where(row >= k, pltpu.roll(v, k, 0), fill)


def _shift_up(v, k, fill=0.0, period=None):
    if k == 0:
        return v
    s = v.shape[0]
    row = lax.broadcasted_iota(jnp.int32, v.shape, 0)
    if period is not None:
        row, s = row & (period - 1), period
    return jnp.where(row < s - k, pltpu.roll(v, v.shape[0] - k, 0), fill)


SCAN_BLOCK = 64


def _scan(a, b, shift):
    s = a.shape[0]
    blk = min(SCAN_BLOCK, s)
    k = 1
    while k < blk:
        b = a * shift(b, k, 0.0, blk) + b
        a = a * shift(a, k, 1.0, blk)
        k *= 2
    nblk = s // blk
    forward = shift is _shift_down
    order = range(nblk) if forward else range(nblk - 1, -1, -1)
    edge = blk - 1 if forward else 0
    out = [None] * nblk
    carry = None
    for i in order:
        h = b[i * blk:(i + 1) * blk]
        if carry is not None:
            h = a[i * blk:(i + 1) * blk] * carry + h
        carry = h[edge:edge + 1]
        out[i] = h
    return jnp.concatenate(out, axis=0) if nblk > 1 else out[0]


def _norm_mod(x, g, scale, shift, name, deps=()):
    s, d = x.shape
    ts = _tile(s, 256)

    def body(x_ref, g_ref, sc_ref, sh_ref, h_ref):
        xv = x_ref[...]
        rstd = lax.rsqrt(jnp.mean(xv * xv, axis=-1, keepdims=True) + EPS)
        nrm = xv * rstd * g_ref[...]
        h_ref[...] = (nrm * (1.0 + sc_ref[...]) + sh_ref[...]).astype(h_ref.dtype)

    vec = pl.BlockSpec((1, d), lambda i: (0, 0))
    return pl.pallas_call(
        _after(body, 4, deps), name=name, grid=(s // ts,),
        in_specs=[pl.BlockSpec((ts, d), lambda i: (i, 0)), vec, vec, vec] + [ANY] * len(deps),
        out_specs=pl.BlockSpec((ts, d), lambda i: (i, 0)),
        out_shape=jax.ShapeDtypeStruct((s, d), BF16),
        compiler_params=_params("arbitrary"),
    )(x, g, scale, shift, *deps)


def _gate_terms(dx, y_ref, gate_ref, dy_ref, dgate_ref):
    dy_ref[...] = (dx * gate_ref[...]).astype(dy_ref.dtype)
    dgate_ref[...] += jnp.sum(dx * y_ref[...].astype(F32), axis=0, keepdims=True)


def _norm_mod_bwd(x, dh, dx_res, g, scale, name, below=None, deps=()):
    s, d = x.shape
    ts = _tile(s, 256)
    nb = 2 if below is not None else 0

    def body(x_ref, dh_ref, dr_ref, g_ref, sc_ref, *rest):
        dx_ref, dsc_ref, dsh_ref, dg_ref = rest[nb:nb + 4]

        @pl.when(pl.program_id(0) == 0)
        def _():
            for ref in rest[nb + 1:nb + 4] + rest[nb + 5:]:
                ref[...] = jnp.zeros_like(ref)

        xv = x_ref[...]
        dh_v = dh_ref[...].astype(F32)
        gv = g_ref[...]
        rstd = lax.rsqrt(jnp.mean(xv * xv, axis=-1, keepdims=True) + EPS)
        xhat = xv * rstd
        dsc_ref[...] += jnp.sum(dh_v * xhat * gv, axis=0, keepdims=True)
        dsh_ref[...] += jnp.sum(dh_v, axis=0, keepdims=True)
        dn = dh_v * (1.0 + sc_ref[...])
        dg_ref[...] += jnp.sum(dn * xhat, axis=0, keepdims=True)
        dxhat = dn * gv
        proj = jnp.mean(dxhat * xhat, axis=-1, keepdims=True)
        dx = dr_ref[...] + rstd * (dxhat - xhat * proj)
        dx_ref[...] = dx
        if nb:
            _gate_terms(dx, rest[0], rest[1], rest[nb + 4], rest[nb + 5])

    row = pl.BlockSpec((ts, d), lambda i: (i, 0))
    vec = pl.BlockSpec((1, d), lambda i: (0, 0))
    extra = list(below) if nb else []
    return pl.pallas_call(
        _after(body, 5 + nb, deps), name=name, grid=(s // ts,),
        in_specs=[row, row, row, vec, vec] + [row, vec][:nb] + [ANY] * len(deps),
        out_specs=[row, vec, vec, vec] + [row, vec][:nb],
        out_shape=[jax.ShapeDtypeStruct((s, d), F32)] + [jax.ShapeDtypeStruct((1, d), F32)] * 3
        + [jax.ShapeDtypeStruct((s, d), BF16), jax.ShapeDtypeStruct((1, d), F32)][:nb],
        compiler_params=_params("arbitrary"),
    )(x, dh, dx_res, g, scale, *extra, *deps)


def _final_loss(x, g, target, y, gate, name):
    s, d = x.shape
    ts = _tile(s, 256)

    def body(x_ref, g_ref, t_ref, y_ref, gate_ref, dx_ref, loss_ref, dg_ref, dy_ref, dgate_ref):
        @pl.when(pl.program_id(0) == 0)
        def _():
            loss_ref[...] = jnp.zeros_like(loss_ref)
            dg_ref[...] = jnp.zeros_like(dg_ref)
            dgate_ref[...] = jnp.zeros_like(dgate_ref)

        xv = x_ref[...]
        gv = g_ref[...]
        rstd = lax.rsqrt(jnp.mean(xv * xv, axis=-1, keepdims=True) + EPS)
        xhat = xv * rstd
        err = xhat * gv - t_ref[...]
        loss_ref[...] += 0.5 * jnp.sum(jnp.mean(err * err, axis=-1, keepdims=True))
        dy = err * (1.0 / d)
        dg_ref[...] += jnp.sum(dy * xhat, axis=0, keepdims=True)
        dxhat = dy * gv
        proj = jnp.mean(dxhat * xhat, axis=-1, keepdims=True)
        dx = rstd * (dxhat - xhat * proj)
        dx_ref[...] = dx
        _gate_terms(dx, y_ref, gate_ref, dy_ref, dgate_ref)

    row = pl.BlockSpec((ts, d), lambda i: (i, 0))
    vec = pl.BlockSpec((1, d), lambda i: (0, 0))
    return pl.pallas_call(
        body, name=name, grid=(s // ts,),
        in_specs=[row, vec, row, row, vec],
        out_specs=[row, pl.BlockSpec((1, LANES), lambda i: (0, 0)), vec, row, vec],
        out_shape=[jax.ShapeDtypeStruct((s, d), F32), jax.ShapeDtypeStruct((1, LANES), F32),
                   jax.ShapeDtypeStruct((1, d), F32), jax.ShapeDtypeStruct((s, d), BF16),
                   jax.ShapeDtypeStruct((1, d), F32)],
        compiler_params=_params("arbitrary"),
    )(x, g, target, y, gate)


def _conv(v, w_ref, width):
    out = w_ref[width - 1:width, :] * v
    for k in range(width - 1):
        out = out + w_ref[k:k + 1, :] * _shift_down(v, width - 1 - k)
    return out


def _sc_fwd(proj, conv_w, name, deps=()):
    _, s, e = proj.shape
    te = _tile(e, 256)
    width = conv_w.shape[0]

    def body(b_ref, c_ref, v_ref, g_ref, w_ref, o_ref):
        cv = c_ref[...].astype(F32) * v_ref[...].astype(F32)
        u = _conv(cv, w_ref, width)
        gv = g_ref[...].astype(F32)
        o_ref[...] = (b_ref[...].astype(F32) * u * (gv * _sigmoid(gv))).astype(o_ref.dtype)

    def part(q):
        return pl.BlockSpec((None, s, te), lambda j, q=q: (q, 0, j))

    return pl.pallas_call(
        _after(body, 5, deps), name=name, grid=(e // te,),
        in_specs=[part(0), part(1), part(2), part(3), pl.BlockSpec((width, te), lambda j: (0, j))]
        + [ANY] * len(deps),
        out_specs=pl.BlockSpec((s, te), lambda j: (0, j)),
        out_shape=jax.ShapeDtypeStruct((s, e), BF16),
        compiler_params=_params("arbitrary"),
    )(proj, proj, proj, proj, conv_w, *deps)


def _sc_bwd(proj, dyb, conv_w, name, deps=()):
    _, s, e = proj.shape
    te = _tile(e, 256)
    width = conv_w.shape[0]

    def body(b_ref, c_ref, v_ref, g_ref, dy_ref, w_ref, dp_ref, vec_ref):
        bv = b_ref[...].astype(F32)
        cvl = c_ref[...].astype(F32)
        vv = v_ref[...].astype(F32)
        gv = g_ref[...].astype(F32)
        dyv = dy_ref[...].astype(F32)
        cv = cvl * vv
        u = _conv(cv, w_ref, width)
        sg = _sigmoid(gv)
        silu = gv * sg
        dp_ref[0] = (dyv * u * silu).astype(dp_ref.dtype)
        du = dyv * bv * silu
        dp_ref[3] = (dyv * bv * u * (sg * (1.0 + gv * (1.0 - sg)))).astype(dp_ref.dtype)
        dcv = w_ref[width - 1:width, :] * du
        vec_ref[...] = jnp.zeros_like(vec_ref)
        vec_ref[width - 1:width, :] = jnp.sum(du * cv, axis=0, keepdims=True)
        for k in range(width - 1):
            sh = width - 1 - k
            dcv = dcv + w_ref[k:k + 1, :] * _shift_up(du, sh)
            vec_ref[k:k + 1, :] = jnp.sum(du * _shift_down(cv, sh), axis=0, keepdims=True)
        dp_ref[1] = (dcv * vv).astype(dp_ref.dtype)
        dp_ref[2] = (dcv * cvl).astype(dp_ref.dtype)

    def part(q):
        return pl.BlockSpec((None, s, te), lambda j, q=q: (q, 0, j))

    return pl.pallas_call(
        _after(body, 6, deps), name=name, grid=(e // te,),
        in_specs=[part(0), part(1), part(2), part(3), pl.BlockSpec((s, te), lambda j: (0, j)),
                  pl.BlockSpec((width, te), lambda j: (0, j))] + [ANY] * len(deps),
        out_specs=[pl.BlockSpec((4, s, te), lambda j: (0, 0, j)),
                   pl.BlockSpec((8, te), lambda j: (0, j))],
        out_shape=[jax.ShapeDtypeStruct((4, s, e), BF16), jax.ShapeDtypeStruct((8, e), F32)],
        compiler_params=_params("arbitrary"),
    )(proj, proj, proj, proj, dyb, conv_w, *deps)


def _lru_gates(v_pre, w_ref, cb_ref, wa_ref, ba_ref, wx_ref, bx_ref, lam_ref, width):
    v = _conv(v_pre, w_ref, width) + cb_ref[...]
    vb = v.astype(BF16)
    r = _sigmoid(jnp.dot(vb, wa_ref[...], preferred_element_type=F32) + ba_ref[...])
    i = _sigmoid(jnp.dot(vb, wx_ref[...], preferred_element_type=F32) + bx_ref[...])
    nl = -lam_ref[...]
    sp = jnp.maximum(nl, 0.0) + jnp.log1p(jnp.exp(-jnp.abs(nl)))
    log_a = (-RGLRU_C) * r * sp
    a = jnp.exp(log_a)
    one_minus_a2 = jnp.tanh(-log_a) * (1.0 + a * a)
    mult = jnp.sqrt(one_minus_a2)
    return v, vb, r, i, sp, a, mult


def _lru_specs(s, dh, heads, width):
    head_col = lambda q: pl.BlockSpec((None, s, dh), lambda h, q=q: (q, 0, h))
    vec = pl.BlockSpec((1, dh), lambda h: (0, h))
    mat = pl.BlockSpec((None, dh, dh), lambda h: (h, 0, 0))
    weights = [pl.BlockSpec((width, dh), lambda h: (0, h)), vec, mat, vec, mat, vec, vec]
    return head_col, weights


def _lru_fwd(proj, conv_w, conv_b, w_a, b_a, w_x, b_x, lam, name, deps=()):
    _, s, e = proj.shape
    heads, dh, _ = w_a.shape
    width = conv_w.shape[0]

    def body(v_ref, g_ref, w_ref, cb_ref, wa_ref, ba_ref, wx_ref, bx_ref, lam_ref, yb_ref, keep_ref):
        v, _, r, i, _, a, mult = _lru_gates(v_ref[...].astype(F32), w_ref, cb_ref, wa_ref, ba_ref,
                                           wx_ref, bx_ref, lam_ref, width)
        hs = _scan(a, mult * i * v, _shift_down)
        for k, val in enumerate((hs, v, r, i, a, mult)):
            keep_ref[k] = val
        gv = g_ref[...].astype(F32)
        yb_ref[...] = (hs * (gv * _sigmoid(gv))).astype(yb_ref.dtype)

    head_col, weights = _lru_specs(s, dh, heads, width)
    return pl.pallas_call(
        _after(body, 9, deps), name=name, grid=(heads,),
        in_specs=[head_col(0), head_col(1)] + weights + [ANY] * len(deps),
        out_specs=[pl.BlockSpec((s, dh), lambda h: (0, h)), pl.BlockSpec((6, s, dh), lambda h: (0, 0, h))],
        out_shape=[jax.ShapeDtypeStruct((s, e), BF16), jax.ShapeDtypeStruct((6, s, e), F32)],
        compiler_params=_params("arbitrary"),
    )(proj, proj, conv_w, conv_b, w_a, b_a, w_x, b_x, lam, *deps)


def _lru_bwd(proj, keep, dyb, conv_w, conv_b, w_a, b_a, w_x, b_x, lam, name, deps=()):
    _, s, e = proj.shape
    heads, dh, _ = w_a.shape
    width = conv_w.shape[0]

    def body(v_ref, g_ref, hs_ref, dy_ref, w_ref, cb_ref, wa_ref, ba_ref, wx_ref, bx_ref, lam_ref,
             dp_ref, dwa_ref, dwx_ref, vec_ref):
        v_pre = v_ref[...].astype(F32)
        hs, v, r, i, a, mult = (hs_ref[k] for k in range(6))
        vb = v.astype(BF16)
        nl = -lam_ref[...]
        sp = jnp.maximum(nl, 0.0) + jnp.log1p(jnp.exp(-jnp.abs(nl)))
        gv = g_ref[...].astype(F32)
        dyv = dy_ref[...].astype(F32)
        sg = _sigmoid(gv)
        dp_ref[1] = (dyv * hs * (sg * (1.0 + gv * (1.0 - sg)))).astype(dp_ref.dtype)
        dhs = dyv * (gv * sg)
        d_h = _scan(_shift_up(a, 1), dhs, _shift_up)
        da = d_h * _shift_down(hs, 1)
        iv = i * v
        dlog_a = da * a - (d_h * iv) * (a * a) / mult
        di = d_h * mult * v
        dv = d_h * mult * i
        dzr = dlog_a * (-RGLRU_C) * sp * r * (1.0 - r)
        dzi = di * i * (1.0 - i)
        dsp = jnp.sum(dlog_a * r, axis=0, keepdims=True) * (-RGLRU_C)
        vec_ref[...] = jnp.zeros_like(vec_ref)
        vec_ref[0:1, :] = jnp.sum(dzr, axis=0, keepdims=True)
        vec_ref[1:2, :] = jnp.sum(dzi, axis=0, keepdims=True)
        vec_ref[2:3, :] = -dsp * _sigmoid(-lam_ref[...])
        dzr_b = dzr.astype(BF16)
        dzi_b = dzi.astype(BF16)
        vt = vb.astype(F32).T.astype(BF16)
        dwa_ref[...] = jnp.dot(vt, dzr_b, preferred_element_type=F32).astype(dwa_ref.dtype)
        dwx_ref[...] = jnp.dot(vt, dzi_b, preferred_element_type=F32).astype(dwx_ref.dtype)
        nt = (((1,), (1,)), ((), ()))
        dv = dv + lax.dot_general(dzr_b, wa_ref[...], nt, preferred_element_type=F32)
        dv = dv + lax.dot_general(dzi_b, wx_ref[...], nt, preferred_element_type=F32)
        vec_ref[3:4, :] = jnp.sum(dv, axis=0, keepdims=True)
        dvp = w_ref[width - 1:width, :] * dv
        vec_ref[4 + width - 1:4 + width, :] = jnp.sum(dv * v_pre, axis=0, keepdims=True)
        for k in range(width - 1):
            sh = width - 1 - k
            dvp = dvp + w_ref[k:k + 1, :] * _shift_up(dv, sh)
            vec_ref[4 + k:5 + k, :] = jnp.sum(dv * _shift_down(v_pre, sh), axis=0, keepdims=True)
        dp_ref[0] = dvp.astype(dp_ref.dtype)

    head_col, weights = _lru_specs(s, dh, heads, width)
    col = pl.BlockSpec((s, dh), lambda h: (0, h))
    mat = pl.BlockSpec((None, dh, dh), lambda h: (h, 0, 0))
    return pl.pallas_call(
        _after(body, 11, deps), name=name, grid=(heads,),
        in_specs=[head_col(0), head_col(1), pl.BlockSpec((6, s, dh), lambda h: (0, 0, h)), col] + weights
        + [ANY] * len(deps),
        out_specs=[pl.BlockSpec((2, s, dh), lambda h: (0, 0, h)), mat, mat,
                   pl.BlockSpec((16, dh), lambda h: (0, h))],
        out_shape=[jax.ShapeDtypeStruct((2, s, e), BF16),
                   jax.ShapeDtypeStruct((heads, dh, dh), BF16),
                   jax.ShapeDtypeStruct((heads, dh, dh), BF16),
                   jax.ShapeDtypeStruct((16, e), F32)],
        compiler_params=_params("arbitrary"),
    )(proj, proj, keep, dyb, conv_w, conv_b, w_a, b_a, w_x, b_x, lam, *deps)


def _ada_mod(c_all, w, b, name):
    layers, d, f = w.shape
    nb = c_all.shape[0]

    def body(c_ref, w_ref, b_ref, o_ref):
        cv = c_ref[...]
        sc = cv * _sigmoid(cv)
        o_ref[...] = jnp.dot(sc, w_ref[...], preferred_element_type=F32,
                             precision=lax.Precision.HIGHEST) + b_ref[...]

    return pl.pallas_call(
        body, name=name, grid=(layers,),
        in_specs=[pl.BlockSpec((nb, d), lambda l: (0, 0)),
                  pl.BlockSpec((None, d, f), lambda l: (l, 0, 0)),
                  pl.BlockSpec((None, 1, f), lambda l: (l, 0, 0))],
        out_specs=pl.BlockSpec((None, nb, f), lambda l: (l, 0, 0)),
        out_shape=jax.ShapeDtypeStruct((layers, nb, f), F32),
        compiler_params=_params("arbitrary"),
    )(c_all, w, b)


def _ada_update(c_all_t, dmod, w, m, v, name):
    d, nb = c_all_t.shape
    layers, _, f = dmod.shape
    tr = _tile(d, 512)

    def body(c_ref, dm_ref, w_ref, m_ref, v_ref, g_ref, d_ref, mo_ref, vo_ref):
        cv = c_ref[...]
        sc = cv * _sigmoid(cv)
        g = sc[:, 0:1] * dm_ref[0:1, :]
        for k in range(1, nb):
            g = g + sc[:, k:k + 1] * dm_ref[k:k + 1, :]
        g_ref[...] = g
        d_ref[...], mo_ref[...], vo_ref[...] = _adamw_math(w_ref[...], g, m_ref[...], v_ref[...])

    blk = pl.BlockSpec((None, tr, f), lambda l, i: (l, i, 0))
    return pl.pallas_call(
        body, name=name, grid=(layers, d // tr),
        in_specs=[pl.BlockSpec((tr, nb), lambda l, i: (i, 0)),
                  pl.BlockSpec((None, nb, f), lambda l, i: (l, 0, 0)), blk, blk, blk],
        out_specs=[blk] * 4,
        out_shape=[jax.ShapeDtypeStruct((layers, d, f), F32)] * 4,
        compiler_params=_params("arbitrary", "arbitrary"),
    )(c_all_t, dmod, w, m, v)


def _device_sum(g, name):
    _, rows, _ = g.shape

    def body(g_ref, o_ref):
        acc = g_ref[0]
        for k in range(1, N_DEV):
            acc = acc + g_ref[k]
        o_ref[...] = acc

    return pl.pallas_call(
        body, name=name,
        in_specs=[VMEM_SPEC], out_specs=VMEM_SPEC,
        out_shape=jax.ShapeDtypeStruct((rows, LANES), F32),
        compiler_params=pltpu.CompilerParams(vmem_limit_bytes=VMEM_LIMIT),
    )(g)


def _adamw_math(w, g, m, v):
    m = ADAM_B1 * m + (1.0 - ADAM_B1) * g
    v = ADAM_B2 * v + (1.0 - ADAM_B2) * (g * g)
    m_hat = m / (1.0 - ADAM_B1 ** ADAM_STEP)
    v_hat = v / (1.0 - ADAM_B2 ** ADAM_STEP)
    delta = -ADAM_LR * (m_hat / (jnp.sqrt(v_hat) + ADAM_EPS) + ADAM_WD * w)
    return delta, m, v


def _adamw(w, g, m, v, name):
    rows, cols = w.shape
    tr = _tile(rows, 256)

    def body(w_ref, g_ref, m_ref, v_ref, d_ref, mo_ref, vo_ref):
        d_ref[...], mo_ref[...], vo_ref[...] = _adamw_math(w_ref[...], g_ref[...], m_ref[...], v_ref[...])

    blk = pl.BlockSpec((tr, cols), lambda i: (i, 0))
    return pl.pallas_call(
        body, name=name, grid=(rows // tr,),
        in_specs=[blk] * 4, out_specs=[blk] * 3,
        out_shape=[jax.ShapeDtypeStruct((rows, cols), F32)] * 3,
        compiler_params=_params("arbitrary"),
    )(w, g, m, v)


def _adamw_reduced(idx, w, m, v, part, got, recvs, name):
    rows, cols = w.shape
    tr = _tile(rows, 256)
    nr = len(recvs)

    def body(idx_ref, w_ref, m_ref, v_ref, p_ref, q_ref, *rest):
        g_ref, d_ref, mo_ref, vo_ref = rest[nr:]
        g = p_ref[...].astype(F32) + q_ref[...].astype(F32)
        for u_ref in rest[:nr]:
            for j in range(u_ref.shape[0]):
                g = g + u_ref[j].astype(F32)
        g_ref[...] = g
        d_ref[...], mo_ref[...], vo_ref[...] = _adamw_math(w_ref[...], g, m_ref[...], v_ref[...])

    blk = pl.BlockSpec((tr, cols), lambda i, idx: (i, 0))
    grid_spec = pltpu.PrefetchScalarGridSpec(
        num_scalar_prefetch=1, grid=(rows // tr,),
        in_specs=[blk, blk, blk,
                  pl.BlockSpec((None, None, tr, cols), lambda i, idx: (idx[3], idx[4], i, 0)),
                  pl.BlockSpec((None, None, tr, cols), lambda i, idx: (idx[3], 0, i, 0))]
        + [pl.BlockSpec((u.shape[0], tr, cols), lambda i, idx: (0, i, 0)) for u in recvs],
        out_specs=[blk] * 4)
    return pl.pallas_call(
        body, name=name, grid_spec=grid_spec,
        out_shape=[jax.ShapeDtypeStruct((rows, cols), F32)] * 4,
        compiler_params=_params("arbitrary"),
    )(idx, w, m, v, part, got, *recvs)


def _pack(vectors):
    flat = jnp.concatenate([v.reshape(-1).astype(F32) for v in vectors])
    pad = (-flat.shape[0]) % (8 * LANES)
    return jnp.pad(flat, (0, pad)).reshape(-1, LANES)


def _unpack(flat, shapes):
    out, off = [], 0
    for shp in shapes:
        size = math.prod(shp)
        out.append(flat[..., off:off + size].reshape(flat.shape[:-1] + tuple(shp)))
        off += size
    return out


def _my_slice(full, me, axis):
    size = full.shape[axis] // N_DEV
    return lax.dynamic_slice_in_dim(full, me * size, size, axis)


def kernel(x, c, norm_g, ada_w, ada_b, sc_w_in, sc_conv_w, sc_w_out, lru_w_in, lru_conv_w, lru_conv_b, lru_w_a, lru_b_a, lru_w_x, lru_b_x, lru_lambda, lru_w_out, final_g, loss_target, m_norm_g, m_ada_w, m_ada_b, m_sc_w_in, m_sc_conv_w, m_sc_w_out, m_lru_w_in, m_lru_conv_w, m_lru_conv_b, m_lru_w_a, m_lru_b_a, m_lru_w_x, m_lru_b_x, m_lru_lambda, m_lru_w_out, m_final_g, v_norm_g, v_ada_w, v_ada_b, v_sc_w_in, v_sc_conv_w, v_sc_w_out, v_lru_w_in, v_lru_conv_w, v_lru_conv_b, v_lru_w_a, v_lru_b_a, v_lru_w_x, v_lru_b_x, v_lru_lambda, v_lru_w_out, v_final_g):
    _, s, d = x.shape
    e = sc_w_out.shape[1] * N_DEV
    heads, dh_s, dh = lru_w_a.shape[1:]
    es = e // N_DEV
    f = ada_w.shape[2]
    mx, my, mc = _position()
    me = 4 * mx + 2 * my + mc
    chip = 2 * mx + my
    idx = jnp.stack([chip ^ 1, chip ^ 2, chip ^ 3, chip, mc]).astype(jnp.int32)

    x0 = x[0]
    target = loss_target[0]

    small_shapes = [(d,), (3, es), (4, es), (es,), (heads, dh_s), (heads, dh_s), (es,)]
    small = _small_gather(_pack([c, sc_conv_w, lru_conv_w, lru_conv_b, lru_b_a, lru_b_x, lru_lambda]),
                          "gather_small_weights").reshape(N_DEV, -1)
    c_all, cw3, cw4, cb, ba, bx, lam = _unpack(small, small_shapes)
    cw3 = cw3.transpose(1, 0, 2).reshape(3, e)
    cw4 = cw4.transpose(1, 0, 2).reshape(4, e)
    cb = cb.reshape(1, e)
    lam = lam.reshape(1, e)
    ba = ba.transpose(1, 0, 2).reshape(1, e)
    bx = bx.transpose(1, 0, 2).reshape(1, e)

    shards = [sc_w_in[0].astype(BF16), sc_w_out[0].astype(BF16), lru_w_in[0].astype(BF16),
              lru_w_a[0].reshape(heads * dh_s, dh).astype(BF16),
              lru_w_x[0].reshape(heads * dh_s, dh).astype(BF16), lru_w_out[0].astype(BF16)]
    lands = [lax.dynamic_update_slice(lax.empty((N_DEV,) + sh.shape, BF16), sh[None], (me, 0, 0))
             for sh in shards]
    every = [1, 2, 3, 0]
    units = [([0], [0]), ([0], [1]), ([0], [2]), ([0], [3]), ([1], every), ([2], every), ([3, 4], every),
             ([5], every)]
    sems, first_sh, first_ld, started = _gather_start(shards[:1], lands[:1], units[:3], [small],
                                                      "gather_start_first")
    shards, lands = first_sh + shards[1:], first_ld + lands[1:]

    ada_b_mine = _my_slice(ada_b, me, 1).reshape(2, 1, f)
    mod_mine = _ada_mod(c_all, ada_w, ada_b_mine, "ada_mod")
    mod_all = _small_gather(_pack([mod_mine]), "gather_mod", deps=[started])

    def start_later(after):
        far_sems, far_sh, far_ld, tok = _gather_start(shards[:1], lands[:1], units[3:4], after, "gather_start_far")
        rest_units = [([i - 1 for i in members], ks) for members, ks in units[4:]]
        rest_sems, rest_sh, rest_ld, tok = _gather_start(shards[1:], lands[1:], rest_units, [tok],
                                                         "gather_start_rest")
        sems.extend(far_sems + rest_sems)
        shards[:], lands[:] = far_sh + rest_sh, far_ld + rest_ld
        return tok

    def gathered(u, after_forward, name):
        members, ks = units[u]
        fwd, shs, lnd, token = _gather_forward(
            [shards[i] for i in members], [lands[i] for i in members], ks, sems[u][0], sems[u][1],
            after_forward, "gather_forward_" + name)
        for i, sh, ld in zip(members, shs, lnd):
            shards[i], lands[i] = sh, ld

        def finish(after):
            out = _gather_finish([lands[i] for i in members], ks, fwd, after, "gather_finish_" + name)
            for i, ld in zip(members, out):
                lands[i] = ld
            return out

        return token, finish

    tok, finish_y = gathered(1, [mod_all], "sc_w_in_near_y")
    tok, finish_x = gathered(2, [tok], "sc_w_in_near_x")
    queued = start_later([tok])

    mod_all = mod_all.reshape(N_DEV, -1)
    mod_all = mod_all[:, :2 * N_DEV * f].reshape(N_DEV, 2, N_DEV, f)
    mod_all = mod_all.transpose(1, 2, 0, 3).reshape(2, N_DEV, 3 * d)
    mod = lax.dynamic_index_in_dim(mod_all, me, 1, keepdims=False)
    shift = [mod[l:l + 1, 0:d] for l in range(2)]
    scale = [mod[l:l + 1, d:2 * d] for l in range(2)]
    gate = [mod[l:l + 1, 2 * d:3 * d] for l in range(2)]
    ng = [norm_g[l:l + 1] for l in range(2)]
    fg = final_g.reshape(1, d)

    h0 = _norm_mod(x0, ng[0], scale[0], shift[0], "norm_mod_0", deps=[queued])
    proj0 = lax.empty((4, s, e), BF16)
    tok, _ = gathered(0, [h0], "sc_w_in_own")
    proj0 = _mm_proj_group(h0, lands[0], idx, 3, proj0, "mm_proj_0_own", deps=[tok])
    for u, name, finish in ((1, "near_y", finish_y), (2, "near_x", finish_x), (3, "far", None)):
        after = [proj0]
        if finish is None:
            tok, finish = gathered(u, [proj0], "sc_w_in_" + name)
            after = [tok]
        wg_in0, = finish(after)
        proj0 = _mm_proj_group(h0, wg_in0, idx, u - 1, proj0, "mm_proj_0_" + name)
    tok, finish = gathered(4, [proj0], "sc_w_out")
    yb0 = _sc_fwd(proj0, cw3, "sc_fwd", deps=[tok])
    w_out0 = finish([yb0])[0].reshape(e, d)
    x1, y0 = _mm_out(yb0, w_out0, x0, gate[0], "mm_out_0")
    tok, finish = gathered(5, [x1], "lru_w_in")
    h1 = _norm_mod(x1, ng[1], scale[1], shift[1], "norm_mod_1", deps=[tok])
    wg_in1, = finish([h1])
    proj1 = _mm_proj(h1, wg_in1, 2, "mm_proj_1")
    tok, finish = gathered(6, [proj1], "lru_gates")
    wg_a, wg_x = finish([tok])
    w_a = wg_a.reshape(N_DEV, heads, dh_s, dh).transpose(1, 0, 2, 3).reshape(heads, dh, dh)
    w_x = wg_x.reshape(N_DEV, heads, dh_s, dh).transpose(1, 0, 2, 3).reshape(heads, dh, dh)
    tok, finish = gathered(7, [w_a, w_x], "lru_w_out")
    yb1, hs = _lru_fwd(proj1, cw4, cb, w_a, ba, w_x, bx, lam, "lru_fwd", deps=[tok])
    w_out1 = finish([yb1])[0].reshape(e, d)
    x2, y1 = _mm_out(yb1, w_out1, x1, gate[1], "mm_out_1")
    dx2, loss_part, d_fg, dy1, dgate1 = _final_loss(x2, fg, target, y1, gate[1], "final_loss")

    def pieces(g, rows, cols):
        return g.reshape(4, 2, rows, cols)

    def by_rows(g):
        return g.reshape(heads, N_DEV, dh_s, dh).transpose(1, 0, 2, 3).reshape(N_DEV, heads * dh_s, dh)

    def pair_begin(parts, group):
        send, recv, parts, lnd, token = _pair_start(parts, "pair_start_" + group)
        return dict(parts=parts, lands=lnd, send=send, recv=recv, group=group), token

    def scatter_start(pair, names, after):
        group = pair["group"]
        parts, gots = _pair_wait(pair["parts"], pair["lands"], pair["send"], pair["recv"], after,
                                 "pair_wait_" + group)
        sums = [_pair_sum(idx, p, q, "pair_sum_" + nm) for p, q, nm in zip(parts, gots, names)]
        empties = [lax.empty(sm.shape, sm.dtype) for sm in sums]
        send, recv, sums, lnd, token = _chip_start(sums, empties, "chip_start_" + group)
        return dict(parts=parts, gots=gots, names=names, group=group, sums=sums, lands=lnd,
                    send=send, recv=recv), token

    big = {"sc_w_in": (sc_w_in, m_sc_w_in, v_sc_w_in), "sc_w_out": (sc_w_out, m_sc_w_out, v_sc_w_out),
           "lru_w_in": (lru_w_in, m_lru_w_in, v_lru_w_in), "lru_w_a": (lru_w_a, m_lru_w_a, v_lru_w_a),
           "lru_w_x": (lru_w_x, m_lru_w_x, v_lru_w_x), "lru_w_out": (lru_w_out, m_lru_w_out, v_lru_w_out)}
    big_res = {}

    def scatter_finish(rs, after):
        recvs = _chip_wait(rs["sums"], rs["lands"], rs["send"], rs["recv"], after, "chip_wait_" + rs["group"])
        done = []
        for p, q, u, nm in zip(rs["parts"], rs["gots"], recvs, rs["names"]):
            w, m, v = big[nm]
            shp2 = p.shape[2:]
            res = _adamw_reduced(idx, w.reshape(shp2), m.reshape(shp2), v.reshape(shp2), p, q, [u], "adamw_" + nm)
            big_res[nm] = [r.reshape(w.shape) for r in res]
            done.append(res[1])
        return done

    dw_out1 = _mm_tn(yb1, dy1[None], 1, "mm_dw_out_1")
    pair, tok = pair_begin([pieces(dw_out1, es, d)], "lru_w_out")
    dyb1 = _mm_nt(dy1[None], w_out1[None], BF16, "mm_dyb_1", deps=[tok])
    rs1, tok = scatter_start(pair, ["lru_w_out"], [dyb1])
    dproj1, dw_a, dw_x, vecs1 = _lru_bwd(proj1, hs, dyb1, cw4, cb, w_a, ba, w_x, bx, lam, "lru_bwd", deps=[tok])
    dw_in1 = _mm_tn(h1, dproj1, N_DEV, "mm_dw_in_1")
    pair, tok = pair_begin([pieces(dw_in1, d, 2 * es), pieces(by_rows(dw_a), heads * dh_s, dh),
                            pieces(by_rows(dw_x), heads * dh_s, dh)], "lru_in")
    dh1 = _mm_nt(dproj1, wg_in1, BF16, "mm_dh_1", deps=[tok])
    rs2, tok = scatter_start(pair, ["lru_w_in", "lru_w_a", "lru_w_x"], [dh1])
    dx1, dscale1, dshift1, dng1, dy0, dgate0 = _norm_mod_bwd(x1, dh1, dx2, ng[1], scale[1], "norm_mod_bwd_1",
                                                             below=(y0, gate[0]), deps=[tok])
    dw_out0 = _mm_tn(yb0, dy0[None], 1, "mm_dw_out_0")
    pair, tok = pair_begin([pieces(dw_out0, es, d)], "sc_w_out")
    dyb0 = _mm_nt(dy0[None], w_out0[None], BF16, "mm_dyb_0", deps=[tok])
    rs3, tok = scatter_start(pair, ["sc_w_out"], [dyb0])
    dproj0, vecs0 = _sc_bwd(proj0, dyb0, cw3, "sc_bwd", deps=[tok])
    idx_one = jnp.stack([jnp.zeros_like(mc)] * 4 + [mc]).astype(jnp.int32)
    sc_w_in_steps = []

    def chip_step(j, pair, after):
        (part,), (got,) = _pair_wait(pair["parts"], pair["lands"], pair["send"], pair["recv"], after,
                                     "pair_wait_sc_w_in_%d" % j)
        sm = _pair_sum(idx_one, part, got, "pair_sum_sc_w_in_%d" % j, nslots=1)
        send, recv, sums, lnd, token = _chip_start([sm], [lax.empty(sm.shape, sm.dtype)],
                                                   "chip_start_sc_w_in_%d" % j, flips=(j,))
        sc_w_in_steps.append((sums, lnd, send, recv, j))
        return token

    pending, done = None, []
    for j in (3, 2, 1, 0):
        part = _mm_tn_group(h0, dproj0, idx, (j - 1) % 4, 2, "mm_dw_in_0_%d" % j, deps=done)[None]
        pair, tok = pair_begin([part], "sc_w_in_%d" % j)
        if j == 3:
            done = [chip_step(j, pair, [tok])]
            continue
        done = [tok]
        if pending is not None:
            done.append(chip_step(pending[0], pending[1], [tok]))
        pending = (j, pair)
    done += scatter_finish(rs1, done)
    done += scatter_finish(rs2, done)
    dh0 = _mm_nt(dproj0, wg_in0, BF16, "mm_dh_0", deps=done)
    pair = pending[1]
    (part,), (got,) = _pair_wait(pair["parts"], pair["lands"], pair["send"], pair["recv"], [dh0],
                                 "pair_wait_sc_w_in_0")
    dx0, dscale0, dshift0, dng0 = _norm_mod_bwd(x0, dh0, dx1, ng[0], scale[0], "norm_mod_bwd_0")
    done = scatter_finish(rs3, [dx0])
    dmod_mine = jnp.concatenate([dshift0, dscale0, dgate0, dshift1, dscale1, dgate1], axis=1)
    end_shapes = [(LANES,), (2, 3 * d), (2, d), (d,), (8, e), (16, e)]
    end_all = _small_gather(
        _pack([loss_part, dmod_mine, jnp.concatenate([dng0, dng1], axis=0), d_fg, vecs0, vecs1]),
        "gather_small_grads", deps=done)
    end_sum = _device_sum(end_all, "sum_small_grads").reshape(-1)
    loss_v, g_ada_b, g_norm_g, g_final_g, sum0, sum1 = _unpack(end_sum, end_shapes)
    loss = loss_v[0]
    dmod_all = _unpack(end_all.reshape(N_DEV, -1), end_shapes)[1].transpose(1, 0, 2)
    dmod_cols = _my_slice(dmod_all, me, 2)
    ada_out = _ada_update(c_all.T, dmod_cols, ada_w, m_ada_w, v_ada_w, "ada_update")

    g_sc_conv_w = _my_slice(sum0[0:3], me, 1)
    g_lru_b_a = _my_slice(sum1[0].reshape(heads, dh), me, 1)
    g_lru_b_x = _my_slice(sum1[1].reshape(heads, dh), me, 1)
    g_lru_lambda = _my_slice(sum1[2:3], me, 1)
    g_lru_conv_b = _my_slice(sum1[3:4], me, 1)
    g_lru_conv_w = _my_slice(sum1[4:8], me, 1)

    small_w = [norm_g, ada_b, final_g, sc_conv_w, lru_conv_w, lru_conv_b, lru_b_a, lru_b_x, lru_lambda]
    small_m = [m_norm_g, m_ada_b, m_final_g, m_sc_conv_w, m_lru_conv_w, m_lru_conv_b, m_lru_b_a, m_lru_b_x,
               m_lru_lambda]
    small_v = [v_norm_g, v_ada_b, v_final_g, v_sc_conv_w, v_lru_conv_w, v_lru_conv_b, v_lru_b_a, v_lru_b_x,
               v_lru_lambda]
    small_g = [g_norm_g, g_ada_b, g_final_g, g_sc_conv_w, g_lru_conv_w, g_lru_conv_b, g_lru_b_a, g_lru_b_x,
               g_lru_lambda]
    small_g = [g.reshape(w.shape) for g, w in zip(small_g, small_w)]
    shapes = [w.shape for w in small_w]
    packed = _adamw(_pack(small_w), _pack(small_g), _pack(small_m), _pack(small_v), "adamw_small")
    small_out = [small_g] + [_unpack(p.reshape(-1), shapes) for p in packed]

    after = [packed[0], ada_out[1]]
    recvs = []
    for sums, lnd, send, recv, j in sc_w_in_steps:
        recvs += _chip_wait(sums, lnd, send, recv, after, "chip_wait_sc_w_in_%d" % j)
    shp2 = part.shape[2:]
    res = _adamw_reduced(idx_one, sc_w_in.reshape(shp2), m_sc_w_in.reshape(shp2), v_sc_w_in.reshape(shp2),
                         part, got, recvs, "adamw_sc_w_in")
    big_res["sc_w_in"] = [r.reshape(sc_w_in.shape) for r in res]
    big_out = [big_res[nm] for nm in ("sc_w_in", "sc_w_out", "lru_w_in", "lru_w_a", "lru_w_x", "lru_w_out")]

    def small(kind, i):
        return small_out[kind][i]

    def bigw(kind, i):
        return big_out[i][kind]

    outs = [loss, dx0[None]]
    for kind in range(4):
        outs += [small(kind, 0), ada_out[kind], small(kind, 1), bigw(kind, 0), small(kind, 3), bigw(kind, 1),
                 bigw(kind, 2), small(kind, 4), small(kind, 5), bigw(kind, 3), small(kind, 6), bigw(kind, 4),
                 small(kind, 7), small(kind, 8), bigw(kind, 5), small(kind, 2)]
    return tuple(outs)
```

```python
import math

import jax
import jax.numpy as jnp
from jax import lax
from jax.experimental import pallas as pl
from jax.experimental.pallas import tpu as pltpu

N_DEV = 8
LANES = 128
EPS = 1e-6
RGLRU_C = 8.0
ADAM_LR = 0.001
ADAM_B1 = 0.9
ADAM_B2 = 0.999
ADAM_EPS = 1e-08
ADAM_WD = 0.01
ADAM_STEP = 10
VMEM_LIMIT = 56 * 1024 * 1024
MESH = pl.DeviceIdType.MESH
F32 = jnp.float32
BF16 = jnp.bfloat16
ANY = pl.BlockSpec(memory_space=pl.ANY)
HBM = pl.BlockSpec(memory_space=pltpu.HBM)
SEM = pl.BlockSpec(memory_space=pltpu.SEMAPHORE)
VMEM_SPEC = pl.BlockSpec(memory_space=pltpu.VMEM)
EFFECT = pltpu.SideEffectType.DATAFLOW_SIDE_EFFECTING
TOKEN = jax.ShapeDtypeStruct((8, LANES), jnp.float32)


def _tile(n, pref):
    t = min(n, pref)
    assert n % t == 0, (n, pref)
    return t


def _params(*sem):
    return pltpu.CompilerParams(dimension_semantics=sem, vmem_limit_bytes=VMEM_LIMIT)


def _position():
    return lax.axis_index("x"), lax.axis_index("y"), lax.axis_index("c")


def _flip(x, y, k):
    return (1 - x if k & 2 else x), (1 - y if k & 1 else y)


def _after(body, n_in, deps):
    if not deps:
        return body

    def wrapped(*refs):
        return body(*refs[:n_in], *refs[n_in + len(deps):])

    return wrapped


def _small_gather(v, name, deps=()):
    rows = v.shape[0]

    def body(v_ref, out_ref, send_sems, recv_sems):
        x, y, c = _position()
        me = 4 * x + 2 * y + c
        out_ref[me] = v_ref[...]
        copies = []
        for k in range(1, N_DEV):
            px, py = _flip(x, y, k >> 1)
            pc = 1 - c if k & 1 else c
            cp = pltpu.make_async_remote_copy(
                src_ref=v_ref, dst_ref=out_ref.at[me],
                send_sem=send_sems.at[k - 1], recv_sem=recv_sems.at[k - 1],
                device_id=(px, py, pc), device_id_type=MESH)
            cp.start()
            copies.append((cp, 4 * px + 2 * py + pc))
        for k, (cp, peer) in enumerate(copies):
            pltpu.make_async_remote_copy(
                src_ref=v_ref, dst_ref=out_ref.at[peer],
                send_sem=send_sems.at[k], recv_sem=recv_sems.at[k],
                device_id=(x, y, c), device_id_type=MESH).wait_recv()
        for cp, _ in copies:
            cp.wait_send()

    return pl.pallas_call(
        _after(body, 1, deps), name=name,
        out_shape=jax.ShapeDtypeStruct((N_DEV, rows, LANES), F32),
        in_specs=[VMEM_SPEC] + [ANY] * len(deps), out_specs=VMEM_SPEC,
        scratch_shapes=[pltpu.SemaphoreType.DMA((N_DEV - 1,)),
                        pltpu.SemaphoreType.DMA((N_DEV - 1,))],
        compiler_params=pltpu.CompilerParams(vmem_limit_bytes=VMEM_LIMIT),
    )(v, *deps)


def _hbm(a):
    return pltpu.with_memory_space_constraint(a, pltpu.HBM)


def _hbm_like(arrays):
    return [pltpu.HBM(a.shape, a.dtype) for a in arrays]


def _remote(src, dst, send, recv, to):
    return pltpu.make_async_remote_copy(src_ref=src, dst_ref=dst, send_sem=send, recv_sem=recv,
                                        device_id=to, device_id_type=MESH)


def _gather_start(lands, units, after, name):
    n, nu = len(lands), len(units)

    def body(*refs):
        lnd = refs[:n]
        sems = refs[n + len(after):n + len(after) + 2 * nu]
        token = refs[-1]
        x, y, c = _position()
        me = 4 * x + 2 * y + c
        targets = [(x, y, 1 - c)] + [(px, py, c) for px, py in (_flip(x, y, k) for k in (1, 2, 3))]
        for u, (members, ks) in enumerate(units):
            for slot, i in enumerate(members):
                for ki, k in enumerate(ks):
                    at = len(ks) * slot + ki
                    mine = lnd[i].at[me]
                    _remote(mine, mine, sems[2 * u].at[at], sems[2 * u + 1].at[at], targets[k]).start()
        token[...] = jnp.zeros_like(token)

    sem_shapes = []
    for members, ks in units:
        count = len(members) * len(ks)
        sem_shapes += [pltpu.SemaphoreType.DMA((count,)), pltpu.SemaphoreType.DMA((count,))]
    out = pl.pallas_call(
        body, name=name,
        out_shape=sem_shapes + _hbm_like(lands) + [TOKEN],
        in_specs=[HBM] * n + [ANY] * len(after),
        out_specs=[SEM] * (2 * nu) + [HBM] * n + [VMEM_SPEC],
        input_output_aliases={i: 2 * nu + i for i in range(n)},
        compiler_params=pltpu.CompilerParams(has_side_effects=EFFECT),
    )(*[_hbm(l) for l in lands], *after)
    sems = [(out[2 * u], out[2 * u + 1]) for u in range(nu)]
    return sems, list(out[2 * nu:2 * nu + n]), out[-1]


def _gather_forward(lands, ks, send, recv, after, name):
    m = len(lands)
    hops = [k for k in ks if k]
    nsem = 2 if hops else 0

    def body(*refs):
        lnd = refs[:m]
        send_ref, recv_ref = refs[m], refs[m + 1]
        outs = refs[m + 2 + len(after):]
        token = refs[-1]
        x, y, c = _position()
        me = (x, y, c)
        for slot in range(m):
            for ki, k in enumerate(ks):
                at = len(ks) * slot + ki
                if k:
                    px, py = _flip(x, y, k)
                    block = lnd[slot].at[4 * px + 2 * py + c]
                else:
                    block = lnd[slot].at[4 * x + 2 * y + (1 - c)]
                arrival = _remote(lnd[slot].at[4 * x + 2 * y + c], block, send_ref.at[at], recv_ref.at[at], me)
                arrival.wait_recv()
                if k:
                    fat = len(hops) * slot + hops.index(k)
                    _remote(block, block, outs[0].at[fat], outs[1].at[fat], (x, y, 1 - c)).start()
                arrival.wait_send()
        token[...] = jnp.zeros_like(token)

    count = len(hops) * m
    sem_shapes = [pltpu.SemaphoreType.DMA((count,)), pltpu.SemaphoreType.DMA((count,))] if hops else []
    out = pl.pallas_call(
        body, name=name,
        out_shape=sem_shapes + _hbm_like(lands) + [TOKEN],
        in_specs=[HBM] * m + [SEM, SEM] + [ANY] * len(after),
        out_specs=[SEM] * nsem + [HBM] * m + [VMEM_SPEC],
        input_output_aliases={i: nsem + i for i in range(m)},
        compiler_params=pltpu.CompilerParams(has_side_effects=EFFECT),
    )(*lands, send, recv, *after)
    fwd = (out[0], out[1]) if hops else None
    return fwd, list(out[nsem:nsem + m]), out[-1]


def _gather_finish(lands, ks, fwd, after, name):
    m = len(lands)
    hops = [k for k in ks if k]

    def body(*refs):
        lnd = refs[:m]
        fsend_ref, frecv_ref = refs[m], refs[m + 1]
        x, y, c = _position()
        for slot in range(m):
            for fi, k in enumerate(hops):
                px, py = _flip(x, y, k)
                sent = lnd[slot].at[4 * px + 2 * py + c]
                came = lnd[slot].at[4 * px + 2 * py + (1 - c)]
                fat = len(hops) * slot + fi
                cp = _remote(sent, came, fsend_ref.at[fat], frecv_ref.at[fat], (x, y, c))
                cp.wait_recv()
                cp.wait_send()

    out = pl.pallas_call(
        body, name=name,
        out_shape=_hbm_like(lands),
        in_specs=[HBM] * m + [SEM, SEM] + [ANY] * len(after), out_specs=[HBM] * m,
        input_output_aliases={i: i for i in range(m)},
        compiler_params=pltpu.CompilerParams(has_side_effects=EFFECT),
    )(*lands, fwd[0], fwd[1], *after)
    return list(out)


def _pair_start(parts, name):
    n = len(parts)
    lands = [lax.empty((p.shape[0], 1) + p.shape[2:], p.dtype) for p in parts]

    def body(*refs):
        ins, lnd = refs[:n], refs[n:2 * n]
        send_ref, recv_ref = refs[2 * n], refs[2 * n + 1]
        token = refs[-1]
        x, y, c = _position()
        for i in range(n):
            _remote(ins[i].at[:, pl.ds(1 - c, 1)], lnd[i], send_ref.at[i], recv_ref.at[i], (x, y, 1 - c)).start()
        token[...] = jnp.zeros_like(token)

    out = pl.pallas_call(
        body, name=name,
        out_shape=[pltpu.SemaphoreType.DMA((n,)), pltpu.SemaphoreType.DMA((n,))]
        + _hbm_like(parts) + _hbm_like(lands) + [TOKEN],
        in_specs=[HBM] * (2 * n), out_specs=[SEM, SEM] + [HBM] * (2 * n) + [VMEM_SPEC],
        input_output_aliases={i: 2 + i for i in range(2 * n)},
        compiler_params=pltpu.CompilerParams(has_side_effects=EFFECT),
    )(*[_hbm(p) for p in parts], *[_hbm(l) for l in lands])
    return out[0], out[1], list(out[2:2 + n]), list(out[2 + n:2 + 2 * n]), out[-1]


def _pair_wait(parts, lands, send, recv, after, name):
    n = len(parts)

    def body(*refs):
        ins, lnd = refs[:n], refs[n:2 * n]
        send_ref, recv_ref = refs[2 * n], refs[2 * n + 1]
        x, y, c = _position()
        for i in range(n):
            cp = _remote(ins[i].at[:, pl.ds(1 - c, 1)], lnd[i], send_ref.at[i], recv_ref.at[i], (x, y, c))
            cp.wait_recv()
            cp.wait_send()

    out = pl.pallas_call(
        body, name=name,
        out_shape=_hbm_like(parts) + _hbm_like(lands),
        in_specs=[HBM] * (2 * n) + [SEM, SEM] + [ANY] * len(after), out_specs=[HBM] * (2 * n),
        input_output_aliases={i: i for i in range(2 * n)},
        compiler_params=pltpu.CompilerParams(has_side_effects=EFFECT),
    )(*parts, *lands, send, recv, *after)
    return list(out[:n]), list(out[n:])


def _chip_start(sums, lands, name, flips=(1, 2, 3)):
    n, ns = len(sums), len(flips)

    def body(*refs):
        ins, lnd = refs[:n], refs[n:2 * n]
        send_ref, recv_ref = refs[2 * n], refs[2 * n + 1]
        token = refs[-1]
        x, y, c = _position()
        for i in range(n):
            for j, flip in enumerate(flips):
                px, py = _flip(x, y, flip)
                _remote(ins[i].at[j], lnd[i].at[j], send_ref.at[ns * i + j], recv_ref.at[ns * i + j],
                        (px, py, c)).start()
        token[...] = jnp.zeros_like(token)

    out = pl.pallas_call(
        body, name=name,
        out_shape=[pltpu.SemaphoreType.DMA((ns * n,)), pltpu.SemaphoreType.DMA((ns * n,))]
        + _hbm_like(sums) + _hbm_like(lands) + [TOKEN],
        in_specs=[HBM] * (2 * n), out_specs=[SEM, SEM] + [HBM] * (2 * n) + [VMEM_SPEC],
        input_output_aliases={i: 2 + i for i in range(2 * n)},
        compiler_params=pltpu.CompilerParams(has_side_effects=EFFECT),
    )(*[_hbm(s) for s in sums], *[_hbm(l) for l in lands])
    return out[0], out[1], out[2:2 + n], out[2 + n:2 + 2 * n], out[-1]


def _chip_wait(sums, lands, send, recv, after, name):
    n, ns = len(sums), sums[0].shape[0]

    def body(*refs):
        ins, lnd = refs[:n], refs[n:2 * n]
        send_ref, recv_ref = refs[2 * n], refs[2 * n + 1]
        x, y, c = _position()
        for i in range(n):
            for j in range(ns):
                cp = _remote(ins[i].at[j], lnd[i].at[j], send_ref.at[ns * i + j], recv_ref.at[ns * i + j], (x, y, c))
                cp.wait_recv()
                cp.wait_send()

    out = pl.pallas_call(
        body, name=name,
        out_shape=_hbm_like(sums) + _hbm_like(lands),
        in_specs=[HBM] * (2 * n) + [SEM, SEM] + [ANY] * len(after), out_specs=[HBM] * (2 * n),
        input_output_aliases={i: i for i in range(2 * n)},
        compiler_params=pltpu.CompilerParams(has_side_effects=EFFECT),
    )(*sums, *lands, send, recv, *after)
    return list(out[n:])


def _pair_sum(idx, part, got, name, nslots=3):
    _, _, rows, cols = part.shape
    tr = _tile(rows, 1024)

    def body(idx_ref, p_ref, q_ref, o_ref):
        o_ref[...] = (p_ref[...].astype(F32) + q_ref[...].astype(F32)).astype(o_ref.dtype)

    grid_spec = pltpu.PrefetchScalarGridSpec(
        num_scalar_prefetch=1, grid=(nslots, rows // tr),
        in_specs=[pl.BlockSpec((None, None, tr, cols), lambda j, r, idx: (idx[j], idx[4], r, 0)),
                  pl.BlockSpec((None, None, tr, cols), lambda j, r, idx: (idx[j], 0, r, 0))],
        out_specs=pl.BlockSpec((None, tr, cols), lambda j, r, idx: (j, r, 0)))
    return pl.pallas_call(
        body, name=name, grid_spec=grid_spec,
        out_shape=jax.ShapeDtypeStruct((nslots, rows, cols), part.dtype),
        compiler_params=_params("arbitrary", "arbitrary"),
    )(idx, part, got)


def _mm_proj(h, wg, groups, name):
    s, k = h.shape
    nchunk, _, n = wg.shape
    e = nchunk * n // groups
    tn = _tile(min(n, e), 512)

    def body(h_ref, w_ref, o_ref):
        o_ref[...] = jnp.dot(h_ref[...], w_ref[...], preferred_element_type=F32).astype(o_ref.dtype)

    return pl.pallas_call(
        body, name=name, grid=(nchunk * n // tn,),
        in_specs=[pl.BlockSpec((s, k), lambda j: (0, 0)),
                  pl.BlockSpec((None, k, tn), lambda j: ((j * tn) // n, 0, ((j * tn) % n) // tn))],
        out_specs=pl.BlockSpec((None, s, tn), lambda j: ((j * tn) // e, 0, ((j * tn) % e) // tn)),
        out_shape=jax.ShapeDtypeStruct((groups, s, e), BF16),
        compiler_params=_params("arbitrary"),
    )(h, wg)


def _mm_proj_group(h, wg, idx, pos, prev, name, deps=()):
    s, k = h.shape
    _, _, n = wg.shape
    _, _, e = prev.shape
    tn = _tile(n, 512)
    nd = len(deps)

    def body(idx_ref, h_ref, w_ref, prev_ref, *rest):
        o_ref = rest[nd]
        o_ref[...] = jnp.dot(h_ref[...], w_ref[...], preferred_element_type=F32).astype(o_ref.dtype)

    def col(j, idx):
        return idx[pos] * (2 * n) + j * tn

    grid_spec = pltpu.PrefetchScalarGridSpec(
        num_scalar_prefetch=1, grid=(2 * n // tn,),
        in_specs=[pl.BlockSpec((s, k), lambda j, idx: (0, 0)),
                  pl.BlockSpec((None, k, tn), lambda j, idx: (col(j, idx) // n, 0, (col(j, idx) % n) // tn)),
                  ANY] + [ANY] * nd,
        out_specs=pl.BlockSpec((None, s, tn), lambda j, idx: (col(j, idx) // e, 0, (col(j, idx) % e) // tn)))
    return pl.pallas_call(
        body, name=name, grid_spec=grid_spec,
        out_shape=jax.ShapeDtypeStruct(prev.shape, prev.dtype),
        input_output_aliases={3: 0},
        compiler_params=_params("arbitrary"),
    )(idx, h, wg, prev, *deps)


def _mm_out(yb, w, x, gate, name):
    s, k = yb.shape
    d = w.shape[1]
    tn = _tile(d, 512)
    tk = _tile(k, 2048)
    nk = k // tk

    def body(a_ref, w_ref, x_ref, g_ref, xo_ref, y_ref, acc_ref):
        kk = pl.program_id(1)

        @pl.when(kk == 0)
        def _():
            acc_ref[...] = jnp.zeros_like(acc_ref)

        acc_ref[...] += jnp.dot(a_ref[...], w_ref[...], preferred_element_type=F32)

        @pl.when(kk == nk - 1)
        def _():
            y = acc_ref[...]
            y_ref[...] = y.astype(y_ref.dtype)
            xo_ref[...] = x_ref[...] + g_ref[...] * y

    return pl.pallas_call(
        body, name=name, grid=(d // tn, nk),
        in_specs=[pl.BlockSpec((s, tk), lambda j, kk: (0, kk)),
                  pl.BlockSpec((tk, tn), lambda j, kk: (kk, j)),
                  pl.BlockSpec((s, tn), lambda j, kk: (0, j)),
                  pl.BlockSpec((1, tn), lambda j, kk: (0, j))],
        out_specs=[pl.BlockSpec((s, tn), lambda j, kk: (0, j)),
                   pl.BlockSpec((s, tn), lambda j, kk: (0, j))],
        out_shape=[jax.ShapeDtypeStruct((s, d), F32), jax.ShapeDtypeStruct((s, d), BF16)],
        scratch_shapes=[pltpu.VMEM((s, tn), F32)],
        compiler_params=_params("arbitrary", "arbitrary"),
    )(yb, w, x, gate)


def _mm_nt(a3, w3, out_dtype, name, deps=()):
    g, s, ea = a3.shape
    cw, n, nw = w3.shape
    total = g * ea
    assert total == cw * nw
    tk = _tile(min(ea, nw), 2048)
    tn = _tile(n, 1024)
    nk = total // tk

    def body(a_ref, w_ref, o_ref, acc_ref):
        kk = pl.program_id(1)

        @pl.when(kk == 0)
        def _():
            acc_ref[...] = jnp.zeros_like(acc_ref)

        acc_ref[...] += lax.dot_general(a_ref[...], w_ref[...], (((1,), (1,)), ((), ())),
                                        preferred_element_type=F32)

        @pl.when(kk == nk - 1)
        def _():
            o_ref[...] = acc_ref[...].astype(o_ref.dtype)

    return pl.pallas_call(
        _after(body, 2, deps), name=name, grid=(n // tn, nk),
        in_specs=[pl.BlockSpec((None, s, tk), lambda j, kk: ((kk * tk) // ea, 0, ((kk * tk) % ea) // tk)),
                  pl.BlockSpec((None, tn, tk), lambda j, kk: ((kk * tk) // nw, j, ((kk * tk) % nw) // tk))]
        + [ANY] * len(deps),
        out_specs=pl.BlockSpec((s, tn), lambda j, kk: (0, j)),
        out_shape=jax.ShapeDtypeStruct((s, n), out_dtype),
        scratch_shapes=[pltpu.VMEM((s, tn), F32)],
        compiler_params=_params("arbitrary", "arbitrary"),
    )(a3, w3, *deps)


def _mm_tn(a, b3, nchunk, name, deps=()):
    s, ka = a.shape
    g, _, eb = b3.shape
    n = g * eb // nchunk
    tm = _tile(ka, 1024)
    tn = _tile(min(n, eb), 1024)

    def body(a_ref, b_ref, o_ref, at_ref):
        @pl.when(pl.program_id(1) == 0)
        def _():
            at_ref[...] = a_ref[...].astype(F32).T.astype(at_ref.dtype)

        o_ref[...] = jnp.dot(at_ref[...], b_ref[...], preferred_element_type=F32).astype(o_ref.dtype)

    return pl.pallas_call(
        _after(body, 2, deps), name=name, grid=(ka // tm, g * eb // tn),
        in_specs=[pl.BlockSpec((s, tm), lambda i, j: (0, i)),
                  pl.BlockSpec((None, s, tn), lambda i, j: ((j * tn) // eb, 0, ((j * tn) % eb) // tn))]
        + [ANY] * len(deps),
        out_specs=pl.BlockSpec((None, tm, tn), lambda i, j: ((j * tn) // n, i, ((j * tn) % n) // tn)),
        out_shape=jax.ShapeDtypeStruct((nchunk, ka, n), BF16),
        scratch_shapes=[pltpu.VMEM((tm, s), BF16)],
        compiler_params=_params("arbitrary", "arbitrary"),
    )(a, b3, *deps)


def _mm_tn_group(a, b3, idx, pos, nchunk, name, deps=()):
    s, ka = a.shape
    _, _, eb = b3.shape
    n = eb // nchunk
    tm = _tile(ka, 1024)
    tn = _tile(n, 1024)
    nd = len(deps)

    def body(idx_ref, a_ref, b_ref, *rest):
        o_ref, at_ref = rest[nd:]

        @pl.when(pl.program_id(1) == 0)
        def _():
            at_ref[...] = a_ref[...].astype(F32).T.astype(at_ref.dtype)

        o_ref[...] = jnp.dot(at_ref[...], b_ref[...], preferred_element_type=F32).astype(o_ref.dtype)

    grid_spec = pltpu.PrefetchScalarGridSpec(
        num_scalar_prefetch=1, grid=(ka // tm, eb // tn),
        in_specs=[pl.BlockSpec((s, tm), lambda i, j, idx: (0, i)),
                  pl.BlockSpec((None, s, tn), lambda i, j, idx: (idx[pos], 0, j))] + [ANY] * nd,
        out_specs=pl.BlockSpec((None, tm, tn), lambda i, j, idx: ((j * tn) // n, i, ((j * tn) % n) // tn)),
        scratch_shapes=[pltpu.VMEM((tm, s), BF16)])
    return pl.pallas_call(
        body, name=name, grid_spec=grid_spec,
        out_shape=jax.ShapeDtypeStruct((nchunk, ka, n), BF16),
        compiler_params=_params("arbitrary", "arbitrary"),
    )(idx, a, b3, *deps)


def _sigmoid(z):
    return jax.nn.sigmoid(z)


def _shift_down(v, k, fill=0.0, period=None):
    if k == 0:
        return v
    row = lax.broadcasted_iota(jnp.int32, v.shape, 0)
    if period is not None:
        row = row & (period - 1)
    return jnp.where(row >= k, pltpu.roll(v, k, 0), fill)


def _shift_up(v, k, fill=0.0, period=None):
    if k == 0:
        return v
    s = v.shape[0]
    row = lax.broadcasted_iota(jnp.int32, v.shape, 0)
    if period is not None:
        row, s = row & (period - 1), period
    return jnp.where(row < s - k, pltpu.roll(v, v.shape[0] - k, 0), fill)


SCAN_BLOCK = 64


def _scan(a, b, shift):
    s = a.shape[0]
    blk = min(SCAN_BLOCK, s)
    k = 1
    while k < blk:
        b = a * shift(b, k, 0.0, blk) + b
        a = a * shift(a, k, 1.0, blk)
        k *= 2
    nblk = s // blk
    forward = shift is _shift_down
    order = range(nblk) if forward else range(nblk - 1, -1, -1)
    edge = blk - 1 if forward else 0
    out = [None] * nblk
    carry = None
    for i in order:
        h = b[i * blk:(i + 1) * blk]
        if carry is not None:
            h = a[i * blk:(i + 1) * blk] * carry + h
        carry = h[edge:edge + 1]
        out[i] = h
    return jnp.concatenate(out, axis=0) if nblk > 1 else out[0]


def _norm_mod(x, g, scale, shift, name, deps=()):
    s, d = x.shape
    ts = _tile(s, 256)

    def body(x_ref, g_ref, sc_ref, sh_ref, h_ref):
        xv = x_ref[...]
        rstd = lax.rsqrt(jnp.mean(xv * xv, axis=-1, keepdims=True) + EPS)
        nrm = xv * rstd * g_ref[...]
        h_ref[...] = (nrm * (1.0 + sc_ref[...]) + sh_ref[...]).astype(h_ref.dtype)

    vec = pl.BlockSpec((1, d), lambda i: (0, 0))
    return pl.pallas_call(
        _after(body, 4, deps), name=name, grid=(s // ts,),
        in_specs=[pl.BlockSpec((ts, d), lambda i: (i, 0)), vec, vec, vec] + [ANY] * len(deps),
        out_specs=pl.BlockSpec((ts, d), lambda i: (i, 0)),
        out_shape=jax.ShapeDtypeStruct((s, d), BF16),
        compiler_params=_params("arbitrary"),
    )(x, g, scale, shift, *deps)


def _gate_terms(dx, y_ref, gate_ref, dy_ref, dgate_ref):
    dy_ref[...] = (dx * gate_ref[...]).astype(dy_ref.dtype)
    dgate_ref[...] += jnp.sum(dx * y_ref[...].astype(F32), axis=0, keepdims=True)


def _norm_mod_bwd(x, dh, dx_res, g, scale, name, below=None, deps=()):
    s, d = x.shape
    ts = _tile(s, 256)
    nb = 2 if below is not None else 0

    def body(x_ref, dh_ref, dr_ref, g_ref, sc_ref, *rest):
        dx_ref, dsc_ref, dsh_ref, dg_ref = rest[nb:nb + 4]

        @pl.when(pl.program_id(0) == 0)
        def _():
            for ref in rest[nb + 1:nb + 4] + rest[nb + 5:]:
                ref[...] = jnp.zeros_like(ref)

        xv = x_ref[...]
        dh_v = dh_ref[...].astype(F32)
        gv = g_ref[...]
        rstd = lax.rsqrt(jnp.mean(xv * xv, axis=-1, keepdims=True) + EPS)
        xhat = xv * rstd
        dsc_ref[...] += jnp.sum(dh_v * xhat * gv, axis=0, keepdims=True)
        dsh_ref[...] += jnp.sum(dh_v, axis=0, keepdims=True)
        dn = dh_v * (1.0 + sc_ref[...])
        dg_ref[...] += jnp.sum(dn * xhat, axis=0, keepdims=True)
        dxhat = dn * gv
        proj = jnp.mean(dxhat * xhat, axis=-1, keepdims=True)
        dx = dr_ref[...] + rstd * (dxhat - xhat * proj)
        dx_ref[...] = dx
        if nb:
            _gate_terms(dx, rest[0], rest[1], rest[nb + 4], rest[nb + 5])

    row = pl.BlockSpec((ts, d), lambda i: (i, 0))
    vec = pl.BlockSpec((1, d), lambda i: (0, 0))
    extra = list(below) if nb else []
    return pl.pallas_call(
        _after(body, 5 + nb, deps), name=name, grid=(s // ts,),
        in_specs=[row, row, row, vec, vec] + [row, vec][:nb] + [ANY] * len(deps),
        out_specs=[row, vec, vec, vec] + [row, vec][:nb],
        out_shape=[jax.ShapeDtypeStruct((s, d), F32)] + [jax.ShapeDtypeStruct((1, d), F32)] * 3
        + [jax.ShapeDtypeStruct((s, d), BF16), jax.ShapeDtypeStruct((1, d), F32)][:nb],
        compiler_params=_params("arbitrary"),
    )(x, dh, dx_res, g, scale, *extra, *deps)


def _final_loss(x, g, target, y, gate, name):
    s, d = x.shape
    ts = _tile(s, 256)

    def body(x_ref, g_ref, t_ref, y_ref, gate_ref, dx_ref, loss_ref, dg_ref, dy_ref, dgate_ref):
        @pl.when(pl.program_id(0) == 0)
        def _():
            loss_ref[...] = jnp.zeros_like(loss_ref)
            dg_ref[...] = jnp.zeros_like(dg_ref)
            dgate_ref[...] = jnp.zeros_like(dgate_ref)

        xv = x_ref[...]
        gv = g_ref[...]
        rstd = lax.rsqrt(jnp.mean(xv * xv, axis=-1, keepdims=True) + EPS)
        xhat = xv * rstd
        err = xhat * gv - t_ref[...]
        loss_ref[...] += 0.5 * jnp.sum(jnp.mean(err * err, axis=-1, keepdims=True))
        dy = err * (1.0 / d)
        dg_ref[...] += jnp.sum(dy * xhat, axis=0, keepdims=True)
        dxhat = dy * gv
        proj = jnp.mean(dxhat * xhat, axis=-1, keepdims=True)
        dx = rstd * (dxhat - xhat * proj)
        dx_ref[...] = dx
        _gate_terms(dx, y_ref, gate_ref, dy_ref, dgate_ref)

    row = pl.BlockSpec((ts, d), lambda i: (i, 0))
    vec = pl.BlockSpec((1, d), lambda i: (0, 0))
    return pl.pallas_call(
        body, name=name, grid=(s // ts,),
        in_specs=[row, vec, row, row, vec],
        out_specs=[row, pl.BlockSpec((1, LANES), lambda i: (0, 0)), vec, row, vec],
        out_shape=[jax.ShapeDtypeStruct((s, d), F32), jax.ShapeDtypeStruct((1, LANES), F32),
                   jax.ShapeDtypeStruct((1, d), F32), jax.ShapeDtypeStruct((s, d), BF16),
                   jax.ShapeDtypeStruct((1, d), F32)],
        compiler_params=_params("arbitrary"),
    )(x, g, target, y, gate)


def _conv(v, w_ref, width):
    out = w_ref[width - 1:width, :] * v
    for k in range(width - 1):
        out = out + w_ref[k:k + 1, :] * _shift_down(v, width - 1 - k)
    return out


def _sc_fwd(proj, conv_w, name, deps=()):
    _, s, e = proj.shape
    te = _tile(e, 256)
    width = conv_w.shape[0]

    def body(b_ref, c_ref, v_ref, g_ref, w_ref, o_ref):
        cv = c_ref[...].astype(F32) * v_ref[...].astype(F32)
        u = _conv(cv, w_ref, width)
        gv = g_ref[...].astype(F32)
        o_ref[...] = (b_ref[...].astype(F32) * u * (gv * _sigmoid(gv))).astype(o_ref.dtype)

    def part(q):
        return pl.BlockSpec((None, s, te), lambda j, q=q: (q, 0, j))

    return pl.pallas_call(
        _after(body, 5, deps), name=name, grid=(e // te,),
        in_specs=[part(0), part(1), part(2), part(3), pl.BlockSpec((width, te), lambda j: (0, j))]
        + [ANY] * len(deps),
        out_specs=pl.BlockSpec((s, te), lambda j: (0, j)),
        out_shape=jax.ShapeDtypeStruct((s, e), BF16),
        compiler_params=_params("arbitrary"),
    )(proj, proj, proj, proj, conv_w, *deps)


def _sc_bwd(proj, dyb, conv_w, name, deps=()):
    _, s, e = proj.shape
    te = _tile(e, 256)
    width = conv_w.shape[0]

    def body(b_ref, c_ref, v_ref, g_ref, dy_ref, w_ref, dp_ref, vec_ref):
        bv = b_ref[...].astype(F32)
        cvl = c_ref[...].astype(F32)
        vv = v_ref[...].astype(F32)
        gv = g_ref[...].astype(F32)
        dyv = dy_ref[...].astype(F32)
        cv = cvl * vv
        u = _conv(cv, w_ref, width)
        sg = _sigmoid(gv)
        silu = gv * sg
        dp_ref[0] = (dyv * u * silu).astype(dp_ref.dtype)
        du = dyv * bv * silu
        dp_ref[3] = (dyv * bv * u * (sg * (1.0 + gv * (1.0 - sg)))).astype(dp_ref.dtype)
        dcv = w_ref[width - 1:width, :] * du
        vec_ref[...] = jnp.zeros_like(vec_ref)
        vec_ref[width - 1:width, :] = jnp.sum(du * cv, axis=0, keepdims=True)
        for k in range(width - 1):
            sh = width - 1 - k
            dcv = dcv + w_ref[k:k + 1, :] * _shift_up(du, sh)
            vec_ref[k:k + 1, :] = jnp.sum(du * _shift_down(cv, sh), axis=0, keepdims=True)
        dp_ref[1] = (dcv * vv).astype(dp_ref.dtype)
        dp_ref[2] = (dcv * cvl).astype(dp_ref.dtype)

    def part(q):
        return pl.BlockSpec((None, s, te), lambda j, q=q: (q, 0, j))

    return pl.pallas_call(
        _after(body, 6, deps), name=name, grid=(e // te,),
        in_specs=[part(0), part(1), part(2), part(3), pl.BlockSpec((s, te), lambda j: (0, j)),
                  pl.BlockSpec((width, te), lambda j: (0, j))] + [ANY] * len(deps),
        out_specs=[pl.BlockSpec((4, s, te), lambda j: (0, 0, j)),
                   pl.BlockSpec((8, te), lambda j: (0, j))],
        out_shape=[jax.ShapeDtypeStruct((4, s, e), BF16), jax.ShapeDtypeStruct((8, e), F32)],
        compiler_params=_params("arbitrary"),
    )(proj, proj, proj, proj, dyb, conv_w, *deps)


def _lru_gates(v_pre, w_ref, cb_ref, wa_ref, ba_ref, wx_ref, bx_ref, lam_ref, width):
    v = _conv(v_pre, w_ref, width) + cb_ref[...]
    vb = v.astype(BF16)
    r = _sigmoid(jnp.dot(vb, wa_ref[...], preferred_element_type=F32) + ba_ref[...])
    i = _sigmoid(jnp.dot(vb, wx_ref[...], preferred_element_type=F32) + bx_ref[...])
    nl = -lam_ref[...]
    sp = jnp.maximum(nl, 0.0) + jnp.log1p(jnp.exp(-jnp.abs(nl)))
    log_a = (-RGLRU_C) * r * sp
    a = jnp.exp(log_a)
    one_minus_a2 = jnp.tanh(-log_a) * (1.0 + a * a)
    mult = jnp.sqrt(one_minus_a2)
    return v, vb, r, i, sp, a, mult


def _lru_specs(s, dh, heads, width):
    head_col = lambda q: pl.BlockSpec((None, s, dh), lambda h, q=q: (q, 0, h))
    vec = pl.BlockSpec((1, dh), lambda h: (0, h))
    mat = pl.BlockSpec((None, dh, dh), lambda h: (h, 0, 0))
    weights = [pl.BlockSpec((width, dh), lambda h: (0, h)), vec, mat, vec, mat, vec, vec]
    return head_col, weights


def _lru_fwd(proj, conv_w, conv_b, w_a, b_a, w_x, b_x, lam, name, deps=()):
    _, s, e = proj.shape
    heads, dh, _ = w_a.shape
    width = conv_w.shape[0]

    def body(v_ref, g_ref, w_ref, cb_ref, wa_ref, ba_ref, wx_ref, bx_ref, lam_ref, yb_ref, keep_ref):
        v, _, r, i, _, a, mult = _lru_gates(v_ref[...].astype(F32), w_ref, cb_ref, wa_ref, ba_ref,
                                           wx_ref, bx_ref, lam_ref, width)
        hs = _scan(a, mult * i * v, _shift_down)
        for k, val in enumerate((hs, v, r, i, a, mult)):
            keep_ref[k] = val
        gv = g_ref[...].astype(F32)
        yb_ref[...] = (hs * (gv * _sigmoid(gv))).astype(yb_ref.dtype)

    head_col, weights = _lru_specs(s, dh, heads, width)
    return pl.pallas_call(
        _after(body, 9, deps), name=name, grid=(heads,),
        in_specs=[head_col(0), head_col(1)] + weights + [ANY] * len(deps),
        out_specs=[pl.BlockSpec((s, dh), lambda h: (0, h)), pl.BlockSpec((6, s, dh), lambda h: (0, 0, h))],
        out_shape=[jax.ShapeDtypeStruct((s, e), BF16), jax.ShapeDtypeStruct((6, s, e), F32)],
        compiler_params=_params("arbitrary"),
    )(proj, proj, conv_w, conv_b, w_a, b_a, w_x, b_x, lam, *deps)


def _lru_bwd(proj, keep, dyb, conv_w, conv_b, w_a, b_a, w_x, b_x, lam, name, deps=()):
    _, s, e = proj.shape
    heads, dh, _ = w_a.shape
    width = conv_w.shape[0]

    def body(v_ref, g_ref, hs_ref, dy_ref, w_ref, cb_ref, wa_ref, ba_ref, wx_ref, bx_ref, lam_ref,
             dp_ref, dwa_ref, dwx_ref, vec_ref):
        v_pre = v_ref[...].astype(F32)
        hs, v, r, i, a, mult = (hs_ref[k] for k in range(6))
        vb = v.astype(BF16)
        nl = -lam_ref[...]
        sp = jnp.maximum(nl, 0.0) + jnp.log1p(jnp.exp(-jnp.abs(nl)))
        gv = g_ref[...].astype(F32)
        dyv = dy_ref[...].astype(F32)
        sg = _sigmoid(gv)
        dp_ref[1] = (dyv * hs * (sg * (1.0 + gv * (1.0 - sg)))).astype(dp_ref.dtype)
        dhs = dyv * (gv * sg)
        d_h = _scan(_shift_up(a, 1), dhs, _shift_up)
        da = d_h * _shift_down(hs, 1)
        iv = i * v
        dlog_a = da * a - (d_h * iv) * (a * a) / mult
        di = d_h * mult * v
        dv = d_h * mult * i
        dzr = dlog_a * (-RGLRU_C) * sp * r * (1.0 - r)
        dzi = di * i * (1.0 - i)
        dsp = jnp.sum(dlog_a * r, axis=0, keepdims=True) * (-RGLRU_C)
        vec_ref[...] = jnp.zeros_like(vec_ref)
        vec_ref[0:1, :] = jnp.sum(dzr, axis=0, keepdims=True)
        vec_ref[1:2, :] = jnp.sum(dzi, axis=0, keepdims=True)
        vec_ref[2:3, :] = -dsp * _sigmoid(-lam_ref[...])
        dzr_b = dzr.astype(BF16)
        dzi_b = dzi.astype(BF16)
        vt = vb.astype(F32).T.astype(BF16)
        dwa_ref[...] = jnp.dot(vt, dzr_b, preferred_element_type=F32).astype(dwa_ref.dtype)
        dwx_ref[...] = jnp.dot(vt, dzi_b, preferred_element_type=F32).astype(dwx_ref.dtype)
        nt = (((1,), (1,)), ((), ()))
        dv = dv + lax.dot_general(dzr_b, wa_ref[...], nt, preferred_element_type=F32)
        dv = dv + lax.dot_general(dzi_b, wx_ref[...], nt, preferred_element_type=F32)
        vec_ref[3:4, :] = jnp.sum(dv, axis=0, keepdims=True)
        dvp = w_ref[width - 1:width, :] * dv
        vec_ref[4 + width - 1:4 + width, :] = jnp.sum(dv * v_pre, axis=0, keepdims=True)
        for k in range(width - 1):
            sh = width - 1 - k
            dvp = dvp + w_ref[k:k + 1, :] * _shift_up(dv, sh)
            vec_ref[4 + k:5 + k, :] = jnp.sum(dv * _shift_down(v_pre, sh), axis=0, keepdims=True)
        dp_ref[0] = dvp.astype(dp_ref.dtype)

    head_col, weights = _lru_specs(s, dh, heads, width)
    col = pl.BlockSpec((s, dh), lambda h: (0, h))
    mat = pl.BlockSpec((None, dh, dh), lambda h: (h, 0, 0))
    return pl.pallas_call(
        _after(body, 11, deps), name=name, grid=(heads,),
        in_specs=[head_col(0), head_col(1), pl.BlockSpec((6, s, dh), lambda h: (0, 0, h)), col] + weights
        + [ANY] * len(deps),
        out_specs=[pl.BlockSpec((2, s, dh), lambda h: (0, 0, h)), mat, mat,
                   pl.BlockSpec((16, dh), lambda h: (0, h))],
        out_shape=[jax.ShapeDtypeStruct((2, s, e), BF16),
                   jax.ShapeDtypeStruct((heads, dh, dh), BF16),
                   jax.ShapeDtypeStruct((heads, dh, dh), BF16),
                   jax.ShapeDtypeStruct((16, e), F32)],
        compiler_params=_params("arbitrary"),
    )(proj, proj, keep, dyb, conv_w, conv_b, w_a, b_a, w_x, b_x, lam, *deps)


def _ada_mod(c_all, w, b, name):
    layers, d, f = w.shape
    nb = c_all.shape[0]

    def body(c_ref, w_ref, b_ref, o_ref):
        cv = c_ref[...]
        sc = cv * _sigmoid(cv)
        o_ref[...] = jnp.dot(sc, w_ref[...], preferred_element_type=F32,
                             precision=lax.Precision.HIGHEST) + b_ref[...]

    return pl.pallas_call(
        body, name=name, grid=(layers,),
        in_specs=[pl.BlockSpec((nb, d), lambda l: (0, 0)),
                  pl.BlockSpec((None, d, f), lambda l: (l, 0, 0)),
                  pl.BlockSpec((None, 1, f), lambda l: (l, 0, 0))],
        out_specs=pl.BlockSpec((None, nb, f), lambda l: (l, 0, 0)),
        out_shape=jax.ShapeDtypeStruct((layers, nb, f), F32),
        compiler_params=_params("arbitrary"),
    )(c_all, w, b)


def _ada_update(c_all_t, dmod, w, m, v, name):
    d, nb = c_all_t.shape
    layers, _, f = dmod.shape
    tr = _tile(d, 512)

    def body(c_ref, dm_ref, w_ref, m_ref, v_ref, g_ref, d_ref, mo_ref, vo_ref):
        cv = c_ref[...]
        sc = cv * _sigmoid(cv)
        g = sc[:, 0:1] * dm_ref[0:1, :]
        for k in range(1, nb):
            g = g + sc[:, k:k + 1] * dm_ref[k:k + 1, :]
        g_ref[...] = g
        d_ref[...], mo_ref[...], vo_ref[...] = _adamw_math(w_ref[...], g, m_ref[...], v_ref[...])

    blk = pl.BlockSpec((None, tr, f), lambda l, i: (l, i, 0))
    return pl.pallas_call(
        body, name=name, grid=(layers, d // tr),
        in_specs=[pl.BlockSpec((tr, nb), lambda l, i: (i, 0)),
                  pl.BlockSpec((None, nb, f), lambda l, i: (l, 0, 0)), blk, blk, blk],
        out_specs=[blk] * 4,
        out_shape=[jax.ShapeDtypeStruct((layers, d, f), F32)] * 4,
        compiler_params=_params("arbitrary", "arbitrary"),
    )(c_all_t, dmod, w, m, v)


def _device_sum(g, name):
    _, rows, _ = g.shape

    def body(g_ref, o_ref):
        acc = g_ref[0]
        for k in range(1, N_DEV):
            acc = acc + g_ref[k]
        o_ref[...] = acc

    return pl.pallas_call(
        body, name=name,
        in_specs=[VMEM_SPEC], out_specs=VMEM_SPEC,
        out_shape=jax.ShapeDtypeStruct((rows, LANES), F32),
        compiler_params=pltpu.CompilerParams(vmem_limit_bytes=VMEM_LIMIT),
    )(g)


def _adamw_math(w, g, m, v):
    m = ADAM_B1 * m + (1.0 - ADAM_B1) * g
    v = ADAM_B2 * v + (1.0 - ADAM_B2) * (g * g)
    m_hat = m / (1.0 - ADAM_B1 ** ADAM_STEP)
    v_hat = v / (1.0 - ADAM_B2 ** ADAM_STEP)
    delta = -ADAM_LR * (m_hat / (jnp.sqrt(v_hat) + ADAM_EPS) + ADAM_WD * w)
    return delta, m, v


def _adamw(w, g, m, v, name):
    rows, cols = w.shape
    tr = _tile(rows, 256)

    def body(w_ref, g_ref, m_ref, v_ref, d_ref, mo_ref, vo_ref):
        d_ref[...], mo_ref[...], vo_ref[...] = _adamw_math(w_ref[...], g_ref[...], m_ref[...], v_ref[...])

    blk = pl.BlockSpec((tr, cols), lambda i: (i, 0))
    return pl.pallas_call(
        body, name=name, grid=(rows // tr,),
        in_specs=[blk] * 4, out_specs=[blk] * 3,
        out_shape=[jax.ShapeDtypeStruct((rows, cols), F32)] * 3,
        compiler_params=_params("arbitrary"),
    )(w, g, m, v)


def _adamw_reduced(idx, w, m, v, part, got, recvs, name):
    rows, cols = w.shape
    tr = _tile(rows, 256)
    nr = len(recvs)

    def body(idx_ref, w_ref, m_ref, v_ref, p_ref, q_ref, *rest):
        g_ref, d_ref, mo_ref, vo_ref = rest[nr:]
        g = p_ref[...].astype(F32) + q_ref[...].astype(F32)
        for u_ref in rest[:nr]:
            for j in range(u_ref.shape[0]):
                g = g + u_ref[j].astype(F32)
        g_ref[...] = g
        d_ref[...], mo_ref[...], vo_ref[...] = _adamw_math(w_ref[...], g, m_ref[...], v_ref[...])

    blk = pl.BlockSpec((tr, cols), lambda i, idx: (i, 0))
    grid_spec = pltpu.PrefetchScalarGridSpec(
        num_scalar_prefetch=1, grid=(rows // tr,),
        in_specs=[blk, blk, blk,
                  pl.BlockSpec((None, None, tr, cols), lambda i, idx: (idx[3], idx[4], i, 0)),
                  pl.BlockSpec((None, None, tr, cols), lambda i, idx: (idx[3], 0, i, 0))]
        + [pl.BlockSpec((u.shape[0], tr, cols), lambda i, idx: (0, i, 0)) for u in recvs],
        out_specs=[blk] * 4)
    return pl.pallas_call(
        body, name=name, grid_spec=grid_spec,
        out_shape=[jax.ShapeDtypeStruct((rows, cols), F32)] * 4,
        compiler_params=_params("arbitrary"),
    )(idx, w, m, v, part, got, *recvs)


def _pack(vectors):
    flat = jnp.concatenate([v.reshape(-1).astype(F32) for v in vectors])
    pad = (-flat.shape[0]) % (8 * LANES)
    return jnp.pad(flat, (0, pad)).reshape(-1, LANES)


def _unpack(flat, shapes):
    out, off = [], 0
    for shp in shapes:
        size = math.prod(shp)
        out.append(flat[..., off:off + size].reshape(flat.shape[:-1] + tuple(shp)))
        off += size
    return out


def _my_slice(full, me, axis):
    size = full.shape[axis] // N_DEV
    return lax.dynamic_slice_in_dim(full, me * size, size, axis)


def kernel(x, c, norm_g, ada_w, ada_b, sc_w_in, sc_conv_w, sc_w_out, lru_w_in, lru_conv_w, lru_conv_b, lru_w_a, lru_b_a, lru_w_x, lru_b_x, lru_lambda, lru_w_out, final_g, loss_target, m_norm_g, m_ada_w, m_ada_b, m_sc_w_in, m_sc_conv_w, m_sc_w_out, m_lru_w_in, m_lru_conv_w, m_lru_conv_b, m_lru_w_a, m_lru_b_a, m_lru_w_x, m_lru_b_x, m_lru_lambda, m_lru_w_out, m_final_g, v_norm_g, v_ada_w, v_ada_b, v_sc_w_in, v_sc_conv_w, v_sc_w_out, v_lru_w_in, v_lru_conv_w, v_lru_conv_b, v_lru_w_a, v_lru_b_a, v_lru_w_x, v_lru_b_x, v_lru_lambda, v_lru_w_out, v_final_g):
    _, s, d = x.shape
    e = sc_w_out.shape[1] * N_DEV
    heads, dh_s, dh = lru_w_a.shape[1:]
    es = e // N_DEV
    f = ada_w.shape[2]
    mx, my, mc = _position()
    me = 4 * mx + 2 * my + mc
    chip = 2 * mx + my
    idx = jnp.stack([chip ^ 1, chip ^ 2, chip ^ 3, chip, mc]).astype(jnp.int32)

    x0 = x[0]
    target = loss_target[0]

    small_shapes = [(d,), (3, es), (4, es), (es,), (heads, dh_s), (heads, dh_s), (es,)]
    small = _small_gather(_pack([c, sc_conv_w, lru_conv_w, lru_conv_b, lru_b_a, lru_b_x, lru_lambda]),
                          "gather_small_weights").reshape(N_DEV, -1)
    c_all, cw3, cw4, cb, ba, bx, lam = _unpack(small, small_shapes)
    cw3 = cw3.transpose(1, 0, 2).reshape(3, e)
    cw4 = cw4.transpose(1, 0, 2).reshape(4, e)
    cb = cb.reshape(1, e)
    lam = lam.reshape(1, e)
    ba = ba.transpose(1, 0, 2).reshape(1, e)
    bx = bx.transpose(1, 0, 2).reshape(1, e)

    mine = [sc_w_in[0], sc_w_out[0], lru_w_in[0], lru_w_a[0].reshape(heads * dh_s, dh),
            lru_w_x[0].reshape(heads * dh_s, dh), lru_w_out[0]]
    lands = [lax.dynamic_update_slice(lax.empty((N_DEV,) + sh.shape, BF16), sh.astype(BF16)[None], (me, 0, 0))
             for sh in mine]
    every = [1, 2, 3, 0]
    units = [([0], [0]), ([0], [1]), ([0], [2]), ([0], [3]), ([1], every), ([2], every), ([3, 4], every),
             ([5], every)]
    sems, first_ld, started = _gather_start(lands[:1], units[:3], [small], "gather_start_first")
    lands = first_ld + lands[1:]

    ada_b_mine = _my_slice(ada_b, me, 1).reshape(2, 1, f)
    mod_mine = _ada_mod(c_all, ada_w, ada_b_mine, "ada_mod")
    mod_all = _small_gather(_pack([mod_mine]), "gather_mod", deps=[started])

    def start_later(after):
        far_sems, far_ld, tok = _gather_start(lands[:1], units[3:4], after, "gather_start_far")
        rest_units = [([i - 1 for i in members], ks) for members, ks in units[4:]]
        rest_sems, rest_ld, tok = _gather_start(lands[1:], rest_units, [tok], "gather_start_rest")
        sems.extend(far_sems + rest_sems)
        lands[:] = far_ld + rest_ld
        return tok

    def gathered(u, after_forward, name):
        members, ks = units[u]
        fwd, lnd, token = _gather_forward([lands[i] for i in members], ks, sems[u][0], sems[u][1],
                                          after_forward, "gather_forward_" + name)
        for i, ld in zip(members, lnd):
            lands[i] = ld

        def finish(after):
            out = _gather_finish([lands[i] for i in members], ks, fwd, after, "gather_finish_" + name)
            for i, ld in zip(members, out):
                lands[i] = ld
            return out

        return token, finish

    tok, finish_y = gathered(1, [mod_all], "sc_w_in_near_y")
    tok, finish_x = gathered(2, [tok], "sc_w_in_near_x")
    queued = start_later([tok])

    mod_all = mod_all.reshape(N_DEV, -1)
    mod_all = mod_all[:, :2 * N_DEV * f].reshape(N_DEV, 2, N_DEV, f)
    mod_all = mod_all.transpose(1, 2, 0, 3).reshape(2, N_DEV, 3 * d)
    mod = lax.dynamic_index_in_dim(mod_all, me, 1, keepdims=False)
    shift = [mod[l:l + 1, 0:d] for l in range(2)]
    scale = [mod[l:l + 1, d:2 * d] for l in range(2)]
    gate = [mod[l:l + 1, 2 * d:3 * d] for l in range(2)]
    ng = [norm_g[l:l + 1] for l in range(2)]
    fg = final_g.reshape(1, d)

    h0 = _norm_mod(x0, ng[0], scale[0], shift[0], "norm_mod_0", deps=[queued])
    proj0 = lax.empty((4, s, e), BF16)
    tok, _ = gathered(0, [h0], "sc_w_in_own")
    proj0 = _mm_proj_group(h0, lands[0], idx, 3, proj0, "mm_proj_0_own", deps=[tok])
    for u, name, finish in ((1, "near_y", finish_y), (2, "near_x", finish_x), (3, "far", None)):
        after = [proj0]
        if finish is None:
            tok, finish = gathered(u, [proj0], "sc_w_in_" + name)
            after = [tok]
        wg_in0, = finish(after)
        proj0 = _mm_proj_group(h0, wg_in0, idx, u - 1, proj0, "mm_proj_0_" + name)
    tok, finish = gathered(4, [proj0], "sc_w_out")
    yb0 = _sc_fwd(proj0, cw3, "sc_fwd", deps=[tok])
    w_out0 = finish([yb0])[0].reshape(e, d)
    x1, y0 = _mm_out(yb0, w_out0, x0, gate[0], "mm_out_0")
    tok, finish = gathered(5, [x1], "lru_w_in")
    h1 = _norm_mod(x1, ng[1], scale[1], shift[1], "norm_mod_1", deps=[tok])
    wg_in1, = finish([h1])
    proj1 = _mm_proj(h1, wg_in1, 2, "mm_proj_1")
    tok, finish = gathered(6, [proj1], "lru_gates")
    wg_a, wg_x = finish([tok])
    w_a = wg_a.reshape(N_DEV, heads, dh_s, dh).transpose(1, 0, 2, 3).reshape(heads, dh, dh)
    w_x = wg_x.reshape(N_DEV, heads, dh_s, dh).transpose(1, 0, 2, 3).reshape(heads, dh, dh)
    tok, finish = gathered(7, [w_a, w_x], "lru_w_out")
    yb1, hs = _lru_fwd(proj1, cw4, cb, w_a, ba, w_x, bx, lam, "lru_fwd", deps=[tok])
    w_out1 = finish([yb1])[0].reshape(e, d)
    x2, y1 = _mm_out(yb1, w_out1, x1, gate[1], "mm_out_1")
    dx2, loss_part, d_fg, dy1, dgate1 = _final_loss(x2, fg, target, y1, gate[1], "final_loss")

    def pieces(g, rows, cols):
        return g.reshape(4, 2, rows, cols)

    def by_rows(g):
        return g.reshape(heads, N_DEV, dh_s, dh).transpose(1, 0, 2, 3).reshape(N_DEV, heads * dh_s, dh)

    def pair_begin(parts, group):
        send, recv, parts, lnd, token = _pair_start(parts, "pair_start_" + group)
        return dict(parts=parts, lands=lnd, send=send, recv=recv, group=group), token

    def scatter_start(pair, names, after):
        group = pair["group"]
        parts, gots = _pair_wait(pair["parts"], pair["lands"], pair["send"], pair["recv"], after,
                                 "pair_wait_" + group)
        sums = [_pair_sum(idx, p, q, "pair_sum_" + nm) for p, q, nm in zip(parts, gots, names)]
        empties = [lax.empty(sm.shape, sm.dtype) for sm in sums]
        send, recv, sums, lnd, token = _chip_start(sums, empties, "chip_start_" + group)
        return dict(parts=parts, gots=gots, names=names, group=group, sums=sums, lands=lnd,
                    send=send, recv=recv), token

    big = {"sc_w_in": (sc_w_in, m_sc_w_in, v_sc_w_in), "sc_w_out": (sc_w_out, m_sc_w_out, v_sc_w_out),
           "lru_w_in": (lru_w_in, m_lru_w_in, v_lru_w_in), "lru_w_a": (lru_w_a, m_lru_w_a, v_lru_w_a),
           "lru_w_x": (lru_w_x, m_lru_w_x, v_lru_w_x), "lru_w_out": (lru_w_out, m_lru_w_out, v_lru_w_out)}
    big_res = {}

    def scatter_finish(rs, after):
        recvs = _chip_wait(rs["sums"], rs["lands"], rs["send"], rs["recv"], after, "chip_wait_" + rs["group"])
        done = []
        for p, q, u, nm in zip(rs["parts"], rs["gots"], recvs, rs["names"]):
            w, m, v = big[nm]
            shp2 = p.shape[2:]
            res = _adamw_reduced(idx, w.reshape(shp2), m.reshape(shp2), v.reshape(shp2), p, q, [u], "adamw_" + nm)
            big_res[nm] = [r.reshape(w.shape) for r in res]
            done.append(res[1])
        return done

    dw_out1 = _mm_tn(yb1, dy1[None], 1, "mm_dw_out_1")
    pair, tok = pair_begin([pieces(dw_out1, es, d)], "lru_w_out")
    dyb1 = _mm_nt(dy1[None], w_out1[None], BF16, "mm_dyb_1", deps=[tok])
    rs1, tok = scatter_start(pair, ["lru_w_out"], [dyb1])
    dproj1, dw_a, dw_x, vecs1 = _lru_bwd(proj1, hs, dyb1, cw4, cb, w_a, ba, w_x, bx, lam, "lru_bwd", deps=[tok])
    done = scatter_finish(rs1, [dproj1])
    dw_in1 = _mm_tn(h1, dproj1, N_DEV, "mm_dw_in_1", deps=done)
    pair, tok = pair_begin([pieces(dw_in1, d, 2 * es), pieces(by_rows(dw_a), heads * dh_s, dh),
                            pieces(by_rows(dw_x), heads * dh_s, dh)], "lru_in")
    dh1 = _mm_nt(dproj1, wg_in1, BF16, "mm_dh_1", deps=[tok])
    rs2, tok = scatter_start(pair, ["lru_w_in", "lru_w_a", "lru_w_x"], [dh1])
    dx1, dscale1, dshift1, dng1, dy0, dgate0 = _norm_mod_bwd(x1, dh1, dx2, ng[1], scale[1], "norm_mod_bwd_1",
                                                             below=(y0, gate[0]), deps=[tok])
    dw_out0 = _mm_tn(yb0, dy0[None], 1, "mm_dw_out_0")
    pair, tok = pair_begin([pieces(dw_out0, es, d)], "sc_w_out")
    dyb0 = _mm_nt(dy0[None], w_out0[None], BF16, "mm_dyb_0", deps=[tok])
    rs3, tok = scatter_start(pair, ["sc_w_out"], [dyb0])
    dproj0, vecs0 = _sc_bwd(proj0, dyb0, cw3, "sc_bwd", deps=[tok])
    idx_one = jnp.stack([jnp.zeros_like(mc)] * 4 + [mc]).astype(jnp.int32)
    sc_w_in_steps = []

    def chip_step(j, pair, after):
        (part,), (got,) = _pair_wait(pair["parts"], pair["lands"], pair["send"], pair["recv"], after,
                                     "pair_wait_sc_w_in_%d" % j)
        sm = _pair_sum(idx_one, part, got, "pair_sum_sc_w_in_%d" % j, nslots=1)
        send, recv, sums, lnd, token = _chip_start([sm], [lax.empty(sm.shape, sm.dtype)],
                                                   "chip_start_sc_w_in_%d" % j, flips=(j,))
        sc_w_in_steps.append((sums, lnd, send, recv, j))
        return token

    pending, done = None, []
    for j in (3, 2, 1, 0):
        part = _mm_tn_group(h0, dproj0, idx, (j - 1) % 4, 2, "mm_dw_in_0_%d" % j, deps=done)[None]
        pair, tok = pair_begin([part], "sc_w_in_%d" % j)
        if j == 3:
            done = [chip_step(j, pair, [tok])]
            continue
        done = [tok]
        if pending is not None:
            done.append(chip_step(pending[0], pending[1], [tok]))
        pending = (j, pair)
    done += scatter_finish(rs2, done)
    dh0 = _mm_nt(dproj0, wg_in0, BF16, "mm_dh_0", deps=done)
    pair = pending[1]
    (part,), (got,) = _pair_wait(pair["parts"], pair["lands"], pair["send"], pair["recv"], [dh0],
                                 "pair_wait_sc_w_in_0")
    dx0, dscale0, dshift0, dng0 = _norm_mod_bwd(x0, dh0, dx1, ng[0], scale[0], "norm_mod_bwd_0")
    done = scatter_finish(rs3, [dx0])
    dmod_mine = jnp.concatenate([dshift0, dscale0, dgate0, dshift1, dscale1, dgate1], axis=1)
    end_shapes = [(LANES,), (2, 3 * d), (2, d), (d,), (8, e), (16, e)]
    end_all = _small_gather(
        _pack([loss_part, dmod_mine, jnp.concatenate([dng0, dng1], axis=0), d_fg, vecs0, vecs1]),
        "gather_small_grads", deps=done)
    end_sum = _device_sum(end_all, "sum_small_grads").reshape(-1)
    loss_v, g_ada_b, g_norm_g, g_final_g, sum0, sum1 = _unpack(end_sum, end_shapes)
    loss = loss_v[0]
    dmod_all = _unpack(end_all.reshape(N_DEV, -1), end_shapes)[1].transpose(1, 0, 2)
    dmod_cols = _my_slice(dmod_all, me, 2)
    ada_out = _ada_update(c_all.T, dmod_cols, ada_w, m_ada_w, v_ada_w, "ada_update")

    g_sc_conv_w = _my_slice(sum0[0:3], me, 1)
    g_lru_b_a = _my_slice(sum1[0].reshape(heads, dh), me, 1)
    g_lru_b_x = _my_slice(sum1[1].reshape(heads, dh), me, 1)
    g_lru_lambda = _my_slice(sum1[2:3], me, 1)
    g_lru_conv_b = _my_slice(sum1[3:4], me, 1)
    g_lru_conv_w = _my_slice(sum1[4:8], me, 1)

    small_w = [norm_g, ada_b, final_g, sc_conv_w, lru_conv_w, lru_conv_b, lru_b_a, lru_b_x, lru_lambda]
    small_m = [m_norm_g, m_ada_b, m_final_g, m_sc_conv_w, m_lru_conv_w, m_lru_conv_b, m_lru_b_a, m_lru_b_x,
               m_lru_lambda]
    small_v = [v_norm_g, v_ada_b, v_final_g, v_sc_conv_w, v_lru_conv_w, v_lru_conv_b, v_lru_b_a, v_lru_b_x,
               v_lru_lambda]
    small_g = [g_norm_g, g_ada_b, g_final_g, g_sc_conv_w, g_lru_conv_w, g_lru_conv_b, g_lru_b_a, g_lru_b_x,
               g_lru_lambda]
    small_g = [g.reshape(w.shape) for g, w in zip(small_g, small_w)]
    shapes = [w.shape for w in small_w]
    packed = _adamw(_pack(small_w), _pack(small_g), _pack(small_m), _pack(small_v), "adamw_small")
    small_out = [small_g] + [_unpack(p.reshape(-1), shapes) for p in packed]

    after = [packed[0], ada_out[1]]
    recvs = []
    for sums, lnd, send, recv, j in sc_w_in_steps:
        recvs += _chip_wait(sums, lnd, send, recv, after, "chip_wait_sc_w_in_%d" % j)
    shp2 = part.shape[2:]
    res = _adamw_reduced(idx_one, sc_w_in.reshape(shp2), m_sc_w_in.reshape(shp2), v_sc_w_in.reshape(shp2),
                         part, got, recvs, "adamw_sc_w_in")
    big_res["sc_w_in"] = [r.reshape(sc_w_in.shape) for r in res]
    big_out = [big_res[nm] for nm in ("sc_w_in", "sc_w_out", "lru_w_in", "lru_w_a", "lru_w_x", "lru_w_out")]

    def small(kind, i):
        return small_out[kind][i]

    def bigw(kind, i):
        return big_out[i][kind]

    outs = [loss, dx0[None]]
    for kind in range(4):
        outs += [small(kind, 0), ada_out[kind], small(kind, 1), bigw(kind, 0), small(kind, 3), bigw(kind, 1),
                 bigw(kind, 2), small(kind, 4), small(kind, 5), bigw(kind, 3), small(kind, 6), bigw(kind, 4),
                 small(kind, 7), small(kind, 8), bigw(kind, 5), small(kind, 2)]
    return tuple(outs)
```

```python
import math

import jax
import jax.numpy as jnp
from jax import lax
from jax.experimental import pallas as pl
from jax.experimental.pallas import tpu as pltpu

N_DEV = 8
LANES = 128
EPS = 1e-6
RGLRU_C = 8.0
ADAM_LR = 0.001
ADAM_B1 = 0.9
ADAM_B2 = 0.999
ADAM_EPS = 1e-08
ADAM_WD = 0.01
ADAM_STEP = 10
VMEM_LIMIT = 56 * 1024 * 1024
MESH = pl.DeviceIdType.MESH
F32 = jnp.float32
BF16 = jnp.bfloat16
ANY = pl.BlockSpec(memory_space=pl.ANY)
HBM = pl.BlockSpec(memory_space=pltpu.HBM)
SEM = pl.BlockSpec(memory_space=pltpu.SEMAPHORE)
VMEM_SPEC = pl.BlockSpec(memory_space=pltpu.VMEM)
EFFECT = pltpu.SideEffectType.DATAFLOW_SIDE_EFFECTING
TOKEN = jax.ShapeDtypeStruct((8, LANES), jnp.float32)


def _tile(n, pref):
    t = min(n, pref)
    assert n % t == 0, (n, pref)
    return t


def _params(*sem):
    return pltpu.CompilerParams(dimension_semantics=sem, vmem_limit_bytes=VMEM_LIMIT)


def _position():
    return lax.axis_index("x"), lax.axis_index("y"), lax.axis_index("c")


def _flip(x, y, k):
    return (1 - x if k & 2 else x), (1 - y if k & 1 else y)


def _after(body, n_in, deps):
    if not deps:
        return body

    def wrapped(*refs):
        return body(*refs[:n_in], *refs[n_in + len(deps):])

    return wrapped


def _small_gather(v, name, deps=()):
    rows = v.shape[0]

    def body(v_ref, out_ref, send_sems, recv_sems):
        x, y, c = _position()
        me = 4 * x + 2 * y + c
        out_ref[me] = v_ref[...]
        copies = []
        for k in range(1, N_DEV):
            px, py = _flip(x, y, k >> 1)
            pc = 1 - c if k & 1 else c
            cp = pltpu.make_async_remote_copy(
                src_ref=v_ref, dst_ref=out_ref.at[me],
                send_sem=send_sems.at[k - 1], recv_sem=recv_sems.at[k - 1],
                device_id=(px, py, pc), device_id_type=MESH)
            cp.start()
            copies.append((cp, 4 * px + 2 * py + pc))
        for k, (cp, peer) in enumerate(copies):
            pltpu.make_async_remote_copy(
                src_ref=v_ref, dst_ref=out_ref.at[peer],
                send_sem=send_sems.at[k], recv_sem=recv_sems.at[k],
                device_id=(x, y, c), device_id_type=MESH).wait_recv()
        for cp, _ in copies:
            cp.wait_send()

    return pl.pallas_call(
        _after(body, 1, deps), name=name,
        out_shape=jax.ShapeDtypeStruct((N_DEV, rows, LANES), F32),
        in_specs=[VMEM_SPEC] + [ANY] * len(deps), out_specs=VMEM_SPEC,
        scratch_shapes=[pltpu.SemaphoreType.DMA((N_DEV - 1,)),
                        pltpu.SemaphoreType.DMA((N_DEV - 1,))],
        compiler_params=pltpu.CompilerParams(vmem_limit_bytes=VMEM_LIMIT),
    )(v, *deps)


def _hbm(a):
    return pltpu.with_memory_space_constraint(a, pltpu.HBM)


def _hbm_like(arrays):
    return [pltpu.HBM(a.shape, a.dtype) for a in arrays]


def _remote(src, dst, send, recv, to):
    return pltpu.make_async_remote_copy(src_ref=src, dst_ref=dst, send_sem=send, recv_sem=recv,
                                        device_id=to, device_id_type=MESH)


def _gather_start(lands, units, after, name):
    n, nu = len(lands), len(units)

    def body(*refs):
        lnd = refs[:n]
        sems = refs[n + len(after):n + len(after) + 2 * nu]
        token = refs[-1]
        x, y, c = _position()
        me = 4 * x + 2 * y + c
        targets = [(x, y, 1 - c)] + [(px, py, c) for px, py in (_flip(x, y, k) for k in (1, 2, 3))]
        for u, (members, ks) in enumerate(units):
            for slot, i in enumerate(members):
                for ki, k in enumerate(ks):
                    at = len(ks) * slot + ki
                    mine = lnd[i].at[me]
                    _remote(mine, mine, sems[2 * u].at[at], sems[2 * u + 1].at[at], targets[k]).start()
        token[...] = jnp.zeros_like(token)

    sem_shapes = []
    for members, ks in units:
        count = len(members) * len(ks)
        sem_shapes += [pltpu.SemaphoreType.DMA((count,)), pltpu.SemaphoreType.DMA((count,))]
    out = pl.pallas_call(
        body, name=name,
        out_shape=sem_shapes + _hbm_like(lands) + [TOKEN],
        in_specs=[HBM] * n + [ANY] * len(after),
        out_specs=[SEM] * (2 * nu) + [HBM] * n + [VMEM_SPEC],
        input_output_aliases={i: 2 * nu + i for i in range(n)},
        compiler_params=pltpu.CompilerParams(has_side_effects=EFFECT),
    )(*[_hbm(l) for l in lands], *after)
    sems = [(out[2 * u], out[2 * u + 1]) for u in range(nu)]
    return sems, list(out[2 * nu:2 * nu + n]), out[-1]


def _gather_forward(lands, ks, send, recv, after, name):
    m = len(lands)
    hops = [k for k in ks if k]
    nsem = 2 if hops else 0

    def body(*refs):
        lnd = refs[:m]
        send_ref, recv_ref = refs[m], refs[m + 1]
        outs = refs[m + 2 + len(after):]
        token = refs[-1]
        x, y, c = _position()
        me = (x, y, c)
        for slot in range(m):
            for ki, k in enumerate(ks):
                at = len(ks) * slot + ki
                if k:
                    px, py = _flip(x, y, k)
                    block = lnd[slot].at[4 * px + 2 * py + c]
                else:
                    block = lnd[slot].at[4 * x + 2 * y + (1 - c)]
                arrival = _remote(lnd[slot].at[4 * x + 2 * y + c], block, send_ref.at[at], recv_ref.at[at], me)
                arrival.wait_recv()
                if k:
                    fat = len(hops) * slot + hops.index(k)
                    _remote(block, block, outs[0].at[fat], outs[1].at[fat], (x, y, 1 - c)).start()
                arrival.wait_send()
        token[...] = jnp.zeros_like(token)

    count = len(hops) * m
    sem_shapes = [pltpu.SemaphoreType.DMA((count,)), pltpu.SemaphoreType.DMA((count,))] if hops else []
    out = pl.pallas_call(
        body, name=name,
        out_shape=sem_shapes + _hbm_like(lands) + [TOKEN],
        in_specs=[HBM] * m + [SEM, SEM] + [ANY] * len(after),
        out_specs=[SEM] * nsem + [HBM] * m + [VMEM_SPEC],
        input_output_aliases={i: nsem + i for i in range(m)},
        compiler_params=pltpu.CompilerParams(has_side_effects=EFFECT),
    )(*lands, send, recv, *after)
    fwd = (out[0], out[1]) if hops else None
    return fwd, list(out[nsem:nsem + m]), out[-1]


def _gather_finish(lands, ks, fwd, after, name):
    m = len(lands)
    hops = [k for k in ks if k]

    def body(*refs):
        lnd = refs[:m]
        fsend_ref, frecv_ref = refs[m], refs[m + 1]
        x, y, c = _position()
        for slot in range(m):
            for fi, k in enumerate(hops):
                px, py = _flip(x, y, k)
                sent = lnd[slot].at[4 * px + 2 * py + c]
                came = lnd[slot].at[4 * px + 2 * py + (1 - c)]
                fat = len(hops) * slot + fi
                cp = _remote(sent, came, fsend_ref.at[fat], frecv_ref.at[fat], (x, y, c))
                cp.wait_recv()
                cp.wait_send()

    out = pl.pallas_call(
        body, name=name,
        out_shape=_hbm_like(lands),
        in_specs=[HBM] * m + [SEM, SEM] + [ANY] * len(after), out_specs=[HBM] * m,
        input_output_aliases={i: i for i in range(m)},
        compiler_params=pltpu.CompilerParams(has_side_effects=EFFECT),
    )(*lands, fwd[0], fwd[1], *after)
    return list(out)


def _pair_start(parts, name):
    n = len(parts)
    lands = [lax.empty((p.shape[0], 1) + p.shape[2:], p.dtype) for p in parts]

    def body(*refs):
        ins, lnd = refs[:n], refs[n:2 * n]
        send_ref, recv_ref = refs[2 * n], refs[2 * n + 1]
        token = refs[-1]
        x, y, c = _position()
        for i in range(n):
            _remote(ins[i].at[:, pl.ds(1 - c, 1)], lnd[i], send_ref.at[i], recv_ref.at[i], (x, y, 1 - c)).start()
        token[...] = jnp.zeros_like(token)

    out = pl.pallas_call(
        body, name=name,
        out_shape=[pltpu.SemaphoreType.DMA((n,)), pltpu.SemaphoreType.DMA((n,))]
        + _hbm_like(parts) + _hbm_like(lands) + [TOKEN],
        in_specs=[HBM] * (2 * n), out_specs=[SEM, SEM] + [HBM] * (2 * n) + [VMEM_SPEC],
        input_output_aliases={i: 2 + i for i in range(2 * n)},
        compiler_params=pltpu.CompilerParams(has_side_effects=EFFECT),
    )(*[_hbm(p) for p in parts], *[_hbm(l) for l in lands])
    return out[0], out[1], list(out[2:2 + n]), list(out[2 + n:2 + 2 * n]), out[-1]


def _pair_wait(parts, lands, send, recv, after, name):
    n = len(parts)

    def body(*refs):
        ins, lnd = refs[:n], refs[n:2 * n]
        send_ref, recv_ref = refs[2 * n], refs[2 * n + 1]
        x, y, c = _position()
        for i in range(n):
            cp = _remote(ins[i].at[:, pl.ds(1 - c, 1)], lnd[i], send_ref.at[i], recv_ref.at[i], (x, y, c))
            cp.wait_recv()
            cp.wait_send()

    out = pl.pallas_call(
        body, name=name,
        out_shape=_hbm_like(parts) + _hbm_like(lands),
        in_specs=[HBM] * (2 * n) + [SEM, SEM] + [ANY] * len(after), out_specs=[HBM] * (2 * n),
        input_output_aliases={i: i for i in range(2 * n)},
        compiler_params=pltpu.CompilerParams(has_side_effects=EFFECT),
    )(*parts, *lands, send, recv, *after)
    return list(out[:n]), list(out[n:])


def _chip_start(sums, lands, name, flips=(1, 2, 3)):
    n, ns = len(sums), len(flips)

    def body(*refs):
        ins, lnd = refs[:n], refs[n:2 * n]
        send_ref, recv_ref = refs[2 * n], refs[2 * n + 1]
        token = refs[-1]
        x, y, c = _position()
        for i in range(n):
            for j, flip in enumerate(flips):
                px, py = _flip(x, y, flip)
                _remote(ins[i].at[j], lnd[i].at[j], send_ref.at[ns * i + j], recv_ref.at[ns * i + j],
                        (px, py, c)).start()
        token[...] = jnp.zeros_like(token)

    out = pl.pallas_call(
        body, name=name,
        out_shape=[pltpu.SemaphoreType.DMA((ns * n,)), pltpu.SemaphoreType.DMA((ns * n,))]
        + _hbm_like(sums) + _hbm_like(lands) + [TOKEN],
        in_specs=[HBM] * (2 * n), out_specs=[SEM, SEM] + [HBM] * (2 * n) + [VMEM_SPEC],
        input_output_aliases={i: 2 + i for i in range(2 * n)},
        compiler_params=pltpu.CompilerParams(has_side_effects=EFFECT),
    )(*[_hbm(s) for s in sums], *[_hbm(l) for l in lands])
    return out[0], out[1], out[2:2 + n], out[2 + n:2 + 2 * n], out[-1]


def _chip_wait(sums, lands, send, recv, after, name):
    n, ns = len(sums), sums[0].shape[0]

    def body(*refs):
        ins, lnd = refs[:n], refs[n:2 * n]
        send_ref, recv_ref = refs[2 * n], refs[2 * n + 1]
        x, y, c = _position()
        for i in range(n):
            for j in range(ns):
                cp = _remote(ins[i].at[j], lnd[i].at[j], send_ref.at[ns * i + j], recv_ref.at[ns * i + j], (x, y, c))
                cp.wait_recv()
                cp.wait_send()

    out = pl.pallas_call(
        body, name=name,
        out_shape=_hbm_like(sums) + _hbm_like(lands),
        in_specs=[HBM] * (2 * n) + [SEM, SEM] + [ANY] * len(after), out_specs=[HBM] * (2 * n),
        input_output_aliases={i: i for i in range(2 * n)},
        compiler_params=pltpu.CompilerParams(has_side_effects=EFFECT),
    )(*sums, *lands, send, recv, *after)
    return list(out[n:])


def _pair_sum(idx, part, got, name, nslots=3):
    _, _, rows, cols = part.shape
    tr = _tile(rows, 1024)

    def body(idx_ref, p_ref, q_ref, o_ref):
        o_ref[...] = (p_ref[...].astype(F32) + q_ref[...].astype(F32)).astype(o_ref.dtype)

    grid_spec = pltpu.PrefetchScalarGridSpec(
        num_scalar_prefetch=1, grid=(nslots, rows // tr),
        in_specs=[pl.BlockSpec((None, None, tr, cols), lambda j, r, idx: (idx[j], idx[4], r, 0)),
                  pl.BlockSpec((None, None, tr, cols), lambda j, r, idx: (idx[j], 0, r, 0))],
        out_specs=pl.BlockSpec((None, tr, cols), lambda j, r, idx: (j, r, 0)))
    return pl.pallas_call(
        body, name=name, grid_spec=grid_spec,
        out_shape=jax.ShapeDtypeStruct((nslots, rows, cols), part.dtype),
        compiler_params=_params("arbitrary", "arbitrary"),
    )(idx, part, got)


def _mm_proj(h, wg, groups, name):
    s, k = h.shape
    nchunk, _, n = wg.shape
    e = nchunk * n // groups
    tn = _tile(min(n, e), 512)

    def body(h_ref, w_ref, o_ref):
        o_ref[...] = jnp.dot(h_ref[...], w_ref[...], preferred_element_type=F32).astype(o_ref.dtype)

    return pl.pallas_call(
        body, name=name, grid=(nchunk * n // tn,),
        in_specs=[pl.BlockSpec((s, k), lambda j: (0, 0)),
                  pl.BlockSpec((None, k, tn), lambda j: ((j * tn) // n, 0, ((j * tn) % n) // tn))],
        out_specs=pl.BlockSpec((None, s, tn), lambda j: ((j * tn) // e, 0, ((j * tn) % e) // tn)),
        out_shape=jax.ShapeDtypeStruct((groups, s, e), BF16),
        compiler_params=_params("arbitrary"),
    )(h, wg)


def _mm_proj_group(h, wg, idx, pos, prev, name, deps=()):
    s, k = h.shape
    _, _, n = wg.shape
    _, _, e = prev.shape
    tn = _tile(n, 512)
    nd = len(deps)

    def body(idx_ref, h_ref, w_ref, prev_ref, *rest):
        o_ref = rest[nd]
        o_ref[...] = jnp.dot(h_ref[...], w_ref[...], preferred_element_type=F32).astype(o_ref.dtype)

    def col(j, idx):
        return idx[pos] * (2 * n) + j * tn

    grid_spec = pltpu.PrefetchScalarGridSpec(
        num_scalar_prefetch=1, grid=(2 * n // tn,),
        in_specs=[pl.BlockSpec((s, k), lambda j, idx: (0, 0)),
                  pl.BlockSpec((None, k, tn), lambda j, idx: (col(j, idx) // n, 0, (col(j, idx) % n) // tn)),
                  ANY] + [ANY] * nd,
        out_specs=pl.BlockSpec((None, s, tn), lambda j, idx: (col(j, idx) // e, 0, (col(j, idx) % e) // tn)))
    return pl.pallas_call(
        body, name=name, grid_spec=grid_spec,
        out_shape=jax.ShapeDtypeStruct(prev.shape, prev.dtype),
        input_output_aliases={3: 0},
        compiler_params=_params("arbitrary"),
    )(idx, h, wg, prev, *deps)


def _mm_out(yb, w, x, gate, name):
    s, k = yb.shape
    d = w.shape[1]
    tn = _tile(d, 512)
    tk = _tile(k, 2048)
    nk = k // tk

    def body(a_ref, w_ref, x_ref, g_ref, xo_ref, y_ref, acc_ref):
        kk = pl.program_id(1)

        @pl.when(kk == 0)
        def _():
            acc_ref[...] = jnp.zeros_like(acc_ref)

        acc_ref[...] += jnp.dot(a_ref[...], w_ref[...], preferred_element_type=F32)

        @pl.when(kk == nk - 1)
        def _():
            y = acc_ref[...]
            y_ref[...] = y.astype(y_ref.dtype)
            xo_ref[...] = x_ref[...] + g_ref[...] * y

    return pl.pallas_call(
        body, name=name, grid=(d // tn, nk),
        in_specs=[pl.BlockSpec((s, tk), lambda j, kk: (0, kk)),
                  pl.BlockSpec((tk, tn), lambda j, kk: (kk, j)),
                  pl.BlockSpec((s, tn), lambda j, kk: (0, j)),
                  pl.BlockSpec((1, tn), lambda j, kk: (0, j))],
        out_specs=[pl.BlockSpec((s, tn), lambda j, kk: (0, j)),
                   pl.BlockSpec((s, tn), lambda j, kk: (0, j))],
        out_shape=[jax.ShapeDtypeStruct((s, d), F32), jax.ShapeDtypeStruct((s, d), BF16)],
        scratch_shapes=[pltpu.VMEM((s, tn), F32)],
        compiler_params=_params("arbitrary", "arbitrary"),
    )(yb, w, x, gate)


def _mm_nt(a3, w3, out_dtype, name, deps=()):
    g, s, ea = a3.shape
    cw, n, nw = w3.shape
    total = g * ea
    assert total == cw * nw
    tk = _tile(min(ea, nw), 2048)
    tn = _tile(n, 1024)
    nk = total // tk

    def body(a_ref, w_ref, o_ref, acc_ref):
        kk = pl.program_id(1)

        @pl.when(kk == 0)
        def _():
            acc_ref[...] = jnp.zeros_like(acc_ref)

        acc_ref[...] += lax.dot_general(a_ref[...], w_ref[...], (((1,), (1,)), ((), ())),
                                        preferred_element_type=F32)

        @pl.when(kk == nk - 1)
        def _():
            o_ref[...] = acc_ref[...].astype(o_ref.dtype)

    return pl.pallas_call(
        _after(body, 2, deps), name=name, grid=(n // tn, nk),
        in_specs=[pl.BlockSpec((None, s, tk), lambda j, kk: ((kk * tk) // ea, 0, ((kk * tk) % ea) // tk)),
                  pl.BlockSpec((None, tn, tk), lambda j, kk: ((kk * tk) // nw, j, ((kk * tk) % nw) // tk))]
        + [ANY] * len(deps),
        out_specs=pl.BlockSpec((s, tn), lambda j, kk: (0, j)),
        out_shape=jax.ShapeDtypeStruct((s, n), out_dtype),
        scratch_shapes=[pltpu.VMEM((s, tn), F32)],
        compiler_params=_params("arbitrary", "arbitrary"),
    )(a3, w3, *deps)


def _mm_tn(a, b3, nchunk, name, deps=()):
    s, ka = a.shape
    g, _, eb = b3.shape
    n = g * eb // nchunk
    tm = _tile(ka, 1024)
    tn = _tile(min(n, eb), 1024)

    def body(a_ref, b_ref, o_ref, at_ref):
        @pl.when(pl.program_id(1) == 0)
        def _():
            at_ref[...] = a_ref[...].astype(F32).T.astype(at_ref.dtype)

        o_ref[...] = jnp.dot(at_ref[...], b_ref[...], preferred_element_type=F32).astype(o_ref.dtype)

    return pl.pallas_call(
        _after(body, 2, deps), name=name, grid=(ka // tm, g * eb // tn),
        in_specs=[pl.BlockSpec((s, tm), lambda i, j: (0, i)),
                  pl.BlockSpec((None, s, tn), lambda i, j: ((j * tn) // eb, 0, ((j * tn) % eb) // tn))]
        + [ANY] * len(deps),
        out_specs=pl.BlockSpec((None, tm, tn), lambda i, j: ((j * tn) // n, i, ((j * tn) % n) // tn)),
        out_shape=jax.ShapeDtypeStruct((nchunk, ka, n), BF16),
        scratch_shapes=[pltpu.VMEM((tm, s), BF16)],
        compiler_params=_params("arbitrary", "arbitrary"),
    )(a, b3, *deps)


def _mm_tn_group(a, b3, idx, pos, nchunk, name, deps=()):
    s, ka = a.shape
    _, _, eb = b3.shape
    n = eb // nchunk
    tm = _tile(ka, 1024)
    tn = _tile(n, 1024)
    nd = len(deps)

    def body(idx_ref, a_ref, b_ref, *rest):
        o_ref, at_ref = rest[nd:]

        @pl.when(pl.program_id(1) == 0)
        def _():
            at_ref[...] = a_ref[...].astype(F32).T.astype(at_ref.dtype)

        o_ref[...] = jnp.dot(at_ref[...], b_ref[...], preferred_element_type=F32).astype(o_ref.dtype)

    grid_spec = pltpu.PrefetchScalarGridSpec(
        num_scalar_prefetch=1, grid=(ka // tm, eb // tn),
        in_specs=[pl.BlockSpec((s, tm), lambda i, j, idx: (0, i)),
                  pl.BlockSpec((None, s, tn), lambda i, j, idx: (idx[pos], 0, j))] + [ANY] * nd,
        out_specs=pl.BlockSpec((None, tm, tn), lambda i, j, idx: ((j * tn) // n, i, ((j * tn) % n) // tn)),
        scratch_shapes=[pltpu.VMEM((tm, s), BF16)])
    return pl.pallas_call(
        body, name=name, grid_spec=grid_spec,
        out_shape=jax.ShapeDtypeStruct((nchunk, ka, n), BF16),
        compiler_params=_params("arbitrary", "arbitrary"),
    )(idx, a, b3, *deps)


def _sigmoid(z):
    return jax.nn.sigmoid(z)


def _shift_down(v, k, fill=0.0, period=None):
    if k == 0:
        return v
    row = lax.broadcasted_iota(jnp.int32, v.shape, 0)
    if period is not None:
        row = row & (period - 1)
    return jnp.where(row >= k, pltpu.roll(v, k, 0), fill)


def _shift_up(v, k, fill=0.0, period=None):
    if k == 0:
        return v
    s = v.shape[0]
    row = lax.broadcasted_iota(jnp.int32, v.shape, 0)
    if period is not None:
        row, s = row & (period - 1), period
    return jnp.where(row < s - k, pltpu.roll(v, v.shape[0] - k, 0), fill)


SCAN_BLOCK = 64


def _scan(a, b, shift):
    s = a.shape[0]
    blk = min(SCAN_BLOCK, s)
    k = 1
    while k < blk:
        b = a * shift(b, k, 0.0, blk) + b
        a = a * shift(a, k, 1.0, blk)
        k *= 2
    nblk = s // blk
    forward = shift is _shift_down
    order = range(nblk) if forward else range(nblk - 1, -1, -1)
    edge = blk - 1 if forward else 0
    out = [None] * nblk
    carry = None
    for i in order:
        h = b[i * blk:(i + 1) * blk]
        if carry is not None:
            h = a[i * blk:(i + 1) * blk] * carry + h
        carry = h[edge:edge + 1]
        out[i] = h
    return jnp.concatenate(out, axis=0) if nblk > 1 else out[0]


def _norm_mod(x, g, scale, shift, name, deps=()):
    s, d = x.shape
    ts = _tile(s, 256)

    def body(x_ref, g_ref, sc_ref, sh_ref, h_ref):
        xv = x_ref[...]
        rstd = lax.rsqrt(jnp.mean(xv * xv, axis=-1, keepdims=True) + EPS)
        nrm = xv * rstd * g_ref[...]
        h_ref[...] = (nrm * (1.0 + sc_ref[...]) + sh_ref[...]).astype(h_ref.dtype)

    vec = pl.BlockSpec((1, d), lambda i: (0, 0))
    return pl.pallas_call(
        _after(body, 4, deps), name=name, grid=(s // ts,),
        in_specs=[pl.BlockSpec((ts, d), lambda i: (i, 0)), vec, vec, vec] + [ANY] * len(deps),
        out_specs=pl.BlockSpec((ts, d), lambda i: (i, 0)),
        out_shape=jax.ShapeDtypeStruct((s, d), BF16),
        compiler_params=_params("arbitrary"),
    )(x, g, scale, shift, *deps)


def _gate_terms(dx, y_ref, gate_ref, dy_ref, dgate_ref):
    dy_ref[...] = (dx * gate_ref[...]).astype(dy_ref.dtype)
    dgate_ref[...] += jnp.sum(dx * y_ref[...].astype(F32), axis=0, keepdims=True)


def _norm_mod_bwd(x, dh, dx_res, g, scale, name, below=None, deps=()):
    s, d = x.shape
    ts = _tile(s, 256)
    nb = 2 if below is not None else 0

    def body(x_ref, dh_ref, dr_ref, g_ref, sc_ref, *rest):
        dx_ref, dsc_ref, dsh_ref, dg_ref = rest[nb:nb + 4]

        @pl.when(pl.program_id(0) == 0)
        def _():
            for ref in rest[nb + 1:nb + 4] + rest[nb + 5:]:
                ref[...] = jnp.zeros_like(ref)

        xv = x_ref[...]
        dh_v = dh_ref[...].astype(F32)
        gv = g_ref[...]
        rstd = lax.rsqrt(jnp.mean(xv * xv, axis=-1, keepdims=True) + EPS)
        xhat = xv * rstd
        dsc_ref[...] += jnp.sum(dh_v * xhat * gv, axis=0, keepdims=True)
        dsh_ref[...] += jnp.sum(dh_v, axis=0, keepdims=True)
        dn = dh_v * (1.0 + sc_ref[...])
        dg_ref[...] += jnp.sum(dn * xhat, axis=0, keepdims=True)
        dxhat = dn * gv
        proj = jnp.mean(dxhat * xhat, axis=-1, keepdims=True)
        dx = dr_ref[...] + rstd * (dxhat - xhat * proj)
        dx_ref[...] = dx
        if nb:
            _gate_terms(dx, rest[0], rest[1], rest[nb + 4], rest[nb + 5])

    row = pl.BlockSpec((ts, d), lambda i: (i, 0))
    vec = pl.BlockSpec((1, d), lambda i: (0, 0))
    extra = list(below) if nb else []
    return pl.pallas_call(
        _after(body, 5 + nb, deps), name=name, grid=(s // ts,),
        in_specs=[row, row, row, vec, vec] + [row, vec][:nb] + [ANY] * len(deps),
        out_specs=[row, vec, vec, vec] + [row, vec][:nb],
        out_shape=[jax.ShapeDtypeStruct((s, d), F32)] + [jax.ShapeDtypeStruct((1, d), F32)] * 3
        + [jax.ShapeDtypeStruct((s, d), BF16), jax.ShapeDtypeStruct((1, d), F32)][:nb],
        compiler_params=_params("arbitrary"),
    )(x, dh, dx_res, g, scale, *extra, *deps)


def _final_loss(x, g, target, y, gate, name):
    s, d = x.shape
    ts = _tile(s, 256)

    def body(x_ref, g_ref, t_ref, y_ref, gate_ref, dx_ref, loss_ref, dg_ref, dy_ref, dgate_ref):
        @pl.when(pl.program_id(0) == 0)
        def _():
            loss_ref[...] = jnp.zeros_like(loss_ref)
            dg_ref[...] = jnp.zeros_like(dg_ref)
            dgate_ref[...] = jnp.zeros_like(dgate_ref)

        xv = x_ref[...]
        gv = g_ref[...]
        rstd = lax.rsqrt(jnp.mean(xv * xv, axis=-1, keepdims=True) + EPS)
        xhat = xv * rstd
        err = xhat * gv - t_ref[...]
        loss_ref[...] += 0.5 * jnp.sum(jnp.mean(err * err, axis=-1, keepdims=True))
        dy = err * (1.0 / d)
        dg_ref[...] += jnp.sum(dy * xhat, axis=0, keepdims=True)
        dxhat = dy * gv
        proj = jnp.mean(dxhat * xhat, axis=-1, keepdims=True)
        dx = rstd * (dxhat - xhat * proj)
        dx_ref[...] = dx
        _gate_terms(dx, y_ref, gate_ref, dy_ref, dgate_ref)

    row = pl.BlockSpec((ts, d), lambda i: (i, 0))
    vec = pl.BlockSpec((1, d), lambda i: (0, 0))
    return pl.pallas_call(
        body, name=name, grid=(s // ts,),
        in_specs=[row, vec, row, row, vec],
        out_specs=[row, pl.BlockSpec((1, LANES), lambda i: (0, 0)), vec, row, vec],
        out_shape=[jax.ShapeDtypeStruct((s, d), F32), jax.ShapeDtypeStruct((1, LANES), F32),
                   jax.ShapeDtypeStruct((1, d), F32), jax.ShapeDtypeStruct((s, d), BF16),
                   jax.ShapeDtypeStruct((1, d), F32)],
        compiler_params=_params("arbitrary"),
    )(x, g, target, y, gate)


def _conv(v, w_ref, width):
    out = w_ref[width - 1:width, :] * v
    for k in range(width - 1):
        out = out + w_ref[k:k + 1, :] * _shift_down(v, width - 1 - k)
    return out


def _sc_fwd(proj, conv_w, name, deps=()):
    _, s, e = proj.shape
    te = _tile(e, 256)
    width = conv_w.shape[0]

    def body(b_ref, c_ref, v_ref, g_ref, w_ref, o_ref):
        cv = c_ref[...].astype(F32) * v_ref[...].astype(F32)
        u = _conv(cv, w_ref, width)
        gv = g_ref[...].astype(F32)
        o_ref[...] = (b_ref[...].astype(F32) * u * (gv * _sigmoid(gv))).astype(o_ref.dtype)

    def part(q):
        return pl.BlockSpec((None, s, te), lambda j, q=q: (q, 0, j))

    return pl.pallas_call(
        _after(body, 5, deps), name=name, grid=(e // te,),
        in_specs=[part(0), part(1), part(2), part(3), pl.BlockSpec((width, te), lambda j: (0, j))]
        + [ANY] * len(deps),
        out_specs=pl.BlockSpec((s, te), lambda j: (0, j)),
        out_shape=jax.ShapeDtypeStruct((s, e), BF16),
        compiler_params=_params("arbitrary"),
    )(proj, proj, proj, proj, conv_w, *deps)


def _sc_bwd(proj, dyb, conv_w, name, deps=()):
    _, s, e = proj.shape
    te = _tile(e, 256)
    width = conv_w.shape[0]

    def body(b_ref, c_ref, v_ref, g_ref, dy_ref, w_ref, dp_ref, vec_ref):
        bv = b_ref[...].astype(F32)
        cvl = c_ref[...].astype(F32)
        vv = v_ref[...].astype(F32)
        gv = g_ref[...].astype(F32)
        dyv = dy_ref[...].astype(F32)
        cv = cvl * vv
        u = _conv(cv, w_ref, width)
        sg = _sigmoid(gv)
        silu = gv * sg
        dp_ref[0] = (dyv * u * silu).astype(dp_ref.dtype)
        du = dyv * bv * silu
        dp_ref[3] = (dyv * bv * u * (sg * (1.0 + gv * (1.0 - sg)))).astype(dp_ref.dtype)
        dcv = w_ref[width - 1:width, :] * du
        vec_ref[...] = jnp.zeros_like(vec_ref)
        vec_ref[width - 1:width, :] = jnp.sum(du * cv, axis=0, keepdims=True)
        for k in range(width - 1):
            sh = width - 1 - k
            dcv = dcv + w_ref[k:k + 1, :] * _shift_up(du, sh)
            vec_ref[k:k + 1, :] = jnp.sum(du * _shift_down(cv, sh), axis=0, keepdims=True)
        dp_ref[1] = (dcv * vv).astype(dp_ref.dtype)
        dp_ref[2] = (dcv * cvl).astype(dp_ref.dtype)

    def part(q):
        return pl.BlockSpec((None, s, te), lambda j, q=q: (q, 0, j))

    return pl.pallas_call(
        _after(body, 6, deps), name=name, grid=(e // te,),
        in_specs=[part(0), part(1), part(2), part(3), pl.BlockSpec((s, te), lambda j: (0, j)),
                  pl.BlockSpec((width, te), lambda j: (0, j))] + [ANY] * len(deps),
        out_specs=[pl.BlockSpec((4, s, te), lambda j: (0, 0, j)),
                   pl.BlockSpec((8, te), lambda j: (0, j))],
        out_shape=[jax.ShapeDtypeStruct((4, s, e), BF16), jax.ShapeDtypeStruct((8, e), F32)],
        compiler_params=_params("arbitrary"),
    )(proj, proj, proj, proj, dyb, conv_w, *deps)


def _lru_gates(v_pre, w_ref, cb_ref, wa_ref, ba_ref, wx_ref, bx_ref, lam_ref, width):
    v = _conv(v_pre, w_ref, width) + cb_ref[...]
    vb = v.astype(BF16)
    r = _sigmoid(jnp.dot(vb, wa_ref[...], preferred_element_type=F32) + ba_ref[...])
    i = _sigmoid(jnp.dot(vb, wx_ref[...], preferred_element_type=F32) + bx_ref[...])
    nl = -lam_ref[...]
    sp = jnp.maximum(nl, 0.0) + jnp.log1p(jnp.exp(-jnp.abs(nl)))
    log_a = (-RGLRU_C) * r * sp
    a = jnp.exp(log_a)
    one_minus_a2 = jnp.tanh(-log_a) * (1.0 + a * a)
    mult = jnp.sqrt(one_minus_a2)
    return v, vb, r, i, sp, a, mult


def _lru_specs(s, dh, heads, width):
    head_col = lambda q: pl.BlockSpec((None, s, dh), lambda h, q=q: (q, 0, h))
    vec = pl.BlockSpec((1, dh), lambda h: (0, h))
    mat = pl.BlockSpec((None, dh, dh), lambda h: (h, 0, 0))
    weights = [pl.BlockSpec((width, dh), lambda h: (0, h)), vec, mat, vec, mat, vec, vec]
    return head_col, weights


def _lru_fwd(proj, conv_w, conv_b, w_a, b_a, w_x, b_x, lam, name, deps=()):
    _, s, e = proj.shape
    heads, dh, _ = w_a.shape
    width = conv_w.shape[0]

    def body(v_ref, g_ref, w_ref, cb_ref, wa_ref, ba_ref, wx_ref, bx_ref, lam_ref, yb_ref, keep_ref):
        v, _, r, i, _, a, mult = _lru_gates(v_ref[...].astype(F32), w_ref, cb_ref, wa_ref, ba_ref,
                                           wx_ref, bx_ref, lam_ref, width)
        hs = _scan(a, mult * i * v, _shift_down)
        for k, val in enumerate((hs, v, r, i, a, mult)):
            keep_ref[k] = val
        gv = g_ref[...].astype(F32)
        yb_ref[...] = (hs * (gv * _sigmoid(gv))).astype(yb_ref.dtype)

    head_col, weights = _lru_specs(s, dh, heads, width)
    return pl.pallas_call(
        _after(body, 9, deps), name=name, grid=(heads,),
        in_specs=[head_col(0), head_col(1)] + weights + [ANY] * len(deps),
        out_specs=[pl.BlockSpec((s, dh), lambda h: (0, h)), pl.BlockSpec((6, s, dh), lambda h: (0, 0, h))],
        out_shape=[jax.ShapeDtypeStruct((s, e), BF16), jax.ShapeDtypeStruct((6, s, e), F32)],
        compiler_params=_params("arbitrary"),
    )(proj, proj, conv_w, conv_b, w_a, b_a, w_x, b_x, lam, *deps)


def _lru_bwd(proj, keep, dyb, conv_w, conv_b, w_a, b_a, w_x, b_x, lam, name, deps=()):
    _, s, e = proj.shape
    heads, dh, _ = w_a.shape
    width = conv_w.shape[0]

    def body(v_ref, g_ref, hs_ref, dy_ref, w_ref, cb_ref, wa_ref, ba_ref, wx_ref, bx_ref, lam_ref,
             dp_ref, dwa_ref, dwx_ref, vec_ref):
        v_pre = v_ref[...].astype(F32)
        hs, v, r, i, a, mult = (hs_ref[k] for k in range(6))
        vb = v.astype(BF16)
        nl = -lam_ref[...]
        sp = jnp.maximum(nl, 0.0) + jnp.log1p(jnp.exp(-jnp.abs(nl)))
        gv = g_ref[...].astype(F32)
        dyv = dy_ref[...].astype(F32)
        sg = _sigmoid(gv)
        dp_ref[1] = (dyv * hs * (sg * (1.0 + gv * (1.0 - sg)))).astype(dp_ref.dtype)
        dhs = dyv * (gv * sg)
        d_h = _scan(_shift_up(a, 1), dhs, _shift_up)
        da = d_h * _shift_down(hs, 1)
        iv = i * v
        dlog_a = da * a - (d_h * iv) * (a * a) / mult
        dhm = d_h * mult
        di = dhm * v
        dv = dhm * i
        dlr = dlog_a * r
        dzr = dlr * (1.0 - r) * ((-RGLRU_C) * sp)
        dzi = di * i * (1.0 - i)
        dsp = jnp.sum(dlr, axis=0, keepdims=True) * (-RGLRU_C)
        vec_ref[...] = jnp.zeros_like(vec_ref)
        vec_ref[0:1, :] = jnp.sum(dzr, axis=0, keepdims=True)
        vec_ref[1:2, :] = jnp.sum(dzi, axis=0, keepdims=True)
        vec_ref[2:3, :] = -dsp * _sigmoid(-lam_ref[...])
        dzr_b = dzr.astype(BF16)
        dzi_b = dzi.astype(BF16)
        vt = vb.astype(F32).T.astype(BF16)
        dwa_ref[...] = jnp.dot(vt, dzr_b, preferred_element_type=F32).astype(dwa_ref.dtype)
        dwx_ref[...] = jnp.dot(vt, dzi_b, preferred_element_type=F32).astype(dwx_ref.dtype)
        nt = (((1,), (1,)), ((), ()))
        dv = dv + lax.dot_general(dzr_b, wa_ref[...], nt, preferred_element_type=F32)
        dv = dv + lax.dot_general(dzi_b, wx_ref[...], nt, preferred_element_type=F32)
        vec_ref[3:4, :] = jnp.sum(dv, axis=0, keepdims=True)
        dvp = w_ref[width - 1:width, :] * dv
        vec_ref[4 + width - 1:4 + width, :] = jnp.sum(dv * v_pre, axis=0, keepdims=True)
        for k in range(width - 1):
            sh = width - 1 - k
            dvp = dvp + w_ref[k:k + 1, :] * _shift_up(dv, sh)
            vec_ref[4 + k:5 + k, :] = jnp.sum(dv * _shift_down(v_pre, sh), axis=0, keepdims=True)
        dp_ref[0] = dvp.astype(dp_ref.dtype)

    head_col, weights = _lru_specs(s, dh, heads, width)
    col = pl.BlockSpec((s, dh), lambda h: (0, h))
    mat = pl.BlockSpec((None, dh, dh), lambda h: (h, 0, 0))
    return pl.pallas_call(
        _after(body, 11, deps), name=name, grid=(heads,),
        in_specs=[head_col(0), head_col(1), pl.BlockSpec((6, s, dh), lambda h: (0, 0, h)), col] + weights
        + [ANY] * len(deps),
        out_specs=[pl.BlockSpec((2, s, dh), lambda h: (0, 0, h)), mat, mat,
                   pl.BlockSpec((16, dh), lambda h: (0, h))],
        out_shape=[jax.ShapeDtypeStruct((2, s, e), BF16),
                   jax.ShapeDtypeStruct((heads, dh, dh), BF16),
                   jax.ShapeDtypeStruct((heads, dh, dh), BF16),
                   jax.ShapeDtypeStruct((16, e), F32)],
        compiler_params=_params("arbitrary"),
    )(proj, proj, keep, dyb, conv_w, conv_b, w_a, b_a, w_x, b_x, lam, *deps)


def _ada_mod(c_all, w, b, name):
    layers, d, f = w.shape
    nb = c_all.shape[0]

    def body(c_ref, w_ref, b_ref, o_ref):
        cv = c_ref[...]
        sc = cv * _sigmoid(cv)
        o_ref[...] = jnp.dot(sc, w_ref[...], preferred_element_type=F32,
                             precision=lax.Precision.HIGHEST) + b_ref[...]

    return pl.pallas_call(
        body, name=name, grid=(layers,),
        in_specs=[pl.BlockSpec((nb, d), lambda l: (0, 0)),
                  pl.BlockSpec((None, d, f), lambda l: (l, 0, 0)),
                  pl.BlockSpec((None, 1, f), lambda l: (l, 0, 0))],
        out_specs=pl.BlockSpec((None, nb, f), lambda l: (l, 0, 0)),
        out_shape=jax.ShapeDtypeStruct((layers, nb, f), F32),
        compiler_params=_params("arbitrary"),
    )(c_all, w, b)


def _ada_update(c_all_t, dmod, w, m, v, name):
    d, nb = c_all_t.shape
    layers, _, f = dmod.shape
    tr = _tile(d, 512)

    def body(c_ref, dm_ref, w_ref, m_ref, v_ref, g_ref, d_ref, mo_ref, vo_ref):
        cv = c_ref[...]
        sc = cv * _sigmoid(cv)
        g = sc[:, 0:1] * dm_ref[0:1, :]
        for k in range(1, nb):
            g = g + sc[:, k:k + 1] * dm_ref[k:k + 1, :]
        g_ref[...] = g
        d_ref[...], mo_ref[...], vo_ref[...] = _adamw_math(w_ref[...], g, m_ref[...], v_ref[...])

    blk = pl.BlockSpec((None, tr, f), lambda l, i: (l, i, 0))
    return pl.pallas_call(
        body, name=name, grid=(layers, d // tr),
        in_specs=[pl.BlockSpec((tr, nb), lambda l, i: (i, 0)),
                  pl.BlockSpec((None, nb, f), lambda l, i: (l, 0, 0)), blk, blk, blk],
        out_specs=[blk] * 4,
        out_shape=[jax.ShapeDtypeStruct((layers, d, f), F32)] * 4,
        compiler_params=_params("arbitrary", "arbitrary"),
    )(c_all_t, dmod, w, m, v)


def _device_sum(g, name):
    _, rows, _ = g.shape

    def body(g_ref, o_ref):
        acc = g_ref[0]
        for k in range(1, N_DEV):
            acc = acc + g_ref[k]
        o_ref[...] = acc

    return pl.pallas_call(
        body, name=name,
        in_specs=[VMEM_SPEC], out_specs=VMEM_SPEC,
        out_shape=jax.ShapeDtypeStruct((rows, LANES), F32),
        compiler_params=pltpu.CompilerParams(vmem_limit_bytes=VMEM_LIMIT),
    )(g)


def _adamw_math(w, g, m, v):
    m = ADAM_B1 * m + (1.0 - ADAM_B1) * g
    v = ADAM_B2 * v + (1.0 - ADAM_B2) * (g * g)
    m_hat = m / (1.0 - ADAM_B1 ** ADAM_STEP)
    v_hat = v / (1.0 - ADAM_B2 ** ADAM_STEP)
    delta = -ADAM_LR * (m_hat / (jnp.sqrt(v_hat) + ADAM_EPS) + ADAM_WD * w)
    return delta, m, v


def _adamw(w, g, m, v, name):
    rows, cols = w.shape
    tr = _tile(rows, 256)

    def body(w_ref, g_ref, m_ref, v_ref, d_ref, mo_ref, vo_ref):
        d_ref[...], mo_ref[...], vo_ref[...] = _adamw_math(w_ref[...], g_ref[...], m_ref[...], v_ref[...])

    blk = pl.BlockSpec((tr, cols), lambda i: (i, 0))
    return pl.pallas_call(
        body, name=name, grid=(rows // tr,),
        in_specs=[blk] * 4, out_specs=[blk] * 3,
        out_shape=[jax.ShapeDtypeStruct((rows, cols), F32)] * 3,
        compiler_params=_params("arbitrary"),
    )(w, g, m, v)


def _adamw_reduced(idx, w, m, v, part, got, recvs, name):
    rows, cols = w.shape
    tr = _tile(rows, 256)
    nr = len(recvs)

    def body(idx_ref, w_ref, m_ref, v_ref, p_ref, q_ref, *rest):
        g_ref, d_ref, mo_ref, vo_ref = rest[nr:]
        g = p_ref[...].astype(F32) + q_ref[...].astype(F32)
        for u_ref in rest[:nr]:
            for j in range(u_ref.shape[0]):
                g = g + u_ref[j].astype(F32)
        g_ref[...] = g
        d_ref[...], mo_ref[...], vo_ref[...] = _adamw_math(w_ref[...], g, m_ref[...], v_ref[...])

    blk = pl.BlockSpec((tr, cols), lambda i, idx: (i, 0))
    grid_spec = pltpu.PrefetchScalarGridSpec(
        num_scalar_prefetch=1, grid=(rows // tr,),
        in_specs=[blk, blk, blk,
                  pl.BlockSpec((None, None, tr, cols), lambda i, idx: (idx[3], idx[4], i, 0)),
                  pl.BlockSpec((None, None, tr, cols), lambda i, idx: (idx[3], 0, i, 0))]
        + [pl.BlockSpec((u.shape[0], tr, cols), lambda i, idx: (0, i, 0)) for u in recvs],
        out_specs=[blk] * 4)
    return pl.pallas_call(
        body, name=name, grid_spec=grid_spec,
        out_shape=[jax.ShapeDtypeStruct((rows, cols), F32)] * 4,
        compiler_params=_params("arbitrary"),
    )(idx, w, m, v, part, got, *recvs)


def _pack(vectors):
    flat = jnp.concatenate([v.reshape(-1).astype(F32) for v in vectors])
    pad = (-flat.shape[0]) % (8 * LANES)
    return jnp.pad(flat, (0, pad)).reshape(-1, LANES)


def _unpack(flat, shapes):
    out, off = [], 0
    for shp in shapes:
        size = math.prod(shp)
        out.append(flat[..., off:off + size].reshape(flat.shape[:-1] + tuple(shp)))
        off += size
    return out


def _my_slice(full, me, axis):
    size = full.shape[axis] // N_DEV
    return lax.dynamic_slice_in_dim(full, me * size, size, axis)


def kernel(x, c, norm_g, ada_w, ada_b, sc_w_in, sc_conv_w, sc_w_out, lru_w_in, lru_conv_w, lru_conv_b, lru_w_a, lru_b_a, lru_w_x, lru_b_x, lru_lambda, lru_w_out, final_g, loss_target, m_norm_g, m_ada_w, m_ada_b, m_sc_w_in, m_sc_conv_w, m_sc_w_out, m_lru_w_in, m_lru_conv_w, m_lru_conv_b, m_lru_w_a, m_lru_b_a, m_lru_w_x, m_lru_b_x, m_lru_lambda, m_lru_w_out, m_final_g, v_norm_g, v_ada_w, v_ada_b, v_sc_w_in, v_sc_conv_w, v_sc_w_out, v_lru_w_in, v_lru_conv_w, v_lru_conv_b, v_lru_w_a, v_lru_b_a, v_lru_w_x, v_lru_b_x, v_lru_lambda, v_lru_w_out, v_final_g):
    _, s, d = x.shape
    e = sc_w_out.shape[1] * N_DEV
    heads, dh_s, dh = lru_w_a.shape[1:]
    es = e // N_DEV
    f = ada_w.shape[2]
    mx, my, mc = _position()
    me = 4 * mx + 2 * my + mc
    chip = 2 * mx + my
    idx = jnp.stack([chip ^ 1, chip ^ 2, chip ^ 3, chip, mc]).astype(jnp.int32)

    x0 = x[0]
    target = loss_target[0]

    small_shapes = [(d,), (3, es), (4, es), (es,), (heads, dh_s), (heads, dh_s), (es,)]
    small = _small_gather(_pack([c, sc_conv_w, lru_conv_w, lru_conv_b, lru_b_a, lru_b_x, lru_lambda]),
                          "gather_small_weights").reshape(N_DEV, -1)
    c_all, cw3, cw4, cb, ba, bx, lam = _unpack(small, small_shapes)
    cw3 = cw3.transpose(1, 0, 2).reshape(3, e)
    cw4 = cw4.transpose(1, 0, 2).reshape(4, e)
    cb = cb.reshape(1, e)
    lam = lam.reshape(1, e)
    ba = ba.transpose(1, 0, 2).reshape(1, e)
    bx = bx.transpose(1, 0, 2).reshape(1, e)

    mine = [sc_w_in[0], sc_w_out[0], lru_w_in[0], lru_w_a[0].reshape(heads * dh_s, dh),
            lru_w_x[0].reshape(heads * dh_s, dh), lru_w_out[0]]
    lands = [lax.dynamic_update_slice(lax.empty((N_DEV,) + sh.shape, BF16), sh.astype(BF16)[None], (me, 0, 0))
             for sh in mine]
    every = [1, 2, 3, 0]
    units = [([0], [0]), ([0], [1]), ([0], [2]), ([0], [3]), ([1], every), ([2], every), ([3, 4], every),
             ([5], every)]
    sems, first_ld, started = _gather_start(lands[:1], units[:3], [small], "gather_start_first")
    lands = first_ld + lands[1:]

    ada_b_mine = _my_slice(ada_b, me, 1).reshape(2, 1, f)
    mod_mine = _ada_mod(c_all, ada_w, ada_b_mine, "ada_mod")
    mod_all = _small_gather(_pack([mod_mine]), "gather_mod", deps=[started])

    def start_later(after):
        far_sems, far_ld, tok = _gather_start(lands[:1], units[3:4], after, "gather_start_far")
        rest_units = [([i - 1 for i in members], ks) for members, ks in units[4:]]
        rest_sems, rest_ld, tok = _gather_start(lands[1:], rest_units, [tok], "gather_start_rest")
        sems.extend(far_sems + rest_sems)
        lands[:] = far_ld + rest_ld
        return tok

    def gathered(u, after_forward, name):
        members, ks = units[u]
        fwd, lnd, token = _gather_forward([lands[i] for i in members], ks, sems[u][0], sems[u][1],
                                          after_forward, "gather_forward_" + name)
        for i, ld in zip(members, lnd):
            lands[i] = ld

        def finish(after):
            out = _gather_finish([lands[i] for i in members], ks, fwd, after, "gather_finish_" + name)
            for i, ld in zip(members, out):
                lands[i] = ld
            return out

        return token, finish

    tok, finish_y = gathered(1, [mod_all], "sc_w_in_near_y")
    tok, finish_x = gathered(2, [tok], "sc_w_in_near_x")
    queued = start_later([tok])

    mod_all = mod_all.reshape(N_DEV, -1)
    mod_all = mod_all[:, :2 * N_DEV * f].reshape(N_DEV, 2, N_DEV, f)
    mod_all = mod_all.transpose(1, 2, 0, 3).reshape(2, N_DEV, 3 * d)
    mod = lax.dynamic_index_in_dim(mod_all, me, 1, keepdims=False)
    shift = [mod[l:l + 1, 0:d] for l in range(2)]
    scale = [mod[l:l + 1, d:2 * d] for l in range(2)]
    gate = [mod[l:l + 1, 2 * d:3 * d] for l in range(2)]
    ng = [norm_g[l:l + 1] for l in range(2)]
    fg = final_g.reshape(1, d)

    h0 = _norm_mod(x0, ng[0], scale[0], shift[0], "norm_mod_0", deps=[queued])
    proj0 = lax.empty((4, s, e), BF16)
    tok, _ = gathered(0, [h0], "sc_w_in_own")
    proj0 = _mm_proj_group(h0, lands[0], idx, 3, proj0, "mm_proj_0_own", deps=[tok])
    for u, name, finish in ((1, "near_y", finish_y), (2, "near_x", finish_x), (3, "far", None)):
        after = [proj0]
        if finish is None:
            tok, finish = gathered(u, [proj0], "sc_w_in_" + name)
            after = [tok]
        wg_in0, = finish(after)
        proj0 = _mm_proj_group(h0, wg_in0, idx, u - 1, proj0, "mm_proj_0_" + name)
    tok, finish = gathered(4, [proj0], "sc_w_out")
    yb0 = _sc_fwd(proj0, cw3, "sc_fwd", deps=[tok])
    w_out0 = finish([yb0])[0].reshape(e, d)
    x1, y0 = _mm_out(yb0, w_out0, x0, gate[0], "mm_out_0")
    tok, finish = gathered(5, [x1], "lru_w_in")
    h1 = _norm_mod(x1, ng[1], scale[1], shift[1], "norm_mod_1", deps=[tok])
    wg_in1, = finish([h1])
    proj1 = _mm_proj(h1, wg_in1, 2, "mm_proj_1")
    tok, finish = gathered(6, [proj1], "lru_gates")
    wg_a, wg_x = finish([tok])
    w_a = wg_a.reshape(N_DEV, heads, dh_s, dh).transpose(1, 0, 2, 3).reshape(heads, dh, dh)
    w_x = wg_x.reshape(N_DEV, heads, dh_s, dh).transpose(1, 0, 2, 3).reshape(heads, dh, dh)
    tok, finish = gathered(7, [w_a, w_x], "lru_w_out")
    yb1, hs = _lru_fwd(proj1, cw4, cb, w_a, ba, w_x, bx, lam, "lru_fwd", deps=[tok])
    w_out1 = finish([yb1])[0].reshape(e, d)
    x2, y1 = _mm_out(yb1, w_out1, x1, gate[1], "mm_out_1")
    dx2, loss_part, d_fg, dy1, dgate1 = _final_loss(x2, fg, target, y1, gate[1], "final_loss")

    def pieces(g, rows, cols):
        return g.reshape(4, 2, rows, cols)

    def by_rows(g):
        return g.reshape(heads, N_DEV, dh_s, dh).transpose(1, 0, 2, 3).reshape(N_DEV, heads * dh_s, dh)

    def pair_begin(parts, group):
        send, recv, parts, lnd, token = _pair_start(parts, "pair_start_" + group)
        return dict(parts=parts, lands=lnd, send=send, recv=recv, group=group), token

    def scatter_start(pair, names, after):
        group = pair["group"]
        parts, gots = _pair_wait(pair["parts"], pair["lands"], pair["send"], pair["recv"], after,
                                 "pair_wait_" + group)
        sums = [_pair_sum(idx, p, q, "pair_sum_" + nm) for p, q, nm in zip(parts, gots, names)]
        empties = [lax.empty(sm.shape, sm.dtype) for sm in sums]
        send, recv, sums, lnd, token = _chip_start(sums, empties, "chip_start_" + group)
        return dict(parts=parts, gots=gots, names=names, group=group, sums=sums, lands=lnd,
                    send=send, recv=recv), token

    big = {"sc_w_in": (sc_w_in, m_sc_w_in, v_sc_w_in), "sc_w_out": (sc_w_out, m_sc_w_out, v_sc_w_out),
           "lru_w_in": (lru_w_in, m_lru_w_in, v_lru_w_in), "lru_w_a": (lru_w_a, m_lru_w_a, v_lru_w_a),
           "lru_w_x": (lru_w_x, m_lru_w_x, v_lru_w_x), "lru_w_out": (lru_w_out, m_lru_w_out, v_lru_w_out)}
    big_res = {}

    def scatter_finish(rs, after):
        recvs = _chip_wait(rs["sums"], rs["lands"], rs["send"], rs["recv"], after, "chip_wait_" + rs["group"])
        done = []
        for p, q, u, nm in zip(rs["parts"], rs["gots"], recvs, rs["names"]):
            w, m, v = big[nm]
            shp2 = p.shape[2:]
            res = _adamw_reduced(idx, w.reshape(shp2), m.reshape(shp2), v.reshape(shp2), p, q, [u], "adamw_" + nm)
            big_res[nm] = [r.reshape(w.shape) for r in res]
            done.append(res[1])
        return done

    dw_out1 = _mm_tn(yb1, dy1[None], 1, "mm_dw_out_1")
    pair, tok = pair_begin([pieces(dw_out1, es, d)], "lru_w_out")
    dyb1 = _mm_nt(dy1[None], w_out1[None], BF16, "mm_dyb_1", deps=[tok])
    rs1, tok = scatter_start(pair, ["lru_w_out"], [dyb1])
    dproj1, dw_a, dw_x, vecs1 = _lru_bwd(proj1, hs, dyb1, cw4, cb, w_a, ba, w_x, bx, lam, "lru_bwd", deps=[tok])
    done = scatter_finish(rs1, [dproj1])
    dw_in1 = _mm_tn(h1, dproj1, N_DEV, "mm_dw_in_1", deps=done)
    pair, tok = pair_begin([pieces(dw_in1, d, 2 * es), pieces(by_rows(dw_a), heads * dh_s, dh),
                            pieces(by_rows(dw_x), heads * dh_s, dh)], "lru_in")
    dh1 = _mm_nt(dproj1, wg_in1, BF16, "mm_dh_1", deps=[tok])
    rs2, tok = scatter_start(pair, ["lru_w_in", "lru_w_a", "lru_w_x"], [dh1])
    dx1, dscale1, dshift1, dng1, dy0, dgate0 = _norm_mod_bwd(x1, dh1, dx2, ng[1], scale[1], "norm_mod_bwd_1",
                                                             below=(y0, gate[0]), deps=[tok])
    dw_out0 = _mm_tn(yb0, dy0[None], 1, "mm_dw_out_0")
    pair, tok = pair_begin([pieces(dw_out0, es, d)], "sc_w_out")
    dyb0 = _mm_nt(dy0[None], w_out0[None], BF16, "mm_dyb_0", deps=[tok])
    rs3, tok = scatter_start(pair, ["sc_w_out"], [dyb0])
    dproj0, vecs0 = _sc_bwd(proj0, dyb0, cw3, "sc_bwd", deps=[tok])
    idx_one = jnp.stack([jnp.zeros_like(mc)] * 4 + [mc]).astype(jnp.int32)
    sc_w_in_steps = []

    def chip_step(j, pair, after):
        (part,), (got,) = _pair_wait(pair["parts"], pair["lands"], pair["send"], pair["recv"], after,
                                     "pair_wait_sc_w_in_%d" % j)
        sm = _pair_sum(idx_one, part, got, "pair_sum_sc_w_in_%d" % j, nslots=1)
        send, recv, sums, lnd, token = _chip_start([sm], [lax.empty(sm.shape, sm.dtype)],
                                                   "chip_start_sc_w_in_%d" % j, flips=(j,))
        sc_w_in_steps.append((sums, lnd, send, recv, j))
        return token

    pending, done = None, []
    for j in (3, 2, 1, 0):
        part = _mm_tn_group(h0, dproj0, idx, (j - 1) % 4, 2, "mm_dw_in_0_%d" % j, deps=done)[None]
        pair, tok = pair_begin([part], "sc_w_in_%d" % j)
        if j == 3:
            done = [chip_step(j, pair, [tok])]
            continue
        done = [tok]
        if pending is not None:
            done.append(chip_step(pending[0], pending[1], [tok]))
        pending = (j, pair)
    done += scatter_finish(rs2, done)
    dh0 = _mm_nt(dproj0, wg_in0, BF16, "mm_dh_0", deps=done)
    pair = pending[1]
    (part,), (got,) = _pair_wait(pair["parts"], pair["lands"], pair["send"], pair["recv"], [dh0],
                                 "pair_wait_sc_w_in_0")
    dx0, dscale0, dshift0, dng0 = _norm_mod_bwd(x0, dh0, dx1, ng[0], scale[0], "norm_mod_bwd_0")
    done = scatter_finish(rs3, [dx0])
    dmod_mine = jnp.concatenate([dshift0, dscale0, dgate0, dshift1, dscale1, dgate1], axis=1)
    end_shapes = [(LANES,), (2, 3 * d), (2, d), (d,), (8, e), (16, e)]
    end_all = _small_gather(
        _pack([loss_part, dmod_mine, jnp.concatenate([dng0, dng1], axis=0), d_fg, vecs0, vecs1]),
        "gather_small_grads", deps=done)
    end_sum = _device_sum(end_all, "sum_small_grads").reshape(-1)
    loss_v, g_ada_b, g_norm_g, g_final_g, sum0, sum1 = _unpack(end_sum, end_shapes)
    loss = loss_v[0]
    dmod_all = _unpack(end_all.reshape(N_DEV, -1), end_shapes)[1].transpose(1, 0, 2)
    dmod_cols = _my_slice(dmod_all, me, 2)
    ada_out = _ada_update(c_all.T, dmod_cols, ada_w, m_ada_w, v_ada_w, "ada_update")

    g_sc_conv_w = _my_slice(sum0[0:3], me, 1)
    g_lru_b_a = _my_slice(sum1[0].reshape(heads, dh), me, 1)
    g_lru_b_x = _my_slice(sum1[1].reshape(heads, dh), me, 1)
    g_lru_lambda = _my_slice(sum1[2:3], me, 1)
    g_lru_conv_b = _my_slice(sum1[3:4], me, 1)
    g_lru_conv_w = _my_slice(sum1[4:8], me, 1)

    small_w = [norm_g, ada_b, final_g, sc_conv_w, lru_conv_w, lru_conv_b, lru_b_a, lru_b_x, lru_lambda]
    small_m = [m_norm_g, m_ada_b, m_final_g, m_sc_conv_w, m_lru_conv_w, m_lru_conv_b, m_lru_b_a, m_lru_b_x,
               m_lru_lambda]
    small_v = [v_norm_g, v_ada_b, v_final_g, v_sc_conv_w, v_lru_conv_w, v_lru_conv_b, v_lru_b_a, v_lru_b_x,
               v_lru_lambda]
    small_g = [g_norm_g, g_ada_b, g_final_g, g_sc_conv_w, g_lru_conv_w, g_lru_conv_b, g_lru_b_a, g_lru_b_x,
               g_lru_lambda]
    small_g = [g.reshape(w.shape) for g, w in zip(small_g, small_w)]
    shapes = [w.shape for w in small_w]
    packed = _adamw(_pack(small_w), _pack(small_g), _pack(small_m), _pack(small_v), "adamw_small")
    small_out = [small_g] + [_unpack(p.reshape(-1), shapes) for p in packed]

    after = [packed[0], ada_out[1]]
    recvs = []
    for sums, lnd, send, recv, j in sc_w_in_steps:
        recvs += _chip_wait(sums, lnd, send, recv, after, "chip_wait_sc_w_in_%d" % j)
    shp2 = part.shape[2:]
    res = _adamw_reduced(idx_one, sc_w_in.reshape(shp2), m_sc_w_in.reshape(shp2), v_sc_w_in.reshape(shp2),
                         part, got, recvs, "adamw_sc_w_in")
    big_res["sc_w_in"] = [r.reshape(sc_w_in.shape) for r in res]
    big_out = [big_res[nm] for nm in ("sc_w_in", "sc_w_out", "lru_w_in", "lru_w_a", "lru_w_x", "lru_w_out")]

    def small(kind, i):
        return small_out[kind][i]

    def bigw(kind, i):
        return big_out[i][kind]

    outs = [loss, dx0[None]]
    for kind in range(4):
        outs += [small(kind, 0), ada_out[kind], small(kind, 1), bigw(kind, 0), small(kind, 3), bigw(kind, 1),
                 bigw(kind, 2), small(kind, 4), small(kind, 5), bigw(kind, 3), small(kind, 6), bigw(kind, 4),
                 small(kind, 7), small(kind, 8), bigw(kind, 5), small(kind, 2)]
    return tuple(outs)
```

```python
import math

import jax
import jax.numpy as jnp
from jax import lax
from jax.experimental import pallas as pl
from jax.experimental.pallas import tpu as pltpu

N_DEV = 8
LANES = 128
EPS = 1e-6
RGLRU_C = 8.0
ADAM_LR = 0.001
ADAM_B1 = 0.9
ADAM_B2 = 0.999
ADAM_EPS = 1e-08
ADAM_WD = 0.01
ADAM_STEP = 10
VMEM_LIMIT = 56 * 1024 * 1024
MESH = pl.DeviceIdType.MESH
F32 = jnp.float32
BF16 = jnp.bfloat16
ANY = pl.BlockSpec(memory_space=pl.ANY)
HBM = pl.BlockSpec(memory_space=pltpu.HBM)
SEM = pl.BlockSpec(memory_space=pltpu.SEMAPHORE)
VMEM_SPEC = pl.BlockSpec(memory_space=pltpu.VMEM)
EFFECT = pltpu.SideEffectType.DATAFLOW_SIDE_EFFECTING
TOKEN = jax.ShapeDtypeStruct((8, LANES), jnp.float32)


def _tile(n, pref):
    t = min(n, pref)
    assert n % t == 0, (n, pref)
    return t


def _params(*sem):
    return pltpu.CompilerParams(dimension_semantics=sem, vmem_limit_bytes=VMEM_LIMIT)


def _position():
    return lax.axis_index("x"), lax.axis_index("y"), lax.axis_index("c")


def _flip(x, y, k):
    return (1 - x if k & 2 else x), (1 - y if k & 1 else y)


def _after(body, n_in, deps):
    if not deps:
        return body

    def wrapped(*refs):
        return body(*refs[:n_in], *refs[n_in + len(deps):])

    return wrapped


def _small_gather(v, name, deps=()):
    rows = v.shape[0]

    def body(v_ref, out_ref, send_sems, recv_sems):
        x, y, c = _position()
        me = 4 * x + 2 * y + c
        out_ref[me] = v_ref[...]
        copies = []
        for k in range(1, N_DEV):
            px, py = _flip(x, y, k >> 1)
            pc = 1 - c if k & 1 else c
            cp = pltpu.make_async_remote_copy(
                src_ref=v_ref, dst_ref=out_ref.at[me],
                send_sem=send_sems.at[k - 1], recv_sem=recv_sems.at[k - 1],
                device_id=(px, py, pc), device_id_type=MESH)
            cp.start()
            copies.append((cp, 4 * px + 2 * py + pc))
        for k, (cp, peer) in enumerate(copies):
            pltpu.make_async_remote_copy(
                src_ref=v_ref, dst_ref=out_ref.at[peer],
                send_sem=send_sems.at[k], recv_sem=recv_sems.at[k],
                device_id=(x, y, c), device_id_type=MESH).wait_recv()
        for cp, _ in copies:
            cp.wait_send()

    return pl.pallas_call(
        _after(body, 1, deps), name=name,
        out_shape=jax.ShapeDtypeStruct((N_DEV, rows, LANES), F32),
        in_specs=[VMEM_SPEC] + [ANY] * len(deps), out_specs=VMEM_SPEC,
        scratch_shapes=[pltpu.SemaphoreType.DMA((N_DEV - 1,)),
                        pltpu.SemaphoreType.DMA((N_DEV - 1,))],
        compiler_params=pltpu.CompilerParams(vmem_limit_bytes=VMEM_LIMIT),
    )(v, *deps)


def _hbm(a):
    return pltpu.with_memory_space_constraint(a, pltpu.HBM)


def _hbm_like(arrays):
    return [pltpu.HBM(a.shape, a.dtype) for a in arrays]


def _remote(src, dst, send, recv, to):
    return pltpu.make_async_remote_copy(src_ref=src, dst_ref=dst, send_sem=send, recv_sem=recv,
                                        device_id=to, device_id_type=MESH)


def _gather_start(lands, units, after, name):
    n, nu = len(lands), len(units)

    def body(*refs):
        lnd = refs[:n]
        sems = refs[n + len(after):n + len(after) + 2 * nu]
        token = refs[-1]
        x, y, c = _position()
        me = 4 * x + 2 * y + c
        targets = [(x, y, 1 - c)] + [(px, py, c) for px, py in (_flip(x, y, k) for k in (1, 2, 3))]
        for u, (members, ks) in enumerate(units):
            for slot, i in enumerate(members):
                for ki, k in enumerate(ks):
                    at = len(ks) * slot + ki
                    mine = lnd[i].at[me]
                    _remote(mine, mine, sems[2 * u].at[at], sems[2 * u + 1].at[at], targets[k]).start()
        token[...] = jnp.zeros_like(token)

    sem_shapes = []
    for members, ks in units:
        count = len(members) * len(ks)
        sem_shapes += [pltpu.SemaphoreType.DMA((count,)), pltpu.SemaphoreType.DMA((count,))]
    out = pl.pallas_call(
        body, name=name,
        out_shape=sem_shapes + _hbm_like(lands) + [TOKEN],
        in_specs=[HBM] * n + [ANY] * len(after),
        out_specs=[SEM] * (2 * nu) + [HBM] * n + [VMEM_SPEC],
        input_output_aliases={i: 2 * nu + i for i in range(n)},
        compiler_params=pltpu.CompilerParams(has_side_effects=EFFECT),
    )(*[_hbm(l) for l in lands], *after)
    sems = [(out[2 * u], out[2 * u + 1]) for u in range(nu)]
    return sems, list(out[2 * nu:2 * nu + n]), out[-1]


def _gather_forward(lands, ks, send, recv, after, name):
    m = len(lands)
    hops = [k for k in ks if k]
    nsem = 2 if hops else 0

    def body(*refs):
        lnd = refs[:m]
        send_ref, recv_ref = refs[m], refs[m + 1]
        outs = refs[m + 2 + len(after):]
        token = refs[-1]
        x, y, c = _position()
        me = (x, y, c)
        for slot in range(m):
            for ki, k in enumerate(ks):
                at = len(ks) * slot + ki
                if k:
                    px, py = _flip(x, y, k)
                    block = lnd[slot].at[4 * px + 2 * py + c]
                else:
                    block = lnd[slot].at[4 * x + 2 * y + (1 - c)]
                arrival = _remote(lnd[slot].at[4 * x + 2 * y + c], block, send_ref.at[at], recv_ref.at[at], me)
                arrival.wait_recv()
                if k:
                    fat = len(hops) * slot + hops.index(k)
                    _remote(block, block, outs[0].at[fat], outs[1].at[fat], (x, y, 1 - c)).start()
                arrival.wait_send()
        token[...] = jnp.zeros_like(token)

    count = len(hops) * m
    sem_shapes = [pltpu.SemaphoreType.DMA((count,)), pltpu.SemaphoreType.DMA((count,))] if hops else []
    out = pl.pallas_call(
        body, name=name,
        out_shape=sem_shapes + _hbm_like(lands) + [TOKEN],
        in_specs=[HBM] * m + [SEM, SEM] + [ANY] * len(after),
        out_specs=[SEM] * nsem + [HBM] * m + [VMEM_SPEC],
        input_output_aliases={i: nsem + i for i in range(m)},
        compiler_params=pltpu.CompilerParams(has_side_effects=EFFECT),
    )(*lands, send, recv, *after)
    fwd = (out[0], out[1]) if hops else None
    return fwd, list(out[nsem:nsem + m]), out[-1]


def _gather_finish(lands, ks, fwd, after, name):
    m = len(lands)
    hops = [k for k in ks if k]

    def body(*refs):
        lnd = refs[:m]
        fsend_ref, frecv_ref = refs[m], refs[m + 1]
        x, y, c = _position()
        for slot in range(m):
            for fi, k in enumerate(hops):
                px, py = _flip(x, y, k)
                sent = lnd[slot].at[4 * px + 2 * py + c]
                came = lnd[slot].at[4 * px + 2 * py + (1 - c)]
                fat = len(hops) * slot + fi
                cp = _remote(sent, came, fsend_ref.at[fat], frecv_ref.at[fat], (x, y, c))
                cp.wait_recv()
                cp.wait_send()

    out = pl.pallas_call(
        body, name=name,
        out_shape=_hbm_like(lands),
        in_specs=[HBM] * m + [SEM, SEM] + [ANY] * len(after), out_specs=[HBM] * m,
        input_output_aliases={i: i for i in range(m)},
        compiler_params=pltpu.CompilerParams(has_side_effects=EFFECT),
    )(*lands, fwd[0], fwd[1], *after)
    return list(out)


def _pair_start(parts, name):
    n = len(parts)
    lands = [lax.empty((p.shape[0], 1) + p.shape[2:], p.dtype) for p in parts]

    def body(*refs):
        ins, lnd = refs[:n], refs[n:2 * n]
        send_ref, recv_ref = refs[2 * n], refs[2 * n + 1]
        token = refs[-1]
        x, y, c = _position()
        for i in range(n):
            _remote(ins[i].at[:, pl.ds(1 - c, 1)], lnd[i], send_ref.at[i], recv_ref.at[i], (x, y, 1 - c)).start()
        token[...] = jnp.zeros_like(token)

    out = pl.pallas_call(
        body, name=name,
        out_shape=[pltpu.SemaphoreType.DMA((n,)), pltpu.SemaphoreType.DMA((n,))]
        + _hbm_like(parts) + _hbm_like(lands) + [TOKEN],
        in_specs=[HBM] * (2 * n), out_specs=[SEM, SEM] + [HBM] * (2 * n) + [VMEM_SPEC],
        input_output_aliases={i: 2 + i for i in range(2 * n)},
        compiler_params=pltpu.CompilerParams(has_side_effects=EFFECT),
    )(*[_hbm(p) for p in parts], *[_hbm(l) for l in lands])
    return out[0], out[1], list(out[2:2 + n]), list(out[2 + n:2 + 2 * n]), out[-1]


def _pair_wait(parts, lands, send, recv, after, name):
    n = len(parts)

    def body(*refs):
        ins, lnd = refs[:n], refs[n:2 * n]
        send_ref, recv_ref = refs[2 * n], refs[2 * n + 1]
        x, y, c = _position()
        for i in range(n):
            cp = _remote(ins[i].at[:, pl.ds(1 - c, 1)], lnd[i], send_ref.at[i], recv_ref.at[i], (x, y, c))
            cp.wait_recv()
            cp.wait_send()

    out = pl.pallas_call(
        body, name=name,
        out_shape=_hbm_like(parts) + _hbm_like(lands),
        in_specs=[HBM] * (2 * n) + [SEM, SEM] + [ANY] * len(after), out_specs=[HBM] * (2 * n),
        input_output_aliases={i: i for i in range(2 * n)},
        compiler_params=pltpu.CompilerParams(has_side_effects=EFFECT),
    )(*parts, *lands, send, recv, *after)
    return list(out[:n]), list(out[n:])


def _chip_start(sums, lands, name, flips=(1, 2, 3)):
    n, ns = len(sums), len(flips)

    def body(*refs):
        ins, lnd = refs[:n], refs[n:2 * n]
        send_ref, recv_ref = refs[2 * n], refs[2 * n + 1]
        token = refs[-1]
        x, y, c = _position()
        for i in range(n):
            for j, flip in enumerate(flips):
                px, py = _flip(x, y, flip)
                _remote(ins[i].at[j], lnd[i].at[j], send_ref.at[ns * i + j], recv_ref.at[ns * i + j],
                        (px, py, c)).start()
        token[...] = jnp.zeros_like(token)

    out = pl.pallas_call(
        body, name=name,
        out_shape=[pltpu.SemaphoreType.DMA((ns * n,)), pltpu.SemaphoreType.DMA((ns * n,))]
        + _hbm_like(sums) + _hbm_like(lands) + [TOKEN],
        in_specs=[HBM] * (2 * n), out_specs=[SEM, SEM] + [HBM] * (2 * n) + [VMEM_SPEC],
        input_output_aliases={i: 2 + i for i in range(2 * n)},
        compiler_params=pltpu.CompilerParams(has_side_effects=EFFECT),
    )(*[_hbm(s) for s in sums], *[_hbm(l) for l in lands])
    return out[0], out[1], out[2:2 + n], out[2 + n:2 + 2 * n], out[-1]


def _chip_wait(sums, lands, send, recv, after, name):
    n, ns = len(sums), sums[0].shape[0]

    def body(*refs):
        ins, lnd = refs[:n], refs[n:2 * n]
        send_ref, recv_ref = refs[2 * n], refs[2 * n + 1]
        x, y, c = _position()
        for i in range(n):
            for j in range(ns):
                cp = _remote(ins[i].at[j], lnd[i].at[j], send_ref.at[ns * i + j], recv_ref.at[ns * i + j], (x, y, c))
                cp.wait_recv()
                cp.wait_send()

    out = pl.pallas_call(
        body, name=name,
        out_shape=_hbm_like(sums) + _hbm_like(lands),
        in_specs=[HBM] * (2 * n) + [SEM, SEM] + [ANY] * len(after), out_specs=[HBM] * (2 * n),
        input_output_aliases={i: i for i in range(2 * n)},
        compiler_params=pltpu.CompilerParams(has_side_effects=EFFECT),
    )(*sums, *lands, send, recv, *after)
    return list(out[n:])


def _pair_sum(idx, part, got, name, nslots=3):
    _, _, rows, cols = part.shape
    tr = _tile(rows, 1024)

    def body(idx_ref, p_ref, q_ref, o_ref):
        o_ref[...] = (p_ref[...].astype(F32) + q_ref[...].astype(F32)).astype(o_ref.dtype)

    grid_spec = pltpu.PrefetchScalarGridSpec(
        num_scalar_prefetch=1, grid=(nslots, rows // tr),
        in_specs=[pl.BlockSpec((None, None, tr, cols), lambda j, r, idx: (idx[j], idx[4], r, 0)),
                  pl.BlockSpec((None, None, tr, cols), lambda j, r, idx: (idx[j], 0, r, 0))],
        out_specs=pl.BlockSpec((None, tr, cols), lambda j, r, idx: (j, r, 0)))
    return pl.pallas_call(
        body, name=name, grid_spec=grid_spec,
        out_shape=jax.ShapeDtypeStruct((nslots, rows, cols), part.dtype),
        compiler_params=_params("arbitrary", "arbitrary"),
    )(idx, part, got)


def _mm_proj(h, wg, groups, name):
    s, k = h.shape
    nchunk, _, n = wg.shape
    e = nchunk * n // groups
    tn = _tile(min(n, e), 512)

    def body(h_ref, w_ref, o_ref):
        o_ref[...] = jnp.dot(h_ref[...], w_ref[...], preferred_element_type=F32).astype(o_ref.dtype)

    return pl.pallas_call(
        body, name=name, grid=(nchunk * n // tn,),
        in_specs=[pl.BlockSpec((s, k), lambda j: (0, 0)),
                  pl.BlockSpec((None, k, tn), lambda j: ((j * tn) // n, 0, ((j * tn) % n) // tn))],
        out_specs=pl.BlockSpec((None, s, tn), lambda j: ((j * tn) // e, 0, ((j * tn) % e) // tn)),
        out_shape=jax.ShapeDtypeStruct((groups, s, e), BF16),
        compiler_params=_params("arbitrary"),
    )(h, wg)


def _mm_proj_group(h, wg, idx, pos, prev, name, deps=()):
    s, k = h.shape
    _, _, n = wg.shape
    _, _, e = prev.shape
    tn = _tile(n, 512)
    nd = len(deps)

    def body(idx_ref, h_ref, w_ref, prev_ref, *rest):
        o_ref = rest[nd]
        o_ref[...] = jnp.dot(h_ref[...], w_ref[...], preferred_element_type=F32).astype(o_ref.dtype)

    def col(j, idx):
        return idx[pos] * (2 * n) + j * tn

    grid_spec = pltpu.PrefetchScalarGridSpec(
        num_scalar_prefetch=1, grid=(2 * n // tn,),
        in_specs=[pl.BlockSpec((s, k), lambda j, idx: (0, 0)),
                  pl.BlockSpec((None, k, tn), lambda j, idx: (col(j, idx) // n, 0, (col(j, idx) % n) // tn)),
                  ANY] + [ANY] * nd,
        out_specs=pl.BlockSpec((None, s, tn), lambda j, idx: (col(j, idx) // e, 0, (col(j, idx) % e) // tn)))
    return pl.pallas_call(
        body, name=name, grid_spec=grid_spec,
        out_shape=jax.ShapeDtypeStruct(prev.shape, prev.dtype),
        input_output_aliases={3: 0},
        compiler_params=_params("arbitrary"),
    )(idx, h, wg, prev, *deps)


def _mm_out(yb, w, x, gate, name):
    s, k = yb.shape
    d = w.shape[1]
    tn = _tile(d, 512)
    tk = _tile(k, 2048)
    nk = k // tk

    def body(a_ref, w_ref, x_ref, g_ref, xo_ref, y_ref, acc_ref):
        kk = pl.program_id(1)

        @pl.when(kk == 0)
        def _():
            acc_ref[...] = jnp.zeros_like(acc_ref)

        acc_ref[...] += jnp.dot(a_ref[...], w_ref[...], preferred_element_type=F32)

        @pl.when(kk == nk - 1)
        def _():
            y = acc_ref[...]
            y_ref[...] = y.astype(y_ref.dtype)
            xo_ref[...] = x_ref[...] + g_ref[...] * y

    return pl.pallas_call(
        body, name=name, grid=(d // tn, nk),
        in_specs=[pl.BlockSpec((s, tk), lambda j, kk: (0, kk)),
                  pl.BlockSpec((tk, tn), lambda j, kk: (kk, j)),
                  pl.BlockSpec((s, tn), lambda j, kk: (0, j)),
                  pl.BlockSpec((1, tn), lambda j, kk: (0, j))],
        out_specs=[pl.BlockSpec((s, tn), lambda j, kk: (0, j)),
                   pl.BlockSpec((s, tn), lambda j, kk: (0, j))],
        out_shape=[jax.ShapeDtypeStruct((s, d), F32), jax.ShapeDtypeStruct((s, d), BF16)],
        scratch_shapes=[pltpu.VMEM((s, tn), F32)],
        compiler_params=_params("arbitrary", "arbitrary"),
    )(yb, w, x, gate)


def _mm_nt(a3, w3, out_dtype, name, deps=()):
    g, s, ea = a3.shape
    cw, n, nw = w3.shape
    total = g * ea
    assert total == cw * nw
    tk = _tile(min(ea, nw), 2048)
    tn = _tile(n, 1024)
    nk = total // tk

    def body(a_ref, w_ref, o_ref, acc_ref):
        kk = pl.program_id(1)

        @pl.when(kk == 0)
        def _():
            acc_ref[...] = jnp.zeros_like(acc_ref)

        acc_ref[...] += lax.dot_general(a_ref[...], w_ref[...], (((1,), (1,)), ((), ())),
                                        preferred_element_type=F32)

        @pl.when(kk == nk - 1)
        def _():
            o_ref[...] = acc_ref[...].astype(o_ref.dtype)

    return pl.pallas_call(
        _after(body, 2, deps), name=name, grid=(n // tn, nk),
        in_specs=[pl.BlockSpec((None, s, tk), lambda j, kk: ((kk * tk) // ea, 0, ((kk * tk) % ea) // tk)),
                  pl.BlockSpec((None, tn, tk), lambda j, kk: ((kk * tk) // nw, j, ((kk * tk) % nw) // tk))]
        + [ANY] * len(deps),
        out_specs=pl.BlockSpec((s, tn), lambda j, kk: (0, j)),
        out_shape=jax.ShapeDtypeStruct((s, n), out_dtype),
        scratch_shapes=[pltpu.VMEM((s, tn), F32)],
        compiler_params=_params("arbitrary", "arbitrary"),
    )(a3, w3, *deps)


def _mm_tn(a, b3, nchunk, name, deps=()):
    s, ka = a.shape
    g, _, eb = b3.shape
    n = g * eb // nchunk
    tm = _tile(ka, 1024)
    tn = _tile(min(n, eb), 1024)

    def body(a_ref, b_ref, o_ref, at_ref):
        @pl.when(pl.program_id(1) == 0)
        def _():
            at_ref[...] = a_ref[...].astype(F32).T.astype(at_ref.dtype)

        o_ref[...] = jnp.dot(at_ref[...], b_ref[...], preferred_element_type=F32).astype(o_ref.dtype)

    return pl.pallas_call(
        _after(body, 2, deps), name=name, grid=(ka // tm, g * eb // tn),
        in_specs=[pl.BlockSpec((s, tm), lambda i, j: (0, i)),
                  pl.BlockSpec((None, s, tn), lambda i, j: ((j * tn) // eb, 0, ((j * tn) % eb) // tn))]
        + [ANY] * len(deps),
        out_specs=pl.BlockSpec((None, tm, tn), lambda i, j: ((j * tn) // n, i, ((j * tn) % n) // tn)),
        out_shape=jax.ShapeDtypeStruct((nchunk, ka, n), BF16),
        scratch_shapes=[pltpu.VMEM((tm, s), BF16)],
        compiler_params=_params("arbitrary", "arbitrary"),
    )(a, b3, *deps)


def _mm_tn_group(a, b3, idx, pos, nchunk, name, deps=()):
    s, ka = a.shape
    _, _, eb = b3.shape
    n = eb // nchunk
    tm = _tile(ka, 1024)
    tn = _tile(n, 1024)
    nd = len(deps)

    def body(idx_ref, a_ref, b_ref, *rest):
        o_ref, at_ref = rest[nd:]

        @pl.when(pl.program_id(1) == 0)
        def _():
            at_ref[...] = a_ref[...].astype(F32).T.astype(at_ref.dtype)

        o_ref[...] = jnp.dot(at_ref[...], b_ref[...], preferred_element_type=F32).astype(o_ref.dtype)

    grid_spec = pltpu.PrefetchScalarGridSpec(
        num_scalar_prefetch=1, grid=(ka // tm, eb // tn),
        in_specs=[pl.BlockSpec((s, tm), lambda i, j, idx: (0, i)),
                  pl.BlockSpec((None, s, tn), lambda i, j, idx: (idx[pos], 0, j))] + [ANY] * nd,
        out_specs=pl.BlockSpec((None, tm, tn), lambda i, j, idx: ((j * tn) // n, i, ((j * tn) % n) // tn)),
        scratch_shapes=[pltpu.VMEM((tm, s), BF16)])
    return pl.pallas_call(
        body, name=name, grid_spec=grid_spec,
        out_shape=jax.ShapeDtypeStruct((nchunk, ka, n), BF16),
        compiler_params=_params("arbitrary", "arbitrary"),
    )(idx, a, b3, *deps)


def _sigmoid(z):
    return jax.nn.sigmoid(z)


def _shift_down(v, k, fill=0.0, period=None):
    if k == 0:
        return v
    row = lax.broadcasted_iota(jnp.int32, v.shape, 0)
    if period is not None:
        row = row & (period - 1)
    return jnp.where(row >= k, pltpu.roll(v, k, 0), fill)


def _shift_up(v, k, fill=0.0, period=None):
    if k == 0:
        return v
    s = v.shape[0]
    row = lax.broadcasted_iota(jnp.int32, v.shape, 0)
    if period is not None:
        row, s = row & (period - 1), period
    return jnp.where(row < s - k, pltpu.roll(v, v.shape[0] - k, 0), fill)


SCAN_BLOCK = 64


def _scan(a, b, shift):
    s = a.shape[0]
    blk = min(SCAN_BLOCK, s)
    k = 1
    while k < blk:
        b = a * shift(b, k, 0.0, blk) + b
        a = a * shift(a, k, 1.0, blk)
        k *= 2
    nblk = s // blk
    forward = shift is _shift_down
    order = range(nblk) if forward else range(nblk - 1, -1, -1)
    edge = blk - 1 if forward else 0
    out = [None] * nblk
    carry = None
    for i in order:
        h = b[i * blk:(i + 1) * blk]
        if carry is not None:
            h = a[i * blk:(i + 1) * blk] * carry + h
        carry = h[edge:edge + 1]
        out[i] = h
    return jnp.concatenate(out, axis=0) if nblk > 1 else out[0]


def _norm_mod(x, g, scale, shift, name, deps=()):
    s, d = x.shape
    ts = _tile(s, 256)

    def body(x_ref, g_ref, sc_ref, sh_ref, h_ref):
        xv = x_ref[...]
        rstd = lax.rsqrt(jnp.mean(xv * xv, axis=-1, keepdims=True) + EPS)
        nrm = xv * rstd * g_ref[...]
        h_ref[...] = (nrm * (1.0 + sc_ref[...]) + sh_ref[...]).astype(h_ref.dtype)

    vec = pl.BlockSpec((1, d), lambda i: (0, 0))
    return pl.pallas_call(
        _after(body, 4, deps), name=name, grid=(s // ts,),
        in_specs=[pl.BlockSpec((ts, d), lambda i: (i, 0)), vec, vec, vec] + [ANY] * len(deps),
        out_specs=pl.BlockSpec((ts, d), lambda i: (i, 0)),
        out_shape=jax.ShapeDtypeStruct((s, d), BF16),
        compiler_params=_params("arbitrary"),
    )(x, g, scale, shift, *deps)


def _gate_terms(dx, y_ref, gate_ref, dy_ref, dgate_ref):
    dy_ref[...] = (dx * gate_ref[...]).astype(dy_ref.dtype)
    dgate_ref[...] += jnp.sum(dx * y_ref[...].astype(F32), axis=0, keepdims=True)


def _norm_mod_bwd(x, dh, dx_res, g, scale, name, below=None, deps=()):
    s, d = x.shape
    ts = _tile(s, 256)
    nb = 2 if below is not None else 0

    def body(x_ref, dh_ref, dr_ref, g_ref, sc_ref, *rest):
        dx_ref, dsc_ref, dsh_ref, dg_ref = rest[nb:nb + 4]

        @pl.when(pl.program_id(0) == 0)
        def _():
            for ref in rest[nb + 1:nb + 4] + rest[nb + 5:]:
                ref[...] = jnp.zeros_like(ref)

        xv = x_ref[...]
        dh_v = dh_ref[...].astype(F32)
        gv = g_ref[...]
        rstd = lax.rsqrt(jnp.mean(xv * xv, axis=-1, keepdims=True) + EPS)
        xhat = xv * rstd
        dsc_ref[...] += jnp.sum(dh_v * xhat * gv, axis=0, keepdims=True)
        dsh_ref[...] += jnp.sum(dh_v, axis=0, keepdims=True)
        dn = dh_v * (1.0 + sc_ref[...])
        dg_ref[...] += jnp.sum(dn * xhat, axis=0, keepdims=True)
        dxhat = dn * gv
        proj = jnp.mean(dxhat * xhat, axis=-1, keepdims=True)
        dx = dr_ref[...] + rstd * (dxhat - xhat * proj)
        dx_ref[...] = dx
        if nb:
            _gate_terms(dx, rest[0], rest[1], rest[nb + 4], rest[nb + 5])

    row = pl.BlockSpec((ts, d), lambda i: (i, 0))
    vec = pl.BlockSpec((1, d), lambda i: (0, 0))
    extra = list(below) if nb else []
    return pl.pallas_call(
        _after(body, 5 + nb, deps), name=name, grid=(s // ts,),
        in_specs=[row, row, row, vec, vec] + [row, vec][:nb] + [ANY] * len(deps),
        out_specs=[row, vec, vec, vec] + [row, vec][:nb],
        out_shape=[jax.ShapeDtypeStruct((s, d), F32)] + [jax.ShapeDtypeStruct((1, d), F32)] * 3
        + [jax.ShapeDtypeStruct((s, d), BF16), jax.ShapeDtypeStruct((1, d), F32)][:nb],
        compiler_params=_params("arbitrary"),
    )(x, dh, dx_res, g, scale, *extra, *deps)


def _final_loss(x, g, target, y, gate, name):
    s, d = x.shape
    ts = _tile(s, 256)

    def body(x_ref, g_ref, t_ref, y_ref, gate_ref, dx_ref, loss_ref, dg_ref, dy_ref, dgate_ref):
        @pl.when(pl.program_id(0) == 0)
        def _():
            loss_ref[...] = jnp.zeros_like(loss_ref)
            dg_ref[...] = jnp.zeros_like(dg_ref)
            dgate_ref[...] = jnp.zeros_like(dgate_ref)

        xv = x_ref[...]
        gv = g_ref[...]
        rstd = lax.rsqrt(jnp.mean(xv * xv, axis=-1, keepdims=True) + EPS)
        xhat = xv * rstd
        err = xhat * gv - t_ref[...]
        loss_ref[...] += 0.5 * jnp.sum(jnp.mean(err * err, axis=-1, keepdims=True))
        dy = err * (1.0 / d)
        dg_ref[...] += jnp.sum(dy * xhat, axis=0, keepdims=True)
        dxhat = dy * gv
        proj = jnp.mean(dxhat * xhat, axis=-1, keepdims=True)
        dx = rstd * (dxhat - xhat * proj)
        dx_ref[...] = dx
        _gate_terms(dx, y_ref, gate_ref, dy_ref, dgate_ref)

    row = pl.BlockSpec((ts, d), lambda i: (i, 0))
    vec = pl.BlockSpec((1, d), lambda i: (0, 0))
    return pl.pallas_call(
        body, name=name, grid=(s // ts,),
        in_specs=[row, vec, row, row, vec],
        out_specs=[row, pl.BlockSpec((1, LANES), lambda i: (0, 0)), vec, row, vec],
        out_shape=[jax.ShapeDtypeStruct((s, d), F32), jax.ShapeDtypeStruct((1, LANES), F32),
                   jax.ShapeDtypeStruct((1, d), F32), jax.ShapeDtypeStruct((s, d), BF16),
                   jax.ShapeDtypeStruct((1, d), F32)],
        compiler_params=_params("arbitrary"),
    )(x, g, target, y, gate)


def _conv(v, w_ref, width):
    out = w_ref[width - 1:width, :] * v
    for k in range(width - 1):
        out = out + w_ref[k:k + 1, :] * _shift_down(v, width - 1 - k)
    return out


def _sc_fwd(proj, conv_w, name, deps=()):
    _, s, e = proj.shape
    te = _tile(e, 256)
    width = conv_w.shape[0]

    def body(b_ref, c_ref, v_ref, g_ref, w_ref, o_ref):
        cv = c_ref[...].astype(F32) * v_ref[...].astype(F32)
        u = _conv(cv, w_ref, width)
        gv = g_ref[...].astype(F32)
        o_ref[...] = (b_ref[...].astype(F32) * u * (gv * _sigmoid(gv))).astype(o_ref.dtype)

    def part(q):
        return pl.BlockSpec((None, s, te), lambda j, q=q: (q, 0, j))

    return pl.pallas_call(
        _after(body, 5, deps), name=name, grid=(e // te,),
        in_specs=[part(0), part(1), part(2), part(3), pl.BlockSpec((width, te), lambda j: (0, j))]
        + [ANY] * len(deps),
        out_specs=pl.BlockSpec((s, te), lambda j: (0, j)),
        out_shape=jax.ShapeDtypeStruct((s, e), BF16),
        compiler_params=_params("arbitrary"),
    )(proj, proj, proj, proj, conv_w, *deps)


def _sc_bwd(proj, dyb, conv_w, name, deps=()):
    _, s, e = proj.shape
    te = _tile(e, 256)
    width = conv_w.shape[0]

    def body(b_ref, c_ref, v_ref, g_ref, dy_ref, w_ref, dp_ref, vec_ref):
        bv = b_ref[...].astype(F32)
        cvl = c_ref[...].astype(F32)
        vv = v_ref[...].astype(F32)
        gv = g_ref[...].astype(F32)
        dyv = dy_ref[...].astype(F32)
        cv = cvl * vv
        u = _conv(cv, w_ref, width)
        sg = _sigmoid(gv)
        silu = gv * sg
        dys = dyv * silu
        dp_ref[0] = (dys * u).astype(dp_ref.dtype)
        du = dys * bv
        dp_ref[3] = (dyv * bv * u * (sg + silu * (1.0 - sg))).astype(dp_ref.dtype)
        dcv = w_ref[width - 1:width, :] * du
        vec_ref[...] = jnp.zeros_like(vec_ref)
        vec_ref[width - 1:width, :] = jnp.sum(du * cv, axis=0, keepdims=True)
        for k in range(width - 1):
            sh = width - 1 - k
            dcv = dcv + w_ref[k:k + 1, :] * _shift_up(du, sh)
            vec_ref[k:k + 1, :] = jnp.sum(du * _shift_down(cv, sh), axis=0, keepdims=True)
        dp_ref[1] = (dcv * vv).astype(dp_ref.dtype)
        dp_ref[2] = (dcv * cvl).astype(dp_ref.dtype)

    def part(q):
        return pl.BlockSpec((None, s, te), lambda j, q=q: (q, 0, j))

    return pl.pallas_call(
        _after(body, 6, deps), name=name, grid=(e // te,),
        in_specs=[part(0), part(1), part(2), part(3), pl.BlockSpec((s, te), lambda j: (0, j)),
                  pl.BlockSpec((width, te), lambda j: (0, j))] + [ANY] * len(deps),
        out_specs=[pl.BlockSpec((4, s, te), lambda j: (0, 0, j)),
                   pl.BlockSpec((8, te), lambda j: (0, j))],
        out_shape=[jax.ShapeDtypeStruct((4, s, e), BF16), jax.ShapeDtypeStruct((8, e), F32)],
        compiler_params=_params("arbitrary"),
    )(proj, proj, proj, proj, dyb, conv_w, *deps)


def _lru_gates(v_pre, w_ref, cb_ref, wa_ref, ba_ref, wx_ref, bx_ref, lam_ref, width):
    v = _conv(v_pre, w_ref, width) + cb_ref[...]
    vb = v.astype(BF16)
    r = _sigmoid(jnp.dot(vb, wa_ref[...], preferred_element_type=F32) + ba_ref[...])
    i = _sigmoid(jnp.dot(vb, wx_ref[...], preferred_element_type=F32) + bx_ref[...])
    nl = -lam_ref[...]
    sp = jnp.maximum(nl, 0.0) + jnp.log1p(jnp.exp(-jnp.abs(nl)))
    log_a = (-RGLRU_C) * r * sp
    a = jnp.exp(log_a)
    one_minus_a2 = jnp.tanh(-log_a) * (1.0 + a * a)
    mult = jnp.sqrt(one_minus_a2)
    return v, vb, r, i, sp, a, mult


def _lru_specs(s, dh, heads, width):
    head_col = lambda q: pl.BlockSpec((None, s, dh), lambda h, q=q: (q, 0, h))
    vec = pl.BlockSpec((1, dh), lambda h: (0, h))
    mat = pl.BlockSpec((None, dh, dh), lambda h: (h, 0, 0))
    weights = [pl.BlockSpec((width, dh), lambda h: (0, h)), vec, mat, vec, mat, vec, vec]
    return head_col, weights


def _lru_fwd(proj, conv_w, conv_b, w_a, b_a, w_x, b_x, lam, name, deps=()):
    _, s, e = proj.shape
    heads, dh, _ = w_a.shape
    width = conv_w.shape[0]

    def body(v_ref, g_ref, w_ref, cb_ref, wa_ref, ba_ref, wx_ref, bx_ref, lam_ref, yb_ref, keep_ref):
        v, _, r, i, _, a, mult = _lru_gates(v_ref[...].astype(F32), w_ref, cb_ref, wa_ref, ba_ref,
                                           wx_ref, bx_ref, lam_ref, width)
        hs = _scan(a, mult * i * v, _shift_down)
        for k, val in enumerate((hs, v, r, i, a, mult)):
            keep_ref[k] = val
        gv = g_ref[...].astype(F32)
        yb_ref[...] = (hs * (gv * _sigmoid(gv))).astype(yb_ref.dtype)

    head_col, weights = _lru_specs(s, dh, heads, width)
    return pl.pallas_call(
        _after(body, 9, deps), name=name, grid=(heads,),
        in_specs=[head_col(0), head_col(1)] + weights + [ANY] * len(deps),
        out_specs=[pl.BlockSpec((s, dh), lambda h: (0, h)), pl.BlockSpec((6, s, dh), lambda h: (0, 0, h))],
        out_shape=[jax.ShapeDtypeStruct((s, e), BF16), jax.ShapeDtypeStruct((6, s, e), F32)],
        compiler_params=_params("arbitrary"),
    )(proj, proj, conv_w, conv_b, w_a, b_a, w_x, b_x, lam, *deps)


def _lru_bwd(proj, keep, dyb, conv_w, conv_b, w_a, b_a, w_x, b_x, lam, name, deps=()):
    _, s, e = proj.shape
    heads, dh, _ = w_a.shape
    width = conv_w.shape[0]

    def body(v_ref, g_ref, hs_ref, dy_ref, w_ref, cb_ref, wa_ref, ba_ref, wx_ref, bx_ref, lam_ref,
             dp_ref, dwa_ref, dwx_ref, vec_ref):
        v_pre = v_ref[...].astype(F32)
        hs, v, r, i, a, mult = (hs_ref[k] for k in range(6))
        vb = v.astype(BF16)
        nl = -lam_ref[...]
        sp = jnp.maximum(nl, 0.0) + jnp.log1p(jnp.exp(-jnp.abs(nl)))
        gv = g_ref[...].astype(F32)
        dyv = dy_ref[...].astype(F32)
        sg = _sigmoid(gv)
        silu = gv * sg
        dp_ref[1] = (dyv * hs * (sg + silu * (1.0 - sg))).astype(dp_ref.dtype)
        dhs = dyv * silu
        d_h = _scan(_shift_up(a, 1), dhs, _shift_up)
        da = d_h * _shift_down(hs, 1)
        iv = i * v
        dlog_a = da * a - (d_h * iv) * (a * a) / mult
        dhm = d_h * mult
        di = dhm * v
        dv = dhm * i
        dlr = dlog_a * r
        dzr = dlr * (1.0 - r) * ((-RGLRU_C) * sp)
        dzi = di * i * (1.0 - i)
        dsp = jnp.sum(dlr, axis=0, keepdims=True) * (-RGLRU_C)
        vec_ref[...] = jnp.zeros_like(vec_ref)
        vec_ref[0:1, :] = jnp.sum(dzr, axis=0, keepdims=True)
        vec_ref[1:2, :] = jnp.sum(dzi, axis=0, keepdims=True)
        vec_ref[2:3, :] = -dsp * _sigmoid(-lam_ref[...])
        dzr_b = dzr.astype(BF16)
        dzi_b = dzi.astype(BF16)
        vt = vb.astype(F32).T.astype(BF16)
        dwa_ref[...] = jnp.dot(vt, dzr_b, preferred_element_type=F32).astype(dwa_ref.dtype)
        dwx_ref[...] = jnp.dot(vt, dzi_b, preferred_element_type=F32).astype(dwx_ref.dtype)
        nt = (((1,), (1,)), ((), ()))
        dv = dv + lax.dot_general(dzr_b, wa_ref[...], nt, preferred_element_type=F32)
        dv = dv + lax.dot_general(dzi_b, wx_ref[...], nt, preferred_element_type=F32)
        vec_ref[3:4, :] = jnp.sum(dv, axis=0, keepdims=True)
        dvp = w_ref[width - 1:width, :] * dv
        vec_ref[4 + width - 1:4 + width, :] = jnp.sum(dv * v_pre, axis=0, keepdims=True)
        for k in range(width - 1):
            sh = width - 1 - k
            dvp = dvp + w_ref[k:k + 1, :] * _shift_up(dv, sh)
            vec_ref[4 + k:5 + k, :] = jnp.sum(dv * _shift_down(v_pre, sh), axis=0, keepdims=True)
        dp_ref[0] = dvp.astype(dp_ref.dtype)

    head_col, weights = _lru_specs(s, dh, heads, width)
    col = pl.BlockSpec((s, dh), lambda h: (0, h))
    mat = pl.BlockSpec((None, dh, dh), lambda h: (h, 0, 0))
    return pl.pallas_call(
        _after(body, 11, deps), name=name, grid=(heads,),
        in_specs=[head_col(0), head_col(1), pl.BlockSpec((6, s, dh), lambda h: (0, 0, h)), col] + weights
        + [ANY] * len(deps),
        out_specs=[pl.BlockSpec((2, s, dh), lambda h: (0, 0, h)), mat, mat,
                   pl.BlockSpec((16, dh), lambda h: (0, h))],
        out_shape=[jax.ShapeDtypeStruct((2, s, e), BF16),
                   jax.ShapeDtypeStruct((heads, dh, dh), BF16),
                   jax.ShapeDtypeStruct((heads, dh, dh), BF16),
                   jax.ShapeDtypeStruct((16, e), F32)],
        compiler_params=_params("arbitrary"),
    )(proj, proj, keep, dyb, conv_w, conv_b, w_a, b_a, w_x, b_x, lam, *deps)


def _ada_mod(c_all, w, b, name):
    layers, d, f = w.shape
    nb = c_all.shape[0]

    def body(c_ref, w_ref, b_ref, o_ref):
        cv = c_ref[...]
        sc = cv * _sigmoid(cv)
        o_ref[...] = jnp.dot(sc, w_ref[...], preferred_element_type=F32,
                             precision=lax.Precision.HIGHEST) + b_ref[...]

    return pl.pallas_call(
        body, name=name, grid=(layers,),
        in_specs=[pl.BlockSpec((nb, d), lambda l: (0, 0)),
                  pl.BlockSpec((None, d, f), lambda l: (l, 0, 0)),
                  pl.BlockSpec((None, 1, f), lambda l: (l, 0, 0))],
        out_specs=pl.BlockSpec((None, nb, f), lambda l: (l, 0, 0)),
        out_shape=jax.ShapeDtypeStruct((layers, nb, f), F32),
        compiler_params=_params("arbitrary"),
    )(c_all, w, b)


def _ada_update(c_all_t, dmod, w, m, v, name):
    d, nb = c_all_t.shape
    layers, _, f = dmod.shape
    tr = _tile(d, 512)

    def body(c_ref, dm_ref, w_ref, m_ref, v_ref, g_ref, d_ref, mo_ref, vo_ref):
        cv = c_ref[...]
        sc = cv * _sigmoid(cv)
        g = sc[:, 0:1] * dm_ref[0:1, :]
        for k in range(1, nb):
            g = g + sc[:, k:k + 1] * dm_ref[k:k + 1, :]
        g_ref[...] = g
        d_ref[...], mo_ref[...], vo_ref[...] = _adamw_math(w_ref[...], g, m_ref[...], v_ref[...])

    blk = pl.BlockSpec((None, tr, f), lambda l, i: (l, i, 0))
    return pl.pallas_call(
        body, name=name, grid=(layers, d // tr),
        in_specs=[pl.BlockSpec((tr, nb), lambda l, i: (i, 0)),
                  pl.BlockSpec((None, nb, f), lambda l, i: (l, 0, 0)), blk, blk, blk],
        out_specs=[blk] * 4,
        out_shape=[jax.ShapeDtypeStruct((layers, d, f), F32)] * 4,
        compiler_params=_params("arbitrary", "arbitrary"),
    )(c_all_t, dmod, w, m, v)


def _device_sum(g, name):
    _, rows, _ = g.shape

    def body(g_ref, o_ref):
        acc = g_ref[0]
        for k in range(1, N_DEV):
            acc = acc + g_ref[k]
        o_ref[...] = acc

    return pl.pallas_call(
        body, name=name,
        in_specs=[VMEM_SPEC], out_specs=VMEM_SPEC,
        out_shape=jax.ShapeDtypeStruct((rows, LANES), F32),
        compiler_params=pltpu.CompilerParams(vmem_limit_bytes=VMEM_LIMIT),
    )(g)


def _adamw_math(w, g, m, v):
    m = ADAM_B1 * m + (1.0 - ADAM_B1) * g
    v = ADAM_B2 * v + (1.0 - ADAM_B2) * (g * g)
    m_hat = m / (1.0 - ADAM_B1 ** ADAM_STEP)
    v_hat = v / (1.0 - ADAM_B2 ** ADAM_STEP)
    delta = -ADAM_LR * (m_hat / (jnp.sqrt(v_hat) + ADAM_EPS) + ADAM_WD * w)
    return delta, m, v


def _adamw(w, g, m, v, name):
    rows, cols = w.shape
    tr = _tile(rows, 256)

    def body(w_ref, g_ref, m_ref, v_ref, d_ref, mo_ref, vo_ref):
        d_ref[...], mo_ref[...], vo_ref[...] = _adamw_math(w_ref[...], g_ref[...], m_ref[...], v_ref[...])

    blk = pl.BlockSpec((tr, cols), lambda i: (i, 0))
    return pl.pallas_call(
        body, name=name, grid=(rows // tr,),
        in_specs=[blk] * 4, out_specs=[blk] * 3,
        out_shape=[jax.ShapeDtypeStruct((rows, cols), F32)] * 3,
        compiler_params=_params("arbitrary"),
    )(w, g, m, v)


def _adamw_reduced(idx, w, m, v, part, got, recvs, name):
    rows, cols = w.shape
    tr = _tile(rows, 256)
    nr = len(recvs)

    def body(idx_ref, w_ref, m_ref, v_ref, p_ref, q_ref, *rest):
        g_ref, d_ref, mo_ref, vo_ref = rest[nr:]
        g = p_ref[...].astype(F32) + q_ref[...].astype(F32)
        for u_ref in rest[:nr]:
            for j in range(u_ref.shape[0]):
                g = g + u_ref[j].astype(F32)
        g_ref[...] = g
        d_ref[...], mo_ref[...], vo_ref[...] = _adamw_math(w_ref[...], g, m_ref[...], v_ref[...])

    blk = pl.BlockSpec((tr, cols), lambda i, idx: (i, 0))
    grid_spec = pltpu.PrefetchScalarGridSpec(
        num_scalar_prefetch=1, grid=(rows // tr,),
        in_specs=[blk, blk, blk,
                  pl.BlockSpec((None, None, tr, cols), lambda i, idx: (idx[3], idx[4], i, 0)),
                  pl.BlockSpec((None, None, tr, cols), lambda i, idx: (idx[3], 0, i, 0))]
        + [pl.BlockSpec((u.shape[0], tr, cols), lambda i, idx: (0, i, 0)) for u in recvs],
        out_specs=[blk] * 4)
    return pl.pallas_call(
        body, name=name, grid_spec=grid_spec,
        out_shape=[jax.ShapeDtypeStruct((rows, cols), F32)] * 4,
        compiler_params=_params("arbitrary"),
    )(idx, w, m, v, part, got, *recvs)


def _pack(vectors):
    flat = jnp.concatenate([v.reshape(-1).astype(F32) for v in vectors])
    pad = (-flat.shape[0]) % (8 * LANES)
    return jnp.pad(flat, (0, pad)).reshape(-1, LANES)


def _unpack(flat, shapes):
    out, off = [], 0
    for shp in shapes:
        size = math.prod(shp)
        out.append(flat[..., off:off + size].reshape(flat.shape[:-1] + tuple(shp)))
        off += size
    return out


def _my_slice(full, me, axis):
    size = full.shape[axis] // N_DEV
    return lax.dynamic_slice_in_dim(full, me * size, size, axis)


def kernel(x, c, norm_g, ada_w, ada_b, sc_w_in, sc_conv_w, sc_w_out, lru_w_in, lru_conv_w, lru_conv_b, lru_w_a, lru_b_a, lru_w_x, lru_b_x, lru_lambda, lru_w_out, final_g, loss_target, m_norm_g, m_ada_w, m_ada_b, m_sc_w_in, m_sc_conv_w, m_sc_w_out, m_lru_w_in, m_lru_conv_w, m_lru_conv_b, m_lru_w_a, m_lru_b_a, m_lru_w_x, m_lru_b_x, m_lru_lambda, m_lru_w_out, m_final_g, v_norm_g, v_ada_w, v_ada_b, v_sc_w_in, v_sc_conv_w, v_sc_w_out, v_lru_w_in, v_lru_conv_w, v_lru_conv_b, v_lru_w_a, v_lru_b_a, v_lru_w_x, v_lru_b_x, v_lru_lambda, v_lru_w_out, v_final_g):
    _, s, d = x.shape
    e = sc_w_out.shape[1] * N_DEV
    heads, dh_s, dh = lru_w_a.shape[1:]
    es = e // N_DEV
    f = ada_w.shape[2]
    mx, my, mc = _position()
    me = 4 * mx + 2 * my + mc
    chip = 2 * mx + my
    idx = jnp.stack([chip ^ 1, chip ^ 2, chip ^ 3, chip, mc]).astype(jnp.int32)

    x0 = x[0]
    target = loss_target[0]

    small_shapes = [(d,), (3, es), (4, es), (es,), (heads, dh_s), (heads, dh_s), (es,)]
    small = _small_gather(_pack([c, sc_conv_w, lru_conv_w, lru_conv_b, lru_b_a, lru_b_x, lru_lambda]),
                          "gather_small_weights").reshape(N_DEV, -1)
    c_all, cw3, cw4, cb, ba, bx, lam = _unpack(small, small_shapes)
    cw3 = cw3.transpose(1, 0, 2).reshape(3, e)
    cw4 = cw4.transpose(1, 0, 2).reshape(4, e)
    cb = cb.reshape(1, e)
    lam = lam.reshape(1, e)
    ba = ba.transpose(1, 0, 2).reshape(1, e)
    bx = bx.transpose(1, 0, 2).reshape(1, e)

    mine = [sc_w_in[0], sc_w_out[0], lru_w_in[0], lru_w_a[0].reshape(heads * dh_s, dh),
            lru_w_x[0].reshape(heads * dh_s, dh), lru_w_out[0]]
    lands = [lax.dynamic_update_slice(lax.empty((N_DEV,) + sh.shape, BF16), sh.astype(BF16)[None], (me, 0, 0))
             for sh in mine]
    every = [1, 2, 3, 0]
    units = [([0], [0]), ([0], [1]), ([0], [2]), ([0], [3]), ([1], every), ([2], every), ([3, 4], every),
             ([5], every)]
    sems, first_ld, started = _gather_start(lands[:1], units[:3], [small], "gather_start_first")
    lands = first_ld + lands[1:]

    ada_b_mine = _my_slice(ada_b, me, 1).reshape(2, 1, f)
    mod_mine = _ada_mod(c_all, ada_w, ada_b_mine, "ada_mod")
    mod_all = _small_gather(_pack([mod_mine]), "gather_mod", deps=[started])

    def start_later(after):
        far_sems, far_ld, tok = _gather_start(lands[:1], units[3:4], after, "gather_start_far")
        rest_units = [([i - 1 for i in members], ks) for members, ks in units[4:]]
        rest_sems, rest_ld, tok = _gather_start(lands[1:], rest_units, [tok], "gather_start_rest")
        sems.extend(far_sems + rest_sems)
        lands[:] = far_ld + rest_ld
        return tok

    def gathered(u, after_forward, name):
        members, ks = units[u]
        fwd, lnd, token = _gather_forward([lands[i] for i in members], ks, sems[u][0], sems[u][1],
                                          after_forward, "gather_forward_" + name)
        for i, ld in zip(members, lnd):
            lands[i] = ld

        def finish(after):
            out = _gather_finish([lands[i] for i in members], ks, fwd, after, "gather_finish_" + name)
            for i, ld in zip(members, out):
                lands[i] = ld
            return out

        return token, finish

    tok, finish_y = gathered(1, [mod_all], "sc_w_in_near_y")
    tok, finish_x = gathered(2, [tok], "sc_w_in_near_x")
    queued = start_later([tok])

    mod_all = mod_all.reshape(N_DEV, -1)
    mod_all = mod_all[:, :2 * N_DEV * f].reshape(N_DEV, 2, N_DEV, f)
    mod_all = mod_all.transpose(1, 2, 0, 3).reshape(2, N_DEV, 3 * d)
    mod = lax.dynamic_index_in_dim(mod_all, me, 1, keepdims=False)
    shift = [mod[l:l + 1, 0:d] for l in range(2)]
    scale = [mod[l:l + 1, d:2 * d] for l in range(2)]
    gate = [mod[l:l + 1, 2 * d:3 * d] for l in range(2)]
    ng = [norm_g[l:l + 1] for l in range(2)]
    fg = final_g.reshape(1, d)

    h0 = _norm_mod(x0, ng[0], scale[0], shift[0], "norm_mod_0", deps=[queued])
    proj0 = lax.empty((4, s, e), BF16)
    tok, _ = gathered(0, [h0], "sc_w_in_own")
    proj0 = _mm_proj_group(h0, lands[0], idx, 3, proj0, "mm_proj_0_own", deps=[tok])
    for u, name, finish in ((1, "near_y", finish_y), (2, "near_x", finish_x), (3, "far", None)):
        after = [proj0]
        if finish is None:
            tok, finish = gathered(u, [proj0], "sc_w_in_" + name)
            after = [tok]
        wg_in0, = finish(after)
        proj0 = _mm_proj_group(h0, wg_in0, idx, u - 1, proj0, "mm_proj_0_" + name)
    tok, finish = gathered(4, [proj0], "sc_w_out")
    yb0 = _sc_fwd(proj0, cw3, "sc_fwd", deps=[tok])
    w_out0 = finish([yb0])[0].reshape(e, d)
    x1, y0 = _mm_out(yb0, w_out0, x0, gate[0], "mm_out_0")
    tok, finish = gathered(5, [x1], "lru_w_in")
    h1 = _norm_mod(x1, ng[1], scale[1], shift[1], "norm_mod_1", deps=[tok])
    wg_in1, = finish([h1])
    proj1 = _mm_proj(h1, wg_in1, 2, "mm_proj_1")
    tok, finish = gathered(6, [proj1], "lru_gates")
    wg_a, wg_x = finish([tok])
    w_a = wg_a.reshape(N_DEV, heads, dh_s, dh).transpose(1, 0, 2, 3).reshape(heads, dh, dh)
    w_x = wg_x.reshape(N_DEV, heads, dh_s, dh).transpose(1, 0, 2, 3).reshape(heads, dh, dh)
    tok, finish = gathered(7, [w_a, w_x], "lru_w_out")
    yb1, hs = _lru_fwd(proj1, cw4, cb, w_a, ba, w_x, bx, lam, "lru_fwd", deps=[tok])
    w_out1 = finish([yb1])[0].reshape(e, d)
    x2, y1 = _mm_out(yb1, w_out1, x1, gate[1], "mm_out_1")
    dx2, loss_part, d_fg, dy1, dgate1 = _final_loss(x2, fg, target, y1, gate[1], "final_loss")

    def pieces(g, rows, cols):
        return g.reshape(4, 2, rows, cols)

    def by_rows(g):
        return g.reshape(heads, N_DEV, dh_s, dh).transpose(1, 0, 2, 3).reshape(N_DEV, heads * dh_s, dh)

    def pair_begin(parts, group):
        send, recv, parts, lnd, token = _pair_start(parts, "pair_start_" + group)
        return dict(parts=parts, lands=lnd, send=send, recv=recv, group=group), token

    def scatter_start(pair, names, after):
        group = pair["group"]
        parts, gots = _pair_wait(pair["parts"], pair["lands"], pair["send"], pair["recv"], after,
                                 "pair_wait_" + group)
        sums = [_pair_sum(idx, p, q, "pair_sum_" + nm) for p, q, nm in zip(parts, gots, names)]
        empties = [lax.empty(sm.shape, sm.dtype) for sm in sums]
        send, recv, sums, lnd, token = _chip_start(sums, empties, "chip_start_" + group)
        return dict(parts=parts, gots=gots, names=names, group=group, sums=sums, lands=lnd,
                    send=send, recv=recv), token

    big = {"sc_w_in": (sc_w_in, m_sc_w_in, v_sc_w_in), "sc_w_out": (sc_w_out, m_sc_w_out, v_sc_w_out),
           "lru_w_in": (lru_w_in, m_lru_w_in, v_lru_w_in), "lru_w_a": (lru_w_a, m_lru_w_a, v_lru_w_a),
           "lru_w_x": (lru_w_x, m_lru_w_x, v_lru_w_x), "lru_w_out": (lru_w_out, m_lru_w_out, v_lru_w_out)}
    big_res = {}

    def scatter_finish(rs, after):
        recvs = _chip_wait(rs["sums"], rs["lands"], rs["send"], rs["recv"], after, "chip_wait_" + rs["group"])
        done = []
        for p, q, u, nm in zip(rs["parts"], rs["gots"], recvs, rs["names"]):
            w, m, v = big[nm]
            shp2 = p.shape[2:]
            res = _adamw_reduced(idx, w.reshape(shp2), m.reshape(shp2), v.reshape(shp2), p, q, [u], "adamw_" + nm)
            big_res[nm] = [r.reshape(w.shape) for r in res]
            done.append(res[1])
        return done

    dw_out1 = _mm_tn(yb1, dy1[None], 1, "mm_dw_out_1")
    pair, tok = pair_begin([pieces(dw_out1, es, d)], "lru_w_out")
    dyb1 = _mm_nt(dy1[None], w_out1[None], BF16, "mm_dyb_1", deps=[tok])
    rs1, tok = scatter_start(pair, ["lru_w_out"], [dyb1])
    dproj1, dw_a, dw_x, vecs1 = _lru_bwd(proj1, hs, dyb1, cw4, cb, w_a, ba, w_x, bx, lam, "lru_bwd", deps=[tok])
    done = scatter_finish(rs1, [dproj1])
    dw_in1 = _mm_tn(h1, dproj1, N_DEV, "mm_dw_in_1", deps=done)
    pair, tok = pair_begin([pieces(dw_in1, d, 2 * es), pieces(by_rows(dw_a), heads * dh_s, dh),
                            pieces(by_rows(dw_x), heads * dh_s, dh)], "lru_in")
    dh1 = _mm_nt(dproj1, wg_in1, BF16, "mm_dh_1", deps=[tok])
    rs2, tok = scatter_start(pair, ["lru_w_in", "lru_w_a", "lru_w_x"], [dh1])
    dx1, dscale1, dshift1, dng1, dy0, dgate0 = _norm_mod_bwd(x1, dh1, dx2, ng[1], scale[1], "norm_mod_bwd_1",
                                                             below=(y0, gate[0]), deps=[tok])
    dw_out0 = _mm_tn(yb0, dy0[None], 1, "mm_dw_out_0")
    pair, tok = pair_begin([pieces(dw_out0, es, d)], "sc_w_out")
    dyb0 = _mm_nt(dy0[None], w_out0[None], BF16, "mm_dyb_0", deps=[tok])
    rs3, tok = scatter_start(pair, ["sc_w_out"], [dyb0])
    dproj0, vecs0 = _sc_bwd(proj0, dyb0, cw3, "sc_bwd", deps=[tok])
    idx_one = jnp.stack([jnp.zeros_like(mc)] * 4 + [mc]).astype(jnp.int32)
    sc_w_in_steps = []

    def chip_step(j, pair, after):
        (part,), (got,) = _pair_wait(pair["parts"], pair["lands"], pair["send"], pair["recv"], after,
                                     "pair_wait_sc_w_in_%d" % j)
        sm = _pair_sum(idx_one, part, got, "pair_sum_sc_w_in_%d" % j, nslots=1)
        send, recv, sums, lnd, token = _chip_start([sm], [lax.empty(sm.shape, sm.dtype)],
                                                   "chip_start_sc_w_in_%d" % j, flips=(j,))
        sc_w_in_steps.append((sums, lnd, send, recv, j))
        return token

    pending, done = None, []
    for j in (3, 2, 1, 0):
        part = _mm_tn_group(h0, dproj0, idx, (j - 1) % 4, 2, "mm_dw_in_0_%d" % j, deps=done)[None]
        pair, tok = pair_begin([part], "sc_w_in_%d" % j)
        if j == 3:
            done = [chip_step(j, pair, [tok])]
            continue
        done = [tok]
        if pending is not None:
            done.append(chip_step(pending[0], pending[1], [tok]))
        pending = (j, pair)
    done += scatter_finish(rs2, done)
    dh0 = _mm_nt(dproj0, wg_in0, BF16, "mm_dh_0", deps=done)
    pair = pending[1]
    (part,), (got,) = _pair_wait(pair["parts"], pair["lands"], pair["send"], pair["recv"], [dh0],
                                 "pair_wait_sc_w_in_0")
    dx0, dscale0, dshift0, dng0 = _norm_mod_bwd(x0, dh0, dx1, ng[0], scale[0], "norm_mod_bwd_0")
    done = scatter_finish(rs3, [dx0])
    dmod_mine = jnp.concatenate([dshift0, dscale0, dgate0, dshift1, dscale1, dgate1], axis=1)
    end_shapes = [(LANES,), (2, 3 * d), (2, d), (d,), (8, e), (16, e)]
    end_all = _small_gather(
        _pack([loss_part, dmod_mine, jnp.concatenate([dng0, dng1], axis=0), d_fg, vecs0, vecs1]),
        "gather_small_grads", deps=done)
    end_sum = _device_sum(end_all, "sum_small_grads").reshape(-1)
    loss_v, g_ada_b, g_norm_g, g_final_g, sum0, sum1 = _unpack(end_sum, end_shapes)
    loss = loss_v[0]
    dmod_all = _unpack(end_all.reshape(N_DEV, -1), end_shapes)[1].transpose(1, 0, 2)
    dmod_cols = _my_slice(dmod_all, me, 2)
    ada_out = _ada_update(c_all.T, dmod_cols, ada_w, m_ada_w, v_ada_w, "ada_update")

    g_sc_conv_w = _my_slice(sum0[0:3], me, 1)
    g_lru_b_a = _my_slice(sum1[0].reshape(heads, dh), me, 1)
    g_lru_b_x = _my_slice(sum1[1].reshape(heads, dh), me, 1)
    g_lru_lambda = _my_slice(sum1[2:3], me, 1)
    g_lru_conv_b = _my_slice(sum1[3:4], me, 1)
    g_lru_conv_w = _my_slice(sum1[4:8], me, 1)

    small_w = [norm_g, ada_b, final_g, sc_conv_w, lru_conv_w, lru_conv_b, lru_b_a, lru_b_x, lru_lambda]
    small_m = [m_norm_g, m_ada_b, m_final_g, m_sc_conv_w, m_lru_conv_w, m_lru_conv_b, m_lru_b_a, m_lru_b_x,
               m_lru_lambda]
    small_v = [v_norm_g, v_ada_b, v_final_g, v_sc_conv_w, v_lru_conv_w, v_lru_conv_b, v_lru_b_a, v_lru_b_x,
               v_lru_lambda]
    small_g = [g_norm_g, g_ada_b, g_final_g, g_sc_conv_w, g_lru_conv_w, g_lru_conv_b, g_lru_b_a, g_lru_b_x,
               g_lru_lambda]
    small_g = [g.reshape(w.shape) for g, w in zip(small_g, small_w)]
    shapes = [w.shape for w in small_w]
    packed = _adamw(_pack(small_w), _pack(small_g), _pack(small_m), _pack(small_v), "adamw_small")
    small_out = [small_g] + [_unpack(p.reshape(-1), shapes) for p in packed]

    after = [packed[0], ada_out[1]]
    recvs = []
    for sums, lnd, send, recv, j in sc_w_in_steps:
        recvs += _chip_wait(sums, lnd, send, recv, after, "chip_wait_sc_w_in_%d" % j)
    shp2 = part.shape[2:]
    res = _adamw_reduced(idx_one, sc_w_in.reshape(shp2), m_sc_w_in.reshape(shp2), v_sc_w_in.reshape(shp2),
                         part, got, recvs, "adamw_sc_w_in")
    big_res["sc_w_in"] = [r.reshape(sc_w_in.shape) for r in res]
    big_out = [big_res[nm] for nm in ("sc_w_in", "sc_w_out", "lru_w_in", "lru_w_a", "lru_w_x", "lru_w_out")]

    def small(kind, i):
        return small_out[kind][i]

    def bigw(kind, i):
        return big_out[i][kind]

    outs = [loss, dx0[None]]
    for kind in range(4):
        outs += [small(kind, 0), ada_out[kind], small(kind, 1), bigw(kind, 0), small(kind, 3), bigw(kind, 1),
                 bigw(kind, 2), small(kind, 4), small(kind, 5), bigw(kind, 3), small(kind, 6), bigw(kind, 4),
                 small(kind, 7), small(kind, 8), bigw(kind, 5), small(kind, 2)]
    return tuple(outs)
```

```python
import math

import jax
import jax.numpy as jnp
from jax import lax
from jax.experimental import pallas as pl
from jax.experimental.pallas import tpu as pltpu

N_DEV = 8
LANES = 128
EPS = 1e-6
RGLRU_C = 8.0
ADAM_LR = 0.001
ADAM_B1 = 0.9
ADAM_B2 = 0.999
ADAM_EPS = 1e-08
ADAM_WD = 0.01
ADAM_STEP = 10
VMEM_LIMIT = 56 * 1024 * 1024
MESH = pl.DeviceIdType.MESH
F32 = jnp.float32
BF16 = jnp.bfloat16
ANY = pl.BlockSpec(memory_space=pl.ANY)
HBM = pl.BlockSpec(memory_space=pltpu.HBM)
SEM = pl.BlockSpec(memory_space=pltpu.SEMAPHORE)
VMEM_SPEC = pl.BlockSpec(memory_space=pltpu.VMEM)
EFFECT = pltpu.SideEffectType.DATAFLOW_SIDE_EFFECTING
TOKEN = jax.ShapeDtypeStruct((8, LANES), jnp.float32)


def _tile(n, pref):
    t = min(n, pref)
    assert n % t == 0, (n, pref)
    return t


def _params(*sem):
    return pltpu.CompilerParams(dimension_semantics=sem, vmem_limit_bytes=VMEM_LIMIT)


def _position():
    return lax.axis_index("x"), lax.axis_index("y"), lax.axis_index("c")


def _flip(x, y, k):
    return (1 - x if k & 2 else x), (1 - y if k & 1 else y)


def _after(body, n_in, deps):
    if not deps:
        return body

    def wrapped(*refs):
        return body(*refs[:n_in], *refs[n_in + len(deps):])

    return wrapped


def _small_gather(v, name, deps=()):
    rows = v.shape[0]

    def body(v_ref, out_ref, send_sems, recv_sems):
        x, y, c = _position()
        me = 4 * x + 2 * y + c
        out_ref[me] = v_ref[...]
        copies = []
        for k in range(1, N_DEV):
            px, py = _flip(x, y, k >> 1)
            pc = 1 - c if k & 1 else c
            cp = pltpu.make_async_remote_copy(
                src_ref=v_ref, dst_ref=out_ref.at[me],
                send_sem=send_sems.at[k - 1], recv_sem=recv_sems.at[k - 1],
                device_id=(px, py, pc), device_id_type=MESH)
            cp.start()
            copies.append((cp, 4 * px + 2 * py + pc))
        for k, (cp, peer) in enumerate(copies):
            pltpu.make_async_remote_copy(
                src_ref=v_ref, dst_ref=out_ref.at[peer],
                send_sem=send_sems.at[k], recv_sem=recv_sems.at[k],
                device_id=(x, y, c), device_id_type=MESH).wait_recv()
        for cp, _ in copies:
            cp.wait_send()

    return pl.pallas_call(
        _after(body, 1, deps), name=name,
        out_shape=jax.ShapeDtypeStruct((N_DEV, rows, LANES), F32),
        in_specs=[VMEM_SPEC] + [ANY] * len(deps), out_specs=VMEM_SPEC,
        scratch_shapes=[pltpu.SemaphoreType.DMA((N_DEV - 1,)),
                        pltpu.SemaphoreType.DMA((N_DEV - 1,))],
        compiler_params=pltpu.CompilerParams(vmem_limit_bytes=VMEM_LIMIT),
    )(v, *deps)


def _hbm(a):
    return pltpu.with_memory_space_constraint(a, pltpu.HBM)


def _hbm_like(arrays):
    return [pltpu.HBM(a.shape, a.dtype) for a in arrays]


def _remote(src, dst, send, recv, to):
    return pltpu.make_async_remote_copy(src_ref=src, dst_ref=dst, send_sem=send, recv_sem=recv,
                                        device_id=to, device_id_type=MESH)


def _gather_start(lands, units, after, name):
    n, nu = len(lands), len(units)

    def body(*refs):
        lnd = refs[:n]
        sems = refs[n + len(after):n + len(after) + 2 * nu]
        token = refs[-1]
        x, y, c = _position()
        me = 4 * x + 2 * y + c
        targets = [(x, y, 1 - c)] + [(px, py, c) for px, py in (_flip(x, y, k) for k in (1, 2, 3))]
        for u, (members, ks) in enumerate(units):
            for slot, i in enumerate(members):
                for ki, k in enumerate(ks):
                    at = len(ks) * slot + ki
                    mine = lnd[i].at[me]
                    _remote(mine, mine, sems[2 * u].at[at], sems[2 * u + 1].at[at], targets[k]).start()
        token[...] = jnp.zeros_like(token)

    sem_shapes = []
    for members, ks in units:
        count = len(members) * len(ks)
        sem_shapes += [pltpu.SemaphoreType.DMA((count,)), pltpu.SemaphoreType.DMA((count,))]
    out = pl.pallas_call(
        body, name=name,
        out_shape=sem_shapes + _hbm_like(lands) + [TOKEN],
        in_specs=[HBM] * n + [ANY] * len(after),
        out_specs=[SEM] * (2 * nu) + [HBM] * n + [VMEM_SPEC],
        input_output_aliases={i: 2 * nu + i for i in range(n)},
        compiler_params=pltpu.CompilerParams(has_side_effects=EFFECT),
    )(*[_hbm(l) for l in lands], *after)
    sems = [(out[2 * u], out[2 * u + 1]) for u in range(nu)]
    return sems, list(out[2 * nu:2 * nu + n]), out[-1]


def _gather_forward(lands, ks, send, recv, after, name):
    m = len(lands)
    hops = [k for k in ks if k]
    nsem = 2 if hops else 0

    def body(*refs):
        lnd = refs[:m]
        send_ref, recv_ref = refs[m], refs[m + 1]
        outs = refs[m + 2 + len(after):]
        token = refs[-1]
        x, y, c = _position()
        me = (x, y, c)
        for slot in range(m):
            for ki, k in enumerate(ks):
                at = len(ks) * slot + ki
                if k:
                    px, py = _flip(x, y, k)
                    block = lnd[slot].at[4 * px + 2 * py + c]
                else:
                    block = lnd[slot].at[4 * x + 2 * y + (1 - c)]
                arrival = _remote(lnd[slot].at[4 * x + 2 * y + c], block, send_ref.at[at], recv_ref.at[at], me)
                arrival.wait_recv()
                if k:
                    fat = len(hops) * slot + hops.index(k)
                    _remote(block, block, outs[0].at[fat], outs[1].at[fat], (x, y, 1 - c)).start()
                arrival.wait_send()
        token[...] = jnp.zeros_like(token)

    count = len(hops) * m
    sem_shapes = [pltpu.SemaphoreType.DMA((count,)), pltpu.SemaphoreType.DMA((count,))] if hops else []
    out = pl.pallas_call(
        body, name=name,
        out_shape=sem_shapes + _hbm_like(lands) + [TOKEN],
        in_specs=[HBM] * m + [SEM, SEM] + [ANY] * len(after),
        out_specs=[SEM] * nsem + [HBM] * m + [VMEM_SPEC],
        input_output_aliases={i: nsem + i for i in range(m)},
        compiler_params=pltpu.CompilerParams(has_side_effects=EFFECT),
    )(*lands, send, recv, *after)
    fwd = (out[0], out[1]) if hops else None
    return fwd, list(out[nsem:nsem + m]), out[-1]


def _gather_finish(lands, ks, fwd, after, name):
    m = len(lands)
    hops = [k for k in ks if k]

    def body(*refs):
        lnd = refs[:m]
        fsend_ref, frecv_ref = refs[m], refs[m + 1]
        x, y, c = _position()
        for slot in range(m):
            for fi, k in enumerate(hops):
                px, py = _flip(x, y, k)
                sent = lnd[slot].at[4 * px + 2 * py + c]
                came = lnd[slot].at[4 * px + 2 * py + (1 - c)]
                fat = len(hops) * slot + fi
                cp = _remote(sent, came, fsend_ref.at[fat], frecv_ref.at[fat], (x, y, c))
                cp.wait_recv()
                cp.wait_send()

    out = pl.pallas_call(
        body, name=name,
        out_shape=_hbm_like(lands),
        in_specs=[HBM] * m + [SEM, SEM] + [ANY] * len(after), out_specs=[HBM] * m,
        input_output_aliases={i: i for i in range(m)},
        compiler_params=pltpu.CompilerParams(has_side_effects=EFFECT),
    )(*lands, fwd[0], fwd[1], *after)
    return list(out)


def _pair_start(parts, name):
    n = len(parts)
    lands = [lax.empty((p.shape[0], 1) + p.shape[2:], p.dtype) for p in parts]

    def body(*refs):
        ins, lnd = refs[:n], refs[n:2 * n]
        send_ref, recv_ref = refs[2 * n], refs[2 * n + 1]
        token = refs[-1]
        x, y, c = _position()
        for i in range(n):
            _remote(ins[i].at[:, pl.ds(1 - c, 1)], lnd[i], send_ref.at[i], recv_ref.at[i], (x, y, 1 - c)).start()
        token[...] = jnp.zeros_like(token)

    out = pl.pallas_call(
        body, name=name,
        out_shape=[pltpu.SemaphoreType.DMA((n,)), pltpu.SemaphoreType.DMA((n,))]
        + _hbm_like(parts) + _hbm_like(lands) + [TOKEN],
        in_specs=[HBM] * (2 * n), out_specs=[SEM, SEM] + [HBM] * (2 * n) + [VMEM_SPEC],
        input_output_aliases={i: 2 + i for i in range(2 * n)},
        compiler_params=pltpu.CompilerParams(has_side_effects=EFFECT),
    )(*[_hbm(p) for p in parts], *[_hbm(l) for l in lands])
    return out[0], out[1], list(out[2:2 + n]), list(out[2 + n:2 + 2 * n]), out[-1]


def _pair_wait(parts, lands, send, recv, after, name):
    n = len(parts)

    def body(*refs):
        ins, lnd = refs[:n], refs[n:2 * n]
        send_ref, recv_ref = refs[2 * n], refs[2 * n + 1]
        x, y, c = _position()
        for i in range(n):
            cp = _remote(ins[i].at[:, pl.ds(1 - c, 1)], lnd[i], send_ref.at[i], recv_ref.at[i], (x, y, c))
            cp.wait_recv()
            cp.wait_send()

    out = pl.pallas_call(
        body, name=name,
        out_shape=_hbm_like(parts) + _hbm_like(lands),
        in_specs=[HBM] * (2 * n) + [SEM, SEM] + [ANY] * len(after), out_specs=[HBM] * (2 * n),
        input_output_aliases={i: i for i in range(2 * n)},
        compiler_params=pltpu.CompilerParams(has_side_effects=EFFECT),
    )(*parts, *lands, send, recv, *after)
    return list(out[:n]), list(out[n:])


def _chip_start(sums, lands, name, flips=(1, 2, 3)):
    n, ns = len(sums), len(flips)

    def body(*refs):
        ins, lnd = refs[:n], refs[n:2 * n]
        send_ref, recv_ref = refs[2 * n], refs[2 * n + 1]
        token = refs[-1]
        x, y, c = _position()
        for i in range(n):
            for j, flip in enumerate(flips):
                px, py = _flip(x, y, flip)
                _remote(ins[i].at[j], lnd[i].at[j], send_ref.at[ns * i + j], recv_ref.at[ns * i + j],
                        (px, py, c)).start()
        token[...] = jnp.zeros_like(token)

    out = pl.pallas_call(
        body, name=name,
        out_shape=[pltpu.SemaphoreType.DMA((ns * n,)), pltpu.SemaphoreType.DMA((ns * n,))]
        + _hbm_like(sums) + _hbm_like(lands) + [TOKEN],
        in_specs=[HBM] * (2 * n), out_specs=[SEM, SEM] + [HBM] * (2 * n) + [VMEM_SPEC],
        input_output_aliases={i: 2 + i for i in range(2 * n)},
        compiler_params=pltpu.CompilerParams(has_side_effects=EFFECT),
    )(*[_hbm(s) for s in sums], *[_hbm(l) for l in lands])
    return out[0], out[1], out[2:2 + n], out[2 + n:2 + 2 * n], out[-1]


def _chip_wait(sums, lands, send, recv, after, name):
    n, ns = len(sums), sums[0].shape[0]

    def body(*refs):
        ins, lnd = refs[:n], refs[n:2 * n]
        send_ref, recv_ref = refs[2 * n], refs[2 * n + 1]
        x, y, c = _position()
        for i in range(n):
            for j in range(ns):
                cp = _remote(ins[i].at[j], lnd[i].at[j], send_ref.at[ns * i + j], recv_ref.at[ns * i + j], (x, y, c))
                cp.wait_recv()
                cp.wait_send()

    out = pl.pallas_call(
        body, name=name,
        out_shape=_hbm_like(sums) + _hbm_like(lands),
        in_specs=[HBM] * (2 * n) + [SEM, SEM] + [ANY] * len(after), out_specs=[HBM] * (2 * n),
        input_output_aliases={i: i for i in range(2 * n)},
        compiler_params=pltpu.CompilerParams(has_side_effects=EFFECT),
    )(*sums, *lands, send, recv, *after)
    return list(out[n:])


def _pair_sum(idx, part, got, name, nslots=3):
    _, _, rows, cols = part.shape
    tr = _tile(rows, 1024)

    def body(idx_ref, p_ref, q_ref, o_ref):
        o_ref[...] = (p_ref[...].astype(F32) + q_ref[...].astype(F32)).astype(o_ref.dtype)

    grid_spec = pltpu.PrefetchScalarGridSpec(
        num_scalar_prefetch=1, grid=(nslots, rows // tr),
        in_specs=[pl.BlockSpec((None, None, tr, cols), lambda j, r, idx: (idx[j], idx[4], r, 0)),
                  pl.BlockSpec((None, None, tr, cols), lambda j, r, idx: (idx[j], 0, r, 0))],
        out_specs=pl.BlockSpec((None, tr, cols), lambda j, r, idx: (j, r, 0)))
    return pl.pallas_call(
        body, name=name, grid_spec=grid_spec,
        out_shape=jax.ShapeDtypeStruct((nslots, rows, cols), part.dtype),
        compiler_params=_params("arbitrary", "arbitrary"),
    )(idx, part, got)


def _mm_proj(h, wg, groups, name):
    s, k = h.shape
    nchunk, _, n = wg.shape
    e = nchunk * n // groups
    tn = _tile(min(n, e), 512)

    def body(h_ref, w_ref, o_ref):
        o_ref[...] = jnp.dot(h_ref[...], w_ref[...], preferred_element_type=F32).astype(o_ref.dtype)

    return pl.pallas_call(
        body, name=name, grid=(nchunk * n // tn,),
        in_specs=[pl.BlockSpec((s, k), lambda j: (0, 0)),
                  pl.BlockSpec((None, k, tn), lambda j: ((j * tn) // n, 0, ((j * tn) % n) // tn))],
        out_specs=pl.BlockSpec((None, s, tn), lambda j: ((j * tn) // e, 0, ((j * tn) % e) // tn)),
        out_shape=jax.ShapeDtypeStruct((groups, s, e), BF16),
        compiler_params=_params("arbitrary"),
    )(h, wg)


def _mm_proj_group(h, wg, idx, pos, prev, name, deps=()):
    s, k = h.shape
    _, _, n = wg.shape
    _, _, e = prev.shape
    tn = _tile(n, 512)
    nd = len(deps)

    def body(idx_ref, h_ref, w_ref, prev_ref, *rest):
        o_ref = rest[nd]
        o_ref[...] = jnp.dot(h_ref[...], w_ref[...], preferred_element_type=F32).astype(o_ref.dtype)

    def col(j, idx):
        return idx[pos] * (2 * n) + j * tn

    grid_spec = pltpu.PrefetchScalarGridSpec(
        num_scalar_prefetch=1, grid=(2 * n // tn,),
        in_specs=[pl.BlockSpec((s, k), lambda j, idx: (0, 0)),
                  pl.BlockSpec((None, k, tn), lambda j, idx: (col(j, idx) // n, 0, (col(j, idx) % n) // tn)),
                  ANY] + [ANY] * nd,
        out_specs=pl.BlockSpec((None, s, tn), lambda j, idx: (col(j, idx) // e, 0, (col(j, idx) % e) // tn)))
    return pl.pallas_call(
        body, name=name, grid_spec=grid_spec,
        out_shape=jax.ShapeDtypeStruct(prev.shape, prev.dtype),
        input_output_aliases={3: 0},
        compiler_params=_params("arbitrary"),
    )(idx, h, wg, prev, *deps)


def _mm_out(yb, w, x, gate, name):
    s, k = yb.shape
    d = w.shape[1]
    tn = _tile(d, 512)
    tk = _tile(k, 2048)
    nk = k // tk

    def body(a_ref, w_ref, x_ref, g_ref, xo_ref, y_ref, acc_ref):
        kk = pl.program_id(1)

        @pl.when(kk == 0)
        def _():
            acc_ref[...] = jnp.zeros_like(acc_ref)

        acc_ref[...] += jnp.dot(a_ref[...], w_ref[...], preferred_element_type=F32)

        @pl.when(kk == nk - 1)
        def _():
            y = acc_ref[...]
            y_ref[...] = y.astype(y_ref.dtype)
            xo_ref[...] = x_ref[...] + g_ref[...] * y

    return pl.pallas_call(
        body, name=name, grid=(d // tn, nk),
        in_specs=[pl.BlockSpec((s, tk), lambda j, kk: (0, kk)),
                  pl.BlockSpec((tk, tn), lambda j, kk: (kk, j)),
                  pl.BlockSpec((s, tn), lambda j, kk: (0, j)),
                  pl.BlockSpec((1, tn), lambda j, kk: (0, j))],
        out_specs=[pl.BlockSpec((s, tn), lambda j, kk: (0, j)),
                   pl.BlockSpec((s, tn), lambda j, kk: (0, j))],
        out_shape=[jax.ShapeDtypeStruct((s, d), F32), jax.ShapeDtypeStruct((s, d), BF16)],
        scratch_shapes=[pltpu.VMEM((s, tn), F32)],
        compiler_params=_params("arbitrary", "arbitrary"),
    )(yb, w, x, gate)


def _mm_nt(a3, w3, out_dtype, name, deps=()):
    g, s, ea = a3.shape
    cw, n, nw = w3.shape
    total = g * ea
    assert total == cw * nw
    tk = _tile(min(ea, nw), 2048)
    tn = _tile(n, 1024)
    nk = total // tk

    def body(a_ref, w_ref, o_ref, acc_ref):
        kk = pl.program_id(1)

        @pl.when(kk == 0)
        def _():
            acc_ref[...] = jnp.zeros_like(acc_ref)

        acc_ref[...] += lax.dot_general(a_ref[...], w_ref[...], (((1,), (1,)), ((), ())),
                                        preferred_element_type=F32)

        @pl.when(kk == nk - 1)
        def _():
            o_ref[...] = acc_ref[...].astype(o_ref.dtype)

    return pl.pallas_call(
        _after(body, 2, deps), name=name, grid=(n // tn, nk),
        in_specs=[pl.BlockSpec((None, s, tk), lambda j, kk: ((kk * tk) // ea, 0, ((kk * tk) % ea) // tk)),
                  pl.BlockSpec((None, tn, tk), lambda j, kk: ((kk * tk) // nw, j, ((kk * tk) % nw) // tk))]
        + [ANY] * len(deps),
        out_specs=pl.BlockSpec((s, tn), lambda j, kk: (0, j)),
        out_shape=jax.ShapeDtypeStruct((s, n), out_dtype),
        scratch_shapes=[pltpu.VMEM((s, tn), F32)],
        compiler_params=_params("arbitrary", "arbitrary"),
    )(a3, w3, *deps)


def _mm_tn(a, b3, nchunk, name, deps=()):
    s, ka = a.shape
    g, _, eb = b3.shape
    n = g * eb // nchunk
    tm = _tile(ka, 1024)
    tn = _tile(min(n, eb), 1024)

    def body(a_ref, b_ref, o_ref, at_ref):
        @pl.when(pl.program_id(1) == 0)
        def _():
            at_ref[...] = a_ref[...].astype(F32).T.astype(at_ref.dtype)

        o_ref[...] = jnp.dot(at_ref[...], b_ref[...], preferred_element_type=F32).astype(o_ref.dtype)

    return pl.pallas_call(
        _after(body, 2, deps), name=name, grid=(ka // tm, g * eb // tn),
        in_specs=[pl.BlockSpec((s, tm), lambda i, j: (0, i)),
                  pl.BlockSpec((None, s, tn), lambda i, j: ((j * tn) // eb, 0, ((j * tn) % eb) // tn))]
        + [ANY] * len(deps),
        out_specs=pl.BlockSpec((None, tm, tn), lambda i, j: ((j * tn) // n, i, ((j * tn) % n) // tn)),
        out_shape=jax.ShapeDtypeStruct((nchunk, ka, n), BF16),
        scratch_shapes=[pltpu.VMEM((tm, s), BF16)],
        compiler_params=_params("arbitrary", "arbitrary"),
    )(a, b3, *deps)


def _mm_tn_group(a, b3, idx, pos, nchunk, name, deps=()):
    s, ka = a.shape
    _, _, eb = b3.shape
    n = eb // nchunk
    tm = _tile(ka, 1024)
    tn = _tile(n, 1024)
    nd = len(deps)

    def body(idx_ref, a_ref, b_ref, *rest):
        o_ref, at_ref = rest[nd:]

        @pl.when(pl.program_id(1) == 0)
        def _():
            at_ref[...] = a_ref[...].astype(F32).T.astype(at_ref.dtype)

        o_ref[...] = jnp.dot(at_ref[...], b_ref[...], preferred_element_type=F32).astype(o_ref.dtype)

    grid_spec = pltpu.PrefetchScalarGridSpec(
        num_scalar_prefetch=1, grid=(ka // tm, eb // tn),
        in_specs=[pl.BlockSpec((s, tm), lambda i, j, idx: (0, i)),
                  pl.BlockSpec((None, s, tn), lambda i, j, idx: (idx[pos], 0, j))] + [ANY] * nd,
        out_specs=pl.BlockSpec((None, tm, tn), lambda i, j, idx: ((j * tn) // n, i, ((j * tn) % n) // tn)),
        scratch_shapes=[pltpu.VMEM((tm, s), BF16)])
    return pl.pallas_call(
        body, name=name, grid_spec=grid_spec,
        out_shape=jax.ShapeDtypeStruct((nchunk, ka, n), BF16),
        compiler_params=_params("arbitrary", "arbitrary"),
    )(idx, a, b3, *deps)


def _sigmoid(z):
    return jax.nn.sigmoid(z)


def _shift_down(v, k, fill=0.0, period=None):
    if k == 0:
        return v
    row = lax.broadcasted_iota(jnp.int32, v.shape, 0)
    if period is not None:
        row = row & (period - 1)
    return jnp.where(row >= k, pltpu.roll(v, k, 0), fill)


def _shift_up(v, k, fill=0.0, period=None):
    if k == 0:
        return v
    s = v.shape[0]
    row = lax.broadcasted_iota(jnp.int32, v.shape, 0)
    if period is not None:
        row, s = row & (period - 1), period
    return jnp.where(row < s - k, pltpu.roll(v, v.shape[0] - k, 0), fill)


SCAN_BLOCK = 64


def _scan(a, b, shift, emit=None):
    s = a.shape[0]
    blk = min(SCAN_BLOCK, s)
    k = 1
    while k < blk:
        b = a * shift(b, k, 0.0, blk) + b
        a = a * shift(a, k, 1.0, blk)
        k *= 2
    nblk = s // blk
    forward = shift is _shift_down
    order = range(nblk) if forward else range(nblk - 1, -1, -1)
    edge = blk - 1 if forward else 0
    out = [None] * nblk
    carry = None
    for i in order:
        h = b[i * blk:(i + 1) * blk]
        if carry is not None:
            h = a[i * blk:(i + 1) * blk] * carry + h
        carry = h[edge:edge + 1]
        if emit is not None:
            emit(i * blk, blk, h)
        else:
            out[i] = h
    if emit is not None:
        return None
    return jnp.concatenate(out, axis=0) if nblk > 1 else out[0]


def _norm_mod(x, g, scale, shift, name, deps=()):
    s, d = x.shape
    ts = _tile(s, 256)

    def body(x_ref, g_ref, sc_ref, sh_ref, h_ref):
        xv = x_ref[...]
        rstd = lax.rsqrt(jnp.mean(xv * xv, axis=-1, keepdims=True) + EPS)
        nrm = xv * rstd * g_ref[...]
        h_ref[...] = (nrm * (1.0 + sc_ref[...]) + sh_ref[...]).astype(h_ref.dtype)

    vec = pl.BlockSpec((1, d), lambda i: (0, 0))
    return pl.pallas_call(
        _after(body, 4, deps), name=name, grid=(s // ts,),
        in_specs=[pl.BlockSpec((ts, d), lambda i: (i, 0)), vec, vec, vec] + [ANY] * len(deps),
        out_specs=pl.BlockSpec((ts, d), lambda i: (i, 0)),
        out_shape=jax.ShapeDtypeStruct((s, d), BF16),
        compiler_params=_params("arbitrary"),
    )(x, g, scale, shift, *deps)


def _gate_terms(dx, y_ref, gate_ref, dy_ref, dgate_ref):
    dy_ref[...] = (dx * gate_ref[...]).astype(dy_ref.dtype)
    dgate_ref[...] += jnp.sum(dx * y_ref[...].astype(F32), axis=0, keepdims=True)


def _norm_mod_bwd(x, dh, dx_res, g, scale, name, below=None, deps=()):
    s, d = x.shape
    ts = _tile(s, 256)
    nb = 2 if below is not None else 0

    def body(x_ref, dh_ref, dr_ref, g_ref, sc_ref, *rest):
        dx_ref, dsc_ref, dsh_ref, dg_ref = rest[nb:nb + 4]

        @pl.when(pl.program_id(0) == 0)
        def _():
            for ref in rest[nb + 1:nb + 4] + rest[nb + 5:]:
                ref[...] = jnp.zeros_like(ref)

        xv = x_ref[...]
        dh_v = dh_ref[...].astype(F32)
        gv = g_ref[...]
        rstd = lax.rsqrt(jnp.mean(xv * xv, axis=-1, keepdims=True) + EPS)
        xhat = xv * rstd
        dsc_ref[...] += jnp.sum(dh_v * xhat * gv, axis=0, keepdims=True)
        dsh_ref[...] += jnp.sum(dh_v, axis=0, keepdims=True)
        dn = dh_v * (1.0 + sc_ref[...])
        dg_ref[...] += jnp.sum(dn * xhat, axis=0, keepdims=True)
        dxhat = dn * gv
        proj = jnp.mean(dxhat * xhat, axis=-1, keepdims=True)
        dx = dr_ref[...] + rstd * (dxhat - xhat * proj)
        dx_ref[...] = dx
        if nb:
            _gate_terms(dx, rest[0], rest[1], rest[nb + 4], rest[nb + 5])

    row = pl.BlockSpec((ts, d), lambda i: (i, 0))
    vec = pl.BlockSpec((1, d), lambda i: (0, 0))
    extra = list(below) if nb else []
    return pl.pallas_call(
        _after(body, 5 + nb, deps), name=name, grid=(s // ts,),
        in_specs=[row, row, row, vec, vec] + [row, vec][:nb] + [ANY] * len(deps),
        out_specs=[row, vec, vec, vec] + [row, vec][:nb],
        out_shape=[jax.ShapeDtypeStruct((s, d), F32)] + [jax.ShapeDtypeStruct((1, d), F32)] * 3
        + [jax.ShapeDtypeStruct((s, d), BF16), jax.ShapeDtypeStruct((1, d), F32)][:nb],
        compiler_params=_params("arbitrary"),
    )(x, dh, dx_res, g, scale, *extra, *deps)


def _final_loss(x, g, target, y, gate, name):
    s, d = x.shape
    ts = _tile(s, 256)

    def body(x_ref, g_ref, t_ref, y_ref, gate_ref, dx_ref, loss_ref, dg_ref, dy_ref, dgate_ref):
        @pl.when(pl.program_id(0) == 0)
        def _():
            loss_ref[...] = jnp.zeros_like(loss_ref)
            dg_ref[...] = jnp.zeros_like(dg_ref)
            dgate_ref[...] = jnp.zeros_like(dgate_ref)

        xv = x_ref[...]
        gv = g_ref[...]
        rstd = lax.rsqrt(jnp.mean(xv * xv, axis=-1, keepdims=True) + EPS)
        xhat = xv * rstd
        err = xhat * gv - t_ref[...]
        loss_ref[...] += 0.5 * jnp.sum(jnp.mean(err * err, axis=-1, keepdims=True))
        dy = err * (1.0 / d)
        dg_ref[...] += jnp.sum(dy * xhat, axis=0, keepdims=True)
        dxhat = dy * gv
        proj = jnp.mean(dxhat * xhat, axis=-1, keepdims=True)
        dx = rstd * (dxhat - xhat * proj)
        dx_ref[...] = dx
        _gate_terms(dx, y_ref, gate_ref, dy_ref, dgate_ref)

    row = pl.BlockSpec((ts, d), lambda i: (i, 0))
    vec = pl.BlockSpec((1, d), lambda i: (0, 0))
    return pl.pallas_call(
        body, name=name, grid=(s // ts,),
        in_specs=[row, vec, row, row, vec],
        out_specs=[row, pl.BlockSpec((1, LANES), lambda i: (0, 0)), vec, row, vec],
        out_shape=[jax.ShapeDtypeStruct((s, d), F32), jax.ShapeDtypeStruct((1, LANES), F32),
                   jax.ShapeDtypeStruct((1, d), F32), jax.ShapeDtypeStruct((s, d), BF16),
                   jax.ShapeDtypeStruct((1, d), F32)],
        compiler_params=_params("arbitrary"),
    )(x, g, target, y, gate)


def _conv(v, w_ref, width):
    out = w_ref[width - 1:width, :] * v
    for k in range(width - 1):
        out = out + w_ref[k:k + 1, :] * _shift_down(v, width - 1 - k)
    return out


def _sc_fwd(proj, conv_w, name, deps=()):
    _, s, e = proj.shape
    te = _tile(e, 256)
    width = conv_w.shape[0]

    def body(b_ref, c_ref, v_ref, g_ref, w_ref, o_ref):
        cv = c_ref[...].astype(F32) * v_ref[...].astype(F32)
        u = _conv(cv, w_ref, width)
        gv = g_ref[...].astype(F32)
        o_ref[...] = (b_ref[...].astype(F32) * u * (gv * _sigmoid(gv))).astype(o_ref.dtype)

    def part(q):
        return pl.BlockSpec((None, s, te), lambda j, q=q: (q, 0, j))

    return pl.pallas_call(
        _after(body, 5, deps), name=name, grid=(e // te,),
        in_specs=[part(0), part(1), part(2), part(3), pl.BlockSpec((width, te), lambda j: (0, j))]
        + [ANY] * len(deps),
        out_specs=pl.BlockSpec((s, te), lambda j: (0, j)),
        out_shape=jax.ShapeDtypeStruct((s, e), BF16),
        compiler_params=_params("arbitrary"),
    )(proj, proj, proj, proj, conv_w, *deps)


def _sc_bwd(proj, dyb, conv_w, name, deps=()):
    _, s, e = proj.shape
    te = _tile(e, 256)
    width = conv_w.shape[0]

    def body(b_ref, c_ref, v_ref, g_ref, dy_ref, w_ref, dp_ref, vec_ref):
        bv = b_ref[...].astype(F32)
        cvl = c_ref[...].astype(F32)
        vv = v_ref[...].astype(F32)
        gv = g_ref[...].astype(F32)
        dyv = dy_ref[...].astype(F32)
        cv = cvl * vv
        u = _conv(cv, w_ref, width)
        sg = _sigmoid(gv)
        silu = gv * sg
        dp_ref[0] = (dyv * u * silu).astype(dp_ref.dtype)
        du = dyv * bv * silu
        dp_ref[3] = (dyv * bv * u * (sg * (1.0 + gv * (1.0 - sg)))).astype(dp_ref.dtype)
        dcv = w_ref[width - 1:width, :] * du
        vec_ref[...] = jnp.zeros_like(vec_ref)
        vec_ref[width - 1:width, :] = jnp.sum(du * cv, axis=0, keepdims=True)
        for k in range(width - 1):
            sh = width - 1 - k
            dcv = dcv + w_ref[k:k + 1, :] * _shift_up(du, sh)
            vec_ref[k:k + 1, :] = jnp.sum(du * _shift_down(cv, sh), axis=0, keepdims=True)
        dp_ref[1] = (dcv * vv).astype(dp_ref.dtype)
        dp_ref[2] = (dcv * cvl).astype(dp_ref.dtype)

    def part(q):
        return pl.BlockSpec((None, s, te), lambda j, q=q: (q, 0, j))

    return pl.pallas_call(
        _after(body, 6, deps), name=name, grid=(e // te,),
        in_specs=[part(0), part(1), part(2), part(3), pl.BlockSpec((s, te), lambda j: (0, j)),
                  pl.BlockSpec((width, te), lambda j: (0, j))] + [ANY] * len(deps),
        out_specs=[pl.BlockSpec((4, s, te), lambda j: (0, 0, j)),
                   pl.BlockSpec((8, te), lambda j: (0, j))],
        out_shape=[jax.ShapeDtypeStruct((4, s, e), BF16), jax.ShapeDtypeStruct((8, e), F32)],
        compiler_params=_params("arbitrary"),
    )(proj, proj, proj, proj, dyb, conv_w, *deps)


def _lru_gates(v_pre, w_ref, cb_ref, wa_ref, ba_ref, wx_ref, bx_ref, lam_ref, width):
    v = _conv(v_pre, w_ref, width) + cb_ref[...]
    vb = v.astype(BF16)
    r = _sigmoid(jnp.dot(vb, wa_ref[...], preferred_element_type=F32) + ba_ref[...])
    i = _sigmoid(jnp.dot(vb, wx_ref[...], preferred_element_type=F32) + bx_ref[...])
    nl = -lam_ref[...]
    sp = jnp.maximum(nl, 0.0) + jnp.log1p(jnp.exp(-jnp.abs(nl)))
    log_a = (-RGLRU_C) * r * sp
    a = jnp.exp(log_a)
    one_minus_a2 = jnp.tanh(-log_a) * (1.0 + a * a)
    mult = jnp.sqrt(one_minus_a2)
    return v, vb, r, i, sp, a, mult


def _lru_specs(s, dh, heads, width):
    head_col = lambda q: pl.BlockSpec((None, s, dh), lambda h, q=q: (q, 0, h))
    vec = pl.BlockSpec((1, dh), lambda h: (0, h))
    mat = pl.BlockSpec((None, dh, dh), lambda h: (h, 0, 0))
    weights = [pl.BlockSpec((width, dh), lambda h: (0, h)), vec, mat, vec, mat, vec, vec]
    return head_col, weights


def _lru_fwd(proj, conv_w, conv_b, w_a, b_a, w_x, b_x, lam, name, deps=()):
    _, s, e = proj.shape
    heads, dh, _ = w_a.shape
    width = conv_w.shape[0]

    def body(v_ref, g_ref, w_ref, cb_ref, wa_ref, ba_ref, wx_ref, bx_ref, lam_ref, yb_ref, keep_ref):
        v, _, r, i, _, a, mult = _lru_gates(v_ref[...].astype(F32), w_ref, cb_ref, wa_ref, ba_ref,
                                           wx_ref, bx_ref, lam_ref, width)
        for k, val in enumerate((v, r, i, a, mult)):
            keep_ref[k + 1] = val

        def emit(row0, rows, h):
            keep_ref[0, row0:row0 + rows, :] = h
            gv = g_ref[row0:row0 + rows, :].astype(F32)
            yb_ref[row0:row0 + rows, :] = (h * (gv * _sigmoid(gv))).astype(yb_ref.dtype)

        _scan(a, mult * i * v, _shift_down, emit)

    head_col, weights = _lru_specs(s, dh, heads, width)
    return pl.pallas_call(
        _after(body, 9, deps), name=name, grid=(heads,),
        in_specs=[head_col(0), head_col(1)] + weights + [ANY] * len(deps),
        out_specs=[pl.BlockSpec((s, dh), lambda h: (0, h)), pl.BlockSpec((6, s, dh), lambda h: (0, 0, h))],
        out_shape=[jax.ShapeDtypeStruct((s, e), BF16), jax.ShapeDtypeStruct((6, s, e), F32)],
        compiler_params=_params("arbitrary"),
    )(proj, proj, conv_w, conv_b, w_a, b_a, w_x, b_x, lam, *deps)


def _lru_bwd(proj, keep, dyb, conv_w, conv_b, w_a, b_a, w_x, b_x, lam, name, deps=()):
    _, s, e = proj.shape
    heads, dh, _ = w_a.shape
    width = conv_w.shape[0]

    def body(v_ref, g_ref, hs_ref, dy_ref, w_ref, cb_ref, wa_ref, ba_ref, wx_ref, bx_ref, lam_ref,
             dp_ref, dwa_ref, dwx_ref, vec_ref):
        v_pre = v_ref[...].astype(F32)
        hs, v, r, i, a, mult = (hs_ref[k] for k in range(6))
        vb = v.astype(BF16)
        nl = -lam_ref[...]
        sp = jnp.maximum(nl, 0.0) + jnp.log1p(jnp.exp(-jnp.abs(nl)))
        gv = g_ref[...].astype(F32)
        dyv = dy_ref[...].astype(F32)
        sg = _sigmoid(gv)
        dp_ref[1] = (dyv * hs * (sg * (1.0 + gv * (1.0 - sg)))).astype(dp_ref.dtype)
        dhs = dyv * (gv * sg)
        d_h = _scan(_shift_up(a, 1), dhs, _shift_up)
        da = d_h * _shift_down(hs, 1)
        iv = i * v
        dlog_a = da * a - (d_h * iv) * (a * a) / mult
        dhm = d_h * mult
        di = dhm * v
        dv = dhm * i
        dlr = dlog_a * r
        dzr = dlr * (1.0 - r) * ((-RGLRU_C) * sp)
        dzi = di * i * (1.0 - i)
        dsp = jnp.sum(dlr, axis=0, keepdims=True) * (-RGLRU_C)
        vec_ref[...] = jnp.zeros_like(vec_ref)
        vec_ref[0:1, :] = jnp.sum(dzr, axis=0, keepdims=True)
        vec_ref[1:2, :] = jnp.sum(dzi, axis=0, keepdims=True)
        vec_ref[2:3, :] = -dsp * _sigmoid(-lam_ref[...])
        dzr_b = dzr.astype(BF16)
        dzi_b = dzi.astype(BF16)
        vt = vb.astype(F32).T.astype(BF16)
        dwa_ref[...] = jnp.dot(vt, dzr_b, preferred_element_type=F32).astype(dwa_ref.dtype)
        dwx_ref[...] = jnp.dot(vt, dzi_b, preferred_element_type=F32).astype(dwx_ref.dtype)
        nt = (((1,), (1,)), ((), ()))
        dv = dv + lax.dot_general(dzr_b, wa_ref[...], nt, preferred_element_type=F32)
        dv = dv + lax.dot_general(dzi_b, wx_ref[...], nt, preferred_element_type=F32)
        vec_ref[3:4, :] = jnp.sum(dv, axis=0, keepdims=True)
        dvp = w_ref[width - 1:width, :] * dv
        vec_ref[4 + width - 1:4 + width, :] = jnp.sum(dv * v_pre, axis=0, keepdims=True)
        for k in range(width - 1):
            sh = width - 1 - k
            dvp = dvp + w_ref[k:k + 1, :] * _shift_up(dv, sh)
            vec_ref[4 + k:5 + k, :] = jnp.sum(dv * _shift_down(v_pre, sh), axis=0, keepdims=True)
        dp_ref[0] = dvp.astype(dp_ref.dtype)

    head_col, weights = _lru_specs(s, dh, heads, width)
    col = pl.BlockSpec((s, dh), lambda h: (0, h))
    mat = pl.BlockSpec((None, dh, dh), lambda h: (h, 0, 0))
    return pl.pallas_call(
        _after(body, 11, deps), name=name, grid=(heads,),
        in_specs=[head_col(0), head_col(1), pl.BlockSpec((6, s, dh), lambda h: (0, 0, h)), col] + weights
        + [ANY] * len(deps),
        out_specs=[pl.BlockSpec((2, s, dh), lambda h: (0, 0, h)), mat, mat,
                   pl.BlockSpec((16, dh), lambda h: (0, h))],
        out_shape=[jax.ShapeDtypeStruct((2, s, e), BF16),
                   jax.ShapeDtypeStruct((heads, dh, dh), BF16),
                   jax.ShapeDtypeStruct((heads, dh, dh), BF16),
                   jax.ShapeDtypeStruct((16, e), F32)],
        compiler_params=_params("arbitrary"),
    )(proj, proj, keep, dyb, conv_w, conv_b, w_a, b_a, w_x, b_x, lam, *deps)


def _ada_mod(c_all, w, b, name):
    layers, d, f = w.shape
    nb = c_all.shape[0]

    def body(c_ref, w_ref, b_ref, o_ref):
        cv = c_ref[...]
        sc = cv * _sigmoid(cv)
        o_ref[...] = jnp.dot(sc, w_ref[...], preferred_element_type=F32,
                             precision=lax.Precision.HIGHEST) + b_ref[...]

    return pl.pallas_call(
        body, name=name, grid=(layers,),
        in_specs=[pl.BlockSpec((nb, d), lambda l: (0, 0)),
                  pl.BlockSpec((None, d, f), lambda l: (l, 0, 0)),
                  pl.BlockSpec((None, 1, f), lambda l: (l, 0, 0))],
        out_specs=pl.BlockSpec((None, nb, f), lambda l: (l, 0, 0)),
        out_shape=jax.ShapeDtypeStruct((layers, nb, f), F32),
        compiler_params=_params("arbitrary"),
    )(c_all, w, b)


def _ada_update(c_all_t, dmod, w, m, v, name):
    d, nb = c_all_t.shape
    layers, _, f = dmod.shape
    tr = _tile(d, 512)

    def body(c_ref, dm_ref, w_ref, m_ref, v_ref, g_ref, d_ref, mo_ref, vo_ref):
        cv = c_ref[...]
        sc = cv * _sigmoid(cv)
        g = sc[:, 0:1] * dm_ref[0:1, :]
        for k in range(1, nb):
            g = g + sc[:, k:k + 1] * dm_ref[k:k + 1, :]
        g_ref[...] = g
        d_ref[...], mo_ref[...], vo_ref[...] = _adamw_math(w_ref[...], g, m_ref[...], v_ref[...])

    blk = pl.BlockSpec((None, tr, f), lambda l, i: (l, i, 0))
    return pl.pallas_call(
        body, name=name, grid=(layers, d // tr),
        in_specs=[pl.BlockSpec((tr, nb), lambda l, i: (i, 0)),
                  pl.BlockSpec((None, nb, f), lambda l, i: (l, 0, 0)), blk, blk, blk],
        out_specs=[blk] * 4,
        out_shape=[jax.ShapeDtypeStruct((layers, d, f), F32)] * 4,
        compiler_params=_params("arbitrary", "arbitrary"),
    )(c_all_t, dmod, w, m, v)


def _device_sum(g, name):
    _, rows, _ = g.shape

    def body(g_ref, o_ref):
        acc = g_ref[0]
        for k in range(1, N_DEV):
            acc = acc + g_ref[k]
        o_ref[...] = acc

    return pl.pallas_call(
        body, name=name,
        in_specs=[VMEM_SPEC], out_specs=VMEM_SPEC,
        out_shape=jax.ShapeDtypeStruct((rows, LANES), F32),
        compiler_params=pltpu.CompilerParams(vmem_limit_bytes=VMEM_LIMIT),
    )(g)


def _adamw_math(w, g, m, v):
    m = ADAM_B1 * m + (1.0 - ADAM_B1) * g
    v = ADAM_B2 * v + (1.0 - ADAM_B2) * (g * g)
    m_hat = m / (1.0 - ADAM_B1 ** ADAM_STEP)
    v_hat = v / (1.0 - ADAM_B2 ** ADAM_STEP)
    delta = -ADAM_LR * (m_hat / (jnp.sqrt(v_hat) + ADAM_EPS) + ADAM_WD * w)
    return delta, m, v


def _adamw(w, g, m, v, name):
    rows, cols = w.shape
    tr = _tile(rows, 256)

    def body(w_ref, g_ref, m_ref, v_ref, d_ref, mo_ref, vo_ref):
        d_ref[...], mo_ref[...], vo_ref[...] = _adamw_math(w_ref[...], g_ref[...], m_ref[...], v_ref[...])

    blk = pl.BlockSpec((tr, cols), lambda i: (i, 0))
    return pl.pallas_call(
        body, name=name, grid=(rows // tr,),
        in_specs=[blk] * 4, out_specs=[blk] * 3,
        out_shape=[jax.ShapeDtypeStruct((rows, cols), F32)] * 3,
        compiler_params=_params("arbitrary"),
    )(w, g, m, v)


def _adamw_reduced(idx, w, m, v, part, got, recvs, name):
    rows, cols = w.shape
    tr = _tile(rows, 256)
    nr = len(recvs)

    def body(idx_ref, w_ref, m_ref, v_ref, p_ref, q_ref, *rest):
        g_ref, d_ref, mo_ref, vo_ref = rest[nr:]
        g = p_ref[...].astype(F32) + q_ref[...].astype(F32)
        for u_ref in rest[:nr]:
            for j in range(u_ref.shape[0]):
                g = g + u_ref[j].astype(F32)
        g_ref[...] = g
        d_ref[...], mo_ref[...], vo_ref[...] = _adamw_math(w_ref[...], g, m_ref[...], v_ref[...])

    blk = pl.BlockSpec((tr, cols), lambda i, idx: (i, 0))
    grid_spec = pltpu.PrefetchScalarGridSpec(
        num_scalar_prefetch=1, grid=(rows // tr,),
        in_specs=[blk, blk, blk,
                  pl.BlockSpec((None, None, tr, cols), lambda i, idx: (idx[3], idx[4], i, 0)),
                  pl.BlockSpec((None, None, tr, cols), lambda i, idx: (idx[3], 0, i, 0))]
        + [pl.BlockSpec((u.shape[0], tr, cols), lambda i, idx: (0, i, 0)) for u in recvs],
        out_specs=[blk] * 4)
    return pl.pallas_call(
        body, name=name, grid_spec=grid_spec,
        out_shape=[jax.ShapeDtypeStruct((rows, cols), F32)] * 4,
        compiler_params=_params("arbitrary"),
    )(idx, w, m, v, part, got, *recvs)


def _pack(vectors):
    flat = jnp.concatenate([v.reshape(-1).astype(F32) for v in vectors])
    pad = (-flat.shape[0]) % (8 * LANES)
    return jnp.pad(flat, (0, pad)).reshape(-1, LANES)


def _unpack(flat, shapes):
    out, off = [], 0
    for shp in shapes:
        size = math.prod(shp)
        out.append(flat[..., off:off + size].reshape(flat.shape[:-1] + tuple(shp)))
        off += size
    return out


def _my_slice(full, me, axis):
    size = full.shape[axis] // N_DEV
    return lax.dynamic_slice_in_dim(full, me * size, size, axis)


def kernel(x, c, norm_g, ada_w, ada_b, sc_w_in, sc_conv_w, sc_w_out, lru_w_in, lru_conv_w, lru_conv_b, lru_w_a, lru_b_a, lru_w_x, lru_b_x, lru_lambda, lru_w_out, final_g, loss_target, m_norm_g, m_ada_w, m_ada_b, m_sc_w_in, m_sc_conv_w, m_sc_w_out, m_lru_w_in, m_lru_conv_w, m_lru_conv_b, m_lru_w_a, m_lru_b_a, m_lru_w_x, m_lru_b_x, m_lru_lambda, m_lru_w_out, m_final_g, v_norm_g, v_ada_w, v_ada_b, v_sc_w_in, v_sc_conv_w, v_sc_w_out, v_lru_w_in, v_lru_conv_w, v_lru_conv_b, v_lru_w_a, v_lru_b_a, v_lru_w_x, v_lru_b_x, v_lru_lambda, v_lru_w_out, v_final_g):
    _, s, d = x.shape
    e = sc_w_out.shape[1] * N_DEV
    heads, dh_s, dh = lru_w_a.shape[1:]
    es = e // N_DEV
    f = ada_w.shape[2]
    mx, my, mc = _position()
    me = 4 * mx + 2 * my + mc
    chip = 2 * mx + my
    idx = jnp.stack([chip ^ 1, chip ^ 2, chip ^ 3, chip, mc]).astype(jnp.int32)

    x0 = x[0]
    target = loss_target[0]

    small_shapes = [(d,), (3, es), (4, es), (es,), (heads, dh_s), (heads, dh_s), (es,)]
    small = _small_gather(_pack([c, sc_conv_w, lru_conv_w, lru_conv_b, lru_b_a, lru_b_x, lru_lambda]),
                          "gather_small_weights").reshape(N_DEV, -1)
    c_all, cw3, cw4, cb, ba, bx, lam = _unpack(small, small_shapes)
    cw3 = cw3.transpose(1, 0, 2).reshape(3, e)
    cw4 = cw4.transpose(1, 0, 2).reshape(4, e)
    cb = cb.reshape(1, e)
    lam = lam.reshape(1, e)
    ba = ba.transpose(1, 0, 2).reshape(1, e)
    bx = bx.transpose(1, 0, 2).reshape(1, e)

    mine = [sc_w_in[0], sc_w_out[0], lru_w_in[0], lru_w_a[0].reshape(heads * dh_s, dh),
            lru_w_x[0].reshape(heads * dh_s, dh), lru_w_out[0]]
    lands = [lax.dynamic_update_slice(lax.empty((N_DEV,) + sh.shape, BF16), sh.astype(BF16)[None], (me, 0, 0))
             for sh in mine]
    every = [1, 2, 3, 0]
    units = [([0], [0]), ([0], [1]), ([0], [2]), ([0], [3]), ([1], every), ([2], every), ([3, 4], every),
             ([5], every)]
    sems, first_ld, started = _gather_start(lands[:1], units[:3], [small], "gather_start_first")
    lands = first_ld + lands[1:]

    ada_b_mine = _my_slice(ada_b, me, 1).reshape(2, 1, f)
    mod_mine = _ada_mod(c_all, ada_w, ada_b_mine, "ada_mod")
    mod_all = _small_gather(_pack([mod_mine]), "gather_mod", deps=[started])

    def start_later(after):
        far_sems, far_ld, tok = _gather_start(lands[:1], units[3:4], after, "gather_start_far")
        rest_units = [([i - 1 for i in members], ks) for members, ks in units[4:]]
        rest_sems, rest_ld, tok = _gather_start(lands[1:], rest_units, [tok], "gather_start_rest")
        sems.extend(far_sems + rest_sems)
        lands[:] = far_ld + rest_ld
        return tok

    def gathered(u, after_forward, name):
        members, ks = units[u]
        fwd, lnd, token = _gather_forward([lands[i] for i in members], ks, sems[u][0], sems[u][1],
                                          after_forward, "gather_forward_" + name)
        for i, ld in zip(members, lnd):
            lands[i] = ld

        def finish(after):
            out = _gather_finish([lands[i] for i in members], ks, fwd, after, "gather_finish_" + name)
            for i, ld in zip(members, out):
                lands[i] = ld
            return out

        return token, finish

    tok, finish_y = gathered(1, [mod_all], "sc_w_in_near_y")
    tok, finish_x = gathered(2, [tok], "sc_w_in_near_x")
    queued = start_later([tok])

    mod_all = mod_all.reshape(N_DEV, -1)
    mod_all = mod_all[:, :2 * N_DEV * f].reshape(N_DEV, 2, N_DEV, f)
    mod_all = mod_all.transpose(1, 2, 0, 3).reshape(2, N_DEV, 3 * d)
    mod = lax.dynamic_index_in_dim(mod_all, me, 1, keepdims=False)
    shift = [mod[l:l + 1, 0:d] for l in range(2)]
    scale = [mod[l:l + 1, d:2 * d] for l in range(2)]
    gate = [mod[l:l + 1, 2 * d:3 * d] for l in range(2)]
    ng = [norm_g[l:l + 1] for l in range(2)]
    fg = final_g.reshape(1, d)

    h0 = _norm_mod(x0, ng[0], scale[0], shift[0], "norm_mod_0", deps=[queued])
    proj0 = lax.empty((4, s, e), BF16)
    tok, _ = gathered(0, [h0], "sc_w_in_own")
    proj0 = _mm_proj_group(h0, lands[0], idx, 3, proj0, "mm_proj_0_own", deps=[tok])
    for u, name, finish in ((1, "near_y", finish_y), (2, "near_x", finish_x), (3, "far", None)):
        after = [proj0]
        if finish is None:
            tok, finish = gathered(u, [proj0], "sc_w_in_" + name)
            after = [tok]
        wg_in0, = finish(after)
        proj0 = _mm_proj_group(h0, wg_in0, idx, u - 1, proj0, "mm_proj_0_" + name)
    tok, finish = gathered(4, [proj0], "sc_w_out")
    yb0 = _sc_fwd(proj0, cw3, "sc_fwd", deps=[tok])
    w_out0 = finish([yb0])[0].reshape(e, d)
    x1, y0 = _mm_out(yb0, w_out0, x0, gate[0], "mm_out_0")
    tok, finish = gathered(5, [x1], "lru_w_in")
    h1 = _norm_mod(x1, ng[1], scale[1], shift[1], "norm_mod_1", deps=[tok])
    wg_in1, = finish([h1])
    proj1 = _mm_proj(h1, wg_in1, 2, "mm_proj_1")
    tok, finish = gathered(6, [proj1], "lru_gates")
    wg_a, wg_x = finish([tok])
    w_a = wg_a.reshape(N_DEV, heads, dh_s, dh).transpose(1, 0, 2, 3).reshape(heads, dh, dh)
    w_x = wg_x.reshape(N_DEV, heads, dh_s, dh).transpose(1, 0, 2, 3).reshape(heads, dh, dh)
    tok, finish = gathered(7, [w_a, w_x], "lru_w_out")
    yb1, hs = _lru_fwd(proj1, cw4, cb, w_a, ba, w_x, bx, lam, "lru_fwd", deps=[tok])
    w_out1 = finish([yb1])[0].reshape(e, d)
    x2, y1 = _mm_out(yb1, w_out1, x1, gate[1], "mm_out_1")
    dx2, loss_part, d_fg, dy1, dgate1 = _final_loss(x2, fg, target, y1, gate[1], "final_loss")

    def pieces(g, rows, cols):
        return g.reshape(4, 2, rows, cols)

    def by_rows(g):
        return g.reshape(heads, N_DEV, dh_s, dh).transpose(1, 0, 2, 3).reshape(N_DEV, heads * dh_s, dh)

    def pair_begin(parts, group):
        send, recv, parts, lnd, token = _pair_start(parts, "pair_start_" + group)
        return dict(parts=parts, lands=lnd, send=send, recv=recv, group=group), token

    def scatter_start(pair, names, after):
        group = pair["group"]
        parts, gots = _pair_wait(pair["parts"], pair["lands"], pair["send"], pair["recv"], after,
                                 "pair_wait_" + group)
        sums = [_pair_sum(idx, p, q, "pair_sum_" + nm) for p, q, nm in zip(parts, gots, names)]
        empties = [lax.empty(sm.shape, sm.dtype) for sm in sums]
        send, recv, sums, lnd, token = _chip_start(sums, empties, "chip_start_" + group)
        return dict(parts=parts, gots=gots, names=names, group=group, sums=sums, lands=lnd,
                    send=send, recv=recv), token

    big = {"sc_w_in": (sc_w_in, m_sc_w_in, v_sc_w_in), "sc_w_out": (sc_w_out, m_sc_w_out, v_sc_w_out),
           "lru_w_in": (lru_w_in, m_lru_w_in, v_lru_w_in), "lru_w_a": (lru_w_a, m_lru_w_a, v_lru_w_a),
           "lru_w_x": (lru_w_x, m_lru_w_x, v_lru_w_x), "lru_w_out": (lru_w_out, m_lru_w_out, v_lru_w_out)}
    big_res = {}

    def scatter_finish(rs, after):
        recvs = _chip_wait(rs["sums"], rs["lands"], rs["send"], rs["recv"], after, "chip_wait_" + rs["group"])
        done = []
        for p, q, u, nm in zip(rs["parts"], rs["gots"], recvs, rs["names"]):
            w, m, v = big[nm]
            shp2 = p.shape[2:]
            res = _adamw_reduced(idx, w.reshape(shp2), m.reshape(shp2), v.reshape(shp2), p, q, [u], "adamw_" + nm)
            big_res[nm] = [r.reshape(w.shape) for r in res]
            done.append(res[1])
        return done

    dw_out1 = _mm_tn(yb1, dy1[None], 1, "mm_dw_out_1")
    pair, tok = pair_begin([pieces(dw_out1, es, d)], "lru_w_out")
    dyb1 = _mm_nt(dy1[None], w_out1[None], BF16, "mm_dyb_1", deps=[tok])
    rs1, tok = scatter_start(pair, ["lru_w_out"], [dyb1])
    dproj1, dw_a, dw_x, vecs1 = _lru_bwd(proj1, hs, dyb1, cw4, cb, w_a, ba, w_x, bx, lam, "lru_bwd", deps=[tok])
    done = scatter_finish(rs1, [dproj1])
    dw_in1 = _mm_tn(h1, dproj1, N_DEV, "mm_dw_in_1", deps=done)
    pair, tok = pair_begin([pieces(dw_in1, d, 2 * es), pieces(by_rows(dw_a), heads * dh_s, dh),
                            pieces(by_rows(dw_x), heads * dh_s, dh)], "lru_in")
    dh1 = _mm_nt(dproj1, wg_in1, BF16, "mm_dh_1", deps=[tok])
    rs2, tok = scatter_start(pair, ["lru_w_in", "lru_w_a", "lru_w_x"], [dh1])
    dx1, dscale1, dshift1, dng1, dy0, dgate0 = _norm_mod_bwd(x1, dh1, dx2, ng[1], scale[1], "norm_mod_bwd_1",
                                                             below=(y0, gate[0]), deps=[tok])
    dw_out0 = _mm_tn(yb0, dy0[None], 1, "mm_dw_out_0")
    pair, tok = pair_begin([pieces(dw_out0, es, d)], "sc_w_out")
    dyb0 = _mm_nt(dy0[None], w_out0[None], BF16, "mm_dyb_0", deps=[tok])
    rs3, tok = scatter_start(pair, ["sc_w_out"], [dyb0])
    dproj0, vecs0 = _sc_bwd(proj0, dyb0, cw3, "sc_bwd", deps=[tok])
    idx_one = jnp.stack([jnp.zeros_like(mc)] * 4 + [mc]).astype(jnp.int32)
    sc_w_in_steps = []

    def chip_step(j, pair, after):
        (part,), (got,) = _pair_wait(pair["parts"], pair["lands"], pair["send"], pair["recv"], after,
                                     "pair_wait_sc_w_in_%d" % j)
        sm = _pair_sum(idx_one, part, got, "pair_sum_sc_w_in_%d" % j, nslots=1)
        send, recv, sums, lnd, token = _chip_start([sm], [lax.empty(sm.shape, sm.dtype)],
                                                   "chip_start_sc_w_in_%d" % j, flips=(j,))
        sc_w_in_steps.append((sums, lnd, send, recv, j))
        return token

    pending, done = None, []
    for j in (3, 2, 1, 0):
        part = _mm_tn_group(h0, dproj0, idx, (j - 1) % 4, 2, "mm_dw_in_0_%d" % j, deps=done)[None]
        pair, tok = pair_begin([part], "sc_w_in_%d" % j)
        if j == 3:
            done = [chip_step(j, pair, [tok])]
            continue
        done = [tok]
        if pending is not None:
            done.append(chip_step(pending[0], pending[1], [tok]))
        pending = (j, pair)
    done += scatter_finish(rs2, done)
    dh0 = _mm_nt(dproj0, wg_in0, BF16, "mm_dh_0", deps=done)
    pair = pending[1]
    (part,), (got,) = _pair_wait(pair["parts"], pair["lands"], pair["send"], pair["recv"], [dh0],
                                 "pair_wait_sc_w_in_0")
    dx0, dscale0, dshift0, dng0 = _norm_mod_bwd(x0, dh0, dx1, ng[0], scale[0], "norm_mod_bwd_0")
    done = scatter_finish(rs3, [dx0])
    dmod_mine = jnp.concatenate([dshift0, dscale0, dgate0, dshift1, dscale1, dgate1], axis=1)
    end_shapes = [(LANES,), (2, 3 * d), (2, d), (d,), (8, e), (16, e)]
    end_all = _small_gather(
        _pack([loss_part, dmod_mine, jnp.concatenate([dng0, dng1], axis=0), d_fg, vecs0, vecs1]),
        "gather_small_grads", deps=done)
    end_sum = _device_sum(end_all, "sum_small_grads").reshape(-1)
    loss_v, g_ada_b, g_norm_g, g_final_g, sum0, sum1 = _unpack(end_sum, end_shapes)
    loss = loss_v[0]
    dmod_all = _unpack(end_all.reshape(N_DEV, -1), end_shapes)[1].transpose(1, 0, 2)
    dmod_cols = _my_slice(dmod_all, me, 2)
    ada_out = _ada_update(c_all.T, dmod_cols, ada_w, m_ada_w, v_ada_w, "ada_update")

    g_sc_conv_w = _my_slice(sum0[0:3], me, 1)
    g_lru_b_a = _my_slice(sum1[0].reshape(heads, dh), me, 1)
    g_lru_b_x = _my_slice(sum1[1].reshape(heads, dh), me, 1)
    g_lru_lambda = _my_slice(sum1[2:3], me, 1)
    g_lru_conv_b = _my_slice(sum1[3:4], me, 1)
    g_lru_conv_w = _my_slice(sum1[4:8], me, 1)

    small_w = [norm_g, ada_b, final_g, sc_conv_w, lru_conv_w, lru_conv_b, lru_b_a, lru_b_x, lru_lambda]
    small_m = [m_norm_g, m_ada_b, m_final_g, m_sc_conv_w, m_lru_conv_w, m_lru_conv_b, m_lru_b_a, m_lru_b_x,
               m_lru_lambda]
    small_v = [v_norm_g, v_ada_b, v_final_g, v_sc_conv_w, v_lru_conv_w, v_lru_conv_b, v_lru_b_a, v_lru_b_x,
               v_lru_lambda]
    small_g = [g_norm_g, g_ada_b, g_final_g, g_sc_conv_w, g_lru_conv_w, g_lru_conv_b, g_lru_b_a, g_lru_b_x,
               g_lru_lambda]
    small_g = [g.reshape(w.shape) for g, w in zip(small_g, small_w)]
    shapes = [w.shape for w in small_w]
    packed = _adamw(_pack(small_w), _pack(small_g), _pack(small_m), _pack(small_v), "adamw_small")
    small_out = [small_g] + [_unpack(p.reshape(-1), shapes) for p in packed]

    after = [packed[0], ada_out[1]]
    recvs = []
    for sums, lnd, send, recv, j in sc_w_in_steps:
        recvs += _chip_wait(sums, lnd, send, recv, after, "chip_wait_sc_w_in_%d" % j)
    shp2 = part.shape[2:]
    res = _adamw_reduced(idx_one, sc_w_in.reshape(shp2), m_sc_w_in.reshape(shp2), v_sc_w_in.reshape(shp2),
                         part, got, recvs, "adamw_sc_w_in")
    big_res["sc_w_in"] = [r.reshape(sc_w_in.shape) for r in res]
    big_out = [big_res[nm] for nm in ("sc_w_in", "sc_w_out", "lru_w_in", "lru_w_a", "lru_w_x", "lru_w_out")]

    def small(kind, i):
        return small_out[kind][i]

    def bigw(kind, i):
        return big_out[i][kind]

    outs = [loss, dx0[None]]
    for kind in range(4):
        outs += [small(kind, 0), ada_out[kind], small(kind, 1), bigw(kind, 0), small(kind, 3), bigw(kind, 1),
                 bigw(kind, 2), small(kind, 4), small(kind, 5), bigw(kind, 3), small(kind, 6), bigw(kind, 4),
                 small(kind, 7), small(kind, 8), bigw(kind, 5), small(kind, 2)]
    return tuple(outs)
```
